```python
import math, functools
import jax, jax.numpy as jnp
from jax import lax
import numpy as np

D_MODEL = 1024
BATCH = 8
SEQ = 2048
DEPTH = 1
DEC_BATCH = 128
DEC_SEQ = 1
PAST_LEN = 8192
PAGE_SIZE = 128

H_A = 8
N_A = 64
C_A = H_A * N_A
R_W = 64
R_A = 64
R_G = 128
C_SHIFT = 3 * C_A + R_W + R_A + R_G
LNX_EPS = 64e-5
H_Q = 8
H_KV = 2
GQA = H_Q // H_KV
HEAD_DIM = 64
WINDOW = 128
ROT_DIM = HEAD_DIM // 4
ROPE_THETA = 500000.0
N_MEM = 256
MEM_HEADS = 4
MEM_W = MEM_HEADS * HEAD_DIM
Q_W = H_Q * HEAD_DIM
KV_W = H_KV * HEAD_DIM
N_BRANCH = 3
SPLITS = [C_SHIFT, C_SHIFT + Q_W, C_SHIFT + Q_W + KV_W, C_SHIFT + Q_W + 2 * KV_W,
          C_SHIFT + Q_W + 2 * KV_W + MEM_W]
N_PROJ = SPLITS[-1] + N_BRANCH * D_MODEL
N_GROUPS = 4
EXPERTS_PER_GROUP = 8
N_EXPERTS = N_GROUPS * EXPERTS_PER_GROUP
TOP_K = 2
EXPERT_FF = 512
EXPERT_BLOCK = 128
ALPHA = (2.0 * DEPTH) ** 0.25
BETA = (8.0 * DEPTH) ** -0.25
LN_EPS = 1e-5
NEG_INF = -1e30

kernel_name = 'hybrid_rwkv7_swa_sink_memxattn_hmoe_deepnorm_step'


def layer_norm(x, g, b):
    xf = x.astype(jnp.float32)
    mu = jnp.mean(xf, axis=-1, keepdims=True)
    var = jnp.mean(jnp.square(xf - mu), axis=-1, keepdims=True)
    y = (xf - mu) * lax.rsqrt(var + LN_EPS) * g.astype(jnp.float32) + b.astype(jnp.float32)
    return y.astype(x.dtype)


def partial_rope(x, pos):
    half = ROT_DIM // 2
    inv_freq = ROPE_THETA ** (-jnp.arange(half, dtype=jnp.float32) / half)
    ang = pos[:, None] * inv_freq[None, :]
    cos = jnp.cos(ang)[:, None, :]
    sin = jnp.sin(ang)[:, None, :]
    xf = x.astype(jnp.float32)
    x1, x2 = xf[..., :half], xf[..., half:ROT_DIM]
    out = jnp.concatenate([x1 * cos - x2 * sin, x2 * cos + x1 * sin, xf[..., ROT_DIM:]], axis=-1)
    return out.astype(x.dtype)


def wkv_step(S, inp):
    r, w, k, v, a, b = inp
    sa = jnp.einsum('bhvk,bhk->bhv', S, a)
    S = S * w[:, :, None, :] + sa[..., None] * b[:, :, None, :] + v[..., None] * k[:, :, None, :]
    return S, jnp.einsum('bhvk,bhk->bhv', S, r)


def rwkv_time_mix(z, shift_prev, wkv0, p):
    B, T, _ = z.shape
    prev = jnp.concatenate([shift_prev[:, None, :].astype(z.dtype), z[:, :-1]], axis=1)
    zs = z + (prev - z) * p['mu']
    r, k, v, xw, xa, xg = jnp.split(zs, [C_A, 2 * C_A, 3 * C_A, 3 * C_A + R_W, 3 * C_A + R_W + R_A], axis=-1)
    w = -jax.nn.softplus(-(p['w0'] + jnp.tanh(xw) @ p['w_up'])) - 0.5
    decay = jnp.exp(-jnp.exp(w.astype(jnp.float32)))
    a = jax.nn.sigmoid(p['a0'] + xa @ p['a_up'])
    g = jax.nn.sigmoid(xg) @ p['g_up']
    hs = lambda t: t.astype(jnp.float32).reshape(B, T, H_A, N_A)
    kk = hs(k * p['k_k'])
    kk = kk / jnp.maximum(jnp.linalg.norm(kk, axis=-1, keepdims=True), 1e-12)
    k = k * (1.0 + (a - 1.0) * p['k_a'])
    rh, kh, vh, ah = hs(r), hs(k), hs(v), hs(a)
    tm = lambda t: jnp.moveaxis(t, 1, 0)
    xs_scan = (tm(rh), tm(hs(decay)), tm(kh), tm(vh), tm(-kk), tm(kk * ah))
    S_final, y = lax.scan(wkv_step, wkv0.astype(jnp.float32), xs_scan)
    y = jnp.moveaxis(y, 0, 1)
    mu_y = jnp.mean(y, axis=-1, keepdims=True)
    var_y = jnp.mean(jnp.square(y - mu_y), axis=-1, keepdims=True)
    y = ((y - mu_y) * lax.rsqrt(var_y + LNX_EPS)).reshape(B, T, C_A)
    y = y * p['lnx_g'].astype(jnp.float32) + p['lnx_b'].astype(jnp.float32)
    bonus = jnp.sum(rh * kh * p['r_k'].astype(jnp.float32), axis=-1, keepdims=True) * vh
    y = (y + bonus.reshape(B, T, C_A)).astype(z.dtype) * g
    return y, S_final, z[:, -1]


def sink_softmax(s, valid, sink):
    s = jnp.where(valid, s, NEG_INF)
    sink = jnp.broadcast_to(sink.astype(jnp.float32), s.shape[:-1] + (1,))
    return jax.nn.softmax(jnp.concatenate([s, sink], axis=-1), axis=-1)[..., :-1]


def swa_banded(q, k, v, sinks):
    B, T = q.shape[0], q.shape[1]
    nb = T // WINDOW
    qb = q.reshape(B, nb, WINDOW, H_KV, GQA, HEAD_DIM)
    kb = k.reshape(B, nb, WINDOW, H_KV, HEAD_DIM)
    vb = v.reshape(B, nb, WINDOW, H_KV, HEAD_DIM)
    shift = lambda t: jnp.concatenate([jnp.zeros_like(t[:, :1]), t[:, :-1]], axis=1)
    kw = jnp.concatenate([shift(kb), kb], axis=2)
    vw = jnp.concatenate([shift(vb), vb], axis=2)
    s = jnp.einsum('bnqhgd,bnkhd->bnhgqk', qb, kw).astype(jnp.float32) * (HEAD_DIM ** -0.5)
    qi = jnp.arange(WINDOW)[:, None] + WINDOW
    kj = jnp.arange(2 * WINDOW)[None, :]
    band = (kj <= qi) & (qi - kj < WINDOW)
    valid = band[None] & ((jnp.arange(nb) > 0)[:, None, None] | (kj >= WINDOW)[None])
    prob = sink_softmax(s, valid[None, :, None, None], sinks.reshape(H_KV, GQA)[None, None, :, :, None, None])
    o = jnp.einsum('bnhgqk,bnkhd->bnqhgd', prob.astype(v.dtype), vw).reshape(B, T, Q_W)
    return o, k[:, -WINDOW:], v[:, -WINDOW:]


def swa_cached(q, k, v, sinks, ck, cv):
    B, S = q.shape[0], q.shape[1]
    kk = jnp.concatenate([ck, k.astype(ck.dtype)], axis=1)
    vv = jnp.concatenate([cv, v.astype(cv.dtype)], axis=1)
    qb = q.reshape(B, S, H_KV, GQA, HEAD_DIM)
    s = jnp.einsum('bqhgd,bkhd->bhgqk', qb, kk).astype(jnp.float32) * (HEAD_DIM ** -0.5)
    qi = jnp.arange(S)[:, None] + WINDOW
    kj = jnp.arange(WINDOW + S)[None, :]
    valid = (kj <= qi) & (qi - kj < WINDOW)
    prob = sink_softmax(s, valid[None, None, None], sinks.reshape(H_KV, GQA)[None, :, :, None, None])
    o = jnp.einsum('bhgqk,bkhd->bqhgd', prob.astype(vv.dtype), vv).reshape(B, S, Q_W)
    return o, kk[:, -WINDOW:], vv[:, -WINDOW:]


def mem_attention(qm, mk, mv):
    B, T = qm.shape[0], qm.shape[1]
    s = jnp.einsum('bthd,bmhd->bhtm', qm, mk).astype(jnp.float32) * (HEAD_DIM ** -0.5)
    prob = jax.nn.softmax(s, axis=-1)
    return jnp.einsum('bhtm,bmhd->bthd', prob.astype(mv.dtype), mv).reshape(B, T, MEM_W)


def mixing_sublayer(x, pos0, shift_prev, wkv0, swa_fn, mem_k, mem_v, p):
    B, T, _ = x.shape
    z = x @ p['w_in']
    z_rwkv, q, k, v, qm, gates = jnp.split(z, SPLITS, axis=-1)
    o_a, wkv_new, shift_new = rwkv_time_mix(z_rwkv, shift_prev, wkv0, p)
    pos = pos0 + jnp.arange(T, dtype=jnp.float32)
    q = partial_rope(q.reshape(B, T, H_Q, HEAD_DIM), pos)
    k = partial_rope(k.reshape(B, T, H_KV, HEAD_DIM), pos)
    v = v.reshape(B, T, H_KV, HEAD_DIM)
    o_b, k_buf, v_buf = swa_fn(q, k, v, p['sinks'])
    o_m = mem_attention(qm.reshape(B, T, MEM_HEADS, HEAD_DIM), mem_k, mem_v)
    g = jax.nn.sigmoid(gates).reshape(B, T, N_BRANCH, D_MODEL)
    merged = (g[:, :, 0] * (o_a @ p['p_a']) + g[:, :, 1] * (o_b @ p['p_b'])
              + g[:, :, 2] * (o_m @ p['p_m']))
    return merged @ p['w_o'], wkv_new, shift_new, k_buf, v_buf


def routed_experts(xt, experts, gates, e_gate, e_up, e_down):
    n, D = xt.shape
    A = n * TOP_K
    flat_e = experts.reshape(A)
    flat_tok = jnp.arange(A, dtype=jnp.int32) // TOP_K
    flat_g = gates.reshape(A)
    order = jnp.argsort(flat_e)
    se = flat_e[order]
    counts = jnp.bincount(flat_e, length=N_EXPERTS)
    padded = (counts + EXPERT_BLOCK - 1) // EXPERT_BLOCK * EXPERT_BLOCK
    pad_end = jnp.cumsum(padded)
    pad_start = pad_end - padded
    start = jnp.cumsum(counts) - counts
    dest = pad_start[se] + jnp.arange(A, dtype=jnp.int32) - start[se]
    n_blocks = -(-A // EXPERT_BLOCK) + N_EXPERTS
    L = n_blocks * EXPERT_BLOCK
    row_tok = jnp.full((L,), n, jnp.int32).at[dest].set(flat_tok[order])
    row_gate = jnp.zeros((L,), xt.dtype).at[dest].set(flat_g[order])
    xs = jnp.concatenate([xt, jnp.zeros((1, D), xt.dtype)], axis=0)[row_tok]
    xs = xs.reshape(n_blocks, EXPERT_BLOCK, D)
    blk_e = jnp.minimum(jnp.searchsorted(pad_end, jnp.arange(n_blocks) * EXPERT_BLOCK, side='right'),
                        N_EXPERTS - 1)

    def expert_block(args):
        xb, e = args
        h = jax.nn.silu(xb @ e_gate[e]) * (xb @ e_up[e])
        return h @ e_down[e]

    ys = lax.map(expert_block, (xs, blk_e)).reshape(L, D)
    out = jnp.zeros((n + 1, D), xt.dtype).at[row_tok].add(ys * row_gate[:, None])
    return out[:n]


def hier_moe(x, w_group, b_group, w_router, b_router, e_gate, e_up, e_down):
    B, T, D = x.shape
    xt = x.reshape(B * T, D)
    n = B * T
    rows = jnp.arange(n)
    gl = (xt @ w_group).astype(jnp.float32) + b_group.astype(jnp.float32)
    g_idx = jnp.argmax(gl, axis=-1)
    g_w = jax.nn.softmax(gl, axis=-1)[rows, g_idx]
    el = ((xt @ w_router).astype(jnp.float32) + b_router.astype(jnp.float32))
    el = el.reshape(n, N_GROUPS, EXPERTS_PER_GROUP)[rows, g_idx]
    top_v, top_i = lax.top_k(el, TOP_K)
    gates = (g_w[:, None] * jax.nn.softmax(top_v, axis=-1)).astype(x.dtype)
    experts = (g_idx[:, None] * EXPERTS_PER_GROUP + top_i).astype(jnp.int32)
    y = routed_experts(xt, experts, gates, e_gate, e_up, e_down)
    return y.reshape(B, T, D)


def setup_inputs(seed: int = 0) -> dict:
    key = jax.random.key(seed)
    ks = iter(jax.random.split(key, 48))
    nrm = lambda shape, scale: jax.random.normal(next(ks), shape, jnp.float32) * scale
    L = DEPTH
    return {
        'x_prompt': nrm((BATCH, SEQ, D_MODEL), 1.0),
        'x_sample': nrm((DEC_BATCH, DEC_SEQ, D_MODEL), 1.0),
        'mem_prompt': nrm((BATCH, N_MEM, D_MODEL), 1.0),
        'state_wkv': nrm((L, DEC_BATCH, H_A, N_A, N_A), 0.3),
        'state_shift': nrm((L, DEC_BATCH, C_SHIFT), 1.0),
        'cache_win_k': nrm((L, DEC_BATCH, WINDOW, H_KV, HEAD_DIM), 1.0),
        'cache_win_v': nrm((L, DEC_BATCH, WINDOW, H_KV, HEAD_DIM), 1.0),
        'cache_mem_k': nrm((L, DEC_BATCH, N_MEM, MEM_HEADS, HEAD_DIM), 1.0),
        'cache_mem_v': nrm((L, DEC_BATCH, N_MEM, MEM_HEADS, HEAD_DIM), 1.0),
        'ln_in_g': 1.0 + nrm((D_MODEL,), 0.02),
        'ln_in_b': nrm((D_MODEL,), 0.02),
        'w_in': nrm((L, D_MODEL, N_PROJ), D_MODEL ** -0.5),
        'mu': jax.random.uniform(next(ks), (L, C_SHIFT), jnp.float32),
        'w0': jnp.linspace(-6.0, -1.0, C_A, dtype=jnp.float32)[None, :] + nrm((L, C_A), 0.1),
        'w_up': nrm((L, R_W, C_A), 0.1 * R_W ** -0.5),
        'a0': nrm((L, C_A), 0.1),
        'a_up': nrm((L, R_A, C_A), 0.5 * R_A ** -0.5),
        'g_up': nrm((L, R_G, C_A), R_G ** -0.5),
        'k_k': 0.85 + nrm((L, C_A), 0.05),
        'k_a': 1.0 + nrm((L, C_A), 0.05),
        'r_k': nrm((L, H_A, N_A), 0.1),
        'lnx_g': 1.0 + nrm((L, C_A), 0.02),
        'lnx_b': nrm((L, C_A), 0.02),
        'sinks': nrm((L, H_Q), 0.5),
        'w_mem_kv': nrm((L, D_MODEL, 2 * MEM_W), D_MODEL ** -0.5),
        'p_a': nrm((L, C_A, D_MODEL), C_A ** -0.5),
        'p_b': nrm((L, Q_W, D_MODEL), Q_W ** -0.5),
        'p_m': nrm((L, MEM_W, D_MODEL), MEM_W ** -0.5),
        'w_o': nrm((L, D_MODEL, D_MODEL), BETA * D_MODEL ** -0.5),
        'ln1_g': 1.0 + nrm((L, D_MODEL), 0.02),
        'ln1_b': nrm((L, D_MODEL), 0.02),
        'w_group': nrm((L, D_MODEL, N_GROUPS), D_MODEL ** -0.5),
        'b_group': nrm((L, N_GROUPS), 0.01),
        'w_router': nrm((L, D_MODEL, N_EXPERTS), D_MODEL ** -0.5),
        'b_router': nrm((L, N_EXPERTS), 0.01),
        'e_gate': nrm((L, N_EXPERTS, D_MODEL, EXPERT_FF), D_MODEL ** -0.5),
        'e_up': nrm((L, N_EXPERTS, D_MODEL, EXPERT_FF), D_MODEL ** -0.5),
        'e_down': nrm((L, N_EXPERTS, EXPERT_FF, D_MODEL), BETA * EXPERT_FF ** -0.5),
        'ln2_g': 1.0 + nrm((L, D_MODEL), 0.02),
        'ln2_b': nrm((L, D_MODEL), 0.02),
    }


def reference(x_prompt, x_sample, mem_prompt, state_wkv, state_shift, cache_win_k, cache_win_v,
              cache_mem_k, cache_mem_v, ln_in_g, ln_in_b, w_in, mu, w0, w_up, a0, a_up, g_up,
              k_k, k_a, r_k, lnx_g, lnx_b, sinks, w_mem_kv, p_a, p_b, p_m, w_o, ln1_g, ln1_b,
              w_group, b_group, w_router, b_router, e_gate, e_up, e_down, ln2_g, ln2_b):
    B = x_prompt.shape[0]
    xp = layer_norm(x_prompt, ln_in_g, ln_in_b)
    xs = layer_norm(x_sample, ln_in_g, ln_in_b)
    wkv_p_l, wkv_s_l, sh_p_l, sh_s_l = [], [], [], []
    kb_p_l, vb_p_l, kb_s_l, vb_s_l, mk_p_l, mv_p_l = [], [], [], [], [], []
    for l in range(DEPTH):
        p = dict(w_in=w_in[l], mu=mu[l], w0=w0[l], w_up=w_up[l], a0=a0[l], a_up=a_up[l],
                 g_up=g_up[l], k_k=k_k[l], k_a=k_a[l], r_k=r_k[l], lnx_g=lnx_g[l], lnx_b=lnx_b[l],
                 sinks=sinks[l], p_a=p_a[l], p_b=p_b[l], p_m=p_m[l], w_o=w_o[l])
        moe_w = (w_group[l], b_group[l], w_router[l], b_router[l], e_gate[l], e_up[l], e_down[l])
        mkv = mem_prompt @ w_mem_kv[l]
        mk_p = mkv[..., :MEM_W].reshape(B, N_MEM, MEM_HEADS, HEAD_DIM)
        mv_p = mkv[..., MEM_W:].reshape(B, N_MEM, MEM_HEADS, HEAD_DIM)
        hp, wkv_p, sh_p, kb_p, vb_p = mixing_sublayer(
            xp, 0.0, jnp.zeros((B, C_SHIFT), xp.dtype), jnp.zeros((B, H_A, N_A, N_A), jnp.float32),
            swa_banded, mk_p, mv_p, p)
        xp = layer_norm(ALPHA * xp + hp, ln1_g[l], ln1_b[l])
        xp = layer_norm(ALPHA * xp + hier_moe(xp, *moe_w), ln2_g[l], ln2_b[l])
        swa_s = functools.partial(swa_cached, ck=cache_win_k[l], cv=cache_win_v[l])
        hs, wkv_s, sh_s, kb_s, vb_s = mixing_sublayer(
            xs, float(PAST_LEN), state_shift[l], state_wkv[l], swa_s, cache_mem_k[l], cache_mem_v[l], p)
        xs = layer_norm(ALPHA * xs + hs, ln1_g[l], ln1_b[l])
        xs = layer_norm(ALPHA * xs + hier_moe(xs, *moe_w), ln2_g[l], ln2_b[l])
        wkv_p_l.append(wkv_p); wkv_s_l.append(wkv_s); sh_p_l.append(sh_p); sh_s_l.append(sh_s)
        kb_p_l.append(kb_p); vb_p_l.append(vb_p); kb_s_l.append(kb_s); vb_s_l.append(vb_s)
        mk_p_l.append(mk_p); mv_p_l.append(mv_p)
    sd = state_wkv.dtype
    return (xp, xs,
            jnp.stack(wkv_p_l).astype(sd), jnp.stack(wkv_s_l).astype(sd),
            jnp.stack(sh_p_l), jnp.stack(sh_s_l),
            jnp.stack(kb_p_l), jnp.stack(vb_p_l), jnp.stack(kb_s_l), jnp.stack(vb_s_l),
            jnp.stack(mk_p_l), jnp.stack(mv_p_l))
```

```python
import functools
import math

import jax
import jax.numpy as jnp
from jax import lax
from jax.experimental import pallas as pl
from jax.experimental.pallas import tpu as pltpu

F32 = jnp.float32
BF16 = jnp.bfloat16
HI = lax.Precision.HIGHEST

HEAD = 64
LANES = 128
H_KV = 2
ROT_HALF = 8
ROPE_THETA = 500000.0
PAST_LEN = 8192
N_GROUPS = 4
EXPERTS_PER_GROUP = 8
EXPERT_BLOCK = 128
LN_EPS = 1e-5
LNX_EPS = 64e-5
NEG_INF = -1e30
VMEM_LIMIT = 48 * 1024 * 1024


def _hdot(a, b):
    return jnp.dot(a, b, precision=HI, preferred_element_type=F32)


def _hdot_t(a, b):
    return lax.dot_general(a, b, (((1,), (1,)), ((), ())), precision=HI, preferred_element_type=F32)


def _bdot(a, b):
    return jnp.dot(a.astype(BF16), b.astype(BF16), preferred_element_type=F32)


def _ln(x, g, b):
    mu = jnp.mean(x, axis=-1, keepdims=True)
    xc = x - mu
    var = jnp.mean(xc * xc, axis=-1, keepdims=True)
    return xc * lax.rsqrt(var + LN_EPS) * g + b


def _const_spec(shape):
    nd = len(shape)
    return pl.BlockSpec(shape, lambda *_: (0,) * nd)


def _params(sem):
    return pltpu.CompilerParams(dimension_semantics=sem, vmem_limit_bytes=VMEM_LIMIT)


def _ln_proj_kernel(x_ref, g_ref, b_ref, *refs):
    nw = len(refs) // 2
    xn = _ln(x_ref[...], g_ref[...], b_ref[...]).astype(BF16)
    for w_ref, o_ref in zip(refs[:nw], refs[nw:]):
        o_ref[...] = jnp.dot(xn, w_ref[...], preferred_element_type=F32)


def ln_proj(x, g, b, ws, tm):
    n, d = x.shape
    return pl.pallas_call(
        _ln_proj_kernel, name="ln_proj",
        grid=(n // tm,),
        in_specs=[pl.BlockSpec((tm, d), lambda i: (i, 0)), _const_spec((1, d)), _const_spec((1, d))]
        + [_const_spec(w.shape) for w in ws],
        out_specs=[pl.BlockSpec((tm, w.shape[1]), lambda i: (i, 0)) for w in ws],
        out_shape=[jax.ShapeDtypeStruct((n, w.shape[1]), F32) for w in ws],
        compiler_params=_params(("parallel",)),
    )(x, g.reshape(1, d), b.reshape(1, d), *ws)


def _rwkv_prep_kernel(chunk, explicit_prev, z_ref, prev_ref, mu_ref, w0_ref, wup_ref, a0_ref, aup_ref, gup_ref,
                      kk_ref, ka_ref, rk_ref, hsum_ref, tril_ref, csum_ref,
                      ra_ref, at_ref, bt_ref, kt_ref, bh_ref, kh_ref, v_ref, g_ref, bonus_ref, wl_ref, carry_ref):
    z = z_ref[0]
    tt = z.shape[0]
    c_a = w0_ref.shape[-1]
    r_w, r_a, r_g = wup_ref.shape[0], aup_ref.shape[0], gup_ref.shape[0]
    if explicit_prev:
        prev = prev_ref[0]
    else:
        @pl.when(pl.program_id(1) == 0)
        def _():
            carry_ref[...] = prev_ref[0]

        row = lax.broadcasted_iota(jnp.int32, z.shape, 0)
        prev = jnp.where(row == 0, carry_ref[...], pltpu.roll(z, 1, 0))
        carry_ref[...] = z[tt - 1:tt, :]
    zs = z + (prev - z) * mu_ref[...]
    r = zs[:, :c_a]
    k = zs[:, c_a:2 * c_a]
    v = zs[:, 2 * c_a:3 * c_a]
    o = 3 * c_a
    xw = zs[:, o:o + r_w]
    xa = zs[:, o + r_w:o + r_w + r_a]
    xg = zs[:, o + r_w + r_a:o + r_w + r_a + r_g]
    warg = -(w0_ref[...] + _hdot(jnp.tanh(xw), wup_ref[...]))
    softplus = jnp.maximum(warg, 0.0) + jnp.log1p(jnp.exp(-jnp.abs(warg)))
    lw = -jnp.exp(-softplus - 0.5)
    a = jax.nn.sigmoid(a0_ref[...] + _hdot(xa, aup_ref[...]))
    g = _hdot(jax.nn.sigmoid(xg), gup_ref[...])
    kkr = k * kk_ref[...]
    kk = kkr / jnp.maximum(jnp.sqrt(_hdot(kkr * kkr, hsum_ref[...])), 1e-12)
    k2 = k * (1.0 + (a - 1.0) * ka_ref[...])
    bonus = _hdot(r * k2 * rk_ref[...], hsum_ref[...]) * v
    if chunk == 1:
        cw, cwl = lw, lw
    else:
        cw = _hdot(tril_ref[...], lw)
        cwl = _hdot(csum_ref[...], lw)
    e_inv = jnp.exp(-cw)
    e_hat = jnp.exp(cwl - cw)
    kb = kk * a
    ra_ref[0] = r * jnp.exp(cw)
    at_ref[0] = -kk * jnp.exp(cw - lw)
    bt_ref[0] = kb * e_inv
    kt_ref[0] = k2 * e_inv
    bh_ref[0] = kb * e_hat
    kh_ref[0] = k2 * e_hat
    v_ref[0] = v
    g_ref[0] = g
    bonus_ref[0] = bonus
    wl = jnp.exp(cwl)
    if chunk == 1:
        wl_ref[0] = wl
    else:
        for c in range(tt // chunk):
            wl_ref[0, c] = wl[c * chunk:c * chunk + 1, :]


def rwkv_prep(z, prev, p, chunk, tt):
    bsz, t, cs = z.shape
    c_a = p['w0'].shape[-1]
    explicit_prev = prev.shape[1] == t and t > 1
    assert explicit_prev == (chunk == 1)
    ridx = jnp.arange(tt)
    same_chunk = (ridx[:, None] // chunk) == (ridx[None, :] // chunk)
    csum = same_chunk.astype(F32)
    tril = (same_chunk & (ridx[None, :] <= ridx[:, None])).astype(F32)
    cidx = jnp.arange(c_a)
    hsum = ((cidx[:, None] // HEAD) == (cidx[None, :] // HEAD)).astype(F32)
    row = lambda x: x.reshape(1, -1)
    tile = pl.BlockSpec((1, tt, c_a), lambda b, i: (b, i, 0))
    if chunk == 1:
        wl_spec, wl_shape = tile, (bsz, t, c_a)
    else:
        wl_spec = pl.BlockSpec((1, tt // chunk, 1, c_a), lambda b, i: (b, i, 0, 0))
        wl_shape = (bsz, t // chunk, 1, c_a)
    prev_spec = (pl.BlockSpec((1, tt, cs), lambda b, i: (b, i, 0)) if explicit_prev
                 else pl.BlockSpec((1, 1, cs), lambda b, i: (b, 0, 0)))
    consts = [row(p['mu']), row(p['w0']), p['w_up'], row(p['a0']), p['a_up'], p['g_up'], row(p['k_k']),
              row(p['k_a']), row(p['r_k']), hsum, tril, csum]
    return pl.pallas_call(
        functools.partial(_rwkv_prep_kernel, chunk, explicit_prev), name="rwkv_prep",
        grid=(bsz, t // tt),
        in_specs=[pl.BlockSpec((1, tt, cs), lambda b, i: (b, i, 0)), prev_spec] + [_const_spec(c.shape) for c in consts],
        out_specs=[tile] * 9 + [wl_spec],
        out_shape=[jax.ShapeDtypeStruct((bsz, t, c_a), F32)] * 9 + [jax.ShapeDtypeStruct(wl_shape, F32)],
        scratch_shapes=[pltpu.VMEM((1, cs), F32)],
        compiler_params=_params(("parallel", "arbitrary")),
    )(z, prev, *consts)


def _wkv_kernel(ra_ref, at_ref, bt_ref, kt_ref, bh_ref, kh_ref, v_ref, g_ref, bonus_ref, wl_ref, s0_ref,
                lg_ref, lb_ref, o_ref, sout_ref, s_ref):
    c = pl.program_id(1)
    length = ra_ref.shape[1]
    heads = ra_ref.shape[2] // HEAD

    @pl.when(c == 0)
    def _():
        s_ref[...] = s0_ref[0]

    row = lax.broadcasted_iota(jnp.int32, (length, length), 0)
    col = lax.broadcasted_iota(jnp.int32, (length, length), 1)
    strict = row > col
    incl = row >= col
    eye = (row == col).astype(F32)
    ys = []
    for h in range(heads):
        sl = slice(h * HEAD, (h + 1) * HEAD)
        a_t, b_t, k_t, r_t = at_ref[0, :, sl], bt_ref[0, :, sl], kt_ref[0, :, sl], ra_ref[0, :, sl]
        b_h, k_h, v = bh_ref[0, :, sl], kh_ref[0, :, sl], v_ref[0, :, sl]
        s0 = s_ref[h]
        a_ab = jnp.where(strict, _hdot_t(a_t, b_t), 0.0)
        a_ak = jnp.where(strict, _hdot_t(a_t, k_t), 0.0)
        a_rb = jnp.where(incl, _hdot_t(r_t, b_t), 0.0)
        a_rk = jnp.where(incl, _hdot_t(r_t, k_t), 0.0)
        inv = eye + a_ab
        pw = a_ab
        for _ in range(int(math.log2(length)) - 1):
            pw = _hdot(pw, pw)
            inv = inv + _hdot(inv, pw)
        u = _hdot(inv, _hdot_t(a_t, s0) + _hdot(a_ak, v))
        y = _hdot_t(r_t, s0) + _hdot(a_rb, u) + _hdot(a_rk, v)
        s_ref[h] = s0 * wl_ref[0, 0, :, sl] + _hdot(u.T, b_h) + _hdot(v.T, k_h)
        mu = jnp.mean(y, axis=-1, keepdims=True)
        yc = y - mu
        var = jnp.mean(yc * yc, axis=-1, keepdims=True)
        ys.append(yc * lax.rsqrt(var + LNX_EPS))
    y = jnp.concatenate(ys, axis=-1)
    o_ref[0] = (y * lg_ref[...] + lb_ref[...] + bonus_ref[0]) * g_ref[0]

    @pl.when(c == pl.num_programs(1) - 1)
    def _():
        sout_ref[0] = s_ref[...]


def wkv(prep, s0, lnx_g, lnx_b, chunk):
    ra, at, bt, kt, bh, kh, v, g, bonus, wl = prep
    bsz, t, c_a = ra.shape
    heads = c_a // HEAD
    assert chunk & (chunk - 1) == 0 and t % chunk == 0
    tile = pl.BlockSpec((1, chunk, c_a), lambda b, c: (b, c, 0))
    st = pl.BlockSpec((1, heads, HEAD, HEAD), lambda b, c: (b, 0, 0, 0))
    return pl.pallas_call(
        _wkv_kernel, name="wkv",
        grid=(bsz, t // chunk),
        in_specs=[tile] * 9 + [pl.BlockSpec((1, 1, 1, c_a), lambda b, c: (b, c, 0, 0)), st,
                               _const_spec((1, c_a)), _const_spec((1, c_a))],
        out_specs=[tile, st],
        out_shape=[jax.ShapeDtypeStruct((bsz, t, c_a), F32), jax.ShapeDtypeStruct(s0.shape, F32)],
        scratch_shapes=[pltpu.VMEM((heads, HEAD, HEAD), F32)],
        compiler_params=_params(("parallel", "arbitrary")),
    )(ra, at, bt, kt, bh, kh, v, g, bonus, wl, s0, lnx_g.reshape(1, c_a), lnx_b.reshape(1, c_a))


def rope_tables(pos):
    inv_freq = ROPE_THETA ** (-jnp.arange(ROT_HALF, dtype=F32) / ROT_HALF)
    ang = pos[:, None] * inv_freq[None, :]
    cos, sin = jnp.cos(ang), jnp.sin(ang)
    t = pos.shape[0]
    rest = HEAD - 2 * ROT_HALF
    c = jnp.concatenate([cos, cos, jnp.ones((t, rest), F32)], axis=1)
    s1 = jnp.concatenate([jnp.zeros((t, ROT_HALF), F32), sin, jnp.zeros((t, rest), F32)], axis=1)
    s2 = jnp.concatenate([-sin, jnp.zeros((t, HEAD - ROT_HALF), F32)], axis=1)
    rep = LANES // HEAD
    return tuple(jnp.tile(x, (1, rep)) for x in (c, s1, s2))


def _rope(x, c, s1, s2):
    width = x.shape[-1]
    rep = width // c.shape[-1]
    if rep > 1:
        c, s1, s2 = (jnp.concatenate([t] * rep, axis=1) for t in (c, s1, s2))
    return x * c + pltpu.roll(x, ROT_HALF, 1) * s1 + pltpu.roll(x, width - ROT_HALF, 1) * s2


def _attn_kernel(window, q_w, kv_w, zq_ref, kp_ref, vp_ref, rc_ref, rs1_ref, rs2_ref, pc_ref, ps1_ref, ps2_ref,
                 mk_ref, mv_ref, sink_ref, ob_ref, om_ref, kr_ref):
    n = pl.program_id(1)
    zq = zq_ref[0]
    scale = HEAD ** -0.5
    q = zq[:, :q_w]
    k = zq[:, q_w:q_w + kv_w]
    v = zq[:, q_w + kv_w:q_w + 2 * kv_w]
    qm = zq[:, q_w + 2 * kv_w:]
    qr = _rope(q, rc_ref[...], rs1_ref[...], rs2_ref[...]) * scale
    kr = _rope(k, rc_ref[...], rs1_ref[...], rs2_ref[...])
    kr_ref[0] = kr
    kpr = _rope(kp_ref[0], pc_ref[...], ps1_ref[...], ps2_ref[...])
    vp = vp_ref[0]
    gqa = q_w // kv_w
    assert window & (window - 1) == 0
    qi = lax.broadcasted_iota(jnp.int32, (gqa * window, 2 * window), 0) & (window - 1)
    kj = lax.broadcasted_iota(jnp.int32, (gqa * window, 2 * window), 1)
    valid = (kj > qi) & (kj <= qi + window) & ((kj >= window) | (n > 0))
    outs = [None] * (q_w // HEAD)
    for hk in range(kv_w // HEAD):
        sl = slice(hk * HEAD, (hk + 1) * HEAD)
        kcat = jnp.concatenate([kpr[:, sl], kr[:, sl]], axis=0).astype(BF16)
        vcat = jnp.concatenate([vp[:, sl], v[:, sl]], axis=0).astype(BF16)
        heads = [hk * gqa + g for g in range(gqa)]
        qs = jnp.concatenate([qr[:, h * HEAD:(h + 1) * HEAD] for h in heads], axis=0).astype(BF16)
        s = lax.dot_general(qs, kcat, (((1,), (1,)), ((), ())), preferred_element_type=F32)
        s = jnp.where(valid, s, NEG_INF)
        sink = jnp.concatenate([jnp.full((window, 1), sink_ref[h], F32) for h in heads], axis=0)
        m = jnp.maximum(jnp.max(s, axis=-1, keepdims=True), sink)
        p = jnp.exp(s - m)
        denom = jnp.sum(p, axis=-1, keepdims=True) + jnp.exp(sink - m)
        o = jnp.dot(p.astype(BF16), vcat, preferred_element_type=F32) / denom
        for g, h in enumerate(heads):
            outs[h] = o[g * window:(g + 1) * window]
    ob_ref[0] = jnp.concatenate(outs, axis=1)
    qmb = (qm * scale).astype(BF16)
    mk = mk_ref[0].astype(BF16)
    mv = mv_ref[0].astype(BF16)
    mouts = []
    for h in range(qm.shape[1] // HEAD):
        sl = slice(h * HEAD, (h + 1) * HEAD)
        s = lax.dot_general(qmb[:, sl], mk[:, sl], (((1,), (1,)), ((), ())), preferred_element_type=F32)
        p = jnp.exp(s - jnp.max(s, axis=-1, keepdims=True))
        mouts.append(jnp.dot(p.astype(BF16), mv[:, sl], preferred_element_type=F32)
                     / jnp.sum(p, axis=-1, keepdims=True))
    om_ref[0] = jnp.concatenate(mouts, axis=1)


def attn_prompt(zq, mk, mv, sinks, tables, window, q_w, kv_w):
    bsz, t, zw = zq.shape
    mem_w = zw - q_w - 2 * kv_w
    assert kv_w == LANES and q_w % kv_w == 0
    kcol, vcol = q_w // kv_w, q_w // kv_w + 1
    prev = lambda n: jnp.maximum(n - 1, 0)
    tab = pl.BlockSpec((window, LANES), lambda b, n: (n, 0))
    ptab = pl.BlockSpec((window, LANES), lambda b, n: (prev(n), 0))
    mem = pl.BlockSpec((1,) + mk.shape[1:], lambda b, n: (b, 0, 0))
    return pl.pallas_call(
        functools.partial(_attn_kernel, window, q_w, kv_w), name="attn",
        grid=(bsz, t // window),
        in_specs=[pl.BlockSpec((1, window, zw), lambda b, n: (b, n, 0)),
                  pl.BlockSpec((1, window, kv_w), lambda b, n: (b, prev(n), kcol)),
                  pl.BlockSpec((1, window, kv_w), lambda b, n: (b, prev(n), vcol)),
                  tab, tab, tab, ptab, ptab, ptab, mem, mem,
                  pl.BlockSpec(memory_space=pltpu.SMEM)],
        out_specs=[pl.BlockSpec((1, window, q_w), lambda b, n: (b, n, 0)),
                   pl.BlockSpec((1, window, mem_w), lambda b, n: (b, n, 0)),
                   pl.BlockSpec((1, window, kv_w), lambda b, n: (b, n, 0))],
        out_shape=[jax.ShapeDtypeStruct((bsz, t, q_w), F32), jax.ShapeDtypeStruct((bsz, t, mem_w), F32),
                   jax.ShapeDtypeStruct((bsz, t, kv_w), F32)],
        compiler_params=_params(("parallel", "parallel")),
    )(zq, zq, zq, *tables, *tables, mk, mv, sinks)


def _seg_attend(qrow, keys, vals, seg_sum, seg_exp, valid, extra):
    s = _hdot(keys * qrow, seg_sum)
    if valid is not None:
        s = jnp.where(valid, s, NEG_INF)
    m = jnp.max(s, axis=0, keepdims=True)
    if extra is not None:
        k_new, v_new, sink = extra
        s_new = _hdot(k_new * qrow, seg_sum)
        m = jnp.maximum(m, jnp.maximum(s_new, sink))
    p = jnp.exp(s - m)
    denom = jnp.sum(p, axis=0, keepdims=True)
    acc = jnp.sum(_hdot(p, seg_exp) * vals, axis=0, keepdims=True)
    if extra is not None:
        p_new = jnp.exp(s_new - m)
        denom = denom + p_new + jnp.exp(sink - m)
        acc = acc + _hdot(p_new, seg_exp) * v_new
    return acc / _hdot(denom, seg_exp)


def _attn_step_kernel(q_w, kv_w, zq_ref, ck_ref, cv_ref, mk_ref, mv_ref, rc_ref, rs1_ref, rs2_ref, sink_ref,
                      ssum_ref, sexp_ref, msum_ref, mexp_ref, ob_ref, om_ref, nk_ref, nv_ref):
    bt = zq_ref.shape[0]
    window = ck_ref.shape[1]
    scale = HEAD ** -0.5
    gqa = q_w // kv_w
    n_kv = kv_w // HEAD
    lane = lax.broadcasted_iota(jnp.int32, (1, LANES), 1)
    wrow = lax.broadcasted_iota(jnp.int32, (window, kv_w), 0)
    valid = lax.broadcasted_iota(jnp.int32, (window, LANES), 0) >= 1
    for b in range(bt):
        zq = zq_ref[b:b + 1, :]
        q = _rope(zq[:, :q_w], rc_ref[...], rs1_ref[...], rs2_ref[...]) * scale
        k_new = _rope(zq[:, q_w:q_w + kv_w], rc_ref[...], rs1_ref[...], rs2_ref[...])
        v_new = zq[:, q_w + kv_w:q_w + 2 * kv_w]
        qm = zq[:, q_w + 2 * kv_w:] * scale
        ck, cv = ck_ref[b], cv_ref[b]
        nk_ref[b] = jnp.where(wrow == window - 1, k_new, pltpu.roll(ck, window - 1, 0))
        nv_ref[b] = jnp.where(wrow == window - 1, v_new, pltpu.roll(cv, window - 1, 0))
        og = []
        for g in range(gqa):
            qg = jnp.concatenate([q[:, (hk * gqa + g) * HEAD:(hk * gqa + g + 1) * HEAD] for hk in range(n_kv)], axis=1)
            sink = jnp.zeros((1, LANES), F32)
            for hk in range(n_kv):
                sink = jnp.where(lane == hk, sink_ref[hk * gqa + g], sink)
            og.append(_seg_attend(qg, ck, cv, ssum_ref[...], sexp_ref[...], valid, (k_new, v_new, sink)))
        ob_ref[b:b + 1, :] = jnp.concatenate(
            [og[g][:, hk * HEAD:(hk + 1) * HEAD] for hk in range(n_kv) for g in range(gqa)], axis=1)
        om_ref[b:b + 1, :] = _seg_attend(qm, mk_ref[b], mv_ref[b], msum_ref[...], mexp_ref[...], None, None)


def _seg_mats(width):
    lane = jnp.arange(width)
    seg_sum = (lane[:, None] // HEAD == jnp.arange(LANES)[None, :]).astype(F32)
    return seg_sum, seg_sum.T


def attn_step(zq, ck, cv, mk, mv, sinks, tables, q_w, kv_w, bt):
    bsz, zw = zq.shape
    mem_w = zw - q_w - 2 * kv_w
    window, n_mem = ck.shape[1], mk.shape[1]
    ssum, sexp = _seg_mats(kv_w)
    msum, mexp = _seg_mats(mem_w)
    rows = lambda w: pl.BlockSpec((bt, w), lambda i: (i, 0))
    cache = pl.BlockSpec((bt, window, kv_w), lambda i: (i, 0, 0))
    mem = pl.BlockSpec((bt, n_mem, mem_w), lambda i: (i, 0, 0))
    consts = list(tables) + [sinks, ssum, sexp, msum, mexp]
    const_specs = [_const_spec(t.shape) for t in tables] + [pl.BlockSpec(memory_space=pltpu.SMEM)] + [
        _const_spec(m.shape) for m in (ssum, sexp, msum, mexp)]
    return pl.pallas_call(
        functools.partial(_attn_step_kernel, q_w, kv_w), name="attn_step",
        grid=(bsz // bt,),
        in_specs=[rows(zw), cache, cache, mem, mem] + const_specs,
        out_specs=[rows(q_w), rows(mem_w), cache, cache],
        out_shape=[jax.ShapeDtypeStruct((bsz, q_w), F32), jax.ShapeDtypeStruct((bsz, mem_w), F32),
                   jax.ShapeDtypeStruct(ck.shape, F32), jax.ShapeDtypeStruct(cv.shape, F32)],
        compiler_params=_params(("parallel",)),
    )(zq, ck, cv, mk, mv, *consts)


def _proj_kernel(x_ref, w_ref, o_ref):
    o_ref[...] = jnp.dot(x_ref[...].astype(BF16), w_ref[...], preferred_element_type=F32)


def proj(x, w, tm):
    n, d = x.shape
    return pl.pallas_call(
        _proj_kernel, name="proj",
        grid=(n // tm,),
        in_specs=[pl.BlockSpec((tm, d), lambda i: (i, 0)), _const_spec(w.shape)],
        out_specs=pl.BlockSpec((tm, w.shape[1]), lambda i: (i, 0)),
        out_shape=jax.ShapeDtypeStruct((n, w.shape[1]), F32),
        compiler_params=_params(("parallel",)),
    )(x, w)


def _merge_kernel(alpha, x_ref, oa_ref, ob_ref, om_ref, gt_ref, lig_ref, lib_ref, pa_ref, pb_ref, pm_ref, wo_ref,
                  l1g_ref, l1b_ref, wr_ref, br_ref, x1_ref, eidx_ref, gate_ref):
    d = x_ref.shape[1]
    xn = _ln(x_ref[...], lig_ref[...], lib_ref[...])
    gts = jax.nn.sigmoid(gt_ref[...])
    merged = (gts[:, :d] * _bdot(oa_ref[...], pa_ref[...]) + gts[:, d:2 * d] * _bdot(ob_ref[...], pb_ref[...])
              + gts[:, 2 * d:] * _bdot(om_ref[...], pm_ref[...]))
    x1 = _ln(alpha * xn + _bdot(merged, wo_ref[...]), l1g_ref[...], l1b_ref[...])
    x1_ref[...] = x1
    logits = _hdot(x1, wr_ref[...]) + br_ref[...]
    lane = lax.broadcasted_iota(jnp.int32, logits.shape, 1)
    lane_f = lane.astype(F32)
    first = lambda hit: jnp.min(jnp.where(hit, lane_f, float(LANES)), axis=-1, keepdims=True).astype(jnp.int32)
    gmask = lane < N_GROUPS
    gl = jnp.where(gmask, logits, NEG_INF)
    gmax = jnp.max(gl, axis=-1, keepdims=True)
    gidx = first(gl == gmax)
    g_w = 1.0 / jnp.sum(jnp.where(gmask, jnp.exp(gl - gmax), 0.0), axis=-1, keepdims=True)
    lo = N_GROUPS + gidx * EXPERTS_PER_GROUP
    el = jnp.where((lane >= lo) & (lane < lo + EXPERTS_PER_GROUP), logits, NEG_INF)
    v1 = jnp.max(el, axis=-1, keepdims=True)
    i1 = first(el == v1)
    el2 = jnp.where(lane == i1, NEG_INF, el)
    v2 = jnp.max(el2, axis=-1, keepdims=True)
    i2 = first(el2 == v2)
    e2 = jnp.exp(v2 - v1)
    gate1 = g_w / (1.0 + e2)
    eidx_ref[...] = jnp.where(lane == 0, i1 - N_GROUPS, jnp.where(lane == 1, i2 - N_GROUPS, 0))
    gate_ref[...] = jnp.where(lane == 0, gate1, jnp.where(lane == 1, gate1 * e2, 0.0))


def merge(x, oa, ob, om, gt, w, tm, alpha):
    n, d = x.shape
    rows = lambda a: pl.BlockSpec((tm, a.shape[1]), lambda i: (i, 0))
    consts = [w['ln_in_g'], w['ln_in_b'], w['p_a'], w['p_b'], w['p_m'], w['w_o'], w['ln1_g'], w['ln1_b'],
              w['w_route'], w['b_route']]
    return pl.pallas_call(
        functools.partial(_merge_kernel, alpha), name="merge",
        grid=(n // tm,),
        in_specs=[rows(a) for a in (x, oa, ob, om, gt)] + [_const_spec(c.shape) for c in consts],
        out_specs=[pl.BlockSpec((tm, d), lambda i: (i, 0)), pl.BlockSpec((tm, LANES), lambda i: (i, 0)),
                   pl.BlockSpec((tm, LANES), lambda i: (i, 0))],
        out_shape=[jax.ShapeDtypeStruct((n, d), F32), jax.ShapeDtypeStruct((n, LANES), jnp.int32),
                   jax.ShapeDtypeStruct((n, LANES), F32)],
        compiler_params=_params(("parallel",)),
    )(x, oa, ob, om, gt, *consts)


def _gather_rows(idx_ref, base, count, src_hbm, dst, sem, start):
    def body(r, carry):
        copy = pltpu.make_async_copy(src_hbm.at[pl.ds(idx_ref[base + r], 1)], dst.at[pl.ds(r, 1)], sem)
        if start:
            copy.start()
        else:
            copy.wait()
        return carry
    lax.fori_loop(0, count, body, 0, unroll=8)


def _moe_expert_kernel(tok_ref, be_ref, nu_ref, x_hbm, wg_ref, wu_ref, wd_ref, y_ref, xbuf, sem):
    i = pl.program_id(0)
    used = nu_ref[0]
    rows = xbuf.shape[1]

    @pl.when((i == 0) & (used > 0))
    def _():
        _gather_rows(tok_ref, 0, rows, x_hbm, xbuf.at[0], sem.at[0], True)

    @pl.when(i + 1 < used)
    def _():
        nxt = (i + 1) % 2
        _gather_rows(tok_ref, (i + 1) * rows, rows, x_hbm, xbuf.at[nxt], sem.at[nxt], True)

    @pl.when(i < used)
    def _():
        slot = i % 2
        _gather_rows(tok_ref, i * rows, rows, x_hbm, xbuf.at[slot], sem.at[slot], False)
        xb = xbuf[slot].astype(BF16)
        hg = jnp.dot(xb, wg_ref[0].astype(BF16), preferred_element_type=F32)
        hu = jnp.dot(xb, wu_ref[0].astype(BF16), preferred_element_type=F32)
        h = hg * jax.nn.sigmoid(hg) * hu
        y_ref[...] = jnp.dot(h.astype(BF16), wd_ref[0].astype(BF16), preferred_element_type=F32)

    @pl.when(i >= used)
    def _():
        y_ref[...] = jnp.zeros_like(y_ref)


def moe_experts(x1, row_tok, blk_e, n_used, e_gate, e_up, e_down):
    n, d = x1.shape
    n_blocks = blk_e.shape[0]
    ff = e_gate.shape[-1]
    return pl.pallas_call(
        _moe_expert_kernel, name="moe_expert",
        grid_spec=pltpu.PrefetchScalarGridSpec(
            num_scalar_prefetch=3,
            grid=(n_blocks,),
            in_specs=[pl.BlockSpec(memory_space=pl.ANY),
                      pl.BlockSpec((1, d, ff), lambda i, tok, be, nu: (be[i], 0, 0)),
                      pl.BlockSpec((1, d, ff), lambda i, tok, be, nu: (be[i], 0, 0)),
                      pl.BlockSpec((1, ff, d), lambda i, tok, be, nu: (be[i], 0, 0))],
            out_specs=pl.BlockSpec((EXPERT_BLOCK, d), lambda i, tok, be, nu: (i, 0)),
            scratch_shapes=[pltpu.VMEM((2, EXPERT_BLOCK, d), F32), pltpu.SemaphoreType.DMA((2,))]),
        out_shape=jax.ShapeDtypeStruct((n_blocks * EXPERT_BLOCK, d), F32),
        compiler_params=_params(("arbitrary",)),
    )(row_tok, blk_e, n_used, x1, e_gate, e_up, e_down)


def _moe_combine_kernel(alpha, dest_ref, ys_hbm, x1_ref, gate_ref, g_ref, b_ref, o_ref, buf, sem):
    i = pl.program_id(0)
    steps = pl.num_programs(0)
    rows = buf.shape[1]

    @pl.when(i == 0)
    def _():
        _gather_rows(dest_ref, 0, rows, ys_hbm, buf.at[0], sem.at[0], True)

    @pl.when(i + 1 < steps)
    def _():
        nxt = (i + 1) % 2
        _gather_rows(dest_ref, (i + 1) * rows, rows, ys_hbm, buf.at[nxt], sem.at[nxt], True)

    slot = i % 2
    _gather_rows(dest_ref, i * rows, rows, ys_hbm, buf.at[slot], sem.at[slot], False)
    tm = x1_ref.shape[0]
    gate = gate_ref[...]
    moe = gate[:, 0:1] * buf[slot, :tm, :] + gate[:, 1:2] * buf[slot, tm:, :]
    o_ref[...] = _ln(alpha * x1_ref[...] + moe, g_ref[...], b_ref[...])


def moe_combine(ys, dest, x1, gate, g, b, tm, alpha):
    n, d = x1.shape
    top_k = dest.shape[0] // n
    assert top_k == 2
    return pl.pallas_call(
        functools.partial(_moe_combine_kernel, alpha), name="moe_combine",
        grid_spec=pltpu.PrefetchScalarGridSpec(
            num_scalar_prefetch=1,
            grid=(n // tm,),
            in_specs=[pl.BlockSpec(memory_space=pl.ANY),
                      pl.BlockSpec((tm, d), lambda i, dest: (i, 0)),
                      pl.BlockSpec((tm, LANES), lambda i, dest: (i, 0)),
                      pl.BlockSpec((1, d), lambda i, dest: (0, 0)),
                      pl.BlockSpec((1, d), lambda i, dest: (0, 0))],
            out_specs=pl.BlockSpec((tm, d), lambda i, dest: (i, 0)),
            scratch_shapes=[pltpu.VMEM((2, top_k * tm, d), F32), pltpu.SemaphoreType.DMA((2,))]),
        out_shape=jax.ShapeDtypeStruct((n, d), F32),
        compiler_params=_params(("arbitrary",)),
    )(dest, ys, x1, gate, g.reshape(1, d), b.reshape(1, d))


def moe_routing(experts):
    n, top_k = experts.shape
    n_exp = N_GROUPS * EXPERTS_PER_GROUP
    a = n * top_k
    flat_e = experts.reshape(a)
    onehot = (flat_e[:, None] == jnp.arange(n_exp, dtype=jnp.int32)[None, :]).astype(jnp.int32)
    rank = jnp.take_along_axis(jnp.cumsum(onehot, axis=0) - onehot, flat_e[:, None], axis=1)[:, 0]
    counts = jnp.sum(onehot, axis=0)
    padded = (counts + EXPERT_BLOCK - 1) // EXPERT_BLOCK * EXPERT_BLOCK
    pad_end = jnp.cumsum(padded)
    dest = (pad_end - padded)[flat_e] + rank
    n_blocks = -(-a // EXPERT_BLOCK) + n_exp
    row_tok = jnp.zeros((n_blocks * EXPERT_BLOCK,), jnp.int32).at[dest].set(jnp.arange(a, dtype=jnp.int32) // top_k)
    blk_e = jnp.minimum(jnp.searchsorted(pad_end, jnp.arange(n_blocks, dtype=jnp.int32) * EXPERT_BLOCK, side='right'),
                        n_exp - 1).astype(jnp.int32)
    n_used = (pad_end[-1:] // EXPERT_BLOCK).astype(jnp.int32)
    return row_tok, dest.astype(jnp.int32), blk_e, n_used


def hier_moe_ln(x1, eidx, gate, w, tm, alpha):
    row_tok, dest, blk_e, n_used = moe_routing(eidx[:, :2])
    ys = moe_experts(x1, row_tok, blk_e, n_used, w['e_gate'], w['e_up'], w['e_down'])
    dest = dest.reshape(-1, tm, 2).transpose(0, 2, 1).reshape(-1)
    return moe_combine(ys, dest, x1, gate, w['ln2_g'], w['ln2_b'], tm, alpha)


def kernel(x_prompt, x_sample, mem_prompt, state_wkv, state_shift, cache_win_k, cache_win_v, cache_mem_k, cache_mem_v, ln_in_g, ln_in_b, w_in, mu, w0, w_up, a0, a_up, g_up, k_k, k_a, r_k, lnx_g, lnx_b, sinks, w_mem_kv, p_a, p_b, p_m, w_o, ln1_g, ln1_b, w_group, b_group, w_router, b_router, e_gate, e_up, e_down, ln2_g, ln2_b):
    depth = w_in.shape[0]
    assert depth == 1, "single-layer step"
    bsz, seq, d = x_prompt.shape
    dec = x_sample.shape[0]
    assert x_sample.shape[1] == 1
    c_shift = mu.shape[-1]
    c_a = w0.shape[-1]
    window, kv_w = cache_win_k.shape[2], cache_win_k.shape[3] * cache_win_k.shape[4]
    n_mem, mem_w = cache_mem_k.shape[2], cache_mem_k.shape[3] * cache_mem_k.shape[4]
    q_w = sinks.shape[-1] * HEAD
    qkvm_w = q_w + 2 * kv_w + mem_w
    alpha = (2.0 * depth) ** 0.25
    past_len = float(PAST_LEN)
    chunk = 64
    dec_chunk = 8

    w_in_b = w_in[0].astype(BF16)
    w_parts = [w_in_b[:, :c_shift], w_in_b[:, c_shift:c_shift + qkvm_w], w_in_b[:, c_shift + qkvm_w:]]
    rp = dict(mu=mu[0], w0=w0[0], w_up=w_up[0], a0=a0[0], a_up=a_up[0], g_up=g_up[0], k_k=k_k[0], k_a=k_a[0],
              r_k=r_k[0].reshape(-1))
    n_route = N_GROUPS * (1 + EXPERTS_PER_GROUP)
    mw = dict(ln_in_g=ln_in_g.reshape(1, d), ln_in_b=ln_in_b.reshape(1, d), p_a=p_a[0].astype(BF16),
              p_b=p_b[0].astype(BF16), p_m=p_m[0].astype(BF16), w_o=w_o[0].astype(BF16),
              ln1_g=ln1_g[0].reshape(1, d), ln1_b=ln1_b[0].reshape(1, d),
              w_route=jnp.pad(jnp.concatenate([w_group[0], w_router[0]], axis=1), ((0, 0), (0, LANES - n_route))),
              b_route=jnp.pad(jnp.concatenate([b_group[0], b_router[0]]), (0, LANES - n_route)).reshape(1, LANES),
              e_gate=e_gate[0], e_up=e_up[0], e_down=e_down[0], ln2_g=ln2_g[0], ln2_b=ln2_b[0])

    xp = x_prompt.reshape(bsz * seq, d)
    zr, zq, zg = ln_proj(xp, ln_in_g, ln_in_b, w_parts, 256)
    zr3 = zr.reshape(bsz, seq, c_shift)
    prep = rwkv_prep(zr3, jnp.zeros((bsz, 1, c_shift), F32), rp, chunk, 256)
    o_a, wkv_p = wkv(prep, jnp.zeros((bsz, c_a // HEAD, HEAD, HEAD), F32), lnx_g[0], lnx_b[0], chunk)
    mkv = proj(mem_prompt.reshape(bsz * n_mem, d), w_mem_kv[0].astype(BF16), 256).reshape(bsz, n_mem, 2 * mem_w)
    mk_p, mv_p = mkv[..., :mem_w], mkv[..., mem_w:]
    tables = rope_tables(jnp.arange(seq, dtype=F32))
    o_b, o_m, k_rot = attn_prompt(zq.reshape(bsz, seq, qkvm_w), mk_p, mv_p, sinks[0], tables, window, q_w, kv_w)
    x1, eidx, gate = merge(xp, o_a.reshape(-1, c_a), o_b.reshape(-1, q_w), o_m.reshape(-1, mem_w), zg, mw, 256, alpha)
    y_prompt = hier_moe_ln(x1, eidx, gate, mw, 128, alpha).reshape(bsz, seq, d)
    shift_p = zr3[:, -1]
    kb_p = k_rot[:, -window:].reshape(bsz, window, H_KV, HEAD)
    vb_p = zq.reshape(bsz, seq, qkvm_w)[:, -window:, q_w + kv_w:q_w + 2 * kv_w].reshape(bsz, window, H_KV, HEAD)

    xs = x_sample.reshape(dec, d)
    zr_s, zq_s, zg_s = ln_proj(xs, ln_in_g, ln_in_b, w_parts, dec)
    prep_s = rwkv_prep(zr_s.reshape(1, dec, c_shift), state_shift[0].reshape(1, dec, c_shift), rp, 1, dec)
    pad_t = lambda a: jnp.pad(a.reshape(dec, 1, c_a), ((0, 0), (0, dec_chunk - 1), (0, 0)))
    prep_s = [pad_t(a) for a in prep_s[:9]] + [prep_s[9].reshape(dec, 1, 1, c_a)]
    o_a_s, wkv_s = wkv(prep_s, state_wkv[0], lnx_g[0], lnx_b[0], dec_chunk)
    tables_s = rope_tables(jnp.full((1,), past_len, F32))
    o_b_s, o_m_s, nk_s, nv_s = attn_step(
        zq_s, cache_win_k[0].reshape(dec, window, kv_w), cache_win_v[0].reshape(dec, window, kv_w),
        cache_mem_k[0].reshape(dec, n_mem, mem_w), cache_mem_v[0].reshape(dec, n_mem, mem_w),
        sinks[0], tables_s, q_w, kv_w, 8)
    x1_s, eidx_s, gate_s = merge(xs, o_a_s[:, 0], o_b_s, o_m_s, zg_s, mw, dec, alpha)
    y_sample = hier_moe_ln(x1_s, eidx_s, gate_s, mw, dec, alpha).reshape(dec, 1, d)

    sd = state_wkv.dtype
    return (y_prompt, y_sample, wkv_p[None].astype(sd), wkv_s[None].astype(sd), shift_p[None], zr_s[None],
            kb_p[None], vb_p[None], nk_s.reshape(dec, window, H_KV, HEAD)[None],
            nv_s.reshape(dec, window, H_KV, HEAD)[None],
            mk_p.reshape(bsz, n_mem, -1, HEAD)[None], mv_p.reshape(bsz, n_mem, -1, HEAD)[None])
```

```python
import functools
import math

import jax
import jax.numpy as jnp
from jax import lax
from jax.experimental import pallas as pl
from jax.experimental.pallas import tpu as pltpu

F32 = jnp.float32
BF16 = jnp.bfloat16
HI = lax.Precision.HIGHEST

HEAD = 64
LANES = 128
H_KV = 2
ROT_HALF = 8
ROPE_THETA = 500000.0
PAST_LEN = 8192
N_GROUPS = 4
EXPERTS_PER_GROUP = 8
EXPERT_BLOCK = 128
LN_EPS = 1e-5
LNX_EPS = 64e-5
NEG_INF = -1e30
VMEM_LIMIT = 48 * 1024 * 1024


def _hdot(a, b):
    return jnp.dot(a, b, precision=HI, preferred_element_type=F32)


def _hdot_t(a, b):
    return lax.dot_general(a, b, (((1,), (1,)), ((), ())), precision=HI, preferred_element_type=F32)


def _bdot(a, b):
    return jnp.dot(a.astype(BF16), b.astype(BF16), preferred_element_type=F32)


def _ln(x, g, b):
    mu = jnp.mean(x, axis=-1, keepdims=True)
    xc = x - mu
    var = jnp.mean(xc * xc, axis=-1, keepdims=True)
    return xc * lax.rsqrt(var + LN_EPS) * g + b


def _const_spec(shape):
    nd = len(shape)
    return pl.BlockSpec(shape, lambda *_: (0,) * nd)


def _params(sem):
    return pltpu.CompilerParams(dimension_semantics=sem, vmem_limit_bytes=VMEM_LIMIT)


def _ln_proj_kernel(x_ref, g_ref, b_ref, *refs):
    nw = len(refs) // 2
    xn = _ln(x_ref[...], g_ref[...], b_ref[...]).astype(BF16)
    for w_ref, o_ref in zip(refs[:nw], refs[nw:]):
        o_ref[...] = jnp.dot(xn, w_ref[...], preferred_element_type=F32)


def ln_proj(x, g, b, ws, tm):
    n, d = x.shape
    return pl.pallas_call(
        _ln_proj_kernel, name="ln_proj",
        grid=(n // tm,),
        in_specs=[pl.BlockSpec((tm, d), lambda i: (i, 0)), _const_spec((1, d)), _const_spec((1, d))]
        + [_const_spec(w.shape) for w in ws],
        out_specs=[pl.BlockSpec((tm, w.shape[1]), lambda i: (i, 0)) for w in ws],
        out_shape=[jax.ShapeDtypeStruct((n, w.shape[1]), F32) for w in ws],
        compiler_params=_params(("parallel",)),
    )(x, g.reshape(1, d), b.reshape(1, d), *ws)


def _rwkv_prep_kernel(chunk, z_ref, prev_ref, mu_ref, w0_ref, wup_ref, a0_ref, aup_ref, gup_ref,
                      kk_ref, ka_ref, rk_ref, hsum_ref, tril_ref, csum_ref, *refs):
    out_refs, carry_ref = refs[:-1], refs[-1]
    z = z_ref[0]
    tt = z.shape[0]
    c_a = w0_ref.shape[-1]
    r_w, r_a, r_g = wup_ref.shape[0], aup_ref.shape[0], gup_ref.shape[0]
    if chunk == 1:
        prev = prev_ref[0]
    else:
        @pl.when(pl.program_id(1) == 0)
        def _():
            carry_ref[...] = prev_ref[0]

        row = lax.broadcasted_iota(jnp.int32, z.shape, 0)
        prev = jnp.where(row == 0, carry_ref[...], pltpu.roll(z, 1, 0))
        carry_ref[...] = z[tt - 1:tt, :]
    zs = z + (prev - z) * mu_ref[...]
    r = zs[:, :c_a]
    k = zs[:, c_a:2 * c_a]
    v = zs[:, 2 * c_a:3 * c_a]
    o = 3 * c_a
    xw = zs[:, o:o + r_w]
    xa = zs[:, o + r_w:o + r_w + r_a]
    xg = zs[:, o + r_w + r_a:o + r_w + r_a + r_g]
    warg = -(w0_ref[...] + _hdot(jnp.tanh(xw), wup_ref[...]))
    softplus = jnp.maximum(warg, 0.0) + jnp.log1p(jnp.exp(-jnp.abs(warg)))
    lw = -jnp.exp(-softplus - 0.5)
    a = jax.nn.sigmoid(a0_ref[...] + _hdot(xa, aup_ref[...]))
    g = _hdot(jax.nn.sigmoid(xg), gup_ref[...])
    kkr = k * kk_ref[...]
    kk = kkr / jnp.maximum(jnp.sqrt(_hdot(kkr * kkr, hsum_ref[...])), 1e-12)
    k2 = k * (1.0 + (a - 1.0) * ka_ref[...])
    bonus = _hdot(r * k2 * rk_ref[...], hsum_ref[...]) * v
    kb = kk * a
    if chunk == 1:
        outs = (r, jnp.exp(lw), k2, v, -kk, kb, g, bonus)
        for ref, val in zip(out_refs, outs):
            ref[0] = val
        return
    cw = _hdot(tril_ref[...], lw)
    cwl = _hdot(csum_ref[...], lw)
    e_inv = jnp.exp(-cw)
    e_hat = jnp.exp(cwl - cw)
    outs = (r * jnp.exp(cw), -kk * jnp.exp(cw - lw), kb * e_inv, k2 * e_inv, kb * e_hat, k2 * e_hat, v, g, bonus)
    for ref, val in zip(out_refs[:-1], outs):
        ref[0] = val
    wl = jnp.exp(cwl)
    for c in range(tt // chunk):
        out_refs[-1][0, c] = wl[c * chunk:c * chunk + 1, :]


def rwkv_prep(z, prev, p, chunk, tt):
    bsz, t, cs = z.shape
    c_a = p['w0'].shape[-1]
    assert prev.shape[1] == (t if chunk == 1 else 1)
    ridx = jnp.arange(tt)
    same_chunk = (ridx[:, None] // chunk) == (ridx[None, :] // chunk)
    csum = same_chunk.astype(F32)
    tril = (same_chunk & (ridx[None, :] <= ridx[:, None])).astype(F32)
    cidx = jnp.arange(c_a)
    hsum = ((cidx[:, None] // HEAD) == (cidx[None, :] // HEAD)).astype(F32)
    row = lambda x: x.reshape(1, -1)
    tile = pl.BlockSpec((1, tt, c_a), lambda b, i: (b, i, 0))
    full = jax.ShapeDtypeStruct((bsz, t, c_a), F32)
    if chunk == 1:
        out_specs, out_shape = [tile] * 8, [full] * 8
        prev_spec = pl.BlockSpec((1, tt, cs), lambda b, i: (b, i, 0))
    else:
        out_specs = [tile] * 9 + [pl.BlockSpec((1, tt // chunk, 1, c_a), lambda b, i: (b, i, 0, 0))]
        out_shape = [full] * 9 + [jax.ShapeDtypeStruct((bsz, t // chunk, 1, c_a), F32)]
        prev_spec = pl.BlockSpec((1, 1, cs), lambda b, i: (b, 0, 0))
    consts = [row(p['mu']), row(p['w0']), p['w_up'], row(p['a0']), p['a_up'], p['g_up'], row(p['k_k']),
              row(p['k_a']), row(p['r_k']), hsum, tril, csum]
    return pl.pallas_call(
        functools.partial(_rwkv_prep_kernel, chunk), name="rwkv_prep",
        grid=(bsz, t // tt),
        in_specs=[pl.BlockSpec((1, tt, cs), lambda b, i: (b, i, 0)), prev_spec] + [_const_spec(c.shape) for c in consts],
        out_specs=out_specs,
        out_shape=out_shape,
        scratch_shapes=[pltpu.VMEM((1, cs), F32)],
        compiler_params=_params(("parallel", "arbitrary")),
    )(z, prev, *consts)


def _split3(x, axis, lhs):
    hi = x.astype(BF16).astype(F32)
    lo = x - hi
    return jnp.concatenate([hi, hi, lo] if lhs else [hi, lo, hi], axis=axis).astype(BF16)


def _dot3(a, b):
    return jnp.dot(_split3(a, 1, True), _split3(b, 0, False), preferred_element_type=F32)


def _dot3_t(a, b):
    return lax.dot_general(_split3(a, 1, True), _split3(b, 1, False), (((1,), (1,)), ((), ())),
                           preferred_element_type=F32)


def _wkv_chunk_kernel(ra_ref, at_ref, bt_ref, kt_ref, bh_ref, kh_ref, v_ref, wl_ref, rp_ref, y0_ref, m_ref, n_ref):
    length = ra_ref.shape[1]
    heads = ra_ref.shape[2] // HEAD
    row = lax.broadcasted_iota(jnp.int32, (length, length), 0)
    col = lax.broadcasted_iota(jnp.int32, (length, length), 1)
    strict = row > col
    incl = row >= col
    hrow = lax.broadcasted_iota(jnp.int32, (HEAD, HEAD), 0)
    hcol = lax.broadcasted_iota(jnp.int32, (HEAD, HEAD), 1)
    hs = range(heads)
    sls = [slice(h * HEAD, (h + 1) * HEAD) for h in hs]
    gram = [_dot3_t(jnp.concatenate([at_ref[0, :, sl], ra_ref[0, :, sl]], axis=0),
                    jnp.concatenate([bt_ref[0, :, sl], kt_ref[0, :, sl]], axis=0)) for sl in sls]
    a_ab = [jnp.where(strict, g[:length, :length], 0.0) for g in gram]
    a_ak = [jnp.where(strict, g[:length, length:], 0.0) for g in gram]
    a_rb = [jnp.where(incl, g[length:, :length], 0.0) for g in gram]
    a_rk = [jnp.where(incl, g[length:, length:], 0.0) for g in gram]
    inv = [jnp.where(row == col, 1.0, a) for a in a_ab]
    pw = [_dot3(a, a) for a in a_ab]
    av = [_dot3(a_ak[h], v_ref[0, :, sls[h]]) for h in hs]
    y0 = [_dot3(a_rk[h], v_ref[0, :, sls[h]]) for h in hs]
    nk = [_dot3(v_ref[0, :, sl].T, kh_ref[0, :, sl]) for sl in sls]
    for _ in range(int(math.log2(length)) - 2):
        both = [_dot3(jnp.concatenate([pw[h], inv[h]], axis=0), pw[h]) for h in hs]
        pw = [b[:length] for b in both]
        inv = [inv[h] + both[h][length:] for h in hs]
    inv = [inv[h] + _dot3(inv[h], pw[h]) for h in hs]
    pq = [_dot3(inv[h], jnp.concatenate([at_ref[0, :, sls[h]], av[h]], axis=1)) for h in hs]
    ry = [_dot3(a_rb[h], pq[h]) for h in hs]
    mn = [_dot3(pq[h].T, bh_ref[0, :, sls[h]]) for h in hs]
    rp_ref[0] = jnp.concatenate([ra_ref[0, :, sls[h]] + ry[h][:, :HEAD] for h in hs], axis=1)
    y0_ref[0] = jnp.concatenate([y0[h] + ry[h][:, HEAD:] for h in hs], axis=1)
    m_ref[0, 0] = jnp.concatenate(
        [jnp.where(hrow == hcol, wl_ref[0, 0, :, sls[h]], 0.0) + mn[h][:HEAD] for h in hs], axis=1)
    n_ref[0, 0] = jnp.concatenate([nk[h] + mn[h][HEAD:] for h in hs], axis=1)


def _head_norm(y):
    mu = jnp.mean(y, axis=-1, keepdims=True)
    yc = y - mu
    return yc * lax.rsqrt(jnp.mean(yc * yc, axis=-1, keepdims=True) + LNX_EPS)


def _wkv_scan_kernel(rp_ref, y0_ref, m_ref, n_ref, g_ref, bonus_ref, s0_ref, lg_ref, lb_ref, o_ref, sout_ref, s_ref):
    c = pl.program_id(1)
    heads = rp_ref.shape[2] // HEAD

    @pl.when(c == 0)
    def _():
        s_ref[...] = s0_ref[0]

    sls = [slice(h * HEAD, (h + 1) * HEAD) for h in range(heads)]
    s = [s_ref[h] for h in range(heads)]
    s_new = [n_ref[0, 0, :, sl] + _dot3(s[h], m_ref[0, 0, :, sl]) for h, sl in enumerate(sls)]
    ys = [_head_norm(y0_ref[0, :, sl] + _dot3_t(rp_ref[0, :, sl], s[h])) for h, sl in enumerate(sls)]
    for h in range(heads):
        s_ref[h] = s_new[h]
    y = jnp.concatenate(ys, axis=-1)
    o_ref[0] = (y * lg_ref[...] + lb_ref[...] + bonus_ref[0]) * g_ref[0]

    @pl.when(c == pl.num_programs(1) - 1)
    def _():
        sout_ref[0] = s_ref[...]


def wkv(prep, s0, lnx_g, lnx_b, chunk):
    ra, at, bt, kt, bh, kh, v, g, bonus, wl = prep
    bsz, t, c_a = ra.shape
    heads = c_a // HEAD
    n_chunks = t // chunk
    assert chunk & (chunk - 1) == 0 and chunk >= 4 and t % chunk == 0
    tile = pl.BlockSpec((1, chunk, c_a), lambda b, c: (b, c, 0))
    mat = pl.BlockSpec((1, 1, HEAD, c_a), lambda b, c: (b, c, 0, 0))
    st = pl.BlockSpec((1, heads, HEAD, HEAD), lambda b, c: (b, 0, 0, 0))
    full = jax.ShapeDtypeStruct((bsz, t, c_a), F32)
    mats = jax.ShapeDtypeStruct((bsz, n_chunks, HEAD, c_a), F32)
    rp, y0, m, n = pl.pallas_call(
        _wkv_chunk_kernel, name="wkv_chunk",
        grid=(bsz, n_chunks),
        in_specs=[tile] * 7 + [pl.BlockSpec((1, 1, 1, c_a), lambda b, c: (b, c, 0, 0))],
        out_specs=[tile, tile, mat, mat],
        out_shape=[full, full, mats, mats],
        compiler_params=_params(("parallel", "parallel")),
    )(ra, at, bt, kt, bh, kh, v, wl)
    return pl.pallas_call(
        _wkv_scan_kernel, name="wkv_scan",
        grid=(bsz, n_chunks),
        in_specs=[tile, tile, mat, mat, tile, tile, st, _const_spec((1, c_a)), _const_spec((1, c_a))],
        out_specs=[tile, st],
        out_shape=[full, jax.ShapeDtypeStruct(s0.shape, F32)],
        scratch_shapes=[pltpu.VMEM((heads, HEAD, HEAD), F32)],
        compiler_params=_params(("parallel", "arbitrary")),
    )(rp, y0, m, n, g, bonus, s0, lnx_g.reshape(1, c_a), lnx_b.reshape(1, c_a))


def _wkv_step_kernel(r_ref, w_ref, k_ref, v_ref, a_ref, b_ref, g_ref, bonus_ref, s_ref, lg_ref, lb_ref, hsum_ref,
                     o_ref, sout_ref):
    bt, heads = s_ref.shape[0], s_ref.shape[1]
    eye = lax.broadcasted_iota(jnp.int32, (HEAD, HEAD), 0) == lax.broadcasted_iota(jnp.int32, (HEAD, HEAD), 1)
    pairs = [(i, h) for i in range(bt) for h in range(heads)]
    row = lambda ref, p: ref[p[0]:p[0] + 1, p[1] * HEAD:(p[1] + 1) * HEAD]
    sa = [jnp.sum(s_ref[p] * row(a_ref, p), axis=1, keepdims=True) for p in pairs]
    v_col = [jnp.sum(jnp.where(eye, row(v_ref, p), 0.0), axis=1, keepdims=True) for p in pairs]
    y_col = []
    for j, p in enumerate(pairs):
        s = s_ref[p] * row(w_ref, p) + sa[j] * row(b_ref, p) + v_col[j] * row(k_ref, p)
        sout_ref[p] = s
        y_col.append(jnp.sum(s * row(r_ref, p), axis=1, keepdims=True))
    y_row = [jnp.sum(jnp.where(eye, y, 0.0), axis=0, keepdims=True) for y in y_col]
    y = jnp.concatenate([jnp.concatenate(y_row[i * heads:(i + 1) * heads], axis=1) for i in range(bt)], axis=0)
    mu = _hdot(y, hsum_ref[...]) * (1.0 / HEAD)
    yc = y - mu
    y = yc * lax.rsqrt(_hdot(yc * yc, hsum_ref[...]) * (1.0 / HEAD) + LNX_EPS)
    o_ref[...] = (y * lg_ref[...] + lb_ref[...] + bonus_ref[...]) * g_ref[...]


def wkv_step(ops, s0, lnx_g, lnx_b, bt):
    bsz, c_a = ops[0].shape
    rows = pl.BlockSpec((bt, c_a), lambda i: (i, 0))
    st = pl.BlockSpec((bt,) + s0.shape[1:], lambda i: (i, 0, 0, 0))
    cidx = jnp.arange(c_a)
    hsum = ((cidx[:, None] // HEAD) == (cidx[None, :] // HEAD)).astype(F32)
    return pl.pallas_call(
        _wkv_step_kernel, name="wkv_step",
        grid=(bsz // bt,),
        in_specs=[rows] * 8 + [st, _const_spec((1, c_a)), _const_spec((1, c_a)), _const_spec(hsum.shape)],
        out_specs=[rows, st],
        out_shape=[jax.ShapeDtypeStruct((bsz, c_a), F32), jax.ShapeDtypeStruct(s0.shape, F32)],
        compiler_params=_params(("parallel",)),
    )(*ops, s0, lnx_g.reshape(1, c_a), lnx_b.reshape(1, c_a), hsum)


def rope_tables(pos):
    inv_freq = ROPE_THETA ** (-jnp.arange(ROT_HALF, dtype=F32) / ROT_HALF)
    ang = pos[:, None] * inv_freq[None, :]
    cos, sin = jnp.cos(ang), jnp.sin(ang)
    t = pos.shape[0]
    rest = HEAD - 2 * ROT_HALF
    c = jnp.concatenate([cos, cos, jnp.ones((t, rest), F32)], axis=1)
    s1 = jnp.concatenate([jnp.zeros((t, ROT_HALF), F32), sin, jnp.zeros((t, rest), F32)], axis=1)
    s2 = jnp.concatenate([-sin, jnp.zeros((t, HEAD - ROT_HALF), F32)], axis=1)
    rep = LANES // HEAD
    return tuple(jnp.tile(x, (1, rep)) for x in (c, s1, s2))


def _rope(x, c, s1, s2):
    width = x.shape[-1]
    rep = width // c.shape[-1]
    if rep > 1:
        c, s1, s2 = (jnp.concatenate([t] * rep, axis=1) for t in (c, s1, s2))
    return x * c + pltpu.roll(x, ROT_HALF, 1) * s1 + pltpu.roll(x, width - ROT_HALF, 1) * s2


def _attn_kernel(window, q_w, kv_w, zq_ref, kp_ref, vp_ref, rc_ref, rs1_ref, rs2_ref, pc_ref, ps1_ref, ps2_ref,
                 mk_ref, mv_ref, sink_ref, ob_ref, om_ref, kr_ref):
    n = pl.program_id(1)
    zq = zq_ref[0]
    scale = HEAD ** -0.5
    q = zq[:, :q_w]
    k = zq[:, q_w:q_w + kv_w]
    v = zq[:, q_w + kv_w:q_w + 2 * kv_w]
    qm = zq[:, q_w + 2 * kv_w:]
    qr = _rope(q, rc_ref[...], rs1_ref[...], rs2_ref[...]) * scale
    kr = _rope(k, rc_ref[...], rs1_ref[...], rs2_ref[...])
    kr_ref[0] = kr
    kpr = _rope(kp_ref[0], pc_ref[...], ps1_ref[...], ps2_ref[...])
    vp = vp_ref[0]
    gqa = q_w // kv_w
    assert window & (window - 1) == 0
    qi = lax.broadcasted_iota(jnp.int32, (gqa * window, 2 * window), 0) & (window - 1)
    kj = lax.broadcasted_iota(jnp.int32, (gqa * window, 2 * window), 1)
    valid = (kj > qi) & (kj <= qi + window) & ((kj >= window) | (n > 0))
    outs = [None] * (q_w // HEAD)
    for hk in range(kv_w // HEAD):
        sl = slice(hk * HEAD, (hk + 1) * HEAD)
        kcat = jnp.concatenate([kpr[:, sl], kr[:, sl]], axis=0).astype(BF16)
        vcat = jnp.concatenate([vp[:, sl], v[:, sl]], axis=0).astype(BF16)
        heads = [hk * gqa + g for g in range(gqa)]
        qs = jnp.concatenate([qr[:, h * HEAD:(h + 1) * HEAD] for h in heads], axis=0).astype(BF16)
        s = lax.dot_general(qs, kcat, (((1,), (1,)), ((), ())), preferred_element_type=F32)
        s = jnp.where(valid, s, NEG_INF)
        sink = jnp.concatenate([jnp.full((window, 1), sink_ref[h], F32) for h in heads], axis=0)
        m = jnp.maximum(jnp.max(s, axis=-1, keepdims=True), sink)
        p = jnp.exp(s - m)
        denom = jnp.sum(p, axis=-1, keepdims=True) + jnp.exp(sink - m)
        o = jnp.dot(p.astype(BF16), vcat, preferred_element_type=F32) / denom
        for g, h in enumerate(heads):
            outs[h] = o[g * window:(g + 1) * window]
    ob_ref[0] = jnp.concatenate(outs, axis=1)
    qmb = (qm * scale).astype(BF16)
    mk = mk_ref[0].astype(BF16)
    mv = mv_ref[0].astype(BF16)
    mouts = []
    for h in range(qm.shape[1] // HEAD):
        sl = slice(h * HEAD, (h + 1) * HEAD)
        s = lax.dot_general(qmb[:, sl], mk[:, sl], (((1,), (1,)), ((), ())), preferred_element_type=F32)
        p = jnp.exp(s - jnp.max(s, axis=-1, keepdims=True))
        mouts.append(jnp.dot(p.astype(BF16), mv[:, sl], preferred_element_type=F32)
                     / jnp.sum(p, axis=-1, keepdims=True))
    om_ref[0] = jnp.concatenate(mouts, axis=1)


def attn_prompt(zq, mk, mv, sinks, tables, window, q_w, kv_w):
    bsz, t, zw = zq.shape
    mem_w = zw - q_w - 2 * kv_w
    assert kv_w == LANES and q_w % kv_w == 0
    kcol, vcol = q_w // kv_w, q_w // kv_w + 1
    prev = lambda n: jnp.maximum(n - 1, 0)
    tab = pl.BlockSpec((window, LANES), lambda b, n: (n, 0))
    ptab = pl.BlockSpec((window, LANES), lambda b, n: (prev(n), 0))
    mem = pl.BlockSpec((1,) + mk.shape[1:], lambda b, n: (b, 0, 0))
    return pl.pallas_call(
        functools.partial(_attn_kernel, window, q_w, kv_w), name="attn",
        grid=(bsz, t // window),
        in_specs=[pl.BlockSpec((1, window, zw), lambda b, n: (b, n, 0)),
                  pl.BlockSpec((1, window, kv_w), lambda b, n: (b, prev(n), kcol)),
                  pl.BlockSpec((1, window, kv_w), lambda b, n: (b, prev(n), vcol)),
                  tab, tab, tab, ptab, ptab, ptab, mem, mem,
                  pl.BlockSpec(memory_space=pltpu.SMEM)],
        out_specs=[pl.BlockSpec((1, window, q_w), lambda b, n: (b, n, 0)),
                   pl.BlockSpec((1, window, mem_w), lambda b, n: (b, n, 0)),
                   pl.BlockSpec((1, window, kv_w), lambda b, n: (b, n, 0))],
        out_shape=[jax.ShapeDtypeStruct((bsz, t, q_w), F32), jax.ShapeDtypeStruct((bsz, t, mem_w), F32),
                   jax.ShapeDtypeStruct((bsz, t, kv_w), F32)],
        compiler_params=_params(("parallel", "parallel")),
    )(zq, zq, zq, *tables, *tables, mk, mv, sinks)


def _seg_attend(qrow, keys, vals, seg_sum, seg_exp, valid, extra):
    s = _hdot(keys * qrow, seg_sum)
    if valid is not None:
        s = jnp.where(valid, s, NEG_INF)
    m = jnp.max(s, axis=0, keepdims=True)
    if extra is not None:
        k_new, v_new, sink = extra
        s_new = _hdot(k_new * qrow, seg_sum)
        m = jnp.maximum(m, jnp.maximum(s_new, sink))
    p = jnp.exp(s - m)
    denom = jnp.sum(p, axis=0, keepdims=True)
    acc = jnp.sum(_hdot(p, seg_exp) * vals, axis=0, keepdims=True)
    if extra is not None:
        p_new = jnp.exp(s_new - m)
        denom = denom + p_new + jnp.exp(sink - m)
        acc = acc + _hdot(p_new, seg_exp) * v_new
    return acc / _hdot(denom, seg_exp)


def _attn_step_kernel(q_w, kv_w, zq_ref, ck_ref, cv_ref, mk_ref, mv_ref, rc_ref, rs1_ref, rs2_ref, sink_ref,
                      ssum_ref, sexp_ref, msum_ref, mexp_ref, ob_ref, om_ref, nk_ref, nv_ref):
    bt = zq_ref.shape[0]
    window = ck_ref.shape[1]
    scale = HEAD ** -0.5
    gqa = q_w // kv_w
    n_kv = kv_w // HEAD
    lane = lax.broadcasted_iota(jnp.int32, (1, LANES), 1)
    wrow = lax.broadcasted_iota(jnp.int32, (window, kv_w), 0)
    valid = lax.broadcasted_iota(jnp.int32, (window, LANES), 0) >= 1
    for b in range(bt):
        zq = zq_ref[b:b + 1, :]
        q = _rope(zq[:, :q_w], rc_ref[...], rs1_ref[...], rs2_ref[...]) * scale
        k_new = _rope(zq[:, q_w:q_w + kv_w], rc_ref[...], rs1_ref[...], rs2_ref[...])
        v_new = zq[:, q_w + kv_w:q_w + 2 * kv_w]
        qm = zq[:, q_w + 2 * kv_w:] * scale
        ck, cv = ck_ref[b], cv_ref[b]
        nk_ref[b] = jnp.where(wrow == window - 1, k_new, pltpu.roll(ck, window - 1, 0))
        nv_ref[b] = jnp.where(wrow == window - 1, v_new, pltpu.roll(cv, window - 1, 0))
        og = []
        for g in range(gqa):
            qg = jnp.concatenate([q[:, (hk * gqa + g) * HEAD:(hk * gqa + g + 1) * HEAD] for hk in range(n_kv)], axis=1)
            sink = jnp.zeros((1, LANES), F32)
            for hk in range(n_kv):
                sink = jnp.where(lane == hk, sink_ref[hk * gqa + g], sink)
            og.append(_seg_attend(qg, ck, cv, ssum_ref[...], sexp_ref[...], valid, (k_new, v_new, sink)))
        ob_ref[b:b + 1, :] = jnp.concatenate(
            [og[g][:, hk * HEAD:(hk + 1) * HEAD] for hk in range(n_kv) for g in range(gqa)], axis=1)
        om_ref[b:b + 1, :] = _seg_attend(qm, mk_ref[b], mv_ref[b], msum_ref[...], mexp_ref[...], None, None)


def _seg_mats(width):
    lane = jnp.arange(width)
    seg_sum = (lane[:, None] // HEAD == jnp.arange(LANES)[None, :]).astype(F32)
    return seg_sum, seg_sum.T


def attn_step(zq, ck, cv, mk, mv, sinks, tables, q_w, kv_w, bt):
    bsz, zw = zq.shape
    mem_w = zw - q_w - 2 * kv_w
    window, n_mem = ck.shape[1], mk.shape[1]
    ssum, sexp = _seg_mats(kv_w)
    msum, mexp = _seg_mats(mem_w)
    rows = lambda w: pl.BlockSpec((bt, w), lambda i: (i, 0))
    cache = pl.BlockSpec((bt, window, kv_w), lambda i: (i, 0, 0))
    mem = pl.BlockSpec((bt, n_mem, mem_w), lambda i: (i, 0, 0))
    consts = list(tables) + [sinks, ssum, sexp, msum, mexp]
    const_specs = [_const_spec(t.shape) for t in tables] + [pl.BlockSpec(memory_space=pltpu.SMEM)] + [
        _const_spec(m.shape) for m in (ssum, sexp, msum, mexp)]
    return pl.pallas_call(
        functools.partial(_attn_step_kernel, q_w, kv_w), name="attn_step",
        grid=(bsz // bt,),
        in_specs=[rows(zw), cache, cache, mem, mem] + const_specs,
        out_specs=[rows(q_w), rows(mem_w), cache, cache],
        out_shape=[jax.ShapeDtypeStruct((bsz, q_w), F32), jax.ShapeDtypeStruct((bsz, mem_w), F32),
                   jax.ShapeDtypeStruct(ck.shape, F32), jax.ShapeDtypeStruct(cv.shape, F32)],
        compiler_params=_params(("parallel",)),
    )(zq, ck, cv, mk, mv, *consts)


def _proj_kernel(x_ref, w_ref, o_ref):
    o_ref[...] = jnp.dot(x_ref[...].astype(BF16), w_ref[...], preferred_element_type=F32)


def proj(x, w, tm):
    n, d = x.shape
    return pl.pallas_call(
        _proj_kernel, name="proj",
        grid=(n // tm,),
        in_specs=[pl.BlockSpec((tm, d), lambda i: (i, 0)), _const_spec(w.shape)],
        out_specs=pl.BlockSpec((tm, w.shape[1]), lambda i: (i, 0)),
        out_shape=jax.ShapeDtypeStruct((n, w.shape[1]), F32),
        compiler_params=_params(("parallel",)),
    )(x, w)


def _merge_kernel(alpha, x_ref, oa_ref, ob_ref, om_ref, gt_ref, lig_ref, lib_ref, pa_ref, pb_ref, pm_ref, wo_ref,
                  l1g_ref, l1b_ref, wr_ref, br_ref, x1_ref, eidx_ref, gate_ref):
    d = x_ref.shape[1]
    xn = _ln(x_ref[...], lig_ref[...], lib_ref[...])
    gts = jax.nn.sigmoid(gt_ref[...])
    merged = (gts[:, :d] * _bdot(oa_ref[...], pa_ref[...]) + gts[:, d:2 * d] * _bdot(ob_ref[...], pb_ref[...])
              + gts[:, 2 * d:] * _bdot(om_ref[...], pm_ref[...]))
    x1 = _ln(alpha * xn + _bdot(merged, wo_ref[...]), l1g_ref[...], l1b_ref[...])
    x1_ref[...] = x1
    logits = _hdot(x1, wr_ref[...]) + br_ref[...]
    lane = lax.broadcasted_iota(jnp.int32, logits.shape, 1)
    lane_f = lane.astype(F32)
    first = lambda hit: jnp.min(jnp.where(hit, lane_f, float(LANES)), axis=-1, keepdims=True).astype(jnp.int32)
    gmask = lane < N_GROUPS
    gl = jnp.where(gmask, logits, NEG_INF)
    gmax = jnp.max(gl, axis=-1, keepdims=True)
    gidx = first(gl == gmax)
    g_w = 1.0 / jnp.sum(jnp.where(gmask, jnp.exp(gl - gmax), 0.0), axis=-1, keepdims=True)
    lo = N_GROUPS + gidx * EXPERTS_PER_GROUP
    el = jnp.where((lane >= lo) & (lane < lo + EXPERTS_PER_GROUP), logits, NEG_INF)
    v1 = jnp.max(el, axis=-1, keepdims=True)
    i1 = first(el == v1)
    el2 = jnp.where(lane == i1, NEG_INF, el)
    v2 = jnp.max(el2, axis=-1, keepdims=True)
    i2 = first(el2 == v2)
    e2 = jnp.exp(v2 - v1)
    gate1 = g_w / (1.0 + e2)
    eidx_ref[...] = jnp.where(lane == 0, i1 - N_GROUPS, jnp.where(lane == 1, i2 - N_GROUPS, 0))
    gate_ref[...] = jnp.where(lane == 0, gate1, jnp.where(lane == 1, gate1 * e2, 0.0))


def merge(x, oa, ob, om, gt, w, tm, alpha):
    n, d = x.shape
    rows = lambda a: pl.BlockSpec((tm, a.shape[1]), lambda i: (i, 0))
    consts = [w['ln_in_g'], w['ln_in_b'], w['p_a'], w['p_b'], w['p_m'], w['w_o'], w['ln1_g'], w['ln1_b'],
              w['w_route'], w['b_route']]
    return pl.pallas_call(
        functools.partial(_merge_kernel, alpha), name="merge",
        grid=(n // tm,),
        in_specs=[rows(a) for a in (x, oa, ob, om, gt)] + [_const_spec(c.shape) for c in consts],
        out_specs=[pl.BlockSpec((tm, d), lambda i: (i, 0)), pl.BlockSpec((tm, LANES), lambda i: (i, 0)),
                   pl.BlockSpec((tm, LANES), lambda i: (i, 0))],
        out_shape=[jax.ShapeDtypeStruct((n, d), F32), jax.ShapeDtypeStruct((n, LANES), jnp.int32),
                   jax.ShapeDtypeStruct((n, LANES), F32)],
        compiler_params=_params(("parallel",)),
    )(x, oa, ob, om, gt, *consts)


def _gather_rows(idx_ref, base, count, src_hbm, dst, sem, start):
    def body(r, carry):
        copy = pltpu.make_async_copy(src_hbm.at[pl.ds(idx_ref[base + r], 1)], dst.at[pl.ds(r, 1)], sem)
        if start:
            copy.start()
        else:
            copy.wait()
        return carry
    lax.fori_loop(0, count, body, 0, unroll=8)


def _moe_expert_kernel(tok_ref, be_ref, nu_ref, x_hbm, wg_ref, wu_ref, wd_ref, y_ref, xbuf, sem):
    i = pl.program_id(0)
    used = nu_ref[0]
    rows = xbuf.shape[1]

    @pl.when((i == 0) & (used > 0))
    def _():
        _gather_rows(tok_ref, 0, rows, x_hbm, xbuf.at[0], sem.at[0], True)

    @pl.when(i + 1 < used)
    def _():
        nxt = (i + 1) % 2
        _gather_rows(tok_ref, (i + 1) * rows, rows, x_hbm, xbuf.at[nxt], sem.at[nxt], True)

    @pl.when(i < used)
    def _():
        slot = i % 2
        _gather_rows(tok_ref, i * rows, rows, x_hbm, xbuf.at[slot], sem.at[slot], False)
        xb = xbuf[slot].astype(BF16)
        hg = jnp.dot(xb, wg_ref[0].astype(BF16), preferred_element_type=F32)
        hu = jnp.dot(xb, wu_ref[0].astype(BF16), preferred_element_type=F32)
        h = hg * jax.nn.sigmoid(hg) * hu
        y_ref[...] = jnp.dot(h.astype(BF16), wd_ref[0].astype(BF16), preferred_element_type=F32)

    @pl.when(i >= used)
    def _():
        y_ref[...] = jnp.zeros_like(y_ref)


def moe_experts(x1, row_tok, blk_e, n_used, e_gate, e_up, e_down):
    n, d = x1.shape
    n_blocks = blk_e.shape[0]
    ff = e_gate.shape[-1]
    return pl.pallas_call(
        _moe_expert_kernel, name="moe_expert",
        grid_spec=pltpu.PrefetchScalarGridSpec(
            num_scalar_prefetch=3,
            grid=(n_blocks,),
            in_specs=[pl.BlockSpec(memory_space=pl.ANY),
                      pl.BlockSpec((1, d, ff), lambda i, tok, be, nu: (be[i], 0, 0)),
                      pl.BlockSpec((1, d, ff), lambda i, tok, be, nu: (be[i], 0, 0)),
                      pl.BlockSpec((1, ff, d), lambda i, tok, be, nu: (be[i], 0, 0))],
            out_specs=pl.BlockSpec((EXPERT_BLOCK, d), lambda i, tok, be, nu: (i, 0)),
            scratch_shapes=[pltpu.VMEM((2, EXPERT_BLOCK, d), F32), pltpu.SemaphoreType.DMA((2,))]),
        out_shape=jax.ShapeDtypeStruct((n_blocks * EXPERT_BLOCK, d), F32),
        compiler_params=_params(("arbitrary",)),
    )(row_tok, blk_e, n_used, x1, e_gate, e_up, e_down)


def _moe_combine_kernel(alpha, dest_ref, ys_hbm, x1_ref, gate_ref, g_ref, b_ref, o_ref, buf, sem):
    i = pl.program_id(0)
    steps = pl.num_programs(0)
    rows = buf.shape[1]

    @pl.when(i == 0)
    def _():
        _gather_rows(dest_ref, 0, rows, ys_hbm, buf.at[0], sem.at[0], True)

    @pl.when(i + 1 < steps)
    def _():
        nxt = (i + 1) % 2
        _gather_rows(dest_ref, (i + 1) * rows, rows, ys_hbm, buf.at[nxt], sem.at[nxt], True)

    slot = i % 2
    _gather_rows(dest_ref, i * rows, rows, ys_hbm, buf.at[slot], sem.at[slot], False)
    tm = x1_ref.shape[0]
    gate = gate_ref[...]
    moe = gate[:, 0:1] * buf[slot, :tm, :] + gate[:, 1:2] * buf[slot, tm:, :]
    o_ref[...] = _ln(alpha * x1_ref[...] + moe, g_ref[...], b_ref[...])


def moe_combine(ys, dest, x1, gate, g, b, tm, alpha):
    n, d = x1.shape
    top_k = dest.shape[0] // n
    assert top_k == 2
    return pl.pallas_call(
        functools.partial(_moe_combine_kernel, alpha), name="moe_combine",
        grid_spec=pltpu.PrefetchScalarGridSpec(
            num_scalar_prefetch=1,
            grid=(n // tm,),
            in_specs=[pl.BlockSpec(memory_space=pl.ANY),
                      pl.BlockSpec((tm, d), lambda i, dest: (i, 0)),
                      pl.BlockSpec((tm, LANES), lambda i, dest: (i, 0)),
                      pl.BlockSpec((1, d), lambda i, dest: (0, 0)),
                      pl.BlockSpec((1, d), lambda i, dest: (0, 0))],
            out_specs=pl.BlockSpec((tm, d), lambda i, dest: (i, 0)),
            scratch_shapes=[pltpu.VMEM((2, top_k * tm, d), F32), pltpu.SemaphoreType.DMA((2,))]),
        out_shape=jax.ShapeDtypeStruct((n, d), F32),
        compiler_params=_params(("arbitrary",)),
    )(dest, ys, x1, gate, g.reshape(1, d), b.reshape(1, d))


def moe_routing(experts):
    n, top_k = experts.shape
    n_exp = N_GROUPS * EXPERTS_PER_GROUP
    a = n * top_k
    flat_e = experts.reshape(a)
    onehot = (flat_e[:, None] == jnp.arange(n_exp, dtype=jnp.int32)[None, :]).astype(jnp.int32)
    rank = jnp.take_along_axis(jnp.cumsum(onehot, axis=0) - onehot, flat_e[:, None], axis=1)[:, 0]
    counts = jnp.sum(onehot, axis=0)
    padded = (counts + EXPERT_BLOCK - 1) // EXPERT_BLOCK * EXPERT_BLOCK
    pad_end = jnp.cumsum(padded)
    dest = (pad_end - padded)[flat_e] + rank
    n_blocks = -(-a // EXPERT_BLOCK) + n_exp
    row_tok = jnp.zeros((n_blocks * EXPERT_BLOCK,), jnp.int32).at[dest].set(jnp.arange(a, dtype=jnp.int32) // top_k)
    blk_e = jnp.minimum(jnp.searchsorted(pad_end, jnp.arange(n_blocks, dtype=jnp.int32) * EXPERT_BLOCK, side='right'),
                        n_exp - 1).astype(jnp.int32)
    n_used = (pad_end[-1:] // EXPERT_BLOCK).astype(jnp.int32)
    return row_tok, dest.astype(jnp.int32), blk_e, n_used


def hier_moe_ln(x1, eidx, gate, w, tm, alpha):
    row_tok, dest, blk_e, n_used = moe_routing(eidx[:, :2])
    ys = moe_experts(x1, row_tok, blk_e, n_used, w['e_gate'], w['e_up'], w['e_down'])
    dest = dest.reshape(-1, tm, 2).transpose(0, 2, 1).reshape(-1)
    return moe_combine(ys, dest, x1, gate, w['ln2_g'], w['ln2_b'], tm, alpha)


def kernel(x_prompt, x_sample, mem_prompt, state_wkv, state_shift, cache_win_k, cache_win_v, cache_mem_k, cache_mem_v, ln_in_g, ln_in_b, w_in, mu, w0, w_up, a0, a_up, g_up, k_k, k_a, r_k, lnx_g, lnx_b, sinks, w_mem_kv, p_a, p_b, p_m, w_o, ln1_g, ln1_b, w_group, b_group, w_router, b_router, e_gate, e_up, e_down, ln2_g, ln2_b):
    depth = w_in.shape[0]
    assert depth == 1, "single-layer step"
    bsz, seq, d = x_prompt.shape
    dec = x_sample.shape[0]
    assert x_sample.shape[1] == 1
    c_shift = mu.shape[-1]
    c_a = w0.shape[-1]
    window, kv_w = cache_win_k.shape[2], cache_win_k.shape[3] * cache_win_k.shape[4]
    n_mem, mem_w = cache_mem_k.shape[2], cache_mem_k.shape[3] * cache_mem_k.shape[4]
    q_w = sinks.shape[-1] * HEAD
    qkvm_w = q_w + 2 * kv_w + mem_w
    alpha = (2.0 * depth) ** 0.25
    past_len = float(PAST_LEN)
    chunk = 64

    w_in_b = w_in[0].astype(BF16)
    w_parts = [w_in_b[:, :c_shift], w_in_b[:, c_shift:c_shift + qkvm_w], w_in_b[:, c_shift + qkvm_w:]]
    rp = dict(mu=mu[0], w0=w0[0], w_up=w_up[0], a0=a0[0], a_up=a_up[0], g_up=g_up[0], k_k=k_k[0], k_a=k_a[0],
              r_k=r_k[0].reshape(-1))
    n_route = N_GROUPS * (1 + EXPERTS_PER_GROUP)
    mw = dict(ln_in_g=ln_in_g.reshape(1, d), ln_in_b=ln_in_b.reshape(1, d), p_a=p_a[0].astype(BF16),
              p_b=p_b[0].astype(BF16), p_m=p_m[0].astype(BF16), w_o=w_o[0].astype(BF16),
              ln1_g=ln1_g[0].reshape(1, d), ln1_b=ln1_b[0].reshape(1, d),
              w_route=jnp.pad(jnp.concatenate([w_group[0], w_router[0]], axis=1), ((0, 0), (0, LANES - n_route))),
              b_route=jnp.pad(jnp.concatenate([b_group[0], b_router[0]]), (0, LANES - n_route)).reshape(1, LANES),
              e_gate=e_gate[0], e_up=e_up[0], e_down=e_down[0], ln2_g=ln2_g[0], ln2_b=ln2_b[0])

    xp = x_prompt.reshape(bsz * seq, d)
    zr, zq, zg = ln_proj(xp, ln_in_g, ln_in_b, w_parts, 256)
    zr3 = zr.reshape(bsz, seq, c_shift)
    prep = rwkv_prep(zr3, jnp.zeros((bsz, 1, c_shift), F32), rp, chunk, 256)
    o_a, wkv_p = wkv(prep, jnp.zeros((bsz, c_a // HEAD, HEAD, HEAD), F32), lnx_g[0], lnx_b[0], chunk)
    mkv = proj(mem_prompt.reshape(bsz * n_mem, d), w_mem_kv[0].astype(BF16), 256).reshape(bsz, n_mem, 2 * mem_w)
    mk_p, mv_p = mkv[..., :mem_w], mkv[..., mem_w:]
    tables = rope_tables(jnp.arange(seq, dtype=F32))
    o_b, o_m, k_rot = attn_prompt(zq.reshape(bsz, seq, qkvm_w), mk_p, mv_p, sinks[0], tables, window, q_w, kv_w)
    x1, eidx, gate = merge(xp, o_a.reshape(-1, c_a), o_b.reshape(-1, q_w), o_m.reshape(-1, mem_w), zg, mw, 256, alpha)
    y_prompt = hier_moe_ln(x1, eidx, gate, mw, 128, alpha).reshape(bsz, seq, d)
    shift_p = zr3[:, -1]
    kb_p = k_rot[:, -window:].reshape(bsz, window, H_KV, HEAD)
    vb_p = zq.reshape(bsz, seq, qkvm_w)[:, -window:, q_w + kv_w:q_w + 2 * kv_w].reshape(bsz, window, H_KV, HEAD)

    xs = x_sample.reshape(dec, d)
    zr_s, zq_s, zg_s = ln_proj(xs, ln_in_g, ln_in_b, w_parts, dec)
    ops_s = rwkv_prep(zr_s.reshape(1, dec, c_shift), state_shift[0].reshape(1, dec, c_shift), rp, 1, dec)
    o_a_s, wkv_s = wkv_step([a.reshape(dec, c_a) for a in ops_s], state_wkv[0], lnx_g[0], lnx_b[0], 8)
    tables_s = rope_tables(jnp.full((1,), past_len, F32))
    o_b_s, o_m_s, nk_s, nv_s = attn_step(
        zq_s, cache_win_k[0].reshape(dec, window, kv_w), cache_win_v[0].reshape(dec, window, kv_w),
        cache_mem_k[0].reshape(dec, n_mem, mem_w), cache_mem_v[0].reshape(dec, n_mem, mem_w),
        sinks[0], tables_s, q_w, kv_w, 8)
    x1_s, eidx_s, gate_s = merge(xs, o_a_s, o_b_s, o_m_s, zg_s, mw, dec, alpha)
    y_sample = hier_moe_ln(x1_s, eidx_s, gate_s, mw, dec, alpha).reshape(dec, 1, d)

    sd = state_wkv.dtype
    return (y_prompt, y_sample, wkv_p[None].astype(sd), wkv_s[None].astype(sd), shift_p[None], zr_s[None],
            kb_p[None], vb_p[None], nk_s.reshape(dec, window, H_KV, HEAD)[None],
            nv_s.reshape(dec, window, H_KV, HEAD)[None],
            mk_p.reshape(bsz, n_mem, -1, HEAD)[None], mv_p.reshape(bsz, n_mem, -1, HEAD)[None])
```

```python
import functools
import math

import jax
import jax.numpy as jnp
from jax import lax
from jax.experimental import pallas as pl
from jax.experimental.pallas import tpu as pltpu

F32 = jnp.float32
BF16 = jnp.bfloat16
SCAN_BATCH = 4

HEAD = 64
LANES = 128
H_KV = 2
ROT_HALF = 8
ROPE_THETA = 500000.0
PAST_LEN = 8192
N_GROUPS = 4
EXPERTS_PER_GROUP = 8
EXPERT_BLOCK = 128
LN_EPS = 1e-5
LNX_EPS = 64e-5
NEG_INF = -1e30
VMEM_LIMIT = 48 * 1024 * 1024


def _pieces(x, n):
    out = []
    for _ in range(n):
        p = x.astype(BF16)
        out.append(p)
        x = x - p.astype(F32)
    return out


def _mask_dot(x, mask, n=2):
    return sum(jnp.dot(p, mask, preferred_element_type=F32) for p in _pieces(x, n))


def _split3(x, axis, lhs):
    hi = x.astype(BF16).astype(F32)
    lo = x - hi
    return jnp.concatenate([hi, hi, lo] if lhs else [hi, lo, hi], axis=axis).astype(BF16)


def _dot3(a, b):
    return jnp.dot(_split3(a, 1, True), _split3(b, 0, False), preferred_element_type=F32)


def _dot3_t(a, b):
    return lax.dot_general(_split3(a, 1, True), _split3(b, 1, False), (((1,), (1,)), ((), ())),
                           preferred_element_type=F32)


def _bdot(a, b):
    return jnp.dot(a.astype(BF16), b.astype(BF16), preferred_element_type=F32)


def _ln(x, g, b):
    mu = jnp.mean(x, axis=-1, keepdims=True)
    xc = x - mu
    var = jnp.mean(xc * xc, axis=-1, keepdims=True)
    return xc * lax.rsqrt(var + LN_EPS) * g + b


def _const_spec(shape):
    nd = len(shape)
    return pl.BlockSpec(shape, lambda *_: (0,) * nd)


def _params(sem):
    return pltpu.CompilerParams(dimension_semantics=sem, vmem_limit_bytes=VMEM_LIMIT)


def _ln_proj_kernel(x_ref, g_ref, b_ref, *refs):
    nw = len(refs) // 2
    xn = _ln(x_ref[...], g_ref[...], b_ref[...]).astype(BF16)
    for w_ref, o_ref in zip(refs[:nw], refs[nw:]):
        o_ref[...] = jnp.dot(xn, w_ref[...], preferred_element_type=F32)


def ln_proj(x, g, b, ws, tm):
    n, d = x.shape
    return pl.pallas_call(
        _ln_proj_kernel, name="ln_proj",
        grid=(n // tm,),
        in_specs=[pl.BlockSpec((tm, d), lambda i: (i, 0)), _const_spec((1, d)), _const_spec((1, d))]
        + [_const_spec(w.shape) for w in ws],
        out_specs=[pl.BlockSpec((tm, w.shape[1]), lambda i: (i, 0)) for w in ws],
        out_shape=[jax.ShapeDtypeStruct((n, w.shape[1]), F32) for w in ws],
        compiler_params=_params(("parallel",)),
    )(x, g.reshape(1, d), b.reshape(1, d), *ws)


def _rwkv_prep_kernel(chunk, z_ref, prev_ref, mu_ref, w0_ref, wup_ref, a0_ref, aup_ref, gup_ref,
                      kk_ref, ka_ref, rk_ref, hsum_ref, tril_ref, *refs):
    out_refs, carry_ref = refs[:-1], refs[-1]
    z = z_ref[0]
    tt = z.shape[0]
    c_a = w0_ref.shape[-1]
    r_w, r_a, r_g = wup_ref.shape[0], aup_ref.shape[0], gup_ref.shape[0]
    if chunk == 1:
        prev = prev_ref[0]
    else:
        @pl.when(pl.program_id(1) == 0)
        def _():
            carry_ref[...] = prev_ref[0]

        row = lax.broadcasted_iota(jnp.int32, z.shape, 0)
        prev = jnp.where(row == 0, carry_ref[...], pltpu.roll(z, 1, 0))
        carry_ref[...] = z[tt - 1:tt, :]
    zs = z + (prev - z) * mu_ref[...]
    r = zs[:, :c_a]
    k = zs[:, c_a:2 * c_a]
    v = zs[:, 2 * c_a:3 * c_a]
    o = 3 * c_a
    xw = zs[:, o:o + r_w]
    xa = zs[:, o + r_w:o + r_w + r_a]
    xg = zs[:, o + r_w + r_a:o + r_w + r_a + r_g]
    warg = -(w0_ref[...] + _dot3(jnp.tanh(xw), wup_ref[...]))
    softplus = jnp.maximum(warg, 0.0) + jnp.log1p(jnp.exp(-jnp.abs(warg)))
    lw = -jnp.exp(-softplus - 0.5)
    a = jax.nn.sigmoid(a0_ref[...] + _dot3(xa, aup_ref[...]))
    g = _dot3(jax.nn.sigmoid(xg), gup_ref[...])
    kkr = k * kk_ref[...]
    kk = kkr / jnp.maximum(jnp.sqrt(_mask_dot(kkr * kkr, hsum_ref[...])), 1e-12)
    k2 = k * (1.0 + (a - 1.0) * ka_ref[...])
    bonus = _mask_dot(r * k2 * rk_ref[...], hsum_ref[...]) * v
    kb = kk * a
    if chunk == 1:
        outs = (r, jnp.exp(lw), k2, v, -kk, kb, g, bonus)
        for ref, val in zip(out_refs, outs):
            ref[0] = val
        return
    pieces = _pieces(lw, 3)
    ones3 = jnp.ones((chunk, 3 * chunk), BF16)
    cw, cwl = [], []
    for c in range(tt // chunk):
        stack = jnp.concatenate([p[c * chunk:(c + 1) * chunk] for p in pieces], axis=0)
        cw.append(jnp.dot(tril_ref[...], stack, preferred_element_type=F32))
        cwl.append(jnp.dot(ones3, stack, preferred_element_type=F32))
    cw = jnp.concatenate(cw, axis=0)
    cwl = jnp.concatenate(cwl, axis=0)
    e_inv = jnp.exp(-cw)
    e_hat = jnp.exp(cwl - cw)
    outs = (r * jnp.exp(cw), -kk * jnp.exp(cw - lw), kb * e_inv, k2 * e_inv, kb * e_hat, k2 * e_hat, v, g, bonus)
    for ref, val in zip(out_refs[:-1], outs):
        ref[0] = val
    wl = jnp.exp(cwl)
    for c in range(tt // chunk):
        out_refs[-1][0, c] = wl[c * chunk:c * chunk + 1, :]


def _head_sum_matrix(width):
    idx = jnp.arange(width)
    return ((idx[:, None] // HEAD) == (idx[None, :] // HEAD)).astype(BF16)


def rwkv_prep(z, prev, p, chunk, tt):
    bsz, t, cs = z.shape
    c_a = p['w0'].shape[-1]
    assert prev.shape[1] == (t if chunk == 1 else 1)
    ridx = jnp.arange(chunk)
    tril = jnp.tile((ridx[None, :] <= ridx[:, None]).astype(BF16), (1, 3))
    hsum = _head_sum_matrix(c_a)
    row = lambda x: x.reshape(1, -1)
    tile = pl.BlockSpec((1, tt, c_a), lambda b, i: (b, i, 0))
    full = jax.ShapeDtypeStruct((bsz, t, c_a), F32)
    if chunk == 1:
        out_specs, out_shape = [tile] * 8, [full] * 8
        prev_spec = pl.BlockSpec((1, tt, cs), lambda b, i: (b, i, 0))
    else:
        out_specs = [tile] * 9 + [pl.BlockSpec((1, tt // chunk, 1, c_a), lambda b, i: (b, i, 0, 0))]
        out_shape = [full] * 9 + [jax.ShapeDtypeStruct((bsz, t // chunk, 1, c_a), F32)]
        prev_spec = pl.BlockSpec((1, 1, cs), lambda b, i: (b, 0, 0))
    consts = [row(p['mu']), row(p['w0']), p['w_up'], row(p['a0']), p['a_up'], p['g_up'], row(p['k_k']),
              row(p['k_a']), row(p['r_k']), hsum, tril]
    return pl.pallas_call(
        functools.partial(_rwkv_prep_kernel, chunk), name="rwkv_prep",
        grid=(bsz, t // tt),
        in_specs=[pl.BlockSpec((1, tt, cs), lambda b, i: (b, i, 0)), prev_spec] + [_const_spec(c.shape) for c in consts],
        out_specs=out_specs,
        out_shape=out_shape,
        scratch_shapes=[pltpu.VMEM((1, cs), F32)],
        compiler_params=_params(("parallel", "arbitrary")),
    )(z, prev, *consts)


def _wkv_chunk_kernel(ra_ref, at_ref, bt_ref, kt_ref, bh_ref, kh_ref, v_ref, wl_ref, rp_ref, y0_ref, m_ref, n_ref):
    length = ra_ref.shape[1]
    heads = ra_ref.shape[2] // HEAD
    row = lax.broadcasted_iota(jnp.int32, (length, length), 0)
    col = lax.broadcasted_iota(jnp.int32, (length, length), 1)
    strict = row > col
    incl = row >= col
    hrow = lax.broadcasted_iota(jnp.int32, (HEAD, HEAD), 0)
    hcol = lax.broadcasted_iota(jnp.int32, (HEAD, HEAD), 1)
    hs = range(heads)
    sls = [slice(h * HEAD, (h + 1) * HEAD) for h in hs]
    gram = [_dot3_t(jnp.concatenate([at_ref[0, :, sl], ra_ref[0, :, sl]], axis=0),
                    jnp.concatenate([bt_ref[0, :, sl], kt_ref[0, :, sl]], axis=0)) for sl in sls]
    a_ab = [jnp.where(strict, g[:length, :length], 0.0) for g in gram]
    a_ak = [jnp.where(strict, g[:length, length:], 0.0) for g in gram]
    a_rb = [jnp.where(incl, g[length:, :length], 0.0) for g in gram]
    a_rk = [jnp.where(incl, g[length:, length:], 0.0) for g in gram]
    inv = [jnp.where(row == col, 1.0, a) for a in a_ab]
    pw = [_dot3(a, a) for a in a_ab]
    av = [_dot3(a_ak[h], v_ref[0, :, sls[h]]) for h in hs]
    y0 = [_dot3(a_rk[h], v_ref[0, :, sls[h]]) for h in hs]
    nk = [_dot3(v_ref[0, :, sl].T, kh_ref[0, :, sl]) for sl in sls]
    for _ in range(int(math.log2(length)) - 2):
        both = [_dot3(jnp.concatenate([pw[h], inv[h]], axis=0), pw[h]) for h in hs]
        pw = [b[:length] for b in both]
        inv = [inv[h] + both[h][length:] for h in hs]
    inv = [inv[h] + _dot3(inv[h], pw[h]) for h in hs]
    pq = [_dot3(inv[h], jnp.concatenate([at_ref[0, :, sls[h]], av[h]], axis=1)) for h in hs]
    ry = [_dot3(a_rb[h], pq[h]) for h in hs]
    mn = [_dot3(pq[h].T, bh_ref[0, :, sls[h]]) for h in hs]
    rp_ref[0] = jnp.concatenate([ra_ref[0, :, sls[h]] + ry[h][:, :HEAD] for h in hs], axis=1)
    y0_ref[0] = jnp.concatenate([y0[h] + ry[h][:, HEAD:] for h in hs], axis=1)
    m_ref[0, 0] = jnp.concatenate(
        [jnp.where(hrow == hcol, wl_ref[0, 0, :, sls[h]], 0.0) + mn[h][:HEAD] for h in hs], axis=1)
    n_ref[0, 0] = jnp.concatenate([nk[h] + mn[h][HEAD:] for h in hs], axis=1)


def _head_norm_wide(y, hsum):
    yc = y - _mask_dot(y, hsum) * (1.0 / HEAD)
    return yc * lax.rsqrt(_mask_dot(yc * yc, hsum) * (1.0 / HEAD) + LNX_EPS)


def _wkv_scan_kernel(rp_ref, y0_ref, m_ref, n_ref, g_ref, bonus_ref, s0_ref, lg_ref, lb_ref, hsum_ref,
                     o_ref, sout_ref, s_ref):
    c = pl.program_id(1)
    nb = rp_ref.shape[0]
    heads = rp_ref.shape[2] // HEAD

    @pl.when(c == 0)
    def _():
        s_ref[...] = s0_ref[...]

    pairs = [(b, h, slice(h * HEAD, (h + 1) * HEAD)) for b in range(nb) for h in range(heads)]
    s = [s_ref[b, h] for b, h, _ in pairs]
    s_new = [n_ref[b, 0, :, sl] + _dot3(s[j], m_ref[b, 0, :, sl]) for j, (b, h, sl) in enumerate(pairs)]
    ys = [y0_ref[b, :, sl] + _dot3_t(rp_ref[b, :, sl], s[j]) for j, (b, h, sl) in enumerate(pairs)]
    for j, (b, h, _) in enumerate(pairs):
        s_ref[b, h] = s_new[j]
    for b in range(nb):
        y = _head_norm_wide(jnp.concatenate(ys[b * heads:(b + 1) * heads], axis=-1), hsum_ref[...])
        o_ref[b] = (y * lg_ref[...] + lb_ref[...] + bonus_ref[b]) * g_ref[b]

    @pl.when(c == pl.num_programs(1) - 1)
    def _():
        sout_ref[...] = s_ref[...]


def wkv(prep, s0, lnx_g, lnx_b, chunk):
    ra, at, bt, kt, bh, kh, v, g, bonus, wl = prep
    bsz, t, c_a = ra.shape
    heads = c_a // HEAD
    n_chunks = t // chunk
    assert chunk & (chunk - 1) == 0 and chunk >= 4 and t % chunk == 0
    tile = pl.BlockSpec((1, chunk, c_a), lambda b, c: (b, c, 0))
    mat = pl.BlockSpec((1, 1, HEAD, c_a), lambda b, c: (b, c, 0, 0))
    st = pl.BlockSpec((1, heads, HEAD, HEAD), lambda b, c: (b, 0, 0, 0))
    full = jax.ShapeDtypeStruct((bsz, t, c_a), F32)
    mats = jax.ShapeDtypeStruct((bsz, n_chunks, HEAD, c_a), F32)
    rp, y0, m, n = pl.pallas_call(
        _wkv_chunk_kernel, name="wkv_chunk",
        grid=(bsz, n_chunks),
        in_specs=[tile] * 7 + [pl.BlockSpec((1, 1, 1, c_a), lambda b, c: (b, c, 0, 0))],
        out_specs=[tile, tile, mat, mat],
        out_shape=[full, full, mats, mats],
        compiler_params=_params(("parallel", "parallel")),
    )(ra, at, bt, kt, bh, kh, v, wl)
    nb = SCAN_BATCH if bsz % SCAN_BATCH == 0 else 1
    tile = pl.BlockSpec((nb, chunk, c_a), lambda b, c: (b, c, 0))
    mat = pl.BlockSpec((nb, 1, HEAD, c_a), lambda b, c: (b, c, 0, 0))
    st = pl.BlockSpec((nb, heads, HEAD, HEAD), lambda b, c: (b, 0, 0, 0))
    return pl.pallas_call(
        _wkv_scan_kernel, name="wkv_scan",
        grid=(bsz // nb, n_chunks),
        in_specs=[tile, tile, mat, mat, tile, tile, st, _const_spec((1, c_a)), _const_spec((1, c_a)),
                  _const_spec((c_a, c_a))],
        out_specs=[tile, st],
        out_shape=[full, jax.ShapeDtypeStruct(s0.shape, F32)],
        scratch_shapes=[pltpu.VMEM((nb, heads, HEAD, HEAD), F32)],
        compiler_params=_params(("parallel", "arbitrary")),
    )(rp, y0, m, n, g, bonus, s0, lnx_g.reshape(1, c_a), lnx_b.reshape(1, c_a), _head_sum_matrix(c_a))


def _wkv_step_kernel(r_ref, w_ref, k_ref, v_ref, a_ref, b_ref, g_ref, bonus_ref, s_ref, lg_ref, lb_ref, hsum_ref,
                     o_ref, sout_ref):
    bt, heads = s_ref.shape[0], s_ref.shape[1]
    c_a = heads * HEAD
    hsum = hsum_ref[...]
    diag = (lax.broadcasted_iota(jnp.int32, (HEAD, c_a), 1) % HEAD
            == lax.broadcasted_iota(jnp.int32, (HEAD, c_a), 0))
    seqs = range(bt)
    row = lambda ref, i: ref[i:i + 1, :]
    s = [jnp.concatenate([s_ref[i, h] for h in range(heads)], axis=1) for i in seqs]
    sa = [_mask_dot(s[i] * row(a_ref, i), hsum) for i in seqs]
    v_rows = [_mask_dot(jnp.where(diag, row(v_ref, i), 0.0), hsum) for i in seqs]
    s = [s[i] * row(w_ref, i) + sa[i] * row(b_ref, i) + v_rows[i] * row(k_ref, i) for i in seqs]
    for i in seqs:
        for h in range(heads):
            sout_ref[i, h] = s[i][:, h * HEAD:(h + 1) * HEAD]
    y_rows = [_mask_dot(s[i] * row(r_ref, i), hsum) for i in seqs]
    y = jnp.concatenate([jnp.sum(jnp.where(diag, y_rows[i], 0.0), axis=0, keepdims=True) for i in seqs], axis=0)
    y = _head_norm_wide(y, hsum)
    o_ref[...] = (y * lg_ref[...] + lb_ref[...] + bonus_ref[...]) * g_ref[...]


def wkv_step(ops, s0, lnx_g, lnx_b, bt):
    bsz, c_a = ops[0].shape
    rows = pl.BlockSpec((bt, c_a), lambda i: (i, 0))
    st = pl.BlockSpec((bt,) + s0.shape[1:], lambda i: (i, 0, 0, 0))
    hsum = _head_sum_matrix(c_a)
    return pl.pallas_call(
        _wkv_step_kernel, name="wkv_step",
        grid=(bsz // bt,),
        in_specs=[rows] * 8 + [st, _const_spec((1, c_a)), _const_spec((1, c_a)), _const_spec(hsum.shape)],
        out_specs=[rows, st],
        out_shape=[jax.ShapeDtypeStruct((bsz, c_a), F32), jax.ShapeDtypeStruct(s0.shape, F32)],
        compiler_params=_params(("parallel",)),
    )(*ops, s0, lnx_g.reshape(1, c_a), lnx_b.reshape(1, c_a), hsum)


def rope_tables(pos):
    inv_freq = ROPE_THETA ** (-jnp.arange(ROT_HALF, dtype=F32) / ROT_HALF)
    ang = pos[:, None] * inv_freq[None, :]
    cos, sin = jnp.cos(ang), jnp.sin(ang)
    t = pos.shape[0]
    rest = HEAD - 2 * ROT_HALF
    c = jnp.concatenate([cos, cos, jnp.ones((t, rest), F32)], axis=1)
    s1 = jnp.concatenate([jnp.zeros((t, ROT_HALF), F32), sin, jnp.zeros((t, rest), F32)], axis=1)
    s2 = jnp.concatenate([-sin, jnp.zeros((t, HEAD - ROT_HALF), F32)], axis=1)
    rep = LANES // HEAD
    return tuple(jnp.tile(x, (1, rep)) for x in (c, s1, s2))


def _rope(x, c, s1, s2):
    width = x.shape[-1]
    rep = width // c.shape[-1]
    if rep > 1:
        c, s1, s2 = (jnp.concatenate([t] * rep, axis=1) for t in (c, s1, s2))
    return x * c + pltpu.roll(x, ROT_HALF, 1) * s1 + pltpu.roll(x, width - ROT_HALF, 1) * s2


def _attn_kernel(window, q_w, kv_w, zq_ref, kp_ref, vp_ref, rc_ref, rs1_ref, rs2_ref, pc_ref, ps1_ref, ps2_ref,
                 mk_ref, mv_ref, sink_ref, ob_ref, om_ref, kr_ref):
    n = pl.program_id(1)
    zq = zq_ref[0]
    scale = HEAD ** -0.5
    q = zq[:, :q_w]
    k = zq[:, q_w:q_w + kv_w]
    v = zq[:, q_w + kv_w:q_w + 2 * kv_w]
    qm = zq[:, q_w + 2 * kv_w:]
    qr = _rope(q, rc_ref[...], rs1_ref[...], rs2_ref[...]) * scale
    kr = _rope(k, rc_ref[...], rs1_ref[...], rs2_ref[...])
    kr_ref[0] = kr
    kpr = _rope(kp_ref[0], pc_ref[...], ps1_ref[...], ps2_ref[...])
    vp = vp_ref[0]
    gqa = q_w // kv_w
    assert window & (window - 1) == 0
    qi = lax.broadcasted_iota(jnp.int32, (gqa * window, 2 * window), 0) & (window - 1)
    kj = lax.broadcasted_iota(jnp.int32, (gqa * window, 2 * window), 1)
    valid = (kj > qi) & (kj <= qi + window) & ((kj >= window) | (n > 0))
    tdot = lambda a, b: lax.dot_general(a, b, (((1,), (1,)), ((), ())), preferred_element_type=F32)
    dot = lambda a, b: jnp.dot(a, b, preferred_element_type=F32)
    kvs = range(kv_w // HEAD)
    mhs = range(qm.shape[1] // HEAD)
    hsl = lambda h: slice(h * HEAD, (h + 1) * HEAD)
    qmb = (qm * scale).astype(BF16)
    mk = mk_ref[0].astype(BF16)
    mv = mv_ref[0].astype(BF16)
    kcat = [jnp.concatenate([kpr[:, hsl(hk)], kr[:, hsl(hk)]], axis=0).astype(BF16) for hk in kvs]
    vcat = [jnp.concatenate([vp[:, hsl(hk)], v[:, hsl(hk)]], axis=0).astype(BF16) for hk in kvs]
    qs = [jnp.concatenate([qr[:, hsl(hk * gqa + g)] for g in range(gqa)], axis=0).astype(BF16) for hk in kvs]
    s = [jnp.where(valid, tdot(qs[hk], kcat[hk]), NEG_INF) for hk in kvs]
    sm = [tdot(qmb[:, hsl(h)], mk[:, hsl(h)]) for h in mhs]
    sink = [jnp.concatenate([jnp.full((window, 1), sink_ref[hk * gqa + g], F32) for g in range(gqa)], axis=0)
            for hk in kvs]
    m = [jnp.maximum(jnp.max(s[hk], axis=-1, keepdims=True), sink[hk]) for hk in kvs]
    p = [jnp.exp(s[hk] - m[hk]) for hk in kvs]
    pm = [jnp.exp(sm[h] - jnp.max(sm[h], axis=-1, keepdims=True)) for h in mhs]
    o = [dot(p[hk].astype(BF16), vcat[hk])
         / (jnp.sum(p[hk], axis=-1, keepdims=True) + jnp.exp(sink[hk] - m[hk])) for hk in kvs]
    om = [dot(pm[h].astype(BF16), mv[:, hsl(h)]) / jnp.sum(pm[h], axis=-1, keepdims=True) for h in mhs]
    ob_ref[0] = jnp.concatenate([o[hk][g * window:(g + 1) * window] for hk in kvs for g in range(gqa)], axis=1)
    om_ref[0] = jnp.concatenate(om, axis=1)


def attn_prompt(zq, mk, mv, sinks, tables, window, q_w, kv_w):
    bsz, t, zw = zq.shape
    mem_w = zw - q_w - 2 * kv_w
    assert kv_w == LANES and q_w % kv_w == 0
    kcol, vcol = q_w // kv_w, q_w // kv_w + 1
    prev = lambda n: jnp.maximum(n - 1, 0)
    tab = pl.BlockSpec((window, LANES), lambda b, n: (n, 0))
    ptab = pl.BlockSpec((window, LANES), lambda b, n: (prev(n), 0))
    mem = pl.BlockSpec((1,) + mk.shape[1:], lambda b, n: (b, 0, 0))
    return pl.pallas_call(
        functools.partial(_attn_kernel, window, q_w, kv_w), name="attn",
        grid=(bsz, t // window),
        in_specs=[pl.BlockSpec((1, window, zw), lambda b, n: (b, n, 0)),
                  pl.BlockSpec((1, window, kv_w), lambda b, n: (b, prev(n), kcol)),
                  pl.BlockSpec((1, window, kv_w), lambda b, n: (b, prev(n), vcol)),
                  tab, tab, tab, ptab, ptab, ptab, mem, mem,
                  pl.BlockSpec(memory_space=pltpu.SMEM)],
        out_specs=[pl.BlockSpec((1, window, q_w), lambda b, n: (b, n, 0)),
                   pl.BlockSpec((1, window, mem_w), lambda b, n: (b, n, 0)),
                   pl.BlockSpec((1, window, kv_w), lambda b, n: (b, n, 0))],
        out_shape=[jax.ShapeDtypeStruct((bsz, t, q_w), F32), jax.ShapeDtypeStruct((bsz, t, mem_w), F32),
                   jax.ShapeDtypeStruct((bsz, t, kv_w), F32)],
        compiler_params=_params(("parallel", "parallel")),
    )(zq, zq, zq, *tables, *tables, mk, mv, sinks)


def _attn_step_kernel(q_w, kv_w, zq_ref, ck_ref, cv_ref, mk_ref, mv_ref, rc_ref, rs1_ref, rs2_ref, sink_ref,
                      ob_ref, om_ref, nk_ref, nv_ref):
    bt = zq_ref.shape[0]
    window = ck_ref.shape[1]
    mem_w = om_ref.shape[1]
    n_q, gqa, per_vreg = q_w // HEAD, q_w // kv_w, LANES // HEAD
    scale = HEAD ** -0.5
    zq = zq_ref[...]
    q = _rope(zq[:, :q_w], rc_ref[...], rs1_ref[...], rs2_ref[...]) * scale
    k_new = _rope(zq[:, q_w:q_w + kv_w], rc_ref[...], rs1_ref[...], rs2_ref[...])
    v_new = zq[:, q_w + kv_w:q_w + 2 * kv_w]
    qm = zq[:, q_w + 2 * kv_w:] * scale
    own = lambda w: (lax.broadcasted_iota(jnp.int32, (n_q, w), 1) // HEAD
                     == lax.broadcasted_iota(jnp.int32, (n_q, w), 0))
    own_q, own_m = own(q_w), own(mem_w)
    hrow = lax.broadcasted_iota(jnp.int32, (n_q, LANES), 0)
    hblk = lax.broadcasted_iota(jnp.int32, (n_q, LANES), 1) // HEAD
    swap = (hrow % per_vreg) != (hrow // gqa)
    keep = hblk == hrow % per_vreg
    key_ok = lax.broadcasted_iota(jnp.int32, (n_q, window), 1) >= 1
    wrow = lax.broadcasted_iota(jnp.int32, (window, kv_w), 0)
    sink = sink_ref[...]
    bs = range(bt)
    tdot = lambda a, b: lax.dot_general(a.astype(BF16), b.astype(BF16), (((1,), (1,)), ((), ())),
                                        preferred_element_type=F32)
    for b in bs:
        nk_ref[b] = jnp.where(wrow == window - 1, k_new[b:b + 1], pltpu.roll(ck_ref[b], window - 1, 0))
        nv_ref[b] = jnp.where(wrow == window - 1, v_new[b:b + 1], pltpu.roll(cv_ref[b], window - 1, 0))
    q8 = []
    for b in bs:
        rep = jnp.where(own_q, q[b:b + 1], 0.0)
        fold = sum(rep[:, c * LANES:(c + 1) * LANES] for c in range(q_w // LANES))
        q8.append(jnp.where(swap, pltpu.roll(fold, HEAD, 1), fold))
    qm8 = [jnp.where(own_m, qm[b:b + 1], 0.0) for b in bs]
    s = [jnp.where(key_ok, tdot(q8[b], ck_ref[b]), NEG_INF) for b in bs]
    sm = [tdot(qm8[b], mk_ref[b]) for b in bs]
    s_new = [jnp.sum(q8[b] * k_new[b:b + 1], axis=1, keepdims=True) for b in bs]
    m = [jnp.maximum(jnp.maximum(jnp.max(s[b], axis=1, keepdims=True), s_new[b]), sink) for b in bs]
    p = [jnp.exp(s[b] - m[b]) for b in bs]
    p_new = [jnp.exp(s_new[b] - m[b]) for b in bs]
    pm = [jnp.exp(sm[b] - jnp.max(sm[b], axis=1, keepdims=True)) for b in bs]
    o8 = [(_bdot(p[b], cv_ref[b]) + p_new[b] * v_new[b:b + 1])
          / (jnp.sum(p[b], axis=1, keepdims=True) + p_new[b] + jnp.exp(sink - m[b])) for b in bs]
    om8 = [_bdot(pm[b], mv_ref[b]) / jnp.sum(pm[b], axis=1, keepdims=True) for b in bs]
    ob, om = [], []
    for b in bs:
        o = jnp.where(swap, pltpu.roll(o8[b], HEAD, 1), o8[b])
        o = jnp.concatenate([jnp.where(keep, o, 0.0)] * (q_w // LANES), axis=1)
        ob.append(jnp.sum(jnp.where(own_q, o, 0.0), axis=0, keepdims=True))
        om.append(jnp.sum(jnp.where(own_m, om8[b], 0.0), axis=0, keepdims=True))
    ob_ref[...] = jnp.concatenate(ob, axis=0)
    om_ref[...] = jnp.concatenate(om, axis=0)


def attn_step(zq, ck, cv, mk, mv, sinks, tables, q_w, kv_w, bt):
    bsz, zw = zq.shape
    mem_w = zw - q_w - 2 * kv_w
    window, n_mem = ck.shape[1], mk.shape[1]
    n_q = q_w // HEAD
    assert kv_w == LANES and mem_w // HEAD <= n_q and n_q == 8
    rows = lambda w: pl.BlockSpec((bt, w), lambda i: (i, 0))
    cache = pl.BlockSpec((bt, window, kv_w), lambda i: (i, 0, 0))
    mem = pl.BlockSpec((bt, n_mem, mem_w), lambda i: (i, 0, 0))
    consts = list(tables) + [sinks.reshape(n_q, 1)]
    const_specs = [_const_spec(c.shape) for c in consts]
    return pl.pallas_call(
        functools.partial(_attn_step_kernel, q_w, kv_w), name="attn_step",
        grid=(bsz // bt,),
        in_specs=[rows(zw), cache, cache, mem, mem] + const_specs,
        out_specs=[rows(q_w), rows(mem_w), cache, cache],
        out_shape=[jax.ShapeDtypeStruct((bsz, q_w), F32), jax.ShapeDtypeStruct((bsz, mem_w), F32),
                   jax.ShapeDtypeStruct(ck.shape, F32), jax.ShapeDtypeStruct(cv.shape, F32)],
        compiler_params=_params(("parallel",)),
    )(zq, ck, cv, mk, mv, *consts)


def _proj_kernel(x_ref, w_ref, o_ref):
    o_ref[...] = jnp.dot(x_ref[...].astype(BF16), w_ref[...], preferred_element_type=F32)


def proj(x, w, tm):
    n, d = x.shape
    return pl.pallas_call(
        _proj_kernel, name="proj",
        grid=(n // tm,),
        in_specs=[pl.BlockSpec((tm, d), lambda i: (i, 0)), _const_spec(w.shape)],
        out_specs=pl.BlockSpec((tm, w.shape[1]), lambda i: (i, 0)),
        out_shape=jax.ShapeDtypeStruct((n, w.shape[1]), F32),
        compiler_params=_params(("parallel",)),
    )(x, w)


def _merge_kernel(alpha, x_ref, oa_ref, ob_ref, om_ref, gt_ref, lig_ref, lib_ref, pa_ref, pb_ref, pm_ref, wo_ref,
                  l1g_ref, l1b_ref, wr_ref, br_ref, x1_ref, eidx_ref, gate_ref):
    d = x_ref.shape[1]
    xn = _ln(x_ref[...], lig_ref[...], lib_ref[...])
    gts = jax.nn.sigmoid(gt_ref[...])
    merged = (gts[:, :d] * _bdot(oa_ref[...], pa_ref[...]) + gts[:, d:2 * d] * _bdot(ob_ref[...], pb_ref[...])
              + gts[:, 2 * d:] * _bdot(om_ref[...], pm_ref[...]))
    x1 = _ln(alpha * xn + _bdot(merged, wo_ref[...]), l1g_ref[...], l1b_ref[...])
    x1_ref[...] = x1
    x_hi, x_lo = _pieces(x1, 2)
    w_hi, w_lo = _pieces(wr_ref[...], 2)
    dot = lambda a, b: jnp.dot(a, b, preferred_element_type=F32)
    logits = dot(x_hi, w_hi) + dot(x_hi, w_lo) + dot(x_lo, w_hi) + br_ref[...]
    lane = lax.broadcasted_iota(jnp.int32, logits.shape, 1)
    lane_f = lane.astype(F32)
    first = lambda hit: jnp.min(jnp.where(hit, lane_f, float(LANES)), axis=-1, keepdims=True).astype(jnp.int32)
    gmask = lane < N_GROUPS
    gl = jnp.where(gmask, logits, NEG_INF)
    gmax = jnp.max(gl, axis=-1, keepdims=True)
    gidx = first(gl == gmax)
    g_w = 1.0 / jnp.sum(jnp.where(gmask, jnp.exp(gl - gmax), 0.0), axis=-1, keepdims=True)
    lo = N_GROUPS + gidx * EXPERTS_PER_GROUP
    el = jnp.where((lane >= lo) & (lane < lo + EXPERTS_PER_GROUP), logits, NEG_INF)
    v1 = jnp.max(el, axis=-1, keepdims=True)
    i1 = first(el == v1)
    el2 = jnp.where(lane == i1, NEG_INF, el)
    v2 = jnp.max(el2, axis=-1, keepdims=True)
    i2 = first(el2 == v2)
    e2 = jnp.exp(v2 - v1)
    gate1 = g_w / (1.0 + e2)
    eidx_ref[...] = jnp.where(lane == 0, i1 - N_GROUPS, jnp.where(lane == 1, i2 - N_GROUPS, 0))
    gate_ref[...] = jnp.where(lane == 0, gate1, jnp.where(lane == 1, gate1 * e2, 0.0))


def merge(x, oa, ob, om, gt, w, tm, alpha):
    n, d = x.shape
    rows = lambda a: pl.BlockSpec((tm, a.shape[1]), lambda i: (i, 0))
    consts = [w['ln_in_g'], w['ln_in_b'], w['p_a'], w['p_b'], w['p_m'], w['w_o'], w['ln1_g'], w['ln1_b'],
              w['w_route'], w['b_route']]
    return pl.pallas_call(
        functools.partial(_merge_kernel, alpha), name="merge",
        grid=(n // tm,),
        in_specs=[rows(a) for a in (x, oa, ob, om, gt)] + [_const_spec(c.shape) for c in consts],
        out_specs=[pl.BlockSpec((tm, d), lambda i: (i, 0)), pl.BlockSpec((tm, LANES), lambda i: (i, 0)),
                   pl.BlockSpec((tm, LANES), lambda i: (i, 0))],
        out_shape=[jax.ShapeDtypeStruct((n, d), F32), jax.ShapeDtypeStruct((n, LANES), jnp.int32),
                   jax.ShapeDtypeStruct((n, LANES), F32)],
        compiler_params=_params(("parallel",)),
    )(x, oa, ob, om, gt, *consts)


def _gather_rows(idx_ref, base, count, src_hbm, dst, sem, start):
    def body(r, carry):
        copy = pltpu.make_async_copy(src_hbm.at[pl.ds(idx_ref[base + r], 1)], dst.at[pl.ds(r, 1)], sem)
        if start:
            copy.start()
        else:
            copy.wait()
        return carry
    lax.fori_loop(0, count, body, 0, unroll=8)


def _moe_expert_kernel(tok_ref, be_ref, nu_ref, x_hbm, wg_ref, wu_ref, wd_ref, y_ref, xbuf, wgb, wub, wdb, sem):
    i = pl.program_id(0)
    used = nu_ref[0]
    rows = xbuf.shape[1]

    @pl.when((i == 0) | (be_ref[i] != be_ref[jnp.maximum(i - 1, 0)]))
    def _():
        wgb[...] = wg_ref[0].astype(BF16)
        wub[...] = wu_ref[0].astype(BF16)
        wdb[...] = wd_ref[0].astype(BF16)

    @pl.when((i == 0) & (used > 0))
    def _():
        _gather_rows(tok_ref, 0, rows, x_hbm, xbuf.at[0], sem.at[0], True)

    @pl.when(i + 1 < used)
    def _():
        nxt = (i + 1) % 2
        _gather_rows(tok_ref, (i + 1) * rows, rows, x_hbm, xbuf.at[nxt], sem.at[nxt], True)

    @pl.when(i < used)
    def _():
        slot = i % 2
        _gather_rows(tok_ref, i * rows, rows, x_hbm, xbuf.at[slot], sem.at[slot], False)
        xb = xbuf[slot].astype(BF16)
        hg = jnp.dot(xb, wgb[...], preferred_element_type=F32)
        hu = jnp.dot(xb, wub[...], preferred_element_type=F32)
        h = hg * jax.nn.sigmoid(hg) * hu
        y_ref[...] = jnp.dot(h.astype(BF16), wdb[...], preferred_element_type=F32)

    @pl.when(i >= used)
    def _():
        y_ref[...] = jnp.zeros_like(y_ref)


def moe_experts(x1, row_tok, blk_e, n_used, e_gate, e_up, e_down):
    n, d = x1.shape
    n_blocks = blk_e.shape[0]
    ff = e_gate.shape[-1]
    return pl.pallas_call(
        _moe_expert_kernel, name="moe_expert",
        grid_spec=pltpu.PrefetchScalarGridSpec(
            num_scalar_prefetch=3,
            grid=(n_blocks,),
            in_specs=[pl.BlockSpec(memory_space=pl.ANY),
                      pl.BlockSpec((1, d, ff), lambda i, tok, be, nu: (be[i], 0, 0)),
                      pl.BlockSpec((1, d, ff), lambda i, tok, be, nu: (be[i], 0, 0)),
                      pl.BlockSpec((1, ff, d), lambda i, tok, be, nu: (be[i], 0, 0))],
            out_specs=pl.BlockSpec((EXPERT_BLOCK, d), lambda i, tok, be, nu: (i, 0)),
            scratch_shapes=[pltpu.VMEM((2, EXPERT_BLOCK, d), F32), pltpu.VMEM((d, ff), BF16),
                            pltpu.VMEM((d, ff), BF16), pltpu.VMEM((ff, d), BF16), pltpu.SemaphoreType.DMA((2,))]),
        out_shape=jax.ShapeDtypeStruct((n_blocks * EXPERT_BLOCK, d), F32),
        compiler_params=_params(("arbitrary",)),
    )(row_tok, blk_e, n_used, x1, e_gate, e_up, e_down)


def _moe_combine_kernel(alpha, dest_ref, ys_hbm, x1_ref, gate_ref, g_ref, b_ref, o_ref, buf, sem):
    i = pl.program_id(0)
    steps = pl.num_programs(0)
    rows = buf.shape[1]

    @pl.when(i == 0)
    def _():
        _gather_rows(dest_ref, 0, rows, ys_hbm, buf.at[0], sem.at[0], True)

    @pl.when(i + 1 < steps)
    def _():
        nxt = (i + 1) % 2
        _gather_rows(dest_ref, (i + 1) * rows, rows, ys_hbm, buf.at[nxt], sem.at[nxt], True)

    slot = i % 2
    _gather_rows(dest_ref, i * rows, rows, ys_hbm, buf.at[slot], sem.at[slot], False)
    tm = x1_ref.shape[0]
    gate = gate_ref[...]
    moe = gate[:, 0:1] * buf[slot, :tm, :] + gate[:, 1:2] * buf[slot, tm:, :]
    o_ref[...] = _ln(alpha * x1_ref[...] + moe, g_ref[...], b_ref[...])


def moe_combine(ys, dest, x1, gate, g, b, tm, alpha):
    n, d = x1.shape
    top_k = dest.shape[0] // n
    assert top_k == 2
    return pl.pallas_call(
        functools.partial(_moe_combine_kernel, alpha), name="moe_combine",
        grid_spec=pltpu.PrefetchScalarGridSpec(
            num_scalar_prefetch=1,
            grid=(n // tm,),
            in_specs=[pl.BlockSpec(memory_space=pl.ANY),
                      pl.BlockSpec((tm, d), lambda i, dest: (i, 0)),
                      pl.BlockSpec((tm, LANES), lambda i, dest: (i, 0)),
                      pl.BlockSpec((1, d), lambda i, dest: (0, 0)),
                      pl.BlockSpec((1, d), lambda i, dest: (0, 0))],
            out_specs=pl.BlockSpec((tm, d), lambda i, dest: (i, 0)),
            scratch_shapes=[pltpu.VMEM((2, top_k * tm, d), F32), pltpu.SemaphoreType.DMA((2,))]),
        out_shape=jax.ShapeDtypeStruct((n, d), F32),
        compiler_params=_params(("arbitrary",)),
    )(dest, ys, x1, gate, g.reshape(1, d), b.reshape(1, d))


def moe_routing(experts):
    n, top_k = experts.shape
    n_exp = N_GROUPS * EXPERTS_PER_GROUP
    a = n * top_k
    flat_e = experts.reshape(a)
    onehot = (flat_e[:, None] == jnp.arange(n_exp, dtype=jnp.int32)[None, :]).astype(jnp.int32)
    rank = jnp.take_along_axis(jnp.cumsum(onehot, axis=0) - onehot, flat_e[:, None], axis=1)[:, 0]
    counts = jnp.sum(onehot, axis=0)
    padded = (counts + EXPERT_BLOCK - 1) // EXPERT_BLOCK * EXPERT_BLOCK
    pad_end = jnp.cumsum(padded)
    dest = (pad_end - padded)[flat_e] + rank
    n_blocks = -(-a // EXPERT_BLOCK) + n_exp
    row_tok = jnp.zeros((n_blocks * EXPERT_BLOCK,), jnp.int32).at[dest].set(jnp.arange(a, dtype=jnp.int32) // top_k)
    blk_start = jnp.arange(n_blocks, dtype=jnp.int32) * EXPERT_BLOCK
    blk_e = jnp.minimum(jnp.sum((pad_end[None, :] <= blk_start[:, None]).astype(jnp.int32), axis=1), n_exp - 1)
    n_used = (pad_end[-1:] // EXPERT_BLOCK).astype(jnp.int32)
    return row_tok, dest.astype(jnp.int32), blk_e, n_used


def hier_moe_ln(x1, eidx, gate, w, tm, alpha):
    row_tok, dest, blk_e, n_used = moe_routing(eidx[:, :2])
    ys = moe_experts(x1, row_tok, blk_e, n_used, w['e_gate'], w['e_up'], w['e_down'])
    dest = dest.reshape(-1, tm, 2).transpose(0, 2, 1).reshape(-1)
    return moe_combine(ys, dest, x1, gate, w['ln2_g'], w['ln2_b'], tm, alpha)


def kernel(x_prompt, x_sample, mem_prompt, state_wkv, state_shift, cache_win_k, cache_win_v, cache_mem_k, cache_mem_v, ln_in_g, ln_in_b, w_in, mu, w0, w_up, a0, a_up, g_up, k_k, k_a, r_k, lnx_g, lnx_b, sinks, w_mem_kv, p_a, p_b, p_m, w_o, ln1_g, ln1_b, w_group, b_group, w_router, b_router, e_gate, e_up, e_down, ln2_g, ln2_b):
    depth = w_in.shape[0]
    assert depth == 1, "single-layer step"
    bsz, seq, d = x_prompt.shape
    dec = x_sample.shape[0]
    assert x_sample.shape[1] == 1
    c_shift = mu.shape[-1]
    c_a = w0.shape[-1]
    window, kv_w = cache_win_k.shape[2], cache_win_k.shape[3] * cache_win_k.shape[4]
    n_mem, mem_w = cache_mem_k.shape[2], cache_mem_k.shape[3] * cache_mem_k.shape[4]
    q_w = sinks.shape[-1] * HEAD
    qkvm_w = q_w + 2 * kv_w + mem_w
    alpha = (2.0 * depth) ** 0.25
    past_len = float(PAST_LEN)
    chunk = 64

    w_in_b = w_in[0].astype(BF16)
    w_parts = [w_in_b[:, :c_shift], w_in_b[:, c_shift:c_shift + qkvm_w], w_in_b[:, c_shift + qkvm_w:]]
    rp = dict(mu=mu[0], w0=w0[0], w_up=w_up[0], a0=a0[0], a_up=a_up[0], g_up=g_up[0], k_k=k_k[0], k_a=k_a[0],
              r_k=r_k[0].reshape(-1))
    n_route = N_GROUPS * (1 + EXPERTS_PER_GROUP)
    mw = dict(ln_in_g=ln_in_g.reshape(1, d), ln_in_b=ln_in_b.reshape(1, d), p_a=p_a[0].astype(BF16),
              p_b=p_b[0].astype(BF16), p_m=p_m[0].astype(BF16), w_o=w_o[0].astype(BF16),
              ln1_g=ln1_g[0].reshape(1, d), ln1_b=ln1_b[0].reshape(1, d),
              w_route=jnp.pad(jnp.concatenate([w_group[0], w_router[0]], axis=1), ((0, 0), (0, LANES - n_route))),
              b_route=jnp.pad(jnp.concatenate([b_group[0], b_router[0]]), (0, LANES - n_route)).reshape(1, LANES),
              e_gate=e_gate[0], e_up=e_up[0], e_down=e_down[0], ln2_g=ln2_g[0], ln2_b=ln2_b[0])

    xp = x_prompt.reshape(bsz * seq, d)
    zr, zq, zg = ln_proj(xp, ln_in_g, ln_in_b, w_parts, 256)
    zr3 = zr.reshape(bsz, seq, c_shift)
    prep = rwkv_prep(zr3, jnp.zeros((bsz, 1, c_shift), F32), rp, chunk, 256)
    o_a, wkv_p = wkv(prep, jnp.zeros((bsz, c_a // HEAD, HEAD, HEAD), F32), lnx_g[0], lnx_b[0], chunk)
    mkv = proj(mem_prompt.reshape(bsz * n_mem, d), w_mem_kv[0].astype(BF16), 256).reshape(bsz, n_mem, 2 * mem_w)
    mk_p, mv_p = mkv[..., :mem_w], mkv[..., mem_w:]
    tables = rope_tables(jnp.arange(seq, dtype=F32))
    o_b, o_m, k_rot = attn_prompt(zq.reshape(bsz, seq, qkvm_w), mk_p, mv_p, sinks[0], tables, window, q_w, kv_w)
    x1, eidx, gate = merge(xp, o_a.reshape(-1, c_a), o_b.reshape(-1, q_w), o_m.reshape(-1, mem_w), zg, mw, 256, alpha)
    y_prompt = hier_moe_ln(x1, eidx, gate, mw, 128, alpha).reshape(bsz, seq, d)
    shift_p = zr3[:, -1]
    kb_p = k_rot[:, -window:].reshape(bsz, window, H_KV, HEAD)
    vb_p = zq.reshape(bsz, seq, qkvm_w)[:, -window:, q_w + kv_w:q_w + 2 * kv_w].reshape(bsz, window, H_KV, HEAD)

    xs = x_sample.reshape(dec, d)
    zr_s, zq_s, zg_s = ln_proj(xs, ln_in_g, ln_in_b, w_parts, dec)
    ops_s = rwkv_prep(zr_s.reshape(1, dec, c_shift), state_shift[0].reshape(1, dec, c_shift), rp, 1, dec)
    o_a_s, wkv_s = wkv_step([a.reshape(dec, c_a) for a in ops_s], state_wkv[0], lnx_g[0], lnx_b[0], 8)
    tables_s = rope_tables(jnp.full((1,), past_len, F32))
    o_b_s, o_m_s, nk_s, nv_s = attn_step(
        zq_s, cache_win_k[0].reshape(dec, window, kv_w), cache_win_v[0].reshape(dec, window, kv_w),
        cache_mem_k[0].reshape(dec, n_mem, mem_w), cache_mem_v[0].reshape(dec, n_mem, mem_w),
        sinks[0], tables_s, q_w, kv_w, 8)
    x1_s, eidx_s, gate_s = merge(xs, o_a_s, o_b_s, o_m_s, zg_s, mw, dec, alpha)
    y_sample = hier_moe_ln(x1_s, eidx_s, gate_s, mw, dec, alpha).reshape(dec, 1, d)

    sd = state_wkv.dtype
    return (y_prompt, y_sample, wkv_p[None].astype(sd), wkv_s[None].astype(sd), shift_p[None], zr_s[None],
            kb_p[None], vb_p[None], nk_s.reshape(dec, window, H_KV, HEAD)[None],
            nv_s.reshape(dec, window, H_KV, HEAD)[None],
            mk_p.reshape(bsz, n_mem, -1, HEAD)[None], mv_p.reshape(bsz, n_mem, -1, HEAD)[None])
```

```python
import functools
import math

import jax
import jax.numpy as jnp
from jax import lax
from jax.experimental import pallas as pl
from jax.experimental.pallas import tpu as pltpu

F32 = jnp.float32
BF16 = jnp.bfloat16
SCAN_BATCH = 4

HEAD = 64
LANES = 128
H_KV = 2
ROT_HALF = 8
ROPE_THETA = 500000.0
PAST_LEN = 8192
N_GROUPS = 4
EXPERTS_PER_GROUP = 8
EXPERT_BLOCK = 128
LN_EPS = 1e-5
LNX_EPS = 64e-5
NEG_INF = -1e30
VMEM_LIMIT = 48 * 1024 * 1024


def _pieces(x, n):
    out = []
    for _ in range(n):
        p = x.astype(BF16)
        out.append(p)
        x = x - p.astype(F32)
    return out


def _mask_dot(x, mask, n=2):
    return sum(jnp.dot(p, mask, preferred_element_type=F32) for p in _pieces(x, n))


def _split3(x, axis, lhs):
    hi = x.astype(BF16).astype(F32)
    lo = x - hi
    return jnp.concatenate([hi, hi, lo] if lhs else [hi, lo, hi], axis=axis).astype(BF16)


def _dot3(a, b):
    return jnp.dot(_split3(a, 1, True), _split3(b, 0, False), preferred_element_type=F32)


def _dot3_t(a, b):
    return lax.dot_general(_split3(a, 1, True), _split3(b, 1, False), (((1,), (1,)), ((), ())),
                           preferred_element_type=F32)


def _bdot(a, b):
    return jnp.dot(a.astype(BF16), b.astype(BF16), preferred_element_type=F32)


def _bdot_t(a, b):
    return lax.dot_general(a.astype(BF16), b.astype(BF16), (((1,), (1,)), ((), ())), preferred_element_type=F32)


def _ln(x, g, b):
    mu = jnp.mean(x, axis=-1, keepdims=True)
    xc = x - mu
    var = jnp.mean(xc * xc, axis=-1, keepdims=True)
    return xc * lax.rsqrt(var + LN_EPS) * g + b


def _const_spec(shape):
    nd = len(shape)
    return pl.BlockSpec(shape, lambda *_: (0,) * nd)


def _params(sem):
    return pltpu.CompilerParams(dimension_semantics=sem, vmem_limit_bytes=VMEM_LIMIT)


def _ln_proj_kernel(x_ref, g_ref, b_ref, *refs):
    nw = len(refs) // 2
    xn = _ln(x_ref[...], g_ref[...], b_ref[...]).astype(BF16)
    for w_ref, o_ref in zip(refs[:nw], refs[nw:]):
        o_ref[...] = jnp.dot(xn, w_ref[...], preferred_element_type=F32)


def ln_proj(x, g, b, ws, tm):
    n, d = x.shape
    return pl.pallas_call(
        _ln_proj_kernel, name="ln_proj",
        grid=(n // tm,),
        in_specs=[pl.BlockSpec((tm, d), lambda i: (i, 0)), _const_spec((1, d)), _const_spec((1, d))]
        + [_const_spec(w.shape) for w in ws],
        out_specs=[pl.BlockSpec((tm, w.shape[1]), lambda i: (i, 0)) for w in ws],
        out_shape=[jax.ShapeDtypeStruct((n, w.shape[1]), F32) for w in ws],
        compiler_params=_params(("parallel",)),
    )(x, g.reshape(1, d), b.reshape(1, d), *ws)


def _rwkv_prep_kernel(chunk, z_ref, prev_ref, mu_ref, w0_ref, wup_ref, a0_ref, aup_ref, gup_ref,
                      kk_ref, ka_ref, rk_ref, hsum_ref, tril_ref, *refs):
    out_refs, carry_ref = refs[:-1], refs[-1]
    z = z_ref[0]
    tt = z.shape[0]
    c_a = w0_ref.shape[-1]
    r_w, r_a, r_g = wup_ref.shape[0], aup_ref.shape[0], gup_ref.shape[0]
    if chunk == 1:
        prev = prev_ref[0]
    else:
        @pl.when(pl.program_id(1) == 0)
        def _():
            carry_ref[...] = prev_ref[0]

        row = lax.broadcasted_iota(jnp.int32, z.shape, 0)
        prev = jnp.where(row == 0, carry_ref[...], pltpu.roll(z, 1, 0))
        carry_ref[...] = z[tt - 1:tt, :]
    zs = z + (prev - z) * mu_ref[...]
    r = zs[:, :c_a]
    k = zs[:, c_a:2 * c_a]
    v = zs[:, 2 * c_a:3 * c_a]
    o = 3 * c_a
    xw = zs[:, o:o + r_w]
    xa = zs[:, o + r_w:o + r_w + r_a]
    xg = zs[:, o + r_w + r_a:o + r_w + r_a + r_g]
    warg = -(w0_ref[...] + _dot3(jnp.tanh(xw), wup_ref[...]))
    softplus = jnp.maximum(warg, 0.0) + jnp.log1p(jnp.exp(-jnp.abs(warg)))
    lw = -jnp.exp(-softplus - 0.5)
    a = jax.nn.sigmoid(a0_ref[...] + _dot3(xa, aup_ref[...]))
    g = _dot3(jax.nn.sigmoid(xg), gup_ref[...])
    kkr = k * kk_ref[...]
    kk = kkr / jnp.maximum(jnp.sqrt(_mask_dot(kkr * kkr, hsum_ref[...])), 1e-12)
    k2 = k * (1.0 + (a - 1.0) * ka_ref[...])
    bonus = _mask_dot(r * k2 * rk_ref[...], hsum_ref[...]) * v
    kb = kk * a
    if chunk == 1:
        outs = (r, jnp.exp(lw), k2, v, -kk, kb, g, bonus)
        for ref, val in zip(out_refs, outs):
            ref[0] = val
        return
    pieces = _pieces(lw, 3)
    ones3 = jnp.ones((chunk, 3 * chunk), BF16)
    cw, cwl = [], []
    for c in range(tt // chunk):
        stack = jnp.concatenate([p[c * chunk:(c + 1) * chunk] for p in pieces], axis=0)
        cw.append(jnp.dot(tril_ref[...], stack, preferred_element_type=F32))
        cwl.append(jnp.dot(ones3, stack, preferred_element_type=F32))
    cw = jnp.concatenate(cw, axis=0)
    cwl = jnp.concatenate(cwl, axis=0)
    e_inv = jnp.exp(-cw)
    e_hat = jnp.exp(cwl - cw)
    outs = (r * jnp.exp(cw), -kk * jnp.exp(cw - lw), kb * e_inv, k2 * e_inv, kb * e_hat, k2 * e_hat, v, g, bonus)
    for ref, val in zip(out_refs[:-1], outs):
        ref[0] = val
    wl = jnp.exp(cwl)
    for c in range(tt // chunk):
        out_refs[-1][0, c] = wl[c * chunk:c * chunk + 1, :]


def _head_sum_matrix(width):
    idx = jnp.arange(width)
    return ((idx[:, None] // HEAD) == (idx[None, :] // HEAD)).astype(BF16)


def rwkv_prep(z, prev, p, chunk, tt):
    bsz, t, cs = z.shape
    c_a = p['w0'].shape[-1]
    assert prev.shape[1] == (t if chunk == 1 else 1)
    ridx = jnp.arange(chunk)
    tril = jnp.tile((ridx[None, :] <= ridx[:, None]).astype(BF16), (1, 3))
    hsum = _head_sum_matrix(c_a)
    row = lambda x: x.reshape(1, -1)
    tile = pl.BlockSpec((1, tt, c_a), lambda b, i: (b, i, 0))
    full = jax.ShapeDtypeStruct((bsz, t, c_a), F32)
    if chunk == 1:
        out_specs, out_shape = [tile] * 8, [full] * 8
        prev_spec = pl.BlockSpec((1, tt, cs), lambda b, i: (b, i, 0))
    else:
        out_specs = [tile] * 9 + [pl.BlockSpec((1, tt // chunk, 1, c_a), lambda b, i: (b, i, 0, 0))]
        out_shape = [full] * 9 + [jax.ShapeDtypeStruct((bsz, t // chunk, 1, c_a), F32)]
        prev_spec = pl.BlockSpec((1, 1, cs), lambda b, i: (b, 0, 0))
    consts = [row(p['mu']), row(p['w0']), p['w_up'], row(p['a0']), p['a_up'], p['g_up'], row(p['k_k']),
              row(p['k_a']), row(p['r_k']), hsum, tril]
    return pl.pallas_call(
        functools.partial(_rwkv_prep_kernel, chunk), name="rwkv_prep",
        grid=(bsz, t // tt),
        in_specs=[pl.BlockSpec((1, tt, cs), lambda b, i: (b, i, 0)), prev_spec] + [_const_spec(c.shape) for c in consts],
        out_specs=out_specs,
        out_shape=out_shape,
        scratch_shapes=[pltpu.VMEM((1, cs), F32)],
        compiler_params=_params(("parallel", "arbitrary")),
    )(z, prev, *consts)


def _wkv_chunk_kernel(ra_ref, at_ref, bt_ref, kt_ref, bh_ref, kh_ref, v_ref, wl_ref, rp_ref, y0_ref, m_ref, n_ref):
    length = ra_ref.shape[1]
    heads = ra_ref.shape[2] // HEAD
    row = lax.broadcasted_iota(jnp.int32, (length, length), 0)
    col = lax.broadcasted_iota(jnp.int32, (length, length), 1)
    strict = row > col
    incl = row >= col
    hrow = lax.broadcasted_iota(jnp.int32, (HEAD, HEAD), 0)
    hcol = lax.broadcasted_iota(jnp.int32, (HEAD, HEAD), 1)
    hs = range(heads)
    sls = [slice(h * HEAD, (h + 1) * HEAD) for h in hs]
    gram = [_bdot_t(jnp.concatenate([at_ref[0, :, sl], ra_ref[0, :, sl]], axis=0),
                    jnp.concatenate([bt_ref[0, :, sl], kt_ref[0, :, sl]], axis=0)) for sl in sls]
    a_ab = [jnp.where(strict, g[:length, :length], 0.0) for g in gram]
    a_ak = [jnp.where(strict, g[:length, length:], 0.0) for g in gram]
    a_rb = [jnp.where(incl, g[length:, :length], 0.0) for g in gram]
    a_rk = [jnp.where(incl, g[length:, length:], 0.0) for g in gram]
    inv = [jnp.where(row == col, 1.0, a) for a in a_ab]
    pw = [_bdot(a, a) for a in a_ab]
    av = [_dot3(a_ak[h], v_ref[0, :, sls[h]]) for h in hs]
    y0 = [_dot3(a_rk[h], v_ref[0, :, sls[h]]) for h in hs]
    nk = [_dot3(v_ref[0, :, sl].T, kh_ref[0, :, sl]) for sl in sls]
    for _ in range(int(math.log2(length)) - 2):
        both = [_bdot(jnp.concatenate([pw[h], inv[h]], axis=0), pw[h]) for h in hs]
        pw = [b[:length] for b in both]
        inv = [inv[h] + both[h][length:] for h in hs]
    inv = [inv[h] + _bdot(inv[h], pw[h]) for h in hs]
    pq = [_dot3(inv[h], jnp.concatenate([at_ref[0, :, sls[h]], av[h]], axis=1)) for h in hs]
    ry = [_dot3(a_rb[h], pq[h]) for h in hs]
    mn = [_dot3(pq[h].T, bh_ref[0, :, sls[h]]) for h in hs]
    rp_ref[0] = jnp.concatenate([ra_ref[0, :, sls[h]] + ry[h][:, :HEAD] for h in hs], axis=1)
    y0_ref[0] = jnp.concatenate([y0[h] + ry[h][:, HEAD:] for h in hs], axis=1)
    m_ref[0, 0] = jnp.concatenate(
        [jnp.where(hrow == hcol, wl_ref[0, 0, :, sls[h]], 0.0) + mn[h][:HEAD] for h in hs], axis=1)
    n_ref[0, 0] = jnp.concatenate([nk[h] + mn[h][HEAD:] for h in hs], axis=1)


def _head_norm_wide(y, hsum):
    yc = y - _mask_dot(y, hsum) * (1.0 / HEAD)
    return yc * lax.rsqrt(_mask_dot(yc * yc, hsum) * (1.0 / HEAD) + LNX_EPS)


def _wkv_scan_kernel(rp_ref, y0_ref, m_ref, n_ref, g_ref, bonus_ref, s0_ref, lg_ref, lb_ref, hsum_ref,
                     o_ref, sout_ref, s_ref):
    c = pl.program_id(1)
    nb = rp_ref.shape[0]
    heads = rp_ref.shape[2] // HEAD

    @pl.when(c == 0)
    def _():
        s_ref[...] = s0_ref[...]

    pairs = [(b, h, slice(h * HEAD, (h + 1) * HEAD)) for b in range(nb) for h in range(heads)]
    s = [s_ref[b, h] for b, h, _ in pairs]
    s_new = [n_ref[b, 0, :, sl] + _dot3(s[j], m_ref[b, 0, :, sl]) for j, (b, h, sl) in enumerate(pairs)]
    ys = [y0_ref[b, :, sl] + _dot3_t(rp_ref[b, :, sl], s[j]) for j, (b, h, sl) in enumerate(pairs)]
    for j, (b, h, _) in enumerate(pairs):
        s_ref[b, h] = s_new[j]
    for b in range(nb):
        y = _head_norm_wide(jnp.concatenate(ys[b * heads:(b + 1) * heads], axis=-1), hsum_ref[...])
        o_ref[b] = (y * lg_ref[...] + lb_ref[...] + bonus_ref[b]) * g_ref[b]

    @pl.when(c == pl.num_programs(1) - 1)
    def _():
        sout_ref[...] = s_ref[...]


def wkv(prep, s0, lnx_g, lnx_b, chunk):
    ra, at, bt, kt, bh, kh, v, g, bonus, wl = prep
    bsz, t, c_a = ra.shape
    heads = c_a // HEAD
    n_chunks = t // chunk
    assert chunk & (chunk - 1) == 0 and chunk >= 4 and t % chunk == 0
    tile = pl.BlockSpec((1, chunk, c_a), lambda b, c: (b, c, 0))
    mat = pl.BlockSpec((1, 1, HEAD, c_a), lambda b, c: (b, c, 0, 0))
    st = pl.BlockSpec((1, heads, HEAD, HEAD), lambda b, c: (b, 0, 0, 0))
    full = jax.ShapeDtypeStruct((bsz, t, c_a), F32)
    mats = jax.ShapeDtypeStruct((bsz, n_chunks, HEAD, c_a), F32)
    rp, y0, m, n = pl.pallas_call(
        _wkv_chunk_kernel, name="wkv_chunk",
        grid=(bsz, n_chunks),
        in_specs=[tile] * 7 + [pl.BlockSpec((1, 1, 1, c_a), lambda b, c: (b, c, 0, 0))],
        out_specs=[tile, tile, mat, mat],
        out_shape=[full, full, mats, mats],
        compiler_params=_params(("parallel", "parallel")),
    )(ra, at, bt, kt, bh, kh, v, wl)
    nb = SCAN_BATCH if bsz % SCAN_BATCH == 0 else 1
    tile = pl.BlockSpec((nb, chunk, c_a), lambda b, c: (b, c, 0))
    mat = pl.BlockSpec((nb, 1, HEAD, c_a), lambda b, c: (b, c, 0, 0))
    st = pl.BlockSpec((nb, heads, HEAD, HEAD), lambda b, c: (b, 0, 0, 0))
    return pl.pallas_call(
        _wkv_scan_kernel, name="wkv_scan",
        grid=(bsz // nb, n_chunks),
        in_specs=[tile, tile, mat, mat, tile, tile, st, _const_spec((1, c_a)), _const_spec((1, c_a)),
                  _const_spec((c_a, c_a))],
        out_specs=[tile, st],
        out_shape=[full, jax.ShapeDtypeStruct(s0.shape, F32)],
        scratch_shapes=[pltpu.VMEM((nb, heads, HEAD, HEAD), F32)],
        compiler_params=_params(("parallel", "arbitrary")),
    )(rp, y0, m, n, g, bonus, s0, lnx_g.reshape(1, c_a), lnx_b.reshape(1, c_a), _head_sum_matrix(c_a))


def _wkv_step_kernel(r_ref, w_ref, k_ref, v_ref, a_ref, b_ref, g_ref, bonus_ref, s_ref, lg_ref, lb_ref, hsum_ref,
                     o_ref, sout_ref):
    bt, heads = s_ref.shape[0], s_ref.shape[1]
    c_a = heads * HEAD
    hsum = hsum_ref[...]
    diag = (lax.broadcasted_iota(jnp.int32, (HEAD, c_a), 1) % HEAD
            == lax.broadcasted_iota(jnp.int32, (HEAD, c_a), 0))
    seqs = range(bt)
    row = lambda ref, i: ref[i:i + 1, :]
    s = [jnp.concatenate([s_ref[i, h] for h in range(heads)], axis=1) for i in seqs]
    sa = [_mask_dot(s[i] * row(a_ref, i), hsum) for i in seqs]
    v_rows = [_mask_dot(jnp.where(diag, row(v_ref, i), 0.0), hsum) for i in seqs]
    s = [s[i] * row(w_ref, i) + sa[i] * row(b_ref, i) + v_rows[i] * row(k_ref, i) for i in seqs]
    for i in seqs:
        for h in range(heads):
            sout_ref[i, h] = s[i][:, h * HEAD:(h + 1) * HEAD]
    y_rows = [_mask_dot(s[i] * row(r_ref, i), hsum) for i in seqs]
    y = jnp.concatenate([jnp.sum(jnp.where(diag, y_rows[i], 0.0), axis=0, keepdims=True) for i in seqs], axis=0)
    y = _head_norm_wide(y, hsum)
    o_ref[...] = (y * lg_ref[...] + lb_ref[...] + bonus_ref[...]) * g_ref[...]


def wkv_step(ops, s0, lnx_g, lnx_b, bt):
    bsz, c_a = ops[0].shape
    rows = pl.BlockSpec((bt, c_a), lambda i: (i, 0))
    st = pl.BlockSpec((bt,) + s0.shape[1:], lambda i: (i, 0, 0, 0))
    hsum = _head_sum_matrix(c_a)
    return pl.pallas_call(
        _wkv_step_kernel, name="wkv_step",
        grid=(bsz // bt,),
        in_specs=[rows] * 8 + [st, _const_spec((1, c_a)), _const_spec((1, c_a)), _const_spec(hsum.shape)],
        out_specs=[rows, st],
        out_shape=[jax.ShapeDtypeStruct((bsz, c_a), F32), jax.ShapeDtypeStruct(s0.shape, F32)],
        compiler_params=_params(("parallel",)),
    )(*ops, s0, lnx_g.reshape(1, c_a), lnx_b.reshape(1, c_a), hsum)


def rope_tables(pos):
    inv_freq = ROPE_THETA ** (-jnp.arange(ROT_HALF, dtype=F32) / ROT_HALF)
    ang = pos[:, None] * inv_freq[None, :]
    cos, sin = jnp.cos(ang), jnp.sin(ang)
    t = pos.shape[0]
    rest = HEAD - 2 * ROT_HALF
    c = jnp.concatenate([cos, cos, jnp.ones((t, rest), F32)], axis=1)
    s1 = jnp.concatenate([jnp.zeros((t, ROT_HALF), F32), sin, jnp.zeros((t, rest), F32)], axis=1)
    s2 = jnp.concatenate([-sin, jnp.zeros((t, HEAD - ROT_HALF), F32)], axis=1)
    rep = LANES // HEAD
    return tuple(jnp.tile(x, (1, rep)) for x in (c, s1, s2))


def _rope(x, c, s1, s2):
    width = x.shape[-1]
    rep = width // c.shape[-1]
    if rep > 1:
        c, s1, s2 = (jnp.concatenate([t] * rep, axis=1) for t in (c, s1, s2))
    return x * c + pltpu.roll(x, ROT_HALF, 1) * s1 + pltpu.roll(x, width - ROT_HALF, 1) * s2


def _attn_kernel(window, q_w, kv_w, zq_ref, kp_ref, vp_ref, rc_ref, rs1_ref, rs2_ref, pc_ref, ps1_ref, ps2_ref,
                 mk_ref, mv_ref, sink_ref, ob_ref, om_ref, kr_ref):
    n = pl.program_id(1)
    zq = zq_ref[0]
    scale = HEAD ** -0.5
    q = zq[:, :q_w]
    k = zq[:, q_w:q_w + kv_w]
    v = zq[:, q_w + kv_w:q_w + 2 * kv_w]
    qm = zq[:, q_w + 2 * kv_w:]
    qr = _rope(q, rc_ref[...], rs1_ref[...], rs2_ref[...]) * scale
    kr = _rope(k, rc_ref[...], rs1_ref[...], rs2_ref[...])
    kr_ref[0] = kr
    kpr = _rope(kp_ref[0], pc_ref[...], ps1_ref[...], ps2_ref[...])
    vp = vp_ref[0]
    gqa = q_w // kv_w
    assert window & (window - 1) == 0
    qi = lax.broadcasted_iota(jnp.int32, (gqa * window, 2 * window), 0) & (window - 1)
    kj = lax.broadcasted_iota(jnp.int32, (gqa * window, 2 * window), 1)
    valid = (kj > qi) & (kj <= qi + window) & ((kj >= window) | (n > 0))
    tdot = lambda a, b: lax.dot_general(a, b, (((1,), (1,)), ((), ())), preferred_element_type=F32)
    dot = lambda a, b: jnp.dot(a, b, preferred_element_type=F32)
    kvs = range(kv_w // HEAD)
    mhs = range(qm.shape[1] // HEAD)
    hsl = lambda h: slice(h * HEAD, (h + 1) * HEAD)
    qmb = (qm * scale).astype(BF16)
    mk = mk_ref[0].astype(BF16)
    mv = mv_ref[0].astype(BF16)
    kcat = [jnp.concatenate([kpr[:, hsl(hk)], kr[:, hsl(hk)]], axis=0).astype(BF16) for hk in kvs]
    vcat = [jnp.concatenate([vp[:, hsl(hk)], v[:, hsl(hk)]], axis=0).astype(BF16) for hk in kvs]
    qs = [jnp.concatenate([qr[:, hsl(hk * gqa + g)] for g in range(gqa)], axis=0).astype(BF16) for hk in kvs]
    s = [jnp.where(valid, tdot(qs[hk], kcat[hk]), NEG_INF) for hk in kvs]
    sm = [tdot(qmb[:, hsl(h)], mk[:, hsl(h)]) for h in mhs]
    sink = [jnp.concatenate([jnp.full((window, 1), sink_ref[hk * gqa + g], F32) for g in range(gqa)], axis=0)
            for hk in kvs]
    m = [jnp.maximum(jnp.max(s[hk], axis=-1, keepdims=True), sink[hk]) for hk in kvs]
    p = [jnp.exp(s[hk] - m[hk]) for hk in kvs]
    pm = [jnp.exp(sm[h] - jnp.max(sm[h], axis=-1, keepdims=True)) for h in mhs]
    o = [dot(p[hk].astype(BF16), vcat[hk])
         / (jnp.sum(p[hk], axis=-1, keepdims=True) + jnp.exp(sink[hk] - m[hk])) for hk in kvs]
    om = [dot(pm[h].astype(BF16), mv[:, hsl(h)]) / jnp.sum(pm[h], axis=-1, keepdims=True) for h in mhs]
    ob_ref[0] = jnp.concatenate([o[hk][g * window:(g + 1) * window] for hk in kvs for g in range(gqa)], axis=1)
    om_ref[0] = jnp.concatenate(om, axis=1)


def attn_prompt(zq, mk, mv, sinks, tables, window, q_w, kv_w):
    bsz, t, zw = zq.shape
    mem_w = zw - q_w - 2 * kv_w
    assert kv_w == LANES and q_w % kv_w == 0
    kcol, vcol = q_w // kv_w, q_w // kv_w + 1
    prev = lambda n: jnp.maximum(n - 1, 0)
    tab = pl.BlockSpec((window, LANES), lambda b, n: (n, 0))
    ptab = pl.BlockSpec((window, LANES), lambda b, n: (prev(n), 0))
    mem = pl.BlockSpec((1,) + mk.shape[1:], lambda b, n: (b, 0, 0))
    return pl.pallas_call(
        functools.partial(_attn_kernel, window, q_w, kv_w), name="attn",
        grid=(bsz, t // window),
        in_specs=[pl.BlockSpec((1, window, zw), lambda b, n: (b, n, 0)),
                  pl.BlockSpec((1, window, kv_w), lambda b, n: (b, prev(n), kcol)),
                  pl.BlockSpec((1, window, kv_w), lambda b, n: (b, prev(n), vcol)),
                  tab, tab, tab, ptab, ptab, ptab, mem, mem,
                  pl.BlockSpec(memory_space=pltpu.SMEM)],
        out_specs=[pl.BlockSpec((1, window, q_w), lambda b, n: (b, n, 0)),
                   pl.BlockSpec((1, window, mem_w), lambda b, n: (b, n, 0)),
                   pl.BlockSpec((1, window, kv_w), lambda b, n: (b, n, 0))],
        out_shape=[jax.ShapeDtypeStruct((bsz, t, q_w), F32), jax.ShapeDtypeStruct((bsz, t, mem_w), F32),
                   jax.ShapeDtypeStruct((bsz, t, kv_w), F32)],
        compiler_params=_params(("parallel", "parallel")),
    )(zq, zq, zq, *tables, *tables, mk, mv, sinks)


def _attn_step_kernel(q_w, kv_w, zq_ref, ck_ref, cv_ref, mk_ref, mv_ref, rc_ref, rs1_ref, rs2_ref, sink_ref,
                      ob_ref, om_ref, nk_ref, nv_ref):
    bt = zq_ref.shape[0]
    window = ck_ref.shape[1]
    mem_w = om_ref.shape[1]
    n_q, gqa, per_vreg = q_w // HEAD, q_w // kv_w, LANES // HEAD
    scale = HEAD ** -0.5
    zq = zq_ref[...]
    q = _rope(zq[:, :q_w], rc_ref[...], rs1_ref[...], rs2_ref[...]) * scale
    k_new = _rope(zq[:, q_w:q_w + kv_w], rc_ref[...], rs1_ref[...], rs2_ref[...])
    v_new = zq[:, q_w + kv_w:q_w + 2 * kv_w]
    qm = zq[:, q_w + 2 * kv_w:] * scale
    own = lambda w: (lax.broadcasted_iota(jnp.int32, (n_q, w), 1) // HEAD
                     == lax.broadcasted_iota(jnp.int32, (n_q, w), 0))
    own_q, own_m = own(q_w), own(mem_w)
    hrow = lax.broadcasted_iota(jnp.int32, (n_q, LANES), 0)
    hblk = lax.broadcasted_iota(jnp.int32, (n_q, LANES), 1) // HEAD
    swap = (hrow % per_vreg) != (hrow // gqa)
    keep = hblk == hrow % per_vreg
    key_ok = lax.broadcasted_iota(jnp.int32, (n_q, window), 1) >= 1
    wrow = lax.broadcasted_iota(jnp.int32, (window, kv_w), 0)
    sink = sink_ref[...]
    bs = range(bt)
    tdot = lambda a, b: lax.dot_general(a.astype(BF16), b.astype(BF16), (((1,), (1,)), ((), ())),
                                        preferred_element_type=F32)
    for b in bs:
        nk_ref[b] = jnp.where(wrow == window - 1, k_new[b:b + 1], pltpu.roll(ck_ref[b], window - 1, 0))
        nv_ref[b] = jnp.where(wrow == window - 1, v_new[b:b + 1], pltpu.roll(cv_ref[b], window - 1, 0))
    q8 = []
    for b in bs:
        rep = jnp.where(own_q, q[b:b + 1], 0.0)
        fold = sum(rep[:, c * LANES:(c + 1) * LANES] for c in range(q_w // LANES))
        q8.append(jnp.where(swap, pltpu.roll(fold, HEAD, 1), fold))
    qm8 = [jnp.where(own_m, qm[b:b + 1], 0.0) for b in bs]
    s = [jnp.where(key_ok, tdot(q8[b], ck_ref[b]), NEG_INF) for b in bs]
    sm = [tdot(qm8[b], mk_ref[b]) for b in bs]
    s_new = [jnp.sum(q8[b] * k_new[b:b + 1], axis=1, keepdims=True) for b in bs]
    m = [jnp.maximum(jnp.maximum(jnp.max(s[b], axis=1, keepdims=True), s_new[b]), sink) for b in bs]
    p = [jnp.exp(s[b] - m[b]) for b in bs]
    p_new = [jnp.exp(s_new[b] - m[b]) for b in bs]
    pm = [jnp.exp(sm[b] - jnp.max(sm[b], axis=1, keepdims=True)) for b in bs]
    o8 = [(_bdot(p[b], cv_ref[b]) + p_new[b] * v_new[b:b + 1])
          / (jnp.sum(p[b], axis=1, keepdims=True) + p_new[b] + jnp.exp(sink - m[b])) for b in bs]
    om8 = [_bdot(pm[b], mv_ref[b]) / jnp.sum(pm[b], axis=1, keepdims=True) for b in bs]
    ob, om = [], []
    for b in bs:
        o = jnp.where(swap, pltpu.roll(o8[b], HEAD, 1), o8[b])
        o = jnp.concatenate([jnp.where(keep, o, 0.0)] * (q_w // LANES), axis=1)
        ob.append(jnp.sum(jnp.where(own_q, o, 0.0), axis=0, keepdims=True))
        om.append(jnp.sum(jnp.where(own_m, om8[b], 0.0), axis=0, keepdims=True))
    ob_ref[...] = jnp.concatenate(ob, axis=0)
    om_ref[...] = jnp.concatenate(om, axis=0)


def attn_step(zq, ck, cv, mk, mv, sinks, tables, q_w, kv_w, bt):
    bsz, zw = zq.shape
    mem_w = zw - q_w - 2 * kv_w
    window, n_mem = ck.shape[1], mk.shape[1]
    n_q = q_w // HEAD
    assert kv_w == LANES and mem_w // HEAD <= n_q and n_q == 8
    rows = lambda w: pl.BlockSpec((bt, w), lambda i: (i, 0))
    cache = pl.BlockSpec((bt, window, kv_w), lambda i: (i, 0, 0))
    mem = pl.BlockSpec((bt, n_mem, mem_w), lambda i: (i, 0, 0))
    consts = list(tables) + [sinks.reshape(n_q, 1)]
    const_specs = [_const_spec(c.shape) for c in consts]
    return pl.pallas_call(
        functools.partial(_attn_step_kernel, q_w, kv_w), name="attn_step",
        grid=(bsz // bt,),
        in_specs=[rows(zw), cache, cache, mem, mem] + const_specs,
        out_specs=[rows(q_w), rows(mem_w), cache, cache],
        out_shape=[jax.ShapeDtypeStruct((bsz, q_w), F32), jax.ShapeDtypeStruct((bsz, mem_w), F32),
                   jax.ShapeDtypeStruct(ck.shape, F32), jax.ShapeDtypeStruct(cv.shape, F32)],
        compiler_params=_params(("parallel",)),
    )(zq, ck, cv, mk, mv, *consts)


def _proj_kernel(x_ref, w_ref, o_ref):
    o_ref[...] = jnp.dot(x_ref[...].astype(BF16), w_ref[...], preferred_element_type=F32)


def proj(x, w, tm):
    n, d = x.shape
    return pl.pallas_call(
        _proj_kernel, name="proj",
        grid=(n // tm,),
        in_specs=[pl.BlockSpec((tm, d), lambda i: (i, 0)), _const_spec(w.shape)],
        out_specs=pl.BlockSpec((tm, w.shape[1]), lambda i: (i, 0)),
        out_shape=jax.ShapeDtypeStruct((n, w.shape[1]), F32),
        compiler_params=_params(("parallel",)),
    )(x, w)


def _merge_kernel(alpha, x_ref, oa_ref, ob_ref, om_ref, gt_ref, lig_ref, lib_ref, pa_ref, pb_ref, pm_ref, wo_ref,
                  l1g_ref, l1b_ref, wr_ref, br_ref, x1_ref, eidx_ref, gate_ref):
    d = x_ref.shape[1]
    xn = _ln(x_ref[...], lig_ref[...], lib_ref[...])
    gts = jax.nn.sigmoid(gt_ref[...])
    merged = (gts[:, :d] * _bdot(oa_ref[...], pa_ref[...]) + gts[:, d:2 * d] * _bdot(ob_ref[...], pb_ref[...])
              + gts[:, 2 * d:] * _bdot(om_ref[...], pm_ref[...]))
    x1 = _ln(alpha * xn + _bdot(merged, wo_ref[...]), l1g_ref[...], l1b_ref[...])
    x1_ref[...] = x1
    x_hi, x_lo = _pieces(x1, 2)
    w_hi, w_lo = _pieces(wr_ref[...], 2)
    dot = lambda a, b: jnp.dot(a, b, preferred_element_type=F32)
    logits = dot(x_hi, w_hi) + dot(x_hi, w_lo) + dot(x_lo, w_hi) + br_ref[...]
    lane = lax.broadcasted_iota(jnp.int32, logits.shape, 1)
    lane_f = lane.astype(F32)
    first = lambda hit: jnp.min(jnp.where(hit, lane_f, float(LANES)), axis=-1, keepdims=True).astype(jnp.int32)
    gmask = lane < N_GROUPS
    gl = jnp.where(gmask, logits, NEG_INF)
    gmax = jnp.max(gl, axis=-1, keepdims=True)
    gidx = first(gl == gmax)
    g_w = 1.0 / jnp.sum(jnp.where(gmask, jnp.exp(gl - gmax), 0.0), axis=-1, keepdims=True)
    lo = N_GROUPS + gidx * EXPERTS_PER_GROUP
    el = jnp.where((lane >= lo) & (lane < lo + EXPERTS_PER_GROUP), logits, NEG_INF)
    v1 = jnp.max(el, axis=-1, keepdims=True)
    i1 = first(el == v1)
    el2 = jnp.where(lane == i1, NEG_INF, el)
    v2 = jnp.max(el2, axis=-1, keepdims=True)
    i2 = first(el2 == v2)
    e2 = jnp.exp(v2 - v1)
    gate1 = g_w / (1.0 + e2)
    eidx_ref[...] = jnp.where(lane == 0, i1 - N_GROUPS, jnp.where(lane == 1, i2 - N_GROUPS, 0))
    gate_ref[...] = jnp.where(lane == 0, gate1, jnp.where(lane == 1, gate1 * e2, 0.0))


def merge(x, oa, ob, om, gt, w, tm, alpha):
    n, d = x.shape
    rows = lambda a: pl.BlockSpec((tm, a.shape[1]), lambda i: (i, 0))
    consts = [w['ln_in_g'], w['ln_in_b'], w['p_a'], w['p_b'], w['p_m'], w['w_o'], w['ln1_g'], w['ln1_b'],
              w['w_route'], w['b_route']]
    return pl.pallas_call(
        functools.partial(_merge_kernel, alpha), name="merge",
        grid=(n // tm,),
        in_specs=[rows(a) for a in (x, oa, ob, om, gt)] + [_const_spec(c.shape) for c in consts],
        out_specs=[pl.BlockSpec((tm, d), lambda i: (i, 0)), pl.BlockSpec((tm, LANES), lambda i: (i, 0)),
                   pl.BlockSpec((tm, LANES), lambda i: (i, 0))],
        out_shape=[jax.ShapeDtypeStruct((n, d), F32), jax.ShapeDtypeStruct((n, LANES), jnp.int32),
                   jax.ShapeDtypeStruct((n, LANES), F32)],
        compiler_params=_params(("parallel",)),
    )(x, oa, ob, om, gt, *consts)


GATHER_UNROLL = 8


def _gather_rows(idx_ref, base, count, src_hbm, dst, sem, start):
    if not start:
        pltpu.make_async_copy(src_hbm.at[pl.ds(0, count)], dst, sem).wait()
        return

    def body(g, carry):
        for j in range(GATHER_UNROLL):
            r = g * GATHER_UNROLL + j
            pltpu.make_async_copy(src_hbm.at[pl.ds(idx_ref[base + r], 1)], dst.at[pl.ds(r, 1)], sem).start(
                priority=j % 2)
        return carry
    lax.fori_loop(0, count // GATHER_UNROLL, body, 0)


def _moe_expert_kernel(tok_ref, be_ref, nu_ref, x_hbm, wg_ref, wu_ref, wd_ref, y_ref, xbuf, wgb, wub, wdb, sem):
    i = pl.program_id(0)
    used = nu_ref[0]
    rows = xbuf.shape[1]

    @pl.when((i == 0) | (be_ref[i] != be_ref[jnp.maximum(i - 1, 0)]))
    def _():
        wgb[...] = wg_ref[0].astype(BF16)
        wub[...] = wu_ref[0].astype(BF16)
        wdb[...] = wd_ref[0].astype(BF16)

    @pl.when((i == 0) & (used > 0))
    def _():
        _gather_rows(tok_ref, 0, rows, x_hbm, xbuf.at[0], sem.at[0], True)

    @pl.when(i + 1 < used)
    def _():
        nxt = (i + 1) % 2
        _gather_rows(tok_ref, (i + 1) * rows, rows, x_hbm, xbuf.at[nxt], sem.at[nxt], True)

    @pl.when(i < used)
    def _():
        slot = i % 2
        _gather_rows(tok_ref, i * rows, rows, x_hbm, xbuf.at[slot], sem.at[slot], False)
        xb = xbuf[slot].astype(BF16)
        hg = jnp.dot(xb, wgb[...], preferred_element_type=F32)
        hu = jnp.dot(xb, wub[...], preferred_element_type=F32)
        h = hg * jax.nn.sigmoid(hg) * hu
        y_ref[...] = jnp.dot(h.astype(BF16), wdb[...], preferred_element_type=F32)

    @pl.when(i >= used)
    def _():
        y_ref[...] = jnp.zeros_like(y_ref)


def moe_experts(x1, row_tok, blk_e, n_used, e_gate, e_up, e_down):
    n, d = x1.shape
    n_blocks = blk_e.shape[0]
    ff = e_gate.shape[-1]
    return pl.pallas_call(
        _moe_expert_kernel, name="moe_expert",
        grid_spec=pltpu.PrefetchScalarGridSpec(
            num_scalar_prefetch=3,
            grid=(n_blocks,),
            in_specs=[pl.BlockSpec(memory_space=pl.ANY),
                      pl.BlockSpec((1, d, ff), lambda i, tok, be, nu: (be[i], 0, 0)),
                      pl.BlockSpec((1, d, ff), lambda i, tok, be, nu: (be[i], 0, 0)),
                      pl.BlockSpec((1, ff, d), lambda i, tok, be, nu: (be[i], 0, 0))],
            out_specs=pl.BlockSpec((EXPERT_BLOCK, d), lambda i, tok, be, nu: (i, 0)),
            scratch_shapes=[pltpu.VMEM((2, EXPERT_BLOCK, d), F32), pltpu.VMEM((d, ff), BF16),
                            pltpu.VMEM((d, ff), BF16), pltpu.VMEM((ff, d), BF16), pltpu.SemaphoreType.DMA((2,))]),
        out_shape=jax.ShapeDtypeStruct((n_blocks * EXPERT_BLOCK, d), F32),
        compiler_params=_params(("arbitrary",)),
    )(row_tok, blk_e, n_used, x1, e_gate, e_up, e_down)


def _moe_combine_kernel(alpha, dest_ref, ys_hbm, x1_ref, gate_ref, g_ref, b_ref, o_ref, buf, sem):
    i = pl.program_id(0)
    steps = pl.num_programs(0)
    rows = buf.shape[1]

    @pl.when(i == 0)
    def _():
        _gather_rows(dest_ref, 0, rows, ys_hbm, buf.at[0], sem.at[0], True)

    @pl.when(i + 1 < steps)
    def _():
        nxt = (i + 1) % 2
        _gather_rows(dest_ref, (i + 1) * rows, rows, ys_hbm, buf.at[nxt], sem.at[nxt], True)

    slot = i % 2
    _gather_rows(dest_ref, i * rows, rows, ys_hbm, buf.at[slot], sem.at[slot], False)
    tm = x1_ref.shape[0]
    gate = gate_ref[...]
    moe = gate[:, 0:1] * buf[slot, :tm, :] + gate[:, 1:2] * buf[slot, tm:, :]
    o_ref[...] = _ln(alpha * x1_ref[...] + moe, g_ref[...], b_ref[...])


def moe_combine(ys, dest, x1, gate, g, b, tm, alpha):
    n, d = x1.shape
    top_k = dest.shape[0] // n
    assert top_k == 2
    return pl.pallas_call(
        functools.partial(_moe_combine_kernel, alpha), name="moe_combine",
        grid_spec=pltpu.PrefetchScalarGridSpec(
            num_scalar_prefetch=1,
            grid=(n // tm,),
            in_specs=[pl.BlockSpec(memory_space=pl.ANY),
                      pl.BlockSpec((tm, d), lambda i, dest: (i, 0)),
                      pl.BlockSpec((tm, LANES), lambda i, dest: (i, 0)),
                      pl.BlockSpec((1, d), lambda i, dest: (0, 0)),
                      pl.BlockSpec((1, d), lambda i, dest: (0, 0))],
            out_specs=pl.BlockSpec((tm, d), lambda i, dest: (i, 0)),
            scratch_shapes=[pltpu.VMEM((2, top_k * tm, d), F32), pltpu.SemaphoreType.DMA((2,))]),
        out_shape=jax.ShapeDtypeStruct((n, d), F32),
        compiler_params=_params(("arbitrary",)),
    )(dest, ys, x1, gate, g.reshape(1, d), b.reshape(1, d))


def moe_routing(experts):
    n, top_k = experts.shape
    n_exp = N_GROUPS * EXPERTS_PER_GROUP
    a = n * top_k
    flat_e = experts.reshape(a)
    onehot = (flat_e[:, None] == jnp.arange(n_exp, dtype=jnp.int32)[None, :]).astype(jnp.int32)
    rank = jnp.take_along_axis(jnp.cumsum(onehot, axis=0) - onehot, flat_e[:, None], axis=1)[:, 0]
    counts = jnp.sum(onehot, axis=0)
    padded = (counts + EXPERT_BLOCK - 1) // EXPERT_BLOCK * EXPERT_BLOCK
    pad_end = jnp.cumsum(padded)
    dest = (pad_end - padded)[flat_e] + rank
    n_blocks = -(-a // EXPERT_BLOCK) + n_exp
    row_tok = jnp.zeros((n_blocks * EXPERT_BLOCK,), jnp.int32).at[dest].set(jnp.arange(a, dtype=jnp.int32) // top_k)
    blk_start = jnp.arange(n_blocks, dtype=jnp.int32) * EXPERT_BLOCK
    blk_e = jnp.minimum(jnp.sum((pad_end[None, :] <= blk_start[:, None]).astype(jnp.int32), axis=1), n_exp - 1)
    n_used = (pad_end[-1:] // EXPERT_BLOCK).astype(jnp.int32)
    return row_tok, dest.astype(jnp.int32), blk_e, n_used


def hier_moe_ln(x1, eidx, gate, w, tm, alpha):
    row_tok, dest, blk_e, n_used = moe_routing(eidx[:, :2])
    ys = moe_experts(x1, row_tok, blk_e, n_used, w['e_gate'], w['e_up'], w['e_down'])
    dest = dest.reshape(-1, tm, 2).transpose(0, 2, 1).reshape(-1)
    return moe_combine(ys, dest, x1, gate, w['ln2_g'], w['ln2_b'], tm, alpha)


def kernel(x_prompt, x_sample, mem_prompt, state_wkv, state_shift, cache_win_k, cache_win_v, cache_mem_k, cache_mem_v, ln_in_g, ln_in_b, w_in, mu, w0, w_up, a0, a_up, g_up, k_k, k_a, r_k, lnx_g, lnx_b, sinks, w_mem_kv, p_a, p_b, p_m, w_o, ln1_g, ln1_b, w_group, b_group, w_router, b_router, e_gate, e_up, e_down, ln2_g, ln2_b):
    depth = w_in.shape[0]
    assert depth == 1, "single-layer step"
    bsz, seq, d = x_prompt.shape
    dec = x_sample.shape[0]
    assert x_sample.shape[1] == 1
    c_shift = mu.shape[-1]
    c_a = w0.shape[-1]
    window, kv_w = cache_win_k.shape[2], cache_win_k.shape[3] * cache_win_k.shape[4]
    n_mem, mem_w = cache_mem_k.shape[2], cache_mem_k.shape[3] * cache_mem_k.shape[4]
    q_w = sinks.shape[-1] * HEAD
    qkvm_w = q_w + 2 * kv_w + mem_w
    alpha = (2.0 * depth) ** 0.25
    past_len = float(PAST_LEN)
    chunk = 64

    w_in_b = w_in[0].astype(BF16)
    w_parts = [w_in_b[:, :c_shift], w_in_b[:, c_shift:c_shift + qkvm_w], w_in_b[:, c_shift + qkvm_w:]]
    rp = dict(mu=mu[0], w0=w0[0], w_up=w_up[0], a0=a0[0], a_up=a_up[0], g_up=g_up[0], k_k=k_k[0], k_a=k_a[0],
              r_k=r_k[0].reshape(-1))
    n_route = N_GROUPS * (1 + EXPERTS_PER_GROUP)
    mw = dict(ln_in_g=ln_in_g.reshape(1, d), ln_in_b=ln_in_b.reshape(1, d), p_a=p_a[0].astype(BF16),
              p_b=p_b[0].astype(BF16), p_m=p_m[0].astype(BF16), w_o=w_o[0].astype(BF16),
              ln1_g=ln1_g[0].reshape(1, d), ln1_b=ln1_b[0].reshape(1, d),
              w_route=jnp.pad(jnp.concatenate([w_group[0], w_router[0]], axis=1), ((0, 0), (0, LANES - n_route))),
              b_route=jnp.pad(jnp.concatenate([b_group[0], b_router[0]]), (0, LANES - n_route)).reshape(1, LANES),
              e_gate=e_gate[0], e_up=e_up[0], e_down=e_down[0], ln2_g=ln2_g[0], ln2_b=ln2_b[0])

    xp = x_prompt.reshape(bsz * seq, d)
    zr, zq, zg = ln_proj(xp, ln_in_g, ln_in_b, w_parts, 256)
    zr3 = zr.reshape(bsz, seq, c_shift)
    prep = rwkv_prep(zr3, jnp.zeros((bsz, 1, c_shift), F32), rp, chunk, 256)
    o_a, wkv_p = wkv(prep, jnp.zeros((bsz, c_a // HEAD, HEAD, HEAD), F32), lnx_g[0], lnx_b[0], chunk)
    mkv = proj(mem_prompt.reshape(bsz * n_mem, d), w_mem_kv[0].astype(BF16), 256).reshape(bsz, n_mem, 2 * mem_w)
    mk_p, mv_p = mkv[..., :mem_w], mkv[..., mem_w:]
    tables = rope_tables(jnp.arange(seq, dtype=F32))
    o_b, o_m, k_rot = attn_prompt(zq.reshape(bsz, seq, qkvm_w), mk_p, mv_p, sinks[0], tables, window, q_w, kv_w)
    x1, eidx, gate = merge(xp, o_a.reshape(-1, c_a), o_b.reshape(-1, q_w), o_m.reshape(-1, mem_w), zg, mw, 256, alpha)
    y_prompt = hier_moe_ln(x1, eidx, gate, mw, 128, alpha).reshape(bsz, seq, d)
    shift_p = zr3[:, -1]
    kb_p = k_rot[:, -window:].reshape(bsz, window, H_KV, HEAD)
    vb_p = zq.reshape(bsz, seq, qkvm_w)[:, -window:, q_w + kv_w:q_w + 2 * kv_w].reshape(bsz, window, H_KV, HEAD)

    xs = x_sample.reshape(dec, d)
    zr_s, zq_s, zg_s = ln_proj(xs, ln_in_g, ln_in_b, w_parts, dec)
    ops_s = rwkv_prep(zr_s.reshape(1, dec, c_shift), state_shift[0].reshape(1, dec, c_shift), rp, 1, dec)
    o_a_s, wkv_s = wkv_step([a.reshape(dec, c_a) for a in ops_s], state_wkv[0], lnx_g[0], lnx_b[0], 8)
    tables_s = rope_tables(jnp.full((1,), past_len, F32))
    o_b_s, o_m_s, nk_s, nv_s = attn_step(
        zq_s, cache_win_k[0].reshape(dec, window, kv_w), cache_win_v[0].reshape(dec, window, kv_w),
        cache_mem_k[0].reshape(dec, n_mem, mem_w), cache_mem_v[0].reshape(dec, n_mem, mem_w),
        sinks[0], tables_s, q_w, kv_w, 8)
    x1_s, eidx_s, gate_s = merge(xs, o_a_s, o_b_s, o_m_s, zg_s, mw, dec, alpha)
    y_sample = hier_moe_ln(x1_s, eidx_s, gate_s, mw, dec, alpha).reshape(dec, 1, d)

    sd = state_wkv.dtype
    return (y_prompt, y_sample, wkv_p[None].astype(sd), wkv_s[None].astype(sd), shift_p[None], zr_s[None],
            kb_p[None], vb_p[None], nk_s.reshape(dec, window, H_KV, HEAD)[None],
            nv_s.reshape(dec, window, H_KV, HEAD)[None],
            mk_p.reshape(bsz, n_mem, -1, HEAD)[None], mv_p.reshape(bsz, n_mem, -1, HEAD)[None])
```

```python
import functools
import math

import jax
import jax.numpy as jnp
from jax import lax
from jax.experimental import pallas as pl
from jax.experimental.pallas import tpu as pltpu

F32 = jnp.float32
BF16 = jnp.bfloat16
SCAN_BATCH = 4

HEAD = 64
LANES = 128
H_KV = 2
ROT_HALF = 8
ROPE_THETA = 500000.0
PAST_LEN = 8192
N_GROUPS = 4
EXPERTS_PER_GROUP = 8
EXPERT_BLOCK = 128
LN_EPS = 1e-5
LNX_EPS = 64e-5
NEG_INF = -1e30
VMEM_LIMIT = 48 * 1024 * 1024


def _pieces(x, n):
    out = []
    for _ in range(n):
        p = x.astype(BF16)
        out.append(p)
        x = x - p.astype(F32)
    return out


def _mask_dot(x, mask, n=2):
    return sum(jnp.dot(p, mask, preferred_element_type=F32) for p in _pieces(x, n))


def _split3(x, axis, lhs):
    hi = x.astype(BF16).astype(F32)
    lo = x - hi
    return jnp.concatenate([hi, hi, lo] if lhs else [hi, lo, hi], axis=axis).astype(BF16)


def _dot3(a, b):
    return jnp.dot(_split3(a, 1, True), _split3(b, 0, False), preferred_element_type=F32)


def _dot3_t(a, b):
    return lax.dot_general(_split3(a, 1, True), _split3(b, 1, False), (((1,), (1,)), ((), ())),
                           preferred_element_type=F32)


def _bdot(a, b):
    return jnp.dot(a.astype(BF16), b.astype(BF16), preferred_element_type=F32)


def _bdot_t(a, b):
    return lax.dot_general(a.astype(BF16), b.astype(BF16), (((1,), (1,)), ((), ())), preferred_element_type=F32)


def _ln(x, g, b):
    mu = jnp.mean(x, axis=-1, keepdims=True)
    xc = x - mu
    var = jnp.mean(xc * xc, axis=-1, keepdims=True)
    return xc * lax.rsqrt(var + LN_EPS) * g + b


def _const_spec(shape):
    nd = len(shape)
    return pl.BlockSpec(shape, lambda *_: (0,) * nd)


def _params(sem):
    return pltpu.CompilerParams(dimension_semantics=sem, vmem_limit_bytes=VMEM_LIMIT)


def _ln_proj_kernel(x_ref, g_ref, b_ref, *refs):
    nw = len(refs) // 2
    xn = _ln(x_ref[...], g_ref[...], b_ref[...]).astype(BF16)
    for w_ref, o_ref in zip(refs[:nw], refs[nw:]):
        o_ref[...] = jnp.dot(xn, w_ref[...], preferred_element_type=F32)


def ln_proj(x, g, b, ws, tm):
    n, d = x.shape
    return pl.pallas_call(
        _ln_proj_kernel, name="ln_proj",
        grid=(n // tm,),
        in_specs=[pl.BlockSpec((tm, d), lambda i: (i, 0)), _const_spec((1, d)), _const_spec((1, d))]
        + [_const_spec(w.shape) for w in ws],
        out_specs=[pl.BlockSpec((tm, w.shape[1]), lambda i: (i, 0)) for w in ws],
        out_shape=[jax.ShapeDtypeStruct((n, w.shape[1]), F32) for w in ws],
        compiler_params=_params(("parallel",)),
    )(x, g.reshape(1, d), b.reshape(1, d), *ws)


def _rwkv_prep_kernel(chunk, z_ref, prev_ref, mu_ref, w0_ref, wup_ref, a0_ref, aup_ref, gup_ref,
                      kk_ref, ka_ref, rk_ref, hsum_ref, tril_ref, *refs):
    out_refs, carry_ref = refs[:-1], refs[-1]
    z = z_ref[0]
    tt = z.shape[0]
    c_a = w0_ref.shape[-1]
    r_w, r_a, r_g = wup_ref.shape[0], aup_ref.shape[0], gup_ref.shape[0]
    if chunk == 1:
        prev = prev_ref[0]
    else:
        @pl.when(pl.program_id(1) == 0)
        def _():
            carry_ref[...] = prev_ref[0]

        row = lax.broadcasted_iota(jnp.int32, z.shape, 0)
        prev = jnp.where(row == 0, carry_ref[...], pltpu.roll(z, 1, 0))
        carry_ref[...] = z[tt - 1:tt, :]
    zs = z + (prev - z) * mu_ref[...]
    r = zs[:, :c_a]
    k = zs[:, c_a:2 * c_a]
    v = zs[:, 2 * c_a:3 * c_a]
    o = 3 * c_a
    xw = zs[:, o:o + r_w]
    xa = zs[:, o + r_w:o + r_w + r_a]
    xg = zs[:, o + r_w + r_a:o + r_w + r_a + r_g]
    warg = -(w0_ref[...] + _dot3(jnp.tanh(xw), wup_ref[...]))
    softplus = jnp.maximum(warg, 0.0) + jnp.log1p(jnp.exp(-jnp.abs(warg)))
    lw = -jnp.exp(-softplus - 0.5)
    a = jax.nn.sigmoid(a0_ref[...] + _dot3(xa, aup_ref[...]))
    g = _dot3(jax.nn.sigmoid(xg), gup_ref[...])
    kkr = k * kk_ref[...]
    kk = kkr / jnp.maximum(jnp.sqrt(_mask_dot(kkr * kkr, hsum_ref[...])), 1e-12)
    k2 = k * (1.0 + (a - 1.0) * ka_ref[...])
    bonus = _mask_dot(r * k2 * rk_ref[...], hsum_ref[...]) * v
    kb = kk * a
    if chunk == 1:
        outs = (r, jnp.exp(lw), k2, v, -kk, kb, g, bonus)
        for ref, val in zip(out_refs, outs):
            ref[0] = val
        return
    pieces = _pieces(lw, 3)
    ones3 = jnp.ones((chunk, 3 * chunk), BF16)
    cw, cwl = [], []
    for c in range(tt // chunk):
        stack = jnp.concatenate([p[c * chunk:(c + 1) * chunk] for p in pieces], axis=0)
        cw.append(jnp.dot(tril_ref[...], stack, preferred_element_type=F32))
        cwl.append(jnp.dot(ones3, stack, preferred_element_type=F32))
    cw = jnp.concatenate(cw, axis=0)
    cwl = jnp.concatenate(cwl, axis=0)
    e_inv = jnp.exp(-cw)
    e_hat = jnp.exp(cwl - cw)
    outs = (r * jnp.exp(cw), -kk * jnp.exp(cw - lw), kb * e_inv, k2 * e_inv, kb * e_hat, k2 * e_hat, v, g, bonus)
    for ref, val in zip(out_refs[:-1], outs):
        ref[0] = val
    wl = jnp.exp(cwl)
    for c in range(tt // chunk):
        out_refs[-1][0, c] = wl[c * chunk:c * chunk + 1, :]


def _head_sum_matrix(width):
    idx = jnp.arange(width)
    return ((idx[:, None] // HEAD) == (idx[None, :] // HEAD)).astype(BF16)


def rwkv_prep(z, prev, p, chunk, tt):
    bsz, t, cs = z.shape
    c_a = p['w0'].shape[-1]
    assert prev.shape[1] == (t if chunk == 1 else 1)
    ridx = jnp.arange(chunk)
    tril = jnp.tile((ridx[None, :] <= ridx[:, None]).astype(BF16), (1, 3))
    hsum = _head_sum_matrix(c_a)
    row = lambda x: x.reshape(1, -1)
    tile = pl.BlockSpec((1, tt, c_a), lambda b, i: (b, i, 0))
    full = jax.ShapeDtypeStruct((bsz, t, c_a), F32)
    if chunk == 1:
        out_specs, out_shape = [tile] * 8, [full] * 8
        prev_spec = pl.BlockSpec((1, tt, cs), lambda b, i: (b, i, 0))
    else:
        out_specs = [tile] * 9 + [pl.BlockSpec((1, tt // chunk, 1, c_a), lambda b, i: (b, i, 0, 0))]
        out_shape = [full] * 9 + [jax.ShapeDtypeStruct((bsz, t // chunk, 1, c_a), F32)]
        prev_spec = pl.BlockSpec((1, 1, cs), lambda b, i: (b, 0, 0))
    consts = [row(p['mu']), row(p['w0']), p['w_up'], row(p['a0']), p['a_up'], p['g_up'], row(p['k_k']),
              row(p['k_a']), row(p['r_k']), hsum, tril]
    return pl.pallas_call(
        functools.partial(_rwkv_prep_kernel, chunk), name="rwkv_prep",
        grid=(bsz, t // tt),
        in_specs=[pl.BlockSpec((1, tt, cs), lambda b, i: (b, i, 0)), prev_spec] + [_const_spec(c.shape) for c in consts],
        out_specs=out_specs,
        out_shape=out_shape,
        scratch_shapes=[pltpu.VMEM((1, cs), F32)],
        compiler_params=_params(("parallel", "arbitrary")),
    )(z, prev, *consts)


def _wkv_chunk_kernel(ra_ref, at_ref, bt_ref, kt_ref, bh_ref, kh_ref, v_ref, wl_ref, rp_ref, y0_ref, m_ref, n_ref):
    length = ra_ref.shape[1]
    heads = ra_ref.shape[2] // HEAD
    row = lax.broadcasted_iota(jnp.int32, (length, length), 0)
    col = lax.broadcasted_iota(jnp.int32, (length, length), 1)
    strict = row > col
    incl = row >= col
    hrow = lax.broadcasted_iota(jnp.int32, (HEAD, HEAD), 0)
    hcol = lax.broadcasted_iota(jnp.int32, (HEAD, HEAD), 1)
    hs = range(heads)
    sls = [slice(h * HEAD, (h + 1) * HEAD) for h in hs]
    gram = [_bdot_t(jnp.concatenate([at_ref[0, :, sl], ra_ref[0, :, sl]], axis=0),
                    jnp.concatenate([bt_ref[0, :, sl], kt_ref[0, :, sl]], axis=0)) for sl in sls]
    a_ab = [jnp.where(strict, g[:length, :length], 0.0) for g in gram]
    a_ak = [jnp.where(strict, g[:length, length:], 0.0) for g in gram]
    a_rb = [jnp.where(incl, g[length:, :length], 0.0) for g in gram]
    a_rk = [jnp.where(incl, g[length:, length:], 0.0) for g in gram]
    inv = [jnp.where(row == col, 1.0, a) for a in a_ab]
    pw = [_bdot(a, a) for a in a_ab]
    av = [_dot3(a_ak[h], v_ref[0, :, sls[h]]) for h in hs]
    y0 = [_dot3(a_rk[h], v_ref[0, :, sls[h]]) for h in hs]
    nk = [_dot3(v_ref[0, :, sl].T, kh_ref[0, :, sl]) for sl in sls]
    for _ in range(int(math.log2(length)) - 2):
        both = [_bdot(jnp.concatenate([pw[h], inv[h]], axis=0), pw[h]) for h in hs]
        pw = [b[:length] for b in both]
        inv = [inv[h] + both[h][length:] for h in hs]
    inv = [inv[h] + _bdot(inv[h], pw[h]) for h in hs]
    pq = [_dot3(inv[h], jnp.concatenate([at_ref[0, :, sls[h]], av[h]], axis=1)) for h in hs]
    ry = [_dot3(a_rb[h], pq[h]) for h in hs]
    mn = [_dot3(pq[h].T, bh_ref[0, :, sls[h]]) for h in hs]
    rp_ref[0] = jnp.concatenate([ra_ref[0, :, sls[h]] + ry[h][:, :HEAD] for h in hs], axis=1)
    y0_ref[0] = jnp.concatenate([y0[h] + ry[h][:, HEAD:] for h in hs], axis=1)
    m_ref[0, 0] = jnp.concatenate(
        [jnp.where(hrow == hcol, wl_ref[0, 0, :, sls[h]], 0.0) + mn[h][:HEAD] for h in hs], axis=1)
    n_ref[0, 0] = jnp.concatenate([nk[h] + mn[h][HEAD:] for h in hs], axis=1)


def _head_norm_wide(y, hsum):
    yc = y - _mask_dot(y, hsum) * (1.0 / HEAD)
    return yc * lax.rsqrt(_mask_dot(yc * yc, hsum) * (1.0 / HEAD) + LNX_EPS)


def _wkv_scan_kernel(rp_ref, y0_ref, m_ref, n_ref, g_ref, bonus_ref, s0_ref, lg_ref, lb_ref, hsum_ref,
                     o_ref, sout_ref, s_ref):
    c = pl.program_id(1)
    nb = rp_ref.shape[0]
    heads = rp_ref.shape[2] // HEAD

    @pl.when(c == 0)
    def _():
        s_ref[...] = s0_ref[...]

    pairs = [(b, h, slice(h * HEAD, (h + 1) * HEAD)) for b in range(nb) for h in range(heads)]
    s = [s_ref[b, h] for b, h, _ in pairs]
    s_new = [n_ref[b, 0, :, sl] + _dot3(s[j], m_ref[b, 0, :, sl]) for j, (b, h, sl) in enumerate(pairs)]
    ys = [y0_ref[b, :, sl] + _dot3_t(rp_ref[b, :, sl], s[j]) for j, (b, h, sl) in enumerate(pairs)]
    for j, (b, h, _) in enumerate(pairs):
        s_ref[b, h] = s_new[j]
    for b in range(nb):
        y = _head_norm_wide(jnp.concatenate(ys[b * heads:(b + 1) * heads], axis=-1), hsum_ref[...])
        o_ref[b] = (y * lg_ref[...] + lb_ref[...] + bonus_ref[b]) * g_ref[b]

    @pl.when(c == pl.num_programs(1) - 1)
    def _():
        sout_ref[...] = s_ref[...]


def wkv(prep, s0, lnx_g, lnx_b, chunk):
    ra, at, bt, kt, bh, kh, v, g, bonus, wl = prep
    bsz, t, c_a = ra.shape
    heads = c_a // HEAD
    n_chunks = t // chunk
    assert chunk & (chunk - 1) == 0 and chunk >= 4 and t % chunk == 0
    tile = pl.BlockSpec((1, chunk, c_a), lambda b, c: (b, c, 0))
    mat = pl.BlockSpec((1, 1, HEAD, c_a), lambda b, c: (b, c, 0, 0))
    st = pl.BlockSpec((1, heads, HEAD, HEAD), lambda b, c: (b, 0, 0, 0))
    full = jax.ShapeDtypeStruct((bsz, t, c_a), F32)
    mats = jax.ShapeDtypeStruct((bsz, n_chunks, HEAD, c_a), F32)
    rp, y0, m, n = pl.pallas_call(
        _wkv_chunk_kernel, name="wkv_chunk",
        grid=(bsz, n_chunks),
        in_specs=[tile] * 7 + [pl.BlockSpec((1, 1, 1, c_a), lambda b, c: (b, c, 0, 0))],
        out_specs=[tile, tile, mat, mat],
        out_shape=[full, full, mats, mats],
        compiler_params=_params(("parallel", "parallel")),
    )(ra, at, bt, kt, bh, kh, v, wl)
    nb = SCAN_BATCH if bsz % SCAN_BATCH == 0 else 1
    tile = pl.BlockSpec((nb, chunk, c_a), lambda b, c: (b, c, 0))
    mat = pl.BlockSpec((nb, 1, HEAD, c_a), lambda b, c: (b, c, 0, 0))
    st = pl.BlockSpec((nb, heads, HEAD, HEAD), lambda b, c: (b, 0, 0, 0))
    return pl.pallas_call(
        _wkv_scan_kernel, name="wkv_scan",
        grid=(bsz // nb, n_chunks),
        in_specs=[tile, tile, mat, mat, tile, tile, st, _const_spec((1, c_a)), _const_spec((1, c_a)),
                  _const_spec((c_a, c_a))],
        out_specs=[tile, st],
        out_shape=[full, jax.ShapeDtypeStruct(s0.shape, F32)],
        scratch_shapes=[pltpu.VMEM((nb, heads, HEAD, HEAD), F32)],
        compiler_params=_params(("parallel", "arbitrary")),
    )(rp, y0, m, n, g, bonus, s0, lnx_g.reshape(1, c_a), lnx_b.reshape(1, c_a), _head_sum_matrix(c_a))


def _wkv_step_kernel(r_ref, w_ref, k_ref, v_ref, a_ref, b_ref, g_ref, bonus_ref, s_ref, lg_ref, lb_ref, hsum_ref,
                     o_ref, sout_ref):
    bt, heads = s_ref.shape[0], s_ref.shape[1]
    c_a = heads * HEAD
    hsum = hsum_ref[...]
    diag = (lax.broadcasted_iota(jnp.int32, (HEAD, c_a), 1) % HEAD
            == lax.broadcasted_iota(jnp.int32, (HEAD, c_a), 0))
    seqs = range(bt)
    row = lambda ref, i: ref[i:i + 1, :]
    s = [jnp.concatenate([s_ref[i, h] for h in range(heads)], axis=1) for i in seqs]
    sa = [_mask_dot(s[i] * row(a_ref, i), hsum) for i in seqs]
    v_rows = [_mask_dot(jnp.where(diag, row(v_ref, i), 0.0), hsum) for i in seqs]
    s = [s[i] * row(w_ref, i) + sa[i] * row(b_ref, i) + v_rows[i] * row(k_ref, i) for i in seqs]
    for i in seqs:
        for h in range(heads):
            sout_ref[i, h] = s[i][:, h * HEAD:(h + 1) * HEAD]
    y_rows = [_mask_dot(s[i] * row(r_ref, i), hsum) for i in seqs]
    y = jnp.concatenate([jnp.sum(jnp.where(diag, y_rows[i], 0.0), axis=0, keepdims=True) for i in seqs], axis=0)
    y = _head_norm_wide(y, hsum)
    o_ref[...] = (y * lg_ref[...] + lb_ref[...] + bonus_ref[...]) * g_ref[...]


def wkv_step(ops, s0, lnx_g, lnx_b, bt):
    bsz, c_a = ops[0].shape
    rows = pl.BlockSpec((bt, c_a), lambda i: (i, 0))
    st = pl.BlockSpec((bt,) + s0.shape[1:], lambda i: (i, 0, 0, 0))
    hsum = _head_sum_matrix(c_a)
    return pl.pallas_call(
        _wkv_step_kernel, name="wkv_step",
        grid=(bsz // bt,),
        in_specs=[rows] * 8 + [st, _const_spec((1, c_a)), _const_spec((1, c_a)), _const_spec(hsum.shape)],
        out_specs=[rows, st],
        out_shape=[jax.ShapeDtypeStruct((bsz, c_a), F32), jax.ShapeDtypeStruct(s0.shape, F32)],
        compiler_params=_params(("parallel",)),
    )(*ops, s0, lnx_g.reshape(1, c_a), lnx_b.reshape(1, c_a), hsum)


def rope_tables(pos):
    inv_freq = ROPE_THETA ** (-jnp.arange(ROT_HALF, dtype=F32) / ROT_HALF)
    ang = pos[:, None] * inv_freq[None, :]
    cos, sin = jnp.cos(ang), jnp.sin(ang)
    t = pos.shape[0]
    rest = HEAD - 2 * ROT_HALF
    c = jnp.concatenate([cos, cos, jnp.ones((t, rest), F32)], axis=1)
    s1 = jnp.concatenate([jnp.zeros((t, ROT_HALF), F32), sin, jnp.zeros((t, rest), F32)], axis=1)
    s2 = jnp.concatenate([-sin, jnp.zeros((t, HEAD - ROT_HALF), F32)], axis=1)
    rep = LANES // HEAD
    return tuple(jnp.tile(x, (1, rep)) for x in (c, s1, s2))


def _rope(x, c, s1, s2):
    width = x.shape[-1]
    rep = width // c.shape[-1]
    if rep > 1:
        c, s1, s2 = (jnp.concatenate([t] * rep, axis=1) for t in (c, s1, s2))
    return x * c + pltpu.roll(x, ROT_HALF, 1) * s1 + pltpu.roll(x, width - ROT_HALF, 1) * s2


def _attn_kernel(window, q_w, kv_w, zq_ref, kp_ref, vp_ref, rc_ref, rs1_ref, rs2_ref, pc_ref, ps1_ref, ps2_ref,
                 mk_ref, mv_ref, sink_ref, ob_ref, om_ref, kr_ref):
    n = pl.program_id(1)
    zq = zq_ref[0]
    scale = HEAD ** -0.5
    q = zq[:, :q_w]
    k = zq[:, q_w:q_w + kv_w]
    v = zq[:, q_w + kv_w:q_w + 2 * kv_w]
    qm = zq[:, q_w + 2 * kv_w:]
    qr = _rope(q, rc_ref[...], rs1_ref[...], rs2_ref[...]) * scale
    kr = _rope(k, rc_ref[...], rs1_ref[...], rs2_ref[...])
    kr_ref[0] = kr
    kpr = _rope(kp_ref[0], pc_ref[...], ps1_ref[...], ps2_ref[...])
    vp = vp_ref[0]
    gqa = q_w // kv_w
    assert window & (window - 1) == 0
    qi = lax.broadcasted_iota(jnp.int32, (gqa * window, 2 * window), 0) & (window - 1)
    kj = lax.broadcasted_iota(jnp.int32, (gqa * window, 2 * window), 1)
    valid = (kj > qi) & (kj <= qi + window) & ((kj >= window) | (n > 0))
    tdot = lambda a, b: lax.dot_general(a, b, (((1,), (1,)), ((), ())), preferred_element_type=F32)
    dot = lambda a, b: jnp.dot(a, b, preferred_element_type=F32)
    kvs = range(kv_w // HEAD)
    mhs = range(qm.shape[1] // HEAD)
    hsl = lambda h: slice(h * HEAD, (h + 1) * HEAD)
    qmb = (qm * scale).astype(BF16)
    mk = mk_ref[0].astype(BF16)
    mv = mv_ref[0].astype(BF16)
    kcat = [jnp.concatenate([kpr[:, hsl(hk)], kr[:, hsl(hk)]], axis=0).astype(BF16) for hk in kvs]
    vcat = [jnp.concatenate([vp[:, hsl(hk)], v[:, hsl(hk)]], axis=0).astype(BF16) for hk in kvs]
    qs = [jnp.concatenate([qr[:, hsl(hk * gqa + g)] for g in range(gqa)], axis=0).astype(BF16) for hk in kvs]
    s = [jnp.where(valid, tdot(qs[hk], kcat[hk]), NEG_INF) for hk in kvs]
    sm = [tdot(qmb[:, hsl(h)], mk[:, hsl(h)]) for h in mhs]
    sink = [jnp.concatenate([jnp.full((window, 1), sink_ref[hk * gqa + g], F32) for g in range(gqa)], axis=0)
            for hk in kvs]
    m = [jnp.maximum(jnp.max(s[hk], axis=-1, keepdims=True), sink[hk]) for hk in kvs]
    p = [jnp.exp(s[hk] - m[hk]) for hk in kvs]
    pm = [jnp.exp(sm[h] - jnp.max(sm[h], axis=-1, keepdims=True)) for h in mhs]
    o = [dot(p[hk].astype(BF16), vcat[hk])
         / (jnp.sum(p[hk], axis=-1, keepdims=True) + jnp.exp(sink[hk] - m[hk])) for hk in kvs]
    om = [dot(pm[h].astype(BF16), mv[:, hsl(h)]) / jnp.sum(pm[h], axis=-1, keepdims=True) for h in mhs]
    ob_ref[0] = jnp.concatenate([o[hk][g * window:(g + 1) * window] for hk in kvs for g in range(gqa)], axis=1)
    om_ref[0] = jnp.concatenate(om, axis=1)


def attn_prompt(zq, mk, mv, sinks, tables, window, q_w, kv_w):
    bsz, t, zw = zq.shape
    mem_w = zw - q_w - 2 * kv_w
    assert kv_w == LANES and q_w % kv_w == 0
    kcol, vcol = q_w // kv_w, q_w // kv_w + 1
    prev = lambda n: jnp.maximum(n - 1, 0)
    tab = pl.BlockSpec((window, LANES), lambda b, n: (n, 0))
    ptab = pl.BlockSpec((window, LANES), lambda b, n: (prev(n), 0))
    mem = pl.BlockSpec((1,) + mk.shape[1:], lambda b, n: (b, 0, 0))
    return pl.pallas_call(
        functools.partial(_attn_kernel, window, q_w, kv_w), name="attn",
        grid=(bsz, t // window),
        in_specs=[pl.BlockSpec((1, window, zw), lambda b, n: (b, n, 0)),
                  pl.BlockSpec((1, window, kv_w), lambda b, n: (b, prev(n), kcol)),
                  pl.BlockSpec((1, window, kv_w), lambda b, n: (b, prev(n), vcol)),
                  tab, tab, tab, ptab, ptab, ptab, mem, mem,
                  pl.BlockSpec(memory_space=pltpu.SMEM)],
        out_specs=[pl.BlockSpec((1, window, q_w), lambda b, n: (b, n, 0)),
                   pl.BlockSpec((1, window, mem_w), lambda b, n: (b, n, 0)),
                   pl.BlockSpec((1, window, kv_w), lambda b, n: (b, n, 0))],
        out_shape=[jax.ShapeDtypeStruct((bsz, t, q_w), F32), jax.ShapeDtypeStruct((bsz, t, mem_w), F32),
                   jax.ShapeDtypeStruct((bsz, t, kv_w), F32)],
        compiler_params=_params(("parallel", "parallel")),
    )(zq, zq, zq, *tables, *tables, mk, mv, sinks)


def _attn_step_kernel(q_w, kv_w, zq_ref, ck_ref, cv_ref, mk_ref, mv_ref, rc_ref, rs1_ref, rs2_ref, sink_ref,
                      ob_ref, om_ref, nk_ref, nv_ref):
    bt = zq_ref.shape[0]
    window = ck_ref.shape[1]
    mem_w = om_ref.shape[1]
    n_q, gqa, per_vreg = q_w // HEAD, q_w // kv_w, LANES // HEAD
    scale = HEAD ** -0.5
    zq = zq_ref[...]
    q = _rope(zq[:, :q_w], rc_ref[...], rs1_ref[...], rs2_ref[...]) * scale
    k_new = _rope(zq[:, q_w:q_w + kv_w], rc_ref[...], rs1_ref[...], rs2_ref[...])
    v_new = zq[:, q_w + kv_w:q_w + 2 * kv_w]
    qm = zq[:, q_w + 2 * kv_w:] * scale
    own = lambda w: (lax.broadcasted_iota(jnp.int32, (n_q, w), 1) // HEAD
                     == lax.broadcasted_iota(jnp.int32, (n_q, w), 0))
    own_q, own_m = own(q_w), own(mem_w)
    hrow = lax.broadcasted_iota(jnp.int32, (n_q, LANES), 0)
    hblk = lax.broadcasted_iota(jnp.int32, (n_q, LANES), 1) // HEAD
    swap = (hrow % per_vreg) != (hrow // gqa)
    keep = hblk == hrow % per_vreg
    key_ok = lax.broadcasted_iota(jnp.int32, (n_q, window), 1) >= 1
    wrow = lax.broadcasted_iota(jnp.int32, (window, kv_w), 0)
    sink = sink_ref[...]
    bs = range(bt)
    tdot = lambda a, b: lax.dot_general(a.astype(BF16), b.astype(BF16), (((1,), (1,)), ((), ())),
                                        preferred_element_type=F32)
    for b in bs:
        nk_ref[b] = jnp.where(wrow == window - 1, k_new[b:b + 1], pltpu.roll(ck_ref[b], window - 1, 0))
        nv_ref[b] = jnp.where(wrow == window - 1, v_new[b:b + 1], pltpu.roll(cv_ref[b], window - 1, 0))
    q8 = []
    for b in bs:
        rep = jnp.where(own_q, q[b:b + 1], 0.0)
        fold = sum(rep[:, c * LANES:(c + 1) * LANES] for c in range(q_w // LANES))
        q8.append(jnp.where(swap, pltpu.roll(fold, HEAD, 1), fold))
    qm8 = [jnp.where(own_m, qm[b:b + 1], 0.0) for b in bs]
    s = [jnp.where(key_ok, tdot(q8[b], ck_ref[b]), NEG_INF) for b in bs]
    sm = [tdot(qm8[b], mk_ref[b]) for b in bs]
    s_new = [jnp.sum(q8[b] * k_new[b:b + 1], axis=1, keepdims=True) for b in bs]
    m = [jnp.maximum(jnp.maximum(jnp.max(s[b], axis=1, keepdims=True), s_new[b]), sink) for b in bs]
    p = [jnp.exp(s[b] - m[b]) for b in bs]
    p_new = [jnp.exp(s_new[b] - m[b]) for b in bs]
    pm = [jnp.exp(sm[b] - jnp.max(sm[b], axis=1, keepdims=True)) for b in bs]
    o8 = [(_bdot(p[b], cv_ref[b]) + p_new[b] * v_new[b:b + 1])
          / (jnp.sum(p[b], axis=1, keepdims=True) + p_new[b] + jnp.exp(sink - m[b])) for b in bs]
    om8 = [_bdot(pm[b], mv_ref[b]) / jnp.sum(pm[b], axis=1, keepdims=True) for b in bs]
    ob, om = [], []
    for b in bs:
        o = jnp.where(swap, pltpu.roll(o8[b], HEAD, 1), o8[b])
        o = jnp.concatenate([jnp.where(keep, o, 0.0)] * (q_w // LANES), axis=1)
        ob.append(jnp.sum(jnp.where(own_q, o, 0.0), axis=0, keepdims=True))
        om.append(jnp.sum(jnp.where(own_m, om8[b], 0.0), axis=0, keepdims=True))
    ob_ref[...] = jnp.concatenate(ob, axis=0)
    om_ref[...] = jnp.concatenate(om, axis=0)


def attn_step(zq, ck, cv, mk, mv, sinks, tables, q_w, kv_w, bt):
    bsz, zw = zq.shape
    mem_w = zw - q_w - 2 * kv_w
    window, n_mem = ck.shape[1], mk.shape[1]
    n_q = q_w // HEAD
    assert kv_w == LANES and mem_w // HEAD <= n_q and n_q == 8
    rows = lambda w: pl.BlockSpec((bt, w), lambda i: (i, 0))
    cache = pl.BlockSpec((bt, window, kv_w), lambda i: (i, 0, 0))
    mem = pl.BlockSpec((bt, n_mem, mem_w), lambda i: (i, 0, 0))
    consts = list(tables) + [sinks.reshape(n_q, 1)]
    const_specs = [_const_spec(c.shape) for c in consts]
    return pl.pallas_call(
        functools.partial(_attn_step_kernel, q_w, kv_w), name="attn_step",
        grid=(bsz // bt,),
        in_specs=[rows(zw), cache, cache, mem, mem] + const_specs,
        out_specs=[rows(q_w), rows(mem_w), cache, cache],
        out_shape=[jax.ShapeDtypeStruct((bsz, q_w), F32), jax.ShapeDtypeStruct((bsz, mem_w), F32),
                   jax.ShapeDtypeStruct(ck.shape, F32), jax.ShapeDtypeStruct(cv.shape, F32)],
        compiler_params=_params(("parallel",)),
    )(zq, ck, cv, mk, mv, *consts)


def _proj_kernel(x_ref, w_ref, o_ref):
    o_ref[...] = jnp.dot(x_ref[...].astype(BF16), w_ref[...], preferred_element_type=F32)


def proj(x, w, tm):
    n, d = x.shape
    return pl.pallas_call(
        _proj_kernel, name="proj",
        grid=(n // tm,),
        in_specs=[pl.BlockSpec((tm, d), lambda i: (i, 0)), _const_spec(w.shape)],
        out_specs=pl.BlockSpec((tm, w.shape[1]), lambda i: (i, 0)),
        out_shape=jax.ShapeDtypeStruct((n, w.shape[1]), F32),
        compiler_params=_params(("parallel",)),
    )(x, w)


def _merge_kernel(alpha, n_tiles, x_ref, oa_ref, ob_ref, om_ref, gt_ref, lig_ref, lib_ref, pa_ref, pb_ref, pm_ref,
                  wo_ref, l1g_ref, l1b_ref, wr_ref, br_ref, *refs):
    x1_ref, eidx_ref, gate_ref = refs[-3:]

    @pl.when(pl.program_id(0) >= n_tiles)
    def _():
        for ref in (x1_ref, eidx_ref, gate_ref):
            ref[...] = jnp.zeros_like(ref)

    @pl.when(pl.program_id(0) < n_tiles)
    def _():
        _merge_tile(alpha, x_ref, oa_ref, ob_ref, om_ref, gt_ref, lig_ref, lib_ref, pa_ref, pb_ref, pm_ref, wo_ref,
                    l1g_ref, l1b_ref, wr_ref, br_ref, x1_ref, eidx_ref, gate_ref)


def _merge_tile(alpha, x_ref, oa_ref, ob_ref, om_ref, gt_ref, lig_ref, lib_ref, pa_ref, pb_ref, pm_ref, wo_ref,
                l1g_ref, l1b_ref, wr_ref, br_ref, x1_ref, eidx_ref, gate_ref):
    d = x_ref.shape[1]
    xn = _ln(x_ref[...], lig_ref[...], lib_ref[...])
    gts = jax.nn.sigmoid(gt_ref[...])
    merged = (gts[:, :d] * _bdot(oa_ref[...], pa_ref[...]) + gts[:, d:2 * d] * _bdot(ob_ref[...], pb_ref[...])
              + gts[:, 2 * d:] * _bdot(om_ref[...], pm_ref[...]))
    x1 = _ln(alpha * xn + _bdot(merged, wo_ref[...]), l1g_ref[...], l1b_ref[...])
    x1_ref[...] = x1
    x_hi, x_lo = _pieces(x1, 2)
    w_hi, w_lo = _pieces(wr_ref[...], 2)
    dot = lambda a, b: jnp.dot(a, b, preferred_element_type=F32)
    logits = dot(x_hi, w_hi) + dot(x_hi, w_lo) + dot(x_lo, w_hi) + br_ref[...]
    lane = lax.broadcasted_iota(jnp.int32, logits.shape, 1)
    lane_f = lane.astype(F32)
    first = lambda hit: jnp.min(jnp.where(hit, lane_f, float(LANES)), axis=-1, keepdims=True).astype(jnp.int32)
    gmask = lane < N_GROUPS
    gl = jnp.where(gmask, logits, NEG_INF)
    gmax = jnp.max(gl, axis=-1, keepdims=True)
    gidx = first(gl == gmax)
    g_w = 1.0 / jnp.sum(jnp.where(gmask, jnp.exp(gl - gmax), 0.0), axis=-1, keepdims=True)
    lo = N_GROUPS + gidx * EXPERTS_PER_GROUP
    el = jnp.where((lane >= lo) & (lane < lo + EXPERTS_PER_GROUP), logits, NEG_INF)
    v1 = jnp.max(el, axis=-1, keepdims=True)
    i1 = first(el == v1)
    el2 = jnp.where(lane == i1, NEG_INF, el)
    v2 = jnp.max(el2, axis=-1, keepdims=True)
    i2 = first(el2 == v2)
    e2 = jnp.exp(v2 - v1)
    gate1 = g_w / (1.0 + e2)
    eidx_ref[...] = jnp.where(lane == 0, i1 - N_GROUPS, jnp.where(lane == 1, i2 - N_GROUPS, 0))
    gate_ref[...] = jnp.where(lane == 0, gate1, jnp.where(lane == 1, gate1 * e2, 0.0))


def merge(x, oa, ob, om, gt, w, tm, alpha, n_total, row_offset=0, into=None):
    n, d = x.shape
    assert row_offset % tm == 0 and n % tm == 0 and n_total % tm == 0
    off = row_offset // tm
    into = list(into or [])
    n_tiles = n // tm
    steps = n_tiles if into else n_total // tm
    rows = lambda a: pl.BlockSpec((tm, a.shape[1]), lambda i: (jnp.minimum(i, n_tiles - 1), 0))
    out = lambda width: pl.BlockSpec((tm, width), lambda i: (i + off, 0))
    consts = [w['ln_in_g'], w['ln_in_b'], w['p_a'], w['p_b'], w['p_m'], w['w_o'], w['ln1_g'], w['ln1_b'],
              w['w_route'], w['b_route']]
    n_in = 5 + len(consts)
    return pl.pallas_call(
        functools.partial(_merge_kernel, alpha, n_tiles), name="merge",
        grid=(steps,),
        in_specs=[rows(a) for a in (x, oa, ob, om, gt)] + [_const_spec(c.shape) for c in consts]
        + [pl.BlockSpec(memory_space=pl.ANY)] * len(into),
        out_specs=[out(d), out(LANES), out(LANES)],
        out_shape=[jax.ShapeDtypeStruct((n_total, d), F32), jax.ShapeDtypeStruct((n_total, LANES), jnp.int32),
                   jax.ShapeDtypeStruct((n_total, LANES), F32)],
        input_output_aliases={n_in + k: k for k in range(len(into))},
        compiler_params=_params(("parallel",)),
    )(x, oa, ob, om, gt, *consts, *into)


ROW_DMA_UNROLL = 8
DRAIN_STEPS = 2


def _row_copies(asg_ref, base, count, n_asg, x_hbm, buf, y_hbm, sem, gather, unrolled):
    def one(r, priority):
        a = asg_ref[base + r]
        if gather:
            tok = jnp.minimum(a, n_asg - 1) >> 1
            copy = pltpu.make_async_copy(x_hbm.at[pl.ds(tok, 1)], buf.at[pl.ds(r, 1)], sem)
        else:
            copy = pltpu.make_async_copy(buf.at[pl.ds(r, 1)], y_hbm.at[pl.ds(a, 1)], sem)
        copy.start(priority=priority)

    if unrolled:
        for r in range(count):
            one(r, r % 2)
        return

    def body(g, carry):
        for j in range(ROW_DMA_UNROLL):
            one(g * ROW_DMA_UNROLL + j, j % 2)
        return carry
    lax.fori_loop(0, count // ROW_DMA_UNROLL, body, 0)


def _moe_expert_kernel(n_asg, asg_ref, be_ref, nu_ref, x_hbm, wg_ref, wu_ref, wd_ref, y_hbm,
                       xbuf, ybuf, wgb, wub, wdb, gsem, ssem):
    i = pl.program_id(0)
    used = nu_ref[0]
    rows = xbuf.shape[1]
    n_blocks = be_ref.shape[0]
    blk = jnp.minimum(i, n_blocks - 1)

    @pl.when((i == 0) | (be_ref[blk] != be_ref[jnp.maximum(blk - 1, 0)]))
    def _():
        wgb[...] = wg_ref[0, 0].astype(BF16)
        wub[...] = wu_ref[0, 0].astype(BF16)
        wdb[...] = wd_ref[0, 0].astype(BF16)

    def wait_gather(slot):
        pltpu.make_async_copy(x_hbm.at[pl.ds(0, rows)], xbuf.at[slot], gsem.at[slot]).wait()

    def wait_scatter(slot):
        pltpu.make_async_copy(ybuf.at[slot], y_hbm.at[pl.ds(0, rows)], ssem.at[slot]).wait()

    def gather(b, slot, unrolled):
        _row_copies(asg_ref, b * rows, rows, n_asg, x_hbm, xbuf.at[slot], y_hbm, gsem.at[slot], True, unrolled)

    def scatter(b, slot, unrolled):
        _row_copies(asg_ref, b * rows, rows, n_asg, x_hbm, ybuf.at[slot], y_hbm, ssem.at[slot], False, unrolled)

    def expert(slot):
        xb = xbuf[slot].astype(BF16)
        hg = jnp.dot(xb, wgb[...], preferred_element_type=F32)
        hu = jnp.dot(xb, wub[...], preferred_element_type=F32)
        h = hg * jax.nn.sigmoid(hg) * hu
        ybuf[slot] = jnp.dot(h.astype(BF16), wdb[...], preferred_element_type=F32)

    @pl.when(i == 0)
    def _():
        ybuf[1] = jnp.zeros(ybuf.shape[1:], F32)
        spare = [pltpu.make_async_copy(ybuf.at[1], y_hbm.at[pl.ds(n_asg + j * rows, rows)], ssem.at[1])
                 for j in range((y_hbm.shape[0] - n_asg) // rows)]
        for copy in spare:
            copy.start()
        for copy in spare:
            copy.wait()

    @pl.when((i >= 2) & (i - 2 < used))
    def _():
        wait_scatter(i % 2)

    steady = (i >= 1) & (i + 1 < used)
    for slot in range(2):
        @pl.when(steady & (i % 2 == slot))
        def _():
            wait_gather(slot)
            gather(i + 1, 1 - slot, True)
            scatter(i - 1, 1 - slot, True)
            expert(slot)

    @pl.when(jnp.logical_not(steady))
    def _():
        slot = i % 2

        @pl.when((i == 0) & (used > 0))
        def _():
            gather(0, 0, False)

        @pl.when(i < used)
        def _():
            wait_gather(slot)

        @pl.when(i + 1 < used)
        def _():
            gather(i + 1, 1 - slot, False)

        @pl.when((i >= 1) & (i - 1 < used))
        def _():
            scatter(i - 1, 1 - slot, False)

        @pl.when(i < used)
        def _():
            expert(slot)


def moe_experts(x1, n_asg, asg, blk_e, n_used, e_gate, e_up, e_down):
    d = x1.shape[1]
    n_blocks = blk_e.shape[0]
    ff = e_gate.shape[-1]
    n_rows = n_asg + e_gate.shape[1] * EXPERT_BLOCK
    weight = lambda shape: pl.BlockSpec(
        (1, 1) + shape, lambda i, asg, be, nu: (0, be[jnp.minimum(i, n_blocks - 1)], 0, 0))
    return pl.pallas_call(
        functools.partial(_moe_expert_kernel, n_asg), name="moe_expert",
        grid_spec=pltpu.PrefetchScalarGridSpec(
            num_scalar_prefetch=3,
            grid=(n_blocks + DRAIN_STEPS,),
            in_specs=[pl.BlockSpec(memory_space=pl.ANY), weight((d, ff)), weight((d, ff)), weight((ff, d))],
            out_specs=pl.BlockSpec(memory_space=pl.ANY),
            scratch_shapes=[pltpu.VMEM((2, EXPERT_BLOCK, d), F32), pltpu.VMEM((2, EXPERT_BLOCK, d), F32),
                            pltpu.VMEM((d, ff), BF16), pltpu.VMEM((d, ff), BF16), pltpu.VMEM((ff, d), BF16),
                            pltpu.SemaphoreType.DMA((2,)), pltpu.SemaphoreType.DMA((2,))]),
        out_shape=jax.ShapeDtypeStruct((n_rows, d), F32),
        compiler_params=_params(("arbitrary",)),
    )(asg, blk_e, n_used, x1, e_gate, e_up, e_down)


def _moe_combine_kernel(alpha, lead_tiles, y_ref, x1_ref, gate_ref, g_ref, b_ref, lead_ref, tail_ref):
    i = pl.program_id(0)
    d = x1_ref.shape[1]
    gate = gate_ref[...]
    moe = gate[:, 0:1] * y_ref[:, :d] + gate[:, 1:2] * y_ref[:, d:]
    out = _ln(alpha * x1_ref[...] + moe, g_ref[...], b_ref[...])

    @pl.when(i < lead_tiles)
    def _():
        lead_ref[...] = out

    @pl.when(i >= lead_tiles)
    def _():
        tail_ref[...] = out


def moe_combine(y, x1, gate, g, b, tm, alpha, n, n_lead):
    d = x1.shape[1]
    assert y.shape[0] % 2 == 0 and n_lead % tm == 0 and (n - n_lead) % tm == 0 and n > n_lead
    lead_tiles = n_lead // tm
    y2 = y.reshape(y.shape[0] // 2, 2 * d)
    return pl.pallas_call(
        functools.partial(_moe_combine_kernel, alpha, lead_tiles), name="moe_combine",
        grid=(n // tm,),
        in_specs=[pl.BlockSpec((tm, 2 * d), lambda i: (i, 0)), pl.BlockSpec((tm, d), lambda i: (i, 0)),
                  pl.BlockSpec((tm, LANES), lambda i: (i, 0)), _const_spec((1, d)), _const_spec((1, d))],
        out_specs=[pl.BlockSpec((tm, d), lambda i: (jnp.minimum(i, lead_tiles - 1), 0)),
                   pl.BlockSpec((tm, d), lambda i: (jnp.maximum(i - lead_tiles, 0), 0))],
        out_shape=[jax.ShapeDtypeStruct((n_lead, d), F32), jax.ShapeDtypeStruct((n - n_lead, d), F32)],
        compiler_params=_params(("arbitrary",)),
    )(y2, x1, gate, g.reshape(1, d), b.reshape(1, d))


def moe_routing(experts):
    n, top_k = experts.shape
    n_exp = N_GROUPS * EXPERTS_PER_GROUP
    a = n * top_k
    flat_e = experts.reshape(a)
    onehot = (flat_e[:, None] == jnp.arange(n_exp, dtype=jnp.int32)[None, :]).astype(jnp.int32)
    rank = jnp.take_along_axis(jnp.cumsum(onehot, axis=0) - onehot, flat_e[:, None], axis=1)[:, 0]
    counts = jnp.sum(onehot, axis=0)
    padded = (counts + EXPERT_BLOCK - 1) // EXPERT_BLOCK * EXPERT_BLOCK
    pad_end = jnp.cumsum(padded)
    dest = (pad_end - padded)[flat_e] + rank
    n_blocks = -(-a // EXPERT_BLOCK) + n_exp
    blk_start = jnp.arange(n_blocks, dtype=jnp.int32) * EXPERT_BLOCK
    blk_e = jnp.minimum(jnp.sum((pad_end[None, :] <= blk_start[:, None]).astype(jnp.int32), axis=1), n_exp - 1)
    spare = jnp.arange(n_blocks * EXPERT_BLOCK, dtype=jnp.int32) - jnp.repeat(jnp.cumsum(counts)[blk_e], EXPERT_BLOCK)
    spare = a + jnp.clip(spare, 0, n_exp * EXPERT_BLOCK - 1)
    asg = spare.astype(jnp.int32).at[dest].set(jnp.arange(a, dtype=jnp.int32))
    n_used = (pad_end[-1:] // EXPERT_BLOCK).astype(jnp.int32)
    return asg, blk_e.astype(jnp.int32), n_used


def hier_moe_ln(x1, eidx, gate, w, tm, alpha, n_tokens, n_lead):
    asg, blk_e, n_used = moe_routing(eidx[:n_tokens, :2])
    y = moe_experts(x1, 2 * n_tokens, asg, blk_e, n_used, w['e_gate'], w['e_up'], w['e_down'])
    return moe_combine(y, x1, gate, w['ln2_g'], w['ln2_b'], tm, alpha, n_tokens, n_lead)


def kernel(x_prompt, x_sample, mem_prompt, state_wkv, state_shift, cache_win_k, cache_win_v, cache_mem_k, cache_mem_v, ln_in_g, ln_in_b, w_in, mu, w0, w_up, a0, a_up, g_up, k_k, k_a, r_k, lnx_g, lnx_b, sinks, w_mem_kv, p_a, p_b, p_m, w_o, ln1_g, ln1_b, w_group, b_group, w_router, b_router, e_gate, e_up, e_down, ln2_g, ln2_b):
    depth = w_in.shape[0]
    assert depth == 1, "single-layer step"
    bsz, seq, d = x_prompt.shape
    dec = x_sample.shape[0]
    assert x_sample.shape[1] == 1
    c_shift = mu.shape[-1]
    c_a = w0.shape[-1]
    window, kv_w = cache_win_k.shape[2], cache_win_k.shape[3] * cache_win_k.shape[4]
    n_mem, mem_w = cache_mem_k.shape[2], cache_mem_k.shape[3] * cache_mem_k.shape[4]
    q_w = sinks.shape[-1] * HEAD
    qkvm_w = q_w + 2 * kv_w + mem_w
    alpha = (2.0 * depth) ** 0.25
    past_len = float(PAST_LEN)
    chunk = 64

    w_in_b = w_in[0].astype(BF16)
    w_parts = [w_in_b[:, :c_shift], w_in_b[:, c_shift:c_shift + qkvm_w], w_in_b[:, c_shift + qkvm_w:]]
    rp = dict(mu=mu[0], w0=w0[0], w_up=w_up[0], a0=a0[0], a_up=a_up[0], g_up=g_up[0], k_k=k_k[0], k_a=k_a[0],
              r_k=r_k[0].reshape(-1))
    n_route = N_GROUPS * (1 + EXPERTS_PER_GROUP)
    mw = dict(ln_in_g=ln_in_g.reshape(1, d), ln_in_b=ln_in_b.reshape(1, d), p_a=p_a[0].astype(BF16),
              p_b=p_b[0].astype(BF16), p_m=p_m[0].astype(BF16), w_o=w_o[0].astype(BF16),
              ln1_g=ln1_g[0].reshape(1, d), ln1_b=ln1_b[0].reshape(1, d),
              w_route=jnp.pad(jnp.concatenate([w_group[0], w_router[0]], axis=1), ((0, 0), (0, LANES - n_route))),
              b_route=jnp.pad(jnp.concatenate([b_group[0], b_router[0]]), (0, LANES - n_route)).reshape(1, LANES),
              e_gate=e_gate, e_up=e_up, e_down=e_down, ln2_g=ln2_g[0], ln2_b=ln2_b[0])

    xp = x_prompt.reshape(bsz * seq, d)
    zr, zq, zg = ln_proj(xp, ln_in_g, ln_in_b, w_parts, 256)
    zr3 = zr.reshape(bsz, seq, c_shift)
    prep = rwkv_prep(zr3, jnp.zeros((bsz, 1, c_shift), F32), rp, chunk, 256)
    o_a, wkv_p = wkv(prep, jnp.zeros((bsz, c_a // HEAD, HEAD, HEAD), F32), lnx_g[0], lnx_b[0], chunk)
    mkv = proj(mem_prompt.reshape(bsz * n_mem, d), w_mem_kv[0].astype(BF16), 256).reshape(bsz, n_mem, 2 * mem_w)
    mk_p, mv_p = mkv[..., :mem_w], mkv[..., mem_w:]
    tables = rope_tables(jnp.arange(seq, dtype=F32))
    o_b, o_m, k_rot = attn_prompt(zq.reshape(bsz, seq, qkvm_w), mk_p, mv_p, sinks[0], tables, window, q_w, kv_w)
    n_all = bsz * seq + dec
    n_buf = -(-n_all // 256) * 256
    routed = merge(xp, o_a.reshape(-1, c_a), o_b.reshape(-1, q_w), o_m.reshape(-1, mem_w), zg, mw, 256, alpha, n_buf)
    shift_p = zr3[:, -1]
    kb_p = k_rot[:, -window:].reshape(bsz, window, H_KV, HEAD)
    vb_p = zq.reshape(bsz, seq, qkvm_w)[:, -window:, q_w + kv_w:q_w + 2 * kv_w].reshape(bsz, window, H_KV, HEAD)

    xs = x_sample.reshape(dec, d)
    zr_s, zq_s, zg_s = ln_proj(xs, ln_in_g, ln_in_b, w_parts, dec)
    ops_s = rwkv_prep(zr_s.reshape(1, dec, c_shift), state_shift[0].reshape(1, dec, c_shift), rp, 1, dec)
    o_a_s, wkv_s = wkv_step([a.reshape(dec, c_a) for a in ops_s], state_wkv[0], lnx_g[0], lnx_b[0], 8)
    tables_s = rope_tables(jnp.full((1,), past_len, F32))
    o_b_s, o_m_s, nk_s, nv_s = attn_step(
        zq_s, cache_win_k[0].reshape(dec, window, kv_w), cache_win_v[0].reshape(dec, window, kv_w),
        cache_mem_k[0].reshape(dec, n_mem, mem_w), cache_mem_v[0].reshape(dec, n_mem, mem_w),
        sinks[0], tables_s, q_w, kv_w, 8)
    x1, eidx, gate = merge(xs, o_a_s, o_b_s, o_m_s, zg_s, mw, dec, alpha, n_buf, bsz * seq, routed)

    y_prompt, y_sample = hier_moe_ln(x1, eidx, gate, mw, dec, alpha, n_all, bsz * seq)
    y_prompt = y_prompt.reshape(bsz, seq, d)
    y_sample = y_sample.reshape(dec, 1, d)

    sd = state_wkv.dtype
    return (y_prompt, y_sample, wkv_p[None].astype(sd), wkv_s[None].astype(sd), shift_p[None], zr_s[None],
            kb_p[None], vb_p[None], nk_s.reshape(dec, window, H_KV, HEAD)[None],
            nv_s.reshape(dec, window, H_KV, HEAD)[None],
            mk_p.reshape(bsz, n_mem, -1, HEAD)[None], mv_p.reshape(bsz, n_mem, -1, HEAD)[None])
```

```python
import functools
import math

import jax
import jax.numpy as jnp
from jax import lax
from jax.experimental import pallas as pl
from jax.experimental.pallas import tpu as pltpu

F32 = jnp.float32
BF16 = jnp.bfloat16
SCAN_BATCH = 4

HEAD = 64
LANES = 128
H_KV = 2
ROT_HALF = 8
ROPE_THETA = 500000.0
PAST_LEN = 8192
N_GROUPS = 4
EXPERTS_PER_GROUP = 8
EXPERT_BLOCK = 128
LN_EPS = 1e-5
LNX_EPS = 64e-5
NEG_INF = -1e30
VMEM_LIMIT = 48 * 1024 * 1024


def _pieces(x, n):
    out = []
    for _ in range(n):
        p = x.astype(BF16)
        out.append(p)
        x = x - p.astype(F32)
    return out


def _mask_dot(x, mask, n=2):
    return sum(jnp.dot(p, mask, preferred_element_type=F32) for p in _pieces(x, n))


def _split3(x, axis, lhs):
    hi = x.astype(BF16).astype(F32)
    lo = x - hi
    return jnp.concatenate([hi, hi, lo] if lhs else [hi, lo, hi], axis=axis).astype(BF16)


def _dot3(a, b):
    return jnp.dot(_split3(a, 1, True), _split3(b, 0, False), preferred_element_type=F32)


def _dot3_t(a, b):
    return lax.dot_general(_split3(a, 1, True), _split3(b, 1, False), (((1,), (1,)), ((), ())),
                           preferred_element_type=F32)


def _bdot(a, b):
    return jnp.dot(a.astype(BF16), b.astype(BF16), preferred_element_type=F32)


def _bdot_t(a, b):
    return lax.dot_general(a.astype(BF16), b.astype(BF16), (((1,), (1,)), ((), ())), preferred_element_type=F32)


def _ln(x, g, b):
    mu = jnp.mean(x, axis=-1, keepdims=True)
    xc = x - mu
    var = jnp.mean(xc * xc, axis=-1, keepdims=True)
    return xc * lax.rsqrt(var + LN_EPS) * g + b


def _const_spec(shape):
    nd = len(shape)
    return pl.BlockSpec(shape, lambda *_: (0,) * nd)


def _params(sem):
    return pltpu.CompilerParams(dimension_semantics=sem, vmem_limit_bytes=VMEM_LIMIT)


def _ln_proj_kernel(x_ref, g_ref, b_ref, *refs):
    nw = len(refs) // 2
    xn = _ln(x_ref[...], g_ref[...], b_ref[...]).astype(BF16)
    for w_ref, o_ref in zip(refs[:nw], refs[nw:]):
        o_ref[...] = jnp.dot(xn, w_ref[...], preferred_element_type=F32)


def ln_proj(x, g, b, ws, tm):
    n, d = x.shape
    return pl.pallas_call(
        _ln_proj_kernel, name="ln_proj",
        grid=(n // tm,),
        in_specs=[pl.BlockSpec((tm, d), lambda i: (i, 0)), _const_spec((1, d)), _const_spec((1, d))]
        + [_const_spec(w.shape) for w in ws],
        out_specs=[pl.BlockSpec((tm, w.shape[1]), lambda i: (i, 0)) for w in ws],
        out_shape=[jax.ShapeDtypeStruct((n, w.shape[1]), F32) for w in ws],
        compiler_params=_params(("parallel",)),
    )(x, g.reshape(1, d), b.reshape(1, d), *ws)


def _rwkv_prep_kernel(chunk, z_ref, prev_ref, mu_ref, w0_ref, wup_ref, a0_ref, aup_ref, gup_ref,
                      kk_ref, ka_ref, rk_ref, hsum_ref, tril_ref, *refs):
    out_refs, carry_ref = refs[:-1], refs[-1]
    z = z_ref[0]
    tt = z.shape[0]
    c_a = w0_ref.shape[-1]
    r_w, r_a, r_g = wup_ref.shape[0], aup_ref.shape[0], gup_ref.shape[0]
    if chunk == 1:
        prev = prev_ref[0]
    else:
        @pl.when(pl.program_id(1) == 0)
        def _():
            carry_ref[...] = prev_ref[0]

        row = lax.broadcasted_iota(jnp.int32, z.shape, 0)
        prev = jnp.where(row == 0, carry_ref[...], pltpu.roll(z, 1, 0))
        carry_ref[...] = z[tt - 1:tt, :]
    zs = z + (prev - z) * mu_ref[...]
    r = zs[:, :c_a]
    k = zs[:, c_a:2 * c_a]
    v = zs[:, 2 * c_a:3 * c_a]
    o = 3 * c_a
    xw = zs[:, o:o + r_w]
    xa = zs[:, o + r_w:o + r_w + r_a]
    xg = zs[:, o + r_w + r_a:o + r_w + r_a + r_g]
    warg = -(w0_ref[...] + _dot3(jnp.tanh(xw), wup_ref[...]))
    softplus = jnp.maximum(warg, 0.0) + jnp.log1p(jnp.exp(-jnp.abs(warg)))
    lw = -jnp.exp(-softplus - 0.5)
    a = jax.nn.sigmoid(a0_ref[...] + _dot3(xa, aup_ref[...]))
    g = _dot3(jax.nn.sigmoid(xg), gup_ref[...])
    kkr = k * kk_ref[...]
    kk = kkr / jnp.maximum(jnp.sqrt(_mask_dot(kkr * kkr, hsum_ref[...])), 1e-12)
    k2 = k * (1.0 + (a - 1.0) * ka_ref[...])
    bonus = _mask_dot(r * k2 * rk_ref[...], hsum_ref[...]) * v
    kb = kk * a
    if chunk == 1:
        outs = (r, jnp.exp(lw), k2, v, -kk, kb, g, bonus)
        for ref, val in zip(out_refs, outs):
            ref[0] = val
        return
    pieces = _pieces(lw, 3)
    ones3 = jnp.ones((chunk, 3 * chunk), BF16)
    cw, cwl = [], []
    for c in range(tt // chunk):
        stack = jnp.concatenate([p[c * chunk:(c + 1) * chunk] for p in pieces], axis=0)
        cw.append(jnp.dot(tril_ref[...], stack, preferred_element_type=F32))
        cwl.append(jnp.dot(ones3, stack, preferred_element_type=F32))
    cw = jnp.concatenate(cw, axis=0)
    cwl = jnp.concatenate(cwl, axis=0)
    e_inv = jnp.exp(-cw)
    e_hat = jnp.exp(cwl - cw)
    outs = (r * jnp.exp(cw), -kk * jnp.exp(cw - lw), kb * e_inv, k2 * e_inv, kb * e_hat, k2 * e_hat, v, g, bonus)
    for ref, val in zip(out_refs[:-1], outs):
        ref[0] = val
    wl = jnp.exp(cwl)
    for c in range(tt // chunk):
        out_refs[-1][0, c] = wl[c * chunk:c * chunk + 1, :]


def _head_sum_matrix(width):
    idx = jnp.arange(width)
    return ((idx[:, None] // HEAD) == (idx[None, :] // HEAD)).astype(BF16)


def rwkv_prep(z, prev, p, chunk, tt):
    bsz, t, cs = z.shape
    c_a = p['w0'].shape[-1]
    assert prev.shape[1] == (t if chunk == 1 else 1)
    ridx = jnp.arange(chunk)
    tril = jnp.tile((ridx[None, :] <= ridx[:, None]).astype(BF16), (1, 3))
    hsum = _head_sum_matrix(c_a)
    row = lambda x: x.reshape(1, -1)
    tile = pl.BlockSpec((1, tt, c_a), lambda b, i: (b, i, 0))
    full = jax.ShapeDtypeStruct((bsz, t, c_a), F32)
    if chunk == 1:
        out_specs, out_shape = [tile] * 8, [full] * 8
        prev_spec = pl.BlockSpec((1, tt, cs), lambda b, i: (b, i, 0))
    else:
        out_specs = [tile] * 9 + [pl.BlockSpec((1, tt // chunk, 1, c_a), lambda b, i: (b, i, 0, 0))]
        out_shape = [full] * 9 + [jax.ShapeDtypeStruct((bsz, t // chunk, 1, c_a), F32)]
        prev_spec = pl.BlockSpec((1, 1, cs), lambda b, i: (b, 0, 0))
    consts = [row(p['mu']), row(p['w0']), p['w_up'], row(p['a0']), p['a_up'], p['g_up'], row(p['k_k']),
              row(p['k_a']), row(p['r_k']), hsum, tril]
    return pl.pallas_call(
        functools.partial(_rwkv_prep_kernel, chunk), name="rwkv_prep",
        grid=(bsz, t // tt),
        in_specs=[pl.BlockSpec((1, tt, cs), lambda b, i: (b, i, 0)), prev_spec] + [_const_spec(c.shape) for c in consts],
        out_specs=out_specs,
        out_shape=out_shape,
        scratch_shapes=[pltpu.VMEM((1, cs), F32)],
        compiler_params=_params(("parallel", "arbitrary")),
    )(z, prev, *consts)


def _wkv_chunk_kernel(ra_ref, at_ref, bt_ref, kt_ref, bh_ref, kh_ref, v_ref, wl_ref, rp_ref, y0_ref, m_ref, n_ref):
    length = ra_ref.shape[1]
    heads = ra_ref.shape[2] // HEAD
    row = lax.broadcasted_iota(jnp.int32, (length, length), 0)
    col = lax.broadcasted_iota(jnp.int32, (length, length), 1)
    strict = row > col
    incl = row >= col
    hrow = lax.broadcasted_iota(jnp.int32, (HEAD, HEAD), 0)
    hcol = lax.broadcasted_iota(jnp.int32, (HEAD, HEAD), 1)
    hs = range(heads)
    sls = [slice(h * HEAD, (h + 1) * HEAD) for h in hs]
    gram = [_bdot_t(jnp.concatenate([at_ref[0, :, sl], ra_ref[0, :, sl]], axis=0),
                    jnp.concatenate([bt_ref[0, :, sl], kt_ref[0, :, sl]], axis=0)) for sl in sls]
    a_ab = [jnp.where(strict, g[:length, :length], 0.0) for g in gram]
    a_ak = [jnp.where(strict, g[:length, length:], 0.0) for g in gram]
    a_rb = [jnp.where(incl, g[length:, :length], 0.0) for g in gram]
    a_rk = [jnp.where(incl, g[length:, length:], 0.0) for g in gram]
    inv = [jnp.where(row == col, 1.0, a) for a in a_ab]
    pw = [_bdot(a, a) for a in a_ab]
    av = [_dot3(a_ak[h], v_ref[0, :, sls[h]]) for h in hs]
    y0 = [_dot3(a_rk[h], v_ref[0, :, sls[h]]) for h in hs]
    nk = [_dot3(v_ref[0, :, sl].T, kh_ref[0, :, sl]) for sl in sls]
    for _ in range(int(math.log2(length)) - 2):
        both = [_bdot(jnp.concatenate([pw[h], inv[h]], axis=0), pw[h]) for h in hs]
        pw = [b[:length] for b in both]
        inv = [inv[h] + both[h][length:] for h in hs]
    inv = [inv[h] + _bdot(inv[h], pw[h]) for h in hs]
    pq = [_dot3(inv[h], jnp.concatenate([at_ref[0, :, sls[h]], av[h]], axis=1)) for h in hs]
    ry = [_dot3(a_rb[h], pq[h]) for h in hs]
    mn = [_dot3(pq[h].T, bh_ref[0, :, sls[h]]) for h in hs]
    rp_ref[0] = jnp.concatenate([ra_ref[0, :, sls[h]] + ry[h][:, :HEAD] for h in hs], axis=1)
    y0_ref[0] = jnp.concatenate([y0[h] + ry[h][:, HEAD:] for h in hs], axis=1)
    m_ref[0, 0] = jnp.concatenate(
        [jnp.where(hrow == hcol, wl_ref[0, 0, :, sls[h]], 0.0) + mn[h][:HEAD] for h in hs], axis=1)
    n_ref[0, 0] = jnp.concatenate([nk[h] + mn[h][HEAD:] for h in hs], axis=1)


def _head_norm_wide(y, hsum):
    yc = y - _mask_dot(y, hsum) * (1.0 / HEAD)
    return yc * lax.rsqrt(_mask_dot(yc * yc, hsum) * (1.0 / HEAD) + LNX_EPS)


def _wkv_scan_kernel(rp_ref, y0_ref, m_ref, n_ref, g_ref, bonus_ref, s0_ref, lg_ref, lb_ref, hsum_ref,
                     o_ref, sout_ref, s_ref):
    c = pl.program_id(1)
    nb = rp_ref.shape[0]
    heads = rp_ref.shape[2] // HEAD

    @pl.when(c == 0)
    def _():
        s_ref[...] = s0_ref[...]

    pairs = [(b, h, slice(h * HEAD, (h + 1) * HEAD)) for b in range(nb) for h in range(heads)]
    s = [s_ref[b, h] for b, h, _ in pairs]
    s_new = [n_ref[b, 0, :, sl] + _dot3(s[j], m_ref[b, 0, :, sl]) for j, (b, h, sl) in enumerate(pairs)]
    ys = [y0_ref[b, :, sl] + _dot3_t(rp_ref[b, :, sl], s[j]) for j, (b, h, sl) in enumerate(pairs)]
    for j, (b, h, _) in enumerate(pairs):
        s_ref[b, h] = s_new[j]
    for b in range(nb):
        y = _head_norm_wide(jnp.concatenate(ys[b * heads:(b + 1) * heads], axis=-1), hsum_ref[...])
        o_ref[b] = (y * lg_ref[...] + lb_ref[...] + bonus_ref[b]) * g_ref[b]

    @pl.when(c == pl.num_programs(1) - 1)
    def _():
        sout_ref[...] = s_ref[...]


def wkv(prep, s0, lnx_g, lnx_b, chunk):
    ra, at, bt, kt, bh, kh, v, g, bonus, wl = prep
    bsz, t, c_a = ra.shape
    heads = c_a // HEAD
    n_chunks = t // chunk
    assert chunk & (chunk - 1) == 0 and chunk >= 4 and t % chunk == 0
    tile = pl.BlockSpec((1, chunk, c_a), lambda b, c: (b, c, 0))
    mat = pl.BlockSpec((1, 1, HEAD, c_a), lambda b, c: (b, c, 0, 0))
    st = pl.BlockSpec((1, heads, HEAD, HEAD), lambda b, c: (b, 0, 0, 0))
    full = jax.ShapeDtypeStruct((bsz, t, c_a), F32)
    mats = jax.ShapeDtypeStruct((bsz, n_chunks, HEAD, c_a), F32)
    rp, y0, m, n = pl.pallas_call(
        _wkv_chunk_kernel, name="wkv_chunk",
        grid=(bsz, n_chunks),
        in_specs=[tile] * 7 + [pl.BlockSpec((1, 1, 1, c_a), lambda b, c: (b, c, 0, 0))],
        out_specs=[tile, tile, mat, mat],
        out_shape=[full, full, mats, mats],
        compiler_params=_params(("parallel", "parallel")),
    )(ra, at, bt, kt, bh, kh, v, wl)
    nb = SCAN_BATCH if bsz % SCAN_BATCH == 0 else 1
    tile = pl.BlockSpec((nb, chunk, c_a), lambda b, c: (b, c, 0))
    mat = pl.BlockSpec((nb, 1, HEAD, c_a), lambda b, c: (b, c, 0, 0))
    st = pl.BlockSpec((nb, heads, HEAD, HEAD), lambda b, c: (b, 0, 0, 0))
    return pl.pallas_call(
        _wkv_scan_kernel, name="wkv_scan",
        grid=(bsz // nb, n_chunks),
        in_specs=[tile, tile, mat, mat, tile, tile, st, _const_spec((1, c_a)), _const_spec((1, c_a)),
                  _const_spec((c_a, c_a))],
        out_specs=[tile, st],
        out_shape=[full, jax.ShapeDtypeStruct(s0.shape, F32)],
        scratch_shapes=[pltpu.VMEM((nb, heads, HEAD, HEAD), F32)],
        compiler_params=_params(("parallel", "arbitrary")),
    )(rp, y0, m, n, g, bonus, s0, lnx_g.reshape(1, c_a), lnx_b.reshape(1, c_a), _head_sum_matrix(c_a))


def _wkv_step_kernel(r_ref, w_ref, k_ref, v_ref, a_ref, b_ref, g_ref, bonus_ref, s_ref, lg_ref, lb_ref, hsum_ref,
                     o_ref, sout_ref):
    bt, heads = s_ref.shape[0], s_ref.shape[1]
    c_a = heads * HEAD
    hsum = hsum_ref[...]
    diag = (lax.broadcasted_iota(jnp.int32, (HEAD, c_a), 1) % HEAD
            == lax.broadcasted_iota(jnp.int32, (HEAD, c_a), 0))
    seqs = range(bt)
    row = lambda ref, i: ref[i:i + 1, :]
    s = [jnp.concatenate([s_ref[i, h] for h in range(heads)], axis=1) for i in seqs]
    sa = [_mask_dot(s[i] * row(a_ref, i), hsum) for i in seqs]
    v_rows = [_mask_dot(jnp.where(diag, row(v_ref, i), 0.0), hsum) for i in seqs]
    s = [s[i] * row(w_ref, i) + sa[i] * row(b_ref, i) + v_rows[i] * row(k_ref, i) for i in seqs]
    for i in seqs:
        for h in range(heads):
            sout_ref[i, h] = s[i][:, h * HEAD:(h + 1) * HEAD]
    y_rows = [_mask_dot(s[i] * row(r_ref, i), hsum) for i in seqs]
    y = jnp.concatenate([jnp.sum(jnp.where(diag, y_rows[i], 0.0), axis=0, keepdims=True) for i in seqs], axis=0)
    y = _head_norm_wide(y, hsum)
    o_ref[...] = (y * lg_ref[...] + lb_ref[...] + bonus_ref[...]) * g_ref[...]


def wkv_step(ops, s0, lnx_g, lnx_b, bt):
    bsz, c_a = ops[0].shape
    rows = pl.BlockSpec((bt, c_a), lambda i: (i, 0))
    st = pl.BlockSpec((bt,) + s0.shape[1:], lambda i: (i, 0, 0, 0))
    hsum = _head_sum_matrix(c_a)
    return pl.pallas_call(
        _wkv_step_kernel, name="wkv_step",
        grid=(bsz // bt,),
        in_specs=[rows] * 8 + [st, _const_spec((1, c_a)), _const_spec((1, c_a)), _const_spec(hsum.shape)],
        out_specs=[rows, st],
        out_shape=[jax.ShapeDtypeStruct((bsz, c_a), F32), jax.ShapeDtypeStruct(s0.shape, F32)],
        compiler_params=_params(("parallel",)),
    )(*ops, s0, lnx_g.reshape(1, c_a), lnx_b.reshape(1, c_a), hsum)


def rope_tables(pos):
    inv_freq = ROPE_THETA ** (-jnp.arange(ROT_HALF, dtype=F32) / ROT_HALF)
    ang = pos[:, None] * inv_freq[None, :]
    cos, sin = jnp.cos(ang), jnp.sin(ang)
    t = pos.shape[0]
    rest = HEAD - 2 * ROT_HALF
    c = jnp.concatenate([cos, cos, jnp.ones((t, rest), F32)], axis=1)
    s1 = jnp.concatenate([jnp.zeros((t, ROT_HALF), F32), sin, jnp.zeros((t, rest), F32)], axis=1)
    s2 = jnp.concatenate([-sin, jnp.zeros((t, HEAD - ROT_HALF), F32)], axis=1)
    rep = LANES // HEAD
    return tuple(jnp.tile(x, (1, rep)) for x in (c, s1, s2))


def _rope(x, c, s1, s2):
    width = x.shape[-1]
    rep = width // c.shape[-1]
    if rep > 1:
        c, s1, s2 = (jnp.concatenate([t] * rep, axis=1) for t in (c, s1, s2))
    return x * c + pltpu.roll(x, ROT_HALF, 1) * s1 + pltpu.roll(x, width - ROT_HALF, 1) * s2


def _attn_kernel(window, q_w, kv_w, zq_ref, kp_ref, vp_ref, rc_ref, rs1_ref, rs2_ref, pc_ref, ps1_ref, ps2_ref,
                 mk_ref, mv_ref, sink_ref, ob_ref, om_ref, kr_ref):
    n = pl.program_id(1)
    zq = zq_ref[0]
    scale = HEAD ** -0.5
    q = zq[:, :q_w]
    k = zq[:, q_w:q_w + kv_w]
    v = zq[:, q_w + kv_w:q_w + 2 * kv_w]
    qm = zq[:, q_w + 2 * kv_w:]
    qr = _rope(q, rc_ref[...], rs1_ref[...], rs2_ref[...]) * scale
    kr = _rope(k, rc_ref[...], rs1_ref[...], rs2_ref[...])
    kr_ref[0] = kr
    kpr = _rope(kp_ref[0], pc_ref[...], ps1_ref[...], ps2_ref[...])
    vp = vp_ref[0]
    gqa = q_w // kv_w
    assert window & (window - 1) == 0
    qi = lax.broadcasted_iota(jnp.int32, (gqa * window, 2 * window), 0) & (window - 1)
    kj = lax.broadcasted_iota(jnp.int32, (gqa * window, 2 * window), 1)
    valid = (kj > qi) & (kj <= qi + window) & ((kj >= window) | (n > 0))
    tdot = lambda a, b: lax.dot_general(a, b, (((1,), (1,)), ((), ())), preferred_element_type=F32)
    dot = lambda a, b: jnp.dot(a, b, preferred_element_type=F32)
    kvs = range(kv_w // HEAD)
    mhs = range(qm.shape[1] // HEAD)
    hsl = lambda h: slice(h * HEAD, (h + 1) * HEAD)
    qmb = (qm * scale).astype(BF16)
    mk = mk_ref[0].astype(BF16)
    mv = mv_ref[0].astype(BF16)
    kcat = [jnp.concatenate([kpr[:, hsl(hk)], kr[:, hsl(hk)]], axis=0).astype(BF16) for hk in kvs]
    vcat = [jnp.concatenate([vp[:, hsl(hk)], v[:, hsl(hk)]], axis=0).astype(BF16) for hk in kvs]
    qs = [jnp.concatenate([qr[:, hsl(hk * gqa + g)] for g in range(gqa)], axis=0).astype(BF16) for hk in kvs]
    s = [jnp.where(valid, tdot(qs[hk], kcat[hk]), NEG_INF) for hk in kvs]
    sm = [tdot(qmb[:, hsl(h)], mk[:, hsl(h)]) for h in mhs]
    sink = [jnp.concatenate([jnp.full((window, 1), sink_ref[hk * gqa + g], F32) for g in range(gqa)], axis=0)
            for hk in kvs]
    m = [jnp.maximum(jnp.max(s[hk], axis=-1, keepdims=True), sink[hk]) for hk in kvs]
    p = [jnp.exp(s[hk] - m[hk]) for hk in kvs]
    pm = [jnp.exp(sm[h] - jnp.max(sm[h], axis=-1, keepdims=True)) for h in mhs]
    o = [dot(p[hk].astype(BF16), vcat[hk])
         / (jnp.sum(p[hk], axis=-1, keepdims=True) + jnp.exp(sink[hk] - m[hk])) for hk in kvs]
    om = [dot(pm[h].astype(BF16), mv[:, hsl(h)]) / jnp.sum(pm[h], axis=-1, keepdims=True) for h in mhs]
    ob_ref[0] = jnp.concatenate([o[hk][g * window:(g + 1) * window] for hk in kvs for g in range(gqa)], axis=1)
    om_ref[0] = jnp.concatenate(om, axis=1)


def attn_prompt(zq, mk, mv, sinks, tables, window, q_w, kv_w):
    bsz, t, zw = zq.shape
    mem_w = zw - q_w - 2 * kv_w
    assert kv_w == LANES and q_w % kv_w == 0
    kcol, vcol = q_w // kv_w, q_w // kv_w + 1
    prev = lambda n: jnp.maximum(n - 1, 0)
    tab = pl.BlockSpec((window, LANES), lambda b, n: (n, 0))
    ptab = pl.BlockSpec((window, LANES), lambda b, n: (prev(n), 0))
    mem = pl.BlockSpec((1,) + mk.shape[1:], lambda b, n: (b, 0, 0))
    return pl.pallas_call(
        functools.partial(_attn_kernel, window, q_w, kv_w), name="attn",
        grid=(bsz, t // window),
        in_specs=[pl.BlockSpec((1, window, zw), lambda b, n: (b, n, 0)),
                  pl.BlockSpec((1, window, kv_w), lambda b, n: (b, prev(n), kcol)),
                  pl.BlockSpec((1, window, kv_w), lambda b, n: (b, prev(n), vcol)),
                  tab, tab, tab, ptab, ptab, ptab, mem, mem,
                  pl.BlockSpec(memory_space=pltpu.SMEM)],
        out_specs=[pl.BlockSpec((1, window, q_w), lambda b, n: (b, n, 0)),
                   pl.BlockSpec((1, window, mem_w), lambda b, n: (b, n, 0)),
                   pl.BlockSpec((1, window, kv_w), lambda b, n: (b, n, 0))],
        out_shape=[jax.ShapeDtypeStruct((bsz, t, q_w), F32), jax.ShapeDtypeStruct((bsz, t, mem_w), F32),
                   jax.ShapeDtypeStruct((bsz, t, kv_w), F32)],
        compiler_params=_params(("parallel", "parallel")),
    )(zq, zq, zq, *tables, *tables, mk, mv, sinks)


def _attn_step_kernel(q_w, kv_w, zq_ref, ck_ref, cv_ref, mk_ref, mv_ref, rc_ref, rs1_ref, rs2_ref, sink_ref,
                      ob_ref, om_ref, nk_ref, nv_ref):
    bt = zq_ref.shape[0]
    window = ck_ref.shape[1]
    mem_w = om_ref.shape[1]
    n_q, gqa, per_vreg = q_w // HEAD, q_w // kv_w, LANES // HEAD
    scale = HEAD ** -0.5
    zq = zq_ref[...]
    q = _rope(zq[:, :q_w], rc_ref[...], rs1_ref[...], rs2_ref[...]) * scale
    k_new = _rope(zq[:, q_w:q_w + kv_w], rc_ref[...], rs1_ref[...], rs2_ref[...])
    v_new = zq[:, q_w + kv_w:q_w + 2 * kv_w]
    qm = zq[:, q_w + 2 * kv_w:] * scale
    own = lambda w: (lax.broadcasted_iota(jnp.int32, (n_q, w), 1) // HEAD
                     == lax.broadcasted_iota(jnp.int32, (n_q, w), 0))
    own_q, own_m = own(q_w), own(mem_w)
    hrow = lax.broadcasted_iota(jnp.int32, (n_q, LANES), 0)
    hblk = lax.broadcasted_iota(jnp.int32, (n_q, LANES), 1) // HEAD
    swap = (hrow % per_vreg) != (hrow // gqa)
    keep = hblk == hrow % per_vreg
    key_ok = lax.broadcasted_iota(jnp.int32, (n_q, window), 1) >= 1
    wrow = lax.broadcasted_iota(jnp.int32, (window, kv_w), 0)
    sink = sink_ref[...]
    bs = range(bt)
    tdot = lambda a, b: lax.dot_general(a.astype(BF16), b.astype(BF16), (((1,), (1,)), ((), ())),
                                        preferred_element_type=F32)
    for b in bs:
        nk_ref[b] = jnp.where(wrow == window - 1, k_new[b:b + 1], pltpu.roll(ck_ref[b], window - 1, 0))
        nv_ref[b] = jnp.where(wrow == window - 1, v_new[b:b + 1], pltpu.roll(cv_ref[b], window - 1, 0))
    q8 = []
    for b in bs:
        rep = jnp.where(own_q, q[b:b + 1], 0.0)
        fold = sum(rep[:, c * LANES:(c + 1) * LANES] for c in range(q_w // LANES))
        q8.append(jnp.where(swap, pltpu.roll(fold, HEAD, 1), fold))
    qm8 = [jnp.where(own_m, qm[b:b + 1], 0.0) for b in bs]
    s = [jnp.where(key_ok, tdot(q8[b], ck_ref[b]), NEG_INF) for b in bs]
    sm = [tdot(qm8[b], mk_ref[b]) for b in bs]
    s_new = [jnp.sum(q8[b] * k_new[b:b + 1], axis=1, keepdims=True) for b in bs]
    m = [jnp.maximum(jnp.maximum(jnp.max(s[b], axis=1, keepdims=True), s_new[b]), sink) for b in bs]
    p = [jnp.exp(s[b] - m[b]) for b in bs]
    p_new = [jnp.exp(s_new[b] - m[b]) for b in bs]
    pm = [jnp.exp(sm[b] - jnp.max(sm[b], axis=1, keepdims=True)) for b in bs]
    o8 = [(_bdot(p[b], cv_ref[b]) + p_new[b] * v_new[b:b + 1])
          / (jnp.sum(p[b], axis=1, keepdims=True) + p_new[b] + jnp.exp(sink - m[b])) for b in bs]
    om8 = [_bdot(pm[b], mv_ref[b]) / jnp.sum(pm[b], axis=1, keepdims=True) for b in bs]
    ob, om = [], []
    for b in bs:
        o = jnp.where(swap, pltpu.roll(o8[b], HEAD, 1), o8[b])
        o = jnp.concatenate([jnp.where(keep, o, 0.0)] * (q_w // LANES), axis=1)
        ob.append(jnp.sum(jnp.where(own_q, o, 0.0), axis=0, keepdims=True))
        om.append(jnp.sum(jnp.where(own_m, om8[b], 0.0), axis=0, keepdims=True))
    ob_ref[...] = jnp.concatenate(ob, axis=0)
    om_ref[...] = jnp.concatenate(om, axis=0)


def attn_step(zq, ck, cv, mk, mv, sinks, tables, q_w, kv_w, bt):
    bsz, zw = zq.shape
    mem_w = zw - q_w - 2 * kv_w
    window, n_mem = ck.shape[1], mk.shape[1]
    n_q = q_w // HEAD
    assert kv_w == LANES and mem_w // HEAD <= n_q and n_q == 8
    rows = lambda w: pl.BlockSpec((bt, w), lambda i: (i, 0))
    cache = pl.BlockSpec((bt, window, kv_w), lambda i: (i, 0, 0))
    mem = pl.BlockSpec((bt, n_mem, mem_w), lambda i: (i, 0, 0))
    consts = list(tables) + [sinks.reshape(n_q, 1)]
    const_specs = [_const_spec(c.shape) for c in consts]
    return pl.pallas_call(
        functools.partial(_attn_step_kernel, q_w, kv_w), name="attn_step",
        grid=(bsz // bt,),
        in_specs=[rows(zw), cache, cache, mem, mem] + const_specs,
        out_specs=[rows(q_w), rows(mem_w), cache, cache],
        out_shape=[jax.ShapeDtypeStruct((bsz, q_w), F32), jax.ShapeDtypeStruct((bsz, mem_w), F32),
                   jax.ShapeDtypeStruct(ck.shape, F32), jax.ShapeDtypeStruct(cv.shape, F32)],
        compiler_params=_params(("parallel",)),
    )(zq, ck, cv, mk, mv, *consts)


def _proj_kernel(x_ref, w_ref, o_ref):
    o_ref[...] = jnp.dot(x_ref[...].astype(BF16), w_ref[...], preferred_element_type=F32)


def proj(x, w, tm):
    n, d = x.shape
    return pl.pallas_call(
        _proj_kernel, name="proj",
        grid=(n // tm,),
        in_specs=[pl.BlockSpec((tm, d), lambda i: (i, 0)), _const_spec(w.shape)],
        out_specs=pl.BlockSpec((tm, w.shape[1]), lambda i: (i, 0)),
        out_shape=jax.ShapeDtypeStruct((n, w.shape[1]), F32),
        compiler_params=_params(("parallel",)),
    )(x, w)


def _merge_kernel(alpha, n_tiles, x_ref, oa_ref, ob_ref, om_ref, gt_ref, lig_ref, lib_ref, pa_ref, pb_ref, pm_ref,
                  wo_ref, l1g_ref, l1b_ref, wr_ref, br_ref, *refs):
    x1_ref, eidx_ref, gate_ref = refs[-3:]

    @pl.when(pl.program_id(0) >= n_tiles)
    def _():
        for ref in (x1_ref, eidx_ref, gate_ref):
            ref[...] = jnp.zeros_like(ref)

    @pl.when(pl.program_id(0) < n_tiles)
    def _():
        _merge_tile(alpha, x_ref, oa_ref, ob_ref, om_ref, gt_ref, lig_ref, lib_ref, pa_ref, pb_ref, pm_ref, wo_ref,
                    l1g_ref, l1b_ref, wr_ref, br_ref, x1_ref, eidx_ref, gate_ref)


def _merge_tile(alpha, x_ref, oa_ref, ob_ref, om_ref, gt_ref, lig_ref, lib_ref, pa_ref, pb_ref, pm_ref, wo_ref,
                l1g_ref, l1b_ref, wr_ref, br_ref, x1_ref, eidx_ref, gate_ref):
    d = x_ref.shape[1]
    xn = _ln(x_ref[...], lig_ref[...], lib_ref[...])
    gts = jax.nn.sigmoid(gt_ref[...])
    merged = (gts[:, :d] * _bdot(oa_ref[...], pa_ref[...]) + gts[:, d:2 * d] * _bdot(ob_ref[...], pb_ref[...])
              + gts[:, 2 * d:] * _bdot(om_ref[...], pm_ref[...]))
    x1 = _ln(alpha * xn + _bdot(merged, wo_ref[...]), l1g_ref[...], l1b_ref[...])
    x1_ref[...] = x1
    x_hi, x_lo = _pieces(x1, 2)
    w_hi, w_lo = _pieces(wr_ref[...], 2)
    dot = lambda a, b: jnp.dot(a, b, preferred_element_type=F32)
    logits = dot(x_hi, w_hi) + dot(x_hi, w_lo) + dot(x_lo, w_hi) + br_ref[...]
    lane = lax.broadcasted_iota(jnp.int32, logits.shape, 1)
    lane_f = lane.astype(F32)
    first = lambda hit: jnp.min(jnp.where(hit, lane_f, float(LANES)), axis=-1, keepdims=True).astype(jnp.int32)
    gmask = lane < N_GROUPS
    gl = jnp.where(gmask, logits, NEG_INF)
    gmax = jnp.max(gl, axis=-1, keepdims=True)
    gidx = first(gl == gmax)
    g_w = 1.0 / jnp.sum(jnp.where(gmask, jnp.exp(gl - gmax), 0.0), axis=-1, keepdims=True)
    lo = N_GROUPS + gidx * EXPERTS_PER_GROUP
    el = jnp.where((lane >= lo) & (lane < lo + EXPERTS_PER_GROUP), logits, NEG_INF)
    v1 = jnp.max(el, axis=-1, keepdims=True)
    i1 = first(el == v1)
    el2 = jnp.where(lane == i1, NEG_INF, el)
    v2 = jnp.max(el2, axis=-1, keepdims=True)
    i2 = first(el2 == v2)
    e2 = jnp.exp(v2 - v1)
    gate1 = g_w / (1.0 + e2)
    eidx_ref[...] = jnp.where(lane == 0, i1 - N_GROUPS, jnp.where(lane == 1, i2 - N_GROUPS, 0))
    gate_ref[...] = jnp.where(lane == 0, gate1, jnp.where(lane == 1, gate1 * e2, 0.0))


def merge(x, oa, ob, om, gt, w, tm, alpha, n_total, row_offset=0, into=None):
    n, d = x.shape
    assert row_offset % tm == 0 and n % tm == 0 and n_total % tm == 0
    off = row_offset // tm
    into = list(into or [])
    n_tiles = n // tm
    steps = n_tiles if into else n_total // tm
    rows = lambda a: pl.BlockSpec((tm, a.shape[1]), lambda i: (jnp.minimum(i, n_tiles - 1), 0))
    out = lambda width: pl.BlockSpec((tm, width), lambda i: (i + off, 0))
    consts = [w['ln_in_g'], w['ln_in_b'], w['p_a'], w['p_b'], w['p_m'], w['w_o'], w['ln1_g'], w['ln1_b'],
              w['w_route'], w['b_route']]
    n_in = 5 + len(consts)
    return pl.pallas_call(
        functools.partial(_merge_kernel, alpha, n_tiles), name="merge",
        grid=(steps,),
        in_specs=[rows(a) for a in (x, oa, ob, om, gt)] + [_const_spec(c.shape) for c in consts]
        + [pl.BlockSpec(memory_space=pl.ANY)] * len(into),
        out_specs=[out(d), out(LANES), out(LANES)],
        out_shape=[jax.ShapeDtypeStruct((n_total, d), F32), jax.ShapeDtypeStruct((n_total, LANES), jnp.int32),
                   jax.ShapeDtypeStruct((n_total, LANES), F32)],
        input_output_aliases={n_in + k: k for k in range(len(into))},
        compiler_params=_params(("parallel",)),
    )(x, oa, ob, om, gt, *consts, *into)


ROW_DMA_UNROLL = 8
DRAIN_STEPS = 2


def _row_copies(asg_ref, base, count, n_asg, x_hbm, buf, y_hbm, sem, gather, unrolled):
    def one(r, priority):
        a = asg_ref[base + r]
        if gather:
            tok = jnp.minimum(a, n_asg - 1)
            tok = jnp.where(tok >= n_asg // 2, tok - n_asg // 2, tok)
            copy = pltpu.make_async_copy(x_hbm.at[pl.ds(tok, 1)], buf.at[pl.ds(r, 1)], sem)
        else:
            copy = pltpu.make_async_copy(buf.at[pl.ds(r, 1)], y_hbm.at[pl.ds(a, 1)], sem)
        copy.start(priority=priority)

    if unrolled:
        for r in range(count):
            one(r, r % 2)
        return

    def body(g, carry):
        for j in range(ROW_DMA_UNROLL):
            one(g * ROW_DMA_UNROLL + j, j % 2)
        return carry
    lax.fori_loop(0, count // ROW_DMA_UNROLL, body, 0)


def _moe_expert_kernel(n_asg, asg_ref, be_ref, nu_ref, x_hbm, wg_ref, wu_ref, wd_ref, y_hbm,
                       xbuf, ybuf, wgb, wub, wdb, gsem, ssem):
    i = pl.program_id(0)
    used = nu_ref[0]
    rows = xbuf.shape[1]
    n_blocks = be_ref.shape[0]
    blk = jnp.minimum(i, n_blocks - 1)

    @pl.when((i == 0) | (be_ref[blk] != be_ref[jnp.maximum(blk - 1, 0)]))
    def _():
        wgb[...] = wg_ref[0, 0].astype(BF16)
        wub[...] = wu_ref[0, 0].astype(BF16)
        wdb[...] = wd_ref[0, 0].astype(BF16)

    def wait_gather(slot):
        pltpu.make_async_copy(x_hbm.at[pl.ds(0, rows)], xbuf.at[slot], gsem.at[slot]).wait()

    def wait_scatter(slot):
        pltpu.make_async_copy(ybuf.at[slot], y_hbm.at[pl.ds(0, rows)], ssem.at[slot]).wait()

    def gather(b, slot, unrolled):
        _row_copies(asg_ref, b * rows, rows, n_asg, x_hbm, xbuf.at[slot], y_hbm, gsem.at[slot], True, unrolled)

    def scatter(b, slot, unrolled):
        _row_copies(asg_ref, b * rows, rows, n_asg, x_hbm, ybuf.at[slot], y_hbm, ssem.at[slot], False, unrolled)

    def expert(slot):
        xb = xbuf[slot].astype(BF16)
        hg = jnp.dot(xb, wgb[...], preferred_element_type=F32)
        hu = jnp.dot(xb, wub[...], preferred_element_type=F32)
        h = hg * jax.nn.sigmoid(hg) * hu
        ybuf[slot] = jnp.dot(h.astype(BF16), wdb[...], preferred_element_type=F32)

    @pl.when(i == 0)
    def _():
        ybuf[1] = jnp.zeros(ybuf.shape[1:], F32)
        spare = [pltpu.make_async_copy(ybuf.at[1], y_hbm.at[pl.ds(n_asg + j * rows, rows)], ssem.at[1])
                 for j in range((y_hbm.shape[0] - n_asg) // rows)]
        for copy in spare:
            copy.start()
        for copy in spare:
            copy.wait()

    @pl.when((i >= 2) & (i - 2 < used))
    def _():
        wait_scatter(i % 2)

    steady = (i >= 1) & (i + 1 < used)
    for slot in range(2):
        @pl.when(steady & (i % 2 == slot))
        def _():
            wait_gather(slot)
            gather(i + 1, 1 - slot, True)
            scatter(i - 1, 1 - slot, True)
            expert(slot)

    @pl.when(jnp.logical_not(steady))
    def _():
        slot = i % 2

        @pl.when((i == 0) & (used > 0))
        def _():
            gather(0, 0, False)

        @pl.when(i < used)
        def _():
            wait_gather(slot)

        @pl.when(i + 1 < used)
        def _():
            gather(i + 1, 1 - slot, False)

        @pl.when((i >= 1) & (i - 1 < used))
        def _():
            scatter(i - 1, 1 - slot, False)

        @pl.when(i < used)
        def _():
            expert(slot)


def moe_experts(x1, n_asg, asg, blk_e, n_used, e_gate, e_up, e_down):
    d = x1.shape[1]
    n_blocks = blk_e.shape[0]
    ff = e_gate.shape[-1]
    n_rows = n_asg + e_gate.shape[1] * EXPERT_BLOCK
    weight = lambda shape: pl.BlockSpec(
        (1, 1) + shape, lambda i, asg, be, nu: (0, be[jnp.minimum(i, n_blocks - 1)], 0, 0))
    return pl.pallas_call(
        functools.partial(_moe_expert_kernel, n_asg), name="moe_expert",
        grid_spec=pltpu.PrefetchScalarGridSpec(
            num_scalar_prefetch=3,
            grid=(n_blocks + DRAIN_STEPS,),
            in_specs=[pl.BlockSpec(memory_space=pl.ANY), weight((d, ff)), weight((d, ff)), weight((ff, d))],
            out_specs=pl.BlockSpec(memory_space=pl.ANY),
            scratch_shapes=[pltpu.VMEM((2, EXPERT_BLOCK, d), F32), pltpu.VMEM((2, EXPERT_BLOCK, d), F32),
                            pltpu.VMEM((d, ff), BF16), pltpu.VMEM((d, ff), BF16), pltpu.VMEM((ff, d), BF16),
                            pltpu.SemaphoreType.DMA((2,)), pltpu.SemaphoreType.DMA((2,))]),
        out_shape=jax.ShapeDtypeStruct((n_rows, d), F32),
        compiler_params=_params(("arbitrary",)),
    )(asg, blk_e, n_used, x1, e_gate, e_up, e_down)


def _moe_combine_kernel(alpha, lead_tiles, y0_ref, y1_ref, x1_ref, gate_ref, g_ref, b_ref, lead_ref, tail_ref):
    i = pl.program_id(0)
    gate = gate_ref[...]
    moe = gate[:, 0:1] * y0_ref[...] + gate[:, 1:2] * y1_ref[...]
    out = _ln(alpha * x1_ref[...] + moe, g_ref[...], b_ref[...])

    @pl.when(i < lead_tiles)
    def _():
        lead_ref[...] = out

    @pl.when(i >= lead_tiles)
    def _():
        tail_ref[...] = out


def moe_combine(y, x1, gate, g, b, tm, alpha, n, n_lead):
    d = x1.shape[1]
    assert n_lead % tm == 0 and (n - n_lead) % tm == 0 and n > n_lead
    lead_tiles = n_lead // tm
    return pl.pallas_call(
        functools.partial(_moe_combine_kernel, alpha, lead_tiles), name="moe_combine",
        grid=(n // tm,),
        in_specs=[pl.BlockSpec((tm, d), lambda i: (i, 0)), pl.BlockSpec((tm, d), lambda i: (i + n // tm, 0)),
                  pl.BlockSpec((tm, d), lambda i: (i, 0)),
                  pl.BlockSpec((tm, LANES), lambda i: (i, 0)), _const_spec((1, d)), _const_spec((1, d))],
        out_specs=[pl.BlockSpec((tm, d), lambda i: (jnp.minimum(i, lead_tiles - 1), 0)),
                   pl.BlockSpec((tm, d), lambda i: (jnp.maximum(i - lead_tiles, 0), 0))],
        out_shape=[jax.ShapeDtypeStruct((n_lead, d), F32), jax.ShapeDtypeStruct((n - n_lead, d), F32)],
        compiler_params=_params(("arbitrary",)),
    )(y, y, x1, gate, g.reshape(1, d), b.reshape(1, d))


def moe_routing(experts):
    n, top_k = experts.shape
    n_exp = N_GROUPS * EXPERTS_PER_GROUP
    a = n * top_k
    flat_e = experts.T.reshape(a)
    onehot = (flat_e[:, None] == jnp.arange(n_exp, dtype=jnp.int32)[None, :]).astype(jnp.int32)
    rank = jnp.take_along_axis(jnp.cumsum(onehot, axis=0) - onehot, flat_e[:, None], axis=1)[:, 0]
    counts = jnp.sum(onehot, axis=0)
    padded = (counts + EXPERT_BLOCK - 1) // EXPERT_BLOCK * EXPERT_BLOCK
    pad_end = jnp.cumsum(padded)
    dest = (pad_end - padded)[flat_e] + rank
    n_blocks = -(-a // EXPERT_BLOCK) + n_exp
    blk_start = jnp.arange(n_blocks, dtype=jnp.int32) * EXPERT_BLOCK
    blk_e = jnp.minimum(jnp.sum((pad_end[None, :] <= blk_start[:, None]).astype(jnp.int32), axis=1), n_exp - 1)
    spare = jnp.arange(n_blocks * EXPERT_BLOCK, dtype=jnp.int32) - jnp.repeat(jnp.cumsum(counts)[blk_e], EXPERT_BLOCK)
    spare = a + jnp.clip(spare, 0, n_exp * EXPERT_BLOCK - 1)
    asg = spare.astype(jnp.int32).at[dest].set(jnp.arange(a, dtype=jnp.int32))
    n_used = (pad_end[-1:] // EXPERT_BLOCK).astype(jnp.int32)
    return asg, blk_e.astype(jnp.int32), n_used


def hier_moe_ln(x1, eidx, gate, w, tm, alpha, n_tokens, n_lead):
    asg, blk_e, n_used = moe_routing(eidx[:n_tokens, :2])
    y = moe_experts(x1, 2 * n_tokens, asg, blk_e, n_used, w['e_gate'], w['e_up'], w['e_down'])
    return moe_combine(y, x1, gate, w['ln2_g'], w['ln2_b'], tm, alpha, n_tokens, n_lead)


def kernel(x_prompt, x_sample, mem_prompt, state_wkv, state_shift, cache_win_k, cache_win_v, cache_mem_k, cache_mem_v, ln_in_g, ln_in_b, w_in, mu, w0, w_up, a0, a_up, g_up, k_k, k_a, r_k, lnx_g, lnx_b, sinks, w_mem_kv, p_a, p_b, p_m, w_o, ln1_g, ln1_b, w_group, b_group, w_router, b_router, e_gate, e_up, e_down, ln2_g, ln2_b):
    depth = w_in.shape[0]
    assert depth == 1, "single-layer step"
    bsz, seq, d = x_prompt.shape
    dec = x_sample.shape[0]
    assert x_sample.shape[1] == 1
    c_shift = mu.shape[-1]
    c_a = w0.shape[-1]
    window, kv_w = cache_win_k.shape[2], cache_win_k.shape[3] * cache_win_k.shape[4]
    n_mem, mem_w = cache_mem_k.shape[2], cache_mem_k.shape[3] * cache_mem_k.shape[4]
    q_w = sinks.shape[-1] * HEAD
    qkvm_w = q_w + 2 * kv_w + mem_w
    alpha = (2.0 * depth) ** 0.25
    past_len = float(PAST_LEN)
    chunk = 64

    w_in_b = w_in[0].astype(BF16)
    w_parts = [w_in_b[:, :c_shift], w_in_b[:, c_shift:c_shift + qkvm_w], w_in_b[:, c_shift + qkvm_w:]]
    rp = dict(mu=mu[0], w0=w0[0], w_up=w_up[0], a0=a0[0], a_up=a_up[0], g_up=g_up[0], k_k=k_k[0], k_a=k_a[0],
              r_k=r_k[0].reshape(-1))
    n_route = N_GROUPS * (1 + EXPERTS_PER_GROUP)
    mw = dict(ln_in_g=ln_in_g.reshape(1, d), ln_in_b=ln_in_b.reshape(1, d), p_a=p_a[0].astype(BF16),
              p_b=p_b[0].astype(BF16), p_m=p_m[0].astype(BF16), w_o=w_o[0].astype(BF16),
              ln1_g=ln1_g[0].reshape(1, d), ln1_b=ln1_b[0].reshape(1, d),
              w_route=jnp.pad(jnp.concatenate([w_group[0], w_router[0]], axis=1), ((0, 0), (0, LANES - n_route))),
              b_route=jnp.pad(jnp.concatenate([b_group[0], b_router[0]]), (0, LANES - n_route)).reshape(1, LANES),
              e_gate=e_gate, e_up=e_up, e_down=e_down, ln2_g=ln2_g[0], ln2_b=ln2_b[0])

    xp = x_prompt.reshape(bsz * seq, d)
    zr, zq, zg = ln_proj(xp, ln_in_g, ln_in_b, w_parts, 256)
    zr3 = zr.reshape(bsz, seq, c_shift)
    prep = rwkv_prep(zr3, jnp.zeros((bsz, 1, c_shift), F32), rp, chunk, 256)
    o_a, wkv_p = wkv(prep, jnp.zeros((bsz, c_a // HEAD, HEAD, HEAD), F32), lnx_g[0], lnx_b[0], chunk)
    mkv = proj(mem_prompt.reshape(bsz * n_mem, d), w_mem_kv[0].astype(BF16), 256).reshape(bsz, n_mem, 2 * mem_w)
    mk_p, mv_p = mkv[..., :mem_w], mkv[..., mem_w:]
    tables = rope_tables(jnp.arange(seq, dtype=F32))
    o_b, o_m, k_rot = attn_prompt(zq.reshape(bsz, seq, qkvm_w), mk_p, mv_p, sinks[0], tables, window, q_w, kv_w)
    n_all = bsz * seq + dec
    n_buf = -(-n_all // 256) * 256
    routed = merge(xp, o_a.reshape(-1, c_a), o_b.reshape(-1, q_w), o_m.reshape(-1, mem_w), zg, mw, 256, alpha, n_buf)
    shift_p = zr3[:, -1]
    kb_p = k_rot[:, -window:].reshape(bsz, window, H_KV, HEAD)
    vb_p = zq.reshape(bsz, seq, qkvm_w)[:, -window:, q_w + kv_w:q_w + 2 * kv_w].reshape(bsz, window, H_KV, HEAD)

    xs = x_sample.reshape(dec, d)
    zr_s, zq_s, zg_s = ln_proj(xs, ln_in_g, ln_in_b, w_parts, dec)
    ops_s = rwkv_prep(zr_s.reshape(1, dec, c_shift), state_shift[0].reshape(1, dec, c_shift), rp, 1, dec)
    o_a_s, wkv_s = wkv_step([a.reshape(dec, c_a) for a in ops_s], state_wkv[0], lnx_g[0], lnx_b[0], 8)
    tables_s = rope_tables(jnp.full((1,), past_len, F32))
    o_b_s, o_m_s, nk_s, nv_s = attn_step(
        zq_s, cache_win_k[0].reshape(dec, window, kv_w), cache_win_v[0].reshape(dec, window, kv_w),
        cache_mem_k[0].reshape(dec, n_mem, mem_w), cache_mem_v[0].reshape(dec, n_mem, mem_w),
        sinks[0], tables_s, q_w, kv_w, 8)
    x1, eidx, gate = merge(xs, o_a_s, o_b_s, o_m_s, zg_s, mw, dec, alpha, n_buf, bsz * seq, routed)

    y_prompt, y_sample = hier_moe_ln(x1, eidx, gate, mw, dec, alpha, n_all, bsz * seq)
    y_prompt = y_prompt.reshape(bsz, seq, d)
    y_sample = y_sample.reshape(dec, 1, d)

    sd = state_wkv.dtype
    return (y_prompt, y_sample, wkv_p[None].astype(sd), wkv_s[None].astype(sd), shift_p[None], zr_s[None],
            kb_p[None], vb_p[None], nk_s.reshape(dec, window, H_KV, HEAD)[None],
            nv_s.reshape(dec, window, H_KV, HEAD)[None],
            mk_p.reshape(bsz, n_mem, -1, HEAD)[None], mv_p.reshape(bsz, n_mem, -1, HEAD)[None])
```

```python
import functools
import math

import jax
import jax.numpy as jnp
from jax import lax
from jax.experimental import pallas as pl
from jax.experimental.pallas import tpu as pltpu

F32 = jnp.float32
BF16 = jnp.bfloat16
SCAN_BATCH = 4

HEAD = 64
LANES = 128
H_KV = 2
ROT_HALF = 8
ROPE_THETA = 500000.0
PAST_LEN = 8192
N_GROUPS = 4
EXPERTS_PER_GROUP = 8
EXPERT_BLOCK = 128
LN_EPS = 1e-5
LNX_EPS = 64e-5
NEG_INF = -1e30
VMEM_LIMIT = 48 * 1024 * 1024


def _pieces(x, n):
    out = []
    for _ in range(n):
        p = x.astype(BF16)
        out.append(p)
        x = x - p.astype(F32)
    return out


def _mask_dot(x, mask, n=2):
    return sum(jnp.dot(p, mask, preferred_element_type=F32) for p in _pieces(x, n))


def _split3(x, axis, lhs):
    hi = x.astype(BF16).astype(F32)
    lo = x - hi
    return jnp.concatenate([hi, hi, lo] if lhs else [hi, lo, hi], axis=axis).astype(BF16)


def _dot3(a, b):
    return jnp.dot(_split3(a, 1, True), _split3(b, 0, False), preferred_element_type=F32)


def _dot3_t(a, b):
    return lax.dot_general(_split3(a, 1, True), _split3(b, 1, False), (((1,), (1,)), ((), ())),
                           preferred_element_type=F32)


def _bdot(a, b):
    return jnp.dot(a.astype(BF16), b.astype(BF16), preferred_element_type=F32)


def _bdot_t(a, b):
    return lax.dot_general(a.astype(BF16), b.astype(BF16), (((1,), (1,)), ((), ())), preferred_element_type=F32)


def _ln(x, g, b):
    mu = jnp.mean(x, axis=-1, keepdims=True)
    xc = x - mu
    var = jnp.mean(xc * xc, axis=-1, keepdims=True)
    return xc * lax.rsqrt(var + LN_EPS) * g + b


def _const_spec(shape):
    nd = len(shape)
    return pl.BlockSpec(shape, lambda *_: (0,) * nd)


def _params(sem):
    return pltpu.CompilerParams(dimension_semantics=sem, vmem_limit_bytes=VMEM_LIMIT)


def _ln_proj_kernel(x_ref, g_ref, b_ref, *refs):
    nw = len(refs) // 2
    xn = _ln(x_ref[...], g_ref[...], b_ref[...]).astype(BF16)
    for w_ref, o_ref in zip(refs[:nw], refs[nw:]):
        o_ref[...] = jnp.dot(xn, w_ref[...], preferred_element_type=F32)


def ln_proj(x, g, b, ws, tm):
    n, d = x.shape
    return pl.pallas_call(
        _ln_proj_kernel, name="ln_proj",
        grid=(n // tm,),
        in_specs=[pl.BlockSpec((tm, d), lambda i: (i, 0)), _const_spec((1, d)), _const_spec((1, d))]
        + [_const_spec(w.shape) for w in ws],
        out_specs=[pl.BlockSpec((tm, w.shape[1]), lambda i: (i, 0)) for w in ws],
        out_shape=[jax.ShapeDtypeStruct((n, w.shape[1]), F32) for w in ws],
        compiler_params=_params(("parallel",)),
    )(x, g.reshape(1, d), b.reshape(1, d), *ws)


def _rwkv_prep_kernel(chunk, z_ref, prev_ref, mu_ref, w0_ref, wup_ref, a0_ref, aup_ref, gup_ref,
                      kk_ref, ka_ref, rk_ref, hsum_ref, tril_ref, *refs):
    out_refs, carry_ref = refs[:-1], refs[-1]
    z = z_ref[0]
    tt = z.shape[0]
    c_a = w0_ref.shape[-1]
    r_w, r_a, r_g = wup_ref.shape[0], aup_ref.shape[0], gup_ref.shape[0]
    if chunk == 1:
        prev = prev_ref[0]
    else:
        @pl.when(pl.program_id(1) == 0)
        def _():
            carry_ref[...] = prev_ref[0]

        row = lax.broadcasted_iota(jnp.int32, z.shape, 0)
        prev = jnp.where(row == 0, carry_ref[...], pltpu.roll(z, 1, 0))
        carry_ref[...] = z[tt - 1:tt, :]
    zs = z + (prev - z) * mu_ref[...]
    r = zs[:, :c_a]
    k = zs[:, c_a:2 * c_a]
    v = zs[:, 2 * c_a:3 * c_a]
    o = 3 * c_a
    xw = zs[:, o:o + r_w]
    xa = zs[:, o + r_w:o + r_w + r_a]
    xg = zs[:, o + r_w + r_a:o + r_w + r_a + r_g]
    warg = -(w0_ref[...] + _dot3(jnp.tanh(xw), wup_ref[...]))
    softplus = jnp.maximum(warg, 0.0) + jnp.log1p(jnp.exp(-jnp.abs(warg)))
    lw = -jnp.exp(-softplus - 0.5)
    a = jax.nn.sigmoid(a0_ref[...] + _dot3(xa, aup_ref[...]))
    g = _dot3(jax.nn.sigmoid(xg), gup_ref[...])
    kkr = k * kk_ref[...]
    kk = kkr / jnp.maximum(jnp.sqrt(_mask_dot(kkr * kkr, hsum_ref[...])), 1e-12)
    k2 = k * (1.0 + (a - 1.0) * ka_ref[...])
    bonus = _mask_dot(r * k2 * rk_ref[...], hsum_ref[...]) * v
    kb = kk * a
    if chunk == 1:
        outs = (r, jnp.exp(lw), k2, v, -kk, kb, g, bonus)
        for ref, val in zip(out_refs, outs):
            ref[0] = val
        return
    pieces = _pieces(lw, 3)
    ones3 = jnp.ones((chunk, 3 * chunk), BF16)
    cw, cwl = [], []
    for c in range(tt // chunk):
        stack = jnp.concatenate([p[c * chunk:(c + 1) * chunk] for p in pieces], axis=0)
        cw.append(jnp.dot(tril_ref[...], stack, preferred_element_type=F32))
        cwl.append(jnp.dot(ones3, stack, preferred_element_type=F32))
    cw = jnp.concatenate(cw, axis=0)
    cwl = jnp.concatenate(cwl, axis=0)
    e_inv = jnp.exp(-cw)
    e_hat = jnp.exp(cwl - cw)
    outs = (r * jnp.exp(cw), -kk * jnp.exp(cw - lw), kb * e_inv, k2 * e_inv, kb * e_hat, k2 * e_hat, v, g, bonus)
    for ref, val in zip(out_refs[:-1], outs):
        ref[0] = val
    wl = jnp.exp(cwl)
    for c in range(tt // chunk):
        out_refs[-1][0, c] = wl[c * chunk:c * chunk + 1, :]


def _head_sum_matrix(width):
    idx = jnp.arange(width)
    return ((idx[:, None] // HEAD) == (idx[None, :] // HEAD)).astype(BF16)


def rwkv_prep(z, prev, p, chunk, tt):
    bsz, t, cs = z.shape
    c_a = p['w0'].shape[-1]
    assert prev.shape[1] == (t if chunk == 1 else 1)
    ridx = jnp.arange(chunk)
    tril = jnp.tile((ridx[None, :] <= ridx[:, None]).astype(BF16), (1, 3))
    hsum = _head_sum_matrix(c_a)
    row = lambda x: x.reshape(1, -1)
    tile = pl.BlockSpec((1, tt, c_a), lambda b, i: (b, i, 0))
    full = jax.ShapeDtypeStruct((bsz, t, c_a), F32)
    if chunk == 1:
        out_specs, out_shape = [tile] * 8, [full] * 8
        prev_spec = pl.BlockSpec((1, tt, cs), lambda b, i: (b, i, 0))
    else:
        out_specs = [tile] * 9 + [pl.BlockSpec((1, tt // chunk, 1, c_a), lambda b, i: (b, i, 0, 0))]
        out_shape = [full] * 9 + [jax.ShapeDtypeStruct((bsz, t // chunk, 1, c_a), F32)]
        prev_spec = pl.BlockSpec((1, 1, cs), lambda b, i: (b, 0, 0))
    consts = [row(p['mu']), row(p['w0']), p['w_up'], row(p['a0']), p['a_up'], p['g_up'], row(p['k_k']),
              row(p['k_a']), row(p['r_k']), hsum, tril]
    return pl.pallas_call(
        functools.partial(_rwkv_prep_kernel, chunk), name="rwkv_prep",
        grid=(bsz, t // tt),
        in_specs=[pl.BlockSpec((1, tt, cs), lambda b, i: (b, i, 0)), prev_spec] + [_const_spec(c.shape) for c in consts],
        out_specs=out_specs,
        out_shape=out_shape,
        scratch_shapes=[pltpu.VMEM((1, cs), F32)],
        compiler_params=_params(("parallel", "arbitrary")),
    )(z, prev, *consts)


def _wkv_chunk_kernel(ra_ref, at_ref, bt_ref, kt_ref, bh_ref, kh_ref, v_ref, wl_ref, rp_ref, y0_ref, m_ref, n_ref):
    length = ra_ref.shape[1]
    heads = ra_ref.shape[2] // HEAD
    row = lax.broadcasted_iota(jnp.int32, (length, length), 0)
    col = lax.broadcasted_iota(jnp.int32, (length, length), 1)
    strict = row > col
    incl = row >= col
    hrow = lax.broadcasted_iota(jnp.int32, (HEAD, HEAD), 0)
    hcol = lax.broadcasted_iota(jnp.int32, (HEAD, HEAD), 1)
    hs = range(heads)
    sls = [slice(h * HEAD, (h + 1) * HEAD) for h in hs]
    gram = [_bdot_t(jnp.concatenate([at_ref[0, :, sl], ra_ref[0, :, sl]], axis=0),
                    jnp.concatenate([bt_ref[0, :, sl], kt_ref[0, :, sl]], axis=0)) for sl in sls]
    a_ab = [jnp.where(strict, g[:length, :length], 0.0) for g in gram]
    a_ak = [jnp.where(strict, g[:length, length:], 0.0) for g in gram]
    a_rb = [jnp.where(incl, g[length:, :length], 0.0) for g in gram]
    a_rk = [jnp.where(incl, g[length:, length:], 0.0) for g in gram]
    inv = [jnp.where(row == col, 1.0, a) for a in a_ab]
    pw = [_bdot(a, a) for a in a_ab]
    av = [_dot3(a_ak[h], v_ref[0, :, sls[h]]) for h in hs]
    y0 = [_dot3(a_rk[h], v_ref[0, :, sls[h]]) for h in hs]
    nk = [_dot3(v_ref[0, :, sl].T, kh_ref[0, :, sl]) for sl in sls]
    for _ in range(int(math.log2(length)) - 2):
        both = [_bdot(jnp.concatenate([pw[h], inv[h]], axis=0), pw[h]) for h in hs]
        pw = [b[:length] for b in both]
        inv = [inv[h] + both[h][length:] for h in hs]
    inv = [inv[h] + _bdot(inv[h], pw[h]) for h in hs]
    pq = [_dot3(inv[h], jnp.concatenate([at_ref[0, :, sls[h]], av[h]], axis=1)) for h in hs]
    ry = [_dot3(a_rb[h], pq[h]) for h in hs]
    mn = [_dot3(pq[h].T, bh_ref[0, :, sls[h]]) for h in hs]
    rp_ref[0] = jnp.concatenate([ra_ref[0, :, sls[h]] + ry[h][:, :HEAD] for h in hs], axis=1)
    y0_ref[0] = jnp.concatenate([y0[h] + ry[h][:, HEAD:] for h in hs], axis=1)
    m_ref[0, 0] = jnp.concatenate(
        [jnp.where(hrow == hcol, wl_ref[0, 0, :, sls[h]], 0.0) + mn[h][:HEAD] for h in hs], axis=1)
    n_ref[0, 0] = jnp.concatenate([nk[h] + mn[h][HEAD:] for h in hs], axis=1)


def _head_norm_wide(y, hsum):
    yc = y - _mask_dot(y, hsum) * (1.0 / HEAD)
    return yc * lax.rsqrt(_mask_dot(yc * yc, hsum) * (1.0 / HEAD) + LNX_EPS)


def _wkv_scan_kernel(rp_ref, y0_ref, m_ref, n_ref, g_ref, bonus_ref, s0_ref, lg_ref, lb_ref, hsum_ref,
                     o_ref, sout_ref, s_ref):
    c = pl.program_id(1)
    nb = rp_ref.shape[0]
    heads = rp_ref.shape[2] // HEAD

    @pl.when(c == 0)
    def _():
        s_ref[...] = s0_ref[...]

    pairs = [(b, h, slice(h * HEAD, (h + 1) * HEAD)) for b in range(nb) for h in range(heads)]
    s = [s_ref[b, h] for b, h, _ in pairs]
    s_new = [n_ref[b, 0, :, sl] + _dot3(s[j], m_ref[b, 0, :, sl]) for j, (b, h, sl) in enumerate(pairs)]
    ys = [y0_ref[b, :, sl] + _dot3_t(rp_ref[b, :, sl], s[j]) for j, (b, h, sl) in enumerate(pairs)]
    for j, (b, h, _) in enumerate(pairs):
        s_ref[b, h] = s_new[j]
    for b in range(nb):
        y = _head_norm_wide(jnp.concatenate(ys[b * heads:(b + 1) * heads], axis=-1), hsum_ref[...])
        o_ref[b] = (y * lg_ref[...] + lb_ref[...] + bonus_ref[b]) * g_ref[b]

    @pl.when(c == pl.num_programs(1) - 1)
    def _():
        sout_ref[...] = s_ref[...]


def wkv(prep, s0, lnx_g, lnx_b, chunk):
    ra, at, bt, kt, bh, kh, v, g, bonus, wl = prep
    bsz, t, c_a = ra.shape
    heads = c_a // HEAD
    n_chunks = t // chunk
    assert chunk & (chunk - 1) == 0 and chunk >= 4 and t % chunk == 0
    tile = pl.BlockSpec((1, chunk, c_a), lambda b, c: (b, c, 0))
    mat = pl.BlockSpec((1, 1, HEAD, c_a), lambda b, c: (b, c, 0, 0))
    st = pl.BlockSpec((1, heads, HEAD, HEAD), lambda b, c: (b, 0, 0, 0))
    full = jax.ShapeDtypeStruct((bsz, t, c_a), F32)
    mats = jax.ShapeDtypeStruct((bsz, n_chunks, HEAD, c_a), F32)
    rp, y0, m, n = pl.pallas_call(
        _wkv_chunk_kernel, name="wkv_chunk",
        grid=(bsz, n_chunks),
        in_specs=[tile] * 7 + [pl.BlockSpec((1, 1, 1, c_a), lambda b, c: (b, c, 0, 0))],
        out_specs=[tile, tile, mat, mat],
        out_shape=[full, full, mats, mats],
        compiler_params=_params(("parallel", "parallel")),
    )(ra, at, bt, kt, bh, kh, v, wl)
    nb = SCAN_BATCH if bsz % SCAN_BATCH == 0 else 1
    tile = pl.BlockSpec((nb, chunk, c_a), lambda b, c: (b, c, 0))
    mat = pl.BlockSpec((nb, 1, HEAD, c_a), lambda b, c: (b, c, 0, 0))
    st = pl.BlockSpec((nb, heads, HEAD, HEAD), lambda b, c: (b, 0, 0, 0))
    return pl.pallas_call(
        _wkv_scan_kernel, name="wkv_scan",
        grid=(bsz // nb, n_chunks),
        in_specs=[tile, tile, mat, mat, tile, tile, st, _const_spec((1, c_a)), _const_spec((1, c_a)),
                  _const_spec((c_a, c_a))],
        out_specs=[tile, st],
        out_shape=[full, jax.ShapeDtypeStruct(s0.shape, F32)],
        scratch_shapes=[pltpu.VMEM((nb, heads, HEAD, HEAD), F32)],
        compiler_params=_params(("parallel", "arbitrary")),
    )(rp, y0, m, n, g, bonus, s0, lnx_g.reshape(1, c_a), lnx_b.reshape(1, c_a), _head_sum_matrix(c_a))


def _wkv_step_kernel(r_ref, w_ref, k_ref, v_ref, a_ref, b_ref, g_ref, bonus_ref, s_ref, lg_ref, lb_ref, hsum_ref,
                     o_ref, sout_ref):
    bt, heads = s_ref.shape[0], s_ref.shape[1]
    c_a = heads * HEAD
    hsum = hsum_ref[...]
    diag = (lax.broadcasted_iota(jnp.int32, (HEAD, c_a), 1) % HEAD
            == lax.broadcasted_iota(jnp.int32, (HEAD, c_a), 0))
    seqs = range(bt)
    row = lambda ref, i: ref[i:i + 1, :]
    s = [jnp.concatenate([s_ref[i, h] for h in range(heads)], axis=1) for i in seqs]
    sa = [_mask_dot(s[i] * row(a_ref, i), hsum) for i in seqs]
    v_rows = [_mask_dot(jnp.where(diag, row(v_ref, i), 0.0), hsum) for i in seqs]
    s = [s[i] * row(w_ref, i) + sa[i] * row(b_ref, i) + v_rows[i] * row(k_ref, i) for i in seqs]
    for i in seqs:
        for h in range(heads):
            sout_ref[i, h] = s[i][:, h * HEAD:(h + 1) * HEAD]
    y_rows = [_mask_dot(s[i] * row(r_ref, i), hsum) for i in seqs]
    y = jnp.concatenate([jnp.sum(jnp.where(diag, y_rows[i], 0.0), axis=0, keepdims=True) for i in seqs], axis=0)
    y = _head_norm_wide(y, hsum)
    o_ref[...] = (y * lg_ref[...] + lb_ref[...] + bonus_ref[...]) * g_ref[...]


def wkv_step(ops, s0, lnx_g, lnx_b, bt):
    bsz, c_a = ops[0].shape
    rows = pl.BlockSpec((bt, c_a), lambda i: (i, 0))
    st = pl.BlockSpec((bt,) + s0.shape[1:], lambda i: (i, 0, 0, 0))
    hsum = _head_sum_matrix(c_a)
    return pl.pallas_call(
        _wkv_step_kernel, name="wkv_step",
        grid=(bsz // bt,),
        in_specs=[rows] * 8 + [st, _const_spec((1, c_a)), _const_spec((1, c_a)), _const_spec(hsum.shape)],
        out_specs=[rows, st],
        out_shape=[jax.ShapeDtypeStruct((bsz, c_a), F32), jax.ShapeDtypeStruct(s0.shape, F32)],
        compiler_params=_params(("parallel",)),
    )(*ops, s0, lnx_g.reshape(1, c_a), lnx_b.reshape(1, c_a), hsum)


def rope_tables(pos):
    inv_freq = ROPE_THETA ** (-jnp.arange(ROT_HALF, dtype=F32) / ROT_HALF)
    ang = pos[:, None] * inv_freq[None, :]
    cos, sin = jnp.cos(ang), jnp.sin(ang)
    t = pos.shape[0]
    rest = HEAD - 2 * ROT_HALF
    c = jnp.concatenate([cos, cos, jnp.ones((t, rest), F32)], axis=1)
    s1 = jnp.concatenate([jnp.zeros((t, ROT_HALF), F32), sin, jnp.zeros((t, rest), F32)], axis=1)
    s2 = jnp.concatenate([-sin, jnp.zeros((t, HEAD - ROT_HALF), F32)], axis=1)
    rep = LANES // HEAD
    return tuple(jnp.tile(x, (1, rep)) for x in (c, s1, s2))


def _rope(x, c, s1, s2):
    width = x.shape[-1]
    rep = width // c.shape[-1]
    if rep > 1:
        c, s1, s2 = (jnp.concatenate([t] * rep, axis=1) for t in (c, s1, s2))
    return x * c + pltpu.roll(x, ROT_HALF, 1) * s1 + pltpu.roll(x, width - ROT_HALF, 1) * s2


def _attn_kernel(window, q_w, kv_w, zq_ref, kp_ref, vp_ref, rc_ref, rs1_ref, rs2_ref, pc_ref, ps1_ref, ps2_ref,
                 mk_ref, mv_ref, sink_ref, ob_ref, om_ref, kr_ref):
    n = pl.program_id(1)
    zq = zq_ref[0]
    scale = HEAD ** -0.5
    q = zq[:, :q_w]
    k = zq[:, q_w:q_w + kv_w]
    v = zq[:, q_w + kv_w:q_w + 2 * kv_w]
    qm = zq[:, q_w + 2 * kv_w:]
    qr = _rope(q, rc_ref[...], rs1_ref[...], rs2_ref[...]) * scale
    kr = _rope(k, rc_ref[...], rs1_ref[...], rs2_ref[...])
    kr_ref[0] = kr
    kpr = _rope(kp_ref[0], pc_ref[...], ps1_ref[...], ps2_ref[...])
    vp = vp_ref[0]
    gqa = q_w // kv_w
    assert window & (window - 1) == 0
    qi = lax.broadcasted_iota(jnp.int32, (gqa * window, 2 * window), 0) & (window - 1)
    kj = lax.broadcasted_iota(jnp.int32, (gqa * window, 2 * window), 1)
    valid = (kj > qi) & (kj <= qi + window) & ((kj >= window) | (n > 0))
    tdot = lambda a, b: lax.dot_general(a, b, (((1,), (1,)), ((), ())), preferred_element_type=F32)
    dot = lambda a, b: jnp.dot(a, b, preferred_element_type=F32)
    kvs = range(kv_w // HEAD)
    mhs = range(qm.shape[1] // HEAD)
    hsl = lambda h: slice(h * HEAD, (h + 1) * HEAD)
    qmb = (qm * scale).astype(BF16)
    mk = mk_ref[0].astype(BF16)
    mv = mv_ref[0].astype(BF16)
    kcat = [jnp.concatenate([kpr[:, hsl(hk)], kr[:, hsl(hk)]], axis=0).astype(BF16) for hk in kvs]
    vcat = [jnp.concatenate([vp[:, hsl(hk)], v[:, hsl(hk)]], axis=0).astype(BF16) for hk in kvs]
    qs = [jnp.concatenate([qr[:, hsl(hk * gqa + g)] for g in range(gqa)], axis=0).astype(BF16) for hk in kvs]
    s = [jnp.where(valid, tdot(qs[hk], kcat[hk]), NEG_INF) for hk in kvs]
    sm = [tdot(qmb[:, hsl(h)], mk[:, hsl(h)]) for h in mhs]
    sink = [jnp.concatenate([jnp.full((window, 1), sink_ref[hk * gqa + g], F32) for g in range(gqa)], axis=0)
            for hk in kvs]
    m = [jnp.maximum(jnp.max(s[hk], axis=-1, keepdims=True), sink[hk]) for hk in kvs]
    p = [jnp.exp(s[hk] - m[hk]) for hk in kvs]
    pm = [jnp.exp(sm[h] - jnp.max(sm[h], axis=-1, keepdims=True)) for h in mhs]
    o = [dot(p[hk].astype(BF16), vcat[hk])
         / (jnp.sum(p[hk], axis=-1, keepdims=True) + jnp.exp(sink[hk] - m[hk])) for hk in kvs]
    om = [dot(pm[h].astype(BF16), mv[:, hsl(h)]) / jnp.sum(pm[h], axis=-1, keepdims=True) for h in mhs]
    ob_ref[0] = jnp.concatenate([o[hk][g * window:(g + 1) * window] for hk in kvs for g in range(gqa)], axis=1)
    om_ref[0] = jnp.concatenate(om, axis=1)


def attn_prompt(zq, mk, mv, sinks, tables, window, q_w, kv_w):
    bsz, t, zw = zq.shape
    mem_w = zw - q_w - 2 * kv_w
    assert kv_w == LANES and q_w % kv_w == 0
    kcol, vcol = q_w // kv_w, q_w // kv_w + 1
    prev = lambda n: jnp.maximum(n - 1, 0)
    tab = pl.BlockSpec((window, LANES), lambda b, n: (n, 0))
    ptab = pl.BlockSpec((window, LANES), lambda b, n: (prev(n), 0))
    mem = pl.BlockSpec((1,) + mk.shape[1:], lambda b, n: (b, 0, 0))
    return pl.pallas_call(
        functools.partial(_attn_kernel, window, q_w, kv_w), name="attn",
        grid=(bsz, t // window),
        in_specs=[pl.BlockSpec((1, window, zw), lambda b, n: (b, n, 0)),
                  pl.BlockSpec((1, window, kv_w), lambda b, n: (b, prev(n), kcol)),
                  pl.BlockSpec((1, window, kv_w), lambda b, n: (b, prev(n), vcol)),
                  tab, tab, tab, ptab, ptab, ptab, mem, mem,
                  pl.BlockSpec(memory_space=pltpu.SMEM)],
        out_specs=[pl.BlockSpec((1, window, q_w), lambda b, n: (b, n, 0)),
                   pl.BlockSpec((1, window, mem_w), lambda b, n: (b, n, 0)),
                   pl.BlockSpec((1, window, kv_w), lambda b, n: (b, n, 0))],
        out_shape=[jax.ShapeDtypeStruct((bsz, t, q_w), F32), jax.ShapeDtypeStruct((bsz, t, mem_w), F32),
                   jax.ShapeDtypeStruct((bsz, t, kv_w), F32)],
        compiler_params=_params(("parallel", "parallel")),
    )(zq, zq, zq, *tables, *tables, mk, mv, sinks)


def _attn_step_kernel(q_w, kv_w, zq_ref, ck_ref, cv_ref, mk_ref, mv_ref, rc_ref, rs1_ref, rs2_ref, sink_ref,
                      ob_ref, om_ref, nk_ref, nv_ref):
    bt = zq_ref.shape[0]
    window = ck_ref.shape[1]
    mem_w = om_ref.shape[1]
    n_q, gqa, per_vreg = q_w // HEAD, q_w // kv_w, LANES // HEAD
    scale = HEAD ** -0.5
    zq = zq_ref[...]
    q = _rope(zq[:, :q_w], rc_ref[...], rs1_ref[...], rs2_ref[...]) * scale
    k_new = _rope(zq[:, q_w:q_w + kv_w], rc_ref[...], rs1_ref[...], rs2_ref[...])
    v_new = zq[:, q_w + kv_w:q_w + 2 * kv_w]
    qm = zq[:, q_w + 2 * kv_w:] * scale
    own = lambda w: (lax.broadcasted_iota(jnp.int32, (n_q, w), 1) // HEAD
                     == lax.broadcasted_iota(jnp.int32, (n_q, w), 0))
    own_q, own_m = own(q_w), own(mem_w)
    hrow = lax.broadcasted_iota(jnp.int32, (n_q, LANES), 0)
    hblk = lax.broadcasted_iota(jnp.int32, (n_q, LANES), 1) // HEAD
    swap = (hrow % per_vreg) != (hrow // gqa)
    keep = hblk == hrow % per_vreg
    key_ok = lax.broadcasted_iota(jnp.int32, (n_q, window), 1) >= 1
    wrow = lax.broadcasted_iota(jnp.int32, (window, kv_w), 0)
    sink = sink_ref[...]
    bs = range(bt)
    tdot = lambda a, b: lax.dot_general(a.astype(BF16), b.astype(BF16), (((1,), (1,)), ((), ())),
                                        preferred_element_type=F32)
    for b in bs:
        nk_ref[b] = jnp.where(wrow == window - 1, k_new[b:b + 1], pltpu.roll(ck_ref[b], window - 1, 0))
        nv_ref[b] = jnp.where(wrow == window - 1, v_new[b:b + 1], pltpu.roll(cv_ref[b], window - 1, 0))
    q8 = []
    for b in bs:
        rep = jnp.where(own_q, q[b:b + 1], 0.0)
        fold = sum(rep[:, c * LANES:(c + 1) * LANES] for c in range(q_w // LANES))
        q8.append(jnp.where(swap, pltpu.roll(fold, HEAD, 1), fold))
    qm8 = [jnp.where(own_m, qm[b:b + 1], 0.0) for b in bs]
    s = [jnp.where(key_ok, tdot(q8[b], ck_ref[b]), NEG_INF) for b in bs]
    sm = [tdot(qm8[b], mk_ref[b]) for b in bs]
    s_new = [jnp.sum(q8[b] * k_new[b:b + 1], axis=1, keepdims=True) for b in bs]
    m = [jnp.maximum(jnp.maximum(jnp.max(s[b], axis=1, keepdims=True), s_new[b]), sink) for b in bs]
    p = [jnp.exp(s[b] - m[b]) for b in bs]
    p_new = [jnp.exp(s_new[b] - m[b]) for b in bs]
    pm = [jnp.exp(sm[b] - jnp.max(sm[b], axis=1, keepdims=True)) for b in bs]
    o8 = [(_bdot(p[b], cv_ref[b]) + p_new[b] * v_new[b:b + 1])
          / (jnp.sum(p[b], axis=1, keepdims=True) + p_new[b] + jnp.exp(sink - m[b])) for b in bs]
    om8 = [_bdot(pm[b], mv_ref[b]) / jnp.sum(pm[b], axis=1, keepdims=True) for b in bs]
    ob, om = [], []
    for b in bs:
        o = jnp.where(swap, pltpu.roll(o8[b], HEAD, 1), o8[b])
        o = jnp.concatenate([jnp.where(keep, o, 0.0)] * (q_w // LANES), axis=1)
        ob.append(jnp.sum(jnp.where(own_q, o, 0.0), axis=0, keepdims=True))
        om.append(jnp.sum(jnp.where(own_m, om8[b], 0.0), axis=0, keepdims=True))
    ob_ref[...] = jnp.concatenate(ob, axis=0)
    om_ref[...] = jnp.concatenate(om, axis=0)


def attn_step(zq, ck, cv, mk, mv, sinks, tables, q_w, kv_w, bt):
    bsz, zw = zq.shape
    mem_w = zw - q_w - 2 * kv_w
    window, n_mem = ck.shape[1], mk.shape[1]
    n_q = q_w // HEAD
    assert kv_w == LANES and mem_w // HEAD <= n_q and n_q == 8
    rows = lambda w: pl.BlockSpec((bt, w), lambda i: (i, 0))
    cache = pl.BlockSpec((bt, window, kv_w), lambda i: (i, 0, 0))
    mem = pl.BlockSpec((bt, n_mem, mem_w), lambda i: (i, 0, 0))
    consts = list(tables) + [sinks.reshape(n_q, 1)]
    const_specs = [_const_spec(c.shape) for c in consts]
    return pl.pallas_call(
        functools.partial(_attn_step_kernel, q_w, kv_w), name="attn_step",
        grid=(bsz // bt,),
        in_specs=[rows(zw), cache, cache, mem, mem] + const_specs,
        out_specs=[rows(q_w), rows(mem_w), cache, cache],
        out_shape=[jax.ShapeDtypeStruct((bsz, q_w), F32), jax.ShapeDtypeStruct((bsz, mem_w), F32),
                   jax.ShapeDtypeStruct(ck.shape, F32), jax.ShapeDtypeStruct(cv.shape, F32)],
        compiler_params=_params(("parallel",)),
    )(zq, ck, cv, mk, mv, *consts)


def _proj_kernel(x_ref, w_ref, o_ref):
    o_ref[...] = jnp.dot(x_ref[...].astype(BF16), w_ref[...], preferred_element_type=F32)


def proj(x, w, tm):
    n, d = x.shape
    return pl.pallas_call(
        _proj_kernel, name="proj",
        grid=(n // tm,),
        in_specs=[pl.BlockSpec((tm, d), lambda i: (i, 0)), _const_spec(w.shape)],
        out_specs=pl.BlockSpec((tm, w.shape[1]), lambda i: (i, 0)),
        out_shape=jax.ShapeDtypeStruct((n, w.shape[1]), F32),
        compiler_params=_params(("parallel",)),
    )(x, w)


def _merge_kernel(alpha, n_tiles, x_ref, oa_ref, ob_ref, om_ref, gt_ref, lig_ref, lib_ref, pa_ref, pb_ref, pm_ref,
                  wo_ref, l1g_ref, l1b_ref, wr_ref, br_ref, *refs):
    outs = refs[-4:]

    @pl.when(pl.program_id(0) >= n_tiles)
    def _():
        for ref in outs:
            ref[...] = jnp.zeros_like(ref)

    @pl.when(pl.program_id(0) < n_tiles)
    def _():
        _merge_tile(alpha, x_ref, oa_ref, ob_ref, om_ref, gt_ref, lig_ref, lib_ref, pa_ref, pb_ref, pm_ref, wo_ref,
                    l1g_ref, l1b_ref, wr_ref, br_ref, *outs)


def _merge_tile(alpha, x_ref, oa_ref, ob_ref, om_ref, gt_ref, lig_ref, lib_ref, pa_ref, pb_ref, pm_ref, wo_ref,
                l1g_ref, l1b_ref, wr_ref, br_ref, x1_ref, x1t_ref, eidx_ref, gate_ref):
    d = x_ref.shape[1]
    xn = _ln(x_ref[...], lig_ref[...], lib_ref[...])
    gts = jax.nn.sigmoid(gt_ref[...])
    merged = (gts[:, :d] * _bdot(oa_ref[...], pa_ref[...]) + gts[:, d:2 * d] * _bdot(ob_ref[...], pb_ref[...])
              + gts[:, 2 * d:] * _bdot(om_ref[...], pm_ref[...]))
    x1 = _ln(alpha * xn + _bdot(merged, wo_ref[...]), l1g_ref[...], l1b_ref[...])
    x1_ref[...] = x1
    _rows_to_tiles(x1t_ref, x1)
    x_hi, x_lo = _pieces(x1, 2)
    w_hi, w_lo = _pieces(wr_ref[...], 2)
    dot = lambda a, b: jnp.dot(a, b, preferred_element_type=F32)
    logits = dot(x_hi, w_hi) + dot(x_hi, w_lo) + dot(x_lo, w_hi) + br_ref[...]
    lane = lax.broadcasted_iota(jnp.int32, logits.shape, 1)
    lane_f = lane.astype(F32)
    first = lambda hit: jnp.min(jnp.where(hit, lane_f, float(LANES)), axis=-1, keepdims=True).astype(jnp.int32)
    gmask = lane < N_GROUPS
    gl = jnp.where(gmask, logits, NEG_INF)
    gmax = jnp.max(gl, axis=-1, keepdims=True)
    gidx = first(gl == gmax)
    g_w = 1.0 / jnp.sum(jnp.where(gmask, jnp.exp(gl - gmax), 0.0), axis=-1, keepdims=True)
    lo = N_GROUPS + gidx * EXPERTS_PER_GROUP
    el = jnp.where((lane >= lo) & (lane < lo + EXPERTS_PER_GROUP), logits, NEG_INF)
    v1 = jnp.max(el, axis=-1, keepdims=True)
    i1 = first(el == v1)
    el2 = jnp.where(lane == i1, NEG_INF, el)
    v2 = jnp.max(el2, axis=-1, keepdims=True)
    i2 = first(el2 == v2)
    e2 = jnp.exp(v2 - v1)
    gate1 = g_w / (1.0 + e2)
    eidx_ref[...] = jnp.where(lane == 0, i1 - N_GROUPS, jnp.where(lane == 1, i2 - N_GROUPS, 0))
    gate_ref[...] = jnp.where(lane == 0, gate1, jnp.where(lane == 1, gate1 * e2, 0.0))


def merge(x, oa, ob, om, gt, w, tm, alpha, n_total, row_offset=0, into=None):
    n, d = x.shape
    assert row_offset % tm == 0 and n % tm == 0 and n_total % tm == 0
    off = row_offset // tm
    into = list(into or [])
    n_tiles = n // tm
    steps = n_tiles if into else n_total // tm
    rows = lambda a: pl.BlockSpec((tm, a.shape[1]), lambda i: (jnp.minimum(i, n_tiles - 1), 0))
    out = lambda width: pl.BlockSpec((tm, width), lambda i: (i + off, 0))
    consts = [w['ln_in_g'], w['ln_in_b'], w['p_a'], w['p_b'], w['p_m'], w['w_o'], w['ln1_g'], w['ln1_b'],
              w['w_route'], w['b_route']]
    n_in = 5 + len(consts)
    return pl.pallas_call(
        functools.partial(_merge_kernel, alpha, n_tiles), name="merge",
        grid=(steps,),
        in_specs=[rows(a) for a in (x, oa, ob, om, gt)] + [_const_spec(c.shape) for c in consts]
        + [pl.BlockSpec(memory_space=pl.ANY)] * len(into),
        out_specs=[out(d), pl.BlockSpec((tm * TILE_ROWS, LANES), lambda i: (i + off, 0)), out(LANES), out(LANES)],
        out_shape=[jax.ShapeDtypeStruct((n_total, d), F32), jax.ShapeDtypeStruct((n_total * TILE_ROWS, LANES), F32),
                   jax.ShapeDtypeStruct((n_total, LANES), jnp.int32), jax.ShapeDtypeStruct((n_total, LANES), F32)],
        input_output_aliases={n_in + k: k for k in range(len(into))},
        compiler_params=_params(("parallel",)),
    )(x, oa, ob, om, gt, *consts, *into)


ROW_DMA_UNROLL = 8
DRAIN_STEPS = 2


TILE_ROWS = 8


def _rows_from_tiles(ref, n):
    return jnp.concatenate([ref[pl.ds(s, n, stride=TILE_ROWS), :] for s in range(TILE_ROWS)], axis=1)


def _rows_to_tiles(ref, x):
    for s in range(TILE_ROWS):
        ref[pl.ds(s, x.shape[0], stride=TILE_ROWS), :] = x[:, s * LANES:(s + 1) * LANES]


def _row_copies(asg_ref, base, count, n_asg, x_hbm, buf, y_hbm, sem, gather, unrolled):
    def tile(ref, idx):
        start = idx * TILE_ROWS
        return ref.at[pl.ds(start if isinstance(idx, int) else pl.multiple_of(start, TILE_ROWS), TILE_ROWS)]

    def one(r, priority):
        a = asg_ref[base + r]
        if gather:
            tok = jnp.minimum(a, n_asg - 1)
            tok = jnp.where(tok >= n_asg // 2, tok - n_asg // 2, tok)
            copy = pltpu.make_async_copy(tile(x_hbm, tok), tile(buf, r), sem)
        else:
            copy = pltpu.make_async_copy(tile(buf, r), tile(y_hbm, a), sem)
        copy.start(priority=priority)

    if unrolled:
        for r in range(count):
            one(r, r % 2)
        return

    def body(g, carry):
        for j in range(ROW_DMA_UNROLL):
            one(g * ROW_DMA_UNROLL + j, j % 2)
        return carry
    lax.fori_loop(0, count // ROW_DMA_UNROLL, body, 0)


def _moe_expert_kernel(n_asg, asg_ref, be_ref, nu_ref, x_hbm, wg_ref, wu_ref, wd_ref, y_hbm,
                       xbuf, ybuf, wgb, wub, wdb, gsem, ssem):
    i = pl.program_id(0)
    used = nu_ref[0]
    rows = xbuf.shape[1] // TILE_ROWS
    n_blocks = be_ref.shape[0]
    blk = jnp.minimum(i, n_blocks - 1)

    @pl.when((i == 0) | (be_ref[blk] != be_ref[jnp.maximum(blk - 1, 0)]))
    def _():
        wgb[...] = wg_ref[0, 0].astype(BF16)
        wub[...] = wu_ref[0, 0].astype(BF16)
        wdb[...] = wd_ref[0, 0].astype(BF16)

    def wait_gather(slot):
        pltpu.make_async_copy(x_hbm.at[pl.ds(0, rows * TILE_ROWS)], xbuf.at[slot], gsem.at[slot]).wait()

    def wait_scatter(slot):
        pltpu.make_async_copy(ybuf.at[slot], y_hbm.at[pl.ds(0, rows * TILE_ROWS)], ssem.at[slot]).wait()

    def gather(b, slot, unrolled):
        _row_copies(asg_ref, b * rows, rows, n_asg, x_hbm, xbuf.at[slot], y_hbm, gsem.at[slot], True, unrolled)

    def scatter(b, slot, unrolled):
        _row_copies(asg_ref, b * rows, rows, n_asg, x_hbm, ybuf.at[slot], y_hbm, ssem.at[slot], False, unrolled)

    def expert(slot):
        xb = _rows_from_tiles(xbuf.at[slot], rows).astype(BF16)
        hg = jnp.dot(xb, wgb[...], preferred_element_type=F32)
        hu = jnp.dot(xb, wub[...], preferred_element_type=F32)
        h = hg * jax.nn.sigmoid(hg) * hu
        _rows_to_tiles(ybuf.at[slot], jnp.dot(h.astype(BF16), wdb[...], preferred_element_type=F32))

    @pl.when(i == 0)
    def _():
        ybuf[1] = jnp.zeros(ybuf.shape[1:], F32)
        spare = [pltpu.make_async_copy(ybuf.at[1],
                                       y_hbm.at[pl.ds((n_asg + j * rows) * TILE_ROWS, rows * TILE_ROWS)], ssem.at[1])
                 for j in range((y_hbm.shape[0] // TILE_ROWS - n_asg) // rows)]
        for copy in spare:
            copy.start()
        for copy in spare:
            copy.wait()

    @pl.when((i >= 2) & (i - 2 < used))
    def _():
        wait_scatter(i % 2)

    steady = (i >= 1) & (i + 1 < used)
    for slot in range(2):
        @pl.when(steady & (i % 2 == slot))
        def _():
            wait_gather(slot)
            gather(i + 1, 1 - slot, True)
            scatter(i - 1, 1 - slot, True)
            expert(slot)

    @pl.when(jnp.logical_not(steady))
    def _():
        slot = i % 2

        @pl.when((i == 0) & (used > 0))
        def _():
            gather(0, 0, False)

        @pl.when(i < used)
        def _():
            wait_gather(slot)

        @pl.when(i + 1 < used)
        def _():
            gather(i + 1, 1 - slot, False)

        @pl.when((i >= 1) & (i - 1 < used))
        def _():
            scatter(i - 1, 1 - slot, False)

        @pl.when(i < used)
        def _():
            expert(slot)


def moe_experts(x1_tiles, n_asg, asg, blk_e, n_used, e_gate, e_up, e_down):
    d = e_gate.shape[2]
    assert d == TILE_ROWS * LANES and x1_tiles.shape[1] == LANES
    n_blocks = blk_e.shape[0]
    ff = e_gate.shape[-1]
    n_rows = n_asg + e_gate.shape[1] * EXPERT_BLOCK
    weight = lambda shape: pl.BlockSpec(
        (1, 1) + shape, lambda i, asg, be, nu: (0, be[jnp.minimum(i, n_blocks - 1)], 0, 0))
    return pl.pallas_call(
        functools.partial(_moe_expert_kernel, n_asg), name="moe_expert",
        grid_spec=pltpu.PrefetchScalarGridSpec(
            num_scalar_prefetch=3,
            grid=(n_blocks + DRAIN_STEPS,),
            in_specs=[pl.BlockSpec(memory_space=pl.ANY), weight((d, ff)), weight((d, ff)), weight((ff, d))],
            out_specs=pl.BlockSpec(memory_space=pl.ANY),
            scratch_shapes=[pltpu.VMEM((2, EXPERT_BLOCK * TILE_ROWS, LANES), F32),
                            pltpu.VMEM((2, EXPERT_BLOCK * TILE_ROWS, LANES), F32),
                            pltpu.VMEM((d, ff), BF16), pltpu.VMEM((d, ff), BF16), pltpu.VMEM((ff, d), BF16),
                            pltpu.SemaphoreType.DMA((2,)), pltpu.SemaphoreType.DMA((2,))]),
        out_shape=jax.ShapeDtypeStruct((n_rows * TILE_ROWS, LANES), F32),
        compiler_params=_params(("arbitrary",)),
    )(asg, blk_e, n_used, x1_tiles, e_gate, e_up, e_down)


def _moe_combine_kernel(alpha, lead_tiles, y0_ref, y1_ref, x1_ref, gate_ref, g_ref, b_ref, lead_ref, tail_ref):
    i = pl.program_id(0)
    gate = gate_ref[...]
    tm = x1_ref.shape[0]
    moe = gate[:, 0:1] * _rows_from_tiles(y0_ref, tm) + gate[:, 1:2] * _rows_from_tiles(y1_ref, tm)
    out = _ln(alpha * x1_ref[...] + moe, g_ref[...], b_ref[...])

    @pl.when(i < lead_tiles)
    def _():
        lead_ref[...] = out

    @pl.when(i >= lead_tiles)
    def _():
        tail_ref[...] = out


def moe_combine(y, x1, gate, g, b, tm, alpha, n, n_lead):
    d = x1.shape[1]
    assert n_lead % tm == 0 and (n - n_lead) % tm == 0 and n > n_lead
    lead_tiles = n_lead // tm
    return pl.pallas_call(
        functools.partial(_moe_combine_kernel, alpha, lead_tiles), name="moe_combine",
        grid=(n // tm,),
        in_specs=[pl.BlockSpec((tm * TILE_ROWS, LANES), lambda i: (i, 0)),
                  pl.BlockSpec((tm * TILE_ROWS, LANES), lambda i: (i + n // tm, 0)),
                  pl.BlockSpec((tm, d), lambda i: (i, 0)),
                  pl.BlockSpec((tm, LANES), lambda i: (i, 0)), _const_spec((1, d)), _const_spec((1, d))],
        out_specs=[pl.BlockSpec((tm, d), lambda i: (jnp.minimum(i, lead_tiles - 1), 0)),
                   pl.BlockSpec((tm, d), lambda i: (jnp.maximum(i - lead_tiles, 0), 0))],
        out_shape=[jax.ShapeDtypeStruct((n_lead, d), F32), jax.ShapeDtypeStruct((n - n_lead, d), F32)],
        compiler_params=_params(("arbitrary",)),
    )(y, y, x1, gate, g.reshape(1, d), b.reshape(1, d))


def moe_routing(experts):
    n, top_k = experts.shape
    n_exp = N_GROUPS * EXPERTS_PER_GROUP
    a = n * top_k
    flat_e = experts.T.reshape(a)
    onehot = (flat_e[:, None] == jnp.arange(n_exp, dtype=jnp.int32)[None, :]).astype(jnp.int32)
    rank = jnp.take_along_axis(jnp.cumsum(onehot, axis=0) - onehot, flat_e[:, None], axis=1)[:, 0]
    counts = jnp.sum(onehot, axis=0)
    padded = (counts + EXPERT_BLOCK - 1) // EXPERT_BLOCK * EXPERT_BLOCK
    pad_end = jnp.cumsum(padded)
    dest = (pad_end - padded)[flat_e] + rank
    n_blocks = -(-a // EXPERT_BLOCK) + n_exp
    blk_start = jnp.arange(n_blocks, dtype=jnp.int32) * EXPERT_BLOCK
    blk_e = jnp.minimum(jnp.sum((pad_end[None, :] <= blk_start[:, None]).astype(jnp.int32), axis=1), n_exp - 1)
    spare = jnp.arange(n_blocks * EXPERT_BLOCK, dtype=jnp.int32) - jnp.repeat(jnp.cumsum(counts)[blk_e], EXPERT_BLOCK)
    spare = a + jnp.clip(spare, 0, n_exp * EXPERT_BLOCK - 1)
    asg = spare.astype(jnp.int32).at[dest].set(jnp.arange(a, dtype=jnp.int32))
    n_used = (pad_end[-1:] // EXPERT_BLOCK).astype(jnp.int32)
    return asg, blk_e.astype(jnp.int32), n_used


def hier_moe_ln(x1, x1_tiles, eidx, gate, w, tm, alpha, n_tokens, n_lead):
    asg, blk_e, n_used = moe_routing(eidx[:n_tokens, :2])
    y = moe_experts(x1_tiles, 2 * n_tokens, asg, blk_e, n_used, w['e_gate'], w['e_up'], w['e_down'])
    return moe_combine(y, x1, gate, w['ln2_g'], w['ln2_b'], tm, alpha, n_tokens, n_lead)


def kernel(x_prompt, x_sample, mem_prompt, state_wkv, state_shift, cache_win_k, cache_win_v, cache_mem_k, cache_mem_v, ln_in_g, ln_in_b, w_in, mu, w0, w_up, a0, a_up, g_up, k_k, k_a, r_k, lnx_g, lnx_b, sinks, w_mem_kv, p_a, p_b, p_m, w_o, ln1_g, ln1_b, w_group, b_group, w_router, b_router, e_gate, e_up, e_down, ln2_g, ln2_b):
    depth = w_in.shape[0]
    assert depth == 1, "single-layer step"
    bsz, seq, d = x_prompt.shape
    dec = x_sample.shape[0]
    assert x_sample.shape[1] == 1
    c_shift = mu.shape[-1]
    c_a = w0.shape[-1]
    window, kv_w = cache_win_k.shape[2], cache_win_k.shape[3] * cache_win_k.shape[4]
    n_mem, mem_w = cache_mem_k.shape[2], cache_mem_k.shape[3] * cache_mem_k.shape[4]
    q_w = sinks.shape[-1] * HEAD
    qkvm_w = q_w + 2 * kv_w + mem_w
    alpha = (2.0 * depth) ** 0.25
    past_len = float(PAST_LEN)
    chunk = 64

    w_in_b = w_in[0].astype(BF16)
    w_parts = [w_in_b[:, :c_shift], w_in_b[:, c_shift:c_shift + qkvm_w], w_in_b[:, c_shift + qkvm_w:]]
    rp = dict(mu=mu[0], w0=w0[0], w_up=w_up[0], a0=a0[0], a_up=a_up[0], g_up=g_up[0], k_k=k_k[0], k_a=k_a[0],
              r_k=r_k[0].reshape(-1))
    n_route = N_GROUPS * (1 + EXPERTS_PER_GROUP)
    mw = dict(ln_in_g=ln_in_g.reshape(1, d), ln_in_b=ln_in_b.reshape(1, d), p_a=p_a[0].astype(BF16),
              p_b=p_b[0].astype(BF16), p_m=p_m[0].astype(BF16), w_o=w_o[0].astype(BF16),
              ln1_g=ln1_g[0].reshape(1, d), ln1_b=ln1_b[0].reshape(1, d),
              w_route=jnp.pad(jnp.concatenate([w_group[0], w_router[0]], axis=1), ((0, 0), (0, LANES - n_route))),
              b_route=jnp.pad(jnp.concatenate([b_group[0], b_router[0]]), (0, LANES - n_route)).reshape(1, LANES),
              e_gate=e_gate, e_up=e_up, e_down=e_down, ln2_g=ln2_g[0], ln2_b=ln2_b[0])

    xp = x_prompt.reshape(bsz * seq, d)
    zr, zq, zg = ln_proj(xp, ln_in_g, ln_in_b, w_parts, 256)
    zr3 = zr.reshape(bsz, seq, c_shift)
    prep = rwkv_prep(zr3, jnp.zeros((bsz, 1, c_shift), F32), rp, chunk, 256)
    o_a, wkv_p = wkv(prep, jnp.zeros((bsz, c_a // HEAD, HEAD, HEAD), F32), lnx_g[0], lnx_b[0], chunk)
    mkv = proj(mem_prompt.reshape(bsz * n_mem, d), w_mem_kv[0].astype(BF16), 256).reshape(bsz, n_mem, 2 * mem_w)
    mk_p, mv_p = mkv[..., :mem_w], mkv[..., mem_w:]
    tables = rope_tables(jnp.arange(seq, dtype=F32))
    o_b, o_m, k_rot = attn_prompt(zq.reshape(bsz, seq, qkvm_w), mk_p, mv_p, sinks[0], tables, window, q_w, kv_w)
    n_all = bsz * seq + dec
    n_buf = -(-n_all // 256) * 256
    routed = merge(xp, o_a.reshape(-1, c_a), o_b.reshape(-1, q_w), o_m.reshape(-1, mem_w), zg, mw, 256, alpha, n_buf)
    shift_p = zr3[:, -1]
    kb_p = k_rot[:, -window:].reshape(bsz, window, H_KV, HEAD)
    vb_p = zq.reshape(bsz, seq, qkvm_w)[:, -window:, q_w + kv_w:q_w + 2 * kv_w].reshape(bsz, window, H_KV, HEAD)

    xs = x_sample.reshape(dec, d)
    zr_s, zq_s, zg_s = ln_proj(xs, ln_in_g, ln_in_b, w_parts, dec)
    ops_s = rwkv_prep(zr_s.reshape(1, dec, c_shift), state_shift[0].reshape(1, dec, c_shift), rp, 1, dec)
    o_a_s, wkv_s = wkv_step([a.reshape(dec, c_a) for a in ops_s], state_wkv[0], lnx_g[0], lnx_b[0], 8)
    tables_s = rope_tables(jnp.full((1,), past_len, F32))
    o_b_s, o_m_s, nk_s, nv_s = attn_step(
        zq_s, cache_win_k[0].reshape(dec, window, kv_w), cache_win_v[0].reshape(dec, window, kv_w),
        cache_mem_k[0].reshape(dec, n_mem, mem_w), cache_mem_v[0].reshape(dec, n_mem, mem_w),
        sinks[0], tables_s, q_w, kv_w, 8)
    x1, x1_tiles, eidx, gate = merge(xs, o_a_s, o_b_s, o_m_s, zg_s, mw, dec, alpha, n_buf, bsz * seq, routed)

    y_prompt, y_sample = hier_moe_ln(x1, x1_tiles, eidx, gate, mw, dec, alpha, n_all, bsz * seq)
    y_prompt = y_prompt.reshape(bsz, seq, d)
    y_sample = y_sample.reshape(dec, 1, d)

    sd = state_wkv.dtype
    return (y_prompt, y_sample, wkv_p[None].astype(sd), wkv_s[None].astype(sd), shift_p[None], zr_s[None],
            kb_p[None], vb_p[None], nk_s.reshape(dec, window, H_KV, HEAD)[None],
            nv_s.reshape(dec, window, H_KV, HEAD)[None],
            mk_p.reshape(bsz, n_mem, -1, HEAD)[None], mv_p.reshape(bsz, n_mem, -1, HEAD)[None])
```

```python
import functools
import math

import jax
import jax.numpy as jnp
from jax import lax
from jax.experimental import pallas as pl
from jax.experimental.pallas import tpu as pltpu

F32 = jnp.float32
BF16 = jnp.bfloat16
SCAN_BATCH = 4

HEAD = 64
LANES = 128
H_KV = 2
ROT_HALF = 8
ROPE_THETA = 500000.0
PAST_LEN = 8192
N_GROUPS = 4
EXPERTS_PER_GROUP = 8
EXPERT_BLOCK = 128
LN_EPS = 1e-5
LNX_EPS = 64e-5
NEG_INF = -1e30
VMEM_LIMIT = 48 * 1024 * 1024


def _pieces(x, n):
    out = []
    for _ in range(n):
        p = x.astype(BF16)
        out.append(p)
        x = x - p.astype(F32)
    return out


def _mask_dot(x, mask, n=2):
    return sum(jnp.dot(p, mask, preferred_element_type=F32) for p in _pieces(x, n))


def _split3(x, axis, lhs):
    hi = x.astype(BF16).astype(F32)
    lo = x - hi
    return jnp.concatenate([hi, hi, lo] if lhs else [hi, lo, hi], axis=axis).astype(BF16)


def _dot3(a, b):
    return jnp.dot(_split3(a, 1, True), _split3(b, 0, False), preferred_element_type=F32)


def _dot3_t(a, b):
    return lax.dot_general(_split3(a, 1, True), _split3(b, 1, False), (((1,), (1,)), ((), ())),
                           preferred_element_type=F32)


def _bdot(a, b):
    return jnp.dot(a.astype(BF16), b.astype(BF16), preferred_element_type=F32)


def _bdot_t(a, b):
    return lax.dot_general(a.astype(BF16), b.astype(BF16), (((1,), (1,)), ((), ())), preferred_element_type=F32)


def _ln(x, g, b):
    mu = jnp.mean(x, axis=-1, keepdims=True)
    xc = x - mu
    var = jnp.mean(xc * xc, axis=-1, keepdims=True)
    return xc * lax.rsqrt(var + LN_EPS) * g + b


def _const_spec(shape):
    nd = len(shape)
    return pl.BlockSpec(shape, lambda *_: (0,) * nd)


def _params(sem):
    return pltpu.CompilerParams(dimension_semantics=sem, vmem_limit_bytes=VMEM_LIMIT)


def _ln_proj_kernel(x_ref, g_ref, b_ref, *refs):
    nw = len(refs) // 2
    xn = _ln(x_ref[...], g_ref[...], b_ref[...]).astype(BF16)
    for w_ref, o_ref in zip(refs[:nw], refs[nw:]):
        o_ref[...] = jnp.dot(xn, w_ref[...], preferred_element_type=F32).astype(o_ref.dtype)


def ln_proj(x, g, b, ws, tm, out_dtypes):
    n, d = x.shape
    return pl.pallas_call(
        _ln_proj_kernel, name="ln_proj",
        grid=(n // tm,),
        in_specs=[pl.BlockSpec((tm, d), lambda i: (i, 0)), _const_spec((1, d)), _const_spec((1, d))]
        + [_const_spec(w.shape) for w in ws],
        out_specs=[pl.BlockSpec((tm, w.shape[1]), lambda i: (i, 0)) for w in ws],
        out_shape=[jax.ShapeDtypeStruct((n, w.shape[1]), dt) for w, dt in zip(ws, out_dtypes)],
        compiler_params=_params(("parallel",)),
    )(x, g.reshape(1, d), b.reshape(1, d), *ws)


def _rwkv_prep_kernel(chunk, z_ref, prev_ref, mu_ref, w0_ref, wup_ref, a0_ref, aup_ref, gup_ref,
                      kk_ref, ka_ref, rk_ref, hsum_ref, tril_ref, *refs):
    out_refs, carry_ref = refs[:-1], refs[-1]
    z = z_ref[0]
    tt = z.shape[0]
    c_a = w0_ref.shape[-1]
    r_w, r_a, r_g = wup_ref.shape[0], aup_ref.shape[0], gup_ref.shape[0]
    if chunk == 1:
        prev = prev_ref[0]
    else:
        @pl.when(pl.program_id(1) == 0)
        def _():
            carry_ref[...] = prev_ref[0]

        row = lax.broadcasted_iota(jnp.int32, z.shape, 0)
        prev = jnp.where(row == 0, carry_ref[...], pltpu.roll(z, 1, 0))
        carry_ref[...] = z[tt - 1:tt, :]
    zs = z + (prev - z) * mu_ref[...]
    r = zs[:, :c_a]
    k = zs[:, c_a:2 * c_a]
    v = zs[:, 2 * c_a:3 * c_a]
    o = 3 * c_a
    xw = zs[:, o:o + r_w]
    xa = zs[:, o + r_w:o + r_w + r_a]
    xg = zs[:, o + r_w + r_a:o + r_w + r_a + r_g]
    warg = -(w0_ref[...] + _dot3(jnp.tanh(xw), wup_ref[...]))
    softplus = jnp.maximum(warg, 0.0) + jnp.log1p(jnp.exp(-jnp.abs(warg)))
    lw = -jnp.exp(-softplus - 0.5)
    a = jax.nn.sigmoid(a0_ref[...] + _dot3(xa, aup_ref[...]))
    g = _dot3(jax.nn.sigmoid(xg), gup_ref[...])
    kkr = k * kk_ref[...]
    kk = kkr / jnp.maximum(jnp.sqrt(_mask_dot(kkr * kkr, hsum_ref[...])), 1e-12)
    k2 = k * (1.0 + (a - 1.0) * ka_ref[...])
    bonus = _mask_dot(r * k2 * rk_ref[...], hsum_ref[...]) * v
    kb = kk * a
    if chunk == 1:
        outs = (r, jnp.exp(lw), k2, v, -kk, kb, g, bonus)
        for ref, val in zip(out_refs, outs):
            ref[0] = val
        return
    pieces = _pieces(lw, 3)
    ones3 = jnp.ones((chunk, 3 * chunk), BF16)
    cw, cwl = [], []
    for c in range(tt // chunk):
        stack = jnp.concatenate([p[c * chunk:(c + 1) * chunk] for p in pieces], axis=0)
        cw.append(jnp.dot(tril_ref[...], stack, preferred_element_type=F32))
        cwl.append(jnp.dot(ones3, stack, preferred_element_type=F32))
    cw = jnp.concatenate(cw, axis=0)
    cwl = jnp.concatenate(cwl, axis=0)
    e_inv = jnp.exp(-cw)
    e_hat = jnp.exp(cwl - cw)
    outs = (r * jnp.exp(cw), -kk * jnp.exp(cw - lw), kb * e_inv, k2 * e_inv, kb * e_hat, k2 * e_hat, v, g, bonus)
    for ref, val in zip(out_refs[:-1], outs):
        ref[0] = val
    wl = jnp.exp(cwl)
    for c in range(tt // chunk):
        out_refs[-1][0, c] = wl[c * chunk:c * chunk + 1, :]


def _head_sum_matrix(width):
    idx = jnp.arange(width)
    return ((idx[:, None] // HEAD) == (idx[None, :] // HEAD)).astype(BF16)


def rwkv_prep(z, prev, p, chunk, tt):
    bsz, t, cs = z.shape
    c_a = p['w0'].shape[-1]
    assert prev.shape[1] == (t if chunk == 1 else 1)
    ridx = jnp.arange(chunk)
    tril = jnp.tile((ridx[None, :] <= ridx[:, None]).astype(BF16), (1, 3))
    hsum = _head_sum_matrix(c_a)
    row = lambda x: x.reshape(1, -1)
    tile = pl.BlockSpec((1, tt, c_a), lambda b, i: (b, i, 0))
    full = jax.ShapeDtypeStruct((bsz, t, c_a), F32)
    if chunk == 1:
        out_specs, out_shape = [tile] * 8, [full] * 8
        prev_spec = pl.BlockSpec((1, tt, cs), lambda b, i: (b, i, 0))
    else:
        out_specs = [tile] * 9 + [pl.BlockSpec((1, tt // chunk, 1, c_a), lambda b, i: (b, i, 0, 0))]
        out_shape = [full] * 9 + [jax.ShapeDtypeStruct((bsz, t // chunk, 1, c_a), F32)]
        prev_spec = pl.BlockSpec((1, 1, cs), lambda b, i: (b, 0, 0))
    consts = [row(p['mu']), row(p['w0']), p['w_up'], row(p['a0']), p['a_up'], p['g_up'], row(p['k_k']),
              row(p['k_a']), row(p['r_k']), hsum, tril]
    return pl.pallas_call(
        functools.partial(_rwkv_prep_kernel, chunk), name="rwkv_prep",
        grid=(bsz, t // tt),
        in_specs=[pl.BlockSpec((1, tt, cs), lambda b, i: (b, i, 0)), prev_spec] + [_const_spec(c.shape) for c in consts],
        out_specs=out_specs,
        out_shape=out_shape,
        scratch_shapes=[pltpu.VMEM((1, cs), F32)],
        compiler_params=_params(("parallel", "arbitrary")),
    )(z, prev, *consts)


def _wkv_chunk_kernel(ra_ref, at_ref, bt_ref, kt_ref, bh_ref, kh_ref, v_ref, wl_ref, rp_ref, y0_ref, m_ref, n_ref):
    length = ra_ref.shape[1]
    heads = ra_ref.shape[2] // HEAD
    row = lax.broadcasted_iota(jnp.int32, (length, length), 0)
    col = lax.broadcasted_iota(jnp.int32, (length, length), 1)
    strict = row > col
    incl = row >= col
    hrow = lax.broadcasted_iota(jnp.int32, (HEAD, HEAD), 0)
    hcol = lax.broadcasted_iota(jnp.int32, (HEAD, HEAD), 1)
    hs = range(heads)
    sls = [slice(h * HEAD, (h + 1) * HEAD) for h in hs]
    gram = [_bdot_t(jnp.concatenate([at_ref[0, :, sl], ra_ref[0, :, sl]], axis=0),
                    jnp.concatenate([bt_ref[0, :, sl], kt_ref[0, :, sl]], axis=0)) for sl in sls]
    a_ab = [jnp.where(strict, g[:length, :length], 0.0) for g in gram]
    a_ak = [jnp.where(strict, g[:length, length:], 0.0) for g in gram]
    a_rb = [jnp.where(incl, g[length:, :length], 0.0) for g in gram]
    a_rk = [jnp.where(incl, g[length:, length:], 0.0) for g in gram]
    inv = [jnp.where(row == col, 1.0, a) for a in a_ab]
    pw = [_bdot(a, a) for a in a_ab]
    av = [_dot3(a_ak[h], v_ref[0, :, sls[h]]) for h in hs]
    y0 = [_dot3(a_rk[h], v_ref[0, :, sls[h]]) for h in hs]
    nk = [_dot3(v_ref[0, :, sl].T, kh_ref[0, :, sl]) for sl in sls]
    for _ in range(int(math.log2(length)) - 2):
        both = [_bdot(jnp.concatenate([pw[h], inv[h]], axis=0), pw[h]) for h in hs]
        pw = [b[:length] for b in both]
        inv = [inv[h] + both[h][length:] for h in hs]
    inv = [inv[h] + _bdot(inv[h], pw[h]) for h in hs]
    pq = [_dot3(inv[h], jnp.concatenate([at_ref[0, :, sls[h]], av[h]], axis=1)) for h in hs]
    ry = [_dot3(a_rb[h], pq[h]) for h in hs]
    mn = [_dot3(pq[h].T, bh_ref[0, :, sls[h]]) for h in hs]
    rp_ref[0] = jnp.concatenate([ra_ref[0, :, sls[h]] + ry[h][:, :HEAD] for h in hs], axis=1)
    y0_ref[0] = jnp.concatenate([y0[h] + ry[h][:, HEAD:] for h in hs], axis=1)
    m_ref[0, 0] = jnp.concatenate(
        [jnp.where(hrow == hcol, wl_ref[0, 0, :, sls[h]], 0.0) + mn[h][:HEAD] for h in hs], axis=1)
    n_ref[0, 0] = jnp.concatenate([nk[h] + mn[h][HEAD:] for h in hs], axis=1)


def _head_norm_wide(y, hsum):
    yc = y - _mask_dot(y, hsum) * (1.0 / HEAD)
    return yc * lax.rsqrt(_mask_dot(yc * yc, hsum) * (1.0 / HEAD) + LNX_EPS)


def _wkv_scan_kernel(rp_ref, y0_ref, m_ref, n_ref, g_ref, bonus_ref, s0_ref, lg_ref, lb_ref, hsum_ref,
                     o_ref, sout_ref, s_ref):
    c = pl.program_id(1)
    nb = rp_ref.shape[0]
    heads = rp_ref.shape[2] // HEAD

    @pl.when(c == 0)
    def _():
        s_ref[...] = s0_ref[...]

    pairs = [(b, h, slice(h * HEAD, (h + 1) * HEAD)) for b in range(nb) for h in range(heads)]
    s = [s_ref[b, h] for b, h, _ in pairs]
    s_new = [n_ref[b, 0, :, sl] + _dot3(s[j], m_ref[b, 0, :, sl]) for j, (b, h, sl) in enumerate(pairs)]
    ys = [y0_ref[b, :, sl] + _dot3_t(rp_ref[b, :, sl], s[j]) for j, (b, h, sl) in enumerate(pairs)]
    for j, (b, h, _) in enumerate(pairs):
        s_ref[b, h] = s_new[j]
    length = rp_ref.shape[1]
    y = _head_norm_wide(jnp.concatenate([jnp.concatenate(ys[b * heads:(b + 1) * heads], axis=-1) for b in range(nb)],
                                        axis=0), hsum_ref[...])
    for b in range(nb):
        o = (y[b * length:(b + 1) * length] * lg_ref[...] + lb_ref[...] + bonus_ref[b]) * g_ref[b]
        o_ref[b] = o.astype(o_ref.dtype)

    @pl.when(c == pl.num_programs(1) - 1)
    def _():
        sout_ref[...] = s_ref[...]


def wkv(prep, s0, lnx_g, lnx_b, chunk):
    ra, at, bt, kt, bh, kh, v, g, bonus, wl = prep
    bsz, t, c_a = ra.shape
    heads = c_a // HEAD
    n_chunks = t // chunk
    assert chunk & (chunk - 1) == 0 and chunk >= 4 and t % chunk == 0
    tile = pl.BlockSpec((1, chunk, c_a), lambda b, c: (b, c, 0))
    mat = pl.BlockSpec((1, 1, HEAD, c_a), lambda b, c: (b, c, 0, 0))
    st = pl.BlockSpec((1, heads, HEAD, HEAD), lambda b, c: (b, 0, 0, 0))
    full = jax.ShapeDtypeStruct((bsz, t, c_a), F32)
    mats = jax.ShapeDtypeStruct((bsz, n_chunks, HEAD, c_a), F32)
    rp, y0, m, n = pl.pallas_call(
        _wkv_chunk_kernel, name="wkv_chunk",
        grid=(bsz, n_chunks),
        in_specs=[tile] * 7 + [pl.BlockSpec((1, 1, 1, c_a), lambda b, c: (b, c, 0, 0))],
        out_specs=[tile, tile, mat, mat],
        out_shape=[full, full, mats, mats],
        compiler_params=_params(("parallel", "parallel")),
    )(ra, at, bt, kt, bh, kh, v, wl)
    nb = SCAN_BATCH if bsz % SCAN_BATCH == 0 else 1
    tile = pl.BlockSpec((nb, chunk, c_a), lambda b, c: (b, c, 0))
    mat = pl.BlockSpec((nb, 1, HEAD, c_a), lambda b, c: (b, c, 0, 0))
    st = pl.BlockSpec((nb, heads, HEAD, HEAD), lambda b, c: (b, 0, 0, 0))
    return pl.pallas_call(
        _wkv_scan_kernel, name="wkv_scan",
        grid=(bsz // nb, n_chunks),
        in_specs=[tile, tile, mat, mat, tile, tile, st, _const_spec((1, c_a)), _const_spec((1, c_a)),
                  _const_spec((c_a, c_a))],
        out_specs=[tile, st],
        out_shape=[jax.ShapeDtypeStruct((bsz, t, c_a), BF16), jax.ShapeDtypeStruct(s0.shape, F32)],
        scratch_shapes=[pltpu.VMEM((nb, heads, HEAD, HEAD), F32)],
        compiler_params=_params(("parallel", "arbitrary")),
    )(rp, y0, m, n, g, bonus, s0, lnx_g.reshape(1, c_a), lnx_b.reshape(1, c_a), _head_sum_matrix(c_a))


def _wkv_step_kernel(r_ref, w_ref, k_ref, v_ref, a_ref, b_ref, g_ref, bonus_ref, s_ref, lg_ref, lb_ref, hsum_ref,
                     o_ref, sout_ref):
    bt, heads = s_ref.shape[0], s_ref.shape[1]
    c_a = heads * HEAD
    hsum = hsum_ref[...]
    diag = (lax.broadcasted_iota(jnp.int32, (HEAD, c_a), 1) % HEAD
            == lax.broadcasted_iota(jnp.int32, (HEAD, c_a), 0))
    seqs = range(bt)
    row = lambda ref, i: ref[i:i + 1, :]
    stack = lambda f: jnp.concatenate([f(i) for i in seqs], axis=0)
    piece = lambda x, i: x[i * HEAD:(i + 1) * HEAD]
    s = stack(lambda i: jnp.concatenate([s_ref[i, h] for h in range(heads)], axis=1))
    sa = _mask_dot(s * stack(lambda i: jnp.broadcast_to(row(a_ref, i), (HEAD, c_a))), hsum)
    v_rows = _mask_dot(stack(lambda i: jnp.where(diag, row(v_ref, i), 0.0)), hsum)
    s = stack(lambda i: piece(s, i) * row(w_ref, i) + piece(sa, i) * row(b_ref, i) + piece(v_rows, i) * row(k_ref, i))
    for i in seqs:
        for h in range(heads):
            sout_ref[i, h] = piece(s, i)[:, h * HEAD:(h + 1) * HEAD]
    y_rows = _mask_dot(stack(lambda i: piece(s, i) * row(r_ref, i)), hsum)
    y = stack(lambda i: jnp.sum(jnp.where(diag, piece(y_rows, i), 0.0), axis=0, keepdims=True))
    y = _head_norm_wide(y, hsum)
    o_ref[...] = (y * lg_ref[...] + lb_ref[...] + bonus_ref[...]) * g_ref[...]


def wkv_step(ops, s0, lnx_g, lnx_b, bt):
    bsz, c_a = ops[0].shape
    rows = pl.BlockSpec((bt, c_a), lambda i: (i, 0))
    st = pl.BlockSpec((bt,) + s0.shape[1:], lambda i: (i, 0, 0, 0))
    hsum = _head_sum_matrix(c_a)
    return pl.pallas_call(
        _wkv_step_kernel, name="wkv_step",
        grid=(bsz // bt,),
        in_specs=[rows] * 8 + [st, _const_spec((1, c_a)), _const_spec((1, c_a)), _const_spec(hsum.shape)],
        out_specs=[rows, st],
        out_shape=[jax.ShapeDtypeStruct((bsz, c_a), F32), jax.ShapeDtypeStruct(s0.shape, F32)],
        compiler_params=_params(("parallel",)),
    )(*ops, s0, lnx_g.reshape(1, c_a), lnx_b.reshape(1, c_a), hsum)


def rope_tables(pos):
    inv_freq = ROPE_THETA ** (-jnp.arange(ROT_HALF, dtype=F32) / ROT_HALF)
    ang = pos[:, None] * inv_freq[None, :]
    cos, sin = jnp.cos(ang), jnp.sin(ang)
    t = pos.shape[0]
    rest = HEAD - 2 * ROT_HALF
    c = jnp.concatenate([cos, cos, jnp.ones((t, rest), F32)], axis=1)
    s1 = jnp.concatenate([jnp.zeros((t, ROT_HALF), F32), sin, jnp.zeros((t, rest), F32)], axis=1)
    s2 = jnp.concatenate([-sin, jnp.zeros((t, HEAD - ROT_HALF), F32)], axis=1)
    rep = LANES // HEAD
    return tuple(jnp.tile(x, (1, rep)) for x in (c, s1, s2))


def _rope(x, c, s1, s2):
    width = x.shape[-1]
    rep = width // c.shape[-1]
    if rep > 1:
        c, s1, s2 = (jnp.concatenate([t] * rep, axis=1) for t in (c, s1, s2))
    return x * c + pltpu.roll(x, ROT_HALF, 1) * s1 + pltpu.roll(x, width - ROT_HALF, 1) * s2


def _attn_kernel(window, q_w, kv_w, zq_ref, kp_ref, vp_ref, rc_ref, rs1_ref, rs2_ref, pc_ref, ps1_ref, ps2_ref,
                 mk_ref, mv_ref, sink_ref, ob_ref, om_ref, kr_ref):
    n = pl.program_id(1)
    zq = zq_ref[0]
    scale = HEAD ** -0.5
    q = zq[:, :q_w]
    k = zq[:, q_w:q_w + kv_w]
    v = zq[:, q_w + kv_w:q_w + 2 * kv_w]
    qm = zq[:, q_w + 2 * kv_w:]
    qr = _rope(q, rc_ref[...], rs1_ref[...], rs2_ref[...]) * scale
    kr = _rope(k, rc_ref[...], rs1_ref[...], rs2_ref[...])
    kr_ref[0] = kr
    kpr = _rope(kp_ref[0], pc_ref[...], ps1_ref[...], ps2_ref[...])
    vp = vp_ref[0]
    gqa = q_w // kv_w
    assert window & (window - 1) == 0
    qi = lax.broadcasted_iota(jnp.int32, (gqa * window, 2 * window), 0) & (window - 1)
    kj = lax.broadcasted_iota(jnp.int32, (gqa * window, 2 * window), 1)
    valid = (kj > qi) & (kj <= qi + window) & ((kj >= window) | (n > 0))
    tdot = lambda a, b: lax.dot_general(a, b, (((1,), (1,)), ((), ())), preferred_element_type=F32)
    dot = lambda a, b: jnp.dot(a, b, preferred_element_type=F32)
    kvs = range(kv_w // HEAD)
    mhs = range(qm.shape[1] // HEAD)
    hsl = lambda h: slice(h * HEAD, (h + 1) * HEAD)
    qmb = (qm * scale).astype(BF16)
    mk = mk_ref[0].astype(BF16)
    mv = mv_ref[0].astype(BF16)
    kcat = [jnp.concatenate([kpr[:, hsl(hk)], kr[:, hsl(hk)]], axis=0).astype(BF16) for hk in kvs]
    vcat = [jnp.concatenate([vp[:, hsl(hk)], v[:, hsl(hk)]], axis=0).astype(BF16) for hk in kvs]
    qs = [jnp.concatenate([qr[:, hsl(hk * gqa + g)] for g in range(gqa)], axis=0).astype(BF16) for hk in kvs]
    s = [jnp.where(valid, tdot(qs[hk], kcat[hk]), NEG_INF) for hk in kvs]
    sm = [tdot(qmb[:, hsl(h)], mk[:, hsl(h)]) for h in mhs]
    sink = [jnp.concatenate([jnp.full((window, 1), sink_ref[hk * gqa + g], F32) for g in range(gqa)], axis=0)
            for hk in kvs]
    m = [jnp.maximum(jnp.max(s[hk], axis=-1, keepdims=True), sink[hk]) for hk in kvs]
    p = [jnp.exp(s[hk] - m[hk]) for hk in kvs]
    pm = [jnp.exp(sm[h] - jnp.max(sm[h], axis=-1, keepdims=True)) for h in mhs]
    o = [dot(p[hk].astype(BF16), vcat[hk])
         / (jnp.sum(p[hk], axis=-1, keepdims=True) + jnp.exp(sink[hk] - m[hk])) for hk in kvs]
    om = [dot(pm[h].astype(BF16), mv[:, hsl(h)]) / jnp.sum(pm[h], axis=-1, keepdims=True) for h in mhs]
    ob = jnp.concatenate([o[hk][g * window:(g + 1) * window] for hk in kvs for g in range(gqa)], axis=1)
    ob_ref[0] = ob.astype(ob_ref.dtype)
    om_ref[0] = jnp.concatenate(om, axis=1).astype(om_ref.dtype)


def attn_prompt(zq, mk, mv, sinks, tables, window, q_w, kv_w):
    bsz, t, zw = zq.shape
    mem_w = zw - q_w - 2 * kv_w
    assert kv_w == LANES and q_w % kv_w == 0
    kcol, vcol = q_w // kv_w, q_w // kv_w + 1
    prev = lambda n: jnp.maximum(n - 1, 0)
    tab = pl.BlockSpec((window, LANES), lambda b, n: (n, 0))
    ptab = pl.BlockSpec((window, LANES), lambda b, n: (prev(n), 0))
    mem = pl.BlockSpec((1,) + mk.shape[1:], lambda b, n: (b, 0, 0))
    return pl.pallas_call(
        functools.partial(_attn_kernel, window, q_w, kv_w), name="attn",
        grid=(bsz, t // window),
        in_specs=[pl.BlockSpec((1, window, zw), lambda b, n: (b, n, 0)),
                  pl.BlockSpec((1, window, kv_w), lambda b, n: (b, prev(n), kcol)),
                  pl.BlockSpec((1, window, kv_w), lambda b, n: (b, prev(n), vcol)),
                  tab, tab, tab, ptab, ptab, ptab, mem, mem,
                  pl.BlockSpec(memory_space=pltpu.SMEM)],
        out_specs=[pl.BlockSpec((1, window, q_w), lambda b, n: (b, n, 0)),
                   pl.BlockSpec((1, window, mem_w), lambda b, n: (b, n, 0)),
                   pl.BlockSpec((1, window, kv_w), lambda b, n: (b, n, 0))],
        out_shape=[jax.ShapeDtypeStruct((bsz, t, q_w), BF16), jax.ShapeDtypeStruct((bsz, t, mem_w), BF16),
                   jax.ShapeDtypeStruct((bsz, t, kv_w), F32)],
        compiler_params=_params(("parallel", "parallel")),
    )(zq, zq, zq, *tables, *tables, mk, mv, sinks)


def _attn_step_kernel(q_w, kv_w, zq_ref, ck_ref, cv_ref, mk_ref, mv_ref, rc_ref, rs1_ref, rs2_ref, sink_ref,
                      ob_ref, om_ref, nk_ref, nv_ref):
    bt = zq_ref.shape[0]
    window = ck_ref.shape[1]
    mem_w = om_ref.shape[1]
    n_q, gqa, per_vreg = q_w // HEAD, q_w // kv_w, LANES // HEAD
    scale = HEAD ** -0.5
    zq = zq_ref[...]
    q = _rope(zq[:, :q_w], rc_ref[...], rs1_ref[...], rs2_ref[...]) * scale
    k_new = _rope(zq[:, q_w:q_w + kv_w], rc_ref[...], rs1_ref[...], rs2_ref[...])
    v_new = zq[:, q_w + kv_w:q_w + 2 * kv_w]
    qm = zq[:, q_w + 2 * kv_w:] * scale
    own = lambda w: (lax.broadcasted_iota(jnp.int32, (n_q, w), 1) // HEAD
                     == lax.broadcasted_iota(jnp.int32, (n_q, w), 0))
    own_q, own_m = own(q_w), own(mem_w)
    hrow = lax.broadcasted_iota(jnp.int32, (n_q, LANES), 0)
    hblk = lax.broadcasted_iota(jnp.int32, (n_q, LANES), 1) // HEAD
    swap = (hrow % per_vreg) != (hrow // gqa)
    keep = hblk == hrow % per_vreg
    key_ok = lax.broadcasted_iota(jnp.int32, (n_q, window), 1) >= 1
    wrow = lax.broadcasted_iota(jnp.int32, (window, kv_w), 0)
    sink = sink_ref[...]
    bs = range(bt)
    tdot = lambda a, b: lax.dot_general(a.astype(BF16), b.astype(BF16), (((1,), (1,)), ((), ())),
                                        preferred_element_type=F32)
    for b in bs:
        nk_ref[b] = jnp.where(wrow == window - 1, k_new[b:b + 1], pltpu.roll(ck_ref[b], window - 1, 0))
        nv_ref[b] = jnp.where(wrow == window - 1, v_new[b:b + 1], pltpu.roll(cv_ref[b], window - 1, 0))
    q8 = []
    for b in bs:
        rep = jnp.where(own_q, q[b:b + 1], 0.0)
        fold = sum(rep[:, c * LANES:(c + 1) * LANES] for c in range(q_w // LANES))
        q8.append(jnp.where(swap, pltpu.roll(fold, HEAD, 1), fold))
    qm8 = [jnp.where(own_m, qm[b:b + 1], 0.0) for b in bs]
    s = [jnp.where(key_ok, tdot(q8[b], ck_ref[b]), NEG_INF) for b in bs]
    sm = [tdot(qm8[b], mk_ref[b]) for b in bs]
    s_new = [jnp.sum(q8[b] * k_new[b:b + 1], axis=1, keepdims=True) for b in bs]
    m = [jnp.maximum(jnp.maximum(jnp.max(s[b], axis=1, keepdims=True), s_new[b]), sink) for b in bs]
    p = [jnp.exp(s[b] - m[b]) for b in bs]
    p_new = [jnp.exp(s_new[b] - m[b]) for b in bs]
    pm = [jnp.exp(sm[b] - jnp.max(sm[b], axis=1, keepdims=True)) for b in bs]
    o8 = [(_bdot(p[b], cv_ref[b]) + p_new[b] * v_new[b:b + 1])
          / (jnp.sum(p[b], axis=1, keepdims=True) + p_new[b] + jnp.exp(sink - m[b])) for b in bs]
    om8 = [_bdot(pm[b], mv_ref[b]) / jnp.sum(pm[b], axis=1, keepdims=True) for b in bs]
    ob, om = [], []
    for b in bs:
        o = jnp.where(swap, pltpu.roll(o8[b], HEAD, 1), o8[b])
        o = jnp.concatenate([jnp.where(keep, o, 0.0)] * (q_w // LANES), axis=1)
        ob.append(jnp.sum(jnp.where(own_q, o, 0.0), axis=0, keepdims=True))
        om.append(jnp.sum(jnp.where(own_m, om8[b], 0.0), axis=0, keepdims=True))
    ob_ref[...] = jnp.concatenate(ob, axis=0)
    om_ref[...] = jnp.concatenate(om, axis=0)


def attn_step(zq, ck, cv, mk, mv, sinks, tables, q_w, kv_w, bt):
    bsz, zw = zq.shape
    mem_w = zw - q_w - 2 * kv_w
    window, n_mem = ck.shape[1], mk.shape[1]
    n_q = q_w // HEAD
    assert kv_w == LANES and mem_w // HEAD <= n_q and n_q == 8
    rows = lambda w: pl.BlockSpec((bt, w), lambda i: (i, 0))
    cache = pl.BlockSpec((bt, window, kv_w), lambda i: (i, 0, 0))
    mem = pl.BlockSpec((bt, n_mem, mem_w), lambda i: (i, 0, 0))
    consts = list(tables) + [sinks.reshape(n_q, 1)]
    const_specs = [_const_spec(c.shape) for c in consts]
    return pl.pallas_call(
        functools.partial(_attn_step_kernel, q_w, kv_w), name="attn_step",
        grid=(bsz // bt,),
        in_specs=[rows(zw), cache, cache, mem, mem] + const_specs,
        out_specs=[rows(q_w), rows(mem_w), cache, cache],
        out_shape=[jax.ShapeDtypeStruct((bsz, q_w), F32), jax.ShapeDtypeStruct((bsz, mem_w), F32),
                   jax.ShapeDtypeStruct(ck.shape, F32), jax.ShapeDtypeStruct(cv.shape, F32)],
        compiler_params=_params(("parallel",)),
    )(zq, ck, cv, mk, mv, *consts)


def _proj_kernel(x_ref, w_ref, o_ref):
    o_ref[...] = jnp.dot(x_ref[...].astype(BF16), w_ref[...], preferred_element_type=F32)


def proj(x, w, tm):
    n, d = x.shape
    return pl.pallas_call(
        _proj_kernel, name="proj",
        grid=(n // tm,),
        in_specs=[pl.BlockSpec((tm, d), lambda i: (i, 0)), _const_spec(w.shape)],
        out_specs=pl.BlockSpec((tm, w.shape[1]), lambda i: (i, 0)),
        out_shape=jax.ShapeDtypeStruct((n, w.shape[1]), F32),
        compiler_params=_params(("parallel",)),
    )(x, w)


def _merge_kernel(alpha, n_tiles, x_ref, oa_ref, ob_ref, om_ref, gt_ref, lig_ref, lib_ref, pa_ref, pb_ref, pm_ref,
                  wo_ref, l1g_ref, l1b_ref, wr_ref, br_ref, *refs):
    outs = refs[-4:]

    @pl.when(pl.program_id(0) >= n_tiles)
    def _():
        for ref in outs:
            ref[...] = jnp.zeros_like(ref)

    @pl.when(pl.program_id(0) < n_tiles)
    def _():
        _merge_tile(alpha, x_ref, oa_ref, ob_ref, om_ref, gt_ref, lig_ref, lib_ref, pa_ref, pb_ref, pm_ref, wo_ref,
                    l1g_ref, l1b_ref, wr_ref, br_ref, *outs)


def _merge_tile(alpha, x_ref, oa_ref, ob_ref, om_ref, gt_ref, lig_ref, lib_ref, pa_ref, pb_ref, pm_ref, wo_ref,
                l1g_ref, l1b_ref, wr_ref, br_ref, x1_ref, x1t_ref, eidx_ref, gate_ref):
    d = x_ref.shape[1]
    xn = _ln(x_ref[...], lig_ref[...], lib_ref[...])
    gts = jax.nn.sigmoid(gt_ref[...].astype(F32))
    merged = (gts[:, :d] * _bdot(oa_ref[...], pa_ref[...]) + gts[:, d:2 * d] * _bdot(ob_ref[...], pb_ref[...])
              + gts[:, 2 * d:] * _bdot(om_ref[...], pm_ref[...]))
    x1 = _ln(alpha * xn + _bdot(merged, wo_ref[...]), l1g_ref[...], l1b_ref[...])
    x1_ref[...] = x1
    _rows_to_tiles(x1t_ref, x1)
    x_hi, x_lo = _pieces(x1, 2)
    w_hi, w_lo = _pieces(wr_ref[...], 2)
    dot = lambda a, b: jnp.dot(a, b, preferred_element_type=F32)
    logits = dot(x_hi, w_hi) + dot(x_hi, w_lo) + dot(x_lo, w_hi) + br_ref[...]
    lane = lax.broadcasted_iota(jnp.int32, logits.shape, 1)
    lane_f = lane.astype(F32)
    first = lambda hit: jnp.min(jnp.where(hit, lane_f, float(LANES)), axis=-1, keepdims=True).astype(jnp.int32)
    gmask = lane < N_GROUPS
    gl = jnp.where(gmask, logits, NEG_INF)
    gmax = jnp.max(gl, axis=-1, keepdims=True)
    gidx = first(gl == gmax)
    g_w = 1.0 / jnp.sum(jnp.where(gmask, jnp.exp(gl - gmax), 0.0), axis=-1, keepdims=True)
    lo = N_GROUPS + gidx * EXPERTS_PER_GROUP
    el = jnp.where((lane >= lo) & (lane < lo + EXPERTS_PER_GROUP), logits, NEG_INF)
    v1 = jnp.max(el, axis=-1, keepdims=True)
    i1 = first(el == v1)
    el2 = jnp.where(lane == i1, NEG_INF, el)
    v2 = jnp.max(el2, axis=-1, keepdims=True)
    i2 = first(el2 == v2)
    e2 = jnp.exp(v2 - v1)
    gate1 = g_w / (1.0 + e2)
    eidx_ref[...] = jnp.where(lane == 0, i1 - N_GROUPS, jnp.where(lane == 1, i2 - N_GROUPS, 0))
    gate_ref[...] = jnp.where(lane == 0, gate1, jnp.where(lane == 1, gate1 * e2, 0.0))


def merge(x, oa, ob, om, gt, w, tm, alpha, n_total, row_offset=0, into=None):
    n, d = x.shape
    assert row_offset % tm == 0 and n % tm == 0 and n_total % tm == 0
    off = row_offset // tm
    into = list(into or [])
    n_tiles = n // tm
    steps = n_tiles if into else n_total // tm
    rows = lambda a: pl.BlockSpec((tm, a.shape[1]), lambda i: (jnp.minimum(i, n_tiles - 1), 0))
    out = lambda width: pl.BlockSpec((tm, width), lambda i: (i + off, 0))
    consts = [w['ln_in_g'], w['ln_in_b'], w['p_a'], w['p_b'], w['p_m'], w['w_o'], w['ln1_g'], w['ln1_b'],
              w['w_route'], w['b_route']]
    n_in = 5 + len(consts)
    return pl.pallas_call(
        functools.partial(_merge_kernel, alpha, n_tiles), name="merge",
        grid=(steps,),
        in_specs=[rows(a) for a in (x, oa, ob, om, gt)] + [_const_spec(c.shape) for c in consts]
        + [pl.BlockSpec(memory_space=pl.ANY)] * len(into),
        out_specs=[out(d), pl.BlockSpec((tm * TILE_ROWS, LANES), lambda i: (i + off, 0)), out(LANES), out(LANES)],
        out_shape=[jax.ShapeDtypeStruct((n_total, d), F32), jax.ShapeDtypeStruct((n_total * TILE_ROWS, LANES), F32),
                   jax.ShapeDtypeStruct((n_total, LANES), jnp.int32), jax.ShapeDtypeStruct((n_total, LANES), F32)],
        input_output_aliases={n_in + k: k for k in range(len(into))},
        compiler_params=_params(("parallel",)),
    )(x, oa, ob, om, gt, *consts, *into)


ROW_DMA_UNROLL = 8
DRAIN_STEPS = 2


TILE_ROWS = 8


def _rows_from_tiles(ref, n):
    return jnp.concatenate([ref[pl.ds(s, n, stride=TILE_ROWS), :] for s in range(TILE_ROWS)], axis=1)


def _rows_to_tiles(ref, x):
    for s in range(TILE_ROWS):
        ref[pl.ds(s, x.shape[0], stride=TILE_ROWS), :] = x[:, s * LANES:(s + 1) * LANES]


def _row_copies(asg_ref, base, count, n_asg, x_hbm, buf, y_hbm, sem, gather, unrolled):
    def tile(ref, idx):
        start = idx * TILE_ROWS
        return ref.at[pl.ds(start if isinstance(idx, int) else pl.multiple_of(start, TILE_ROWS), TILE_ROWS)]

    def one(r, priority):
        a = asg_ref[base + r]
        if gather:
            tok = jnp.minimum(a, n_asg - 1)
            tok = jnp.where(tok >= n_asg // 2, tok - n_asg // 2, tok)
            copy = pltpu.make_async_copy(tile(x_hbm, tok), tile(buf, r), sem)
        else:
            copy = pltpu.make_async_copy(tile(buf, r), tile(y_hbm, a), sem)
        copy.start(priority=priority)

    if unrolled:
        for r in range(count):
            one(r, r % 2)
        return

    def body(g, carry):
        for j in range(ROW_DMA_UNROLL):
            one(g * ROW_DMA_UNROLL + j, j % 2)
        return carry
    lax.fori_loop(0, count // ROW_DMA_UNROLL, body, 0)


def _moe_expert_kernel(n_asg, asg_ref, be_ref, nu_ref, x_hbm, wg_ref, wu_ref, wd_ref, y_hbm,
                       xbuf, ybuf, wgb, wub, wdb, gsem, ssem):
    i = pl.program_id(0)
    used = nu_ref[0]
    rows = xbuf.shape[1] // TILE_ROWS
    n_blocks = be_ref.shape[0]
    blk = jnp.minimum(i, n_blocks - 1)

    @pl.when((i == 0) | (be_ref[blk] != be_ref[jnp.maximum(blk - 1, 0)]))
    def _():
        wgb[...] = wg_ref[0, 0].astype(BF16)
        wub[...] = wu_ref[0, 0].astype(BF16)
        wdb[...] = wd_ref[0, 0].astype(BF16)

    def wait_gather(slot):
        pltpu.make_async_copy(x_hbm.at[pl.ds(0, rows * TILE_ROWS)], xbuf.at[slot], gsem.at[slot]).wait()

    def wait_scatter(slot):
        pltpu.make_async_copy(ybuf.at[slot], y_hbm.at[pl.ds(0, rows * TILE_ROWS)], ssem.at[slot]).wait()

    def gather(b, slot, unrolled):
        _row_copies(asg_ref, b * rows, rows, n_asg, x_hbm, xbuf.at[slot], y_hbm, gsem.at[slot], True, unrolled)

    def scatter(b, slot, unrolled):
        _row_copies(asg_ref, b * rows, rows, n_asg, x_hbm, ybuf.at[slot], y_hbm, ssem.at[slot], False, unrolled)

    def expert(slot):
        xb = _rows_from_tiles(xbuf.at[slot], rows).astype(BF16)
        hg = jnp.dot(xb, wgb[...], preferred_element_type=F32)
        hu = jnp.dot(xb, wub[...], preferred_element_type=F32)
        h = hg * jax.nn.sigmoid(hg) * hu
        _rows_to_tiles(ybuf.at[slot], jnp.dot(h.astype(BF16), wdb[...], preferred_element_type=F32))

    @pl.when(i == 0)
    def _():
        ybuf[1] = jnp.zeros(ybuf.shape[1:], F32)
        spare = [pltpu.make_async_copy(ybuf.at[1],
                                       y_hbm.at[pl.ds((n_asg + j * rows) * TILE_ROWS, rows * TILE_ROWS)], ssem.at[1])
                 for j in range((y_hbm.shape[0] // TILE_ROWS - n_asg) // rows)]
        for copy in spare:
            copy.start()
        for copy in spare:
            copy.wait()

    @pl.when((i >= 2) & (i - 2 < used))
    def _():
        wait_scatter(i % 2)

    steady = (i >= 1) & (i + 1 < used)
    for slot in range(2):
        @pl.when(steady & (i % 2 == slot))
        def _():
            wait_gather(slot)
            gather(i + 1, 1 - slot, True)
            scatter(i - 1, 1 - slot, True)
            expert(slot)

    @pl.when(jnp.logical_not(steady))
    def _():
        slot = i % 2

        @pl.when((i == 0) & (used > 0))
        def _():
            gather(0, 0, False)

        @pl.when(i < used)
        def _():
            wait_gather(slot)

        @pl.when(i + 1 < used)
        def _():
            gather(i + 1, 1 - slot, False)

        @pl.when((i >= 1) & (i - 1 < used))
        def _():
            scatter(i - 1, 1 - slot, False)

        @pl.when(i < used)
        def _():
            expert(slot)


def moe_experts(x1_tiles, n_asg, asg, blk_e, n_used, e_gate, e_up, e_down):
    d = e_gate.shape[2]
    assert d == TILE_ROWS * LANES and x1_tiles.shape[1] == LANES
    n_blocks = blk_e.shape[0]
    ff = e_gate.shape[-1]
    n_rows = n_asg + e_gate.shape[1] * EXPERT_BLOCK
    weight = lambda shape: pl.BlockSpec(
        (1, 1) + shape, lambda i, asg, be, nu: (0, be[jnp.minimum(i, n_blocks - 1)], 0, 0))
    return pl.pallas_call(
        functools.partial(_moe_expert_kernel, n_asg), name="moe_expert",
        grid_spec=pltpu.PrefetchScalarGridSpec(
            num_scalar_prefetch=3,
            grid=(n_blocks + DRAIN_STEPS,),
            in_specs=[pl.BlockSpec(memory_space=pl.ANY), weight((d, ff)), weight((d, ff)), weight((ff, d))],
            out_specs=pl.BlockSpec(memory_space=pl.ANY),
            scratch_shapes=[pltpu.VMEM((2, EXPERT_BLOCK * TILE_ROWS, LANES), F32),
                            pltpu.VMEM((2, EXPERT_BLOCK * TILE_ROWS, LANES), F32),
                            pltpu.VMEM((d, ff), BF16), pltpu.VMEM((d, ff), BF16), pltpu.VMEM((ff, d), BF16),
                            pltpu.SemaphoreType.DMA((2,)), pltpu.SemaphoreType.DMA((2,))]),
        out_shape=jax.ShapeDtypeStruct((n_rows * TILE_ROWS, LANES), F32),
        compiler_params=_params(("arbitrary",)),
    )(asg, blk_e, n_used, x1_tiles, e_gate, e_up, e_down)


def _moe_combine_kernel(alpha, lead_tiles, y0_ref, y1_ref, x1_ref, gate_ref, g_ref, b_ref, lead_ref, tail_ref):
    i = pl.program_id(0)
    gate = gate_ref[...]
    tm = x1_ref.shape[0]
    moe = gate[:, 0:1] * _rows_from_tiles(y0_ref, tm) + gate[:, 1:2] * _rows_from_tiles(y1_ref, tm)
    out = _ln(alpha * x1_ref[...] + moe, g_ref[...], b_ref[...])

    @pl.when(i < lead_tiles)
    def _():
        lead_ref[...] = out

    @pl.when(i >= lead_tiles)
    def _():
        tail_ref[...] = out


def moe_combine(y, x1, gate, g, b, tm, alpha, n, n_lead):
    d = x1.shape[1]
    assert n_lead % tm == 0 and (n - n_lead) % tm == 0 and n > n_lead
    lead_tiles = n_lead // tm
    return pl.pallas_call(
        functools.partial(_moe_combine_kernel, alpha, lead_tiles), name="moe_combine",
        grid=(n // tm,),
        in_specs=[pl.BlockSpec((tm * TILE_ROWS, LANES), lambda i: (i, 0)),
                  pl.BlockSpec((tm * TILE_ROWS, LANES), lambda i: (i + n // tm, 0)),
                  pl.BlockSpec((tm, d), lambda i: (i, 0)),
                  pl.BlockSpec((tm, LANES), lambda i: (i, 0)), _const_spec((1, d)), _const_spec((1, d))],
        out_specs=[pl.BlockSpec((tm, d), lambda i: (jnp.minimum(i, lead_tiles - 1), 0)),
                   pl.BlockSpec((tm, d), lambda i: (jnp.maximum(i - lead_tiles, 0), 0))],
        out_shape=[jax.ShapeDtypeStruct((n_lead, d), F32), jax.ShapeDtypeStruct((n - n_lead, d), F32)],
        compiler_params=_params(("arbitrary",)),
    )(y, y, x1, gate, g.reshape(1, d), b.reshape(1, d))


def moe_routing(experts):
    n, top_k = experts.shape
    n_exp = N_GROUPS * EXPERTS_PER_GROUP
    a = n * top_k
    flat_e = experts.T.reshape(a)
    onehot = (flat_e[:, None] == jnp.arange(n_exp, dtype=jnp.int32)[None, :]).astype(jnp.int32)
    rank = jnp.take_along_axis(jnp.cumsum(onehot, axis=0) - onehot, flat_e[:, None], axis=1)[:, 0]
    counts = jnp.sum(onehot, axis=0)
    padded = (counts + EXPERT_BLOCK - 1) // EXPERT_BLOCK * EXPERT_BLOCK
    pad_end = jnp.cumsum(padded)
    dest = (pad_end - padded)[flat_e] + rank
    n_blocks = -(-a // EXPERT_BLOCK) + n_exp
    blk_start = jnp.arange(n_blocks, dtype=jnp.int32) * EXPERT_BLOCK
    blk_e = jnp.minimum(jnp.sum((pad_end[None, :] <= blk_start[:, None]).astype(jnp.int32), axis=1), n_exp - 1)
    spare = jnp.arange(n_blocks * EXPERT_BLOCK, dtype=jnp.int32) - jnp.repeat(jnp.cumsum(counts)[blk_e], EXPERT_BLOCK)
    spare = a + jnp.clip(spare, 0, n_exp * EXPERT_BLOCK - 1)
    asg = spare.astype(jnp.int32).at[dest].set(jnp.arange(a, dtype=jnp.int32), unique_indices=True)
    n_used = (pad_end[-1:] // EXPERT_BLOCK).astype(jnp.int32)
    return asg, blk_e.astype(jnp.int32), n_used


def hier_moe_ln(x1, x1_tiles, eidx, gate, w, tm, alpha, n_tokens, n_lead):
    asg, blk_e, n_used = moe_routing(eidx[:n_tokens, :2])
    y = moe_experts(x1_tiles, 2 * n_tokens, asg, blk_e, n_used, w['e_gate'], w['e_up'], w['e_down'])
    return moe_combine(y, x1, gate, w['ln2_g'], w['ln2_b'], tm, alpha, n_tokens, n_lead)


def kernel(x_prompt, x_sample, mem_prompt, state_wkv, state_shift, cache_win_k, cache_win_v, cache_mem_k, cache_mem_v, ln_in_g, ln_in_b, w_in, mu, w0, w_up, a0, a_up, g_up, k_k, k_a, r_k, lnx_g, lnx_b, sinks, w_mem_kv, p_a, p_b, p_m, w_o, ln1_g, ln1_b, w_group, b_group, w_router, b_router, e_gate, e_up, e_down, ln2_g, ln2_b):
    depth = w_in.shape[0]
    assert depth == 1, "single-layer step"
    bsz, seq, d = x_prompt.shape
    dec = x_sample.shape[0]
    assert x_sample.shape[1] == 1
    c_shift = mu.shape[-1]
    c_a = w0.shape[-1]
    window, kv_w = cache_win_k.shape[2], cache_win_k.shape[3] * cache_win_k.shape[4]
    n_mem, mem_w = cache_mem_k.shape[2], cache_mem_k.shape[3] * cache_mem_k.shape[4]
    q_w = sinks.shape[-1] * HEAD
    qkvm_w = q_w + 2 * kv_w + mem_w
    alpha = (2.0 * depth) ** 0.25
    past_len = float(PAST_LEN)
    chunk = 64

    w_in_b = w_in[0].astype(BF16)
    w_parts = [w_in_b[:, :c_shift], w_in_b[:, c_shift:c_shift + qkvm_w], w_in_b[:, c_shift + qkvm_w:]]
    rp = dict(mu=mu[0], w0=w0[0], w_up=w_up[0], a0=a0[0], a_up=a_up[0], g_up=g_up[0], k_k=k_k[0], k_a=k_a[0],
              r_k=r_k[0].reshape(-1))
    n_route = N_GROUPS * (1 + EXPERTS_PER_GROUP)
    mw = dict(ln_in_g=ln_in_g.reshape(1, d), ln_in_b=ln_in_b.reshape(1, d), p_a=p_a[0].astype(BF16),
              p_b=p_b[0].astype(BF16), p_m=p_m[0].astype(BF16), w_o=w_o[0].astype(BF16),
              ln1_g=ln1_g[0].reshape(1, d), ln1_b=ln1_b[0].reshape(1, d),
              w_route=jnp.pad(jnp.concatenate([w_group[0], w_router[0]], axis=1), ((0, 0), (0, LANES - n_route))),
              b_route=jnp.pad(jnp.concatenate([b_group[0], b_router[0]]), (0, LANES - n_route)).reshape(1, LANES),
              e_gate=e_gate, e_up=e_up, e_down=e_down, ln2_g=ln2_g[0], ln2_b=ln2_b[0])

    xp = x_prompt.reshape(bsz * seq, d)
    zr, zq, zg = ln_proj(xp, ln_in_g, ln_in_b, w_parts, 256, (F32, F32, BF16))
    zr3 = zr.reshape(bsz, seq, c_shift)
    prep = rwkv_prep(zr3, jnp.zeros((bsz, 1, c_shift), F32), rp, chunk, 256)
    o_a, wkv_p = wkv(prep, jnp.zeros((bsz, c_a // HEAD, HEAD, HEAD), F32), lnx_g[0], lnx_b[0], chunk)
    mkv = proj(mem_prompt.reshape(bsz * n_mem, d), w_mem_kv[0].astype(BF16), 256).reshape(bsz, n_mem, 2 * mem_w)
    mk_p, mv_p = mkv[..., :mem_w], mkv[..., mem_w:]
    tables = rope_tables(jnp.arange(seq, dtype=F32))
    o_b, o_m, k_rot = attn_prompt(zq.reshape(bsz, seq, qkvm_w), mk_p, mv_p, sinks[0], tables, window, q_w, kv_w)
    n_all = bsz * seq + dec
    n_buf = -(-n_all // 256) * 256
    routed = merge(xp, o_a.reshape(-1, c_a), o_b.reshape(-1, q_w), o_m.reshape(-1, mem_w), zg, mw, 256, alpha, n_buf)
    shift_p = zr3[:, -1]
    kb_p = k_rot[:, -window:].reshape(bsz, window, H_KV, HEAD)
    vb_p = zq.reshape(bsz, seq, qkvm_w)[:, -window:, q_w + kv_w:q_w + 2 * kv_w].reshape(bsz, window, H_KV, HEAD)

    xs = x_sample.reshape(dec, d)
    zr_s, zq_s, zg_s = ln_proj(xs, ln_in_g, ln_in_b, w_parts, dec, (F32, F32, F32))
    ops_s = rwkv_prep(zr_s.reshape(1, dec, c_shift), state_shift[0].reshape(1, dec, c_shift), rp, 1, dec)
    o_a_s, wkv_s = wkv_step([a.reshape(dec, c_a) for a in ops_s], state_wkv[0], lnx_g[0], lnx_b[0], 8)
    tables_s = rope_tables(jnp.full((1,), past_len, F32))
    o_b_s, o_m_s, nk_s, nv_s = attn_step(
        zq_s, cache_win_k[0].reshape(dec, window, kv_w), cache_win_v[0].reshape(dec, window, kv_w),
        cache_mem_k[0].reshape(dec, n_mem, mem_w), cache_mem_v[0].reshape(dec, n_mem, mem_w),
        sinks[0], tables_s, q_w, kv_w, 8)
    x1, x1_tiles, eidx, gate = merge(xs, o_a_s, o_b_s, o_m_s, zg_s, mw, dec, alpha, n_buf, bsz * seq, routed)

    y_prompt, y_sample = hier_moe_ln(x1, x1_tiles, eidx, gate, mw, dec, alpha, n_all, bsz * seq)
    y_prompt = y_prompt.reshape(bsz, seq, d)
    y_sample = y_sample.reshape(dec, 1, d)

    sd = state_wkv.dtype
    return (y_prompt, y_sample, wkv_p[None].astype(sd), wkv_s[None].astype(sd), shift_p[None], zr_s[None],
            kb_p[None], vb_p[None], nk_s.reshape(dec, window, H_KV, HEAD)[None],
            nv_s.reshape(dec, window, H_KV, HEAD)[None],
            mk_p.reshape(bsz, n_mem, -1, HEAD)[None], mv_p.reshape(bsz, n_mem, -1, HEAD)[None])
```

```python
import functools
import math

import jax
import jax.numpy as jnp
from jax import lax
from jax.experimental import pallas as pl
from jax.experimental.pallas import tpu as pltpu

F32 = jnp.float32
BF16 = jnp.bfloat16
SCAN_BATCH = 4
CHUNKS_PER_STEP = 4

HEAD = 64
LANES = 128
H_KV = 2
ROT_HALF = 8
ROPE_THETA = 500000.0
PAST_LEN = 8192
N_GROUPS = 4
EXPERTS_PER_GROUP = 8
EXPERT_BLOCK = 128
LN_EPS = 1e-5
LNX_EPS = 64e-5
NEG_INF = -1e30
VMEM_LIMIT = 48 * 1024 * 1024


def _pieces(x, n):
    out = []
    for _ in range(n):
        p = x.astype(BF16)
        out.append(p)
        x = x - p.astype(F32)
    return out


def _mask_dot(x, mask, n=2):
    return sum(jnp.dot(p, mask, preferred_element_type=F32) for p in _pieces(x, n))


def _split3(x, axis, lhs):
    hi = x.astype(BF16).astype(F32)
    lo = x - hi
    return jnp.concatenate([hi, hi, lo] if lhs else [hi, lo, hi], axis=axis).astype(BF16)


def _dot3(a, b):
    return jnp.dot(_split3(a, 1, True), _split3(b, 0, False), preferred_element_type=F32)


def _dot3_t(a, b):
    return lax.dot_general(_split3(a, 1, True), _split3(b, 1, False), (((1,), (1,)), ((), ())),
                           preferred_element_type=F32)


def _bdot(a, b):
    return jnp.dot(a.astype(BF16), b.astype(BF16), preferred_element_type=F32)


def _bdot_t(a, b):
    return lax.dot_general(a.astype(BF16), b.astype(BF16), (((1,), (1,)), ((), ())), preferred_element_type=F32)


def _ln(x, g, b):
    mu = jnp.mean(x, axis=-1, keepdims=True)
    xc = x - mu
    var = jnp.mean(xc * xc, axis=-1, keepdims=True)
    return xc * lax.rsqrt(var + LN_EPS) * g + b


def _const_spec(shape):
    nd = len(shape)
    return pl.BlockSpec(shape, lambda *_: (0,) * nd)


def _params(sem):
    return pltpu.CompilerParams(dimension_semantics=sem, vmem_limit_bytes=VMEM_LIMIT)


def _ln_proj_kernel(x_ref, g_ref, b_ref, *refs):
    nw = len(refs) // 2
    xn = _ln(x_ref[...], g_ref[...], b_ref[...]).astype(BF16)
    for w_ref, o_ref in zip(refs[:nw], refs[nw:]):
        o_ref[...] = jnp.dot(xn, w_ref[...], preferred_element_type=F32).astype(o_ref.dtype)


def ln_proj(x, g, b, ws, tm, out_dtypes):
    n, d = x.shape
    return pl.pallas_call(
        _ln_proj_kernel, name="ln_proj",
        grid=(n // tm,),
        in_specs=[pl.BlockSpec((tm, d), lambda i: (i, 0)), _const_spec((1, d)), _const_spec((1, d))]
        + [_const_spec(w.shape) for w in ws],
        out_specs=[pl.BlockSpec((tm, w.shape[1]), lambda i: (i, 0)) for w in ws],
        out_shape=[jax.ShapeDtypeStruct((n, w.shape[1]), dt) for w, dt in zip(ws, out_dtypes)],
        compiler_params=_params(("parallel",)),
    )(x, g.reshape(1, d), b.reshape(1, d), *ws)


def _rwkv_prep_kernel(chunk, z_ref, prev_ref, mu_ref, w0_ref, wup_ref, a0_ref, aup_ref, gup_ref,
                      kk_ref, ka_ref, rk_ref, hsum_ref, tril_ref, *refs):
    out_refs, carry_ref = refs[:-1], refs[-1]
    z = z_ref[0]
    tt = z.shape[0]
    c_a = w0_ref.shape[-1]
    r_w, r_a, r_g = wup_ref.shape[0], aup_ref.shape[0], gup_ref.shape[0]
    if chunk == 1:
        prev = prev_ref[0]
    else:
        @pl.when(pl.program_id(1) == 0)
        def _():
            carry_ref[...] = prev_ref[0]

        row = lax.broadcasted_iota(jnp.int32, z.shape, 0)
        prev = jnp.where(row == 0, carry_ref[...], pltpu.roll(z, 1, 0))
        carry_ref[...] = z[tt - 1:tt, :]
    zs = z + (prev - z) * mu_ref[...]
    r = zs[:, :c_a]
    k = zs[:, c_a:2 * c_a]
    v = zs[:, 2 * c_a:3 * c_a]
    o = 3 * c_a
    xw = zs[:, o:o + r_w]
    xa = zs[:, o + r_w:o + r_w + r_a]
    xg = zs[:, o + r_w + r_a:o + r_w + r_a + r_g]
    warg = -(w0_ref[...] + _dot3(jnp.tanh(xw), wup_ref[...]))
    softplus = jnp.maximum(warg, 0.0) + jnp.log1p(jnp.exp(-jnp.abs(warg)))
    lw = -jnp.exp(-softplus - 0.5)
    a = jax.nn.sigmoid(a0_ref[...] + _dot3(xa, aup_ref[...]))
    g = _dot3(jax.nn.sigmoid(xg), gup_ref[...])
    kkr = k * kk_ref[...]
    kk = kkr / jnp.maximum(jnp.sqrt(_mask_dot(kkr * kkr, hsum_ref[...])), 1e-12)
    k2 = k * (1.0 + (a - 1.0) * ka_ref[...])
    bonus = _mask_dot(r * k2 * rk_ref[...], hsum_ref[...]) * v
    kb = kk * a
    if chunk == 1:
        outs = (r, jnp.exp(lw), k2, v, -kk, kb, g, bonus)
        for ref, val in zip(out_refs, outs):
            ref[0] = val
        return
    pieces = _pieces(lw, 3)
    ones3 = jnp.ones((chunk, 3 * chunk), BF16)
    cw, cwl = [], []
    for c in range(tt // chunk):
        stack = jnp.concatenate([p[c * chunk:(c + 1) * chunk] for p in pieces], axis=0)
        cw.append(jnp.dot(tril_ref[...], stack, preferred_element_type=F32))
        cwl.append(jnp.dot(ones3, stack, preferred_element_type=F32))
    cw = jnp.concatenate(cw, axis=0)
    cwl = jnp.concatenate(cwl, axis=0)
    e_inv = jnp.exp(-cw)
    e_hat = jnp.exp(cwl - cw)
    outs = (r * jnp.exp(cw), -kk * jnp.exp(cw - lw), kb * e_inv, k2 * e_inv, kb * e_hat, k2 * e_hat, v, g, bonus)
    for ref, val in zip(out_refs[:-1], outs):
        ref[0] = val
    wl = jnp.exp(cwl)
    for c in range(tt // chunk):
        out_refs[-1][0, c] = wl[c * chunk:c * chunk + 1, :]


def _head_sum_matrix(width):
    idx = jnp.arange(width)
    return ((idx[:, None] // HEAD) == (idx[None, :] // HEAD)).astype(BF16)


def rwkv_prep(z, prev, p, chunk, tt):
    bsz, t, cs = z.shape
    c_a = p['w0'].shape[-1]
    assert prev.shape[1] == (t if chunk == 1 else 1)
    ridx = jnp.arange(chunk)
    tril = jnp.tile((ridx[None, :] <= ridx[:, None]).astype(BF16), (1, 3))
    hsum = _head_sum_matrix(c_a)
    row = lambda x: x.reshape(1, -1)
    tile = pl.BlockSpec((1, tt, c_a), lambda b, i: (b, i, 0))
    full = jax.ShapeDtypeStruct((bsz, t, c_a), F32)
    if chunk == 1:
        out_specs, out_shape = [tile] * 8, [full] * 8
        prev_spec = pl.BlockSpec((1, tt, cs), lambda b, i: (b, i, 0))
    else:
        out_specs = [tile] * 9 + [pl.BlockSpec((1, tt // chunk, 1, c_a), lambda b, i: (b, i, 0, 0))]
        out_shape = [full] * 9 + [jax.ShapeDtypeStruct((bsz, t // chunk, 1, c_a), F32)]
        prev_spec = pl.BlockSpec((1, 1, cs), lambda b, i: (b, 0, 0))
    consts = [row(p['mu']), row(p['w0']), p['w_up'], row(p['a0']), p['a_up'], p['g_up'], row(p['k_k']),
              row(p['k_a']), row(p['r_k']), hsum, tril]
    return pl.pallas_call(
        functools.partial(_rwkv_prep_kernel, chunk), name="rwkv_prep",
        grid=(bsz, t // tt),
        in_specs=[pl.BlockSpec((1, tt, cs), lambda b, i: (b, i, 0)), prev_spec] + [_const_spec(c.shape) for c in consts],
        out_specs=out_specs,
        out_shape=out_shape,
        scratch_shapes=[pltpu.VMEM((1, cs), F32)],
        compiler_params=_params(("parallel", "arbitrary")),
    )(z, prev, *consts)


def _wkv_chunk_kernel(ra_ref, at_ref, bt_ref, kt_ref, bh_ref, kh_ref, v_ref, wl_ref, rp_ref, y0_ref, m_ref, n_ref):
    n_chunks = m_ref.shape[1]
    length = ra_ref.shape[1] // n_chunks
    heads = ra_ref.shape[2] // HEAD
    row = lax.broadcasted_iota(jnp.int32, (length, length), 0)
    col = lax.broadcasted_iota(jnp.int32, (length, length), 1)
    strict = row > col
    incl = row >= col
    hrow = lax.broadcasted_iota(jnp.int32, (HEAD, HEAD), 0)
    hcol = lax.broadcasted_iota(jnp.int32, (HEAD, HEAD), 1)
    units = [(c, h) for c in range(n_chunks) for h in range(heads)]
    us = range(len(units))
    at = lambda ref, u: ref[0, units[u][0] * length:(units[u][0] + 1) * length,
                            units[u][1] * HEAD:(units[u][1] + 1) * HEAD]
    gram = [_bdot_t(jnp.concatenate([at(at_ref, u), at(ra_ref, u)], axis=0),
                    jnp.concatenate([at(bt_ref, u), at(kt_ref, u)], axis=0)) for u in us]
    a_ab = [jnp.where(strict, g[:length, :length], 0.0) for g in gram]
    a_kk = [jnp.concatenate([jnp.where(strict, g[:length, length:], 0.0),
                             jnp.where(incl, g[length:, length:], 0.0)], axis=0) for g in gram]
    a_rb = [jnp.where(incl, g[length:, :length], 0.0) for g in gram]
    inv = [jnp.where(row == col, 1.0, a) for a in a_ab]
    pw = [_bdot(a, a) for a in a_ab]
    avy = [_dot3(a_kk[u], at(v_ref, u)) for u in us]
    nk = [_dot3(at(v_ref, u).T, at(kh_ref, u)) for u in us]
    for _ in range(int(math.log2(length)) - 2):
        both = [_bdot(jnp.concatenate([pw[u], inv[u]], axis=0), pw[u]) for u in us]
        pw = [b[:length] for b in both]
        inv = [inv[u] + both[u][length:] for u in us]
    inv = [inv[u] + _bdot(inv[u], pw[u]) for u in us]
    pq = [_dot3(inv[u], jnp.concatenate([at(at_ref, u), avy[u][:length]], axis=1)) for u in us]
    ry = [_dot3(a_rb[u], pq[u]) for u in us]
    mn = [_dot3(pq[u].T, at(bh_ref, u)) for u in us]
    lanes = lambda f, c: jnp.concatenate([f(c * heads + h) for h in range(heads)], axis=1)
    rows = lambda f: jnp.concatenate([lanes(f, c) for c in range(n_chunks)], axis=0)
    rp_ref[0] = rows(lambda u: at(ra_ref, u) + ry[u][:, :HEAD])
    y0_ref[0] = rows(lambda u: avy[u][length:] + ry[u][:, HEAD:])
    for c in range(n_chunks):
        m_ref[0, c] = lanes(lambda u: jnp.where(hrow == hcol, wl_ref[0, c, :, units[u][1] * HEAD:
                                                                   (units[u][1] + 1) * HEAD], 0.0) + mn[u][:HEAD], c)
        n_ref[0, c] = lanes(lambda u: nk[u] + mn[u][HEAD:], c)


def _head_norm_wide(y, hsum):
    yc = y - _mask_dot(y, hsum) * (1.0 / HEAD)
    return yc * lax.rsqrt(_mask_dot(yc * yc, hsum) * (1.0 / HEAD) + LNX_EPS)


def _wkv_scan_kernel(rp_ref, y0_ref, m_ref, n_ref, g_ref, bonus_ref, s0_ref, lg_ref, lb_ref, hsum_ref,
                     o_ref, sout_ref, s_ref):
    c = pl.program_id(1)
    nb = rp_ref.shape[0]
    heads = rp_ref.shape[2] // HEAD

    @pl.when(c == 0)
    def _():
        s_ref[...] = s0_ref[...]

    pairs = [(b, h, slice(h * HEAD, (h + 1) * HEAD)) for b in range(nb) for h in range(heads)]
    s = [s_ref[b, h] for b, h, _ in pairs]
    s_new = [n_ref[b, 0, :, sl] + _dot3(s[j], m_ref[b, 0, :, sl]) for j, (b, h, sl) in enumerate(pairs)]
    ys = [y0_ref[b, :, sl] + _dot3_t(rp_ref[b, :, sl], s[j]) for j, (b, h, sl) in enumerate(pairs)]
    for j, (b, h, _) in enumerate(pairs):
        s_ref[b, h] = s_new[j]
    length = rp_ref.shape[1]
    y = _head_norm_wide(jnp.concatenate([jnp.concatenate(ys[b * heads:(b + 1) * heads], axis=-1) for b in range(nb)],
                                        axis=0), hsum_ref[...])
    for b in range(nb):
        o = (y[b * length:(b + 1) * length] * lg_ref[...] + lb_ref[...] + bonus_ref[b]) * g_ref[b]
        o_ref[b] = o.astype(o_ref.dtype)

    @pl.when(c == pl.num_programs(1) - 1)
    def _():
        sout_ref[...] = s_ref[...]


def wkv(prep, s0, lnx_g, lnx_b, chunk):
    ra, at, bt, kt, bh, kh, v, g, bonus, wl = prep
    bsz, t, c_a = ra.shape
    heads = c_a // HEAD
    n_chunks = t // chunk
    assert chunk & (chunk - 1) == 0 and chunk >= 4 and t % chunk == 0
    per_step = CHUNKS_PER_STEP if n_chunks % CHUNKS_PER_STEP == 0 else 1
    tile = pl.BlockSpec((1, per_step * chunk, c_a), lambda b, c: (b, c, 0))
    mat = pl.BlockSpec((1, per_step, HEAD, c_a), lambda b, c: (b, c, 0, 0))
    full = jax.ShapeDtypeStruct((bsz, t, c_a), F32)
    mats = jax.ShapeDtypeStruct((bsz, n_chunks, HEAD, c_a), F32)
    rp, y0, m, n = pl.pallas_call(
        _wkv_chunk_kernel, name="wkv_chunk",
        grid=(bsz, n_chunks // per_step),
        in_specs=[tile] * 7 + [pl.BlockSpec((1, per_step, 1, c_a), lambda b, c: (b, c, 0, 0))],
        out_specs=[tile, tile, mat, mat],
        out_shape=[full, full, mats, mats],
        compiler_params=_params(("parallel", "parallel")),
    )(ra, at, bt, kt, bh, kh, v, wl)
    nb = SCAN_BATCH if bsz % SCAN_BATCH == 0 else 1
    tile = pl.BlockSpec((nb, chunk, c_a), lambda b, c: (b, c, 0))
    mat = pl.BlockSpec((nb, 1, HEAD, c_a), lambda b, c: (b, c, 0, 0))
    st = pl.BlockSpec((nb, heads, HEAD, HEAD), lambda b, c: (b, 0, 0, 0))
    return pl.pallas_call(
        _wkv_scan_kernel, name="wkv_scan",
        grid=(bsz // nb, n_chunks),
        in_specs=[tile, tile, mat, mat, tile, tile, st, _const_spec((1, c_a)), _const_spec((1, c_a)),
                  _const_spec((c_a, c_a))],
        out_specs=[tile, st],
        out_shape=[jax.ShapeDtypeStruct((bsz, t, c_a), BF16), jax.ShapeDtypeStruct(s0.shape, F32)],
        scratch_shapes=[pltpu.VMEM((nb, heads, HEAD, HEAD), F32)],
        compiler_params=_params(("parallel", "arbitrary")),
    )(rp, y0, m, n, g, bonus, s0, lnx_g.reshape(1, c_a), lnx_b.reshape(1, c_a), _head_sum_matrix(c_a))


def _wkv_step_kernel(r_ref, w_ref, k_ref, v_ref, a_ref, b_ref, g_ref, bonus_ref, s_ref, lg_ref, lb_ref, hsum_ref,
                     o_ref, sout_ref):
    bt, heads = s_ref.shape[0], s_ref.shape[1]
    c_a = heads * HEAD
    hsum = hsum_ref[...]
    diag = (lax.broadcasted_iota(jnp.int32, (HEAD, c_a), 1) % HEAD
            == lax.broadcasted_iota(jnp.int32, (HEAD, c_a), 0))
    seqs = range(bt)
    row = lambda ref, i: ref[i:i + 1, :]
    stack = lambda f: jnp.concatenate([f(i) for i in seqs], axis=0)
    piece = lambda x, i: x[i * HEAD:(i + 1) * HEAD]
    s = stack(lambda i: jnp.concatenate([s_ref[i, h] for h in range(heads)], axis=1))
    sa = _mask_dot(s * stack(lambda i: jnp.broadcast_to(row(a_ref, i), (HEAD, c_a))), hsum)
    v_rows = _mask_dot(stack(lambda i: jnp.where(diag, row(v_ref, i), 0.0)), hsum)
    s = stack(lambda i: piece(s, i) * row(w_ref, i) + piece(sa, i) * row(b_ref, i) + piece(v_rows, i) * row(k_ref, i))
    for i in seqs:
        for h in range(heads):
            sout_ref[i, h] = piece(s, i)[:, h * HEAD:(h + 1) * HEAD]
    y_rows = _mask_dot(stack(lambda i: piece(s, i) * row(r_ref, i)), hsum)
    y = stack(lambda i: jnp.sum(jnp.where(diag, piece(y_rows, i), 0.0), axis=0, keepdims=True))
    y = _head_norm_wide(y, hsum)
    o_ref[...] = (y * lg_ref[...] + lb_ref[...] + bonus_ref[...]) * g_ref[...]


def wkv_step(ops, s0, lnx_g, lnx_b, bt):
    bsz, c_a = ops[0].shape
    rows = pl.BlockSpec((bt, c_a), lambda i: (i, 0))
    st = pl.BlockSpec((bt,) + s0.shape[1:], lambda i: (i, 0, 0, 0))
    hsum = _head_sum_matrix(c_a)
    return pl.pallas_call(
        _wkv_step_kernel, name="wkv_step",
        grid=(bsz // bt,),
        in_specs=[rows] * 8 + [st, _const_spec((1, c_a)), _const_spec((1, c_a)), _const_spec(hsum.shape)],
        out_specs=[rows, st],
        out_shape=[jax.ShapeDtypeStruct((bsz, c_a), F32), jax.ShapeDtypeStruct(s0.shape, F32)],
        compiler_params=_params(("parallel",)),
    )(*ops, s0, lnx_g.reshape(1, c_a), lnx_b.reshape(1, c_a), hsum)


def rope_tables(pos):
    inv_freq = ROPE_THETA ** (-jnp.arange(ROT_HALF, dtype=F32) / ROT_HALF)
    ang = pos[:, None] * inv_freq[None, :]
    cos, sin = jnp.cos(ang), jnp.sin(ang)
    t = pos.shape[0]
    rest = HEAD - 2 * ROT_HALF
    c = jnp.concatenate([cos, cos, jnp.ones((t, rest), F32)], axis=1)
    s1 = jnp.concatenate([jnp.zeros((t, ROT_HALF), F32), sin, jnp.zeros((t, rest), F32)], axis=1)
    s2 = jnp.concatenate([-sin, jnp.zeros((t, HEAD - ROT_HALF), F32)], axis=1)
    rep = LANES // HEAD
    return tuple(jnp.tile(x, (1, rep)) for x in (c, s1, s2))


def _rope(x, c, s1, s2):
    width = x.shape[-1]
    rep = width // c.shape[-1]
    if rep > 1:
        c, s1, s2 = (jnp.concatenate([t] * rep, axis=1) for t in (c, s1, s2))
    return x * c + pltpu.roll(x, ROT_HALF, 1) * s1 + pltpu.roll(x, width - ROT_HALF, 1) * s2


def _attn_kernel(window, q_w, kv_w, zq_ref, kp_ref, vp_ref, rc_ref, rs1_ref, rs2_ref, pc_ref, ps1_ref, ps2_ref,
                 mk_ref, mv_ref, sink_ref, ob_ref, om_ref, kr_ref):
    n = pl.program_id(1)
    zq = zq_ref[0]
    scale = HEAD ** -0.5
    q = zq[:, :q_w]
    k = zq[:, q_w:q_w + kv_w]
    v = zq[:, q_w + kv_w:q_w + 2 * kv_w]
    qm = zq[:, q_w + 2 * kv_w:]
    qr = _rope(q, rc_ref[...], rs1_ref[...], rs2_ref[...]) * scale
    kr = _rope(k, rc_ref[...], rs1_ref[...], rs2_ref[...])
    kr_ref[0] = kr
    kpr = _rope(kp_ref[0], pc_ref[...], ps1_ref[...], ps2_ref[...])
    vp = vp_ref[0]
    gqa = q_w // kv_w
    assert window & (window - 1) == 0
    qi = lax.broadcasted_iota(jnp.int32, (gqa * window, 2 * window), 0) & (window - 1)
    kj = lax.broadcasted_iota(jnp.int32, (gqa * window, 2 * window), 1)
    valid = (kj > qi) & (kj <= qi + window) & ((kj >= window) | (n > 0))
    tdot = lambda a, b: lax.dot_general(a, b, (((1,), (1,)), ((), ())), preferred_element_type=F32)
    dot = lambda a, b: jnp.dot(a, b, preferred_element_type=F32)
    kvs = range(kv_w // HEAD)
    mhs = range(qm.shape[1] // HEAD)
    hsl = lambda h: slice(h * HEAD, (h + 1) * HEAD)
    qmb = (qm * scale).astype(BF16)
    mk = mk_ref[0].astype(BF16)
    mv = mv_ref[0].astype(BF16)
    kcat = [jnp.concatenate([kpr[:, hsl(hk)], kr[:, hsl(hk)]], axis=0).astype(BF16) for hk in kvs]
    vcat = [jnp.concatenate([vp[:, hsl(hk)], v[:, hsl(hk)]], axis=0).astype(BF16) for hk in kvs]
    qs = [jnp.concatenate([qr[:, hsl(hk * gqa + g)] for g in range(gqa)], axis=0).astype(BF16) for hk in kvs]
    s = [jnp.where(valid, tdot(qs[hk], kcat[hk]), NEG_INF) for hk in kvs]
    sm = [tdot(qmb[:, hsl(h)], mk[:, hsl(h)]) for h in mhs]
    sink = [jnp.concatenate([jnp.full((window, 1), sink_ref[hk * gqa + g], F32) for g in range(gqa)], axis=0)
            for hk in kvs]
    m = [jnp.maximum(jnp.max(s[hk], axis=-1, keepdims=True), sink[hk]) for hk in kvs]
    p = [jnp.exp(s[hk] - m[hk]) for hk in kvs]
    pm = [jnp.exp(sm[h] - jnp.max(sm[h], axis=-1, keepdims=True)) for h in mhs]
    o = [dot(p[hk].astype(BF16), vcat[hk])
         / (jnp.sum(p[hk], axis=-1, keepdims=True) + jnp.exp(sink[hk] - m[hk])) for hk in kvs]
    om = [dot(pm[h].astype(BF16), mv[:, hsl(h)]) / jnp.sum(pm[h], axis=-1, keepdims=True) for h in mhs]
    ob = jnp.concatenate([o[hk][g * window:(g + 1) * window] for hk in kvs for g in range(gqa)], axis=1)
    ob_ref[0] = ob.astype(ob_ref.dtype)
    om_ref[0] = jnp.concatenate(om, axis=1).astype(om_ref.dtype)


def attn_prompt(zq, mk, mv, sinks, tables, window, q_w, kv_w):
    bsz, t, zw = zq.shape
    mem_w = zw - q_w - 2 * kv_w
    assert kv_w == LANES and q_w % kv_w == 0
    kcol, vcol = q_w // kv_w, q_w // kv_w + 1
    prev = lambda n: jnp.maximum(n - 1, 0)
    tab = pl.BlockSpec((window, LANES), lambda b, n: (n, 0))
    ptab = pl.BlockSpec((window, LANES), lambda b, n: (prev(n), 0))
    mem = pl.BlockSpec((1,) + mk.shape[1:], lambda b, n: (b, 0, 0))
    return pl.pallas_call(
        functools.partial(_attn_kernel, window, q_w, kv_w), name="attn",
        grid=(bsz, t // window),
        in_specs=[pl.BlockSpec((1, window, zw), lambda b, n: (b, n, 0)),
                  pl.BlockSpec((1, window, kv_w), lambda b, n: (b, prev(n), kcol)),
                  pl.BlockSpec((1, window, kv_w), lambda b, n: (b, prev(n), vcol)),
                  tab, tab, tab, ptab, ptab, ptab, mem, mem,
                  pl.BlockSpec(memory_space=pltpu.SMEM)],
        out_specs=[pl.BlockSpec((1, window, q_w), lambda b, n: (b, n, 0)),
                   pl.BlockSpec((1, window, mem_w), lambda b, n: (b, n, 0)),
                   pl.BlockSpec((1, window, kv_w), lambda b, n: (b, n, 0))],
        out_shape=[jax.ShapeDtypeStruct((bsz, t, q_w), BF16), jax.ShapeDtypeStruct((bsz, t, mem_w), BF16),
                   jax.ShapeDtypeStruct((bsz, t, kv_w), F32)],
        compiler_params=_params(("parallel", "parallel")),
    )(zq, zq, zq, *tables, *tables, mk, mv, sinks)


def _attn_step_kernel(q_w, kv_w, zq_ref, ck_ref, cv_ref, mk_ref, mv_ref, rc_ref, rs1_ref, rs2_ref, sink_ref,
                      ob_ref, om_ref, nk_ref, nv_ref):
    bt = zq_ref.shape[0]
    window = ck_ref.shape[1]
    mem_w = om_ref.shape[1]
    n_q, gqa, per_vreg = q_w // HEAD, q_w // kv_w, LANES // HEAD
    scale = HEAD ** -0.5
    zq = zq_ref[...]
    q = _rope(zq[:, :q_w], rc_ref[...], rs1_ref[...], rs2_ref[...]) * scale
    k_new = _rope(zq[:, q_w:q_w + kv_w], rc_ref[...], rs1_ref[...], rs2_ref[...])
    v_new = zq[:, q_w + kv_w:q_w + 2 * kv_w]
    qm = zq[:, q_w + 2 * kv_w:] * scale
    own = lambda w: (lax.broadcasted_iota(jnp.int32, (n_q, w), 1) // HEAD
                     == lax.broadcasted_iota(jnp.int32, (n_q, w), 0))
    own_q, own_m = own(q_w), own(mem_w)
    hrow = lax.broadcasted_iota(jnp.int32, (n_q, LANES), 0)
    hblk = lax.broadcasted_iota(jnp.int32, (n_q, LANES), 1) // HEAD
    swap = (hrow % per_vreg) != (hrow // gqa)
    keep = hblk == hrow % per_vreg
    key_ok = lax.broadcasted_iota(jnp.int32, (n_q, window), 1) >= 1
    wrow = lax.broadcasted_iota(jnp.int32, (window, kv_w), 0)
    sink = sink_ref[...]
    bs = range(bt)
    tdot = lambda a, b: lax.dot_general(a.astype(BF16), b.astype(BF16), (((1,), (1,)), ((), ())),
                                        preferred_element_type=F32)
    for b in bs:
        nk_ref[b] = jnp.where(wrow == window - 1, k_new[b:b + 1], pltpu.roll(ck_ref[b], window - 1, 0))
        nv_ref[b] = jnp.where(wrow == window - 1, v_new[b:b + 1], pltpu.roll(cv_ref[b], window - 1, 0))
    q8 = []
    for b in bs:
        rep = jnp.where(own_q, q[b:b + 1], 0.0)
        fold = sum(rep[:, c * LANES:(c + 1) * LANES] for c in range(q_w // LANES))
        q8.append(jnp.where(swap, pltpu.roll(fold, HEAD, 1), fold))
    qm8 = [jnp.where(own_m, qm[b:b + 1], 0.0) for b in bs]
    s = [jnp.where(key_ok, tdot(q8[b], ck_ref[b]), NEG_INF) for b in bs]
    sm = [tdot(qm8[b], mk_ref[b]) for b in bs]
    s_new = [jnp.sum(q8[b] * k_new[b:b + 1], axis=1, keepdims=True) for b in bs]
    m = [jnp.maximum(jnp.maximum(jnp.max(s[b], axis=1, keepdims=True), s_new[b]), sink) for b in bs]
    p = [jnp.exp(s[b] - m[b]) for b in bs]
    p_new = [jnp.exp(s_new[b] - m[b]) for b in bs]
    pm = [jnp.exp(sm[b] - jnp.max(sm[b], axis=1, keepdims=True)) for b in bs]
    o8 = [(_bdot(p[b], cv_ref[b]) + p_new[b] * v_new[b:b + 1])
          / (jnp.sum(p[b], axis=1, keepdims=True) + p_new[b] + jnp.exp(sink - m[b])) for b in bs]
    om8 = [_bdot(pm[b], mv_ref[b]) / jnp.sum(pm[b], axis=1, keepdims=True) for b in bs]
    ob, om = [], []
    for b in bs:
        o = jnp.where(swap, pltpu.roll(o8[b], HEAD, 1), o8[b])
        o = jnp.concatenate([jnp.where(keep, o, 0.0)] * (q_w // LANES), axis=1)
        ob.append(jnp.sum(jnp.where(own_q, o, 0.0), axis=0, keepdims=True))
        om.append(jnp.sum(jnp.where(own_m, om8[b], 0.0), axis=0, keepdims=True))
    ob_ref[...] = jnp.concatenate(ob, axis=0)
    om_ref[...] = jnp.concatenate(om, axis=0)


def attn_step(zq, ck, cv, mk, mv, sinks, tables, q_w, kv_w, bt):
    bsz, zw = zq.shape
    mem_w = zw - q_w - 2 * kv_w
    window, n_mem = ck.shape[1], mk.shape[1]
    n_q = q_w // HEAD
    assert kv_w == LANES and mem_w // HEAD <= n_q and n_q == 8
    rows = lambda w: pl.BlockSpec((bt, w), lambda i: (i, 0))
    cache = pl.BlockSpec((bt, window, kv_w), lambda i: (i, 0, 0))
    mem = pl.BlockSpec((bt, n_mem, mem_w), lambda i: (i, 0, 0))
    consts = list(tables) + [sinks.reshape(n_q, 1)]
    const_specs = [_const_spec(c.shape) for c in consts]
    return pl.pallas_call(
        functools.partial(_attn_step_kernel, q_w, kv_w), name="attn_step",
        grid=(bsz // bt,),
        in_specs=[rows(zw), cache, cache, mem, mem] + const_specs,
        out_specs=[rows(q_w), rows(mem_w), cache, cache],
        out_shape=[jax.ShapeDtypeStruct((bsz, q_w), F32), jax.ShapeDtypeStruct((bsz, mem_w), F32),
                   jax.ShapeDtypeStruct(ck.shape, F32), jax.ShapeDtypeStruct(cv.shape, F32)],
        compiler_params=_params(("parallel",)),
    )(zq, ck, cv, mk, mv, *consts)


def _proj_kernel(x_ref, w_ref, o_ref):
    o_ref[...] = jnp.dot(x_ref[...].astype(BF16), w_ref[...], preferred_element_type=F32)


def proj(x, w, tm):
    n, d = x.shape
    return pl.pallas_call(
        _proj_kernel, name="proj",
        grid=(n // tm,),
        in_specs=[pl.BlockSpec((tm, d), lambda i: (i, 0)), _const_spec(w.shape)],
        out_specs=pl.BlockSpec((tm, w.shape[1]), lambda i: (i, 0)),
        out_shape=jax.ShapeDtypeStruct((n, w.shape[1]), F32),
        compiler_params=_params(("parallel",)),
    )(x, w)


def _merge_kernel(alpha, n_tiles, x_ref, oa_ref, ob_ref, om_ref, gt_ref, lig_ref, lib_ref, pa_ref, pb_ref, pm_ref,
                  wo_ref, l1g_ref, l1b_ref, wr_ref, br_ref, *refs):
    outs = refs[-4:]

    @pl.when(pl.program_id(0) >= n_tiles)
    def _():
        for ref in outs:
            ref[...] = jnp.zeros_like(ref)

    @pl.when(pl.program_id(0) < n_tiles)
    def _():
        _merge_tile(alpha, x_ref, oa_ref, ob_ref, om_ref, gt_ref, lig_ref, lib_ref, pa_ref, pb_ref, pm_ref, wo_ref,
                    l1g_ref, l1b_ref, wr_ref, br_ref, *outs)


def _merge_tile(alpha, x_ref, oa_ref, ob_ref, om_ref, gt_ref, lig_ref, lib_ref, pa_ref, pb_ref, pm_ref, wo_ref,
                l1g_ref, l1b_ref, wr_ref, br_ref, x1_ref, x1t_ref, eidx_ref, gate_ref):
    d = x_ref.shape[1]
    xn = _ln(x_ref[...], lig_ref[...], lib_ref[...])
    gts = jax.nn.sigmoid(gt_ref[...].astype(F32))
    merged = (gts[:, :d] * _bdot(oa_ref[...], pa_ref[...]) + gts[:, d:2 * d] * _bdot(ob_ref[...], pb_ref[...])
              + gts[:, 2 * d:] * _bdot(om_ref[...], pm_ref[...]))
    x1 = _ln(alpha * xn + _bdot(merged, wo_ref[...]), l1g_ref[...], l1b_ref[...])
    x1_ref[...] = x1
    _rows_to_tiles(x1t_ref, x1)
    x_hi, x_lo = _pieces(x1, 2)
    w_hi, w_lo = _pieces(wr_ref[...], 2)
    dot = lambda a, b: jnp.dot(a, b, preferred_element_type=F32)
    logits = dot(x_hi, w_hi) + dot(x_hi, w_lo) + dot(x_lo, w_hi) + br_ref[...]
    lane = lax.broadcasted_iota(jnp.int32, logits.shape, 1)
    lane_f = lane.astype(F32)
    first = lambda hit: jnp.min(jnp.where(hit, lane_f, float(LANES)), axis=-1, keepdims=True).astype(jnp.int32)
    gmask = lane < N_GROUPS
    gl = jnp.where(gmask, logits, NEG_INF)
    gmax = jnp.max(gl, axis=-1, keepdims=True)
    gidx = first(gl == gmax)
    g_w = 1.0 / jnp.sum(jnp.where(gmask, jnp.exp(gl - gmax), 0.0), axis=-1, keepdims=True)
    lo = N_GROUPS + gidx * EXPERTS_PER_GROUP
    el = jnp.where((lane >= lo) & (lane < lo + EXPERTS_PER_GROUP), logits, NEG_INF)
    v1 = jnp.max(el, axis=-1, keepdims=True)
    i1 = first(el == v1)
    el2 = jnp.where(lane == i1, NEG_INF, el)
    v2 = jnp.max(el2, axis=-1, keepdims=True)
    i2 = first(el2 == v2)
    e2 = jnp.exp(v2 - v1)
    gate1 = g_w / (1.0 + e2)
    eidx_ref[...] = jnp.where(lane == 0, i1 - N_GROUPS, jnp.where(lane == 1, i2 - N_GROUPS, 0))
    gate_ref[...] = jnp.where(lane == 0, gate1, jnp.where(lane == 1, gate1 * e2, 0.0))


def merge(x, oa, ob, om, gt, w, tm, alpha, n_total, row_offset=0, into=None):
    n, d = x.shape
    assert row_offset % tm == 0 and n % tm == 0 and n_total % tm == 0
    off = row_offset // tm
    into = list(into or [])
    n_tiles = n // tm
    steps = n_tiles if into else n_total // tm
    rows = lambda a: pl.BlockSpec((tm, a.shape[1]), lambda i: (jnp.minimum(i, n_tiles - 1), 0))
    out = lambda width: pl.BlockSpec((tm, width), lambda i: (i + off, 0))
    consts = [w['ln_in_g'], w['ln_in_b'], w['p_a'], w['p_b'], w['p_m'], w['w_o'], w['ln1_g'], w['ln1_b'],
              w['w_route'], w['b_route']]
    n_in = 5 + len(consts)
    return pl.pallas_call(
        functools.partial(_merge_kernel, alpha, n_tiles), name="merge",
        grid=(steps,),
        in_specs=[rows(a) for a in (x, oa, ob, om, gt)] + [_const_spec(c.shape) for c in consts]
        + [pl.BlockSpec(memory_space=pl.ANY)] * len(into),
        out_specs=[out(d), pl.BlockSpec((tm * TILE_ROWS, LANES), lambda i: (i + off, 0)), out(LANES), out(LANES)],
        out_shape=[jax.ShapeDtypeStruct((n_total, d), F32), jax.ShapeDtypeStruct((n_total * TILE_ROWS, LANES), F32),
                   jax.ShapeDtypeStruct((n_total, LANES), jnp.int32), jax.ShapeDtypeStruct((n_total, LANES), F32)],
        input_output_aliases={n_in + k: k for k in range(len(into))},
        compiler_params=_params(("parallel",)),
    )(x, oa, ob, om, gt, *consts, *into)


ROW_DMA_UNROLL = 8
DRAIN_STEPS = 2


TILE_ROWS = 8


def _rows_from_tiles(ref, n):
    return jnp.concatenate([ref[pl.ds(s, n, stride=TILE_ROWS), :] for s in range(TILE_ROWS)], axis=1)


def _rows_to_tiles(ref, x):
    for s in range(TILE_ROWS):
        ref[pl.ds(s, x.shape[0], stride=TILE_ROWS), :] = x[:, s * LANES:(s + 1) * LANES]


def _row_copies(asg_ref, base, count, n_asg, x_hbm, buf, y_hbm, sem, gather, unrolled):
    def tile(ref, idx):
        start = idx * TILE_ROWS
        return ref.at[pl.ds(start if isinstance(idx, int) else pl.multiple_of(start, TILE_ROWS), TILE_ROWS)]

    def one(r, priority):
        a = asg_ref[base + r]
        if gather:
            tok = jnp.minimum(a, n_asg - 1)
            tok = jnp.where(tok >= n_asg // 2, tok - n_asg // 2, tok)
            copy = pltpu.make_async_copy(tile(x_hbm, tok), tile(buf, r), sem)
        else:
            copy = pltpu.make_async_copy(tile(buf, r), tile(y_hbm, a), sem)
        copy.start(priority=priority)

    if unrolled:
        for r in range(count):
            one(r, r % 2)
        return

    def body(g, carry):
        for j in range(ROW_DMA_UNROLL):
            one(g * ROW_DMA_UNROLL + j, j % 2)
        return carry
    lax.fori_loop(0, count // ROW_DMA_UNROLL, body, 0)


def _moe_expert_kernel(n_asg, asg_ref, be_ref, nu_ref, x_hbm, wg_ref, wu_ref, wd_ref, y_hbm,
                       xbuf, ybuf, wgb, wub, wdb, gsem, ssem):
    i = pl.program_id(0)
    used = nu_ref[0]
    rows = xbuf.shape[1] // TILE_ROWS
    n_blocks = be_ref.shape[0]
    blk = jnp.minimum(i, n_blocks - 1)

    @pl.when((i == 0) | (be_ref[blk] != be_ref[jnp.maximum(blk - 1, 0)]))
    def _():
        wgb[...] = wg_ref[0, 0].astype(BF16)
        wub[...] = wu_ref[0, 0].astype(BF16)
        wdb[...] = wd_ref[0, 0].astype(BF16)

    def wait_gather(slot):
        pltpu.make_async_copy(x_hbm.at[pl.ds(0, rows * TILE_ROWS)], xbuf.at[slot], gsem.at[slot]).wait()

    def wait_scatter(slot):
        pltpu.make_async_copy(ybuf.at[slot], y_hbm.at[pl.ds(0, rows * TILE_ROWS)], ssem.at[slot]).wait()

    def gather(b, slot, unrolled):
        _row_copies(asg_ref, b * rows, rows, n_asg, x_hbm, xbuf.at[slot], y_hbm, gsem.at[slot], True, unrolled)

    def scatter(b, slot, unrolled):
        _row_copies(asg_ref, b * rows, rows, n_asg, x_hbm, ybuf.at[slot], y_hbm, ssem.at[slot], False, unrolled)

    def expert(slot):
        xb = _rows_from_tiles(xbuf.at[slot], rows).astype(BF16)
        hg = jnp.dot(xb, wgb[...], preferred_element_type=F32)
        hu = jnp.dot(xb, wub[...], preferred_element_type=F32)
        h = hg * jax.nn.sigmoid(hg) * hu
        _rows_to_tiles(ybuf.at[slot], jnp.dot(h.astype(BF16), wdb[...], preferred_element_type=F32))

    @pl.when(i == 0)
    def _():
        ybuf[1] = jnp.zeros(ybuf.shape[1:], F32)
        spare = [pltpu.make_async_copy(ybuf.at[1],
                                       y_hbm.at[pl.ds((n_asg + j * rows) * TILE_ROWS, rows * TILE_ROWS)], ssem.at[1])
                 for j in range((y_hbm.shape[0] // TILE_ROWS - n_asg) // rows)]
        for copy in spare:
            copy.start()
        for copy in spare:
            copy.wait()

    @pl.when((i >= 2) & (i - 2 < used))
    def _():
        wait_scatter(i % 2)

    steady = (i >= 1) & (i + 1 < used)
    for slot in range(2):
        @pl.when(steady & (i % 2 == slot))
        def _():
            wait_gather(slot)
            gather(i + 1, 1 - slot, True)
            scatter(i - 1, 1 - slot, True)
            expert(slot)

    @pl.when(jnp.logical_not(steady))
    def _():
        slot = i % 2

        @pl.when((i == 0) & (used > 0))
        def _():
            gather(0, 0, False)

        @pl.when(i < used)
        def _():
            wait_gather(slot)

        @pl.when(i + 1 < used)
        def _():
            gather(i + 1, 1 - slot, False)

        @pl.when((i >= 1) & (i - 1 < used))
        def _():
            scatter(i - 1, 1 - slot, False)

        @pl.when(i < used)
        def _():
            expert(slot)


def moe_experts(x1_tiles, n_asg, asg, blk_e, n_used, e_gate, e_up, e_down):
    d = e_gate.shape[2]
    assert d == TILE_ROWS * LANES and x1_tiles.shape[1] == LANES
    n_blocks = blk_e.shape[0]
    ff = e_gate.shape[-1]
    n_rows = n_asg + e_gate.shape[1] * EXPERT_BLOCK
    weight = lambda shape: pl.BlockSpec(
        (1, 1) + shape, lambda i, asg, be, nu: (0, be[jnp.minimum(i, n_blocks - 1)], 0, 0))
    return pl.pallas_call(
        functools.partial(_moe_expert_kernel, n_asg), name="moe_expert",
        grid_spec=pltpu.PrefetchScalarGridSpec(
            num_scalar_prefetch=3,
            grid=(n_blocks + DRAIN_STEPS,),
            in_specs=[pl.BlockSpec(memory_space=pl.ANY), weight((d, ff)), weight((d, ff)), weight((ff, d))],
            out_specs=pl.BlockSpec(memory_space=pl.ANY),
            scratch_shapes=[pltpu.VMEM((2, EXPERT_BLOCK * TILE_ROWS, LANES), F32),
                            pltpu.VMEM((2, EXPERT_BLOCK * TILE_ROWS, LANES), F32),
                            pltpu.VMEM((d, ff), BF16), pltpu.VMEM((d, ff), BF16), pltpu.VMEM((ff, d), BF16),
                            pltpu.SemaphoreType.DMA((2,)), pltpu.SemaphoreType.DMA((2,))]),
        out_shape=jax.ShapeDtypeStruct((n_rows * TILE_ROWS, LANES), F32),
        compiler_params=_params(("arbitrary",)),
    )(asg, blk_e, n_used, x1_tiles, e_gate, e_up, e_down)


def _moe_combine_kernel(alpha, lead_tiles, y0_ref, y1_ref, x1_ref, gate_ref, g_ref, b_ref, lead_ref, tail_ref):
    i = pl.program_id(0)
    gate = gate_ref[...]
    tm = x1_ref.shape[0]
    moe = gate[:, 0:1] * _rows_from_tiles(y0_ref, tm) + gate[:, 1:2] * _rows_from_tiles(y1_ref, tm)
    out = _ln(alpha * x1_ref[...] + moe, g_ref[...], b_ref[...])

    @pl.when(i < lead_tiles)
    def _():
        lead_ref[...] = out

    @pl.when(i >= lead_tiles)
    def _():
        tail_ref[...] = out


def moe_combine(y, x1, gate, g, b, tm, alpha, n, n_lead):
    d = x1.shape[1]
    assert n_lead % tm == 0 and (n - n_lead) % tm == 0 and n > n_lead
    lead_tiles = n_lead // tm
    return pl.pallas_call(
        functools.partial(_moe_combine_kernel, alpha, lead_tiles), name="moe_combine",
        grid=(n // tm,),
        in_specs=[pl.BlockSpec((tm * TILE_ROWS, LANES), lambda i: (i, 0)),
                  pl.BlockSpec((tm * TILE_ROWS, LANES), lambda i: (i + n // tm, 0)),
                  pl.BlockSpec((tm, d), lambda i: (i, 0)),
                  pl.BlockSpec((tm, LANES), lambda i: (i, 0)), _const_spec((1, d)), _const_spec((1, d))],
        out_specs=[pl.BlockSpec((tm, d), lambda i: (jnp.minimum(i, lead_tiles - 1), 0)),
                   pl.BlockSpec((tm, d), lambda i: (jnp.maximum(i - lead_tiles, 0), 0))],
        out_shape=[jax.ShapeDtypeStruct((n_lead, d), F32), jax.ShapeDtypeStruct((n - n_lead, d), F32)],
        compiler_params=_params(("arbitrary",)),
    )(y, y, x1, gate, g.reshape(1, d), b.reshape(1, d))


def moe_routing(experts):
    n, top_k = experts.shape
    n_exp = N_GROUPS * EXPERTS_PER_GROUP
    a = n * top_k
    flat_e = experts.T.reshape(a)
    onehot = (flat_e[:, None] == jnp.arange(n_exp, dtype=jnp.int32)[None, :]).astype(jnp.int32)
    rank = jnp.take_along_axis(jnp.cumsum(onehot, axis=0) - onehot, flat_e[:, None], axis=1)[:, 0]
    counts = jnp.sum(onehot, axis=0)
    padded = (counts + EXPERT_BLOCK - 1) // EXPERT_BLOCK * EXPERT_BLOCK
    pad_end = jnp.cumsum(padded)
    dest = (pad_end - padded)[flat_e] + rank
    n_blocks = -(-a // EXPERT_BLOCK) + n_exp
    blk_start = jnp.arange(n_blocks, dtype=jnp.int32) * EXPERT_BLOCK
    blk_e = jnp.minimum(jnp.sum((pad_end[None, :] <= blk_start[:, None]).astype(jnp.int32), axis=1), n_exp - 1)
    spare = jnp.arange(n_blocks * EXPERT_BLOCK, dtype=jnp.int32) - jnp.repeat(jnp.cumsum(counts)[blk_e], EXPERT_BLOCK)
    spare = a + jnp.clip(spare, 0, n_exp * EXPERT_BLOCK - 1)
    asg = spare.astype(jnp.int32).at[dest].set(jnp.arange(a, dtype=jnp.int32), unique_indices=True)
    n_used = (pad_end[-1:] // EXPERT_BLOCK).astype(jnp.int32)
    return asg, blk_e.astype(jnp.int32), n_used


def hier_moe_ln(x1, x1_tiles, eidx, gate, w, tm, alpha, n_tokens, n_lead):
    asg, blk_e, n_used = moe_routing(eidx[:n_tokens, :2])
    y = moe_experts(x1_tiles, 2 * n_tokens, asg, blk_e, n_used, w['e_gate'], w['e_up'], w['e_down'])
    return moe_combine(y, x1, gate, w['ln2_g'], w['ln2_b'], tm, alpha, n_tokens, n_lead)


def kernel(x_prompt, x_sample, mem_prompt, state_wkv, state_shift, cache_win_k, cache_win_v, cache_mem_k, cache_mem_v, ln_in_g, ln_in_b, w_in, mu, w0, w_up, a0, a_up, g_up, k_k, k_a, r_k, lnx_g, lnx_b, sinks, w_mem_kv, p_a, p_b, p_m, w_o, ln1_g, ln1_b, w_group, b_group, w_router, b_router, e_gate, e_up, e_down, ln2_g, ln2_b):
    depth = w_in.shape[0]
    assert depth == 1, "single-layer step"
    bsz, seq, d = x_prompt.shape
    dec = x_sample.shape[0]
    assert x_sample.shape[1] == 1
    c_shift = mu.shape[-1]
    c_a = w0.shape[-1]
    window, kv_w = cache_win_k.shape[2], cache_win_k.shape[3] * cache_win_k.shape[4]
    n_mem, mem_w = cache_mem_k.shape[2], cache_mem_k.shape[3] * cache_mem_k.shape[4]
    q_w = sinks.shape[-1] * HEAD
    qkvm_w = q_w + 2 * kv_w + mem_w
    alpha = (2.0 * depth) ** 0.25
    past_len = float(PAST_LEN)
    chunk = 64

    w_in_b = w_in[0].astype(BF16)
    w_parts = [w_in_b[:, :c_shift], w_in_b[:, c_shift:c_shift + qkvm_w], w_in_b[:, c_shift + qkvm_w:]]
    rp = dict(mu=mu[0], w0=w0[0], w_up=w_up[0], a0=a0[0], a_up=a_up[0], g_up=g_up[0], k_k=k_k[0], k_a=k_a[0],
              r_k=r_k[0].reshape(-1))
    n_route = N_GROUPS * (1 + EXPERTS_PER_GROUP)
    mw = dict(ln_in_g=ln_in_g.reshape(1, d), ln_in_b=ln_in_b.reshape(1, d), p_a=p_a[0].astype(BF16),
              p_b=p_b[0].astype(BF16), p_m=p_m[0].astype(BF16), w_o=w_o[0].astype(BF16),
              ln1_g=ln1_g[0].reshape(1, d), ln1_b=ln1_b[0].reshape(1, d),
              w_route=jnp.pad(jnp.concatenate([w_group[0], w_router[0]], axis=1), ((0, 0), (0, LANES - n_route))),
              b_route=jnp.pad(jnp.concatenate([b_group[0], b_router[0]]), (0, LANES - n_route)).reshape(1, LANES),
              e_gate=e_gate, e_up=e_up, e_down=e_down, ln2_g=ln2_g[0], ln2_b=ln2_b[0])

    xp = x_prompt.reshape(bsz * seq, d)
    zr, zq, zg = ln_proj(xp, ln_in_g, ln_in_b, w_parts, 256, (F32, F32, BF16))
    zr3 = zr.reshape(bsz, seq, c_shift)
    prep = rwkv_prep(zr3, jnp.zeros((bsz, 1, c_shift), F32), rp, chunk, 256)
    o_a, wkv_p = wkv(prep, jnp.zeros((bsz, c_a // HEAD, HEAD, HEAD), F32), lnx_g[0], lnx_b[0], chunk)
    mkv = proj(mem_prompt.reshape(bsz * n_mem, d), w_mem_kv[0].astype(BF16), 256).reshape(bsz, n_mem, 2 * mem_w)
    mk_p, mv_p = mkv[..., :mem_w], mkv[..., mem_w:]
    tables = rope_tables(jnp.arange(seq, dtype=F32))
    o_b, o_m, k_rot = attn_prompt(zq.reshape(bsz, seq, qkvm_w), mk_p, mv_p, sinks[0], tables, window, q_w, kv_w)
    n_all = bsz * seq + dec
    n_buf = -(-n_all // 256) * 256
    routed = merge(xp, o_a.reshape(-1, c_a), o_b.reshape(-1, q_w), o_m.reshape(-1, mem_w), zg, mw, 256, alpha, n_buf)
    shift_p = zr3[:, -1]
    kb_p = k_rot[:, -window:].reshape(bsz, window, H_KV, HEAD)
    vb_p = zq.reshape(bsz, seq, qkvm_w)[:, -window:, q_w + kv_w:q_w + 2 * kv_w].reshape(bsz, window, H_KV, HEAD)

    xs = x_sample.reshape(dec, d)
    zr_s, zq_s, zg_s = ln_proj(xs, ln_in_g, ln_in_b, w_parts, dec, (F32, F32, F32))
    ops_s = rwkv_prep(zr_s.reshape(1, dec, c_shift), state_shift[0].reshape(1, dec, c_shift), rp, 1, dec)
    o_a_s, wkv_s = wkv_step([a.reshape(dec, c_a) for a in ops_s], state_wkv[0], lnx_g[0], lnx_b[0], 8)
    tables_s = rope_tables(jnp.full((1,), past_len, F32))
    o_b_s, o_m_s, nk_s, nv_s = attn_step(
        zq_s, cache_win_k[0].reshape(dec, window, kv_w), cache_win_v[0].reshape(dec, window, kv_w),
        cache_mem_k[0].reshape(dec, n_mem, mem_w), cache_mem_v[0].reshape(dec, n_mem, mem_w),
        sinks[0], tables_s, q_w, kv_w, 8)
    x1, x1_tiles, eidx, gate = merge(xs, o_a_s, o_b_s, o_m_s, zg_s, mw, dec, alpha, n_buf, bsz * seq, routed)

    y_prompt, y_sample = hier_moe_ln(x1, x1_tiles, eidx, gate, mw, dec, alpha, n_all, bsz * seq)
    y_prompt = y_prompt.reshape(bsz, seq, d)
    y_sample = y_sample.reshape(dec, 1, d)

    sd = state_wkv.dtype
    return (y_prompt, y_sample, wkv_p[None].astype(sd), wkv_s[None].astype(sd), shift_p[None], zr_s[None],
            kb_p[None], vb_p[None], nk_s.reshape(dec, window, H_KV, HEAD)[None],
            nv_s.reshape(dec, window, H_KV, HEAD)[None],
            mk_p.reshape(bsz, n_mem, -1, HEAD)[None], mv_p.reshape(bsz, n_mem, -1, HEAD)[None])
```

```python
import functools
import math

import jax
import jax.numpy as jnp
from jax import lax
from jax.experimental import pallas as pl
from jax.experimental.pallas import tpu as pltpu

F32 = jnp.float32
BF16 = jnp.bfloat16
SCAN_BATCH = 4
CHUNKS_PER_STEP = 4

HEAD = 64
LANES = 128
H_KV = 2
ROT_HALF = 8
ROPE_THETA = 500000.0
PAST_LEN = 8192
N_GROUPS = 4
EXPERTS_PER_GROUP = 8
EXPERT_BLOCK = 128
LN_EPS = 1e-5
LNX_EPS = 64e-5
NEG_INF = -1e30
VMEM_LIMIT = 48 * 1024 * 1024


def _pieces(x, n):
    out = []
    for _ in range(n):
        p = x.astype(BF16)
        out.append(p)
        x = x - p.astype(F32)
    return out


def _mask_dot(x, mask, n=2):
    return sum(jnp.dot(p, mask, preferred_element_type=F32) for p in _pieces(x, n))


def _split3(x, axis, lhs):
    hi = x.astype(BF16).astype(F32)
    lo = x - hi
    return jnp.concatenate([hi, hi, lo] if lhs else [hi, lo, hi], axis=axis).astype(BF16)


def _dot3(a, b):
    return jnp.dot(_split3(a, 1, True), _split3(b, 0, False), preferred_element_type=F32)


def _dot3_t(a, b):
    return lax.dot_general(_split3(a, 1, True), _split3(b, 1, False), (((1,), (1,)), ((), ())),
                           preferred_element_type=F32)


def _bdot(a, b):
    return jnp.dot(a.astype(BF16), b.astype(BF16), preferred_element_type=F32)


def _bdot_t(a, b):
    return lax.dot_general(a.astype(BF16), b.astype(BF16), (((1,), (1,)), ((), ())), preferred_element_type=F32)


def _ln(x, g, b):
    mu = jnp.mean(x, axis=-1, keepdims=True)
    xc = x - mu
    var = jnp.mean(xc * xc, axis=-1, keepdims=True)
    return xc * lax.rsqrt(var + LN_EPS) * g + b


def _const_spec(shape):
    nd = len(shape)
    return pl.BlockSpec(shape, lambda *_: (0,) * nd)


def _params(sem):
    return pltpu.CompilerParams(dimension_semantics=sem, vmem_limit_bytes=VMEM_LIMIT)


def _ln_proj_kernel(x_ref, g_ref, b_ref, *refs):
    nw = len(refs) // 2
    xn = _ln(x_ref[...], g_ref[...], b_ref[...]).astype(BF16)
    for w_ref, o_ref in zip(refs[:nw], refs[nw:]):
        o_ref[...] = jnp.dot(xn, w_ref[...], preferred_element_type=F32).astype(o_ref.dtype)


def ln_proj(x, g, b, ws, tm, out_dtypes):
    n, d = x.shape
    return pl.pallas_call(
        _ln_proj_kernel, name="ln_proj",
        grid=(n // tm,),
        in_specs=[pl.BlockSpec((tm, d), lambda i: (i, 0)), _const_spec((1, d)), _const_spec((1, d))]
        + [_const_spec(w.shape) for w in ws],
        out_specs=[pl.BlockSpec((tm, w.shape[1]), lambda i: (i, 0)) for w in ws],
        out_shape=[jax.ShapeDtypeStruct((n, w.shape[1]), dt) for w, dt in zip(ws, out_dtypes)],
        compiler_params=_params(("parallel",)),
    )(x, g.reshape(1, d), b.reshape(1, d), *ws)


def _rwkv_prep_kernel(chunk, z_ref, prev_ref, mu_ref, w0_ref, wup_ref, a0_ref, aup_ref, gup_ref,
                      kk_ref, ka_ref, rk_ref, hsum_ref, tril_ref, *refs):
    out_refs, carry_ref = refs[:-1], refs[-1]
    z = z_ref[0]
    tt = z.shape[0]
    c_a = w0_ref.shape[-1]
    r_w, r_a, r_g = wup_ref.shape[0], aup_ref.shape[0], gup_ref.shape[0]
    if chunk == 1:
        prev = prev_ref[0]
    else:
        @pl.when(pl.program_id(1) == 0)
        def _():
            carry_ref[...] = prev_ref[0]

        row = lax.broadcasted_iota(jnp.int32, z.shape, 0)
        prev = jnp.where(row == 0, carry_ref[...], pltpu.roll(z, 1, 0))
        carry_ref[...] = z[tt - 1:tt, :]
    zs = z + (prev - z) * mu_ref[...]
    r = zs[:, :c_a]
    k = zs[:, c_a:2 * c_a]
    v = zs[:, 2 * c_a:3 * c_a]
    o = 3 * c_a
    xw = zs[:, o:o + r_w]
    xa = zs[:, o + r_w:o + r_w + r_a]
    xg = zs[:, o + r_w + r_a:o + r_w + r_a + r_g]
    warg = -(w0_ref[...] + _dot3(jnp.tanh(xw), wup_ref[...]))
    softplus = jnp.maximum(warg, 0.0) + jnp.log1p(jnp.exp(-jnp.abs(warg)))
    lw = -jnp.exp(-softplus - 0.5)
    a = jax.nn.sigmoid(a0_ref[...] + _dot3(xa, aup_ref[...]))
    g = _dot3(jax.nn.sigmoid(xg), gup_ref[...])
    kkr = k * kk_ref[...]
    kk = kkr / jnp.maximum(jnp.sqrt(_mask_dot(kkr * kkr, hsum_ref[...])), 1e-12)
    k2 = k * (1.0 + (a - 1.0) * ka_ref[...])
    bonus = _mask_dot(r * k2 * rk_ref[...], hsum_ref[...]) * v
    kb = kk * a
    if chunk == 1:
        outs = (r, jnp.exp(lw), k2, v, -kk, kb, g, bonus)
        for ref, val in zip(out_refs, outs):
            ref[0] = val
        return
    pieces = _pieces(lw, 3)
    ones3 = jnp.ones((chunk, 3 * chunk), BF16)
    cw, cwl = [], []
    for c in range(tt // chunk):
        stack = jnp.concatenate([p[c * chunk:(c + 1) * chunk] for p in pieces], axis=0)
        cw.append(jnp.dot(tril_ref[...], stack, preferred_element_type=F32))
        cwl.append(jnp.dot(ones3, stack, preferred_element_type=F32))
    cw = jnp.concatenate(cw, axis=0)
    cwl = jnp.concatenate(cwl, axis=0)
    e_inv = jnp.exp(-cw)
    outs = (r * jnp.exp(cw), -kk * jnp.exp(cw - lw), kb * e_inv, k2 * e_inv, v, g, bonus)
    for ref, val in zip(out_refs[:-1], outs):
        ref[0] = val
    wl = jnp.exp(cwl)
    for c in range(tt // chunk):
        out_refs[-1][0, c] = wl[c * chunk:c * chunk + 1, :]


def _head_sum_matrix(width):
    idx = jnp.arange(width)
    return ((idx[:, None] // HEAD) == (idx[None, :] // HEAD)).astype(BF16)


def rwkv_prep(z, prev, p, chunk, tt):
    bsz, t, cs = z.shape
    c_a = p['w0'].shape[-1]
    assert prev.shape[1] == (t if chunk == 1 else 1)
    ridx = jnp.arange(chunk)
    tril = jnp.tile((ridx[None, :] <= ridx[:, None]).astype(BF16), (1, 3))
    hsum = _head_sum_matrix(c_a)
    row = lambda x: x.reshape(1, -1)
    tile = pl.BlockSpec((1, tt, c_a), lambda b, i: (b, i, 0))
    full = jax.ShapeDtypeStruct((bsz, t, c_a), F32)
    if chunk == 1:
        out_specs, out_shape = [tile] * 8, [full] * 8
        prev_spec = pl.BlockSpec((1, tt, cs), lambda b, i: (b, i, 0))
    else:
        out_specs = [tile] * 7 + [pl.BlockSpec((1, tt // chunk, 1, c_a), lambda b, i: (b, i, 0, 0))]
        out_shape = [full] * 7 + [jax.ShapeDtypeStruct((bsz, t // chunk, 1, c_a), F32)]
        prev_spec = pl.BlockSpec((1, 1, cs), lambda b, i: (b, 0, 0))
    consts = [row(p['mu']), row(p['w0']), p['w_up'], row(p['a0']), p['a_up'], p['g_up'], row(p['k_k']),
              row(p['k_a']), row(p['r_k']), hsum, tril]
    return pl.pallas_call(
        functools.partial(_rwkv_prep_kernel, chunk), name="rwkv_prep",
        grid=(bsz, t // tt),
        in_specs=[pl.BlockSpec((1, tt, cs), lambda b, i: (b, i, 0)), prev_spec] + [_const_spec(c.shape) for c in consts],
        out_specs=out_specs,
        out_shape=out_shape,
        scratch_shapes=[pltpu.VMEM((1, cs), F32)],
        compiler_params=_params(("parallel", "arbitrary")),
    )(z, prev, *consts)


def _wkv_chunk_kernel(ra_ref, at_ref, bt_ref, kt_ref, v_ref, wl_ref, rp_ref, y0_ref, m_ref, n_ref):
    n_chunks = m_ref.shape[1]
    length = ra_ref.shape[1] // n_chunks
    heads = ra_ref.shape[2] // HEAD
    row = lax.broadcasted_iota(jnp.int32, (length, length), 0)
    col = lax.broadcasted_iota(jnp.int32, (length, length), 1)
    strict = row > col
    incl = row >= col
    hrow = lax.broadcasted_iota(jnp.int32, (HEAD, HEAD), 0)
    hcol = lax.broadcasted_iota(jnp.int32, (HEAD, HEAD), 1)
    units = [(c, h) for c in range(n_chunks) for h in range(heads)]
    us = range(len(units))
    at = lambda ref, u: ref[0, units[u][0] * length:(units[u][0] + 1) * length,
                            units[u][1] * HEAD:(units[u][1] + 1) * HEAD]
    gram = [_bdot_t(jnp.concatenate([at(at_ref, u), at(ra_ref, u)], axis=0),
                    jnp.concatenate([at(bt_ref, u), at(kt_ref, u)], axis=0)) for u in us]
    a_ab = [jnp.where(strict, g[:length, :length], 0.0) for g in gram]
    a_kk = [jnp.concatenate([jnp.where(strict, g[:length, length:], 0.0),
                             jnp.where(incl, g[length:, length:], 0.0)], axis=0) for g in gram]
    a_rb = [jnp.where(incl, g[length:, :length], 0.0) for g in gram]
    inv = [jnp.where(row == col, 1.0, a) for a in a_ab]
    pw = [_bdot(a, a) for a in a_ab]
    avy = [_dot3(a_kk[u], at(v_ref, u)) for u in us]
    w_l = [wl_ref[0, units[u][0], :, units[u][1] * HEAD:(units[u][1] + 1) * HEAD] for u in us]
    nk = [_dot3(at(v_ref, u).T, at(kt_ref, u) * w_l[u]) for u in us]
    for _ in range(int(math.log2(length)) - 2):
        both = [_bdot(jnp.concatenate([pw[u], inv[u]], axis=0), pw[u]) for u in us]
        pw = [b[:length] for b in both]
        inv = [inv[u] + both[u][length:] for u in us]
    inv = [inv[u] + _bdot(inv[u], pw[u]) for u in us]
    pq = [_dot3(inv[u], jnp.concatenate([at(at_ref, u), avy[u][:length]], axis=1)) for u in us]
    ry = [_dot3(a_rb[u], pq[u]) for u in us]
    mn = [_dot3(pq[u].T, at(bt_ref, u) * w_l[u]) for u in us]
    lanes = lambda f, c: jnp.concatenate([f(c * heads + h) for h in range(heads)], axis=1)
    rows = lambda f: jnp.concatenate([lanes(f, c) for c in range(n_chunks)], axis=0)
    rp_ref[0] = rows(lambda u: at(ra_ref, u) + ry[u][:, :HEAD])
    y0_ref[0] = rows(lambda u: avy[u][length:] + ry[u][:, HEAD:])
    for c in range(n_chunks):
        m_ref[0, c] = lanes(lambda u: jnp.where(hrow == hcol, w_l[u], 0.0) + mn[u][:HEAD], c)
        n_ref[0, c] = lanes(lambda u: nk[u] + mn[u][HEAD:], c)


def _head_norm_wide(y, hsum):
    yc = y - _mask_dot(y, hsum) * (1.0 / HEAD)
    return yc * lax.rsqrt(_mask_dot(yc * yc, hsum) * (1.0 / HEAD) + LNX_EPS)


def _wkv_scan_kernel(rp_ref, y0_ref, m_ref, n_ref, g_ref, bonus_ref, s0_ref, lg_ref, lb_ref, hsum_ref,
                     o_ref, sout_ref, s_ref):
    c = pl.program_id(1)
    nb = rp_ref.shape[0]
    heads = rp_ref.shape[2] // HEAD

    @pl.when(c == 0)
    def _():
        s_ref[...] = s0_ref[...]

    pairs = [(b, h, slice(h * HEAD, (h + 1) * HEAD)) for b in range(nb) for h in range(heads)]
    s = [s_ref[b, h] for b, h, _ in pairs]
    s_new = [n_ref[b, 0, :, sl] + _dot3(s[j], m_ref[b, 0, :, sl]) for j, (b, h, sl) in enumerate(pairs)]
    ys = [y0_ref[b, :, sl] + _dot3_t(rp_ref[b, :, sl], s[j]) for j, (b, h, sl) in enumerate(pairs)]
    for j, (b, h, _) in enumerate(pairs):
        s_ref[b, h] = s_new[j]
    length = rp_ref.shape[1]
    y = _head_norm_wide(jnp.concatenate([jnp.concatenate(ys[b * heads:(b + 1) * heads], axis=-1) for b in range(nb)],
                                        axis=0), hsum_ref[...])
    for b in range(nb):
        o = (y[b * length:(b + 1) * length] * lg_ref[...] + lb_ref[...] + bonus_ref[b]) * g_ref[b]
        o_ref[b] = o.astype(o_ref.dtype)

    @pl.when(c == pl.num_programs(1) - 1)
    def _():
        sout_ref[...] = s_ref[...]


def wkv(prep, s0, lnx_g, lnx_b, chunk):
    ra, at, bt, kt, v, g, bonus, wl = prep
    bsz, t, c_a = ra.shape
    heads = c_a // HEAD
    n_chunks = t // chunk
    assert chunk & (chunk - 1) == 0 and chunk >= 4 and t % chunk == 0
    per_step = CHUNKS_PER_STEP if n_chunks % CHUNKS_PER_STEP == 0 else 1
    tile = pl.BlockSpec((1, per_step * chunk, c_a), lambda b, c: (b, c, 0))
    mat = pl.BlockSpec((1, per_step, HEAD, c_a), lambda b, c: (b, c, 0, 0))
    full = jax.ShapeDtypeStruct((bsz, t, c_a), F32)
    mats = jax.ShapeDtypeStruct((bsz, n_chunks, HEAD, c_a), F32)
    rp, y0, m, n = pl.pallas_call(
        _wkv_chunk_kernel, name="wkv_chunk",
        grid=(bsz, n_chunks // per_step),
        in_specs=[tile] * 5 + [pl.BlockSpec((1, per_step, 1, c_a), lambda b, c: (b, c, 0, 0))],
        out_specs=[tile, tile, mat, mat],
        out_shape=[full, full, mats, mats],
        compiler_params=_params(("parallel", "parallel")),
    )(ra, at, bt, kt, v, wl)
    nb = SCAN_BATCH if bsz % SCAN_BATCH == 0 else 1
    tile = pl.BlockSpec((nb, chunk, c_a), lambda b, c: (b, c, 0))
    mat = pl.BlockSpec((nb, 1, HEAD, c_a), lambda b, c: (b, c, 0, 0))
    st = pl.BlockSpec((nb, heads, HEAD, HEAD), lambda b, c: (b, 0, 0, 0))
    return pl.pallas_call(
        _wkv_scan_kernel, name="wkv_scan",
        grid=(bsz // nb, n_chunks),
        in_specs=[tile, tile, mat, mat, tile, tile, st, _const_spec((1, c_a)), _const_spec((1, c_a)),
                  _const_spec((c_a, c_a))],
        out_specs=[tile, st],
        out_shape=[jax.ShapeDtypeStruct((bsz, t, c_a), BF16), jax.ShapeDtypeStruct(s0.shape, F32)],
        scratch_shapes=[pltpu.VMEM((nb, heads, HEAD, HEAD), F32)],
        compiler_params=_params(("parallel", "arbitrary")),
    )(rp, y0, m, n, g, bonus, s0, lnx_g.reshape(1, c_a), lnx_b.reshape(1, c_a), _head_sum_matrix(c_a))


def _wkv_step_kernel(r_ref, w_ref, k_ref, v_ref, a_ref, b_ref, g_ref, bonus_ref, s_ref, lg_ref, lb_ref, hsum_ref,
                     o_ref, sout_ref):
    bt, heads = s_ref.shape[0], s_ref.shape[1]
    c_a = heads * HEAD
    hsum = hsum_ref[...]
    diag = (lax.broadcasted_iota(jnp.int32, (HEAD, c_a), 1) % HEAD
            == lax.broadcasted_iota(jnp.int32, (HEAD, c_a), 0))
    seqs = range(bt)
    row = lambda ref, i: ref[i:i + 1, :]
    stack = lambda f: jnp.concatenate([f(i) for i in seqs], axis=0)
    piece = lambda x, i: x[i * HEAD:(i + 1) * HEAD]
    s = stack(lambda i: jnp.concatenate([s_ref[i, h] for h in range(heads)], axis=1))
    sa = _mask_dot(s * stack(lambda i: jnp.broadcast_to(row(a_ref, i), (HEAD, c_a))), hsum)
    v_rows = _mask_dot(stack(lambda i: jnp.where(diag, row(v_ref, i), 0.0)), hsum)
    s = stack(lambda i: piece(s, i) * row(w_ref, i) + piece(sa, i) * row(b_ref, i) + piece(v_rows, i) * row(k_ref, i))
    for i in seqs:
        for h in range(heads):
            sout_ref[i, h] = piece(s, i)[:, h * HEAD:(h + 1) * HEAD]
    y_rows = _mask_dot(stack(lambda i: piece(s, i) * row(r_ref, i)), hsum)
    y = stack(lambda i: jnp.sum(jnp.where(diag, piece(y_rows, i), 0.0), axis=0, keepdims=True))
    y = _head_norm_wide(y, hsum)
    o_ref[...] = (y * lg_ref[...] + lb_ref[...] + bonus_ref[...]) * g_ref[...]


def wkv_step(ops, s0, lnx_g, lnx_b, bt):
    bsz, c_a = ops[0].shape
    rows = pl.BlockSpec((bt, c_a), lambda i: (i, 0))
    st = pl.BlockSpec((bt,) + s0.shape[1:], lambda i: (i, 0, 0, 0))
    hsum = _head_sum_matrix(c_a)
    return pl.pallas_call(
        _wkv_step_kernel, name="wkv_step",
        grid=(bsz // bt,),
        in_specs=[rows] * 8 + [st, _const_spec((1, c_a)), _const_spec((1, c_a)), _const_spec(hsum.shape)],
        out_specs=[rows, st],
        out_shape=[jax.ShapeDtypeStruct((bsz, c_a), F32), jax.ShapeDtypeStruct(s0.shape, F32)],
        compiler_params=_params(("parallel",)),
    )(*ops, s0, lnx_g.reshape(1, c_a), lnx_b.reshape(1, c_a), hsum)


def rope_tables(pos):
    inv_freq = ROPE_THETA ** (-jnp.arange(ROT_HALF, dtype=F32) / ROT_HALF)
    ang = pos[:, None] * inv_freq[None, :]
    cos, sin = jnp.cos(ang), jnp.sin(ang)
    t = pos.shape[0]
    rest = HEAD - 2 * ROT_HALF
    c = jnp.concatenate([cos, cos, jnp.ones((t, rest), F32)], axis=1)
    s1 = jnp.concatenate([jnp.zeros((t, ROT_HALF), F32), sin, jnp.zeros((t, rest), F32)], axis=1)
    s2 = jnp.concatenate([-sin, jnp.zeros((t, HEAD - ROT_HALF), F32)], axis=1)
    rep = LANES // HEAD
    return tuple(jnp.tile(x, (1, rep)) for x in (c, s1, s2))


def _rope(x, c, s1, s2):
    width = x.shape[-1]
    rep = width // c.shape[-1]
    if rep > 1:
        c, s1, s2 = (jnp.concatenate([t] * rep, axis=1) for t in (c, s1, s2))
    return x * c + pltpu.roll(x, ROT_HALF, 1) * s1 + pltpu.roll(x, width - ROT_HALF, 1) * s2


def _attn_kernel(window, q_w, kv_w, zq_ref, kp_ref, vp_ref, rc_ref, rs1_ref, rs2_ref, pc_ref, ps1_ref, ps2_ref,
                 mk_ref, mv_ref, sink_ref, ob_ref, om_ref, kr_ref):
    n = pl.program_id(1)
    zq = zq_ref[0]
    scale = HEAD ** -0.5
    q = zq[:, :q_w]
    k = zq[:, q_w:q_w + kv_w]
    v = zq[:, q_w + kv_w:q_w + 2 * kv_w]
    qm = zq[:, q_w + 2 * kv_w:]
    qr = _rope(q, rc_ref[...], rs1_ref[...], rs2_ref[...]) * scale
    kr = _rope(k, rc_ref[...], rs1_ref[...], rs2_ref[...])
    kr_ref[0] = kr
    kpr = _rope(kp_ref[0], pc_ref[...], ps1_ref[...], ps2_ref[...])
    vp = vp_ref[0]
    gqa = q_w // kv_w
    assert window & (window - 1) == 0
    qi = lax.broadcasted_iota(jnp.int32, (gqa * window, 2 * window), 0) & (window - 1)
    kj = lax.broadcasted_iota(jnp.int32, (gqa * window, 2 * window), 1)
    valid = (kj > qi) & (kj <= qi + window) & ((kj >= window) | (n > 0))
    tdot = lambda a, b: lax.dot_general(a, b, (((1,), (1,)), ((), ())), preferred_element_type=F32)
    dot = lambda a, b: jnp.dot(a, b, preferred_element_type=F32)
    kvs = range(kv_w // HEAD)
    mhs = range(qm.shape[1] // HEAD)
    hsl = lambda h: slice(h * HEAD, (h + 1) * HEAD)
    qmb = (qm * scale).astype(BF16)
    mk = mk_ref[0].astype(BF16)
    mv = mv_ref[0].astype(BF16)
    kcat = [jnp.concatenate([kpr[:, hsl(hk)], kr[:, hsl(hk)]], axis=0).astype(BF16) for hk in kvs]
    vcat = [jnp.concatenate([vp[:, hsl(hk)], v[:, hsl(hk)]], axis=0).astype(BF16) for hk in kvs]
    qs = [jnp.concatenate([qr[:, hsl(hk * gqa + g)] for g in range(gqa)], axis=0).astype(BF16) for hk in kvs]
    s = [jnp.where(valid, tdot(qs[hk], kcat[hk]), NEG_INF) for hk in kvs]
    sm = [tdot(qmb[:, hsl(h)], mk[:, hsl(h)]) for h in mhs]
    sink = [jnp.concatenate([jnp.full((window, 1), sink_ref[hk * gqa + g], F32) for g in range(gqa)], axis=0)
            for hk in kvs]
    m = [jnp.maximum(jnp.max(s[hk], axis=-1, keepdims=True), sink[hk]) for hk in kvs]
    p = [jnp.exp(s[hk] - m[hk]) for hk in kvs]
    pm = [jnp.exp(sm[h] - jnp.max(sm[h], axis=-1, keepdims=True)) for h in mhs]
    o = [dot(p[hk].astype(BF16), vcat[hk])
         / (jnp.sum(p[hk], axis=-1, keepdims=True) + jnp.exp(sink[hk] - m[hk])) for hk in kvs]
    om = [dot(pm[h].astype(BF16), mv[:, hsl(h)]) / jnp.sum(pm[h], axis=-1, keepdims=True) for h in mhs]
    ob = jnp.concatenate([o[hk][g * window:(g + 1) * window] for hk in kvs for g in range(gqa)], axis=1)
    ob_ref[0] = ob.astype(ob_ref.dtype)
    om_ref[0] = jnp.concatenate(om, axis=1).astype(om_ref.dtype)


def attn_prompt(zq, mk, mv, sinks, tables, window, q_w, kv_w):
    bsz, t, zw = zq.shape
    mem_w = zw - q_w - 2 * kv_w
    assert kv_w == LANES and q_w % kv_w == 0
    kcol, vcol = q_w // kv_w, q_w // kv_w + 1
    prev = lambda n: jnp.maximum(n - 1, 0)
    tab = pl.BlockSpec((window, LANES), lambda b, n: (n, 0))
    ptab = pl.BlockSpec((window, LANES), lambda b, n: (prev(n), 0))
    mem = pl.BlockSpec((1,) + mk.shape[1:], lambda b, n: (b, 0, 0))
    return pl.pallas_call(
        functools.partial(_attn_kernel, window, q_w, kv_w), name="attn",
        grid=(bsz, t // window),
        in_specs=[pl.BlockSpec((1, window, zw), lambda b, n: (b, n, 0)),
                  pl.BlockSpec((1, window, kv_w), lambda b, n: (b, prev(n), kcol)),
                  pl.BlockSpec((1, window, kv_w), lambda b, n: (b, prev(n), vcol)),
                  tab, tab, tab, ptab, ptab, ptab, mem, mem,
                  pl.BlockSpec(memory_space=pltpu.SMEM)],
        out_specs=[pl.BlockSpec((1, window, q_w), lambda b, n: (b, n, 0)),
                   pl.BlockSpec((1, window, mem_w), lambda b, n: (b, n, 0)),
                   pl.BlockSpec((1, window, kv_w), lambda b, n: (b, n, 0))],
        out_shape=[jax.ShapeDtypeStruct((bsz, t, q_w), BF16), jax.ShapeDtypeStruct((bsz, t, mem_w), BF16),
                   jax.ShapeDtypeStruct((bsz, t, kv_w), F32)],
        compiler_params=_params(("parallel", "parallel")),
    )(zq, zq, zq, *tables, *tables, mk, mv, sinks)


def _attn_step_kernel(q_w, kv_w, zq_ref, ck_ref, cv_ref, mk_ref, mv_ref, rc_ref, rs1_ref, rs2_ref, sink_ref,
                      ob_ref, om_ref, nk_ref, nv_ref):
    bt = zq_ref.shape[0]
    window = ck_ref.shape[1]
    mem_w = om_ref.shape[1]
    n_q, gqa, per_vreg = q_w // HEAD, q_w // kv_w, LANES // HEAD
    scale = HEAD ** -0.5
    zq = zq_ref[...]
    q = _rope(zq[:, :q_w], rc_ref[...], rs1_ref[...], rs2_ref[...]) * scale
    k_new = _rope(zq[:, q_w:q_w + kv_w], rc_ref[...], rs1_ref[...], rs2_ref[...])
    v_new = zq[:, q_w + kv_w:q_w + 2 * kv_w]
    qm = zq[:, q_w + 2 * kv_w:] * scale
    own = lambda w: (lax.broadcasted_iota(jnp.int32, (n_q, w), 1) // HEAD
                     == lax.broadcasted_iota(jnp.int32, (n_q, w), 0))
    own_q, own_m = own(q_w), own(mem_w)
    hrow = lax.broadcasted_iota(jnp.int32, (n_q, LANES), 0)
    hblk = lax.broadcasted_iota(jnp.int32, (n_q, LANES), 1) // HEAD
    swap = (hrow % per_vreg) != (hrow // gqa)
    keep = hblk == hrow % per_vreg
    key_ok = lax.broadcasted_iota(jnp.int32, (n_q, window), 1) >= 1
    wrow = lax.broadcasted_iota(jnp.int32, (window, kv_w), 0)
    sink = sink_ref[...]
    bs = range(bt)
    tdot = lambda a, b: lax.dot_general(a.astype(BF16), b.astype(BF16), (((1,), (1,)), ((), ())),
                                        preferred_element_type=F32)
    for b in bs:
        nk_ref[b] = jnp.where(wrow == window - 1, k_new[b:b + 1], pltpu.roll(ck_ref[b], window - 1, 0))
        nv_ref[b] = jnp.where(wrow == window - 1, v_new[b:b + 1], pltpu.roll(cv_ref[b], window - 1, 0))
    q8 = []
    for b in bs:
        rep = jnp.where(own_q, q[b:b + 1], 0.0)
        fold = sum(rep[:, c * LANES:(c + 1) * LANES] for c in range(q_w // LANES))
        q8.append(jnp.where(swap, pltpu.roll(fold, HEAD, 1), fold))
    qm8 = [jnp.where(own_m, qm[b:b + 1], 0.0) for b in bs]
    s = [jnp.where(key_ok, tdot(q8[b], ck_ref[b]), NEG_INF) for b in bs]
    sm = [tdot(qm8[b], mk_ref[b]) for b in bs]
    s_new = [jnp.sum(q8[b] * k_new[b:b + 1], axis=1, keepdims=True) for b in bs]
    m = [jnp.maximum(jnp.maximum(jnp.max(s[b], axis=1, keepdims=True), s_new[b]), sink) for b in bs]
    p = [jnp.exp(s[b] - m[b]) for b in bs]
    p_new = [jnp.exp(s_new[b] - m[b]) for b in bs]
    pm = [jnp.exp(sm[b] - jnp.max(sm[b], axis=1, keepdims=True)) for b in bs]
    o8 = [(_bdot(p[b], cv_ref[b]) + p_new[b] * v_new[b:b + 1])
          / (jnp.sum(p[b], axis=1, keepdims=True) + p_new[b] + jnp.exp(sink - m[b])) for b in bs]
    om8 = [_bdot(pm[b], mv_ref[b]) / jnp.sum(pm[b], axis=1, keepdims=True) for b in bs]
    ob, om = [], []
    for b in bs:
        o = jnp.where(swap, pltpu.roll(o8[b], HEAD, 1), o8[b])
        o = jnp.concatenate([jnp.where(keep, o, 0.0)] * (q_w // LANES), axis=1)
        ob.append(jnp.sum(jnp.where(own_q, o, 0.0), axis=0, keepdims=True))
        om.append(jnp.sum(jnp.where(own_m, om8[b], 0.0), axis=0, keepdims=True))
    ob_ref[...] = jnp.concatenate(ob, axis=0)
    om_ref[...] = jnp.concatenate(om, axis=0)


def attn_step(zq, ck, cv, mk, mv, sinks, tables, q_w, kv_w, bt):
    bsz, zw = zq.shape
    mem_w = zw - q_w - 2 * kv_w
    window, n_mem = ck.shape[1], mk.shape[1]
    n_q = q_w // HEAD
    assert kv_w == LANES and mem_w // HEAD <= n_q and n_q == 8
    rows = lambda w: pl.BlockSpec((bt, w), lambda i: (i, 0))
    cache = pl.BlockSpec((bt, window, kv_w), lambda i: (i, 0, 0))
    mem = pl.BlockSpec((bt, n_mem, mem_w), lambda i: (i, 0, 0))
    consts = list(tables) + [sinks.reshape(n_q, 1)]
    const_specs = [_const_spec(c.shape) for c in consts]
    return pl.pallas_call(
        functools.partial(_attn_step_kernel, q_w, kv_w), name="attn_step",
        grid=(bsz // bt,),
        in_specs=[rows(zw), cache, cache, mem, mem] + const_specs,
        out_specs=[rows(q_w), rows(mem_w), cache, cache],
        out_shape=[jax.ShapeDtypeStruct((bsz, q_w), F32), jax.ShapeDtypeStruct((bsz, mem_w), F32),
                   jax.ShapeDtypeStruct(ck.shape, F32), jax.ShapeDtypeStruct(cv.shape, F32)],
        compiler_params=_params(("parallel",)),
    )(zq, ck, cv, mk, mv, *consts)


def _proj_kernel(x_ref, w_ref, o_ref):
    o_ref[...] = jnp.dot(x_ref[...].astype(BF16), w_ref[...], preferred_element_type=F32)


def proj(x, w, tm):
    n, d = x.shape
    return pl.pallas_call(
        _proj_kernel, name="proj",
        grid=(n // tm,),
        in_specs=[pl.BlockSpec((tm, d), lambda i: (i, 0)), _const_spec(w.shape)],
        out_specs=pl.BlockSpec((tm, w.shape[1]), lambda i: (i, 0)),
        out_shape=jax.ShapeDtypeStruct((n, w.shape[1]), F32),
        compiler_params=_params(("parallel",)),
    )(x, w)


def _merge_kernel(alpha, n_tiles, x_ref, oa_ref, ob_ref, om_ref, gt_ref, lig_ref, lib_ref, pa_ref, pb_ref, pm_ref,
                  wo_ref, l1g_ref, l1b_ref, wr_ref, br_ref, *refs):
    outs = refs[-4:]

    @pl.when(pl.program_id(0) >= n_tiles)
    def _():
        for ref in outs:
            ref[...] = jnp.zeros_like(ref)

    @pl.when(pl.program_id(0) < n_tiles)
    def _():
        _merge_tile(alpha, x_ref, oa_ref, ob_ref, om_ref, gt_ref, lig_ref, lib_ref, pa_ref, pb_ref, pm_ref, wo_ref,
                    l1g_ref, l1b_ref, wr_ref, br_ref, *outs)


def _merge_tile(alpha, x_ref, oa_ref, ob_ref, om_ref, gt_ref, lig_ref, lib_ref, pa_ref, pb_ref, pm_ref, wo_ref,
                l1g_ref, l1b_ref, wr_ref, br_ref, x1_ref, x1t_ref, eidx_ref, gate_ref):
    d = x_ref.shape[1]
    xn = _ln(x_ref[...], lig_ref[...], lib_ref[...])
    gts = jax.nn.sigmoid(gt_ref[...].astype(F32))
    merged = (gts[:, :d] * _bdot(oa_ref[...], pa_ref[...]) + gts[:, d:2 * d] * _bdot(ob_ref[...], pb_ref[...])
              + gts[:, 2 * d:] * _bdot(om_ref[...], pm_ref[...]))
    x1 = _ln(alpha * xn + _bdot(merged, wo_ref[...]), l1g_ref[...], l1b_ref[...])
    x1_ref[...] = x1
    _rows_to_tiles(x1t_ref, x1)
    x_hi, x_lo = _pieces(x1, 2)
    w_hi, w_lo = _pieces(wr_ref[...], 2)
    dot = lambda a, b: jnp.dot(a, b, preferred_element_type=F32)
    logits = dot(x_hi, w_hi) + dot(x_hi, w_lo) + dot(x_lo, w_hi) + br_ref[...]
    lane = lax.broadcasted_iota(jnp.int32, logits.shape, 1)
    lane_f = lane.astype(F32)
    first = lambda hit: jnp.min(jnp.where(hit, lane_f, float(LANES)), axis=-1, keepdims=True).astype(jnp.int32)
    gmask = lane < N_GROUPS
    gl = jnp.where(gmask, logits, NEG_INF)
    gmax = jnp.max(gl, axis=-1, keepdims=True)
    gidx = first(gl == gmax)
    g_w = 1.0 / jnp.sum(jnp.where(gmask, jnp.exp(gl - gmax), 0.0), axis=-1, keepdims=True)
    lo = N_GROUPS + gidx * EXPERTS_PER_GROUP
    el = jnp.where((lane >= lo) & (lane < lo + EXPERTS_PER_GROUP), logits, NEG_INF)
    v1 = jnp.max(el, axis=-1, keepdims=True)
    i1 = first(el == v1)
    el2 = jnp.where(lane == i1, NEG_INF, el)
    v2 = jnp.max(el2, axis=-1, keepdims=True)
    i2 = first(el2 == v2)
    e2 = jnp.exp(v2 - v1)
    gate1 = g_w / (1.0 + e2)
    eidx_ref[...] = jnp.where(lane == 0, i1 - N_GROUPS, jnp.where(lane == 1, i2 - N_GROUPS, 0))
    gate_ref[...] = jnp.where(lane == 0, gate1, jnp.where(lane == 1, gate1 * e2, 0.0))


def merge(x, oa, ob, om, gt, w, tm, alpha, n_total, row_offset=0, into=None):
    n, d = x.shape
    assert row_offset % tm == 0 and n % tm == 0 and n_total % tm == 0
    off = row_offset // tm
    into = list(into or [])
    n_tiles = n // tm
    steps = n_tiles if into else n_total // tm
    rows = lambda a: pl.BlockSpec((tm, a.shape[1]), lambda i: (jnp.minimum(i, n_tiles - 1), 0))
    out = lambda width: pl.BlockSpec((tm, width), lambda i: (i + off, 0))
    consts = [w['ln_in_g'], w['ln_in_b'], w['p_a'], w['p_b'], w['p_m'], w['w_o'], w['ln1_g'], w['ln1_b'],
              w['w_route'], w['b_route']]
    n_in = 5 + len(consts)
    return pl.pallas_call(
        functools.partial(_merge_kernel, alpha, n_tiles), name="merge",
        grid=(steps,),
        in_specs=[rows(a) for a in (x, oa, ob, om, gt)] + [_const_spec(c.shape) for c in consts]
        + [pl.BlockSpec(memory_space=pl.ANY)] * len(into),
        out_specs=[out(d), pl.BlockSpec((tm * TILE_ROWS, LANES), lambda i: (i + off, 0)), out(LANES), out(LANES)],
        out_shape=[jax.ShapeDtypeStruct((n_total, d), F32), jax.ShapeDtypeStruct((n_total * TILE_ROWS, LANES), F32),
                   jax.ShapeDtypeStruct((n_total, LANES), jnp.int32), jax.ShapeDtypeStruct((n_total, LANES), F32)],
        input_output_aliases={n_in + k: k for k in range(len(into))},
        compiler_params=_params(("parallel",)),
    )(x, oa, ob, om, gt, *consts, *into)


ROW_DMA_UNROLL = 8
DRAIN_STEPS = 2


TILE_ROWS = 8


def _rows_from_tiles(ref, n):
    return jnp.concatenate([ref[pl.ds(s, n, stride=TILE_ROWS), :] for s in range(TILE_ROWS)], axis=1)


def _rows_to_tiles(ref, x):
    for s in range(TILE_ROWS):
        ref[pl.ds(s, x.shape[0], stride=TILE_ROWS), :] = x[:, s * LANES:(s + 1) * LANES]


def _row_copies(asg_ref, base, count, n_asg, x_hbm, buf, y_hbm, sem, gather, unrolled):
    def tile(ref, idx):
        start = idx * TILE_ROWS
        return ref.at[pl.ds(start if isinstance(idx, int) else pl.multiple_of(start, TILE_ROWS), TILE_ROWS)]

    def one(r, priority):
        a = asg_ref[base + r]
        if gather:
            tok = jnp.minimum(a, n_asg - 1)
            tok = jnp.where(tok >= n_asg // 2, tok - n_asg // 2, tok)
            copy = pltpu.make_async_copy(tile(x_hbm, tok), tile(buf, r), sem)
        else:
            copy = pltpu.make_async_copy(tile(buf, r), tile(y_hbm, a), sem)
        copy.start(priority=priority)

    if unrolled:
        for r in range(count):
            one(r, r % 2)
        return

    def body(g, carry):
        for j in range(ROW_DMA_UNROLL):
            one(g * ROW_DMA_UNROLL + j, j % 2)
        return carry
    lax.fori_loop(0, count // ROW_DMA_UNROLL, body, 0)


def _moe_expert_kernel(n_asg, asg_ref, be_ref, nu_ref, x_hbm, wg_ref, wu_ref, wd_ref, y_hbm,
                       xbuf, ybuf, wgb, wub, wdb, gsem, ssem):
    i = pl.program_id(0)
    used = nu_ref[0]
    rows = xbuf.shape[1] // TILE_ROWS
    n_blocks = be_ref.shape[0]
    blk = jnp.minimum(i, n_blocks - 1)

    @pl.when((i == 0) | (be_ref[blk] != be_ref[jnp.maximum(blk - 1, 0)]))
    def _():
        wgb[...] = wg_ref[0, 0].astype(BF16)
        wub[...] = wu_ref[0, 0].astype(BF16)
        wdb[...] = wd_ref[0, 0].astype(BF16)

    def wait_gather(slot):
        pltpu.make_async_copy(x_hbm.at[pl.ds(0, rows * TILE_ROWS)], xbuf.at[slot], gsem.at[slot]).wait()

    def wait_scatter(slot):
        pltpu.make_async_copy(ybuf.at[slot], y_hbm.at[pl.ds(0, rows * TILE_ROWS)], ssem.at[slot]).wait()

    def gather(b, slot, unrolled):
        _row_copies(asg_ref, b * rows, rows, n_asg, x_hbm, xbuf.at[slot], y_hbm, gsem.at[slot], True, unrolled)

    def scatter(b, slot, unrolled):
        _row_copies(asg_ref, b * rows, rows, n_asg, x_hbm, ybuf.at[slot], y_hbm, ssem.at[slot], False, unrolled)

    def expert(slot):
        xb = _rows_from_tiles(xbuf.at[slot], rows).astype(BF16)
        hg = jnp.dot(xb, wgb[...], preferred_element_type=F32)
        hu = jnp.dot(xb, wub[...], preferred_element_type=F32)
        h = hg * jax.nn.sigmoid(hg) * hu
        _rows_to_tiles(ybuf.at[slot], jnp.dot(h.astype(BF16), wdb[...], preferred_element_type=F32))

    @pl.when(i == 0)
    def _():
        ybuf[1] = jnp.zeros(ybuf.shape[1:], F32)
        spare = [pltpu.make_async_copy(ybuf.at[1],
                                       y_hbm.at[pl.ds((n_asg + j * rows) * TILE_ROWS, rows * TILE_ROWS)], ssem.at[1])
                 for j in range((y_hbm.shape[0] // TILE_ROWS - n_asg) // rows)]
        for copy in spare:
            copy.start()
        for copy in spare:
            copy.wait()

    @pl.when((i >= 2) & (i - 2 < used))
    def _():
        wait_scatter(i % 2)

    steady = (i >= 1) & (i + 1 < used)
    for slot in range(2):
        @pl.when(steady & (i % 2 == slot))
        def _():
            wait_gather(slot)
            gather(i + 1, 1 - slot, True)
            scatter(i - 1, 1 - slot, True)
            expert(slot)

    @pl.when(jnp.logical_not(steady))
    def _():
        slot = i % 2

        @pl.when((i == 0) & (used > 0))
        def _():
            gather(0, 0, False)

        @pl.when(i < used)
        def _():
            wait_gather(slot)

        @pl.when(i + 1 < used)
        def _():
            gather(i + 1, 1 - slot, False)

        @pl.when((i >= 1) & (i - 1 < used))
        def _():
            scatter(i - 1, 1 - slot, False)

        @pl.when(i < used)
        def _():
            expert(slot)


def moe_experts(x1_tiles, n_asg, asg, blk_e, n_used, e_gate, e_up, e_down):
    d = e_gate.shape[2]
    assert d == TILE_ROWS * LANES and x1_tiles.shape[1] == LANES
    n_blocks = blk_e.shape[0]
    ff = e_gate.shape[-1]
    n_rows = n_asg + e_gate.shape[1] * EXPERT_BLOCK
    weight = lambda shape: pl.BlockSpec(
        (1, 1) + shape, lambda i, asg, be, nu: (0, be[jnp.minimum(i, n_blocks - 1)], 0, 0))
    return pl.pallas_call(
        functools.partial(_moe_expert_kernel, n_asg), name="moe_expert",
        grid_spec=pltpu.PrefetchScalarGridSpec(
            num_scalar_prefetch=3,
            grid=(n_blocks + DRAIN_STEPS,),
            in_specs=[pl.BlockSpec(memory_space=pl.ANY), weight((d, ff)), weight((d, ff)), weight((ff, d))],
            out_specs=pl.BlockSpec(memory_space=pl.ANY),
            scratch_shapes=[pltpu.VMEM((2, EXPERT_BLOCK * TILE_ROWS, LANES), F32),
                            pltpu.VMEM((2, EXPERT_BLOCK * TILE_ROWS, LANES), F32),
                            pltpu.VMEM((d, ff), BF16), pltpu.VMEM((d, ff), BF16), pltpu.VMEM((ff, d), BF16),
                            pltpu.SemaphoreType.DMA((2,)), pltpu.SemaphoreType.DMA((2,))]),
        out_shape=jax.ShapeDtypeStruct((n_rows * TILE_ROWS, LANES), F32),
        compiler_params=_params(("arbitrary",)),
    )(asg, blk_e, n_used, x1_tiles, e_gate, e_up, e_down)


def _moe_combine_kernel(alpha, lead_tiles, y0_ref, y1_ref, x1_ref, gate_ref, g_ref, b_ref, lead_ref, tail_ref):
    i = pl.program_id(0)
    gate = gate_ref[...]
    tm = x1_ref.shape[0]
    moe = gate[:, 0:1] * _rows_from_tiles(y0_ref, tm) + gate[:, 1:2] * _rows_from_tiles(y1_ref, tm)
    out = _ln(alpha * x1_ref[...] + moe, g_ref[...], b_ref[...])

    @pl.when(i < lead_tiles)
    def _():
        lead_ref[...] = out

    @pl.when(i >= lead_tiles)
    def _():
        tail_ref[...] = out


def moe_combine(y, x1, gate, g, b, tm, alpha, n, n_lead):
    d = x1.shape[1]
    assert n_lead % tm == 0 and (n - n_lead) % tm == 0 and n > n_lead
    lead_tiles = n_lead // tm
    return pl.pallas_call(
        functools.partial(_moe_combine_kernel, alpha, lead_tiles), name="moe_combine",
        grid=(n // tm,),
        in_specs=[pl.BlockSpec((tm * TILE_ROWS, LANES), lambda i: (i, 0)),
                  pl.BlockSpec((tm * TILE_ROWS, LANES), lambda i: (i + n // tm, 0)),
                  pl.BlockSpec((tm, d), lambda i: (i, 0)),
                  pl.BlockSpec((tm, LANES), lambda i: (i, 0)), _const_spec((1, d)), _const_spec((1, d))],
        out_specs=[pl.BlockSpec((tm, d), lambda i: (jnp.minimum(i, lead_tiles - 1), 0)),
                   pl.BlockSpec((tm, d), lambda i: (jnp.maximum(i - lead_tiles, 0), 0))],
        out_shape=[jax.ShapeDtypeStruct((n_lead, d), F32), jax.ShapeDtypeStruct((n - n_lead, d), F32)],
        compiler_params=_params(("arbitrary",)),
    )(y, y, x1, gate, g.reshape(1, d), b.reshape(1, d))


def moe_routing(experts):
    n, top_k = experts.shape
    n_exp = N_GROUPS * EXPERTS_PER_GROUP
    a = n * top_k
    flat_e = experts.T.reshape(a)
    order = jnp.argsort(flat_e, stable=True).astype(jnp.int32)
    counts = jnp.sum((flat_e[:, None] == jnp.arange(n_exp, dtype=jnp.int32)[None, :]).astype(jnp.int32), axis=0)
    ends = jnp.cumsum(counts)
    padded = (counts + EXPERT_BLOCK - 1) // EXPERT_BLOCK * EXPERT_BLOCK
    pad_end = jnp.cumsum(padded)
    n_blocks = -(-a // EXPERT_BLOCK) + n_exp
    blk_start = jnp.arange(n_blocks, dtype=jnp.int32) * EXPERT_BLOCK
    blk_e = jnp.minimum(jnp.sum((pad_end[None, :] <= blk_start[:, None]).astype(jnp.int32), axis=1), n_exp - 1)
    slot = jnp.arange(n_blocks * EXPERT_BLOCK, dtype=jnp.int32)
    slot_e = jnp.repeat(blk_e, EXPERT_BLOCK)
    rank = slot - (pad_end - padded)[slot_e]
    spare = a + jnp.clip(slot - ends[slot_e], 0, n_exp * EXPERT_BLOCK - 1)
    asg = jnp.where(rank < counts[slot_e], order[jnp.clip((ends - counts)[slot_e] + rank, 0, a - 1)], spare)
    n_used = (pad_end[-1:] // EXPERT_BLOCK).astype(jnp.int32)
    return asg.astype(jnp.int32), blk_e.astype(jnp.int32), n_used


def hier_moe_ln(x1, x1_tiles, eidx, gate, w, tm, alpha, n_tokens, n_lead):
    asg, blk_e, n_used = moe_routing(eidx[:n_tokens, :2])
    y = moe_experts(x1_tiles, 2 * n_tokens, asg, blk_e, n_used, w['e_gate'], w['e_up'], w['e_down'])
    return moe_combine(y, x1, gate, w['ln2_g'], w['ln2_b'], tm, alpha, n_tokens, n_lead)


def kernel(x_prompt, x_sample, mem_prompt, state_wkv, state_shift, cache_win_k, cache_win_v, cache_mem_k, cache_mem_v, ln_in_g, ln_in_b, w_in, mu, w0, w_up, a0, a_up, g_up, k_k, k_a, r_k, lnx_g, lnx_b, sinks, w_mem_kv, p_a, p_b, p_m, w_o, ln1_g, ln1_b, w_group, b_group, w_router, b_router, e_gate, e_up, e_down, ln2_g, ln2_b):
    depth = w_in.shape[0]
    assert depth == 1, "single-layer step"
    bsz, seq, d = x_prompt.shape
    dec = x_sample.shape[0]
    assert x_sample.shape[1] == 1
    c_shift = mu.shape[-1]
    c_a = w0.shape[-1]
    window, kv_w = cache_win_k.shape[2], cache_win_k.shape[3] * cache_win_k.shape[4]
    n_mem, mem_w = cache_mem_k.shape[2], cache_mem_k.shape[3] * cache_mem_k.shape[4]
    q_w = sinks.shape[-1] * HEAD
    qkvm_w = q_w + 2 * kv_w + mem_w
    alpha = (2.0 * depth) ** 0.25
    past_len = float(PAST_LEN)
    chunk = 64

    w_in_b = w_in[0].astype(BF16)
    w_parts = [w_in_b[:, :c_shift], w_in_b[:, c_shift:c_shift + qkvm_w], w_in_b[:, c_shift + qkvm_w:]]
    rp = dict(mu=mu[0], w0=w0[0], w_up=w_up[0], a0=a0[0], a_up=a_up[0], g_up=g_up[0], k_k=k_k[0], k_a=k_a[0],
              r_k=r_k[0].reshape(-1))
    n_route = N_GROUPS * (1 + EXPERTS_PER_GROUP)
    mw = dict(ln_in_g=ln_in_g.reshape(1, d), ln_in_b=ln_in_b.reshape(1, d), p_a=p_a[0].astype(BF16),
              p_b=p_b[0].astype(BF16), p_m=p_m[0].astype(BF16), w_o=w_o[0].astype(BF16),
              ln1_g=ln1_g[0].reshape(1, d), ln1_b=ln1_b[0].reshape(1, d),
              w_route=jnp.pad(jnp.concatenate([w_group[0], w_router[0]], axis=1), ((0, 0), (0, LANES - n_route))),
              b_route=jnp.pad(jnp.concatenate([b_group[0], b_router[0]]), (0, LANES - n_route)).reshape(1, LANES),
              e_gate=e_gate, e_up=e_up, e_down=e_down, ln2_g=ln2_g[0], ln2_b=ln2_b[0])

    xp = x_prompt.reshape(bsz * seq, d)
    zr, zq, zg = ln_proj(xp, ln_in_g, ln_in_b, w_parts, 256, (F32, F32, BF16))
    zr3 = zr.reshape(bsz, seq, c_shift)
    prep = rwkv_prep(zr3, jnp.zeros((bsz, 1, c_shift), F32), rp, chunk, 256)
    o_a, wkv_p = wkv(prep, jnp.zeros((bsz, c_a // HEAD, HEAD, HEAD), F32), lnx_g[0], lnx_b[0], chunk)
    mkv = proj(mem_prompt.reshape(bsz * n_mem, d), w_mem_kv[0].astype(BF16), 256).reshape(bsz, n_mem, 2 * mem_w)
    mk_p, mv_p = mkv[..., :mem_w], mkv[..., mem_w:]
    tables = rope_tables(jnp.arange(seq, dtype=F32))
    o_b, o_m, k_rot = attn_prompt(zq.reshape(bsz, seq, qkvm_w), mk_p, mv_p, sinks[0], tables, window, q_w, kv_w)
    n_all = bsz * seq + dec
    n_buf = -(-n_all // 256) * 256
    routed = merge(xp, o_a.reshape(-1, c_a), o_b.reshape(-1, q_w), o_m.reshape(-1, mem_w), zg, mw, 256, alpha, n_buf)
    shift_p = zr3[:, -1]
    kb_p = k_rot[:, -window:].reshape(bsz, window, H_KV, HEAD)
    vb_p = zq.reshape(bsz, seq, qkvm_w)[:, -window:, q_w + kv_w:q_w + 2 * kv_w].reshape(bsz, window, H_KV, HEAD)

    xs = x_sample.reshape(dec, d)
    zr_s, zq_s, zg_s = ln_proj(xs, ln_in_g, ln_in_b, w_parts, dec, (F32, F32, F32))
    ops_s = rwkv_prep(zr_s.reshape(1, dec, c_shift), state_shift[0].reshape(1, dec, c_shift), rp, 1, dec)
    o_a_s, wkv_s = wkv_step([a.reshape(dec, c_a) for a in ops_s], state_wkv[0], lnx_g[0], lnx_b[0], 8)
    tables_s = rope_tables(jnp.full((1,), past_len, F32))
    o_b_s, o_m_s, nk_s, nv_s = attn_step(
        zq_s, cache_win_k[0].reshape(dec, window, kv_w), cache_win_v[0].reshape(dec, window, kv_w),
        cache_mem_k[0].reshape(dec, n_mem, mem_w), cache_mem_v[0].reshape(dec, n_mem, mem_w),
        sinks[0], tables_s, q_w, kv_w, 8)
    x1, x1_tiles, eidx, gate = merge(xs, o_a_s, o_b_s, o_m_s, zg_s, mw, dec, alpha, n_buf, bsz * seq, routed)

    y_prompt, y_sample = hier_moe_ln(x1, x1_tiles, eidx, gate, mw, dec, alpha, n_all, bsz * seq)
    y_prompt = y_prompt.reshape(bsz, seq, d)
    y_sample = y_sample.reshape(dec, 1, d)

    sd = state_wkv.dtype
    return (y_prompt, y_sample, wkv_p[None].astype(sd), wkv_s[None].astype(sd), shift_p[None], zr_s[None],
            kb_p[None], vb_p[None], nk_s.reshape(dec, window, H_KV, HEAD)[None],
            nv_s.reshape(dec, window, H_KV, HEAD)[None],
            mk_p.reshape(bsz, n_mem, -1, HEAD)[None], mv_p.reshape(bsz, n_mem, -1, HEAD)[None])
```

```python
import functools
import math

import jax
import jax.numpy as jnp
from jax import lax
from jax.experimental import pallas as pl
from jax.experimental.pallas import tpu as pltpu

F32 = jnp.float32
BF16 = jnp.bfloat16
SCAN_BATCH = 4
CHUNKS_PER_STEP = 4

HEAD = 64
LANES = 128
H_KV = 2
ROT_HALF = 8
ROPE_THETA = 500000.0
PAST_LEN = 8192
N_GROUPS = 4
EXPERTS_PER_GROUP = 8
EXPERT_BLOCK = 128
LN_EPS = 1e-5
LNX_EPS = 64e-5
NEG_INF = -1e30
VMEM_LIMIT = 48 * 1024 * 1024


def _pieces(x, n):
    out = []
    for _ in range(n):
        p = x.astype(BF16)
        out.append(p)
        x = x - p.astype(F32)
    return out


def _mask_dot(x, mask, n=2):
    return sum(jnp.dot(p, mask, preferred_element_type=F32) for p in _pieces(x, n))


def _split3(x, axis, lhs):
    hi = x.astype(BF16).astype(F32)
    lo = x - hi
    return jnp.concatenate([hi, hi, lo] if lhs else [hi, lo, hi], axis=axis).astype(BF16)


def _dot3(a, b):
    return jnp.dot(_split3(a, 1, True), _split3(b, 0, False), preferred_element_type=F32)


def _dot3_t(a, b):
    return lax.dot_general(_split3(a, 1, True), _split3(b, 1, False), (((1,), (1,)), ((), ())),
                           preferred_element_type=F32)


def _bdot(a, b):
    return jnp.dot(a.astype(BF16), b.astype(BF16), preferred_element_type=F32)


def _bdot_t(a, b):
    return lax.dot_general(a.astype(BF16), b.astype(BF16), (((1,), (1,)), ((), ())), preferred_element_type=F32)


def _ln(x, g, b):
    mu = jnp.mean(x, axis=-1, keepdims=True)
    xc = x - mu
    var = jnp.mean(xc * xc, axis=-1, keepdims=True)
    return xc * lax.rsqrt(var + LN_EPS) * g + b


def _const_spec(shape):
    nd = len(shape)
    return pl.BlockSpec(shape, lambda *_: (0,) * nd)


def _params(sem):
    return pltpu.CompilerParams(dimension_semantics=sem, vmem_limit_bytes=VMEM_LIMIT)


def _ln_proj_prep_kernel(chunk, x_ref, g_ref, b_ref, wr_ref, wq_ref, wg_ref, prev_ref, mu_ref, w0_ref, wup_ref,
                         a0_ref, aup_ref, gup_ref, kk_ref, ka_ref, rk_ref, hsum_ref, tril_ref,
                         zq_ref, zg_ref, zr_ref, *refs):
    out_refs, carry_ref = refs[:-1], refs[-1]
    xn = _ln(x_ref[0], g_ref[...], b_ref[...]).astype(BF16)
    z = jnp.dot(xn, wr_ref[...], preferred_element_type=F32)
    zq_ref[0] = jnp.dot(xn, wq_ref[...], preferred_element_type=F32)
    zg_ref[0] = jnp.dot(xn, wg_ref[...], preferred_element_type=F32).astype(zg_ref.dtype)
    tt = z.shape[0]
    c_a = w0_ref.shape[-1]
    r_w, r_a, r_g = wup_ref.shape[0], aup_ref.shape[0], gup_ref.shape[0]
    if chunk == 1:
        zr_ref[0] = z
        prev = prev_ref[0]
    else:
        zr_ref[0] = z[tt - 1:tt, :]

        @pl.when(pl.program_id(1) == 0)
        def _():
            carry_ref[...] = prev_ref[0]

        row = lax.broadcasted_iota(jnp.int32, z.shape, 0)
        prev = jnp.where(row == 0, carry_ref[...], pltpu.roll(z, 1, 0))
        carry_ref[...] = z[tt - 1:tt, :]
    zs = z + (prev - z) * mu_ref[...]
    r = zs[:, :c_a]
    k = zs[:, c_a:2 * c_a]
    v = zs[:, 2 * c_a:3 * c_a]
    o = 3 * c_a
    xw = zs[:, o:o + r_w]
    xa = zs[:, o + r_w:o + r_w + r_a]
    xg = zs[:, o + r_w + r_a:o + r_w + r_a + r_g]
    warg = -(w0_ref[...] + _dot3(jnp.tanh(xw), wup_ref[...]))
    softplus = jnp.maximum(warg, 0.0) + jnp.log1p(jnp.exp(-jnp.abs(warg)))
    lw = -jnp.exp(-softplus - 0.5)
    a = jax.nn.sigmoid(a0_ref[...] + _dot3(xa, aup_ref[...]))
    g = _dot3(jax.nn.sigmoid(xg), gup_ref[...])
    kkr = k * kk_ref[...]
    kk = kkr / jnp.maximum(jnp.sqrt(_mask_dot(kkr * kkr, hsum_ref[...])), 1e-12)
    k2 = k * (1.0 + (a - 1.0) * ka_ref[...])
    bonus = _mask_dot(r * k2 * rk_ref[...], hsum_ref[...]) * v
    kb = kk * a
    if chunk == 1:
        outs = (r, jnp.exp(lw), k2, v, -kk, kb, g, bonus)
        for ref, val in zip(out_refs, outs):
            ref[0] = val
        return
    pieces = _pieces(lw, 3)
    ones3 = jnp.ones((chunk, 3 * chunk), BF16)
    cw, cwl = [], []
    for c in range(tt // chunk):
        stack = jnp.concatenate([p[c * chunk:(c + 1) * chunk] for p in pieces], axis=0)
        cw.append(jnp.dot(tril_ref[...], stack, preferred_element_type=F32))
        cwl.append(jnp.dot(ones3, stack, preferred_element_type=F32))
    cw = jnp.concatenate(cw, axis=0)
    cwl = jnp.concatenate(cwl, axis=0)
    e_inv = jnp.exp(-cw)
    outs = (r * jnp.exp(cw), -kk * jnp.exp(cw - lw), kb * e_inv, k2 * e_inv, v, g, bonus)
    for ref, val in zip(out_refs[:-1], outs):
        ref[0] = val
    wl = jnp.exp(cwl)
    for c in range(tt // chunk):
        out_refs[-1][0, c] = wl[c * chunk:c * chunk + 1, :]


def _head_sum_matrix(width):
    idx = jnp.arange(width)
    return ((idx[:, None] // HEAD) == (idx[None, :] // HEAD)).astype(BF16)


def ln_proj_prep(x, g, b, ws, prev, p, chunk, tt, gate_dtype):
    bsz, t, d = x.shape
    w_r, w_q, w_g = ws
    cs = w_r.shape[1]
    c_a = p['w0'].shape[-1]
    assert prev.shape[1] == (t if chunk == 1 else 1)
    ridx = jnp.arange(chunk)
    tril = jnp.tile((ridx[None, :] <= ridx[:, None]).astype(BF16), (1, 3))
    hsum = _head_sum_matrix(c_a)
    row = lambda a: a.reshape(1, -1)
    rows = lambda width: pl.BlockSpec((1, tt, width), lambda bi, i: (bi, i, 0))
    full = jax.ShapeDtypeStruct((bsz, t, c_a), F32)
    if chunk == 1:
        out_specs, out_shape = [rows(c_a)] * 8, [full] * 8
        prev_spec, zr_spec, zr_rows = rows(cs), rows(cs), t
    else:
        out_specs = [rows(c_a)] * 7 + [pl.BlockSpec((1, tt // chunk, 1, c_a), lambda bi, i: (bi, i, 0, 0))]
        out_shape = [full] * 7 + [jax.ShapeDtypeStruct((bsz, t // chunk, 1, c_a), F32)]
        prev_spec = zr_spec = pl.BlockSpec((1, 1, cs), lambda bi, i: (bi, 0, 0))
        zr_rows = 1
    consts = [row(p['mu']), row(p['w0']), p['w_up'], row(p['a0']), p['a_up'], p['g_up'], row(p['k_k']),
              row(p['k_a']), row(p['r_k']), hsum, tril]
    weight = lambda w: pl.BlockSpec(w.shape, lambda bi, i: (0, 0), pipeline_mode=pl.Buffered(1))
    outs = pl.pallas_call(
        functools.partial(_ln_proj_prep_kernel, chunk), name="ln_proj_prep",
        grid=(bsz, t // tt),
        in_specs=[rows(d), _const_spec((1, d)), _const_spec((1, d)), weight(w_r), weight(w_q), weight(w_g), prev_spec]
        + [_const_spec(c.shape) for c in consts],
        out_specs=[rows(w_q.shape[1]), rows(w_g.shape[1]), zr_spec] + out_specs,
        out_shape=[jax.ShapeDtypeStruct((bsz, t, w_q.shape[1]), F32),
                   jax.ShapeDtypeStruct((bsz, t, w_g.shape[1]), gate_dtype),
                   jax.ShapeDtypeStruct((bsz, zr_rows, cs), F32)] + out_shape,
        scratch_shapes=[pltpu.VMEM((1, cs), F32)],
        compiler_params=_params(("parallel", "arbitrary")),
    )(x, row(g), row(b), w_r, w_q, w_g, prev, *consts)
    return outs[0], outs[1], outs[2], outs[3:]


def _wkv_chunk_kernel(ra_ref, at_ref, bt_ref, kt_ref, v_ref, wl_ref, rp_ref, y0_ref, m_ref, n_ref):
    n_chunks = m_ref.shape[1]
    length = ra_ref.shape[1] // n_chunks
    heads = ra_ref.shape[2] // HEAD
    row = lax.broadcasted_iota(jnp.int32, (length, length), 0)
    col = lax.broadcasted_iota(jnp.int32, (length, length), 1)
    strict = row > col
    incl = row >= col
    hrow = lax.broadcasted_iota(jnp.int32, (HEAD, HEAD), 0)
    hcol = lax.broadcasted_iota(jnp.int32, (HEAD, HEAD), 1)
    units = [(c, h) for c in range(n_chunks) for h in range(heads)]
    us = range(len(units))
    at = lambda ref, u: ref[0, units[u][0] * length:(units[u][0] + 1) * length,
                            units[u][1] * HEAD:(units[u][1] + 1) * HEAD]
    gram = [_bdot_t(jnp.concatenate([at(at_ref, u), at(ra_ref, u)], axis=0),
                    jnp.concatenate([at(bt_ref, u), at(kt_ref, u)], axis=0)) for u in us]
    a_ab = [jnp.where(strict, g[:length, :length], 0.0) for g in gram]
    a_kk = [jnp.concatenate([jnp.where(strict, g[:length, length:], 0.0),
                             jnp.where(incl, g[length:, length:], 0.0)], axis=0) for g in gram]
    a_rb = [jnp.where(incl, g[length:, :length], 0.0) for g in gram]
    inv = [jnp.where(row == col, 1.0, a) for a in a_ab]
    pw = [_bdot(a, a) for a in a_ab]
    avy = [_dot3(a_kk[u], at(v_ref, u)) for u in us]
    w_l = [wl_ref[0, units[u][0], :, units[u][1] * HEAD:(units[u][1] + 1) * HEAD] for u in us]
    nk = [_dot3(at(v_ref, u).T, at(kt_ref, u) * w_l[u]) for u in us]
    for _ in range(int(math.log2(length)) - 2):
        both = [_bdot(jnp.concatenate([pw[u], inv[u]], axis=0), pw[u]) for u in us]
        pw = [b[:length] for b in both]
        inv = [inv[u] + both[u][length:] for u in us]
    inv = [inv[u] + _bdot(inv[u], pw[u]) for u in us]
    pq = [_dot3(inv[u], jnp.concatenate([at(at_ref, u), avy[u][:length]], axis=1)) for u in us]
    ry = [_dot3(a_rb[u], pq[u]) for u in us]
    mn = [_dot3(pq[u].T, at(bt_ref, u) * w_l[u]) for u in us]
    lanes = lambda f, c: jnp.concatenate([f(c * heads + h) for h in range(heads)], axis=1)
    rows = lambda f: jnp.concatenate([lanes(f, c) for c in range(n_chunks)], axis=0)
    rp_ref[0] = rows(lambda u: at(ra_ref, u) + ry[u][:, :HEAD])
    y0_ref[0] = rows(lambda u: avy[u][length:] + ry[u][:, HEAD:])
    for c in range(n_chunks):
        m_ref[0, c] = lanes(lambda u: jnp.where(hrow == hcol, w_l[u], 0.0) + mn[u][:HEAD], c)
        n_ref[0, c] = lanes(lambda u: nk[u] + mn[u][HEAD:], c)


def _head_norm_wide(y, hsum):
    yc = y - _mask_dot(y, hsum) * (1.0 / HEAD)
    return yc * lax.rsqrt(_mask_dot(yc * yc, hsum) * (1.0 / HEAD) + LNX_EPS)


def _wkv_scan_kernel(rp_ref, y0_ref, m_ref, n_ref, g_ref, bonus_ref, s0_ref, lg_ref, lb_ref, hsum_ref,
                     o_ref, sout_ref, s_ref):
    c = pl.program_id(1)
    nb = rp_ref.shape[0]
    heads = rp_ref.shape[2] // HEAD

    @pl.when(c == 0)
    def _():
        s_ref[...] = s0_ref[...]

    pairs = [(b, h, slice(h * HEAD, (h + 1) * HEAD)) for b in range(nb) for h in range(heads)]
    s = [s_ref[b, h] for b, h, _ in pairs]
    s_new = [n_ref[b, 0, :, sl] + _dot3(s[j], m_ref[b, 0, :, sl]) for j, (b, h, sl) in enumerate(pairs)]
    ys = [y0_ref[b, :, sl] + _dot3_t(rp_ref[b, :, sl], s[j]) for j, (b, h, sl) in enumerate(pairs)]
    for j, (b, h, _) in enumerate(pairs):
        s_ref[b, h] = s_new[j]
    length = rp_ref.shape[1]
    y = _head_norm_wide(jnp.concatenate([jnp.concatenate(ys[b * heads:(b + 1) * heads], axis=-1) for b in range(nb)],
                                        axis=0), hsum_ref[...])
    for b in range(nb):
        o = (y[b * length:(b + 1) * length] * lg_ref[...] + lb_ref[...] + bonus_ref[b]) * g_ref[b]
        o_ref[b] = o.astype(o_ref.dtype)

    @pl.when(c == pl.num_programs(1) - 1)
    def _():
        sout_ref[...] = s_ref[...]


def wkv(prep, s0, lnx_g, lnx_b, chunk):
    ra, at, bt, kt, v, g, bonus, wl = prep
    bsz, t, c_a = ra.shape
    heads = c_a // HEAD
    n_chunks = t // chunk
    assert chunk & (chunk - 1) == 0 and chunk >= 4 and t % chunk == 0
    per_step = CHUNKS_PER_STEP if n_chunks % CHUNKS_PER_STEP == 0 else 1
    tile = pl.BlockSpec((1, per_step * chunk, c_a), lambda b, c: (b, c, 0))
    mat = pl.BlockSpec((1, per_step, HEAD, c_a), lambda b, c: (b, c, 0, 0))
    full = jax.ShapeDtypeStruct((bsz, t, c_a), F32)
    mats = jax.ShapeDtypeStruct((bsz, n_chunks, HEAD, c_a), F32)
    rp, y0, m, n = pl.pallas_call(
        _wkv_chunk_kernel, name="wkv_chunk",
        grid=(bsz, n_chunks // per_step),
        in_specs=[tile] * 5 + [pl.BlockSpec((1, per_step, 1, c_a), lambda b, c: (b, c, 0, 0))],
        out_specs=[tile, tile, mat, mat],
        out_shape=[full, full, mats, mats],
        compiler_params=_params(("parallel", "parallel")),
    )(ra, at, bt, kt, v, wl)
    nb = SCAN_BATCH if bsz % SCAN_BATCH == 0 else 1
    tile = pl.BlockSpec((nb, chunk, c_a), lambda b, c: (b, c, 0))
    mat = pl.BlockSpec((nb, 1, HEAD, c_a), lambda b, c: (b, c, 0, 0))
    st = pl.BlockSpec((nb, heads, HEAD, HEAD), lambda b, c: (b, 0, 0, 0))
    return pl.pallas_call(
        _wkv_scan_kernel, name="wkv_scan",
        grid=(bsz // nb, n_chunks),
        in_specs=[tile, tile, mat, mat, tile, tile, st, _const_spec((1, c_a)), _const_spec((1, c_a)),
                  _const_spec((c_a, c_a))],
        out_specs=[tile, st],
        out_shape=[jax.ShapeDtypeStruct((bsz, t, c_a), BF16), jax.ShapeDtypeStruct(s0.shape, F32)],
        scratch_shapes=[pltpu.VMEM((nb, heads, HEAD, HEAD), F32)],
        compiler_params=_params(("parallel", "arbitrary")),
    )(rp, y0, m, n, g, bonus, s0, lnx_g.reshape(1, c_a), lnx_b.reshape(1, c_a), _head_sum_matrix(c_a))


def _wkv_step_kernel(r_ref, w_ref, k_ref, v_ref, a_ref, b_ref, g_ref, bonus_ref, s_ref, lg_ref, lb_ref, hsum_ref,
                     o_ref, sout_ref):
    bt, heads = s_ref.shape[0], s_ref.shape[1]
    c_a = heads * HEAD
    hsum = hsum_ref[...]
    diag = (lax.broadcasted_iota(jnp.int32, (HEAD, c_a), 1) % HEAD
            == lax.broadcasted_iota(jnp.int32, (HEAD, c_a), 0))
    seqs = range(bt)
    row = lambda ref, i: ref[i:i + 1, :]
    stack = lambda f: jnp.concatenate([f(i) for i in seqs], axis=0)
    piece = lambda x, i: x[i * HEAD:(i + 1) * HEAD]
    s = stack(lambda i: jnp.concatenate([s_ref[i, h] for h in range(heads)], axis=1))
    sa = _mask_dot(s * stack(lambda i: jnp.broadcast_to(row(a_ref, i), (HEAD, c_a))), hsum)
    v_rows = _mask_dot(stack(lambda i: jnp.where(diag, row(v_ref, i), 0.0)), hsum)
    s = stack(lambda i: piece(s, i) * row(w_ref, i) + piece(sa, i) * row(b_ref, i) + piece(v_rows, i) * row(k_ref, i))
    for i in seqs:
        for h in range(heads):
            sout_ref[i, h] = piece(s, i)[:, h * HEAD:(h + 1) * HEAD]
    y_rows = _mask_dot(stack(lambda i: piece(s, i) * row(r_ref, i)), hsum)
    y = stack(lambda i: jnp.sum(jnp.where(diag, piece(y_rows, i), 0.0), axis=0, keepdims=True))
    y = _head_norm_wide(y, hsum)
    o_ref[...] = (y * lg_ref[...] + lb_ref[...] + bonus_ref[...]) * g_ref[...]


def wkv_step(ops, s0, lnx_g, lnx_b, bt):
    bsz, c_a = ops[0].shape
    rows = pl.BlockSpec((bt, c_a), lambda i: (i, 0))
    st = pl.BlockSpec((bt,) + s0.shape[1:], lambda i: (i, 0, 0, 0))
    hsum = _head_sum_matrix(c_a)
    return pl.pallas_call(
        _wkv_step_kernel, name="wkv_step",
        grid=(bsz // bt,),
        in_specs=[rows] * 8 + [st, _const_spec((1, c_a)), _const_spec((1, c_a)), _const_spec(hsum.shape)],
        out_specs=[rows, st],
        out_shape=[jax.ShapeDtypeStruct((bsz, c_a), F32), jax.ShapeDtypeStruct(s0.shape, F32)],
        compiler_params=_params(("parallel",)),
    )(*ops, s0, lnx_g.reshape(1, c_a), lnx_b.reshape(1, c_a), hsum)


def rope_tables(pos):
    inv_freq = ROPE_THETA ** (-jnp.arange(ROT_HALF, dtype=F32) / ROT_HALF)
    ang = pos[:, None] * inv_freq[None, :]
    cos, sin = jnp.cos(ang), jnp.sin(ang)
    t = pos.shape[0]
    rest = HEAD - 2 * ROT_HALF
    c = jnp.concatenate([cos, cos, jnp.ones((t, rest), F32)], axis=1)
    s1 = jnp.concatenate([jnp.zeros((t, ROT_HALF), F32), sin, jnp.zeros((t, rest), F32)], axis=1)
    s2 = jnp.concatenate([-sin, jnp.zeros((t, HEAD - ROT_HALF), F32)], axis=1)
    rep = LANES // HEAD
    return tuple(jnp.tile(x, (1, rep)) for x in (c, s1, s2))


def _rope(x, c, s1, s2):
    width = x.shape[-1]
    rep = width // c.shape[-1]
    if rep > 1:
        c, s1, s2 = (jnp.concatenate([t] * rep, axis=1) for t in (c, s1, s2))
    return x * c + pltpu.roll(x, ROT_HALF, 1) * s1 + pltpu.roll(x, width - ROT_HALF, 1) * s2


def _attn_kernel(window, q_w, kv_w, zq_ref, kp_ref, vp_ref, rc_ref, rs1_ref, rs2_ref, pc_ref, ps1_ref, ps2_ref,
                 mk_ref, mv_ref, sink_ref, ob_ref, om_ref, kr_ref):
    n = pl.program_id(1)
    zq = zq_ref[0]
    scale = HEAD ** -0.5
    q = zq[:, :q_w]
    k = zq[:, q_w:q_w + kv_w]
    v = zq[:, q_w + kv_w:q_w + 2 * kv_w]
    qm = zq[:, q_w + 2 * kv_w:]
    qr = _rope(q, rc_ref[...], rs1_ref[...], rs2_ref[...]) * scale
    kr = _rope(k, rc_ref[...], rs1_ref[...], rs2_ref[...])
    kr_ref[0] = kr
    kpr = _rope(kp_ref[0], pc_ref[...], ps1_ref[...], ps2_ref[...])
    vp = vp_ref[0]
    gqa = q_w // kv_w
    assert window & (window - 1) == 0
    qi = lax.broadcasted_iota(jnp.int32, (gqa * window, 2 * window), 0) & (window - 1)
    kj = lax.broadcasted_iota(jnp.int32, (gqa * window, 2 * window), 1)
    valid = (kj > qi) & (kj <= qi + window) & ((kj >= window) | (n > 0))
    tdot = lambda a, b: lax.dot_general(a, b, (((1,), (1,)), ((), ())), preferred_element_type=F32)
    dot = lambda a, b: jnp.dot(a, b, preferred_element_type=F32)
    kvs = range(kv_w // HEAD)
    mhs = range(qm.shape[1] // HEAD)
    hsl = lambda h: slice(h * HEAD, (h + 1) * HEAD)
    qmb = (qm * scale).astype(BF16)
    mk = mk_ref[0].astype(BF16)
    mv = mv_ref[0].astype(BF16)
    kcat = [jnp.concatenate([kpr[:, hsl(hk)], kr[:, hsl(hk)]], axis=0).astype(BF16) for hk in kvs]
    vcat = [jnp.concatenate([vp[:, hsl(hk)], v[:, hsl(hk)]], axis=0).astype(BF16) for hk in kvs]
    qs = [jnp.concatenate([qr[:, hsl(hk * gqa + g)] for g in range(gqa)], axis=0).astype(BF16) for hk in kvs]
    s = [jnp.where(valid, tdot(qs[hk], kcat[hk]), NEG_INF) for hk in kvs]
    sm = [tdot(qmb[:, hsl(h)], mk[:, hsl(h)]) for h in mhs]
    sink = [jnp.concatenate([jnp.full((window, 1), sink_ref[hk * gqa + g], F32) for g in range(gqa)], axis=0)
            for hk in kvs]
    m = [jnp.maximum(jnp.max(s[hk], axis=-1, keepdims=True), sink[hk]) for hk in kvs]
    p = [jnp.exp(s[hk] - m[hk]) for hk in kvs]
    pm = [jnp.exp(sm[h] - jnp.max(sm[h], axis=-1, keepdims=True)) for h in mhs]
    o = [dot(p[hk].astype(BF16), vcat[hk])
         / (jnp.sum(p[hk], axis=-1, keepdims=True) + jnp.exp(sink[hk] - m[hk])) for hk in kvs]
    om = [dot(pm[h].astype(BF16), mv[:, hsl(h)]) / jnp.sum(pm[h], axis=-1, keepdims=True) for h in mhs]
    ob = jnp.concatenate([o[hk][g * window:(g + 1) * window] for hk in kvs for g in range(gqa)], axis=1)
    ob_ref[0] = ob.astype(ob_ref.dtype)
    om_ref[0] = jnp.concatenate(om, axis=1).astype(om_ref.dtype)


def attn_prompt(zq, mk, mv, sinks, tables, window, q_w, kv_w):
    bsz, t, zw = zq.shape
    mem_w = zw - q_w - 2 * kv_w
    assert kv_w == LANES and q_w % kv_w == 0
    kcol, vcol = q_w // kv_w, q_w // kv_w + 1
    prev = lambda n: jnp.maximum(n - 1, 0)
    tab = pl.BlockSpec((window, LANES), lambda b, n: (n, 0))
    ptab = pl.BlockSpec((window, LANES), lambda b, n: (prev(n), 0))
    mem = pl.BlockSpec((1,) + mk.shape[1:], lambda b, n: (b, 0, 0))
    return pl.pallas_call(
        functools.partial(_attn_kernel, window, q_w, kv_w), name="attn",
        grid=(bsz, t // window),
        in_specs=[pl.BlockSpec((1, window, zw), lambda b, n: (b, n, 0)),
                  pl.BlockSpec((1, window, kv_w), lambda b, n: (b, prev(n), kcol)),
                  pl.BlockSpec((1, window, kv_w), lambda b, n: (b, prev(n), vcol)),
                  tab, tab, tab, ptab, ptab, ptab, mem, mem,
                  pl.BlockSpec(memory_space=pltpu.SMEM)],
        out_specs=[pl.BlockSpec((1, window, q_w), lambda b, n: (b, n, 0)),
                   pl.BlockSpec((1, window, mem_w), lambda b, n: (b, n, 0)),
                   pl.BlockSpec((1, window, kv_w), lambda b, n: (b, n, 0))],
        out_shape=[jax.ShapeDtypeStruct((bsz, t, q_w), BF16), jax.ShapeDtypeStruct((bsz, t, mem_w), BF16),
                   jax.ShapeDtypeStruct((bsz, t, kv_w), F32)],
        compiler_params=_params(("parallel", "parallel")),
    )(zq, zq, zq, *tables, *tables, mk, mv, sinks)


def _attn_step_kernel(q_w, kv_w, zq_ref, ck_ref, cv_ref, mk_ref, mv_ref, rc_ref, rs1_ref, rs2_ref, sink_ref,
                      ob_ref, om_ref, nk_ref, nv_ref):
    bt = zq_ref.shape[0]
    window = ck_ref.shape[1]
    mem_w = om_ref.shape[1]
    n_q, gqa, per_vreg = q_w // HEAD, q_w // kv_w, LANES // HEAD
    scale = HEAD ** -0.5
    zq = zq_ref[...]
    q = _rope(zq[:, :q_w], rc_ref[...], rs1_ref[...], rs2_ref[...]) * scale
    k_new = _rope(zq[:, q_w:q_w + kv_w], rc_ref[...], rs1_ref[...], rs2_ref[...])
    v_new = zq[:, q_w + kv_w:q_w + 2 * kv_w]
    qm = zq[:, q_w + 2 * kv_w:] * scale
    own = lambda w: (lax.broadcasted_iota(jnp.int32, (n_q, w), 1) // HEAD
                     == lax.broadcasted_iota(jnp.int32, (n_q, w), 0))
    own_q, own_m = own(q_w), own(mem_w)
    hrow = lax.broadcasted_iota(jnp.int32, (n_q, LANES), 0)
    hblk = lax.broadcasted_iota(jnp.int32, (n_q, LANES), 1) // HEAD
    swap = (hrow % per_vreg) != (hrow // gqa)
    keep = hblk == hrow % per_vreg
    key_ok = lax.broadcasted_iota(jnp.int32, (n_q, window), 1) >= 1
    wrow = lax.broadcasted_iota(jnp.int32, (window, kv_w), 0)
    sink = sink_ref[...]
    bs = range(bt)
    tdot = lambda a, b: lax.dot_general(a.astype(BF16), b.astype(BF16), (((1,), (1,)), ((), ())),
                                        preferred_element_type=F32)
    for b in bs:
        nk_ref[b] = jnp.where(wrow == window - 1, k_new[b:b + 1], pltpu.roll(ck_ref[b], window - 1, 0))
        nv_ref[b] = jnp.where(wrow == window - 1, v_new[b:b + 1], pltpu.roll(cv_ref[b], window - 1, 0))
    q8 = []
    for b in bs:
        rep = jnp.where(own_q, q[b:b + 1], 0.0)
        fold = sum(rep[:, c * LANES:(c + 1) * LANES] for c in range(q_w // LANES))
        q8.append(jnp.where(swap, pltpu.roll(fold, HEAD, 1), fold))
    qm8 = [jnp.where(own_m, qm[b:b + 1], 0.0) for b in bs]
    s = [jnp.where(key_ok, tdot(q8[b], ck_ref[b]), NEG_INF) for b in bs]
    sm = [tdot(qm8[b], mk_ref[b]) for b in bs]
    s_new = [jnp.sum(q8[b] * k_new[b:b + 1], axis=1, keepdims=True) for b in bs]
    m = [jnp.maximum(jnp.maximum(jnp.max(s[b], axis=1, keepdims=True), s_new[b]), sink) for b in bs]
    p = [jnp.exp(s[b] - m[b]) for b in bs]
    p_new = [jnp.exp(s_new[b] - m[b]) for b in bs]
    pm = [jnp.exp(sm[b] - jnp.max(sm[b], axis=1, keepdims=True)) for b in bs]
    o8 = [(_bdot(p[b], cv_ref[b]) + p_new[b] * v_new[b:b + 1])
          / (jnp.sum(p[b], axis=1, keepdims=True) + p_new[b] + jnp.exp(sink - m[b])) for b in bs]
    om8 = [_bdot(pm[b], mv_ref[b]) / jnp.sum(pm[b], axis=1, keepdims=True) for b in bs]
    ob, om = [], []
    for b in bs:
        o = jnp.where(swap, pltpu.roll(o8[b], HEAD, 1), o8[b])
        o = jnp.concatenate([jnp.where(keep, o, 0.0)] * (q_w // LANES), axis=1)
        ob.append(jnp.sum(jnp.where(own_q, o, 0.0), axis=0, keepdims=True))
        om.append(jnp.sum(jnp.where(own_m, om8[b], 0.0), axis=0, keepdims=True))
    ob_ref[...] = jnp.concatenate(ob, axis=0)
    om_ref[...] = jnp.concatenate(om, axis=0)


def attn_step(zq, ck, cv, mk, mv, sinks, tables, q_w, kv_w, bt):
    bsz, zw = zq.shape
    mem_w = zw - q_w - 2 * kv_w
    window, n_mem = ck.shape[1], mk.shape[1]
    n_q = q_w // HEAD
    assert kv_w == LANES and mem_w // HEAD <= n_q and n_q == 8
    rows = lambda w: pl.BlockSpec((bt, w), lambda i: (i, 0))
    cache = pl.BlockSpec((bt, window, kv_w), lambda i: (i, 0, 0))
    mem = pl.BlockSpec((bt, n_mem, mem_w), lambda i: (i, 0, 0))
    consts = list(tables) + [sinks.reshape(n_q, 1)]
    const_specs = [_const_spec(c.shape) for c in consts]
    return pl.pallas_call(
        functools.partial(_attn_step_kernel, q_w, kv_w), name="attn_step",
        grid=(bsz // bt,),
        in_specs=[rows(zw), cache, cache, mem, mem] + const_specs,
        out_specs=[rows(q_w), rows(mem_w), cache, cache],
        out_shape=[jax.ShapeDtypeStruct((bsz, q_w), F32), jax.ShapeDtypeStruct((bsz, mem_w), F32),
                   jax.ShapeDtypeStruct(ck.shape, F32), jax.ShapeDtypeStruct(cv.shape, F32)],
        compiler_params=_params(("parallel",)),
    )(zq, ck, cv, mk, mv, *consts)


def _proj_kernel(x_ref, w_ref, o_ref):
    o_ref[...] = jnp.dot(x_ref[...].astype(BF16), w_ref[...], preferred_element_type=F32)


def proj(x, w, tm):
    n, d = x.shape
    return pl.pallas_call(
        _proj_kernel, name="proj",
        grid=(n // tm,),
        in_specs=[pl.BlockSpec((tm, d), lambda i: (i, 0)), _const_spec(w.shape)],
        out_specs=pl.BlockSpec((tm, w.shape[1]), lambda i: (i, 0)),
        out_shape=jax.ShapeDtypeStruct((n, w.shape[1]), F32),
        compiler_params=_params(("parallel",)),
    )(x, w)


def _merge_kernel(alpha, n_tiles, x_ref, oa_ref, ob_ref, om_ref, gt_ref, lig_ref, lib_ref, pa_ref, pb_ref, pm_ref,
                  wo_ref, l1g_ref, l1b_ref, wr_ref, br_ref, *refs):
    outs = refs[-4:]

    @pl.when(pl.program_id(0) >= n_tiles)
    def _():
        for ref in outs:
            ref[...] = jnp.zeros_like(ref)

    @pl.when(pl.program_id(0) < n_tiles)
    def _():
        _merge_tile(alpha, x_ref, oa_ref, ob_ref, om_ref, gt_ref, lig_ref, lib_ref, pa_ref, pb_ref, pm_ref, wo_ref,
                    l1g_ref, l1b_ref, wr_ref, br_ref, *outs)


def _merge_tile(alpha, x_ref, oa_ref, ob_ref, om_ref, gt_ref, lig_ref, lib_ref, pa_ref, pb_ref, pm_ref, wo_ref,
                l1g_ref, l1b_ref, wr_ref, br_ref, x1_ref, x1t_ref, eidx_ref, gate_ref):
    d = x_ref.shape[1]
    xn = _ln(x_ref[...], lig_ref[...], lib_ref[...])
    gts = jax.nn.sigmoid(gt_ref[...].astype(F32))
    merged = (gts[:, :d] * _bdot(oa_ref[...], pa_ref[...]) + gts[:, d:2 * d] * _bdot(ob_ref[...], pb_ref[...])
              + gts[:, 2 * d:] * _bdot(om_ref[...], pm_ref[...]))
    x1 = _ln(alpha * xn + _bdot(merged, wo_ref[...]), l1g_ref[...], l1b_ref[...])
    x1_ref[...] = x1
    _rows_to_tiles(x1t_ref, x1)
    x_hi, x_lo = _pieces(x1, 2)
    w_hi, w_lo = _pieces(wr_ref[...], 2)
    dot = lambda a, b: jnp.dot(a, b, preferred_element_type=F32)
    logits = dot(x_hi, w_hi) + dot(x_hi, w_lo) + dot(x_lo, w_hi) + br_ref[...]
    lane = lax.broadcasted_iota(jnp.int32, logits.shape, 1)
    lane_f = lane.astype(F32)
    first = lambda hit: jnp.min(jnp.where(hit, lane_f, float(LANES)), axis=-1, keepdims=True).astype(jnp.int32)
    gmask = lane < N_GROUPS
    gl = jnp.where(gmask, logits, NEG_INF)
    gmax = jnp.max(gl, axis=-1, keepdims=True)
    gidx = first(gl == gmax)
    g_w = 1.0 / jnp.sum(jnp.where(gmask, jnp.exp(gl - gmax), 0.0), axis=-1, keepdims=True)
    lo = N_GROUPS + gidx * EXPERTS_PER_GROUP
    el = jnp.where((lane >= lo) & (lane < lo + EXPERTS_PER_GROUP), logits, NEG_INF)
    v1 = jnp.max(el, axis=-1, keepdims=True)
    i1 = first(el == v1)
    el2 = jnp.where(lane == i1, NEG_INF, el)
    v2 = jnp.max(el2, axis=-1, keepdims=True)
    i2 = first(el2 == v2)
    e2 = jnp.exp(v2 - v1)
    gate1 = g_w / (1.0 + e2)
    eidx_ref[...] = jnp.where(lane == 0, i1 - N_GROUPS, jnp.where(lane == 1, i2 - N_GROUPS, 0))
    gate_ref[...] = jnp.where(lane == 0, gate1, jnp.where(lane == 1, gate1 * e2, 0.0))


def merge(x, oa, ob, om, gt, w, tm, alpha, n_total, row_offset=0, into=None):
    n, d = x.shape
    assert row_offset % tm == 0 and n % tm == 0 and n_total % tm == 0
    off = row_offset // tm
    into = list(into or [])
    n_tiles = n // tm
    steps = n_tiles if into else n_total // tm
    rows = lambda a: pl.BlockSpec((tm, a.shape[1]), lambda i: (jnp.minimum(i, n_tiles - 1), 0))
    out = lambda width: pl.BlockSpec((tm, width), lambda i: (i + off, 0))
    consts = [w['ln_in_g'], w['ln_in_b'], w['p_a'], w['p_b'], w['p_m'], w['w_o'], w['ln1_g'], w['ln1_b'],
              w['w_route'], w['b_route']]
    n_in = 5 + len(consts)
    return pl.pallas_call(
        functools.partial(_merge_kernel, alpha, n_tiles), name="merge",
        grid=(steps,),
        in_specs=[rows(a) for a in (x, oa, ob, om, gt)] + [_const_spec(c.shape) for c in consts]
        + [pl.BlockSpec(memory_space=pl.ANY)] * len(into),
        out_specs=[out(d), pl.BlockSpec((tm * TILE_ROWS, LANES), lambda i: (i + off, 0)), out(LANES), out(LANES)],
        out_shape=[jax.ShapeDtypeStruct((n_total, d), F32), jax.ShapeDtypeStruct((n_total * TILE_ROWS, LANES), F32),
                   jax.ShapeDtypeStruct((n_total, LANES), jnp.int32), jax.ShapeDtypeStruct((n_total, LANES), F32)],
        input_output_aliases={n_in + k: k for k in range(len(into))},
        compiler_params=_params(("parallel",)),
    )(x, oa, ob, om, gt, *consts, *into)


ROW_DMA_UNROLL = 8
DRAIN_STEPS = 2


TILE_ROWS = 8


def _rows_from_tiles(ref, n):
    return jnp.concatenate([ref[pl.ds(s, n, stride=TILE_ROWS), :] for s in range(TILE_ROWS)], axis=1)


def _rows_to_tiles(ref, x):
    for s in range(TILE_ROWS):
        ref[pl.ds(s, x.shape[0], stride=TILE_ROWS), :] = x[:, s * LANES:(s + 1) * LANES]


def _row_copies(asg_ref, base, count, n_asg, x_hbm, buf, y_hbm, sem, gather, unrolled):
    def tile(ref, idx):
        start = idx * TILE_ROWS
        return ref.at[pl.ds(start if isinstance(idx, int) else pl.multiple_of(start, TILE_ROWS), TILE_ROWS)]

    def one(r, priority):
        a = asg_ref[base + r]
        if gather:
            tok = jnp.minimum(a, n_asg - 1)
            tok = jnp.where(tok >= n_asg // 2, tok - n_asg // 2, tok)
            copy = pltpu.make_async_copy(tile(x_hbm, tok), tile(buf, r), sem)
        else:
            copy = pltpu.make_async_copy(tile(buf, r), tile(y_hbm, a), sem)
        copy.start(priority=priority)

    if unrolled:
        for r in range(count):
            one(r, r % 2)
        return

    def body(g, carry):
        for j in range(ROW_DMA_UNROLL):
            one(g * ROW_DMA_UNROLL + j, j % 2)
        return carry
    lax.fori_loop(0, count // ROW_DMA_UNROLL, body, 0)


def _moe_expert_kernel(n_asg, asg_ref, be_ref, nu_ref, x_hbm, wg_ref, wu_ref, wd_ref, y_hbm,
                       xbuf, ybuf, wgb, wub, wdb, gsem, ssem):
    i = pl.program_id(0)
    used = nu_ref[0]
    rows = xbuf.shape[1] // TILE_ROWS
    n_blocks = be_ref.shape[0]
    blk = jnp.minimum(i, n_blocks - 1)

    @pl.when((i == 0) | (be_ref[blk] != be_ref[jnp.maximum(blk - 1, 0)]))
    def _():
        wgb[...] = wg_ref[0, 0].astype(BF16)
        wub[...] = wu_ref[0, 0].astype(BF16)
        wdb[...] = wd_ref[0, 0].astype(BF16)

    def wait_gather(slot):
        pltpu.make_async_copy(x_hbm.at[pl.ds(0, rows * TILE_ROWS)], xbuf.at[slot], gsem.at[slot]).wait()

    def wait_scatter(slot):
        pltpu.make_async_copy(ybuf.at[slot], y_hbm.at[pl.ds(0, rows * TILE_ROWS)], ssem.at[slot]).wait()

    def gather(b, slot, unrolled):
        _row_copies(asg_ref, b * rows, rows, n_asg, x_hbm, xbuf.at[slot], y_hbm, gsem.at[slot], True, unrolled)

    def scatter(b, slot, unrolled):
        _row_copies(asg_ref, b * rows, rows, n_asg, x_hbm, ybuf.at[slot], y_hbm, ssem.at[slot], False, unrolled)

    def expert(slot):
        xb = _rows_from_tiles(xbuf.at[slot], rows).astype(BF16)
        hg = jnp.dot(xb, wgb[...], preferred_element_type=F32)
        hu = jnp.dot(xb, wub[...], preferred_element_type=F32)
        h = hg * jax.nn.sigmoid(hg) * hu
        _rows_to_tiles(ybuf.at[slot], jnp.dot(h.astype(BF16), wdb[...], preferred_element_type=F32))

    @pl.when(i == 0)
    def _():
        ybuf[1] = jnp.zeros(ybuf.shape[1:], F32)
        spare = [pltpu.make_async_copy(ybuf.at[1],
                                       y_hbm.at[pl.ds((n_asg + j * rows) * TILE_ROWS, rows * TILE_ROWS)], ssem.at[1])
                 for j in range((y_hbm.shape[0] // TILE_ROWS - n_asg) // rows)]
        for copy in spare:
            copy.start()
        for copy in spare:
            copy.wait()

    @pl.when((i >= 2) & (i - 2 < used))
    def _():
        wait_scatter(i % 2)

    steady = (i >= 1) & (i + 1 < used)
    for slot in range(2):
        @pl.when(steady & (i % 2 == slot))
        def _():
            wait_gather(slot)
            gather(i + 1, 1 - slot, True)
            scatter(i - 1, 1 - slot, True)
            expert(slot)

    @pl.when(jnp.logical_not(steady))
    def _():
        slot = i % 2

        @pl.when((i == 0) & (used > 0))
        def _():
            gather(0, 0, False)

        @pl.when(i < used)
        def _():
            wait_gather(slot)

        @pl.when(i + 1 < used)
        def _():
            gather(i + 1, 1 - slot, False)

        @pl.when((i >= 1) & (i - 1 < used))
        def _():
            scatter(i - 1, 1 - slot, False)

        @pl.when(i < used)
        def _():
            expert(slot)


def moe_experts(x1_tiles, n_asg, asg, blk_e, n_used, e_gate, e_up, e_down):
    d = e_gate.shape[2]
    assert d == TILE_ROWS * LANES and x1_tiles.shape[1] == LANES
    n_blocks = blk_e.shape[0]
    ff = e_gate.shape[-1]
    n_rows = n_asg + e_gate.shape[1] * EXPERT_BLOCK
    weight = lambda shape: pl.BlockSpec(
        (1, 1) + shape, lambda i, asg, be, nu: (0, be[jnp.minimum(i, n_blocks - 1)], 0, 0))
    return pl.pallas_call(
        functools.partial(_moe_expert_kernel, n_asg), name="moe_expert",
        grid_spec=pltpu.PrefetchScalarGridSpec(
            num_scalar_prefetch=3,
            grid=(n_blocks + DRAIN_STEPS,),
            in_specs=[pl.BlockSpec(memory_space=pl.ANY), weight((d, ff)), weight((d, ff)), weight((ff, d))],
            out_specs=pl.BlockSpec(memory_space=pl.ANY),
            scratch_shapes=[pltpu.VMEM((2, EXPERT_BLOCK * TILE_ROWS, LANES), F32),
                            pltpu.VMEM((2, EXPERT_BLOCK * TILE_ROWS, LANES), F32),
                            pltpu.VMEM((d, ff), BF16), pltpu.VMEM((d, ff), BF16), pltpu.VMEM((ff, d), BF16),
                            pltpu.SemaphoreType.DMA((2,)), pltpu.SemaphoreType.DMA((2,))]),
        out_shape=jax.ShapeDtypeStruct((n_rows * TILE_ROWS, LANES), F32),
        compiler_params=_params(("arbitrary",)),
    )(asg, blk_e, n_used, x1_tiles, e_gate, e_up, e_down)


def _moe_combine_kernel(alpha, lead_tiles, y0_ref, y1_ref, x1_ref, gate_ref, g_ref, b_ref, lead_ref, tail_ref):
    i = pl.program_id(0)
    gate = gate_ref[...]
    tm = x1_ref.shape[0]
    moe = gate[:, 0:1] * _rows_from_tiles(y0_ref, tm) + gate[:, 1:2] * _rows_from_tiles(y1_ref, tm)
    out = _ln(alpha * x1_ref[...] + moe, g_ref[...], b_ref[...])

    @pl.when(i < lead_tiles)
    def _():
        lead_ref[...] = out

    @pl.when(i >= lead_tiles)
    def _():
        tail_ref[...] = out


def moe_combine(y, x1, gate, g, b, tm, alpha, n, n_lead):
    d = x1.shape[1]
    assert n_lead % tm == 0 and (n - n_lead) % tm == 0 and n > n_lead
    lead_tiles = n_lead // tm
    return pl.pallas_call(
        functools.partial(_moe_combine_kernel, alpha, lead_tiles), name="moe_combine",
        grid=(n // tm,),
        in_specs=[pl.BlockSpec((tm * TILE_ROWS, LANES), lambda i: (i, 0)),
                  pl.BlockSpec((tm * TILE_ROWS, LANES), lambda i: (i + n // tm, 0)),
                  pl.BlockSpec((tm, d), lambda i: (i, 0)),
                  pl.BlockSpec((tm, LANES), lambda i: (i, 0)), _const_spec((1, d)), _const_spec((1, d))],
        out_specs=[pl.BlockSpec((tm, d), lambda i: (jnp.minimum(i, lead_tiles - 1), 0)),
                   pl.BlockSpec((tm, d), lambda i: (jnp.maximum(i - lead_tiles, 0), 0))],
        out_shape=[jax.ShapeDtypeStruct((n_lead, d), F32), jax.ShapeDtypeStruct((n - n_lead, d), F32)],
        compiler_params=_params(("arbitrary",)),
    )(y, y, x1, gate, g.reshape(1, d), b.reshape(1, d))


def moe_routing(experts):
    n, top_k = experts.shape
    n_exp = N_GROUPS * EXPERTS_PER_GROUP
    a = n * top_k
    flat_e = experts.T.reshape(a)
    order = jnp.argsort(flat_e, stable=True).astype(jnp.int32)
    counts = jnp.sum((flat_e[:, None] == jnp.arange(n_exp, dtype=jnp.int32)[None, :]).astype(jnp.int32), axis=0)
    ends = jnp.cumsum(counts)
    padded = (counts + EXPERT_BLOCK - 1) // EXPERT_BLOCK * EXPERT_BLOCK
    pad_end = jnp.cumsum(padded)
    n_blocks = -(-a // EXPERT_BLOCK) + n_exp
    blk_start = jnp.arange(n_blocks, dtype=jnp.int32) * EXPERT_BLOCK
    blk_e = jnp.minimum(jnp.sum((pad_end[None, :] <= blk_start[:, None]).astype(jnp.int32), axis=1), n_exp - 1)
    slot = jnp.arange(n_blocks * EXPERT_BLOCK, dtype=jnp.int32)
    slot_e = jnp.repeat(blk_e, EXPERT_BLOCK)
    rank = slot - (pad_end - padded)[slot_e]
    spare = a + jnp.clip(slot - ends[slot_e], 0, n_exp * EXPERT_BLOCK - 1)
    asg = jnp.where(rank < counts[slot_e], order[jnp.clip((ends - counts)[slot_e] + rank, 0, a - 1)], spare)
    n_used = (pad_end[-1:] // EXPERT_BLOCK).astype(jnp.int32)
    return asg.astype(jnp.int32), blk_e.astype(jnp.int32), n_used


def hier_moe_ln(x1, x1_tiles, eidx, gate, w, tm, alpha, n_tokens, n_lead):
    asg, blk_e, n_used = moe_routing(eidx[:n_tokens, :2])
    y = moe_experts(x1_tiles, 2 * n_tokens, asg, blk_e, n_used, w['e_gate'], w['e_up'], w['e_down'])
    return moe_combine(y, x1, gate, w['ln2_g'], w['ln2_b'], tm, alpha, n_tokens, n_lead)


def kernel(x_prompt, x_sample, mem_prompt, state_wkv, state_shift, cache_win_k, cache_win_v, cache_mem_k, cache_mem_v, ln_in_g, ln_in_b, w_in, mu, w0, w_up, a0, a_up, g_up, k_k, k_a, r_k, lnx_g, lnx_b, sinks, w_mem_kv, p_a, p_b, p_m, w_o, ln1_g, ln1_b, w_group, b_group, w_router, b_router, e_gate, e_up, e_down, ln2_g, ln2_b):
    depth = w_in.shape[0]
    assert depth == 1, "single-layer step"
    bsz, seq, d = x_prompt.shape
    dec = x_sample.shape[0]
    assert x_sample.shape[1] == 1
    c_shift = mu.shape[-1]
    c_a = w0.shape[-1]
    window, kv_w = cache_win_k.shape[2], cache_win_k.shape[3] * cache_win_k.shape[4]
    n_mem, mem_w = cache_mem_k.shape[2], cache_mem_k.shape[3] * cache_mem_k.shape[4]
    q_w = sinks.shape[-1] * HEAD
    qkvm_w = q_w + 2 * kv_w + mem_w
    alpha = (2.0 * depth) ** 0.25
    past_len = float(PAST_LEN)
    chunk = 64

    w_in_b = w_in[0].astype(BF16)
    w_parts = [w_in_b[:, :c_shift], w_in_b[:, c_shift:c_shift + qkvm_w], w_in_b[:, c_shift + qkvm_w:]]
    rp = dict(mu=mu[0], w0=w0[0], w_up=w_up[0], a0=a0[0], a_up=a_up[0], g_up=g_up[0], k_k=k_k[0], k_a=k_a[0],
              r_k=r_k[0].reshape(-1))
    n_route = N_GROUPS * (1 + EXPERTS_PER_GROUP)
    mw = dict(ln_in_g=ln_in_g.reshape(1, d), ln_in_b=ln_in_b.reshape(1, d), p_a=p_a[0].astype(BF16),
              p_b=p_b[0].astype(BF16), p_m=p_m[0].astype(BF16), w_o=w_o[0].astype(BF16),
              ln1_g=ln1_g[0].reshape(1, d), ln1_b=ln1_b[0].reshape(1, d),
              w_route=jnp.pad(jnp.concatenate([w_group[0], w_router[0]], axis=1), ((0, 0), (0, LANES - n_route))),
              b_route=jnp.pad(jnp.concatenate([b_group[0], b_router[0]]), (0, LANES - n_route)).reshape(1, LANES),
              e_gate=e_gate, e_up=e_up, e_down=e_down, ln2_g=ln2_g[0], ln2_b=ln2_b[0])

    xp = x_prompt.reshape(bsz * seq, d)
    zq, zg, shift_p, prep = ln_proj_prep(x_prompt, ln_in_g, ln_in_b, w_parts, jnp.zeros((bsz, 1, c_shift), F32), rp,
                                         chunk, 256, BF16)
    o_a, wkv_p = wkv(prep, jnp.zeros((bsz, c_a // HEAD, HEAD, HEAD), F32), lnx_g[0], lnx_b[0], chunk)
    mkv = proj(mem_prompt.reshape(bsz * n_mem, d), w_mem_kv[0].astype(BF16), 256).reshape(bsz, n_mem, 2 * mem_w)
    mk_p, mv_p = mkv[..., :mem_w], mkv[..., mem_w:]
    tables = rope_tables(jnp.arange(seq, dtype=F32))
    o_b, o_m, k_rot = attn_prompt(zq, mk_p, mv_p, sinks[0], tables, window, q_w, kv_w)
    n_all = bsz * seq + dec
    n_buf = -(-n_all // 256) * 256
    routed = merge(xp, o_a.reshape(-1, c_a), o_b.reshape(-1, q_w), o_m.reshape(-1, mem_w),
                   zg.reshape(bsz * seq, -1), mw, 256, alpha, n_buf)
    shift_p = shift_p[:, 0]
    kb_p = k_rot[:, -window:].reshape(bsz, window, H_KV, HEAD)
    vb_p = zq[:, -window:, q_w + kv_w:q_w + 2 * kv_w].reshape(bsz, window, H_KV, HEAD)

    xs = x_sample.reshape(dec, d)
    zq_s, zg_s, zr_s, ops_s = ln_proj_prep(xs.reshape(1, dec, d), ln_in_g, ln_in_b, w_parts,
                                           state_shift[0].reshape(1, dec, c_shift), rp, 1, dec, F32)
    zq_s, zg_s, zr_s = zq_s[0], zg_s[0], zr_s[0]
    o_a_s, wkv_s = wkv_step([a.reshape(dec, c_a) for a in ops_s], state_wkv[0], lnx_g[0], lnx_b[0], 8)
    tables_s = rope_tables(jnp.full((1,), past_len, F32))
    o_b_s, o_m_s, nk_s, nv_s = attn_step(
        zq_s, cache_win_k[0].reshape(dec, window, kv_w), cache_win_v[0].reshape(dec, window, kv_w),
        cache_mem_k[0].reshape(dec, n_mem, mem_w), cache_mem_v[0].reshape(dec, n_mem, mem_w),
        sinks[0], tables_s, q_w, kv_w, 8)
    x1, x1_tiles, eidx, gate = merge(xs, o_a_s, o_b_s, o_m_s, zg_s, mw, dec, alpha, n_buf, bsz * seq, routed)

    y_prompt, y_sample = hier_moe_ln(x1, x1_tiles, eidx, gate, mw, dec, alpha, n_all, bsz * seq)
    y_prompt = y_prompt.reshape(bsz, seq, d)
    y_sample = y_sample.reshape(dec, 1, d)

    sd = state_wkv.dtype
    return (y_prompt, y_sample, wkv_p[None].astype(sd), wkv_s[None].astype(sd), shift_p[None], zr_s[None],
            kb_p[None], vb_p[None], nk_s.reshape(dec, window, H_KV, HEAD)[None],
            nv_s.reshape(dec, window, H_KV, HEAD)[None],
            mk_p.reshape(bsz, n_mem, -1, HEAD)[None], mv_p.reshape(bsz, n_mem, -1, HEAD)[None])
```

```python
import functools
import math

import jax
import jax.numpy as jnp
from jax import lax
from jax.experimental import pallas as pl
from jax.experimental.pallas import tpu as pltpu

F32 = jnp.float32
BF16 = jnp.bfloat16
SCAN_BATCH = 4
ATTN_BLOCKS = 4
CHUNKS_PER_STEP = 4

HEAD = 64
LANES = 128
H_KV = 2
ROT_HALF = 8
ROPE_THETA = 500000.0
PAST_LEN = 8192
N_GROUPS = 4
EXPERTS_PER_GROUP = 8
EXPERT_BLOCK = 128
LN_EPS = 1e-5
LNX_EPS = 64e-5
NEG_INF = -1e30
VMEM_LIMIT = 48 * 1024 * 1024


def _pieces(x, n):
    out = []
    for _ in range(n):
        p = x.astype(BF16)
        out.append(p)
        x = x - p.astype(F32)
    return out


def _mask_dot(x, mask, n=2):
    return sum(jnp.dot(p, mask, preferred_element_type=F32) for p in _pieces(x, n))


def _split3(x, axis, lhs):
    hi = x.astype(BF16).astype(F32)
    lo = x - hi
    return jnp.concatenate([hi, hi, lo] if lhs else [hi, lo, hi], axis=axis).astype(BF16)


def _dot3(a, b):
    return jnp.dot(_split3(a, 1, True), _split3(b, 0, False), preferred_element_type=F32)


def _dot3_t(a, b):
    return lax.dot_general(_split3(a, 1, True), _split3(b, 1, False), (((1,), (1,)), ((), ())),
                           preferred_element_type=F32)


def _bdot(a, b):
    return jnp.dot(a.astype(BF16), b.astype(BF16), preferred_element_type=F32)


def _bdot_t(a, b):
    return lax.dot_general(a.astype(BF16), b.astype(BF16), (((1,), (1,)), ((), ())), preferred_element_type=F32)


def _ln(x, g, b):
    mu = jnp.mean(x, axis=-1, keepdims=True)
    xc = x - mu
    var = jnp.mean(xc * xc, axis=-1, keepdims=True)
    return xc * lax.rsqrt(var + LN_EPS) * g + b


def _const_spec(shape):
    nd = len(shape)
    return pl.BlockSpec(shape, lambda *_: (0,) * nd)


def _params(sem):
    return pltpu.CompilerParams(dimension_semantics=sem, vmem_limit_bytes=VMEM_LIMIT)


def _ln_proj_prep_kernel(chunk, x_ref, g_ref, b_ref, wr_ref, wq_ref, wg_ref, prev_ref, mu_ref, w0_ref, wup_ref,
                         a0_ref, aup_ref, gup_ref, kk_ref, ka_ref, rk_ref, hsum_ref, tril_ref,
                         zq_ref, zg_ref, zr_ref, *refs):
    out_refs, carry_ref = refs[:-1], refs[-1]
    xn = _ln(x_ref[0], g_ref[...], b_ref[...]).astype(BF16)
    z = jnp.dot(xn, wr_ref[...], preferred_element_type=F32)
    zq_ref[0] = jnp.dot(xn, wq_ref[...], preferred_element_type=F32)
    zg_ref[0] = jnp.dot(xn, wg_ref[...], preferred_element_type=F32).astype(zg_ref.dtype)
    tt = z.shape[0]
    c_a = w0_ref.shape[-1]
    r_w, r_a, r_g = wup_ref.shape[0], aup_ref.shape[0], gup_ref.shape[0]
    if chunk == 1:
        zr_ref[0] = z
        prev = prev_ref[0]
    else:
        zr_ref[0] = z[tt - 1:tt, :]

        @pl.when(pl.program_id(1) == 0)
        def _():
            carry_ref[...] = prev_ref[0]

        row = lax.broadcasted_iota(jnp.int32, z.shape, 0)
        prev = jnp.where(row == 0, carry_ref[...], pltpu.roll(z, 1, 0))
        carry_ref[...] = z[tt - 1:tt, :]
    zs = z + (prev - z) * mu_ref[...]
    r = zs[:, :c_a]
    k = zs[:, c_a:2 * c_a]
    v = zs[:, 2 * c_a:3 * c_a]
    o = 3 * c_a
    xw = zs[:, o:o + r_w]
    xa = zs[:, o + r_w:o + r_w + r_a]
    xg = zs[:, o + r_w + r_a:o + r_w + r_a + r_g]
    warg = -(w0_ref[...] + _dot3(jnp.tanh(xw), wup_ref[...]))
    softplus = jnp.maximum(warg, 0.0) + jnp.log1p(jnp.exp(-jnp.abs(warg)))
    lw = -jnp.exp(-softplus - 0.5)
    a = jax.nn.sigmoid(a0_ref[...] + _dot3(xa, aup_ref[...]))
    g = _dot3(jax.nn.sigmoid(xg), gup_ref[...])
    kkr = k * kk_ref[...]
    kk = kkr / jnp.maximum(jnp.sqrt(_mask_dot(kkr * kkr, hsum_ref[...])), 1e-12)
    k2 = k * (1.0 + (a - 1.0) * ka_ref[...])
    bonus = _mask_dot(r * k2 * rk_ref[...], hsum_ref[...]) * v
    kb = kk * a
    if chunk == 1:
        outs = (r, jnp.exp(lw), k2, v, -kk, kb, g, bonus)
        for ref, val in zip(out_refs, outs):
            ref[0] = val
        return
    pieces = _pieces(lw, 3)
    ones3 = jnp.ones((chunk, 3 * chunk), BF16)
    cw, cwl = [], []
    for c in range(tt // chunk):
        stack = jnp.concatenate([p[c * chunk:(c + 1) * chunk] for p in pieces], axis=0)
        cw.append(jnp.dot(tril_ref[...], stack, preferred_element_type=F32))
        cwl.append(jnp.dot(ones3, stack, preferred_element_type=F32))
    cw = jnp.concatenate(cw, axis=0)
    cwl = jnp.concatenate(cwl, axis=0)
    e_inv = jnp.exp(-cw)
    outs = (r * jnp.exp(cw), -kk * jnp.exp(cw - lw), kb * e_inv, k2 * e_inv, v, g, bonus)
    for ref, val in zip(out_refs[:-1], outs):
        ref[0] = val
    wl = jnp.exp(cwl)
    for c in range(tt // chunk):
        out_refs[-1][0, c] = wl[c * chunk:c * chunk + 1, :]


def _head_sum_matrix(width):
    idx = jnp.arange(width)
    return ((idx[:, None] // HEAD) == (idx[None, :] // HEAD)).astype(BF16)


def ln_proj_prep(x, g, b, ws, prev, p, chunk, tt, gate_dtype):
    bsz, t, d = x.shape
    w_r, w_q, w_g = ws
    cs = w_r.shape[1]
    c_a = p['w0'].shape[-1]
    assert prev.shape[1] == (t if chunk == 1 else 1)
    ridx = jnp.arange(chunk)
    tril = jnp.tile((ridx[None, :] <= ridx[:, None]).astype(BF16), (1, 3))
    hsum = _head_sum_matrix(c_a)
    row = lambda a: a.reshape(1, -1)
    rows = lambda width: pl.BlockSpec((1, tt, width), lambda bi, i: (bi, i, 0))
    full = jax.ShapeDtypeStruct((bsz, t, c_a), F32)
    if chunk == 1:
        out_specs, out_shape = [rows(c_a)] * 8, [full] * 8
        prev_spec, zr_spec, zr_rows = rows(cs), rows(cs), t
    else:
        out_specs = [rows(c_a)] * 7 + [pl.BlockSpec((1, tt // chunk, 1, c_a), lambda bi, i: (bi, i, 0, 0))]
        out_shape = [full] * 7 + [jax.ShapeDtypeStruct((bsz, t // chunk, 1, c_a), F32)]
        prev_spec = zr_spec = pl.BlockSpec((1, 1, cs), lambda bi, i: (bi, 0, 0))
        zr_rows = 1
    consts = [row(p['mu']), row(p['w0']), p['w_up'], row(p['a0']), p['a_up'], p['g_up'], row(p['k_k']),
              row(p['k_a']), row(p['r_k']), hsum, tril]
    weight = lambda w: pl.BlockSpec(w.shape, lambda bi, i: (0, 0), pipeline_mode=pl.Buffered(1))
    outs = pl.pallas_call(
        functools.partial(_ln_proj_prep_kernel, chunk), name="ln_proj_prep",
        grid=(bsz, t // tt),
        in_specs=[rows(d), _const_spec((1, d)), _const_spec((1, d)), weight(w_r), weight(w_q), weight(w_g), prev_spec]
        + [_const_spec(c.shape) for c in consts],
        out_specs=[rows(w_q.shape[1]), rows(w_g.shape[1]), zr_spec] + out_specs,
        out_shape=[jax.ShapeDtypeStruct((bsz, t, w_q.shape[1]), F32),
                   jax.ShapeDtypeStruct((bsz, t, w_g.shape[1]), gate_dtype),
                   jax.ShapeDtypeStruct((bsz, zr_rows, cs), F32)] + out_shape,
        scratch_shapes=[pltpu.VMEM((1, cs), F32)],
        compiler_params=_params(("parallel", "arbitrary")),
    )(x, row(g), row(b), w_r, w_q, w_g, prev, *consts)
    return outs[0], outs[1], outs[2], outs[3:]


def _wkv_chunk_kernel(ra_ref, at_ref, bt_ref, kt_ref, v_ref, wl_ref, rp_ref, y0_ref, m_ref, n_ref):
    n_chunks = m_ref.shape[1]
    length = ra_ref.shape[1] // n_chunks
    heads = ra_ref.shape[2] // HEAD
    row = lax.broadcasted_iota(jnp.int32, (length, length), 0)
    col = lax.broadcasted_iota(jnp.int32, (length, length), 1)
    strict = row > col
    incl = row >= col
    hrow = lax.broadcasted_iota(jnp.int32, (HEAD, HEAD), 0)
    hcol = lax.broadcasted_iota(jnp.int32, (HEAD, HEAD), 1)
    units = [(c, h) for c in range(n_chunks) for h in range(heads)]
    us = range(len(units))
    at = lambda ref, u: ref[0, units[u][0] * length:(units[u][0] + 1) * length,
                            units[u][1] * HEAD:(units[u][1] + 1) * HEAD]
    gram = [_bdot_t(jnp.concatenate([at(at_ref, u), at(ra_ref, u)], axis=0),
                    jnp.concatenate([at(bt_ref, u), at(kt_ref, u)], axis=0)) for u in us]
    a_ab = [jnp.where(strict, g[:length, :length], 0.0) for g in gram]
    a_kk = [jnp.concatenate([jnp.where(strict, g[:length, length:], 0.0),
                             jnp.where(incl, g[length:, length:], 0.0)], axis=0) for g in gram]
    a_rb = [jnp.where(incl, g[length:, :length], 0.0) for g in gram]
    inv = [jnp.where(row == col, 1.0, a) for a in a_ab]
    pw = [_bdot(a, a) for a in a_ab]
    avy = [_bdot(a_kk[u], at(v_ref, u)) for u in us]
    w_l = [wl_ref[0, units[u][0], :, units[u][1] * HEAD:(units[u][1] + 1) * HEAD] for u in us]
    nk = [_dot3(at(v_ref, u).T, at(kt_ref, u) * w_l[u]) for u in us]
    for _ in range(int(math.log2(length)) - 2):
        both = [_bdot(jnp.concatenate([pw[u], inv[u]], axis=0), pw[u]) for u in us]
        pw = [b[:length] for b in both]
        inv = [inv[u] + both[u][length:] for u in us]
    inv = [inv[u] + _bdot(inv[u], pw[u]) for u in us]
    pq = [_bdot(inv[u], jnp.concatenate([at(at_ref, u), avy[u][:length]], axis=1)) for u in us]
    ry = [_bdot(a_rb[u], pq[u]) for u in us]
    mn = [_dot3(pq[u].T, at(bt_ref, u) * w_l[u]) for u in us]
    lanes = lambda f, c: jnp.concatenate([f(c * heads + h) for h in range(heads)], axis=1)
    rows = lambda f: jnp.concatenate([lanes(f, c) for c in range(n_chunks)], axis=0)
    rp_ref[0] = rows(lambda u: at(ra_ref, u) + ry[u][:, :HEAD])
    y0_ref[0] = rows(lambda u: avy[u][length:] + ry[u][:, HEAD:])
    for c in range(n_chunks):
        m_ref[0, c] = lanes(lambda u: jnp.where(hrow == hcol, w_l[u], 0.0) + mn[u][:HEAD], c)
        n_ref[0, c] = lanes(lambda u: nk[u] + mn[u][HEAD:], c)


def _head_norm_wide(y, hsum):
    yc = y - _mask_dot(y, hsum) * (1.0 / HEAD)
    return yc * lax.rsqrt(_mask_dot(yc * yc, hsum) * (1.0 / HEAD) + LNX_EPS)


def _wkv_scan_kernel(rp_ref, y0_ref, m_ref, n_ref, g_ref, bonus_ref, s0_ref, lg_ref, lb_ref, hsum_ref,
                     o_ref, sout_ref, s_ref):
    c = pl.program_id(1)
    nb = rp_ref.shape[0]
    heads = rp_ref.shape[2] // HEAD

    @pl.when(c == 0)
    def _():
        s_ref[...] = s0_ref[...]

    pairs = [(b, h, slice(h * HEAD, (h + 1) * HEAD)) for b in range(nb) for h in range(heads)]
    s = [s_ref[b, h] for b, h, _ in pairs]
    s_new = [n_ref[b, 0, :, sl] + _dot3(s[j], m_ref[b, 0, :, sl]) for j, (b, h, sl) in enumerate(pairs)]
    ys = [y0_ref[b, :, sl] + _bdot_t(rp_ref[b, :, sl], s[j]) for j, (b, h, sl) in enumerate(pairs)]
    for j, (b, h, _) in enumerate(pairs):
        s_ref[b, h] = s_new[j]
    length = rp_ref.shape[1]
    y = _head_norm_wide(jnp.concatenate([jnp.concatenate(ys[b * heads:(b + 1) * heads], axis=-1) for b in range(nb)],
                                        axis=0), hsum_ref[...])
    for b in range(nb):
        o = (y[b * length:(b + 1) * length] * lg_ref[...] + lb_ref[...] + bonus_ref[b]) * g_ref[b]
        o_ref[b] = o.astype(o_ref.dtype)

    @pl.when(c == pl.num_programs(1) - 1)
    def _():
        sout_ref[...] = s_ref[...]


def wkv(prep, s0, lnx_g, lnx_b, chunk):
    ra, at, bt, kt, v, g, bonus, wl = prep
    bsz, t, c_a = ra.shape
    heads = c_a // HEAD
    n_chunks = t // chunk
    assert chunk & (chunk - 1) == 0 and chunk >= 4 and t % chunk == 0
    per_step = CHUNKS_PER_STEP if n_chunks % CHUNKS_PER_STEP == 0 else 1
    tile = pl.BlockSpec((1, per_step * chunk, c_a), lambda b, c: (b, c, 0))
    mat = pl.BlockSpec((1, per_step, HEAD, c_a), lambda b, c: (b, c, 0, 0))
    full = jax.ShapeDtypeStruct((bsz, t, c_a), F32)
    mats = jax.ShapeDtypeStruct((bsz, n_chunks, HEAD, c_a), F32)
    rp, y0, m, n = pl.pallas_call(
        _wkv_chunk_kernel, name="wkv_chunk",
        grid=(bsz, n_chunks // per_step),
        in_specs=[tile] * 5 + [pl.BlockSpec((1, per_step, 1, c_a), lambda b, c: (b, c, 0, 0))],
        out_specs=[tile, tile, mat, mat],
        out_shape=[full, full, mats, mats],
        compiler_params=_params(("parallel", "parallel")),
    )(ra, at, bt, kt, v, wl)
    nb = SCAN_BATCH if bsz % SCAN_BATCH == 0 else 1
    tile = pl.BlockSpec((nb, chunk, c_a), lambda b, c: (b, c, 0))
    mat = pl.BlockSpec((nb, 1, HEAD, c_a), lambda b, c: (b, c, 0, 0))
    st = pl.BlockSpec((nb, heads, HEAD, HEAD), lambda b, c: (b, 0, 0, 0))
    return pl.pallas_call(
        _wkv_scan_kernel, name="wkv_scan",
        grid=(bsz // nb, n_chunks),
        in_specs=[tile, tile, mat, mat, tile, tile, st, _const_spec((1, c_a)), _const_spec((1, c_a)),
                  _const_spec((c_a, c_a))],
        out_specs=[tile, st],
        out_shape=[jax.ShapeDtypeStruct((bsz, t, c_a), BF16), jax.ShapeDtypeStruct(s0.shape, F32)],
        scratch_shapes=[pltpu.VMEM((nb, heads, HEAD, HEAD), F32)],
        compiler_params=_params(("parallel", "arbitrary")),
    )(rp, y0, m, n, g, bonus, s0, lnx_g.reshape(1, c_a), lnx_b.reshape(1, c_a), _head_sum_matrix(c_a))


def _wkv_step_kernel(r_ref, w_ref, k_ref, v_ref, a_ref, b_ref, g_ref, bonus_ref, s_ref, lg_ref, lb_ref, hsum_ref,
                     o_ref, sout_ref):
    bt, heads = s_ref.shape[0], s_ref.shape[1]
    c_a = heads * HEAD
    hsum = hsum_ref[...]
    diag = (lax.broadcasted_iota(jnp.int32, (HEAD, c_a), 1) % HEAD
            == lax.broadcasted_iota(jnp.int32, (HEAD, c_a), 0))
    seqs = range(bt)
    row = lambda ref, i: ref[i:i + 1, :]
    stack = lambda f: jnp.concatenate([f(i) for i in seqs], axis=0)
    piece = lambda x, i: x[i * HEAD:(i + 1) * HEAD]
    s = stack(lambda i: jnp.concatenate([s_ref[i, h] for h in range(heads)], axis=1))
    sa = _mask_dot(s * stack(lambda i: jnp.broadcast_to(row(a_ref, i), (HEAD, c_a))), hsum)
    v_rows = _mask_dot(stack(lambda i: jnp.where(diag, row(v_ref, i), 0.0)), hsum)
    s = stack(lambda i: piece(s, i) * row(w_ref, i) + piece(sa, i) * row(b_ref, i) + piece(v_rows, i) * row(k_ref, i))
    for i in seqs:
        for h in range(heads):
            sout_ref[i, h] = piece(s, i)[:, h * HEAD:(h + 1) * HEAD]
    y_rows = _mask_dot(stack(lambda i: piece(s, i) * row(r_ref, i)), hsum)
    y = stack(lambda i: jnp.sum(jnp.where(diag, piece(y_rows, i), 0.0), axis=0, keepdims=True))
    y = _head_norm_wide(y, hsum)
    o_ref[...] = (y * lg_ref[...] + lb_ref[...] + bonus_ref[...]) * g_ref[...]


def wkv_step(ops, s0, lnx_g, lnx_b, bt):
    bsz, c_a = ops[0].shape
    rows = pl.BlockSpec((bt, c_a), lambda i: (i, 0))
    st = pl.BlockSpec((bt,) + s0.shape[1:], lambda i: (i, 0, 0, 0))
    hsum = _head_sum_matrix(c_a)
    return pl.pallas_call(
        _wkv_step_kernel, name="wkv_step",
        grid=(bsz // bt,),
        in_specs=[rows] * 8 + [st, _const_spec((1, c_a)), _const_spec((1, c_a)), _const_spec(hsum.shape)],
        out_specs=[rows, st],
        out_shape=[jax.ShapeDtypeStruct((bsz, c_a), F32), jax.ShapeDtypeStruct(s0.shape, F32)],
        compiler_params=_params(("parallel",)),
    )(*ops, s0, lnx_g.reshape(1, c_a), lnx_b.reshape(1, c_a), hsum)


def rope_tables(pos):
    inv_freq = ROPE_THETA ** (-jnp.arange(ROT_HALF, dtype=F32) / ROT_HALF)
    ang = pos[:, None] * inv_freq[None, :]
    cos, sin = jnp.cos(ang), jnp.sin(ang)
    t = pos.shape[0]
    rest = HEAD - 2 * ROT_HALF
    c = jnp.concatenate([cos, cos, jnp.ones((t, rest), F32)], axis=1)
    s1 = jnp.concatenate([jnp.zeros((t, ROT_HALF), F32), sin, jnp.zeros((t, rest), F32)], axis=1)
    s2 = jnp.concatenate([-sin, jnp.zeros((t, HEAD - ROT_HALF), F32)], axis=1)
    rep = LANES // HEAD
    return tuple(jnp.tile(x, (1, rep)) for x in (c, s1, s2))


def _rope(x, c, s1, s2):
    width = x.shape[-1]
    rep = width // c.shape[-1]
    if rep > 1:
        c, s1, s2 = (jnp.concatenate([t] * rep, axis=1) for t in (c, s1, s2))
    return x * c + pltpu.roll(x, ROT_HALF, 1) * s1 + pltpu.roll(x, width - ROT_HALF, 1) * s2


def _attn_kernel(window, q_w, kv_w, zq_ref, kp_ref, vp_ref, rc_ref, rs1_ref, rs2_ref, pc_ref, ps1_ref, ps2_ref,
                 mk_ref, mv_ref, sink_ref, ob_ref, om_ref, kr_ref):
    zq = zq_ref[0]
    nblk = zq.shape[0] // window
    first = pl.program_id(1) * nblk
    scale = HEAD ** -0.5
    q = zq[:, :q_w]
    k = zq[:, q_w:q_w + kv_w]
    v = zq[:, q_w + kv_w:q_w + 2 * kv_w]
    qm = zq[:, q_w + 2 * kv_w:]
    qr = _rope(q, rc_ref[...], rs1_ref[...], rs2_ref[...]) * scale
    kr = _rope(k, rc_ref[...], rs1_ref[...], rs2_ref[...])
    kr_ref[0] = kr
    k_all = jnp.concatenate([_rope(kp_ref[0], pc_ref[...], ps1_ref[...], ps2_ref[...]), kr], axis=0)
    v_all = jnp.concatenate([vp_ref[0], v], axis=0)
    gqa = q_w // kv_w
    assert window & (window - 1) == 0
    qi = lax.broadcasted_iota(jnp.int32, (gqa * window, 2 * window), 0) & (window - 1)
    kj = lax.broadcasted_iota(jnp.int32, (gqa * window, 2 * window), 1)
    band = (kj > qi) & (kj <= qi + window)
    tdot = lambda a, b: lax.dot_general(a, b, (((1,), (1,)), ((), ())), preferred_element_type=F32)
    dot = lambda a, b: jnp.dot(a, b, preferred_element_type=F32)
    hsl = lambda h: slice(h * HEAD, (h + 1) * HEAD)
    rows = lambda i: slice(i * window, (i + 1) * window)
    keys = lambda i: slice(i * window, (i + 2) * window)
    wins = [(i, hk) for i in range(nblk) for hk in range(kv_w // HEAD)]
    mems = [(i, h) for i in range(nblk) for h in range(qm.shape[1] // HEAD)]
    qmb = (qm * scale).astype(BF16)
    mk = mk_ref[0].astype(BF16)
    mv = mv_ref[0].astype(BF16)
    kcat = [k_all[keys(i), hsl(hk)].astype(BF16) for i, hk in wins]
    vcat = [v_all[keys(i), hsl(hk)].astype(BF16) for i, hk in wins]
    qs = [jnp.concatenate([qr[rows(i), hsl(hk * gqa + g)] for g in range(gqa)], axis=0).astype(BF16)
          for i, hk in wins]
    s = [jnp.where(band & ((kj >= window) | (first + i > 0)), tdot(qs[u], kcat[u]), NEG_INF)
         for u, (i, hk) in enumerate(wins)]
    sm = [tdot(qmb[rows(i), hsl(h)], mk[:, hsl(h)]) for i, h in mems]
    sink = [jnp.concatenate([jnp.full((window, 1), sink_ref[hk * gqa + g], F32) for g in range(gqa)], axis=0)
            for i, hk in wins]
    m = [jnp.maximum(jnp.max(s[u], axis=-1, keepdims=True), sink[u]) for u in range(len(wins))]
    p = [jnp.exp(s[u] - m[u]) for u in range(len(wins))]
    pm = [jnp.exp(x - jnp.max(x, axis=-1, keepdims=True)) for x in sm]
    o = [dot(p[u].astype(BF16), vcat[u])
         / (jnp.sum(p[u], axis=-1, keepdims=True) + jnp.exp(sink[u] - m[u])) for u in range(len(wins))]
    om = [dot(pm[u].astype(BF16), mv[:, hsl(h)]) / jnp.sum(pm[u], axis=-1, keepdims=True)
          for u, (i, h) in enumerate(mems)]
    n_kv, n_mh = kv_w // HEAD, qm.shape[1] // HEAD
    ob = jnp.concatenate([jnp.concatenate([o[i * n_kv + hk][g * window:(g + 1) * window]
                                           for hk in range(n_kv) for g in range(gqa)], axis=1)
                          for i in range(nblk)], axis=0)
    ob_ref[0] = ob.astype(ob_ref.dtype)
    om_ref[0] = jnp.concatenate([jnp.concatenate(om[i * n_mh:(i + 1) * n_mh], axis=1) for i in range(nblk)],
                                axis=0).astype(om_ref.dtype)


def attn_prompt(zq, mk, mv, sinks, tables, window, q_w, kv_w):
    bsz, t, zw = zq.shape
    mem_w = zw - q_w - 2 * kv_w
    assert kv_w == LANES and q_w % kv_w == 0
    kcol, vcol = q_w // kv_w, q_w // kv_w + 1
    nblk = ATTN_BLOCKS if (t // window) % ATTN_BLOCKS == 0 else 1
    rows = nblk * window
    prev = lambda n: jnp.maximum(n * nblk - 1, 0)
    tab = pl.BlockSpec((rows, LANES), lambda b, n: (n, 0))
    ptab = pl.BlockSpec((window, LANES), lambda b, n: (prev(n), 0))
    mem = pl.BlockSpec((1,) + mk.shape[1:], lambda b, n: (b, 0, 0))
    return pl.pallas_call(
        functools.partial(_attn_kernel, window, q_w, kv_w), name="attn",
        grid=(bsz, t // rows),
        in_specs=[pl.BlockSpec((1, rows, zw), lambda b, n: (b, n, 0)),
                  pl.BlockSpec((1, window, kv_w), lambda b, n: (b, prev(n), kcol)),
                  pl.BlockSpec((1, window, kv_w), lambda b, n: (b, prev(n), vcol)),
                  tab, tab, tab, ptab, ptab, ptab, mem, mem,
                  pl.BlockSpec(memory_space=pltpu.SMEM)],
        out_specs=[pl.BlockSpec((1, rows, q_w), lambda b, n: (b, n, 0)),
                   pl.BlockSpec((1, rows, mem_w), lambda b, n: (b, n, 0)),
                   pl.BlockSpec((1, rows, kv_w), lambda b, n: (b, n, 0))],
        out_shape=[jax.ShapeDtypeStruct((bsz, t, q_w), BF16), jax.ShapeDtypeStruct((bsz, t, mem_w), BF16),
                   jax.ShapeDtypeStruct((bsz, t, kv_w), F32)],
        compiler_params=_params(("parallel", "parallel")),
    )(zq, zq, zq, *tables, *tables, mk, mv, sinks)


def _attn_step_kernel(q_w, kv_w, zq_ref, ck_ref, cv_ref, mk_ref, mv_ref, rc_ref, rs1_ref, rs2_ref, sink_ref,
                      ob_ref, om_ref, nk_ref, nv_ref):
    bt = zq_ref.shape[0]
    window = ck_ref.shape[1]
    mem_w = om_ref.shape[1]
    n_q, gqa, per_vreg = q_w // HEAD, q_w // kv_w, LANES // HEAD
    scale = HEAD ** -0.5
    zq = zq_ref[...]
    q = _rope(zq[:, :q_w], rc_ref[...], rs1_ref[...], rs2_ref[...]) * scale
    k_new = _rope(zq[:, q_w:q_w + kv_w], rc_ref[...], rs1_ref[...], rs2_ref[...])
    v_new = zq[:, q_w + kv_w:q_w + 2 * kv_w]
    qm = zq[:, q_w + 2 * kv_w:] * scale
    own = lambda w: (lax.broadcasted_iota(jnp.int32, (n_q, w), 1) // HEAD
                     == lax.broadcasted_iota(jnp.int32, (n_q, w), 0))
    own_q, own_m = own(q_w), own(mem_w)
    hrow = lax.broadcasted_iota(jnp.int32, (n_q, LANES), 0)
    hblk = lax.broadcasted_iota(jnp.int32, (n_q, LANES), 1) // HEAD
    swap = (hrow % per_vreg) != (hrow // gqa)
    keep = hblk == hrow % per_vreg
    key_ok = lax.broadcasted_iota(jnp.int32, (n_q, window), 1) >= 1
    wrow = lax.broadcasted_iota(jnp.int32, (window, kv_w), 0)
    sink = sink_ref[...]
    bs = range(bt)
    tdot = lambda a, b: lax.dot_general(a.astype(BF16), b.astype(BF16), (((1,), (1,)), ((), ())),
                                        preferred_element_type=F32)
    for b in bs:
        nk_ref[b] = jnp.where(wrow == window - 1, k_new[b:b + 1], pltpu.roll(ck_ref[b], window - 1, 0))
        nv_ref[b] = jnp.where(wrow == window - 1, v_new[b:b + 1], pltpu.roll(cv_ref[b], window - 1, 0))
    q8 = []
    for b in bs:
        rep = jnp.where(own_q, q[b:b + 1], 0.0)
        fold = sum(rep[:, c * LANES:(c + 1) * LANES] for c in range(q_w // LANES))
        q8.append(jnp.where(swap, pltpu.roll(fold, HEAD, 1), fold))
    qm8 = [jnp.where(own_m, qm[b:b + 1], 0.0) for b in bs]
    s = [jnp.where(key_ok, tdot(q8[b], ck_ref[b]), NEG_INF) for b in bs]
    sm = [tdot(qm8[b], mk_ref[b]) for b in bs]
    s_new = [jnp.sum(q8[b] * k_new[b:b + 1], axis=1, keepdims=True) for b in bs]
    m = [jnp.maximum(jnp.maximum(jnp.max(s[b], axis=1, keepdims=True), s_new[b]), sink) for b in bs]
    p = [jnp.exp(s[b] - m[b]) for b in bs]
    p_new = [jnp.exp(s_new[b] - m[b]) for b in bs]
    pm = [jnp.exp(sm[b] - jnp.max(sm[b], axis=1, keepdims=True)) for b in bs]
    o8 = [(_bdot(p[b], cv_ref[b]) + p_new[b] * v_new[b:b + 1])
          / (jnp.sum(p[b], axis=1, keepdims=True) + p_new[b] + jnp.exp(sink - m[b])) for b in bs]
    om8 = [_bdot(pm[b], mv_ref[b]) / jnp.sum(pm[b], axis=1, keepdims=True) for b in bs]
    ob, om = [], []
    for b in bs:
        o = jnp.where(swap, pltpu.roll(o8[b], HEAD, 1), o8[b])
        o = jnp.concatenate([jnp.where(keep, o, 0.0)] * (q_w // LANES), axis=1)
        ob.append(jnp.sum(jnp.where(own_q, o, 0.0), axis=0, keepdims=True))
        om.append(jnp.sum(jnp.where(own_m, om8[b], 0.0), axis=0, keepdims=True))
    ob_ref[...] = jnp.concatenate(ob, axis=0)
    om_ref[...] = jnp.concatenate(om, axis=0)


def attn_step(zq, ck, cv, mk, mv, sinks, tables, q_w, kv_w, bt):
    bsz, zw = zq.shape
    mem_w = zw - q_w - 2 * kv_w
    window, n_mem = ck.shape[1], mk.shape[1]
    n_q = q_w // HEAD
    assert kv_w == LANES and mem_w // HEAD <= n_q and n_q == 8
    rows = lambda w: pl.BlockSpec((bt, w), lambda i: (i, 0))
    cache = pl.BlockSpec((bt, window, kv_w), lambda i: (i, 0, 0))
    mem = pl.BlockSpec((bt, n_mem, mem_w), lambda i: (i, 0, 0))
    consts = list(tables) + [sinks.reshape(n_q, 1)]
    const_specs = [_const_spec(c.shape) for c in consts]
    return pl.pallas_call(
        functools.partial(_attn_step_kernel, q_w, kv_w), name="attn_step",
        grid=(bsz // bt,),
        in_specs=[rows(zw), cache, cache, mem, mem] + const_specs,
        out_specs=[rows(q_w), rows(mem_w), cache, cache],
        out_shape=[jax.ShapeDtypeStruct((bsz, q_w), F32), jax.ShapeDtypeStruct((bsz, mem_w), F32),
                   jax.ShapeDtypeStruct(ck.shape, F32), jax.ShapeDtypeStruct(cv.shape, F32)],
        compiler_params=_params(("parallel",)),
    )(zq, ck, cv, mk, mv, *consts)


def _proj_kernel(x_ref, w_ref, o_ref):
    o_ref[...] = jnp.dot(x_ref[...].astype(BF16), w_ref[...], preferred_element_type=F32)


def proj(x, w, tm):
    n, d = x.shape
    return pl.pallas_call(
        _proj_kernel, name="proj",
        grid=(n // tm,),
        in_specs=[pl.BlockSpec((tm, d), lambda i: (i, 0)), _const_spec(w.shape)],
        out_specs=pl.BlockSpec((tm, w.shape[1]), lambda i: (i, 0)),
        out_shape=jax.ShapeDtypeStruct((n, w.shape[1]), F32),
        compiler_params=_params(("parallel",)),
    )(x, w)


def _merge_kernel(alpha, n_tiles, x_ref, oa_ref, ob_ref, om_ref, gt_ref, lig_ref, lib_ref, pa_ref, pb_ref, pm_ref,
                  wo_ref, l1g_ref, l1b_ref, wr_ref, br_ref, *refs):
    outs = refs[-4:]

    @pl.when(pl.program_id(0) >= n_tiles)
    def _():
        for ref in outs:
            ref[...] = jnp.zeros_like(ref)

    @pl.when(pl.program_id(0) < n_tiles)
    def _():
        _merge_tile(alpha, x_ref, oa_ref, ob_ref, om_ref, gt_ref, lig_ref, lib_ref, pa_ref, pb_ref, pm_ref, wo_ref,
                    l1g_ref, l1b_ref, wr_ref, br_ref, *outs)


def _merge_tile(alpha, x_ref, oa_ref, ob_ref, om_ref, gt_ref, lig_ref, lib_ref, pa_ref, pb_ref, pm_ref, wo_ref,
                l1g_ref, l1b_ref, wr_ref, br_ref, x1_ref, x1t_ref, eidx_ref, gate_ref):
    d = x_ref.shape[1]
    xn = _ln(x_ref[...], lig_ref[...], lib_ref[...])
    gts = jax.nn.sigmoid(gt_ref[...].astype(F32))
    merged = (gts[:, :d] * _bdot(oa_ref[...], pa_ref[...]) + gts[:, d:2 * d] * _bdot(ob_ref[...], pb_ref[...])
              + gts[:, 2 * d:] * _bdot(om_ref[...], pm_ref[...]))
    x1 = _ln(alpha * xn + _bdot(merged, wo_ref[...]), l1g_ref[...], l1b_ref[...])
    x1_ref[...] = x1
    _rows_to_tiles(x1t_ref, x1)
    x_hi, x_lo = _pieces(x1, 2)
    w_hi, w_lo = _pieces(wr_ref[...], 2)
    dot = lambda a, b: jnp.dot(a, b, preferred_element_type=F32)
    logits = dot(x_hi, w_hi) + dot(x_hi, w_lo) + dot(x_lo, w_hi) + br_ref[...]
    lane = lax.broadcasted_iota(jnp.int32, logits.shape, 1)
    lane_f = lane.astype(F32)
    first = lambda hit: jnp.min(jnp.where(hit, lane_f, float(LANES)), axis=-1, keepdims=True).astype(jnp.int32)
    gmask = lane < N_GROUPS
    gl = jnp.where(gmask, logits, NEG_INF)
    gmax = jnp.max(gl, axis=-1, keepdims=True)
    gidx = first(gl == gmax)
    g_w = 1.0 / jnp.sum(jnp.where(gmask, jnp.exp(gl - gmax), 0.0), axis=-1, keepdims=True)
    lo = N_GROUPS + gidx * EXPERTS_PER_GROUP
    el = jnp.where((lane >= lo) & (lane < lo + EXPERTS_PER_GROUP), logits, NEG_INF)
    v1 = jnp.max(el, axis=-1, keepdims=True)
    i1 = first(el == v1)
    el2 = jnp.where(lane == i1, NEG_INF, el)
    v2 = jnp.max(el2, axis=-1, keepdims=True)
    i2 = first(el2 == v2)
    e2 = jnp.exp(v2 - v1)
    gate1 = g_w / (1.0 + e2)
    eidx_ref[...] = jnp.where(lane == 0, i1 - N_GROUPS, jnp.where(lane == 1, i2 - N_GROUPS, 0))
    gate_ref[...] = jnp.where(lane == 0, gate1, jnp.where(lane == 1, gate1 * e2, 0.0))


def merge(x, oa, ob, om, gt, w, tm, alpha, n_total, row_offset=0, into=None):
    n, d = x.shape
    assert row_offset % tm == 0 and n % tm == 0 and n_total % tm == 0
    off = row_offset // tm
    into = list(into or [])
    n_tiles = n // tm
    steps = n_tiles if into else n_total // tm
    rows = lambda a: pl.BlockSpec((tm, a.shape[1]), lambda i: (jnp.minimum(i, n_tiles - 1), 0))
    out = lambda width: pl.BlockSpec((tm, width), lambda i: (i + off, 0))
    consts = [w['ln_in_g'], w['ln_in_b'], w['p_a'], w['p_b'], w['p_m'], w['w_o'], w['ln1_g'], w['ln1_b'],
              w['w_route'], w['b_route']]
    n_in = 5 + len(consts)
    return pl.pallas_call(
        functools.partial(_merge_kernel, alpha, n_tiles), name="merge",
        grid=(steps,),
        in_specs=[rows(a) for a in (x, oa, ob, om, gt)] + [_const_spec(c.shape) for c in consts]
        + [pl.BlockSpec(memory_space=pl.ANY)] * len(into),
        out_specs=[out(d), pl.BlockSpec((tm * TILE_ROWS, LANES), lambda i: (i + off, 0)), out(LANES), out(LANES)],
        out_shape=[jax.ShapeDtypeStruct((n_total, d), F32), jax.ShapeDtypeStruct((n_total * TILE_ROWS, LANES), F32),
                   jax.ShapeDtypeStruct((n_total, LANES), jnp.int32), jax.ShapeDtypeStruct((n_total, LANES), F32)],
        input_output_aliases={n_in + k: k for k in range(len(into))},
        compiler_params=_params(("parallel",)),
    )(x, oa, ob, om, gt, *consts, *into)


ROW_DMA_UNROLL = 8
DRAIN_STEPS = 2


TILE_ROWS = 8


def _rows_from_tiles(ref, n):
    return jnp.concatenate([ref[pl.ds(s, n, stride=TILE_ROWS), :] for s in range(TILE_ROWS)], axis=1)


def _rows_to_tiles(ref, x):
    for s in range(TILE_ROWS):
        ref[pl.ds(s, x.shape[0], stride=TILE_ROWS), :] = x[:, s * LANES:(s + 1) * LANES]


def _row_copies(asg_ref, base, count, n_asg, x_hbm, buf, y_hbm, sem, gather, unrolled):
    def tile(ref, idx):
        start = idx * TILE_ROWS
        return ref.at[pl.ds(start if isinstance(idx, int) else pl.multiple_of(start, TILE_ROWS), TILE_ROWS)]

    def one(r, priority):
        a = asg_ref[base + r]
        if gather:
            tok = jnp.minimum(a, n_asg - 1)
            tok = jnp.where(tok >= n_asg // 2, tok - n_asg // 2, tok)
            copy = pltpu.make_async_copy(tile(x_hbm, tok), tile(buf, r), sem)
        else:
            copy = pltpu.make_async_copy(tile(buf, r), tile(y_hbm, a), sem)
        copy.start(priority=priority)

    if unrolled:
        for r in range(count):
            one(r, r % 2)
        return

    def body(g, carry):
        for j in range(ROW_DMA_UNROLL):
            one(g * ROW_DMA_UNROLL + j, j % 2)
        return carry
    lax.fori_loop(0, count // ROW_DMA_UNROLL, body, 0)


def _moe_expert_kernel(n_asg, asg_ref, be_ref, nu_ref, x_hbm, wg_ref, wu_ref, wd_ref, y_hbm,
                       xbuf, ybuf, wgb, wub, wdb, gsem, ssem):
    i = pl.program_id(0)
    used = nu_ref[0]
    rows = xbuf.shape[1] // TILE_ROWS
    n_blocks = be_ref.shape[0]
    blk = jnp.minimum(i, n_blocks - 1)

    @pl.when((i == 0) | (be_ref[blk] != be_ref[jnp.maximum(blk - 1, 0)]))
    def _():
        wgb[...] = wg_ref[0, 0].astype(BF16)
        wub[...] = wu_ref[0, 0].astype(BF16)
        wdb[...] = wd_ref[0, 0].astype(BF16)

    def wait_gather(slot):
        pltpu.make_async_copy(x_hbm.at[pl.ds(0, rows * TILE_ROWS)], xbuf.at[slot], gsem.at[slot]).wait()

    def wait_scatter(slot):
        pltpu.make_async_copy(ybuf.at[slot], y_hbm.at[pl.ds(0, rows * TILE_ROWS)], ssem.at[slot]).wait()

    def gather(b, slot, unrolled):
        _row_copies(asg_ref, b * rows, rows, n_asg, x_hbm, xbuf.at[slot], y_hbm, gsem.at[slot], True, unrolled)

    def scatter(b, slot, unrolled):
        _row_copies(asg_ref, b * rows, rows, n_asg, x_hbm, ybuf.at[slot], y_hbm, ssem.at[slot], False, unrolled)

    def expert(slot):
        xb = _rows_from_tiles(xbuf.at[slot], rows).astype(BF16)
        hg = jnp.dot(xb, wgb[...], preferred_element_type=F32)
        hu = jnp.dot(xb, wub[...], preferred_element_type=F32)
        h = hg * jax.nn.sigmoid(hg) * hu
        _rows_to_tiles(ybuf.at[slot], jnp.dot(h.astype(BF16), wdb[...], preferred_element_type=F32))

    @pl.when(i == 0)
    def _():
        ybuf[1] = jnp.zeros(ybuf.shape[1:], F32)
        spare = [pltpu.make_async_copy(ybuf.at[1],
                                       y_hbm.at[pl.ds((n_asg + j * rows) * TILE_ROWS, rows * TILE_ROWS)], ssem.at[1])
                 for j in range((y_hbm.shape[0] // TILE_ROWS - n_asg) // rows)]
        for copy in spare:
            copy.start()
        for copy in spare:
            copy.wait()

    @pl.when((i >= 2) & (i - 2 < used))
    def _():
        wait_scatter(i % 2)

    steady = (i >= 1) & (i + 1 < used)
    for slot in range(2):
        @pl.when(steady & (i % 2 == slot))
        def _():
            wait_gather(slot)
            gather(i + 1, 1 - slot, True)
            scatter(i - 1, 1 - slot, True)
            expert(slot)

    @pl.when(jnp.logical_not(steady))
    def _():
        slot = i % 2

        @pl.when((i == 0) & (used > 0))
        def _():
            gather(0, 0, False)

        @pl.when(i < used)
        def _():
            wait_gather(slot)

        @pl.when(i + 1 < used)
        def _():
            gather(i + 1, 1 - slot, False)

        @pl.when((i >= 1) & (i - 1 < used))
        def _():
            scatter(i - 1, 1 - slot, False)

        @pl.when(i < used)
        def _():
            expert(slot)


def moe_experts(x1_tiles, n_asg, asg, blk_e, n_used, e_gate, e_up, e_down):
    d = e_gate.shape[2]
    assert d == TILE_ROWS * LANES and x1_tiles.shape[1] == LANES
    n_blocks = blk_e.shape[0]
    ff = e_gate.shape[-1]
    n_rows = n_asg + e_gate.shape[1] * EXPERT_BLOCK
    weight = lambda shape: pl.BlockSpec(
        (1, 1) + shape, lambda i, asg, be, nu: (0, be[jnp.minimum(i, n_blocks - 1)], 0, 0))
    return pl.pallas_call(
        functools.partial(_moe_expert_kernel, n_asg), name="moe_expert",
        grid_spec=pltpu.PrefetchScalarGridSpec(
            num_scalar_prefetch=3,
            grid=(n_blocks + DRAIN_STEPS,),
            in_specs=[pl.BlockSpec(memory_space=pl.ANY), weight((d, ff)), weight((d, ff)), weight((ff, d))],
            out_specs=pl.BlockSpec(memory_space=pl.ANY),
            scratch_shapes=[pltpu.VMEM((2, EXPERT_BLOCK * TILE_ROWS, LANES), F32),
                            pltpu.VMEM((2, EXPERT_BLOCK * TILE_ROWS, LANES), F32),
                            pltpu.VMEM((d, ff), BF16), pltpu.VMEM((d, ff), BF16), pltpu.VMEM((ff, d), BF16),
                            pltpu.SemaphoreType.DMA((2,)), pltpu.SemaphoreType.DMA((2,))]),
        out_shape=jax.ShapeDtypeStruct((n_rows * TILE_ROWS, LANES), F32),
        compiler_params=_params(("arbitrary",)),
    )(asg, blk_e, n_used, x1_tiles, e_gate, e_up, e_down)


def _moe_combine_kernel(alpha, lead_tiles, y0_ref, y1_ref, x1_ref, gate_ref, g_ref, b_ref, lead_ref, tail_ref):
    i = pl.program_id(0)
    gate = gate_ref[...]
    tm = x1_ref.shape[0]
    moe = gate[:, 0:1] * _rows_from_tiles(y0_ref, tm) + gate[:, 1:2] * _rows_from_tiles(y1_ref, tm)
    out = _ln(alpha * x1_ref[...] + moe, g_ref[...], b_ref[...])

    @pl.when(i < lead_tiles)
    def _():
        lead_ref[...] = out

    @pl.when(i >= lead_tiles)
    def _():
        tail_ref[...] = out


def moe_combine(y, x1, gate, g, b, tm, alpha, n, n_lead):
    d = x1.shape[1]
    assert n_lead % tm == 0 and (n - n_lead) % tm == 0 and n > n_lead
    lead_tiles = n_lead // tm
    return pl.pallas_call(
        functools.partial(_moe_combine_kernel, alpha, lead_tiles), name="moe_combine",
        grid=(n // tm,),
        in_specs=[pl.BlockSpec((tm * TILE_ROWS, LANES), lambda i: (i, 0)),
                  pl.BlockSpec((tm * TILE_ROWS, LANES), lambda i: (i + n // tm, 0)),
                  pl.BlockSpec((tm, d), lambda i: (i, 0)),
                  pl.BlockSpec((tm, LANES), lambda i: (i, 0)), _const_spec((1, d)), _const_spec((1, d))],
        out_specs=[pl.BlockSpec((tm, d), lambda i: (jnp.minimum(i, lead_tiles - 1), 0)),
                   pl.BlockSpec((tm, d), lambda i: (jnp.maximum(i - lead_tiles, 0), 0))],
        out_shape=[jax.ShapeDtypeStruct((n_lead, d), F32), jax.ShapeDtypeStruct((n - n_lead, d), F32)],
        compiler_params=_params(("arbitrary",)),
    )(y, y, x1, gate, g.reshape(1, d), b.reshape(1, d))


def moe_routing(experts):
    n, top_k = experts.shape
    n_exp = N_GROUPS * EXPERTS_PER_GROUP
    a = n * top_k
    flat_e = experts.T.reshape(a)
    order = jnp.argsort(flat_e, stable=True).astype(jnp.int32)
    counts = jnp.sum((flat_e[:, None] == jnp.arange(n_exp, dtype=jnp.int32)[None, :]).astype(jnp.int32), axis=0)
    ends = jnp.cumsum(counts)
    padded = (counts + EXPERT_BLOCK - 1) // EXPERT_BLOCK * EXPERT_BLOCK
    pad_end = jnp.cumsum(padded)
    n_blocks = -(-a // EXPERT_BLOCK) + n_exp
    blk_start = jnp.arange(n_blocks, dtype=jnp.int32) * EXPERT_BLOCK
    blk_e = jnp.minimum(jnp.sum((pad_end[None, :] <= blk_start[:, None]).astype(jnp.int32), axis=1), n_exp - 1)
    slot = jnp.arange(n_blocks * EXPERT_BLOCK, dtype=jnp.int32)
    slot_e = jnp.repeat(blk_e, EXPERT_BLOCK)
    rank = slot - (pad_end - padded)[slot_e]
    spare = a + jnp.clip(slot - ends[slot_e], 0, n_exp * EXPERT_BLOCK - 1)
    asg = jnp.where(rank < counts[slot_e], order[jnp.clip((ends - counts)[slot_e] + rank, 0, a - 1)], spare)
    n_used = (pad_end[-1:] // EXPERT_BLOCK).astype(jnp.int32)
    return asg.astype(jnp.int32), blk_e.astype(jnp.int32), n_used


def hier_moe_ln(x1, x1_tiles, eidx, gate, w, tm, alpha, n_tokens, n_lead):
    asg, blk_e, n_used = moe_routing(eidx[:n_tokens, :2])
    y = moe_experts(x1_tiles, 2 * n_tokens, asg, blk_e, n_used, w['e_gate'], w['e_up'], w['e_down'])
    return moe_combine(y, x1, gate, w['ln2_g'], w['ln2_b'], tm, alpha, n_tokens, n_lead)


def kernel(x_prompt, x_sample, mem_prompt, state_wkv, state_shift, cache_win_k, cache_win_v, cache_mem_k, cache_mem_v, ln_in_g, ln_in_b, w_in, mu, w0, w_up, a0, a_up, g_up, k_k, k_a, r_k, lnx_g, lnx_b, sinks, w_mem_kv, p_a, p_b, p_m, w_o, ln1_g, ln1_b, w_group, b_group, w_router, b_router, e_gate, e_up, e_down, ln2_g, ln2_b):
    depth = w_in.shape[0]
    assert depth == 1, "single-layer step"
    bsz, seq, d = x_prompt.shape
    dec = x_sample.shape[0]
    assert x_sample.shape[1] == 1
    c_shift = mu.shape[-1]
    c_a = w0.shape[-1]
    window, kv_w = cache_win_k.shape[2], cache_win_k.shape[3] * cache_win_k.shape[4]
    n_mem, mem_w = cache_mem_k.shape[2], cache_mem_k.shape[3] * cache_mem_k.shape[4]
    q_w = sinks.shape[-1] * HEAD
    qkvm_w = q_w + 2 * kv_w + mem_w
    alpha = (2.0 * depth) ** 0.25
    past_len = float(PAST_LEN)
    chunk = 64

    w_in_b = w_in[0].astype(BF16)
    w_parts = [w_in_b[:, :c_shift], w_in_b[:, c_shift:c_shift + qkvm_w], w_in_b[:, c_shift + qkvm_w:]]
    rp = dict(mu=mu[0], w0=w0[0], w_up=w_up[0], a0=a0[0], a_up=a_up[0], g_up=g_up[0], k_k=k_k[0], k_a=k_a[0],
              r_k=r_k[0].reshape(-1))
    n_route = N_GROUPS * (1 + EXPERTS_PER_GROUP)
    mw = dict(ln_in_g=ln_in_g.reshape(1, d), ln_in_b=ln_in_b.reshape(1, d), p_a=p_a[0].astype(BF16),
              p_b=p_b[0].astype(BF16), p_m=p_m[0].astype(BF16), w_o=w_o[0].astype(BF16),
              ln1_g=ln1_g[0].reshape(1, d), ln1_b=ln1_b[0].reshape(1, d),
              w_route=jnp.pad(jnp.concatenate([w_group[0], w_router[0]], axis=1), ((0, 0), (0, LANES - n_route))),
              b_route=jnp.pad(jnp.concatenate([b_group[0], b_router[0]]), (0, LANES - n_route)).reshape(1, LANES),
              e_gate=e_gate, e_up=e_up, e_down=e_down, ln2_g=ln2_g[0], ln2_b=ln2_b[0])

    xp = x_prompt.reshape(bsz * seq, d)
    zq, zg, shift_p, prep = ln_proj_prep(x_prompt, ln_in_g, ln_in_b, w_parts, jnp.zeros((bsz, 1, c_shift), F32), rp,
                                         chunk, 256, BF16)
    o_a, wkv_p = wkv(prep, jnp.zeros((bsz, c_a // HEAD, HEAD, HEAD), F32), lnx_g[0], lnx_b[0], chunk)
    mkv = proj(mem_prompt.reshape(bsz * n_mem, d), w_mem_kv[0].astype(BF16), 256).reshape(bsz, n_mem, 2 * mem_w)
    mk_p, mv_p = mkv[..., :mem_w], mkv[..., mem_w:]
    tables = rope_tables(jnp.arange(seq, dtype=F32))
    o_b, o_m, k_rot = attn_prompt(zq, mk_p, mv_p, sinks[0], tables, window, q_w, kv_w)
    n_all = bsz * seq + dec
    n_buf = -(-n_all // 256) * 256
    routed = merge(xp, o_a.reshape(-1, c_a), o_b.reshape(-1, q_w), o_m.reshape(-1, mem_w),
                   zg.reshape(bsz * seq, -1), mw, 256, alpha, n_buf)
    shift_p = shift_p[:, 0]
    kb_p = k_rot[:, -window:].reshape(bsz, window, H_KV, HEAD)
    vb_p = zq[:, -window:, q_w + kv_w:q_w + 2 * kv_w].reshape(bsz, window, H_KV, HEAD)

    xs = x_sample.reshape(dec, d)
    zq_s, zg_s, zr_s, ops_s = ln_proj_prep(xs.reshape(1, dec, d), ln_in_g, ln_in_b, w_parts,
                                           state_shift[0].reshape(1, dec, c_shift), rp, 1, dec, F32)
    zq_s, zg_s, zr_s = zq_s[0], zg_s[0], zr_s[0]
    o_a_s, wkv_s = wkv_step([a.reshape(dec, c_a) for a in ops_s], state_wkv[0], lnx_g[0], lnx_b[0], 8)
    tables_s = rope_tables(jnp.full((1,), past_len, F32))
    o_b_s, o_m_s, nk_s, nv_s = attn_step(
        zq_s, cache_win_k[0].reshape(dec, window, kv_w), cache_win_v[0].reshape(dec, window, kv_w),
        cache_mem_k[0].reshape(dec, n_mem, mem_w), cache_mem_v[0].reshape(dec, n_mem, mem_w),
        sinks[0], tables_s, q_w, kv_w, 8)
    x1, x1_tiles, eidx, gate = merge(xs, o_a_s, o_b_s, o_m_s, zg_s, mw, dec, alpha, n_buf, bsz * seq, routed)

    y_prompt, y_sample = hier_moe_ln(x1, x1_tiles, eidx, gate, mw, dec, alpha, n_all, bsz * seq)
    y_prompt = y_prompt.reshape(bsz, seq, d)
    y_sample = y_sample.reshape(dec, 1, d)

    sd = state_wkv.dtype
    return (y_prompt, y_sample, wkv_p[None].astype(sd), wkv_s[None].astype(sd), shift_p[None], zr_s[None],
            kb_p[None], vb_p[None], nk_s.reshape(dec, window, H_KV, HEAD)[None],
            nv_s.reshape(dec, window, H_KV, HEAD)[None],
            mk_p.reshape(bsz, n_mem, -1, HEAD)[None], mv_p.reshape(bsz, n_mem, -1, HEAD)[None])
```

```python
import functools
import math

import jax
import jax.numpy as jnp
from jax import lax
from jax.experimental import pallas as pl
from jax.experimental.pallas import tpu as pltpu

F32 = jnp.float32
BF16 = jnp.bfloat16
SCAN_BATCH = 8
ATTN_BLOCKS = 4
CHUNKS_PER_STEP = 4

HEAD = 64
LANES = 128
H_KV = 2
ROT_HALF = 8
ROPE_THETA = 500000.0
PAST_LEN = 8192
N_GROUPS = 4
EXPERTS_PER_GROUP = 8
EXPERT_BLOCK = 128
LN_EPS = 1e-5
LNX_EPS = 64e-5
NEG_INF = -1e30
VMEM_LIMIT = 48 * 1024 * 1024


def _pieces(x, n):
    out = []
    for _ in range(n):
        p = x.astype(BF16)
        out.append(p)
        x = x - p.astype(F32)
    return out


def _mask_dot(x, mask, n=2):
    return sum(jnp.dot(p, mask, preferred_element_type=F32) for p in _pieces(x, n))


def _split3(x, axis, lhs):
    hi = x.astype(BF16).astype(F32)
    lo = x - hi
    return jnp.concatenate([hi, hi, lo] if lhs else [hi, lo, hi], axis=axis).astype(BF16)


def _dot3(a, b):
    return jnp.dot(_split3(a, 1, True), _split3(b, 0, False), preferred_element_type=F32)


def _dot3_t(a, b):
    return lax.dot_general(_split3(a, 1, True), _split3(b, 1, False), (((1,), (1,)), ((), ())),
                           preferred_element_type=F32)


def _bdot(a, b):
    return jnp.dot(a.astype(BF16), b.astype(BF16), preferred_element_type=F32)


def _bdot_t(a, b):
    return lax.dot_general(a.astype(BF16), b.astype(BF16), (((1,), (1,)), ((), ())), preferred_element_type=F32)


def _sigmoid(x):
    return 0.5 * jnp.tanh(0.5 * x) + 0.5


def _ln(x, g, b):
    mu = jnp.mean(x, axis=-1, keepdims=True)
    xc = x - mu
    var = jnp.mean(xc * xc, axis=-1, keepdims=True)
    return xc * lax.rsqrt(var + LN_EPS) * g + b


def _const_spec(shape):
    nd = len(shape)
    return pl.BlockSpec(shape, lambda *_: (0,) * nd)


def _params(sem):
    return pltpu.CompilerParams(dimension_semantics=sem, vmem_limit_bytes=VMEM_LIMIT)


def _ln_proj_prep_kernel(chunk, x_ref, g_ref, b_ref, wr_ref, wq_ref, wg_ref, prev_ref, mu_ref, w0_ref, wup_ref,
                         a0_ref, aup_ref, gup_ref, kk_ref, ka_ref, rk_ref, hsum_ref, tril_ref,
                         zq_ref, zg_ref, zr_ref, *refs):
    out_refs, carry_ref = refs[:-1], refs[-1]
    xn = _ln(x_ref[0], g_ref[...], b_ref[...]).astype(BF16)
    z = jnp.dot(xn, wr_ref[...], preferred_element_type=F32)
    zq_ref[0] = jnp.dot(xn, wq_ref[...], preferred_element_type=F32)
    zg_ref[0] = jnp.dot(xn, wg_ref[...], preferred_element_type=F32).astype(zg_ref.dtype)
    tt = z.shape[0]
    c_a = w0_ref.shape[-1]
    r_w, r_a, r_g = wup_ref.shape[0], aup_ref.shape[0], gup_ref.shape[0]
    if chunk == 1:
        zr_ref[0] = z
        prev = prev_ref[0]
    else:
        zr_ref[0] = z[tt - 1:tt, :]

        @pl.when(pl.program_id(1) == 0)
        def _():
            carry_ref[...] = prev_ref[0]

        row = lax.broadcasted_iota(jnp.int32, z.shape, 0)
        prev = jnp.where(row == 0, carry_ref[...], pltpu.roll(z, 1, 0))
        carry_ref[...] = z[tt - 1:tt, :]
    zs = z + (prev - z) * mu_ref[...]
    r = zs[:, :c_a]
    k = zs[:, c_a:2 * c_a]
    v = zs[:, 2 * c_a:3 * c_a]
    o = 3 * c_a
    xw = zs[:, o:o + r_w]
    xa = zs[:, o + r_w:o + r_w + r_a]
    xg = zs[:, o + r_w + r_a:o + r_w + r_a + r_g]
    warg = -(w0_ref[...] + _dot3(jnp.tanh(xw), wup_ref[...]))
    softplus = jnp.maximum(warg, 0.0) + jnp.log1p(jnp.exp(-jnp.abs(warg)))
    lw = -jnp.exp(-softplus - 0.5)
    a = _sigmoid(a0_ref[...] + _dot3(xa, aup_ref[...]))
    g = _dot3(_sigmoid(xg), gup_ref[...])
    kkr = k * kk_ref[...]
    kk = kkr / jnp.maximum(jnp.sqrt(_mask_dot(kkr * kkr, hsum_ref[...])), 1e-12)
    k2 = k * (1.0 + (a - 1.0) * ka_ref[...])
    bonus = _mask_dot(r * k2 * rk_ref[...], hsum_ref[...]) * v
    kb = kk * a
    if chunk == 1:
        outs = (r, jnp.exp(lw), k2, v, -kk, kb, g, bonus)
        for ref, val in zip(out_refs, outs):
            ref[0] = val
        return
    pieces = _pieces(lw, 3)
    ones3 = jnp.ones((chunk, 3 * chunk), BF16)
    cw, cwl = [], []
    for c in range(tt // chunk):
        stack = jnp.concatenate([p[c * chunk:(c + 1) * chunk] for p in pieces], axis=0)
        cw.append(jnp.dot(tril_ref[...], stack, preferred_element_type=F32))
        cwl.append(jnp.dot(ones3, stack, preferred_element_type=F32))
    cw = jnp.concatenate(cw, axis=0)
    cwl = jnp.concatenate(cwl, axis=0)
    e_inv = jnp.exp(-cw)
    outs = (r * jnp.exp(cw), -kk * jnp.exp(cw - lw), kb * e_inv, k2 * e_inv, v, g, bonus)
    for ref, val in zip(out_refs[:-1], outs):
        ref[0] = val
    wl = jnp.exp(cwl)
    for c in range(tt // chunk):
        out_refs[-1][0, c] = wl[c * chunk:c * chunk + 1, :]


def _head_sum_matrix(width):
    idx = jnp.arange(width)
    return ((idx[:, None] // HEAD) == (idx[None, :] // HEAD)).astype(BF16)


def ln_proj_prep(x, g, b, ws, prev, p, chunk, tt, gate_dtype):
    bsz, t, d = x.shape
    w_r, w_q, w_g = ws
    cs = w_r.shape[1]
    c_a = p['w0'].shape[-1]
    assert prev.shape[1] == (t if chunk == 1 else 1)
    ridx = jnp.arange(chunk)
    tril = jnp.tile((ridx[None, :] <= ridx[:, None]).astype(BF16), (1, 3))
    hsum = _head_sum_matrix(c_a)
    row = lambda a: a.reshape(1, -1)
    rows = lambda width: pl.BlockSpec((1, tt, width), lambda bi, i: (bi, i, 0))
    full = jax.ShapeDtypeStruct((bsz, t, c_a), F32)
    if chunk == 1:
        out_specs, out_shape = [rows(c_a)] * 8, [full] * 8
        prev_spec, zr_spec, zr_rows = rows(cs), rows(cs), t
    else:
        out_specs = [rows(c_a)] * 7 + [pl.BlockSpec((1, tt // chunk, 1, c_a), lambda bi, i: (bi, i, 0, 0))]
        out_shape = [full] * 7 + [jax.ShapeDtypeStruct((bsz, t // chunk, 1, c_a), F32)]
        prev_spec = zr_spec = pl.BlockSpec((1, 1, cs), lambda bi, i: (bi, 0, 0))
        zr_rows = 1
    consts = [row(p['mu']), row(p['w0']), p['w_up'], row(p['a0']), p['a_up'], p['g_up'], row(p['k_k']),
              row(p['k_a']), row(p['r_k']), hsum, tril]
    weight = lambda w: pl.BlockSpec(w.shape, lambda bi, i: (0, 0), pipeline_mode=pl.Buffered(1))
    outs = pl.pallas_call(
        functools.partial(_ln_proj_prep_kernel, chunk), name="ln_proj_prep",
        grid=(bsz, t // tt),
        in_specs=[rows(d), _const_spec((1, d)), _const_spec((1, d)), weight(w_r), weight(w_q), weight(w_g), prev_spec]
        + [_const_spec(c.shape) for c in consts],
        out_specs=[rows(w_q.shape[1]), rows(w_g.shape[1]), zr_spec] + out_specs,
        out_shape=[jax.ShapeDtypeStruct((bsz, t, w_q.shape[1]), F32),
                   jax.ShapeDtypeStruct((bsz, t, w_g.shape[1]), gate_dtype),
                   jax.ShapeDtypeStruct((bsz, zr_rows, cs), F32)] + out_shape,
        scratch_shapes=[pltpu.VMEM((1, cs), F32)],
        compiler_params=_params(("parallel", "arbitrary")),
    )(x, row(g), row(b), w_r, w_q, w_g, prev, *consts)
    return outs[0], outs[1], outs[2], outs[3:]


def _wkv_chunk_kernel(ra_ref, at_ref, bt_ref, kt_ref, v_ref, wl_ref, rp_ref, y0_ref, m_ref, n_ref):
    n_chunks = m_ref.shape[1]
    length = ra_ref.shape[1] // n_chunks
    heads = ra_ref.shape[2] // HEAD
    row = lax.broadcasted_iota(jnp.int32, (length, length), 0)
    col = lax.broadcasted_iota(jnp.int32, (length, length), 1)
    strict = row > col
    incl = row >= col
    hrow = lax.broadcasted_iota(jnp.int32, (HEAD, HEAD), 0)
    hcol = lax.broadcasted_iota(jnp.int32, (HEAD, HEAD), 1)
    units = [(c, h) for c in range(n_chunks) for h in range(heads)]
    us = range(len(units))
    at = lambda ref, u: ref[0, units[u][0] * length:(units[u][0] + 1) * length,
                            units[u][1] * HEAD:(units[u][1] + 1) * HEAD]
    gram = [_bdot_t(jnp.concatenate([at(at_ref, u), at(ra_ref, u)], axis=0),
                    jnp.concatenate([at(bt_ref, u), at(kt_ref, u)], axis=0)) for u in us]
    a_ab = [jnp.where(strict, g[:length, :length], 0.0) for g in gram]
    a_kk = [jnp.concatenate([jnp.where(strict, g[:length, length:], 0.0),
                             jnp.where(incl, g[length:, length:], 0.0)], axis=0) for g in gram]
    a_rb = [jnp.where(incl, g[length:, :length], 0.0) for g in gram]
    inv = [jnp.where(row == col, 1.0, a) for a in a_ab]
    pw = [_bdot(a, a) for a in a_ab]
    avy = [_bdot(a_kk[u], at(v_ref, u)) for u in us]
    w_l = [wl_ref[0, units[u][0], :, units[u][1] * HEAD:(units[u][1] + 1) * HEAD] for u in us]
    nk = [_dot3(at(v_ref, u).T, at(kt_ref, u) * w_l[u]) for u in us]
    for _ in range(int(math.log2(length)) - 2):
        both = [_bdot(jnp.concatenate([pw[u], inv[u]], axis=0), pw[u]) for u in us]
        pw = [b[:length] for b in both]
        inv = [inv[u] + both[u][length:] for u in us]
    inv = [inv[u] + _bdot(inv[u], pw[u]) for u in us]
    pq = [_bdot(inv[u], jnp.concatenate([at(at_ref, u), avy[u][:length]], axis=1)) for u in us]
    ry = [_bdot(a_rb[u], pq[u]) for u in us]
    mn = [_dot3(pq[u].T, at(bt_ref, u) * w_l[u]) for u in us]
    lanes = lambda f, c: jnp.concatenate([f(c * heads + h) for h in range(heads)], axis=1)
    rows = lambda f: jnp.concatenate([lanes(f, c) for c in range(n_chunks)], axis=0)
    rp_ref[0] = rows(lambda u: at(ra_ref, u) + ry[u][:, :HEAD])
    y0_ref[0] = rows(lambda u: avy[u][length:] + ry[u][:, HEAD:])
    for c in range(n_chunks):
        m_ref[0, c] = lanes(lambda u: jnp.where(hrow == hcol, w_l[u], 0.0) + mn[u][:HEAD], c)
        n_ref[0, c] = lanes(lambda u: nk[u] + mn[u][HEAD:], c)


def _head_norm_wide(y, hsum):
    yc = y - _mask_dot(y, hsum) * (1.0 / HEAD)
    return yc * lax.rsqrt(_mask_dot(yc * yc, hsum) * (1.0 / HEAD) + LNX_EPS)


def _wkv_scan_kernel(rp_ref, y0_ref, m_ref, n_ref, g_ref, bonus_ref, s0_ref, lg_ref, lb_ref, hsum_ref,
                     o_ref, sout_ref, s_ref):
    c = pl.program_id(1)
    nb = rp_ref.shape[0]
    heads = rp_ref.shape[2] // HEAD

    @pl.when(c == 0)
    def _():
        s_ref[...] = s0_ref[...]

    pairs = [(b, h, slice(h * HEAD, (h + 1) * HEAD)) for b in range(nb) for h in range(heads)]
    s = [s_ref[b, h] for b, h, _ in pairs]
    s_new = [n_ref[b, 0, :, sl] + _dot3(s[j], m_ref[b, 0, :, sl]) for j, (b, h, sl) in enumerate(pairs)]
    ys = [y0_ref[b, :, sl] + _bdot_t(rp_ref[b, :, sl], s[j]) for j, (b, h, sl) in enumerate(pairs)]
    for j, (b, h, _) in enumerate(pairs):
        s_ref[b, h] = s_new[j]
    length = rp_ref.shape[1]
    y = _head_norm_wide(jnp.concatenate([jnp.concatenate(ys[b * heads:(b + 1) * heads], axis=-1) for b in range(nb)],
                                        axis=0), hsum_ref[...])
    for b in range(nb):
        o = (y[b * length:(b + 1) * length] * lg_ref[...] + lb_ref[...] + bonus_ref[b]) * g_ref[b]
        o_ref[b] = o.astype(o_ref.dtype)

    @pl.when(c == pl.num_programs(1) - 1)
    def _():
        sout_ref[...] = s_ref[...]


def wkv(prep, s0, lnx_g, lnx_b, chunk):
    ra, at, bt, kt, v, g, bonus, wl = prep
    bsz, t, c_a = ra.shape
    heads = c_a // HEAD
    n_chunks = t // chunk
    assert chunk & (chunk - 1) == 0 and chunk >= 4 and t % chunk == 0
    per_step = CHUNKS_PER_STEP if n_chunks % CHUNKS_PER_STEP == 0 else 1
    tile = pl.BlockSpec((1, per_step * chunk, c_a), lambda b, c: (b, c, 0))
    mat = pl.BlockSpec((1, per_step, HEAD, c_a), lambda b, c: (b, c, 0, 0))
    full = jax.ShapeDtypeStruct((bsz, t, c_a), F32)
    mats = jax.ShapeDtypeStruct((bsz, n_chunks, HEAD, c_a), F32)
    rp, y0, m, n = pl.pallas_call(
        _wkv_chunk_kernel, name="wkv_chunk",
        grid=(bsz, n_chunks // per_step),
        in_specs=[tile] * 5 + [pl.BlockSpec((1, per_step, 1, c_a), lambda b, c: (b, c, 0, 0))],
        out_specs=[tile, tile, mat, mat],
        out_shape=[full, full, mats, mats],
        compiler_params=_params(("parallel", "parallel")),
    )(ra, at, bt, kt, v, wl)
    nb = SCAN_BATCH if bsz % SCAN_BATCH == 0 else 1
    tile = pl.BlockSpec((nb, chunk, c_a), lambda b, c: (b, c, 0))
    mat = pl.BlockSpec((nb, 1, HEAD, c_a), lambda b, c: (b, c, 0, 0))
    st = pl.BlockSpec((nb, heads, HEAD, HEAD), lambda b, c: (b, 0, 0, 0))
    return pl.pallas_call(
        _wkv_scan_kernel, name="wkv_scan",
        grid=(bsz // nb, n_chunks),
        in_specs=[tile, tile, mat, mat, tile, tile, st, _const_spec((1, c_a)), _const_spec((1, c_a)),
                  _const_spec((c_a, c_a))],
        out_specs=[tile, st],
        out_shape=[jax.ShapeDtypeStruct((bsz, t, c_a), BF16), jax.ShapeDtypeStruct(s0.shape, F32)],
        scratch_shapes=[pltpu.VMEM((nb, heads, HEAD, HEAD), F32)],
        compiler_params=_params(("parallel", "arbitrary")),
    )(rp, y0, m, n, g, bonus, s0, lnx_g.reshape(1, c_a), lnx_b.reshape(1, c_a), _head_sum_matrix(c_a))


def _wkv_step_kernel(r_ref, w_ref, k_ref, v_ref, a_ref, b_ref, g_ref, bonus_ref, s_ref, lg_ref, lb_ref, hsum_ref,
                     o_ref, sout_ref):
    bt, heads = s_ref.shape[0], s_ref.shape[1]
    c_a = heads * HEAD
    hsum = hsum_ref[...]
    diag = (lax.broadcasted_iota(jnp.int32, (HEAD, c_a), 1) % HEAD
            == lax.broadcasted_iota(jnp.int32, (HEAD, c_a), 0))
    seqs = range(bt)
    row = lambda ref, i: ref[i:i + 1, :]
    stack = lambda f: jnp.concatenate([f(i) for i in seqs], axis=0)
    piece = lambda x, i: x[i * HEAD:(i + 1) * HEAD]
    s = stack(lambda i: jnp.concatenate([s_ref[i, h] for h in range(heads)], axis=1))
    sa = _mask_dot(s * stack(lambda i: jnp.broadcast_to(row(a_ref, i), (HEAD, c_a))), hsum)
    v_rows = _mask_dot(stack(lambda i: jnp.where(diag, row(v_ref, i), 0.0)), hsum)
    s = stack(lambda i: piece(s, i) * row(w_ref, i) + piece(sa, i) * row(b_ref, i) + piece(v_rows, i) * row(k_ref, i))
    for i in seqs:
        for h in range(heads):
            sout_ref[i, h] = piece(s, i)[:, h * HEAD:(h + 1) * HEAD]
    y_rows = _mask_dot(stack(lambda i: piece(s, i) * row(r_ref, i)), hsum)
    y = stack(lambda i: jnp.sum(jnp.where(diag, piece(y_rows, i), 0.0), axis=0, keepdims=True))
    y = _head_norm_wide(y, hsum)
    o_ref[...] = (y * lg_ref[...] + lb_ref[...] + bonus_ref[...]) * g_ref[...]


def wkv_step(ops, s0, lnx_g, lnx_b, bt):
    bsz, c_a = ops[0].shape
    rows = pl.BlockSpec((bt, c_a), lambda i: (i, 0))
    st = pl.BlockSpec((bt,) + s0.shape[1:], lambda i: (i, 0, 0, 0))
    hsum = _head_sum_matrix(c_a)
    return pl.pallas_call(
        _wkv_step_kernel, name="wkv_step",
        grid=(bsz // bt,),
        in_specs=[rows] * 8 + [st, _const_spec((1, c_a)), _const_spec((1, c_a)), _const_spec(hsum.shape)],
        out_specs=[rows, st],
        out_shape=[jax.ShapeDtypeStruct((bsz, c_a), F32), jax.ShapeDtypeStruct(s0.shape, F32)],
        compiler_params=_params(("parallel",)),
    )(*ops, s0, lnx_g.reshape(1, c_a), lnx_b.reshape(1, c_a), hsum)


def rope_tables(pos):
    inv_freq = ROPE_THETA ** (-jnp.arange(ROT_HALF, dtype=F32) / ROT_HALF)
    ang = pos[:, None] * inv_freq[None, :]
    cos, sin = jnp.cos(ang), jnp.sin(ang)
    t = pos.shape[0]
    rest = HEAD - 2 * ROT_HALF
    c = jnp.concatenate([cos, cos, jnp.ones((t, rest), F32)], axis=1)
    s1 = jnp.concatenate([jnp.zeros((t, ROT_HALF), F32), sin, jnp.zeros((t, rest), F32)], axis=1)
    s2 = jnp.concatenate([-sin, jnp.zeros((t, HEAD - ROT_HALF), F32)], axis=1)
    rep = LANES // HEAD
    return tuple(jnp.tile(x, (1, rep)) for x in (c, s1, s2))


def _rope(x, c, s1, s2):
    width = x.shape[-1]
    rep = width // c.shape[-1]
    if rep > 1:
        c, s1, s2 = (jnp.concatenate([t] * rep, axis=1) for t in (c, s1, s2))
    return x * c + pltpu.roll(x, ROT_HALF, 1) * s1 + pltpu.roll(x, width - ROT_HALF, 1) * s2


def _attn_kernel(window, q_w, kv_w, zq_ref, kp_ref, vp_ref, rc_ref, rs1_ref, rs2_ref, pc_ref, ps1_ref, ps2_ref,
                 mk_ref, mv_ref, sink_ref, ob_ref, om_ref, kr_ref):
    zq = zq_ref[0]
    nblk = zq.shape[0] // window
    first = pl.program_id(1) * nblk
    scale = HEAD ** -0.5
    q = zq[:, :q_w]
    k = zq[:, q_w:q_w + kv_w]
    v = zq[:, q_w + kv_w:q_w + 2 * kv_w]
    qm = zq[:, q_w + 2 * kv_w:]
    qr = _rope(q, rc_ref[...], rs1_ref[...], rs2_ref[...]) * scale
    kr = _rope(k, rc_ref[...], rs1_ref[...], rs2_ref[...])
    kr_ref[0] = kr
    k_all = jnp.concatenate([_rope(kp_ref[0], pc_ref[...], ps1_ref[...], ps2_ref[...]), kr], axis=0)
    v_all = jnp.concatenate([vp_ref[0], v], axis=0)
    gqa = q_w // kv_w
    assert window & (window - 1) == 0
    qi = lax.broadcasted_iota(jnp.int32, (gqa * window, 2 * window), 0) & (window - 1)
    kj = lax.broadcasted_iota(jnp.int32, (gqa * window, 2 * window), 1)
    band = (kj > qi) & (kj <= qi + window)
    tdot = lambda a, b: lax.dot_general(a, b, (((1,), (1,)), ((), ())), preferred_element_type=F32)
    dot = lambda a, b: jnp.dot(a, b, preferred_element_type=F32)
    hsl = lambda h: slice(h * HEAD, (h + 1) * HEAD)
    rows = lambda i: slice(i * window, (i + 1) * window)
    keys = lambda i: slice(i * window, (i + 2) * window)
    wins = [(i, hk) for i in range(nblk) for hk in range(kv_w // HEAD)]
    mems = [(i, h) for i in range(nblk) for h in range(qm.shape[1] // HEAD)]
    qmb = (qm * scale).astype(BF16)
    mk = mk_ref[0].astype(BF16)
    mv = mv_ref[0].astype(BF16)
    kcat = [k_all[keys(i), hsl(hk)].astype(BF16) for i, hk in wins]
    vcat = [v_all[keys(i), hsl(hk)].astype(BF16) for i, hk in wins]
    qs = [jnp.concatenate([qr[rows(i), hsl(hk * gqa + g)] for g in range(gqa)], axis=0).astype(BF16)
          for i, hk in wins]
    s = [jnp.where(band & ((kj >= window) | (first + i > 0)), tdot(qs[u], kcat[u]), NEG_INF)
         for u, (i, hk) in enumerate(wins)]
    sm = [tdot(qmb[rows(i), hsl(h)], mk[:, hsl(h)]) for i, h in mems]
    sink = [jnp.concatenate([jnp.full((window, 1), sink_ref[hk * gqa + g], F32) for g in range(gqa)], axis=0)
            for i, hk in wins]
    m = [jnp.maximum(jnp.max(s[u], axis=-1, keepdims=True), sink[u]) for u in range(len(wins))]
    p = [jnp.exp(s[u] - m[u]) for u in range(len(wins))]
    pm = [jnp.exp(x - jnp.max(x, axis=-1, keepdims=True)) for x in sm]
    o = [dot(p[u].astype(BF16), vcat[u])
         / (jnp.sum(p[u], axis=-1, keepdims=True) + jnp.exp(sink[u] - m[u])) for u in range(len(wins))]
    om = [dot(pm[u].astype(BF16), mv[:, hsl(h)]) / jnp.sum(pm[u], axis=-1, keepdims=True)
          for u, (i, h) in enumerate(mems)]
    n_kv, n_mh = kv_w // HEAD, qm.shape[1] // HEAD
    ob = jnp.concatenate([jnp.concatenate([o[i * n_kv + hk][g * window:(g + 1) * window]
                                           for hk in range(n_kv) for g in range(gqa)], axis=1)
                          for i in range(nblk)], axis=0)
    ob_ref[0] = ob.astype(ob_ref.dtype)
    om_ref[0] = jnp.concatenate([jnp.concatenate(om[i * n_mh:(i + 1) * n_mh], axis=1) for i in range(nblk)],
                                axis=0).astype(om_ref.dtype)


def attn_prompt(zq, mk, mv, sinks, tables, window, q_w, kv_w):
    bsz, t, zw = zq.shape
    mem_w = zw - q_w - 2 * kv_w
    assert kv_w == LANES and q_w % kv_w == 0
    kcol, vcol = q_w // kv_w, q_w // kv_w + 1
    nblk = ATTN_BLOCKS if (t // window) % ATTN_BLOCKS == 0 else 1
    rows = nblk * window
    prev = lambda n: jnp.maximum(n * nblk - 1, 0)
    tab = pl.BlockSpec((rows, LANES), lambda b, n: (n, 0))
    ptab = pl.BlockSpec((window, LANES), lambda b, n: (prev(n), 0))
    mem = pl.BlockSpec((1,) + mk.shape[1:], lambda b, n: (b, 0, 0))
    return pl.pallas_call(
        functools.partial(_attn_kernel, window, q_w, kv_w), name="attn",
        grid=(bsz, t // rows),
        in_specs=[pl.BlockSpec((1, rows, zw), lambda b, n: (b, n, 0)),
                  pl.BlockSpec((1, window, kv_w), lambda b, n: (b, prev(n), kcol)),
                  pl.BlockSpec((1, window, kv_w), lambda b, n: (b, prev(n), vcol)),
                  tab, tab, tab, ptab, ptab, ptab, mem, mem,
                  pl.BlockSpec(memory_space=pltpu.SMEM)],
        out_specs=[pl.BlockSpec((1, rows, q_w), lambda b, n: (b, n, 0)),
                   pl.BlockSpec((1, rows, mem_w), lambda b, n: (b, n, 0)),
                   pl.BlockSpec((1, rows, kv_w), lambda b, n: (b, n, 0))],
        out_shape=[jax.ShapeDtypeStruct((bsz, t, q_w), BF16), jax.ShapeDtypeStruct((bsz, t, mem_w), BF16),
                   jax.ShapeDtypeStruct((bsz, t, kv_w), F32)],
        compiler_params=_params(("parallel", "parallel")),
    )(zq, zq, zq, *tables, *tables, mk, mv, sinks)


def _attn_step_kernel(q_w, kv_w, zq_ref, ck_ref, cv_ref, mk_ref, mv_ref, rc_ref, rs1_ref, rs2_ref, sink_ref,
                      ob_ref, om_ref, nk_ref, nv_ref):
    bt = zq_ref.shape[0]
    window = ck_ref.shape[1]
    mem_w = om_ref.shape[1]
    n_q, gqa, per_vreg = q_w // HEAD, q_w // kv_w, LANES // HEAD
    scale = HEAD ** -0.5
    zq = zq_ref[...]
    q = _rope(zq[:, :q_w], rc_ref[...], rs1_ref[...], rs2_ref[...]) * scale
    k_new = _rope(zq[:, q_w:q_w + kv_w], rc_ref[...], rs1_ref[...], rs2_ref[...])
    v_new = zq[:, q_w + kv_w:q_w + 2 * kv_w]
    qm = zq[:, q_w + 2 * kv_w:] * scale
    own = lambda w: (lax.broadcasted_iota(jnp.int32, (n_q, w), 1) // HEAD
                     == lax.broadcasted_iota(jnp.int32, (n_q, w), 0))
    own_q, own_m = own(q_w), own(mem_w)
    hrow = lax.broadcasted_iota(jnp.int32, (n_q, LANES), 0)
    hblk = lax.broadcasted_iota(jnp.int32, (n_q, LANES), 1) // HEAD
    swap = (hrow % per_vreg) != (hrow // gqa)
    keep = hblk == hrow % per_vreg
    key_ok = lax.broadcasted_iota(jnp.int32, (n_q, window), 1) >= 1
    wrow = lax.broadcasted_iota(jnp.int32, (window, kv_w), 0)
    sink = sink_ref[...]
    bs = range(bt)
    tdot = lambda a, b: lax.dot_general(a.astype(BF16), b.astype(BF16), (((1,), (1,)), ((), ())),
                                        preferred_element_type=F32)
    for b in bs:
        nk_ref[b] = jnp.where(wrow == window - 1, k_new[b:b + 1], pltpu.roll(ck_ref[b], window - 1, 0))
        nv_ref[b] = jnp.where(wrow == window - 1, v_new[b:b + 1], pltpu.roll(cv_ref[b], window - 1, 0))
    q8 = []
    for b in bs:
        rep = jnp.where(own_q, q[b:b + 1], 0.0)
        fold = sum(rep[:, c * LANES:(c + 1) * LANES] for c in range(q_w // LANES))
        q8.append(jnp.where(swap, pltpu.roll(fold, HEAD, 1), fold))
    qm8 = [jnp.where(own_m, qm[b:b + 1], 0.0) for b in bs]
    s = [jnp.where(key_ok, tdot(q8[b], ck_ref[b]), NEG_INF) for b in bs]
    sm = [tdot(qm8[b], mk_ref[b]) for b in bs]
    s_new = [jnp.sum(q8[b] * k_new[b:b + 1], axis=1, keepdims=True) for b in bs]
    m = [jnp.maximum(jnp.maximum(jnp.max(s[b], axis=1, keepdims=True), s_new[b]), sink) for b in bs]
    p = [jnp.exp(s[b] - m[b]) for b in bs]
    p_new = [jnp.exp(s_new[b] - m[b]) for b in bs]
    pm = [jnp.exp(sm[b] - jnp.max(sm[b], axis=1, keepdims=True)) for b in bs]
    o8 = [(_bdot(p[b], cv_ref[b]) + p_new[b] * v_new[b:b + 1])
          / (jnp.sum(p[b], axis=1, keepdims=True) + p_new[b] + jnp.exp(sink - m[b])) for b in bs]
    om8 = [_bdot(pm[b], mv_ref[b]) / jnp.sum(pm[b], axis=1, keepdims=True) for b in bs]
    ob, om = [], []
    for b in bs:
        o = jnp.where(swap, pltpu.roll(o8[b], HEAD, 1), o8[b])
        o = jnp.concatenate([jnp.where(keep, o, 0.0)] * (q_w // LANES), axis=1)
        ob.append(jnp.sum(jnp.where(own_q, o, 0.0), axis=0, keepdims=True))
        om.append(jnp.sum(jnp.where(own_m, om8[b], 0.0), axis=0, keepdims=True))
    ob_ref[...] = jnp.concatenate(ob, axis=0)
    om_ref[...] = jnp.concatenate(om, axis=0)


def attn_step(zq, ck, cv, mk, mv, sinks, tables, q_w, kv_w, bt):
    bsz, zw = zq.shape
    mem_w = zw - q_w - 2 * kv_w
    window, n_mem = ck.shape[1], mk.shape[1]
    n_q = q_w // HEAD
    assert kv_w == LANES and mem_w // HEAD <= n_q and n_q == 8
    rows = lambda w: pl.BlockSpec((bt, w), lambda i: (i, 0))
    cache = pl.BlockSpec((bt, window, kv_w), lambda i: (i, 0, 0))
    mem = pl.BlockSpec((bt, n_mem, mem_w), lambda i: (i, 0, 0))
    consts = list(tables) + [sinks.reshape(n_q, 1)]
    const_specs = [_const_spec(c.shape) for c in consts]
    return pl.pallas_call(
        functools.partial(_attn_step_kernel, q_w, kv_w), name="attn_step",
        grid=(bsz // bt,),
        in_specs=[rows(zw), cache, cache, mem, mem] + const_specs,
        out_specs=[rows(q_w), rows(mem_w), cache, cache],
        out_shape=[jax.ShapeDtypeStruct((bsz, q_w), F32), jax.ShapeDtypeStruct((bsz, mem_w), F32),
                   jax.ShapeDtypeStruct(ck.shape, F32), jax.ShapeDtypeStruct(cv.shape, F32)],
        compiler_params=_params(("parallel",)),
    )(zq, ck, cv, mk, mv, *consts)


def _proj_kernel(x_ref, w_ref, o_ref):
    o_ref[...] = jnp.dot(x_ref[...].astype(BF16), w_ref[...], preferred_element_type=F32)


def proj(x, w, tm):
    n, d = x.shape
    return pl.pallas_call(
        _proj_kernel, name="proj",
        grid=(n // tm,),
        in_specs=[pl.BlockSpec((tm, d), lambda i: (i, 0)), _const_spec(w.shape)],
        out_specs=pl.BlockSpec((tm, w.shape[1]), lambda i: (i, 0)),
        out_shape=jax.ShapeDtypeStruct((n, w.shape[1]), F32),
        compiler_params=_params(("parallel",)),
    )(x, w)


def _merge_kernel(alpha, n_tiles, x_ref, oa_ref, ob_ref, om_ref, gt_ref, lig_ref, lib_ref, pa_ref, pb_ref, pm_ref,
                  wo_ref, l1g_ref, l1b_ref, wr_ref, br_ref, *refs):
    outs = refs[-4:]

    @pl.when(pl.program_id(0) >= n_tiles)
    def _():
        for ref in outs:
            ref[...] = jnp.zeros_like(ref)

    @pl.when(pl.program_id(0) < n_tiles)
    def _():
        _merge_tile(alpha, x_ref, oa_ref, ob_ref, om_ref, gt_ref, lig_ref, lib_ref, pa_ref, pb_ref, pm_ref, wo_ref,
                    l1g_ref, l1b_ref, wr_ref, br_ref, *outs)


def _merge_tile(alpha, x_ref, oa_ref, ob_ref, om_ref, gt_ref, lig_ref, lib_ref, pa_ref, pb_ref, pm_ref, wo_ref,
                l1g_ref, l1b_ref, wr_ref, br_ref, x1_ref, x1t_ref, eidx_ref, gate_ref):
    d = x_ref.shape[1]
    xn = _ln(x_ref[...], lig_ref[...], lib_ref[...])
    gts = _sigmoid(gt_ref[...].astype(F32))
    merged = (gts[:, :d] * _bdot(oa_ref[...], pa_ref[...]) + gts[:, d:2 * d] * _bdot(ob_ref[...], pb_ref[...])
              + gts[:, 2 * d:] * _bdot(om_ref[...], pm_ref[...]))
    x1 = _ln(alpha * xn + _bdot(merged, wo_ref[...]), l1g_ref[...], l1b_ref[...])
    x1_ref[...] = x1
    _rows_to_tiles(x1t_ref, x1)
    x_hi, x_lo = _pieces(x1, 2)
    w_hi, w_lo = wr_ref[0], wr_ref[1]
    dot = lambda a, b: jnp.dot(a, b, preferred_element_type=F32)
    logits = dot(x_hi, w_hi) + dot(x_hi, w_lo) + dot(x_lo, w_hi) + br_ref[...]
    lane = lax.broadcasted_iota(jnp.int32, logits.shape, 1)
    lane_f = lane.astype(F32)
    first = lambda hit: jnp.min(jnp.where(hit, lane_f, float(LANES)), axis=-1, keepdims=True).astype(jnp.int32)
    gmask = lane < N_GROUPS
    gl = jnp.where(gmask, logits, NEG_INF)
    gmax = jnp.max(gl, axis=-1, keepdims=True)
    gidx = first(gl == gmax)
    g_w = 1.0 / jnp.sum(jnp.where(gmask, jnp.exp(gl - gmax), 0.0), axis=-1, keepdims=True)
    lo = N_GROUPS + gidx * EXPERTS_PER_GROUP
    el = jnp.where((lane >= lo) & (lane < lo + EXPERTS_PER_GROUP), logits, NEG_INF)
    v1 = jnp.max(el, axis=-1, keepdims=True)
    i1 = first(el == v1)
    el2 = jnp.where(lane == i1, NEG_INF, el)
    v2 = jnp.max(el2, axis=-1, keepdims=True)
    i2 = first(el2 == v2)
    e2 = jnp.exp(v2 - v1)
    gate1 = g_w / (1.0 + e2)
    eidx_ref[...] = jnp.where(lane == 0, i1 - N_GROUPS, jnp.where(lane == 1, i2 - N_GROUPS, 0))
    gate_ref[...] = jnp.where(lane == 0, gate1, jnp.where(lane == 1, gate1 * e2, 0.0))


def merge(x, oa, ob, om, gt, w, tm, alpha, n_total, row_offset=0, into=None):
    n, d = x.shape
    assert row_offset % tm == 0 and n % tm == 0 and n_total % tm == 0
    off = row_offset // tm
    into = list(into or [])
    n_tiles = n // tm
    steps = n_tiles if into else n_total // tm
    rows = lambda a: pl.BlockSpec((tm, a.shape[1]), lambda i: (jnp.minimum(i, n_tiles - 1), 0))
    out = lambda width: pl.BlockSpec((tm, width), lambda i: (i + off, 0))
    consts = [w['ln_in_g'], w['ln_in_b'], w['p_a'], w['p_b'], w['p_m'], w['w_o'], w['ln1_g'], w['ln1_b'],
              w['w_route'], w['b_route']]
    n_in = 5 + len(consts)
    return pl.pallas_call(
        functools.partial(_merge_kernel, alpha, n_tiles), name="merge",
        grid=(steps,),
        in_specs=[rows(a) for a in (x, oa, ob, om, gt)] + [_const_spec(c.shape) for c in consts]
        + [pl.BlockSpec(memory_space=pl.ANY)] * len(into),
        out_specs=[out(d), pl.BlockSpec((tm * TILE_ROWS, LANES), lambda i: (i + off, 0)), out(LANES), out(LANES)],
        out_shape=[jax.ShapeDtypeStruct((n_total, d), F32), jax.ShapeDtypeStruct((n_total * TILE_ROWS, LANES), F32),
                   jax.ShapeDtypeStruct((n_total, LANES), jnp.int32), jax.ShapeDtypeStruct((n_total, LANES), F32)],
        input_output_aliases={n_in + k: k for k in range(len(into))},
        compiler_params=_params(("parallel",)),
    )(x, oa, ob, om, gt, *consts, *into)


ROW_DMA_UNROLL = 8
MOE_BUFFERS = 3
DRAIN_STEPS = 2


TILE_ROWS = 8


def _rows_from_tiles(ref, n):
    return jnp.concatenate([ref[pl.ds(s, n, stride=TILE_ROWS), :] for s in range(TILE_ROWS)], axis=1)


def _rows_to_tiles(ref, x):
    for s in range(TILE_ROWS):
        ref[pl.ds(s, x.shape[0], stride=TILE_ROWS), :] = x[:, s * LANES:(s + 1) * LANES]


def _row_copies(asg_ref, base, count, n_asg, x_hbm, buf, y_hbm, sem, gather, unrolled):
    def tile(ref, idx):
        start = idx * TILE_ROWS
        return ref.at[pl.ds(start if isinstance(idx, int) else pl.multiple_of(start, TILE_ROWS), TILE_ROWS)]

    def one(r, priority):
        a = asg_ref[base + r]
        if gather:
            tok = jnp.minimum(a, n_asg - 1)
            tok = jnp.where(tok >= n_asg // 2, tok - n_asg // 2, tok)
            copy = pltpu.make_async_copy(tile(x_hbm, tok), tile(buf, r), sem)
        else:
            copy = pltpu.make_async_copy(tile(buf, r), tile(y_hbm, a), sem)
        copy.start(priority=priority)

    if unrolled:
        for r in range(count):
            one(r, r % 2)
        return

    def body(g, carry):
        for j in range(ROW_DMA_UNROLL):
            one(g * ROW_DMA_UNROLL + j, j % 2)
        return carry
    lax.fori_loop(0, count // ROW_DMA_UNROLL, body, 0)


def _moe_expert_kernel(n_asg, asg_ref, be_ref, nu_ref, x_hbm, wg_ref, wu_ref, wd_ref, y_hbm,
                       xbuf, ybuf, wgb, wub, wdb, gsem, ssem):
    i = pl.program_id(0)
    used = nu_ref[0]
    rows = xbuf.shape[1] // TILE_ROWS
    n_buf = xbuf.shape[0]
    n_blocks = be_ref.shape[0]
    blk = jnp.minimum(i, n_blocks - 1)

    @pl.when((i == 0) | (be_ref[blk] != be_ref[jnp.maximum(blk - 1, 0)]))
    def _():
        wgb[...] = wg_ref[0, 0].astype(BF16)
        wub[...] = wu_ref[0, 0].astype(BF16)
        wdb[...] = wd_ref[0, 0].astype(BF16)

    def wait_gather(slot):
        pltpu.make_async_copy(x_hbm.at[pl.ds(0, rows * TILE_ROWS)], xbuf.at[slot], gsem.at[slot]).wait()

    def wait_scatter(slot):
        pltpu.make_async_copy(ybuf.at[slot], y_hbm.at[pl.ds(0, rows * TILE_ROWS)], ssem.at[slot]).wait()

    def gather(b, slot, unrolled):
        _row_copies(asg_ref, b * rows, rows, n_asg, x_hbm, xbuf.at[slot], y_hbm, gsem.at[slot], True, unrolled)

    def scatter(b, slot, unrolled):
        _row_copies(asg_ref, b * rows, rows, n_asg, x_hbm, ybuf.at[slot], y_hbm, ssem.at[slot], False, unrolled)

    def expert(slot):
        xb = _rows_from_tiles(xbuf.at[slot], rows).astype(BF16)
        hg = jnp.dot(xb, wgb[...], preferred_element_type=F32)
        hu = jnp.dot(xb, wub[...], preferred_element_type=F32)
        h = hg * _sigmoid(hg) * hu
        _rows_to_tiles(ybuf.at[slot], jnp.dot(h.astype(BF16), wdb[...], preferred_element_type=F32))

    @pl.when(i == 0)
    def _():
        ybuf[1] = jnp.zeros(ybuf.shape[1:], F32)
        spare = [pltpu.make_async_copy(ybuf.at[1],
                                       y_hbm.at[pl.ds((n_asg + j * rows) * TILE_ROWS, rows * TILE_ROWS)], ssem.at[1])
                 for j in range((y_hbm.shape[0] // TILE_ROWS - n_asg) // rows)]
        for copy in spare:
            copy.start()
        for copy in spare:
            copy.wait()

    @pl.when((i >= 2) & (i - 2 < used))
    def _():
        wait_scatter((i - 2) % n_buf)

    steady = (i >= 1) & (i + 2 < used)
    for slot in range(n_buf):
        @pl.when(steady & (i % n_buf == slot))
        def _():
            wait_gather(slot)
            gather(i + 2, (slot + 2) % n_buf, True)
            scatter(i - 1, (slot - 1) % n_buf, True)
            expert(slot)

    @pl.when(jnp.logical_not(steady))
    def _():
        slot = i % n_buf

        @pl.when(i == 0)
        def _():
            for b in range(2):
                @pl.when(b < used)
                def _():
                    gather(b, b, False)

        @pl.when(i < used)
        def _():
            wait_gather(slot)

        @pl.when(i + 2 < used)
        def _():
            gather(i + 2, (i + 2) % n_buf, False)

        @pl.when((i >= 1) & (i - 1 < used))
        def _():
            scatter(i - 1, (i - 1) % n_buf, False)

        @pl.when(i < used)
        def _():
            expert(slot)


def moe_experts(x1_tiles, n_asg, asg, blk_e, n_used, e_gate, e_up, e_down):
    d = e_gate.shape[2]
    assert d == TILE_ROWS * LANES and x1_tiles.shape[1] == LANES
    n_blocks = blk_e.shape[0]
    ff = e_gate.shape[-1]
    n_rows = n_asg + e_gate.shape[1] * EXPERT_BLOCK
    weight = lambda shape: pl.BlockSpec(
        (1, 1) + shape, lambda i, asg, be, nu: (0, be[jnp.minimum(i, n_blocks - 1)], 0, 0))
    return pl.pallas_call(
        functools.partial(_moe_expert_kernel, n_asg), name="moe_expert",
        grid_spec=pltpu.PrefetchScalarGridSpec(
            num_scalar_prefetch=3,
            grid=(n_blocks + DRAIN_STEPS,),
            in_specs=[pl.BlockSpec(memory_space=pl.ANY), weight((d, ff)), weight((d, ff)), weight((ff, d))],
            out_specs=pl.BlockSpec(memory_space=pl.ANY),
            scratch_shapes=[pltpu.VMEM((MOE_BUFFERS, EXPERT_BLOCK * TILE_ROWS, LANES), F32),
                            pltpu.VMEM((MOE_BUFFERS, EXPERT_BLOCK * TILE_ROWS, LANES), F32),
                            pltpu.VMEM((d, ff), BF16), pltpu.VMEM((d, ff), BF16), pltpu.VMEM((ff, d), BF16),
                            pltpu.SemaphoreType.DMA((MOE_BUFFERS,)), pltpu.SemaphoreType.DMA((MOE_BUFFERS,))]),
        out_shape=jax.ShapeDtypeStruct((n_rows * TILE_ROWS, LANES), F32),
        compiler_params=_params(("arbitrary",)),
    )(asg, blk_e, n_used, x1_tiles, e_gate, e_up, e_down)


def _moe_combine_kernel(alpha, lead_tiles, y0_ref, y1_ref, x1_ref, gate_ref, g_ref, b_ref, lead_ref, tail_ref):
    i = pl.program_id(0)
    gate = gate_ref[...]
    tm = x1_ref.shape[0]
    moe = gate[:, 0:1] * _rows_from_tiles(y0_ref, tm) + gate[:, 1:2] * _rows_from_tiles(y1_ref, tm)
    out = _ln(alpha * x1_ref[...] + moe, g_ref[...], b_ref[...])

    @pl.when(i < lead_tiles)
    def _():
        lead_ref[...] = out

    @pl.when(i >= lead_tiles)
    def _():
        tail_ref[...] = out


def moe_combine(y, x1, gate, g, b, tm, alpha, n, n_lead):
    d = x1.shape[1]
    assert n_lead % tm == 0 and (n - n_lead) % tm == 0 and n > n_lead
    lead_tiles = n_lead // tm
    return pl.pallas_call(
        functools.partial(_moe_combine_kernel, alpha, lead_tiles), name="moe_combine",
        grid=(n // tm,),
        in_specs=[pl.BlockSpec((tm * TILE_ROWS, LANES), lambda i: (i, 0)),
                  pl.BlockSpec((tm * TILE_ROWS, LANES), lambda i: (i + n // tm, 0)),
                  pl.BlockSpec((tm, d), lambda i: (i, 0)),
                  pl.BlockSpec((tm, LANES), lambda i: (i, 0)), _const_spec((1, d)), _const_spec((1, d))],
        out_specs=[pl.BlockSpec((tm, d), lambda i: (jnp.minimum(i, lead_tiles - 1), 0)),
                   pl.BlockSpec((tm, d), lambda i: (jnp.maximum(i - lead_tiles, 0), 0))],
        out_shape=[jax.ShapeDtypeStruct((n_lead, d), F32), jax.ShapeDtypeStruct((n - n_lead, d), F32)],
        compiler_params=_params(("arbitrary",)),
    )(y, y, x1, gate, g.reshape(1, d), b.reshape(1, d))


def moe_routing(experts):
    n, top_k = experts.shape
    n_exp = N_GROUPS * EXPERTS_PER_GROUP
    a = n * top_k
    flat_e = experts.T.reshape(a)
    order = jnp.argsort(flat_e, stable=True).astype(jnp.int32)
    counts = jnp.sum((flat_e[:, None] == jnp.arange(n_exp, dtype=jnp.int32)[None, :]).astype(jnp.int32), axis=0)
    ends = jnp.cumsum(counts)
    padded = (counts + EXPERT_BLOCK - 1) // EXPERT_BLOCK * EXPERT_BLOCK
    pad_end = jnp.cumsum(padded)
    n_blocks = -(-a // EXPERT_BLOCK) + n_exp
    blk_start = jnp.arange(n_blocks, dtype=jnp.int32) * EXPERT_BLOCK
    blk_e = jnp.minimum(jnp.sum((pad_end[None, :] <= blk_start[:, None]).astype(jnp.int32), axis=1), n_exp - 1)
    slot = jnp.arange(n_blocks * EXPERT_BLOCK, dtype=jnp.int32)
    slot_e = jnp.repeat(blk_e, EXPERT_BLOCK)
    rank = slot - (pad_end - padded)[slot_e]
    spare = a + jnp.clip(slot - ends[slot_e], 0, n_exp * EXPERT_BLOCK - 1)
    asg = jnp.where(rank < counts[slot_e], order[jnp.clip((ends - counts)[slot_e] + rank, 0, a - 1)], spare)
    n_used = (pad_end[-1:] // EXPERT_BLOCK).astype(jnp.int32)
    return asg.astype(jnp.int32), blk_e.astype(jnp.int32), n_used


def hier_moe_ln(x1, x1_tiles, eidx, gate, w, tm, alpha, n_tokens, n_lead):
    asg, blk_e, n_used = moe_routing(eidx[:n_tokens, :2])
    y = moe_experts(x1_tiles, 2 * n_tokens, asg, blk_e, n_used, w['e_gate'], w['e_up'], w['e_down'])
    return moe_combine(y, x1, gate, w['ln2_g'], w['ln2_b'], tm, alpha, n_tokens, n_lead)


def kernel(x_prompt, x_sample, mem_prompt, state_wkv, state_shift, cache_win_k, cache_win_v, cache_mem_k, cache_mem_v, ln_in_g, ln_in_b, w_in, mu, w0, w_up, a0, a_up, g_up, k_k, k_a, r_k, lnx_g, lnx_b, sinks, w_mem_kv, p_a, p_b, p_m, w_o, ln1_g, ln1_b, w_group, b_group, w_router, b_router, e_gate, e_up, e_down, ln2_g, ln2_b):
    depth = w_in.shape[0]
    assert depth == 1, "single-layer step"
    bsz, seq, d = x_prompt.shape
    dec = x_sample.shape[0]
    assert x_sample.shape[1] == 1
    c_shift = mu.shape[-1]
    c_a = w0.shape[-1]
    window, kv_w = cache_win_k.shape[2], cache_win_k.shape[3] * cache_win_k.shape[4]
    n_mem, mem_w = cache_mem_k.shape[2], cache_mem_k.shape[3] * cache_mem_k.shape[4]
    q_w = sinks.shape[-1] * HEAD
    qkvm_w = q_w + 2 * kv_w + mem_w
    alpha = (2.0 * depth) ** 0.25
    past_len = float(PAST_LEN)
    chunk = 64

    w_in_b = w_in[0].astype(BF16)
    w_parts = [w_in_b[:, :c_shift], w_in_b[:, c_shift:c_shift + qkvm_w], w_in_b[:, c_shift + qkvm_w:]]
    rp = dict(mu=mu[0], w0=w0[0], w_up=w_up[0], a0=a0[0], a_up=a_up[0], g_up=g_up[0], k_k=k_k[0], k_a=k_a[0],
              r_k=r_k[0].reshape(-1))
    n_route = N_GROUPS * (1 + EXPERTS_PER_GROUP)
    mw = dict(ln_in_g=ln_in_g.reshape(1, d), ln_in_b=ln_in_b.reshape(1, d), p_a=p_a[0].astype(BF16),
              p_b=p_b[0].astype(BF16), p_m=p_m[0].astype(BF16), w_o=w_o[0].astype(BF16),
              ln1_g=ln1_g[0].reshape(1, d), ln1_b=ln1_b[0].reshape(1, d),
              w_route=jnp.stack(_pieces(jnp.pad(jnp.concatenate([w_group[0], w_router[0]], axis=1),
                                                ((0, 0), (0, LANES - n_route))), 2)),
              b_route=jnp.pad(jnp.concatenate([b_group[0], b_router[0]]), (0, LANES - n_route)).reshape(1, LANES),
              e_gate=e_gate, e_up=e_up, e_down=e_down, ln2_g=ln2_g[0], ln2_b=ln2_b[0])

    xp = x_prompt.reshape(bsz * seq, d)
    zq, zg, shift_p, prep = ln_proj_prep(x_prompt, ln_in_g, ln_in_b, w_parts, jnp.zeros((bsz, 1, c_shift), F32), rp,
                                         chunk, 256, BF16)
    o_a, wkv_p = wkv(prep, jnp.zeros((bsz, c_a // HEAD, HEAD, HEAD), F32), lnx_g[0], lnx_b[0], chunk)
    mkv = proj(mem_prompt.reshape(bsz * n_mem, d), w_mem_kv[0].astype(BF16), 256).reshape(bsz, n_mem, 2 * mem_w)
    mk_p, mv_p = mkv[..., :mem_w], mkv[..., mem_w:]
    tables = rope_tables(jnp.arange(seq, dtype=F32))
    o_b, o_m, k_rot = attn_prompt(zq, mk_p, mv_p, sinks[0], tables, window, q_w, kv_w)
    n_all = bsz * seq + dec
    n_buf = -(-n_all // 256) * 256
    routed = merge(xp, o_a.reshape(-1, c_a), o_b.reshape(-1, q_w), o_m.reshape(-1, mem_w),
                   zg.reshape(bsz * seq, -1), mw, 256, alpha, n_buf)
    shift_p = shift_p[:, 0]
    kb_p = k_rot[:, -window:].reshape(bsz, window, H_KV, HEAD)
    vb_p = zq[:, -window:, q_w + kv_w:q_w + 2 * kv_w].reshape(bsz, window, H_KV, HEAD)

    xs = x_sample.reshape(dec, d)
    zq_s, zg_s, zr_s, ops_s = ln_proj_prep(xs.reshape(1, dec, d), ln_in_g, ln_in_b, w_parts,
                                           state_shift[0].reshape(1, dec, c_shift), rp, 1, dec, F32)
    zq_s, zg_s, zr_s = zq_s[0], zg_s[0], zr_s[0]
    o_a_s, wkv_s = wkv_step([a.reshape(dec, c_a) for a in ops_s], state_wkv[0], lnx_g[0], lnx_b[0], 8)
    tables_s = rope_tables(jnp.full((1,), past_len, F32))
    o_b_s, o_m_s, nk_s, nv_s = attn_step(
        zq_s, cache_win_k[0].reshape(dec, window, kv_w), cache_win_v[0].reshape(dec, window, kv_w),
        cache_mem_k[0].reshape(dec, n_mem, mem_w), cache_mem_v[0].reshape(dec, n_mem, mem_w),
        sinks[0], tables_s, q_w, kv_w, 8)
    x1, x1_tiles, eidx, gate = merge(xs, o_a_s, o_b_s, o_m_s, zg_s, mw, dec, alpha, n_buf, bsz * seq, routed)

    y_prompt, y_sample = hier_moe_ln(x1, x1_tiles, eidx, gate, mw, dec, alpha, n_all, bsz * seq)
    y_prompt = y_prompt.reshape(bsz, seq, d)
    y_sample = y_sample.reshape(dec, 1, d)

    sd = state_wkv.dtype
    return (y_prompt, y_sample, wkv_p[None].astype(sd), wkv_s[None].astype(sd), shift_p[None], zr_s[None],
            kb_p[None], vb_p[None], nk_s.reshape(dec, window, H_KV, HEAD)[None],
            nv_s.reshape(dec, window, H_KV, HEAD)[None],
            mk_p.reshape(bsz, n_mem, -1, HEAD)[None], mv_p.reshape(bsz, n_mem, -1, HEAD)[None])
```

```python
import functools
import math

import jax
import jax.numpy as jnp
from jax import lax
from jax.experimental import pallas as pl
from jax.experimental.pallas import tpu as pltpu

F32 = jnp.float32
BF16 = jnp.bfloat16
SCAN_BATCH = 8
PROJ_PIECE = 512
ATTN_BLOCKS = 4
CHUNKS_PER_STEP = 4

HEAD = 64
LANES = 128
H_KV = 2
ROT_HALF = 8
ROPE_THETA = 500000.0
PAST_LEN = 8192
N_GROUPS = 4
EXPERTS_PER_GROUP = 8
EXPERT_BLOCK = 128
LN_EPS = 1e-5
LNX_EPS = 64e-5
NEG_INF = -1e30
VMEM_LIMIT = 48 * 1024 * 1024


def _pieces(x, n):
    out = []
    for _ in range(n):
        p = x.astype(BF16)
        out.append(p)
        x = x - p.astype(F32)
    return out


def _mask_dot(x, mask, n=2):
    return sum(jnp.dot(p, mask, preferred_element_type=F32) for p in _pieces(x, n))


def _split3(x, axis, lhs):
    hi = x.astype(BF16).astype(F32)
    lo = x - hi
    return jnp.concatenate([hi, hi, lo] if lhs else [hi, lo, hi], axis=axis).astype(BF16)


def _dot3(a, b):
    return jnp.dot(_split3(a, 1, True), _split3(b, 0, False), preferred_element_type=F32)


def _dot3_t(a, b):
    return lax.dot_general(_split3(a, 1, True), _split3(b, 1, False), (((1,), (1,)), ((), ())),
                           preferred_element_type=F32)


def _bdot(a, b):
    return jnp.dot(a.astype(BF16), b.astype(BF16), preferred_element_type=F32)


def _bdot_t(a, b):
    return lax.dot_general(a.astype(BF16), b.astype(BF16), (((1,), (1,)), ((), ())), preferred_element_type=F32)


def _sigmoid(x):
    return 0.5 * jnp.tanh(0.5 * x) + 0.5


def _ln(x, g, b):
    mu = jnp.mean(x, axis=-1, keepdims=True)
    xc = x - mu
    var = jnp.mean(xc * xc, axis=-1, keepdims=True)
    return xc * lax.rsqrt(var + LN_EPS) * g + b


def _const_spec(shape):
    nd = len(shape)
    return pl.BlockSpec(shape, lambda *_: (0,) * nd)


def _params(sem):
    return pltpu.CompilerParams(dimension_semantics=sem, vmem_limit_bytes=VMEM_LIMIT)


def _ln_proj_prep_kernel(chunk, x_ref, g_ref, b_ref, wr_ref, wq_ref, wg_ref, prev_ref, mu_ref, w0_ref, wup_ref,
                         a0_ref, aup_ref, gup_ref, kk_ref, ka_ref, rk_ref, hsum_ref, tril_ref,
                         zq_ref, zg_ref, zr_ref, *refs):
    out_refs, carry_ref = refs[:-1], refs[-1]
    xn = _ln(x_ref[0], g_ref[...], b_ref[...]).astype(BF16)
    z = jnp.dot(xn, wr_ref[...], preferred_element_type=F32)
    pending = [(o_ref, w_ref, c) for o_ref, w_ref in ((zq_ref, wq_ref), (zg_ref, wg_ref))
               for c in range(0, w_ref.shape[1], PROJ_PIECE)]

    def project(n_pieces):
        for _ in range(min(n_pieces, len(pending))):
            o_ref, w_ref, c = pending.pop(0)
            o_ref[0, :, c:c + PROJ_PIECE] = jnp.dot(xn, w_ref[:, c:c + PROJ_PIECE],
                                                    preferred_element_type=F32).astype(o_ref.dtype)

    tt = z.shape[0]
    c_a = w0_ref.shape[-1]
    r_w, r_a, r_g = wup_ref.shape[0], aup_ref.shape[0], gup_ref.shape[0]
    if chunk == 1:
        zr_ref[0] = z
        prev = prev_ref[0]
    else:
        zr_ref[0] = z[tt - 1:tt, :]

        @pl.when(pl.program_id(1) == 0)
        def _():
            carry_ref[...] = prev_ref[0]

        row = lax.broadcasted_iota(jnp.int32, z.shape, 0)
        prev = jnp.where(row == 0, carry_ref[...], pltpu.roll(z, 1, 0))
        carry_ref[...] = z[tt - 1:tt, :]
    zs = z + (prev - z) * mu_ref[...]
    r = zs[:, :c_a]
    k = zs[:, c_a:2 * c_a]
    v = zs[:, 2 * c_a:3 * c_a]
    o = 3 * c_a
    xw = zs[:, o:o + r_w]
    xa = zs[:, o + r_w:o + r_w + r_a]
    xg = zs[:, o + r_w + r_a:o + r_w + r_a + r_g]
    project(1)
    warg = -(w0_ref[...] + _dot3(jnp.tanh(xw), wup_ref[...]))
    softplus = jnp.maximum(warg, 0.0) + jnp.log1p(jnp.exp(-jnp.abs(warg)))
    lw = -jnp.exp(-softplus - 0.5)
    project(1)
    a = _sigmoid(a0_ref[...] + _dot3(xa, aup_ref[...]))
    g = _dot3(_sigmoid(xg), gup_ref[...])
    project(1)
    kkr = k * kk_ref[...]
    kk = kkr / jnp.maximum(jnp.sqrt(_mask_dot(kkr * kkr, hsum_ref[...])), 1e-12)
    project(1)
    k2 = k * (1.0 + (a - 1.0) * ka_ref[...])
    bonus = _mask_dot(r * k2 * rk_ref[...], hsum_ref[...]) * v
    kb = kk * a
    project(1)
    if chunk == 1:
        outs = (r, jnp.exp(lw), k2, v, -kk, kb, g, bonus)
        for ref, val in zip(out_refs, outs):
            ref[0] = val
        project(len(pending))
        return
    pieces = _pieces(lw, 3)
    ones3 = jnp.ones((chunk, 3 * chunk), BF16)
    cw, cwl = [], []
    for c in range(tt // chunk):
        stack = jnp.concatenate([p[c * chunk:(c + 1) * chunk] for p in pieces], axis=0)
        cw.append(jnp.dot(tril_ref[...], stack, preferred_element_type=F32))
        cwl.append(jnp.dot(ones3, stack, preferred_element_type=F32))
    cw = jnp.concatenate(cw, axis=0)
    cwl = jnp.concatenate(cwl, axis=0)
    project(1)
    e_inv = jnp.exp(-cw)
    outs = (r * jnp.exp(cw), -kk * jnp.exp(cw - lw), kb * e_inv, k2 * e_inv, v, g, bonus)
    for ref, val in zip(out_refs[:-1], outs):
        ref[0] = val
        project(1)
    wl = jnp.exp(cwl)
    for c in range(tt // chunk):
        out_refs[-1][0, c] = wl[c * chunk:c * chunk + 1, :]
    project(len(pending))


def _head_sum_matrix(width):
    idx = jnp.arange(width)
    return ((idx[:, None] // HEAD) == (idx[None, :] // HEAD)).astype(BF16)


def ln_proj_prep(x, g, b, ws, prev, p, chunk, tt, gate_dtype):
    bsz, t, d = x.shape
    w_r, w_q, w_g = ws
    cs = w_r.shape[1]
    c_a = p['w0'].shape[-1]
    assert prev.shape[1] == (t if chunk == 1 else 1)
    ridx = jnp.arange(chunk)
    tril = jnp.tile((ridx[None, :] <= ridx[:, None]).astype(BF16), (1, 3))
    hsum = _head_sum_matrix(c_a)
    row = lambda a: a.reshape(1, -1)
    rows = lambda width: pl.BlockSpec((1, tt, width), lambda bi, i: (bi, i, 0))
    full = jax.ShapeDtypeStruct((bsz, t, c_a), F32)
    if chunk == 1:
        out_specs, out_shape = [rows(c_a)] * 8, [full] * 8
        prev_spec, zr_spec, zr_rows = rows(cs), rows(cs), t
    else:
        out_specs = [rows(c_a)] * 7 + [pl.BlockSpec((1, tt // chunk, 1, c_a), lambda bi, i: (bi, i, 0, 0))]
        out_shape = [full] * 7 + [jax.ShapeDtypeStruct((bsz, t // chunk, 1, c_a), F32)]
        prev_spec = zr_spec = pl.BlockSpec((1, 1, cs), lambda bi, i: (bi, 0, 0))
        zr_rows = 1
    consts = [row(p['mu']), row(p['w0']), p['w_up'], row(p['a0']), p['a_up'], p['g_up'], row(p['k_k']),
              row(p['k_a']), row(p['r_k']), hsum, tril]
    weight = lambda w: pl.BlockSpec(w.shape, lambda bi, i: (0, 0), pipeline_mode=pl.Buffered(1))
    outs = pl.pallas_call(
        functools.partial(_ln_proj_prep_kernel, chunk), name="ln_proj_prep",
        grid=(bsz, t // tt),
        in_specs=[rows(d), _const_spec((1, d)), _const_spec((1, d)), weight(w_r), weight(w_q), weight(w_g), prev_spec]
        + [_const_spec(c.shape) for c in consts],
        out_specs=[rows(w_q.shape[1]), rows(w_g.shape[1]), zr_spec] + out_specs,
        out_shape=[jax.ShapeDtypeStruct((bsz, t, w_q.shape[1]), F32),
                   jax.ShapeDtypeStruct((bsz, t, w_g.shape[1]), gate_dtype),
                   jax.ShapeDtypeStruct((bsz, zr_rows, cs), F32)] + out_shape,
        scratch_shapes=[pltpu.VMEM((1, cs), F32)],
        compiler_params=_params(("parallel", "arbitrary")),
    )(x, row(g), row(b), w_r, w_q, w_g, prev, *consts)
    return outs[0], outs[1], outs[2], outs[3:]


def _wkv_chunk_kernel(ra_ref, at_ref, bt_ref, kt_ref, v_ref, wl_ref, rp_ref, y0_ref, m_ref, n_ref):
    n_chunks = m_ref.shape[1]
    length = ra_ref.shape[1] // n_chunks
    heads = ra_ref.shape[2] // HEAD
    row = lax.broadcasted_iota(jnp.int32, (length, length), 0)
    col = lax.broadcasted_iota(jnp.int32, (length, length), 1)
    strict = row > col
    incl = row >= col
    hrow = lax.broadcasted_iota(jnp.int32, (HEAD, HEAD), 0)
    hcol = lax.broadcasted_iota(jnp.int32, (HEAD, HEAD), 1)
    units = [(c, h) for c in range(n_chunks) for h in range(heads)]
    us = range(len(units))
    at = lambda ref, u: ref[0, units[u][0] * length:(units[u][0] + 1) * length,
                            units[u][1] * HEAD:(units[u][1] + 1) * HEAD]
    gram = [_bdot_t(jnp.concatenate([at(at_ref, u), at(ra_ref, u)], axis=0),
                    jnp.concatenate([at(bt_ref, u), at(kt_ref, u)], axis=0)) for u in us]
    a_ab = [jnp.where(strict, g[:length, :length], 0.0) for g in gram]
    a_kk = [jnp.concatenate([jnp.where(strict, g[:length, length:], 0.0),
                             jnp.where(incl, g[length:, length:], 0.0)], axis=0) for g in gram]
    a_rb = [jnp.where(incl, g[length:, :length], 0.0) for g in gram]
    inv = [jnp.where(row == col, 1.0, a) for a in a_ab]
    pw = [_bdot(a, a) for a in a_ab]
    avy = [_bdot(a_kk[u], at(v_ref, u)) for u in us]
    w_l = [wl_ref[0, units[u][0], :, units[u][1] * HEAD:(units[u][1] + 1) * HEAD] for u in us]
    nk = [_dot3(at(v_ref, u).T, at(kt_ref, u) * w_l[u]) for u in us]
    for _ in range(int(math.log2(length)) - 2):
        both = [_bdot(jnp.concatenate([pw[u], inv[u]], axis=0), pw[u]) for u in us]
        pw = [b[:length] for b in both]
        inv = [inv[u] + both[u][length:] for u in us]
    inv = [inv[u] + _bdot(inv[u], pw[u]) for u in us]
    pq = [_bdot(inv[u], jnp.concatenate([at(at_ref, u), avy[u][:length]], axis=1)) for u in us]
    ry = [_bdot(a_rb[u], pq[u]) for u in us]
    mn = [_dot3(pq[u].T, at(bt_ref, u) * w_l[u]) for u in us]
    lanes = lambda f, c: jnp.concatenate([f(c * heads + h) for h in range(heads)], axis=1)
    rows = lambda f: jnp.concatenate([lanes(f, c) for c in range(n_chunks)], axis=0)
    rp_ref[0] = rows(lambda u: at(ra_ref, u) + ry[u][:, :HEAD])
    y0_ref[0] = rows(lambda u: avy[u][length:] + ry[u][:, HEAD:])
    for c in range(n_chunks):
        m_ref[0, c] = lanes(lambda u: jnp.where(hrow == hcol, w_l[u], 0.0) + mn[u][:HEAD], c)
        n_ref[0, c] = lanes(lambda u: nk[u] + mn[u][HEAD:], c)


def _head_norm_wide(y, hsum):
    yc = y - _mask_dot(y, hsum) * (1.0 / HEAD)
    return yc * lax.rsqrt(_mask_dot(yc * yc, hsum) * (1.0 / HEAD) + LNX_EPS)


def _wkv_scan_kernel(rp_ref, y0_ref, m_ref, n_ref, g_ref, bonus_ref, s0_ref, lg_ref, lb_ref, hsum_ref,
                     o_ref, sout_ref, s_ref):
    c = pl.program_id(1)
    nb = rp_ref.shape[0]
    heads = rp_ref.shape[2] // HEAD

    @pl.when(c == 0)
    def _():
        s_ref[...] = s0_ref[...]

    pairs = [(b, h, slice(h * HEAD, (h + 1) * HEAD)) for b in range(nb) for h in range(heads)]
    s = [s_ref[b, h] for b, h, _ in pairs]
    s_new = [n_ref[b, 0, :, sl] + _dot3(s[j], m_ref[b, 0, :, sl]) for j, (b, h, sl) in enumerate(pairs)]
    ys = [y0_ref[b, :, sl] + _bdot_t(rp_ref[b, :, sl], s[j]) for j, (b, h, sl) in enumerate(pairs)]
    for j, (b, h, _) in enumerate(pairs):
        s_ref[b, h] = s_new[j]
    length = rp_ref.shape[1]
    y = _head_norm_wide(jnp.concatenate([jnp.concatenate(ys[b * heads:(b + 1) * heads], axis=-1) for b in range(nb)],
                                        axis=0), hsum_ref[...])
    for b in range(nb):
        o = (y[b * length:(b + 1) * length] * lg_ref[...] + lb_ref[...] + bonus_ref[b]) * g_ref[b]
        o_ref[b] = o.astype(o_ref.dtype)

    @pl.when(c == pl.num_programs(1) - 1)
    def _():
        sout_ref[...] = s_ref[...]


def wkv(prep, s0, lnx_g, lnx_b, chunk):
    ra, at, bt, kt, v, g, bonus, wl = prep
    bsz, t, c_a = ra.shape
    heads = c_a // HEAD
    n_chunks = t // chunk
    assert chunk & (chunk - 1) == 0 and chunk >= 4 and t % chunk == 0
    per_step = CHUNKS_PER_STEP if n_chunks % CHUNKS_PER_STEP == 0 else 1
    tile = pl.BlockSpec((1, per_step * chunk, c_a), lambda b, c: (b, c, 0))
    mat = pl.BlockSpec((1, per_step, HEAD, c_a), lambda b, c: (b, c, 0, 0))
    full = jax.ShapeDtypeStruct((bsz, t, c_a), F32)
    mats = jax.ShapeDtypeStruct((bsz, n_chunks, HEAD, c_a), F32)
    rp, y0, m, n = pl.pallas_call(
        _wkv_chunk_kernel, name="wkv_chunk",
        grid=(bsz, n_chunks // per_step),
        in_specs=[tile] * 5 + [pl.BlockSpec((1, per_step, 1, c_a), lambda b, c: (b, c, 0, 0))],
        out_specs=[tile, tile, mat, mat],
        out_shape=[full, full, mats, mats],
        compiler_params=_params(("parallel", "parallel")),
    )(ra, at, bt, kt, v, wl)
    nb = SCAN_BATCH if bsz % SCAN_BATCH == 0 else 1
    tile = pl.BlockSpec((nb, chunk, c_a), lambda b, c: (b, c, 0))
    mat = pl.BlockSpec((nb, 1, HEAD, c_a), lambda b, c: (b, c, 0, 0))
    st = pl.BlockSpec((nb, heads, HEAD, HEAD), lambda b, c: (b, 0, 0, 0))
    return pl.pallas_call(
        _wkv_scan_kernel, name="wkv_scan",
        grid=(bsz // nb, n_chunks),
        in_specs=[tile, tile, mat, mat, tile, tile, st, _const_spec((1, c_a)), _const_spec((1, c_a)),
                  _const_spec((c_a, c_a))],
        out_specs=[tile, st],
        out_shape=[jax.ShapeDtypeStruct((bsz, t, c_a), BF16), jax.ShapeDtypeStruct(s0.shape, F32)],
        scratch_shapes=[pltpu.VMEM((nb, heads, HEAD, HEAD), F32)],
        compiler_params=_params(("parallel", "arbitrary")),
    )(rp, y0, m, n, g, bonus, s0, lnx_g.reshape(1, c_a), lnx_b.reshape(1, c_a), _head_sum_matrix(c_a))


def _wkv_step_kernel(r_ref, w_ref, k_ref, v_ref, a_ref, b_ref, g_ref, bonus_ref, s_ref, lg_ref, lb_ref, hsum_ref,
                     o_ref, sout_ref):
    bt, heads = s_ref.shape[0], s_ref.shape[1]
    c_a = heads * HEAD
    hsum = hsum_ref[...]
    diag = (lax.broadcasted_iota(jnp.int32, (HEAD, c_a), 1) % HEAD
            == lax.broadcasted_iota(jnp.int32, (HEAD, c_a), 0))
    seqs = range(bt)
    row = lambda ref, i: ref[i:i + 1, :]
    stack = lambda f: jnp.concatenate([f(i) for i in seqs], axis=0)
    piece = lambda x, i: x[i * HEAD:(i + 1) * HEAD]
    s = stack(lambda i: jnp.concatenate([s_ref[i, h] for h in range(heads)], axis=1))
    sa = _mask_dot(s * stack(lambda i: jnp.broadcast_to(row(a_ref, i), (HEAD, c_a))), hsum)
    v_rows = _mask_dot(stack(lambda i: jnp.where(diag, row(v_ref, i), 0.0)), hsum)
    s = stack(lambda i: piece(s, i) * row(w_ref, i) + piece(sa, i) * row(b_ref, i) + piece(v_rows, i) * row(k_ref, i))
    for i in seqs:
        for h in range(heads):
            sout_ref[i, h] = piece(s, i)[:, h * HEAD:(h + 1) * HEAD]
    y_rows = _mask_dot(stack(lambda i: piece(s, i) * row(r_ref, i)), hsum)
    y = stack(lambda i: jnp.sum(jnp.where(diag, piece(y_rows, i), 0.0), axis=0, keepdims=True))
    y = _head_norm_wide(y, hsum)
    o_ref[...] = (y * lg_ref[...] + lb_ref[...] + bonus_ref[...]) * g_ref[...]


def wkv_step(ops, s0, lnx_g, lnx_b, bt):
    bsz, c_a = ops[0].shape
    rows = pl.BlockSpec((bt, c_a), lambda i: (i, 0))
    st = pl.BlockSpec((bt,) + s0.shape[1:], lambda i: (i, 0, 0, 0))
    hsum = _head_sum_matrix(c_a)
    return pl.pallas_call(
        _wkv_step_kernel, name="wkv_step",
        grid=(bsz // bt,),
        in_specs=[rows] * 8 + [st, _const_spec((1, c_a)), _const_spec((1, c_a)), _const_spec(hsum.shape)],
        out_specs=[rows, st],
        out_shape=[jax.ShapeDtypeStruct((bsz, c_a), F32), jax.ShapeDtypeStruct(s0.shape, F32)],
        compiler_params=_params(("parallel",)),
    )(*ops, s0, lnx_g.reshape(1, c_a), lnx_b.reshape(1, c_a), hsum)


def rope_tables(pos):
    inv_freq = ROPE_THETA ** (-jnp.arange(ROT_HALF, dtype=F32) / ROT_HALF)
    ang = pos[:, None] * inv_freq[None, :]
    cos, sin = jnp.cos(ang), jnp.sin(ang)
    t = pos.shape[0]
    rest = HEAD - 2 * ROT_HALF
    c = jnp.concatenate([cos, cos, jnp.ones((t, rest), F32)], axis=1)
    s1 = jnp.concatenate([jnp.zeros((t, ROT_HALF), F32), sin, jnp.zeros((t, rest), F32)], axis=1)
    s2 = jnp.concatenate([-sin, jnp.zeros((t, HEAD - ROT_HALF), F32)], axis=1)
    rep = LANES // HEAD
    return tuple(jnp.tile(x, (1, rep)) for x in (c, s1, s2))


def _rope(x, c, s1, s2):
    width = x.shape[-1]
    rep = width // c.shape[-1]
    if rep > 1:
        c, s1, s2 = (jnp.concatenate([t] * rep, axis=1) for t in (c, s1, s2))
    return x * c + pltpu.roll(x, ROT_HALF, 1) * s1 + pltpu.roll(x, width - ROT_HALF, 1) * s2


def _attn_kernel(window, q_w, kv_w, zq_ref, kp_ref, vp_ref, rc_ref, rs1_ref, rs2_ref, pc_ref, ps1_ref, ps2_ref,
                 mk_ref, mv_ref, sink_ref, ob_ref, om_ref, kr_ref):
    zq = zq_ref[0]
    nblk = zq.shape[0] // window
    first = pl.program_id(1) * nblk
    scale = HEAD ** -0.5
    q = zq[:, :q_w]
    k = zq[:, q_w:q_w + kv_w]
    v = zq[:, q_w + kv_w:q_w + 2 * kv_w]
    qm = zq[:, q_w + 2 * kv_w:]
    qr = _rope(q, rc_ref[...], rs1_ref[...], rs2_ref[...]) * scale
    kr = _rope(k, rc_ref[...], rs1_ref[...], rs2_ref[...])
    kr_ref[0] = kr
    k_all = jnp.concatenate([_rope(kp_ref[0], pc_ref[...], ps1_ref[...], ps2_ref[...]), kr], axis=0)
    v_all = jnp.concatenate([vp_ref[0], v], axis=0)
    gqa = q_w // kv_w
    assert window & (window - 1) == 0
    qi = lax.broadcasted_iota(jnp.int32, (gqa * window, 2 * window), 0) & (window - 1)
    kj = lax.broadcasted_iota(jnp.int32, (gqa * window, 2 * window), 1)
    band = (kj > qi) & (kj <= qi + window)
    tdot = lambda a, b: lax.dot_general(a, b, (((1,), (1,)), ((), ())), preferred_element_type=F32)
    dot = lambda a, b: jnp.dot(a, b, preferred_element_type=F32)
    hsl = lambda h: slice(h * HEAD, (h + 1) * HEAD)
    rows = lambda i: slice(i * window, (i + 1) * window)
    keys = lambda i: slice(i * window, (i + 2) * window)
    wins = [(i, hk) for i in range(nblk) for hk in range(kv_w // HEAD)]
    mems = [(i, h) for i in range(nblk) for h in range(qm.shape[1] // HEAD)]
    qmb = (qm * scale).astype(BF16)
    mk = mk_ref[0].astype(BF16)
    mv = mv_ref[0].astype(BF16)
    kcat = [k_all[keys(i), hsl(hk)].astype(BF16) for i, hk in wins]
    vcat = [v_all[keys(i), hsl(hk)].astype(BF16) for i, hk in wins]
    qs = [jnp.concatenate([qr[rows(i), hsl(hk * gqa + g)] for g in range(gqa)], axis=0).astype(BF16)
          for i, hk in wins]
    s = [jnp.where(band & ((kj >= window) | (first + i > 0)), tdot(qs[u], kcat[u]), NEG_INF)
         for u, (i, hk) in enumerate(wins)]
    sm = [tdot(qmb[rows(i), hsl(h)], mk[:, hsl(h)]) for i, h in mems]
    sink = [jnp.concatenate([jnp.full((window, 1), sink_ref[hk * gqa + g], F32) for g in range(gqa)], axis=0)
            for i, hk in wins]
    m = [jnp.maximum(jnp.max(s[u], axis=-1, keepdims=True), sink[u]) for u in range(len(wins))]
    p = [jnp.exp(s[u] - m[u]) for u in range(len(wins))]
    pm = [jnp.exp(x - jnp.max(x, axis=-1, keepdims=True)) for x in sm]
    o = [dot(p[u].astype(BF16), vcat[u])
         / (jnp.sum(p[u], axis=-1, keepdims=True) + jnp.exp(sink[u] - m[u])) for u in range(len(wins))]
    om = [dot(pm[u].astype(BF16), mv[:, hsl(h)]) / jnp.sum(pm[u], axis=-1, keepdims=True)
          for u, (i, h) in enumerate(mems)]
    n_kv, n_mh = kv_w // HEAD, qm.shape[1] // HEAD
    ob = jnp.concatenate([jnp.concatenate([o[i * n_kv + hk][g * window:(g + 1) * window]
                                           for hk in range(n_kv) for g in range(gqa)], axis=1)
                          for i in range(nblk)], axis=0)
    ob_ref[0] = ob.astype(ob_ref.dtype)
    om_ref[0] = jnp.concatenate([jnp.concatenate(om[i * n_mh:(i + 1) * n_mh], axis=1) for i in range(nblk)],
                                axis=0).astype(om_ref.dtype)


def attn_prompt(zq, mk, mv, sinks, tables, window, q_w, kv_w):
    bsz, t, zw = zq.shape
    mem_w = zw - q_w - 2 * kv_w
    assert kv_w == LANES and q_w % kv_w == 0
    kcol, vcol = q_w // kv_w, q_w // kv_w + 1
    nblk = ATTN_BLOCKS if (t // window) % ATTN_BLOCKS == 0 else 1
    rows = nblk * window
    prev = lambda n: jnp.maximum(n * nblk - 1, 0)
    tab = pl.BlockSpec((rows, LANES), lambda b, n: (n, 0))
    ptab = pl.BlockSpec((window, LANES), lambda b, n: (prev(n), 0))
    mem = pl.BlockSpec((1,) + mk.shape[1:], lambda b, n: (b, 0, 0))
    return pl.pallas_call(
        functools.partial(_attn_kernel, window, q_w, kv_w), name="attn",
        grid=(bsz, t // rows),
        in_specs=[pl.BlockSpec((1, rows, zw), lambda b, n: (b, n, 0)),
                  pl.BlockSpec((1, window, kv_w), lambda b, n: (b, prev(n), kcol)),
                  pl.BlockSpec((1, window, kv_w), lambda b, n: (b, prev(n), vcol)),
                  tab, tab, tab, ptab, ptab, ptab, mem, mem,
                  pl.BlockSpec(memory_space=pltpu.SMEM)],
        out_specs=[pl.BlockSpec((1, rows, q_w), lambda b, n: (b, n, 0)),
                   pl.BlockSpec((1, rows, mem_w), lambda b, n: (b, n, 0)),
                   pl.BlockSpec((1, rows, kv_w), lambda b, n: (b, n, 0))],
        out_shape=[jax.ShapeDtypeStruct((bsz, t, q_w), BF16), jax.ShapeDtypeStruct((bsz, t, mem_w), BF16),
                   jax.ShapeDtypeStruct((bsz, t, kv_w), F32)],
        compiler_params=_params(("parallel", "parallel")),
    )(zq, zq, zq, *tables, *tables, mk, mv, sinks)


def _attn_step_kernel(q_w, kv_w, zq_ref, ck_ref, cv_ref, mk_ref, mv_ref, rc_ref, rs1_ref, rs2_ref, sink_ref,
                      ob_ref, om_ref, nk_ref, nv_ref):
    bt = zq_ref.shape[0]
    window = ck_ref.shape[1]
    mem_w = om_ref.shape[1]
    n_q, gqa, per_vreg = q_w // HEAD, q_w // kv_w, LANES // HEAD
    scale = HEAD ** -0.5
    zq = zq_ref[...]
    q = _rope(zq[:, :q_w], rc_ref[...], rs1_ref[...], rs2_ref[...]) * scale
    k_new = _rope(zq[:, q_w:q_w + kv_w], rc_ref[...], rs1_ref[...], rs2_ref[...])
    v_new = zq[:, q_w + kv_w:q_w + 2 * kv_w]
    qm = zq[:, q_w + 2 * kv_w:] * scale
    own = lambda w: (lax.broadcasted_iota(jnp.int32, (n_q, w), 1) // HEAD
                     == lax.broadcasted_iota(jnp.int32, (n_q, w), 0))
    own_q, own_m = own(q_w), own(mem_w)
    hrow = lax.broadcasted_iota(jnp.int32, (n_q, LANES), 0)
    hblk = lax.broadcasted_iota(jnp.int32, (n_q, LANES), 1) // HEAD
    swap = (hrow % per_vreg) != (hrow // gqa)
    keep = hblk == hrow % per_vreg
    key_ok = lax.broadcasted_iota(jnp.int32, (n_q, window), 1) >= 1
    wrow = lax.broadcasted_iota(jnp.int32, (window, kv_w), 0)
    sink = sink_ref[...]
    bs = range(bt)
    tdot = lambda a, b: lax.dot_general(a.astype(BF16), b.astype(BF16), (((1,), (1,)), ((), ())),
                                        preferred_element_type=F32)
    for b in bs:
        nk_ref[b] = jnp.where(wrow == window - 1, k_new[b:b + 1], pltpu.roll(ck_ref[b], window - 1, 0))
        nv_ref[b] = jnp.where(wrow == window - 1, v_new[b:b + 1], pltpu.roll(cv_ref[b], window - 1, 0))
    q8 = []
    for b in bs:
        rep = jnp.where(own_q, q[b:b + 1], 0.0)
        fold = sum(rep[:, c * LANES:(c + 1) * LANES] for c in range(q_w // LANES))
        q8.append(jnp.where(swap, pltpu.roll(fold, HEAD, 1), fold))
    qm8 = [jnp.where(own_m, qm[b:b + 1], 0.0) for b in bs]
    s = [jnp.where(key_ok, tdot(q8[b], ck_ref[b]), NEG_INF) for b in bs]
    sm = [tdot(qm8[b], mk_ref[b]) for b in bs]
    s_new = [jnp.sum(q8[b] * k_new[b:b + 1], axis=1, keepdims=True) for b in bs]
    m = [jnp.maximum(jnp.maximum(jnp.max(s[b], axis=1, keepdims=True), s_new[b]), sink) for b in bs]
    p = [jnp.exp(s[b] - m[b]) for b in bs]
    p_new = [jnp.exp(s_new[b] - m[b]) for b in bs]
    pm = [jnp.exp(sm[b] - jnp.max(sm[b], axis=1, keepdims=True)) for b in bs]
    o8 = [(_bdot(p[b], cv_ref[b]) + p_new[b] * v_new[b:b + 1])
          / (jnp.sum(p[b], axis=1, keepdims=True) + p_new[b] + jnp.exp(sink - m[b])) for b in bs]
    om8 = [_bdot(pm[b], mv_ref[b]) / jnp.sum(pm[b], axis=1, keepdims=True) for b in bs]
    ob, om = [], []
    for b in bs:
        o = jnp.where(swap, pltpu.roll(o8[b], HEAD, 1), o8[b])
        o = jnp.concatenate([jnp.where(keep, o, 0.0)] * (q_w // LANES), axis=1)
        ob.append(jnp.sum(jnp.where(own_q, o, 0.0), axis=0, keepdims=True))
        om.append(jnp.sum(jnp.where(own_m, om8[b], 0.0), axis=0, keepdims=True))
    ob_ref[...] = jnp.concatenate(ob, axis=0)
    om_ref[...] = jnp.concatenate(om, axis=0)


def attn_step(zq, ck, cv, mk, mv, sinks, tables, q_w, kv_w, bt):
    bsz, zw = zq.shape
    mem_w = zw - q_w - 2 * kv_w
    window, n_mem = ck.shape[1], mk.shape[1]
    n_q = q_w // HEAD
    assert kv_w == LANES and mem_w // HEAD <= n_q and n_q == 8
    rows = lambda w: pl.BlockSpec((bt, w), lambda i: (i, 0))
    cache = pl.BlockSpec((bt, window, kv_w), lambda i: (i, 0, 0))
    mem = pl.BlockSpec((bt, n_mem, mem_w), lambda i: (i, 0, 0))
    consts = list(tables) + [sinks.reshape(n_q, 1)]
    const_specs = [_const_spec(c.shape) for c in consts]
    return pl.pallas_call(
        functools.partial(_attn_step_kernel, q_w, kv_w), name="attn_step",
        grid=(bsz // bt,),
        in_specs=[rows(zw), cache, cache, mem, mem] + const_specs,
        out_specs=[rows(q_w), rows(mem_w), cache, cache],
        out_shape=[jax.ShapeDtypeStruct((bsz, q_w), F32), jax.ShapeDtypeStruct((bsz, mem_w), F32),
                   jax.ShapeDtypeStruct(ck.shape, F32), jax.ShapeDtypeStruct(cv.shape, F32)],
        compiler_params=_params(("parallel",)),
    )(zq, ck, cv, mk, mv, *consts)


def _proj_kernel(x_ref, w_ref, o_ref):
    o_ref[...] = jnp.dot(x_ref[...].astype(BF16), w_ref[...], preferred_element_type=F32)


def proj(x, w, tm):
    n, d = x.shape
    return pl.pallas_call(
        _proj_kernel, name="proj",
        grid=(n // tm,),
        in_specs=[pl.BlockSpec((tm, d), lambda i: (i, 0)), _const_spec(w.shape)],
        out_specs=pl.BlockSpec((tm, w.shape[1]), lambda i: (i, 0)),
        out_shape=jax.ShapeDtypeStruct((n, w.shape[1]), F32),
        compiler_params=_params(("parallel",)),
    )(x, w)


def _merge_kernel(alpha, n_tiles, x_ref, oa_ref, ob_ref, om_ref, gt_ref, lig_ref, lib_ref, pa_ref, pb_ref, pm_ref,
                  wo_ref, l1g_ref, l1b_ref, wr_ref, br_ref, *refs):
    outs = refs[-4:]

    @pl.when(pl.program_id(0) >= n_tiles)
    def _():
        for ref in outs:
            ref[...] = jnp.zeros_like(ref)

    @pl.when(pl.program_id(0) < n_tiles)
    def _():
        _merge_tile(alpha, x_ref, oa_ref, ob_ref, om_ref, gt_ref, lig_ref, lib_ref, pa_ref, pb_ref, pm_ref, wo_ref,
                    l1g_ref, l1b_ref, wr_ref, br_ref, *outs)


def _merge_tile(alpha, x_ref, oa_ref, ob_ref, om_ref, gt_ref, lig_ref, lib_ref, pa_ref, pb_ref, pm_ref, wo_ref,
                l1g_ref, l1b_ref, wr_ref, br_ref, x1_ref, x1t_ref, eidx_ref, gate_ref):
    d = x_ref.shape[1]
    xn = _ln(x_ref[...], lig_ref[...], lib_ref[...])
    gts = _sigmoid(gt_ref[...].astype(F32))
    merged = (gts[:, :d] * _bdot(oa_ref[...], pa_ref[...]) + gts[:, d:2 * d] * _bdot(ob_ref[...], pb_ref[...])
              + gts[:, 2 * d:] * _bdot(om_ref[...], pm_ref[...]))
    x1 = _ln(alpha * xn + _bdot(merged, wo_ref[...]), l1g_ref[...], l1b_ref[...])
    x1_ref[...] = x1
    _rows_to_tiles(x1t_ref, x1)
    x_hi, x_lo = _pieces(x1, 2)
    w_hi, w_lo = wr_ref[0], wr_ref[1]
    dot = lambda a, b: jnp.dot(a, b, preferred_element_type=F32)
    logits = dot(x_hi, w_hi) + dot(x_hi, w_lo) + dot(x_lo, w_hi) + br_ref[...]
    lane = lax.broadcasted_iota(jnp.int32, logits.shape, 1)
    lane_f = lane.astype(F32)
    first = lambda hit: jnp.min(jnp.where(hit, lane_f, float(LANES)), axis=-1, keepdims=True).astype(jnp.int32)
    gmask = lane < N_GROUPS
    gl = jnp.where(gmask, logits, NEG_INF)
    gmax = jnp.max(gl, axis=-1, keepdims=True)
    gidx = first(gl == gmax)
    g_w = 1.0 / jnp.sum(jnp.where(gmask, jnp.exp(gl - gmax), 0.0), axis=-1, keepdims=True)
    lo = N_GROUPS + gidx * EXPERTS_PER_GROUP
    el = jnp.where((lane >= lo) & (lane < lo + EXPERTS_PER_GROUP), logits, NEG_INF)
    v1 = jnp.max(el, axis=-1, keepdims=True)
    i1 = first(el == v1)
    el2 = jnp.where(lane == i1, NEG_INF, el)
    v2 = jnp.max(el2, axis=-1, keepdims=True)
    i2 = first(el2 == v2)
    e2 = jnp.exp(v2 - v1)
    gate1 = g_w / (1.0 + e2)
    eidx_ref[...] = jnp.where(lane == 0, i1 - N_GROUPS, jnp.where(lane == 1, i2 - N_GROUPS, 0))
    gate_ref[...] = jnp.where(lane == 0, gate1, jnp.where(lane == 1, gate1 * e2, 0.0))


def merge(x, oa, ob, om, gt, w, tm, alpha, n_total, row_offset=0, into=None):
    n, d = x.shape
    assert row_offset % tm == 0 and n % tm == 0 and n_total % tm == 0
    off = row_offset // tm
    into = list(into or [])
    n_tiles = n // tm
    steps = n_tiles if into else n_total // tm
    rows = lambda a: pl.BlockSpec((tm, a.shape[1]), lambda i: (jnp.minimum(i, n_tiles - 1), 0))
    out = lambda width: pl.BlockSpec((tm, width), lambda i: (i + off, 0))
    consts = [w['ln_in_g'], w['ln_in_b'], w['p_a'], w['p_b'], w['p_m'], w['w_o'], w['ln1_g'], w['ln1_b'],
              w['w_route'], w['b_route']]
    n_in = 5 + len(consts)
    return pl.pallas_call(
        functools.partial(_merge_kernel, alpha, n_tiles), name="merge",
        grid=(steps,),
        in_specs=[rows(a) for a in (x, oa, ob, om, gt)] + [_const_spec(c.shape) for c in consts]
        + [pl.BlockSpec(memory_space=pl.ANY)] * len(into),
        out_specs=[out(d), pl.BlockSpec((tm * TILE_ROWS, LANES), lambda i: (i + off, 0)), out(LANES), out(LANES)],
        out_shape=[jax.ShapeDtypeStruct((n_total, d), F32), jax.ShapeDtypeStruct((n_total * TILE_ROWS, LANES), F32),
                   jax.ShapeDtypeStruct((n_total, LANES), jnp.int32), jax.ShapeDtypeStruct((n_total, LANES), F32)],
        input_output_aliases={n_in + k: k for k in range(len(into))},
        compiler_params=_params(("parallel",)),
    )(x, oa, ob, om, gt, *consts, *into)


ROW_DMA_UNROLL = 8
MOE_BUFFERS = 3
DRAIN_STEPS = 2


TILE_ROWS = 8


def _rows_from_tiles(ref, n):
    return jnp.concatenate([ref[pl.ds(s, n, stride=TILE_ROWS), :] for s in range(TILE_ROWS)], axis=1)


def _rows_to_tiles(ref, x):
    for s in range(TILE_ROWS):
        ref[pl.ds(s, x.shape[0], stride=TILE_ROWS), :] = x[:, s * LANES:(s + 1) * LANES]


def _row_copies(asg_ref, base, count, n_asg, x_hbm, buf, y_hbm, sem, gather, unrolled):
    def tile(ref, idx):
        start = idx * TILE_ROWS
        return ref.at[pl.ds(start if isinstance(idx, int) else pl.multiple_of(start, TILE_ROWS), TILE_ROWS)]

    def one(r, priority):
        a = asg_ref[base + r]
        if gather:
            tok = jnp.minimum(a, n_asg - 1)
            tok = jnp.where(tok >= n_asg // 2, tok - n_asg // 2, tok)
            copy = pltpu.make_async_copy(tile(x_hbm, tok), tile(buf, r), sem)
        else:
            copy = pltpu.make_async_copy(tile(buf, r), tile(y_hbm, a), sem)
        copy.start(priority=priority)

    if unrolled:
        for r in range(count):
            one(r, r % 2)
        return

    def body(g, carry):
        for j in range(ROW_DMA_UNROLL):
            one(g * ROW_DMA_UNROLL + j, j % 2)
        return carry
    lax.fori_loop(0, count // ROW_DMA_UNROLL, body, 0)


def _moe_expert_kernel(n_asg, asg_ref, be_ref, nu_ref, x_hbm, wg_ref, wu_ref, wd_ref, y_hbm,
                       xbuf, ybuf, wgb, wub, wdb, gsem, ssem):
    i = pl.program_id(0)
    used = nu_ref[0]
    rows = xbuf.shape[1] // TILE_ROWS
    n_buf = xbuf.shape[0]
    n_blocks = be_ref.shape[0]
    blk = jnp.minimum(i, n_blocks - 1)

    @pl.when((i == 0) | (be_ref[blk] != be_ref[jnp.maximum(blk - 1, 0)]))
    def _():
        wgb[...] = wg_ref[0, 0].astype(BF16)
        wub[...] = wu_ref[0, 0].astype(BF16)
        wdb[...] = wd_ref[0, 0].astype(BF16)

    def wait_gather(slot):
        pltpu.make_async_copy(x_hbm.at[pl.ds(0, rows * TILE_ROWS)], xbuf.at[slot], gsem.at[slot]).wait()

    def wait_scatter(slot):
        pltpu.make_async_copy(ybuf.at[slot], y_hbm.at[pl.ds(0, rows * TILE_ROWS)], ssem.at[slot]).wait()

    def gather(b, slot, unrolled):
        _row_copies(asg_ref, b * rows, rows, n_asg, x_hbm, xbuf.at[slot], y_hbm, gsem.at[slot], True, unrolled)

    def scatter(b, slot, unrolled):
        _row_copies(asg_ref, b * rows, rows, n_asg, x_hbm, ybuf.at[slot], y_hbm, ssem.at[slot], False, unrolled)

    def expert(slot):
        xb = _rows_from_tiles(xbuf.at[slot], rows).astype(BF16)
        hg = jnp.dot(xb, wgb[...], preferred_element_type=F32)
        hu = jnp.dot(xb, wub[...], preferred_element_type=F32)
        h = hg * _sigmoid(hg) * hu
        _rows_to_tiles(ybuf.at[slot], jnp.dot(h.astype(BF16), wdb[...], preferred_element_type=F32))

    @pl.when(i == 0)
    def _():
        ybuf[1] = jnp.zeros(ybuf.shape[1:], F32)
        spare = [pltpu.make_async_copy(ybuf.at[1],
                                       y_hbm.at[pl.ds((n_asg + j * rows) * TILE_ROWS, rows * TILE_ROWS)], ssem.at[1])
                 for j in range((y_hbm.shape[0] // TILE_ROWS - n_asg) // rows)]
        for copy in spare:
            copy.start()
        for copy in spare:
            copy.wait()

    @pl.when((i >= 2) & (i - 2 < used))
    def _():
        wait_scatter((i - 2) % n_buf)

    steady = (i >= 1) & (i + 2 < used)
    for slot in range(n_buf):
        @pl.when(steady & (i % n_buf == slot))
        def _():
            wait_gather(slot)
            gather(i + 2, (slot + 2) % n_buf, True)
            scatter(i - 1, (slot - 1) % n_buf, True)
            expert(slot)

    @pl.when(jnp.logical_not(steady))
    def _():
        slot = i % n_buf

        @pl.when(i == 0)
        def _():
            for b in range(2):
                @pl.when(b < used)
                def _():
                    gather(b, b, False)

        @pl.when(i < used)
        def _():
            wait_gather(slot)

        @pl.when(i + 2 < used)
        def _():
            gather(i + 2, (i + 2) % n_buf, False)

        @pl.when((i >= 1) & (i - 1 < used))
        def _():
            scatter(i - 1, (i - 1) % n_buf, False)

        @pl.when(i < used)
        def _():
            expert(slot)


def moe_experts(x1_tiles, n_asg, asg, blk_e, n_used, e_gate, e_up, e_down):
    d = e_gate.shape[2]
    assert d == TILE_ROWS * LANES and x1_tiles.shape[1] == LANES
    n_blocks = blk_e.shape[0]
    ff = e_gate.shape[-1]
    n_rows = n_asg + e_gate.shape[1] * EXPERT_BLOCK
    weight = lambda shape: pl.BlockSpec(
        (1, 1) + shape, lambda i, asg, be, nu: (0, be[jnp.minimum(i, n_blocks - 1)], 0, 0))
    return pl.pallas_call(
        functools.partial(_moe_expert_kernel, n_asg), name="moe_expert",
        grid_spec=pltpu.PrefetchScalarGridSpec(
            num_scalar_prefetch=3,
            grid=(n_blocks + DRAIN_STEPS,),
            in_specs=[pl.BlockSpec(memory_space=pl.ANY), weight((d, ff)), weight((d, ff)), weight((ff, d))],
            out_specs=pl.BlockSpec(memory_space=pl.ANY),
            scratch_shapes=[pltpu.VMEM((MOE_BUFFERS, EXPERT_BLOCK * TILE_ROWS, LANES), F32),
                            pltpu.VMEM((MOE_BUFFERS, EXPERT_BLOCK * TILE_ROWS, LANES), F32),
                            pltpu.VMEM((d, ff), BF16), pltpu.VMEM((d, ff), BF16), pltpu.VMEM((ff, d), BF16),
                            pltpu.SemaphoreType.DMA((MOE_BUFFERS,)), pltpu.SemaphoreType.DMA((MOE_BUFFERS,))]),
        out_shape=jax.ShapeDtypeStruct((n_rows * TILE_ROWS, LANES), F32),
        compiler_params=_params(("arbitrary",)),
    )(asg, blk_e, n_used, x1_tiles, e_gate, e_up, e_down)


def _moe_combine_kernel(alpha, lead_tiles, y0_ref, y1_ref, x1_ref, gate_ref, g_ref, b_ref, lead_ref, tail_ref):
    i = pl.program_id(0)
    gate = gate_ref[...]
    tm = x1_ref.shape[0]
    moe = gate[:, 0:1] * _rows_from_tiles(y0_ref, tm) + gate[:, 1:2] * _rows_from_tiles(y1_ref, tm)
    out = _ln(alpha * x1_ref[...] + moe, g_ref[...], b_ref[...])

    @pl.when(i < lead_tiles)
    def _():
        lead_ref[...] = out

    @pl.when(i >= lead_tiles)
    def _():
        tail_ref[...] = out


def moe_combine(y, x1, gate, g, b, tm, alpha, n, n_lead):
    d = x1.shape[1]
    assert n_lead % tm == 0 and (n - n_lead) % tm == 0 and n > n_lead
    lead_tiles = n_lead // tm
    return pl.pallas_call(
        functools.partial(_moe_combine_kernel, alpha, lead_tiles), name="moe_combine",
        grid=(n // tm,),
        in_specs=[pl.BlockSpec((tm * TILE_ROWS, LANES), lambda i: (i, 0)),
                  pl.BlockSpec((tm * TILE_ROWS, LANES), lambda i: (i + n // tm, 0)),
                  pl.BlockSpec((tm, d), lambda i: (i, 0)),
                  pl.BlockSpec((tm, LANES), lambda i: (i, 0)), _const_spec((1, d)), _const_spec((1, d))],
        out_specs=[pl.BlockSpec((tm, d), lambda i: (jnp.minimum(i, lead_tiles - 1), 0)),
                   pl.BlockSpec((tm, d), lambda i: (jnp.maximum(i - lead_tiles, 0), 0))],
        out_shape=[jax.ShapeDtypeStruct((n_lead, d), F32), jax.ShapeDtypeStruct((n - n_lead, d), F32)],
        compiler_params=_params(("arbitrary",)),
    )(y, y, x1, gate, g.reshape(1, d), b.reshape(1, d))


def moe_routing(experts):
    n, top_k = experts.shape
    n_exp = N_GROUPS * EXPERTS_PER_GROUP
    a = n * top_k
    flat_e = experts.T.reshape(a)
    order = jnp.argsort(flat_e, stable=True).astype(jnp.int32)
    counts = jnp.sum((flat_e[:, None] == jnp.arange(n_exp, dtype=jnp.int32)[None, :]).astype(jnp.int32), axis=0)
    ends = jnp.cumsum(counts)
    padded = (counts + EXPERT_BLOCK - 1) // EXPERT_BLOCK * EXPERT_BLOCK
    pad_end = jnp.cumsum(padded)
    n_blocks = -(-a // EXPERT_BLOCK) + n_exp
    blk_start = jnp.arange(n_blocks, dtype=jnp.int32) * EXPERT_BLOCK
    blk_e = jnp.minimum(jnp.sum((pad_end[None, :] <= blk_start[:, None]).astype(jnp.int32), axis=1), n_exp - 1)
    slot = jnp.arange(n_blocks * EXPERT_BLOCK, dtype=jnp.int32)
    slot_e = jnp.repeat(blk_e, EXPERT_BLOCK)
    rank = slot - (pad_end - padded)[slot_e]
    spare = a + jnp.clip(slot - ends[slot_e], 0, n_exp * EXPERT_BLOCK - 1)
    asg = jnp.where(rank < counts[slot_e], order[jnp.clip((ends - counts)[slot_e] + rank, 0, a - 1)], spare)
    n_used = (pad_end[-1:] // EXPERT_BLOCK).astype(jnp.int32)
    return asg.astype(jnp.int32), blk_e.astype(jnp.int32), n_used


def hier_moe_ln(x1, x1_tiles, eidx, gate, w, tm, alpha, n_tokens, n_lead):
    asg, blk_e, n_used = moe_routing(eidx[:n_tokens, :2])
    y = moe_experts(x1_tiles, 2 * n_tokens, asg, blk_e, n_used, w['e_gate'], w['e_up'], w['e_down'])
    return moe_combine(y, x1, gate, w['ln2_g'], w['ln2_b'], tm, alpha, n_tokens, n_lead)


def kernel(x_prompt, x_sample, mem_prompt, state_wkv, state_shift, cache_win_k, cache_win_v, cache_mem_k, cache_mem_v, ln_in_g, ln_in_b, w_in, mu, w0, w_up, a0, a_up, g_up, k_k, k_a, r_k, lnx_g, lnx_b, sinks, w_mem_kv, p_a, p_b, p_m, w_o, ln1_g, ln1_b, w_group, b_group, w_router, b_router, e_gate, e_up, e_down, ln2_g, ln2_b):
    depth = w_in.shape[0]
    assert depth == 1, "single-layer step"
    bsz, seq, d = x_prompt.shape
    dec = x_sample.shape[0]
    assert x_sample.shape[1] == 1
    c_shift = mu.shape[-1]
    c_a = w0.shape[-1]
    window, kv_w = cache_win_k.shape[2], cache_win_k.shape[3] * cache_win_k.shape[4]
    n_mem, mem_w = cache_mem_k.shape[2], cache_mem_k.shape[3] * cache_mem_k.shape[4]
    q_w = sinks.shape[-1] * HEAD
    qkvm_w = q_w + 2 * kv_w + mem_w
    alpha = (2.0 * depth) ** 0.25
    past_len = float(PAST_LEN)
    chunk = 64

    w_in_b = w_in[0].astype(BF16)
    w_parts = [w_in_b[:, :c_shift], w_in_b[:, c_shift:c_shift + qkvm_w], w_in_b[:, c_shift + qkvm_w:]]
    rp = dict(mu=mu[0], w0=w0[0], w_up=w_up[0], a0=a0[0], a_up=a_up[0], g_up=g_up[0], k_k=k_k[0], k_a=k_a[0],
              r_k=r_k[0].reshape(-1))
    n_route = N_GROUPS * (1 + EXPERTS_PER_GROUP)
    mw = dict(ln_in_g=ln_in_g.reshape(1, d), ln_in_b=ln_in_b.reshape(1, d), p_a=p_a[0].astype(BF16),
              p_b=p_b[0].astype(BF16), p_m=p_m[0].astype(BF16), w_o=w_o[0].astype(BF16),
              ln1_g=ln1_g[0].reshape(1, d), ln1_b=ln1_b[0].reshape(1, d),
              w_route=jnp.stack(_pieces(jnp.pad(jnp.concatenate([w_group[0], w_router[0]], axis=1),
                                                ((0, 0), (0, LANES - n_route))), 2)),
              b_route=jnp.pad(jnp.concatenate([b_group[0], b_router[0]]), (0, LANES - n_route)).reshape(1, LANES),
              e_gate=e_gate, e_up=e_up, e_down=e_down, ln2_g=ln2_g[0], ln2_b=ln2_b[0])

    xp = x_prompt.reshape(bsz * seq, d)
    zq, zg, shift_p, prep = ln_proj_prep(x_prompt, ln_in_g, ln_in_b, w_parts, jnp.zeros((bsz, 1, c_shift), F32), rp,
                                         chunk, 256, BF16)
    o_a, wkv_p = wkv(prep, jnp.zeros((bsz, c_a // HEAD, HEAD, HEAD), F32), lnx_g[0], lnx_b[0], chunk)
    mkv = proj(mem_prompt.reshape(bsz * n_mem, d), w_mem_kv[0].astype(BF16), 256).reshape(bsz, n_mem, 2 * mem_w)
    mk_p, mv_p = mkv[..., :mem_w], mkv[..., mem_w:]
    tables = rope_tables(jnp.arange(seq, dtype=F32))
    o_b, o_m, k_rot = attn_prompt(zq, mk_p, mv_p, sinks[0], tables, window, q_w, kv_w)
    n_all = bsz * seq + dec
    n_buf = -(-n_all // 256) * 256
    routed = merge(xp, o_a.reshape(-1, c_a), o_b.reshape(-1, q_w), o_m.reshape(-1, mem_w),
                   zg.reshape(bsz * seq, -1), mw, 256, alpha, n_buf)
    shift_p = shift_p[:, 0]
    kb_p = k_rot[:, -window:].reshape(bsz, window, H_KV, HEAD)
    vb_p = zq[:, -window:, q_w + kv_w:q_w + 2 * kv_w].reshape(bsz, window, H_KV, HEAD)

    xs = x_sample.reshape(dec, d)
    zq_s, zg_s, zr_s, ops_s = ln_proj_prep(xs.reshape(1, dec, d), ln_in_g, ln_in_b, w_parts,
                                           state_shift[0].reshape(1, dec, c_shift), rp, 1, dec, F32)
    zq_s, zg_s, zr_s = zq_s[0], zg_s[0], zr_s[0]
    o_a_s, wkv_s = wkv_step([a.reshape(dec, c_a) for a in ops_s], state_wkv[0], lnx_g[0], lnx_b[0], 8)
    tables_s = rope_tables(jnp.full((1,), past_len, F32))
    o_b_s, o_m_s, nk_s, nv_s = attn_step(
        zq_s, cache_win_k[0].reshape(dec, window, kv_w), cache_win_v[0].reshape(dec, window, kv_w),
        cache_mem_k[0].reshape(dec, n_mem, mem_w), cache_mem_v[0].reshape(dec, n_mem, mem_w),
        sinks[0], tables_s, q_w, kv_w, 8)
    x1, x1_tiles, eidx, gate = merge(xs, o_a_s, o_b_s, o_m_s, zg_s, mw, dec, alpha, n_buf, bsz * seq, routed)

    y_prompt, y_sample = hier_moe_ln(x1, x1_tiles, eidx, gate, mw, dec, alpha, n_all, bsz * seq)
    y_prompt = y_prompt.reshape(bsz, seq, d)
    y_sample = y_sample.reshape(dec, 1, d)

    sd = state_wkv.dtype
    return (y_prompt, y_sample, wkv_p[None].astype(sd), wkv_s[None].astype(sd), shift_p[None], zr_s[None],
            kb_p[None], vb_p[None], nk_s.reshape(dec, window, H_KV, HEAD)[None],
            nv_s.reshape(dec, window, H_KV, HEAD)[None],
            mk_p.reshape(bsz, n_mem, -1, HEAD)[None], mv_p.reshape(bsz, n_mem, -1, HEAD)[None])
```

```python
import functools
import math

import jax
import jax.numpy as jnp
from jax import lax
from jax.experimental import pallas as pl
from jax.experimental.pallas import tpu as pltpu

F32 = jnp.float32
BF16 = jnp.bfloat16
SCAN_BATCH = 8
PROJ_PIECE = 512
ATTN_BLOCKS = 4
CHUNKS_PER_STEP = 4

HEAD = 64
LANES = 128
H_KV = 2
ROT_HALF = 8
ROPE_THETA = 500000.0
PAST_LEN = 8192
N_GROUPS = 4
EXPERTS_PER_GROUP = 8
EXPERT_BLOCK = 128
LN_EPS = 1e-5
LNX_EPS = 64e-5
NEG_INF = -1e30
VMEM_LIMIT = 48 * 1024 * 1024


def _pieces(x, n):
    out = []
    for _ in range(n):
        p = x.astype(BF16)
        out.append(p)
        x = x - p.astype(F32)
    return out


def _mask_dot(x, mask, n=2):
    return sum(jnp.dot(p, mask, preferred_element_type=F32) for p in _pieces(x, n))


def _split3(x, axis, lhs):
    hi = x.astype(BF16).astype(F32)
    lo = x - hi
    return jnp.concatenate([hi, hi, lo] if lhs else [hi, lo, hi], axis=axis).astype(BF16)


def _dot3(a, b):
    return jnp.dot(_split3(a, 1, True), _split3(b, 0, False), preferred_element_type=F32)


def _dot3_t(a, b):
    return lax.dot_general(_split3(a, 1, True), _split3(b, 1, False), (((1,), (1,)), ((), ())),
                           preferred_element_type=F32)


def _bdot(a, b):
    return jnp.dot(a.astype(BF16), b.astype(BF16), preferred_element_type=F32)


def _bdot_t(a, b):
    return lax.dot_general(a.astype(BF16), b.astype(BF16), (((1,), (1,)), ((), ())), preferred_element_type=F32)


def _sigmoid(x):
    return 0.5 * jnp.tanh(0.5 * x) + 0.5


def _ln(x, g, b):
    mu = jnp.mean(x, axis=-1, keepdims=True)
    xc = x - mu
    var = jnp.mean(xc * xc, axis=-1, keepdims=True)
    return xc * lax.rsqrt(var + LN_EPS) * g + b


def _const_spec(shape):
    nd = len(shape)
    return pl.BlockSpec(shape, lambda *_: (0,) * nd)


def _params(sem):
    return pltpu.CompilerParams(dimension_semantics=sem, vmem_limit_bytes=VMEM_LIMIT)


def _ln_proj_prep_kernel(chunk, x_ref, g_ref, b_ref, wr_ref, wq_ref, wg_ref, prev_ref, mu_ref, w0_ref, wup_ref,
                         a0_ref, aup_ref, gup_ref, kk_ref, ka_ref, rk_ref, hsum_ref, tril_ref,
                         zq_ref, zg_ref, zr_ref, *refs):
    out_refs, carry_ref = refs[:-1], refs[-1]
    xn = _ln(x_ref[0], g_ref[...], b_ref[...]).astype(BF16)
    z = jnp.dot(xn, wr_ref[...], preferred_element_type=F32)
    pending = [(o_ref, w_ref, c) for o_ref, w_ref in ((zq_ref, wq_ref), (zg_ref, wg_ref))
               for c in range(0, w_ref.shape[1], PROJ_PIECE)]

    def project(n_pieces):
        for _ in range(min(n_pieces, len(pending))):
            o_ref, w_ref, c = pending.pop(0)
            o_ref[0, :, c:c + PROJ_PIECE] = jnp.dot(xn, w_ref[:, c:c + PROJ_PIECE],
                                                    preferred_element_type=F32).astype(o_ref.dtype)

    tt = z.shape[0]
    c_a = w0_ref.shape[-1]
    r_w, r_a, r_g = wup_ref.shape[0], aup_ref.shape[0], gup_ref.shape[0]
    if chunk == 1:
        zr_ref[0] = z
        prev = prev_ref[0]
    else:
        zr_ref[0] = z[tt - 1:tt, :]

        @pl.when(pl.program_id(1) == 0)
        def _():
            carry_ref[...] = prev_ref[0]

        row = lax.broadcasted_iota(jnp.int32, z.shape, 0)
        prev = jnp.where(row == 0, carry_ref[...], pltpu.roll(z, 1, 0))
        carry_ref[...] = z[tt - 1:tt, :]
    zs = z + (prev - z) * mu_ref[...]
    r = zs[:, :c_a]
    k = zs[:, c_a:2 * c_a]
    v = zs[:, 2 * c_a:3 * c_a]
    o = 3 * c_a
    xw = zs[:, o:o + r_w]
    xa = zs[:, o + r_w:o + r_w + r_a]
    xg = zs[:, o + r_w + r_a:o + r_w + r_a + r_g]
    project(1)
    warg = -(w0_ref[...] + _dot3(jnp.tanh(xw), wup_ref[...]))
    softplus = jnp.maximum(warg, 0.0) + jnp.log1p(jnp.exp(-jnp.abs(warg)))
    lw = -jnp.exp(-softplus - 0.5)
    project(1)
    a = _sigmoid(a0_ref[...] + _dot3(xa, aup_ref[...]))
    g = _dot3(_sigmoid(xg), gup_ref[...])
    project(1)
    kkr = k * kk_ref[...]
    kk = kkr / jnp.maximum(jnp.sqrt(_mask_dot(kkr * kkr, hsum_ref[...])), 1e-12)
    project(1)
    k2 = k * (1.0 + (a - 1.0) * ka_ref[...])
    bonus = _mask_dot(r * k2 * rk_ref[...], hsum_ref[...]) * v
    kb = kk * a
    project(1)
    if chunk == 1:
        outs = (r, jnp.exp(lw), k2, v, -kk, kb, g, bonus)
        for ref, val in zip(out_refs, outs):
            ref[0] = val
        project(len(pending))
        return
    pieces = _pieces(lw, 3)
    ones3 = jnp.ones((chunk, 3 * chunk), BF16)
    cw, cwl = [], []
    for c in range(tt // chunk):
        stack = jnp.concatenate([p[c * chunk:(c + 1) * chunk] for p in pieces], axis=0)
        cw.append(jnp.dot(tril_ref[...], stack, preferred_element_type=F32))
        cwl.append(jnp.dot(ones3, stack, preferred_element_type=F32))
    cw = jnp.concatenate(cw, axis=0)
    cwl = jnp.concatenate(cwl, axis=0)
    project(1)
    e_inv = jnp.exp(-cw)
    outs = (r * jnp.exp(cw), -kk * jnp.exp(cw - lw), kb * e_inv, k2 * e_inv, v, g, bonus)
    for ref, val in zip(out_refs[:-1], outs):
        ref[0] = val
        project(1)
    wl = jnp.exp(cwl)
    for c in range(tt // chunk):
        out_refs[-1][0, c] = wl[c * chunk:c * chunk + 1, :]
    project(len(pending))


def _head_sum_matrix(width):
    idx = jnp.arange(width)
    return ((idx[:, None] // HEAD) == (idx[None, :] // HEAD)).astype(BF16)


def ln_proj_prep(x, g, b, ws, prev, p, chunk, tt, gate_dtype):
    bsz, t, d = x.shape
    w_r, w_q, w_g = ws
    cs = w_r.shape[1]
    c_a = p['w0'].shape[-1]
    assert prev.shape[1] == (t if chunk == 1 else 1)
    ridx = jnp.arange(chunk)
    tril = jnp.tile((ridx[None, :] <= ridx[:, None]).astype(BF16), (1, 3))
    hsum = _head_sum_matrix(c_a)
    row = lambda a: a.reshape(1, -1)
    rows = lambda width: pl.BlockSpec((1, tt, width), lambda bi, i: (bi, i, 0))
    full = jax.ShapeDtypeStruct((bsz, t, c_a), F32)
    if chunk == 1:
        out_specs, out_shape = [rows(c_a)] * 8, [full] * 8
        prev_spec, zr_spec, zr_rows = rows(cs), rows(cs), t
    else:
        out_specs = [rows(c_a)] * 7 + [pl.BlockSpec((1, tt // chunk, 1, c_a), lambda bi, i: (bi, i, 0, 0))]
        out_shape = [full] * 7 + [jax.ShapeDtypeStruct((bsz, t // chunk, 1, c_a), F32)]
        prev_spec = zr_spec = pl.BlockSpec((1, 1, cs), lambda bi, i: (bi, 0, 0))
        zr_rows = 1
    consts = [row(p['mu']), row(p['w0']), p['w_up'], row(p['a0']), p['a_up'], p['g_up'], row(p['k_k']),
              row(p['k_a']), row(p['r_k']), hsum, tril]
    weight = lambda w: pl.BlockSpec(w.shape, lambda bi, i: (0, 0), pipeline_mode=pl.Buffered(1))
    outs = pl.pallas_call(
        functools.partial(_ln_proj_prep_kernel, chunk), name="ln_proj_prep",
        grid=(bsz, t // tt),
        in_specs=[rows(d), _const_spec((1, d)), _const_spec((1, d)), weight(w_r), weight(w_q), weight(w_g), prev_spec]
        + [_const_spec(c.shape) for c in consts],
        out_specs=[rows(w_q.shape[1]), rows(w_g.shape[1]), zr_spec] + out_specs,
        out_shape=[jax.ShapeDtypeStruct((bsz, t, w_q.shape[1]), F32),
                   jax.ShapeDtypeStruct((bsz, t, w_g.shape[1]), gate_dtype),
                   jax.ShapeDtypeStruct((bsz, zr_rows, cs), F32)] + out_shape,
        scratch_shapes=[pltpu.VMEM((1, cs), F32)],
        compiler_params=_params(("parallel", "arbitrary")),
    )(x, row(g), row(b), w_r, w_q, w_g, prev, *consts)
    return outs[0], outs[1], outs[2], outs[3:]


def _wkv_chunk_kernel(ra_ref, at_ref, bt_ref, kt_ref, v_ref, wl_ref, rp_ref, y0_ref, m_ref, n_ref):
    n_chunks = m_ref.shape[1]
    length = ra_ref.shape[1] // n_chunks
    heads = ra_ref.shape[2] // HEAD
    row = lax.broadcasted_iota(jnp.int32, (length, length), 0)
    col = lax.broadcasted_iota(jnp.int32, (length, length), 1)
    strict = row > col
    incl = row >= col
    hrow = lax.broadcasted_iota(jnp.int32, (HEAD, HEAD), 0)
    hcol = lax.broadcasted_iota(jnp.int32, (HEAD, HEAD), 1)
    units = [(c, h) for c in range(n_chunks) for h in range(heads)]
    us = range(len(units))
    at = lambda ref, u: ref[0, units[u][0] * length:(units[u][0] + 1) * length,
                            units[u][1] * HEAD:(units[u][1] + 1) * HEAD]
    gram = [_bdot_t(jnp.concatenate([at(at_ref, u), at(ra_ref, u)], axis=0),
                    jnp.concatenate([at(bt_ref, u), at(kt_ref, u)], axis=0)) for u in us]
    a_ab = [jnp.where(strict, g[:length, :length], 0.0) for g in gram]
    a_kk = [jnp.concatenate([jnp.where(strict, g[:length, length:], 0.0),
                             jnp.where(incl, g[length:, length:], 0.0)], axis=0) for g in gram]
    a_rb = [jnp.where(incl, g[length:, :length], 0.0) for g in gram]
    inv = [jnp.where(row == col, 1.0, a) for a in a_ab]
    pw = [_bdot(a, a) for a in a_ab]
    avy = [_bdot(a_kk[u], at(v_ref, u)) for u in us]
    w_l = [wl_ref[0, units[u][0], :, units[u][1] * HEAD:(units[u][1] + 1) * HEAD] for u in us]
    nk = [_bdot(at(v_ref, u).T, at(kt_ref, u) * w_l[u]) for u in us]
    for _ in range(int(math.log2(length)) - 2):
        both = [_bdot(jnp.concatenate([pw[u], inv[u]], axis=0), pw[u]) for u in us]
        pw = [b[:length] for b in both]
        inv = [inv[u] + both[u][length:] for u in us]
    inv = [inv[u] + _bdot(inv[u], pw[u]) for u in us]
    pq = [_bdot(inv[u], jnp.concatenate([at(at_ref, u), avy[u][:length]], axis=1)) for u in us]
    ry = [_bdot(a_rb[u], pq[u]) for u in us]
    mn = [_bdot(pq[u].T, at(bt_ref, u) * w_l[u]) for u in us]
    lanes = lambda f, c: jnp.concatenate([f(c * heads + h) for h in range(heads)], axis=1)
    rows = lambda f: jnp.concatenate([lanes(f, c) for c in range(n_chunks)], axis=0)
    rp_ref[0] = rows(lambda u: at(ra_ref, u) + ry[u][:, :HEAD])
    y0_ref[0] = rows(lambda u: avy[u][length:] + ry[u][:, HEAD:])
    for c in range(n_chunks):
        m_ref[0, c] = lanes(lambda u: jnp.where(hrow == hcol, w_l[u], 0.0) + mn[u][:HEAD], c)
        n_ref[0, c] = lanes(lambda u: nk[u] + mn[u][HEAD:], c)


def _head_norm_wide(y, hsum):
    yc = y - _mask_dot(y, hsum) * (1.0 / HEAD)
    return yc * lax.rsqrt(_mask_dot(yc * yc, hsum) * (1.0 / HEAD) + LNX_EPS)


def _wkv_scan_kernel(rp_ref, y0_ref, m_ref, n_ref, g_ref, bonus_ref, s0_ref, lg_ref, lb_ref, hsum_ref,
                     o_ref, sout_ref, s_ref):
    c = pl.program_id(1)
    nb = rp_ref.shape[0]
    heads = rp_ref.shape[2] // HEAD

    @pl.when(c == 0)
    def _():
        s_ref[...] = s0_ref[...]

    pairs = [(b, h, slice(h * HEAD, (h + 1) * HEAD)) for b in range(nb) for h in range(heads)]
    s = [s_ref[b, h] for b, h, _ in pairs]
    s_new = [n_ref[b, 0, :, sl] + _dot3(s[j], m_ref[b, 0, :, sl]) for j, (b, h, sl) in enumerate(pairs)]
    ys = [y0_ref[b, :, sl] + _bdot_t(rp_ref[b, :, sl], s[j]) for j, (b, h, sl) in enumerate(pairs)]
    for j, (b, h, _) in enumerate(pairs):
        s_ref[b, h] = s_new[j]
    length = rp_ref.shape[1]
    y = _head_norm_wide(jnp.concatenate([jnp.concatenate(ys[b * heads:(b + 1) * heads], axis=-1) for b in range(nb)],
                                        axis=0), hsum_ref[...])
    for b in range(nb):
        o = (y[b * length:(b + 1) * length] * lg_ref[...] + lb_ref[...] + bonus_ref[b]) * g_ref[b]
        o_ref[b] = o.astype(o_ref.dtype)

    @pl.when(c == pl.num_programs(1) - 1)
    def _():
        sout_ref[...] = s_ref[...]


def wkv(prep, s0, lnx_g, lnx_b, chunk):
    ra, at, bt, kt, v, g, bonus, wl = prep
    bsz, t, c_a = ra.shape
    heads = c_a // HEAD
    n_chunks = t // chunk
    assert chunk & (chunk - 1) == 0 and chunk >= 4 and t % chunk == 0
    per_step = CHUNKS_PER_STEP if n_chunks % CHUNKS_PER_STEP == 0 else 1
    tile = pl.BlockSpec((1, per_step * chunk, c_a), lambda b, c: (b, c, 0))
    mat = pl.BlockSpec((1, per_step, HEAD, c_a), lambda b, c: (b, c, 0, 0))
    full = jax.ShapeDtypeStruct((bsz, t, c_a), F32)
    mats = jax.ShapeDtypeStruct((bsz, n_chunks, HEAD, c_a), F32)
    rp, y0, m, n = pl.pallas_call(
        _wkv_chunk_kernel, name="wkv_chunk",
        grid=(bsz, n_chunks // per_step),
        in_specs=[tile] * 5 + [pl.BlockSpec((1, per_step, 1, c_a), lambda b, c: (b, c, 0, 0))],
        out_specs=[tile, tile, mat, mat],
        out_shape=[full, full, mats, mats],
        compiler_params=_params(("parallel", "parallel")),
    )(ra, at, bt, kt, v, wl)
    nb = SCAN_BATCH if bsz % SCAN_BATCH == 0 else 1
    tile = pl.BlockSpec((nb, chunk, c_a), lambda b, c: (b, c, 0))
    mat = pl.BlockSpec((nb, 1, HEAD, c_a), lambda b, c: (b, c, 0, 0))
    st = pl.BlockSpec((nb, heads, HEAD, HEAD), lambda b, c: (b, 0, 0, 0))
    return pl.pallas_call(
        _wkv_scan_kernel, name="wkv_scan",
        grid=(bsz // nb, n_chunks),
        in_specs=[tile, tile, mat, mat, tile, tile, st, _const_spec((1, c_a)), _const_spec((1, c_a)),
                  _const_spec((c_a, c_a))],
        out_specs=[tile, st],
        out_shape=[jax.ShapeDtypeStruct((bsz, t, c_a), BF16), jax.ShapeDtypeStruct(s0.shape, F32)],
        scratch_shapes=[pltpu.VMEM((nb, heads, HEAD, HEAD), F32)],
        compiler_params=_params(("parallel", "arbitrary")),
    )(rp, y0, m, n, g, bonus, s0, lnx_g.reshape(1, c_a), lnx_b.reshape(1, c_a), _head_sum_matrix(c_a))


def _wkv_step_kernel(r_ref, w_ref, k_ref, v_ref, a_ref, b_ref, g_ref, bonus_ref, s_ref, lg_ref, lb_ref, hsum_ref,
                     o_ref, sout_ref):
    bt, heads = s_ref.shape[0], s_ref.shape[1]
    c_a = heads * HEAD
    hsum = hsum_ref[...]
    diag = (lax.broadcasted_iota(jnp.int32, (HEAD, c_a), 1) % HEAD
            == lax.broadcasted_iota(jnp.int32, (HEAD, c_a), 0))
    seqs = range(bt)
    row = lambda ref, i: ref[i:i + 1, :]
    stack = lambda f: jnp.concatenate([f(i) for i in seqs], axis=0)
    piece = lambda x, i: x[i * HEAD:(i + 1) * HEAD]
    s = stack(lambda i: jnp.concatenate([s_ref[i, h] for h in range(heads)], axis=1))
    sa = _mask_dot(s * stack(lambda i: jnp.broadcast_to(row(a_ref, i), (HEAD, c_a))), hsum)
    v_rows = _mask_dot(stack(lambda i: jnp.where(diag, row(v_ref, i), 0.0)), hsum)
    s = stack(lambda i: piece(s, i) * row(w_ref, i) + piece(sa, i) * row(b_ref, i) + piece(v_rows, i) * row(k_ref, i))
    for i in seqs:
        for h in range(heads):
            sout_ref[i, h] = piece(s, i)[:, h * HEAD:(h + 1) * HEAD]
    y_rows = _mask_dot(stack(lambda i: piece(s, i) * row(r_ref, i)), hsum)
    y = stack(lambda i: jnp.sum(jnp.where(diag, piece(y_rows, i), 0.0), axis=0, keepdims=True))
    y = _head_norm_wide(y, hsum)
    o_ref[...] = (y * lg_ref[...] + lb_ref[...] + bonus_ref[...]) * g_ref[...]


def wkv_step(ops, s0, lnx_g, lnx_b, bt):
    bsz, c_a = ops[0].shape
    rows = pl.BlockSpec((bt, c_a), lambda i: (i, 0))
    st = pl.BlockSpec((bt,) + s0.shape[1:], lambda i: (i, 0, 0, 0))
    hsum = _head_sum_matrix(c_a)
    return pl.pallas_call(
        _wkv_step_kernel, name="wkv_step",
        grid=(bsz // bt,),
        in_specs=[rows] * 8 + [st, _const_spec((1, c_a)), _const_spec((1, c_a)), _const_spec(hsum.shape)],
        out_specs=[rows, st],
        out_shape=[jax.ShapeDtypeStruct((bsz, c_a), F32), jax.ShapeDtypeStruct(s0.shape, F32)],
        compiler_params=_params(("parallel",)),
    )(*ops, s0, lnx_g.reshape(1, c_a), lnx_b.reshape(1, c_a), hsum)


def rope_tables(pos):
    inv_freq = ROPE_THETA ** (-jnp.arange(ROT_HALF, dtype=F32) / ROT_HALF)
    ang = pos[:, None] * inv_freq[None, :]
    cos, sin = jnp.cos(ang), jnp.sin(ang)
    t = pos.shape[0]
    rest = HEAD - 2 * ROT_HALF
    c = jnp.concatenate([cos, cos, jnp.ones((t, rest), F32)], axis=1)
    s1 = jnp.concatenate([jnp.zeros((t, ROT_HALF), F32), sin, jnp.zeros((t, rest), F32)], axis=1)
    s2 = jnp.concatenate([-sin, jnp.zeros((t, HEAD - ROT_HALF), F32)], axis=1)
    rep = LANES // HEAD
    return tuple(jnp.tile(x, (1, rep)) for x in (c, s1, s2))


def _rope(x, c, s1, s2):
    width = x.shape[-1]
    rep = width // c.shape[-1]
    if rep > 1:
        c, s1, s2 = (jnp.concatenate([t] * rep, axis=1) for t in (c, s1, s2))
    return x * c + pltpu.roll(x, ROT_HALF, 1) * s1 + pltpu.roll(x, width - ROT_HALF, 1) * s2


def _attn_kernel(window, q_w, kv_w, zq_ref, kp_ref, vp_ref, rc_ref, rs1_ref, rs2_ref, pc_ref, ps1_ref, ps2_ref,
                 mk_ref, mv_ref, sink_ref, ob_ref, om_ref, kr_ref):
    zq = zq_ref[0]
    nblk = zq.shape[0] // window
    first = pl.program_id(1) * nblk
    scale = HEAD ** -0.5
    q = zq[:, :q_w]
    k = zq[:, q_w:q_w + kv_w]
    v = zq[:, q_w + kv_w:q_w + 2 * kv_w]
    qm = zq[:, q_w + 2 * kv_w:]
    qr = _rope(q, rc_ref[...], rs1_ref[...], rs2_ref[...]) * scale
    kr = _rope(k, rc_ref[...], rs1_ref[...], rs2_ref[...])
    kr_ref[0] = kr
    k_all = jnp.concatenate([_rope(kp_ref[0], pc_ref[...], ps1_ref[...], ps2_ref[...]), kr], axis=0)
    v_all = jnp.concatenate([vp_ref[0], v], axis=0)
    gqa = q_w // kv_w
    assert window & (window - 1) == 0
    qi = lax.broadcasted_iota(jnp.int32, (gqa * window, 2 * window), 0) & (window - 1)
    kj = lax.broadcasted_iota(jnp.int32, (gqa * window, 2 * window), 1)
    band = (kj > qi) & (kj <= qi + window)
    tdot = lambda a, b: lax.dot_general(a, b, (((1,), (1,)), ((), ())), preferred_element_type=F32)
    dot = lambda a, b: jnp.dot(a, b, preferred_element_type=F32)
    hsl = lambda h: slice(h * HEAD, (h + 1) * HEAD)
    rows = lambda i: slice(i * window, (i + 1) * window)
    keys = lambda i: slice(i * window, (i + 2) * window)
    wins = [(i, hk) for i in range(nblk) for hk in range(kv_w // HEAD)]
    mems = [(i, h) for i in range(nblk) for h in range(qm.shape[1] // HEAD)]
    qmb = (qm * scale).astype(BF16)
    mk = mk_ref[0].astype(BF16)
    mv = mv_ref[0].astype(BF16)
    kcat = [k_all[keys(i), hsl(hk)].astype(BF16) for i, hk in wins]
    vcat = [v_all[keys(i), hsl(hk)].astype(BF16) for i, hk in wins]
    qs = [jnp.concatenate([qr[rows(i), hsl(hk * gqa + g)] for g in range(gqa)], axis=0).astype(BF16)
          for i, hk in wins]
    s = [jnp.where(band & ((kj >= window) | (first + i > 0)), tdot(qs[u], kcat[u]), NEG_INF)
         for u, (i, hk) in enumerate(wins)]
    sm = [tdot(qmb[rows(i), hsl(h)], mk[:, hsl(h)]) for i, h in mems]
    sink = [jnp.concatenate([jnp.full((window, 1), sink_ref[hk * gqa + g], F32) for g in range(gqa)], axis=0)
            for i, hk in wins]
    m = [jnp.maximum(jnp.max(s[u], axis=-1, keepdims=True), sink[u]) for u in range(len(wins))]
    p = [jnp.exp(s[u] - m[u]) for u in range(len(wins))]
    pm = [jnp.exp(x - jnp.max(x, axis=-1, keepdims=True)) for x in sm]
    o = [dot(p[u].astype(BF16), vcat[u])
         / (jnp.sum(p[u], axis=-1, keepdims=True) + jnp.exp(sink[u] - m[u])) for u in range(len(wins))]
    om = [dot(pm[u].astype(BF16), mv[:, hsl(h)]) / jnp.sum(pm[u], axis=-1, keepdims=True)
          for u, (i, h) in enumerate(mems)]
    n_kv, n_mh = kv_w // HEAD, qm.shape[1] // HEAD
    ob = jnp.concatenate([jnp.concatenate([o[i * n_kv + hk][g * window:(g + 1) * window]
                                           for hk in range(n_kv) for g in range(gqa)], axis=1)
                          for i in range(nblk)], axis=0)
    ob_ref[0] = ob.astype(ob_ref.dtype)
    om_ref[0] = jnp.concatenate([jnp.concatenate(om[i * n_mh:(i + 1) * n_mh], axis=1) for i in range(nblk)],
                                axis=0).astype(om_ref.dtype)


def attn_prompt(zq, mk, mv, sinks, tables, window, q_w, kv_w):
    bsz, t, zw = zq.shape
    mem_w = zw - q_w - 2 * kv_w
    assert kv_w == LANES and q_w % kv_w == 0
    kcol, vcol = q_w // kv_w, q_w // kv_w + 1
    nblk = ATTN_BLOCKS if (t // window) % ATTN_BLOCKS == 0 else 1
    rows = nblk * window
    prev = lambda n: jnp.maximum(n * nblk - 1, 0)
    tab = pl.BlockSpec((rows, LANES), lambda b, n: (n, 0))
    ptab = pl.BlockSpec((window, LANES), lambda b, n: (prev(n), 0))
    mem = pl.BlockSpec((1,) + mk.shape[1:], lambda b, n: (b, 0, 0))
    return pl.pallas_call(
        functools.partial(_attn_kernel, window, q_w, kv_w), name="attn",
        grid=(bsz, t // rows),
        in_specs=[pl.BlockSpec((1, rows, zw), lambda b, n: (b, n, 0)),
                  pl.BlockSpec((1, window, kv_w), lambda b, n: (b, prev(n), kcol)),
                  pl.BlockSpec((1, window, kv_w), lambda b, n: (b, prev(n), vcol)),
                  tab, tab, tab, ptab, ptab, ptab, mem, mem,
                  pl.BlockSpec(memory_space=pltpu.SMEM)],
        out_specs=[pl.BlockSpec((1, rows, q_w), lambda b, n: (b, n, 0)),
                   pl.BlockSpec((1, rows, mem_w), lambda b, n: (b, n, 0)),
                   pl.BlockSpec((1, rows, kv_w), lambda b, n: (b, n, 0))],
        out_shape=[jax.ShapeDtypeStruct((bsz, t, q_w), BF16), jax.ShapeDtypeStruct((bsz, t, mem_w), BF16),
                   jax.ShapeDtypeStruct((bsz, t, kv_w), F32)],
        compiler_params=_params(("parallel", "parallel")),
    )(zq, zq, zq, *tables, *tables, mk, mv, sinks)


def _attn_step_kernel(q_w, kv_w, zq_ref, ck_ref, cv_ref, mk_ref, mv_ref, rc_ref, rs1_ref, rs2_ref, sink_ref,
                      ob_ref, om_ref, nk_ref, nv_ref):
    bt = zq_ref.shape[0]
    window = ck_ref.shape[1]
    mem_w = om_ref.shape[1]
    n_q, gqa, per_vreg = q_w // HEAD, q_w // kv_w, LANES // HEAD
    scale = HEAD ** -0.5
    zq = zq_ref[...]
    q = _rope(zq[:, :q_w], rc_ref[...], rs1_ref[...], rs2_ref[...]) * scale
    k_new = _rope(zq[:, q_w:q_w + kv_w], rc_ref[...], rs1_ref[...], rs2_ref[...])
    v_new = zq[:, q_w + kv_w:q_w + 2 * kv_w]
    qm = zq[:, q_w + 2 * kv_w:] * scale
    own = lambda w: (lax.broadcasted_iota(jnp.int32, (n_q, w), 1) // HEAD
                     == lax.broadcasted_iota(jnp.int32, (n_q, w), 0))
    own_q, own_m = own(q_w), own(mem_w)
    hrow = lax.broadcasted_iota(jnp.int32, (n_q, LANES), 0)
    hblk = lax.broadcasted_iota(jnp.int32, (n_q, LANES), 1) // HEAD
    swap = (hrow % per_vreg) != (hrow // gqa)
    keep = hblk == hrow % per_vreg
    key_ok = lax.broadcasted_iota(jnp.int32, (n_q, window), 1) >= 1
    wrow = lax.broadcasted_iota(jnp.int32, (window, kv_w), 0)
    sink = sink_ref[...]
    bs = range(bt)
    tdot = lambda a, b: lax.dot_general(a.astype(BF16), b.astype(BF16), (((1,), (1,)), ((), ())),
                                        preferred_element_type=F32)
    for b in bs:
        nk_ref[b] = jnp.where(wrow == window - 1, k_new[b:b + 1], pltpu.roll(ck_ref[b], window - 1, 0))
        nv_ref[b] = jnp.where(wrow == window - 1, v_new[b:b + 1], pltpu.roll(cv_ref[b], window - 1, 0))
    q8 = []
    for b in bs:
        rep = jnp.where(own_q, q[b:b + 1], 0.0)
        fold = sum(rep[:, c * LANES:(c + 1) * LANES] for c in range(q_w // LANES))
        q8.append(jnp.where(swap, pltpu.roll(fold, HEAD, 1), fold))
    qm8 = [jnp.where(own_m, qm[b:b + 1], 0.0) for b in bs]
    s = [jnp.where(key_ok, tdot(q8[b], ck_ref[b]), NEG_INF) for b in bs]
    sm = [tdot(qm8[b], mk_ref[b]) for b in bs]
    s_new = [jnp.sum(q8[b] * k_new[b:b + 1], axis=1, keepdims=True) for b in bs]
    m = [jnp.maximum(jnp.maximum(jnp.max(s[b], axis=1, keepdims=True), s_new[b]), sink) for b in bs]
    p = [jnp.exp(s[b] - m[b]) for b in bs]
    p_new = [jnp.exp(s_new[b] - m[b]) for b in bs]
    pm = [jnp.exp(sm[b] - jnp.max(sm[b], axis=1, keepdims=True)) for b in bs]
    o8 = [(_bdot(p[b], cv_ref[b]) + p_new[b] * v_new[b:b + 1])
          / (jnp.sum(p[b], axis=1, keepdims=True) + p_new[b] + jnp.exp(sink - m[b])) for b in bs]
    om8 = [_bdot(pm[b], mv_ref[b]) / jnp.sum(pm[b], axis=1, keepdims=True) for b in bs]
    ob, om = [], []
    for b in bs:
        o = jnp.where(swap, pltpu.roll(o8[b], HEAD, 1), o8[b])
        o = jnp.concatenate([jnp.where(keep, o, 0.0)] * (q_w // LANES), axis=1)
        ob.append(jnp.sum(jnp.where(own_q, o, 0.0), axis=0, keepdims=True))
        om.append(jnp.sum(jnp.where(own_m, om8[b], 0.0), axis=0, keepdims=True))
    ob_ref[...] = jnp.concatenate(ob, axis=0)
    om_ref[...] = jnp.concatenate(om, axis=0)


def attn_step(zq, ck, cv, mk, mv, sinks, tables, q_w, kv_w, bt):
    bsz, zw = zq.shape
    mem_w = zw - q_w - 2 * kv_w
    window, n_mem = ck.shape[1], mk.shape[1]
    n_q = q_w // HEAD
    assert kv_w == LANES and mem_w // HEAD <= n_q and n_q == 8
    rows = lambda w: pl.BlockSpec((bt, w), lambda i: (i, 0))
    cache = pl.BlockSpec((bt, window, kv_w), lambda i: (i, 0, 0))
    mem = pl.BlockSpec((bt, n_mem, mem_w), lambda i: (i, 0, 0))
    consts = list(tables) + [sinks.reshape(n_q, 1)]
    const_specs = [_const_spec(c.shape) for c in consts]
    return pl.pallas_call(
        functools.partial(_attn_step_kernel, q_w, kv_w), name="attn_step",
        grid=(bsz // bt,),
        in_specs=[rows(zw), cache, cache, mem, mem] + const_specs,
        out_specs=[rows(q_w), rows(mem_w), cache, cache],
        out_shape=[jax.ShapeDtypeStruct((bsz, q_w), F32), jax.ShapeDtypeStruct((bsz, mem_w), F32),
                   jax.ShapeDtypeStruct(ck.shape, F32), jax.ShapeDtypeStruct(cv.shape, F32)],
        compiler_params=_params(("parallel",)),
    )(zq, ck, cv, mk, mv, *consts)


def _proj_kernel(x_ref, w_ref, o_ref):
    o_ref[...] = jnp.dot(x_ref[...].astype(BF16), w_ref[...], preferred_element_type=F32)


def proj(x, w, tm):
    n, d = x.shape
    return pl.pallas_call(
        _proj_kernel, name="proj",
        grid=(n // tm,),
        in_specs=[pl.BlockSpec((tm, d), lambda i: (i, 0)), _const_spec(w.shape)],
        out_specs=pl.BlockSpec((tm, w.shape[1]), lambda i: (i, 0)),
        out_shape=jax.ShapeDtypeStruct((n, w.shape[1]), F32),
        compiler_params=_params(("parallel",)),
    )(x, w)


def _merge_kernel(alpha, n_tiles, x_ref, oa_ref, ob_ref, om_ref, gt_ref, lig_ref, lib_ref, pa_ref, pb_ref, pm_ref,
                  wo_ref, l1g_ref, l1b_ref, wr_ref, br_ref, *refs):
    outs = refs[-4:]

    @pl.when(pl.program_id(0) >= n_tiles)
    def _():
        for ref in outs:
            ref[...] = jnp.zeros_like(ref)

    @pl.when(pl.program_id(0) < n_tiles)
    def _():
        _merge_tile(alpha, x_ref, oa_ref, ob_ref, om_ref, gt_ref, lig_ref, lib_ref, pa_ref, pb_ref, pm_ref, wo_ref,
                    l1g_ref, l1b_ref, wr_ref, br_ref, *outs)


def _merge_tile(alpha, x_ref, oa_ref, ob_ref, om_ref, gt_ref, lig_ref, lib_ref, pa_ref, pb_ref, pm_ref, wo_ref,
                l1g_ref, l1b_ref, wr_ref, br_ref, x1_ref, x1t_ref, eidx_ref, gate_ref):
    d = x_ref.shape[1]
    xn = _ln(x_ref[...], lig_ref[...], lib_ref[...])
    gts = _sigmoid(gt_ref[...].astype(F32))
    merged = (gts[:, :d] * _bdot(oa_ref[...], pa_ref[...]) + gts[:, d:2 * d] * _bdot(ob_ref[...], pb_ref[...])
              + gts[:, 2 * d:] * _bdot(om_ref[...], pm_ref[...]))
    x1 = _ln(alpha * xn + _bdot(merged, wo_ref[...]), l1g_ref[...], l1b_ref[...])
    x1_ref[...] = x1
    _rows_to_tiles(x1t_ref, x1)
    x_hi, x_lo = _pieces(x1, 2)
    w_hi, w_lo = wr_ref[0], wr_ref[1]
    dot = lambda a, b: jnp.dot(a, b, preferred_element_type=F32)
    logits = dot(x_hi, w_hi) + dot(x_hi, w_lo) + dot(x_lo, w_hi) + br_ref[...]
    lane = lax.broadcasted_iota(jnp.int32, logits.shape, 1)
    lane_f = lane.astype(F32)
    first = lambda hit: jnp.min(jnp.where(hit, lane_f, float(LANES)), axis=-1, keepdims=True).astype(jnp.int32)
    gmask = lane < N_GROUPS
    gl = jnp.where(gmask, logits, NEG_INF)
    gmax = jnp.max(gl, axis=-1, keepdims=True)
    gidx = first(gl == gmax)
    g_w = 1.0 / jnp.sum(jnp.where(gmask, jnp.exp(gl - gmax), 0.0), axis=-1, keepdims=True)
    lo = N_GROUPS + gidx * EXPERTS_PER_GROUP
    el = jnp.where((lane >= lo) & (lane < lo + EXPERTS_PER_GROUP), logits, NEG_INF)
    v1 = jnp.max(el, axis=-1, keepdims=True)
    i1 = first(el == v1)
    el2 = jnp.where(lane == i1, NEG_INF, el)
    v2 = jnp.max(el2, axis=-1, keepdims=True)
    i2 = first(el2 == v2)
    e2 = jnp.exp(v2 - v1)
    gate1 = g_w / (1.0 + e2)
    eidx_ref[...] = jnp.where(lane == 0, i1 - N_GROUPS, jnp.where(lane == 1, i2 - N_GROUPS, 0))
    gate_ref[...] = jnp.where(lane == 0, gate1, jnp.where(lane == 1, gate1 * e2, 0.0))


def merge(x, oa, ob, om, gt, w, tm, alpha, n_total, row_offset=0, into=None):
    n, d = x.shape
    assert row_offset % tm == 0 and n % tm == 0 and n_total % tm == 0
    off = row_offset // tm
    into = list(into or [])
    n_tiles = n // tm
    steps = n_tiles if into else n_total // tm
    rows = lambda a: pl.BlockSpec((tm, a.shape[1]), lambda i: (jnp.minimum(i, n_tiles - 1), 0))
    out = lambda width: pl.BlockSpec((tm, width), lambda i: (i + off, 0))
    consts = [w['ln_in_g'], w['ln_in_b'], w['p_a'], w['p_b'], w['p_m'], w['w_o'], w['ln1_g'], w['ln1_b'],
              w['w_route'], w['b_route']]
    n_in = 5 + len(consts)
    return pl.pallas_call(
        functools.partial(_merge_kernel, alpha, n_tiles), name="merge",
        grid=(steps,),
        in_specs=[rows(a) for a in (x, oa, ob, om, gt)] + [_const_spec(c.shape) for c in consts]
        + [pl.BlockSpec(memory_space=pl.ANY)] * len(into),
        out_specs=[out(d), pl.BlockSpec((tm * TILE_ROWS, LANES), lambda i: (i + off, 0)), out(LANES), out(LANES)],
        out_shape=[jax.ShapeDtypeStruct((n_total, d), F32), jax.ShapeDtypeStruct((n_total * TILE_ROWS, LANES), F32),
                   jax.ShapeDtypeStruct((n_total, LANES), jnp.int32), jax.ShapeDtypeStruct((n_total, LANES), F32)],
        input_output_aliases={n_in + k: k for k in range(len(into))},
        compiler_params=_params(("parallel",)),
    )(x, oa, ob, om, gt, *consts, *into)


ROW_DMA_UNROLL = 8
MOE_BUFFERS = 3
DRAIN_STEPS = 2


TILE_ROWS = 8


def _rows_from_tiles(ref, n):
    return jnp.concatenate([ref[pl.ds(s, n, stride=TILE_ROWS), :] for s in range(TILE_ROWS)], axis=1)


def _rows_to_tiles(ref, x):
    for s in range(TILE_ROWS):
        ref[pl.ds(s, x.shape[0], stride=TILE_ROWS), :] = x[:, s * LANES:(s + 1) * LANES]


def _row_copies(asg_ref, base, count, n_asg, x_hbm, buf, y_hbm, sem, gather, unrolled):
    def tile(ref, idx):
        start = idx * TILE_ROWS
        return ref.at[pl.ds(start if isinstance(idx, int) else pl.multiple_of(start, TILE_ROWS), TILE_ROWS)]

    def one(r, priority):
        a = asg_ref[base + r]
        if gather:
            tok = jnp.minimum(a, n_asg - 1)
            tok = jnp.where(tok >= n_asg // 2, tok - n_asg // 2, tok)
            copy = pltpu.make_async_copy(tile(x_hbm, tok), tile(buf, r), sem)
        else:
            copy = pltpu.make_async_copy(tile(buf, r), tile(y_hbm, a), sem)
        copy.start(priority=priority)

    if unrolled:
        for r in range(count):
            one(r, r % 2)
        return

    def body(g, carry):
        for j in range(ROW_DMA_UNROLL):
            one(g * ROW_DMA_UNROLL + j, j % 2)
        return carry
    lax.fori_loop(0, count // ROW_DMA_UNROLL, body, 0)


def _moe_expert_kernel(n_asg, asg_ref, be_ref, nu_ref, x_hbm, wg_ref, wu_ref, wd_ref, y_hbm,
                       xbuf, ybuf, wgb, wub, wdb, gsem, ssem):
    i = pl.program_id(0)
    used = nu_ref[0]
    rows = xbuf.shape[1] // TILE_ROWS
    n_buf = xbuf.shape[0]
    n_blocks = be_ref.shape[0]
    blk = jnp.minimum(i, n_blocks - 1)

    @pl.when((i == 0) | (be_ref[blk] != be_ref[jnp.maximum(blk - 1, 0)]))
    def _():
        wgb[...] = wg_ref[0, 0].astype(BF16)
        wub[...] = wu_ref[0, 0].astype(BF16)
        wdb[...] = wd_ref[0, 0].astype(BF16)

    def wait_gather(slot):
        pltpu.make_async_copy(x_hbm.at[pl.ds(0, rows * TILE_ROWS)], xbuf.at[slot], gsem.at[slot]).wait()

    def wait_scatter(slot):
        pltpu.make_async_copy(ybuf.at[slot], y_hbm.at[pl.ds(0, rows * TILE_ROWS)], ssem.at[slot]).wait()

    def gather(b, slot, unrolled):
        _row_copies(asg_ref, b * rows, rows, n_asg, x_hbm, xbuf.at[slot], y_hbm, gsem.at[slot], True, unrolled)

    def scatter(b, slot, unrolled):
        _row_copies(asg_ref, b * rows, rows, n_asg, x_hbm, ybuf.at[slot], y_hbm, ssem.at[slot], False, unrolled)

    def expert(slot):
        xb = _rows_from_tiles(xbuf.at[slot], rows).astype(BF16)
        hg = jnp.dot(xb, wgb[...], preferred_element_type=F32)
        hu = jnp.dot(xb, wub[...], preferred_element_type=F32)
        h = hg * _sigmoid(hg) * hu
        _rows_to_tiles(ybuf.at[slot], jnp.dot(h.astype(BF16), wdb[...], preferred_element_type=F32))

    @pl.when(i == 0)
    def _():
        ybuf[1] = jnp.zeros(ybuf.shape[1:], F32)
        spare = [pltpu.make_async_copy(ybuf.at[1],
                                       y_hbm.at[pl.ds((n_asg + j * rows) * TILE_ROWS, rows * TILE_ROWS)], ssem.at[1])
                 for j in range((y_hbm.shape[0] // TILE_ROWS - n_asg) // rows)]
        for copy in spare:
            copy.start()
        for copy in spare:
            copy.wait()

    @pl.when((i >= 2) & (i - 2 < used))
    def _():
        wait_scatter((i - 2) % n_buf)

    steady = (i >= 1) & (i + 2 < used)
    for slot in range(n_buf):
        @pl.when(steady & (i % n_buf == slot))
        def _():
            wait_gather(slot)
            gather(i + 2, (slot + 2) % n_buf, True)
            scatter(i - 1, (slot - 1) % n_buf, True)
            expert(slot)

    @pl.when(jnp.logical_not(steady))
    def _():
        slot = i % n_buf

        @pl.when(i == 0)
        def _():
            for b in range(2):
                @pl.when(b < used)
                def _():
                    gather(b, b, False)

        @pl.when(i < used)
        def _():
            wait_gather(slot)

        @pl.when(i + 2 < used)
        def _():
            gather(i + 2, (i + 2) % n_buf, False)

        @pl.when((i >= 1) & (i - 1 < used))
        def _():
            scatter(i - 1, (i - 1) % n_buf, False)

        @pl.when(i < used)
        def _():
            expert(slot)


def moe_experts(x1_tiles, n_asg, asg, blk_e, n_used, e_gate, e_up, e_down):
    d = e_gate.shape[2]
    assert d == TILE_ROWS * LANES and x1_tiles.shape[1] == LANES
    n_blocks = blk_e.shape[0]
    ff = e_gate.shape[-1]
    n_rows = n_asg + e_gate.shape[1] * EXPERT_BLOCK
    weight = lambda shape: pl.BlockSpec(
        (1, 1) + shape, lambda i, asg, be, nu: (0, be[jnp.minimum(i, n_blocks - 1)], 0, 0))
    return pl.pallas_call(
        functools.partial(_moe_expert_kernel, n_asg), name="moe_expert",
        grid_spec=pltpu.PrefetchScalarGridSpec(
            num_scalar_prefetch=3,
            grid=(n_blocks + DRAIN_STEPS,),
            in_specs=[pl.BlockSpec(memory_space=pl.ANY), weight((d, ff)), weight((d, ff)), weight((ff, d))],
            out_specs=pl.BlockSpec(memory_space=pl.ANY),
            scratch_shapes=[pltpu.VMEM((MOE_BUFFERS, EXPERT_BLOCK * TILE_ROWS, LANES), F32),
                            pltpu.VMEM((MOE_BUFFERS, EXPERT_BLOCK * TILE_ROWS, LANES), F32),
                            pltpu.VMEM((d, ff), BF16), pltpu.VMEM((d, ff), BF16), pltpu.VMEM((ff, d), BF16),
                            pltpu.SemaphoreType.DMA((MOE_BUFFERS,)), pltpu.SemaphoreType.DMA((MOE_BUFFERS,))]),
        out_shape=jax.ShapeDtypeStruct((n_rows * TILE_ROWS, LANES), F32),
        compiler_params=_params(("arbitrary",)),
    )(asg, blk_e, n_used, x1_tiles, e_gate, e_up, e_down)


def _moe_combine_kernel(alpha, lead_tiles, y0_ref, y1_ref, x1_ref, gate_ref, g_ref, b_ref, lead_ref, tail_ref):
    i = pl.program_id(0)
    gate = gate_ref[...]
    tm = x1_ref.shape[0]
    moe = gate[:, 0:1] * _rows_from_tiles(y0_ref, tm) + gate[:, 1:2] * _rows_from_tiles(y1_ref, tm)
    out = _ln(alpha * x1_ref[...] + moe, g_ref[...], b_ref[...])

    @pl.when(i < lead_tiles)
    def _():
        lead_ref[...] = out

    @pl.when(i >= lead_tiles)
    def _():
        tail_ref[...] = out


def moe_combine(y, x1, gate, g, b, tm, alpha, n, n_lead):
    d = x1.shape[1]
    assert n_lead % tm == 0 and (n - n_lead) % tm == 0 and n > n_lead
    lead_tiles = n_lead // tm
    return pl.pallas_call(
        functools.partial(_moe_combine_kernel, alpha, lead_tiles), name="moe_combine",
        grid=(n // tm,),
        in_specs=[pl.BlockSpec((tm * TILE_ROWS, LANES), lambda i: (i, 0)),
                  pl.BlockSpec((tm * TILE_ROWS, LANES), lambda i: (i + n // tm, 0)),
                  pl.BlockSpec((tm, d), lambda i: (i, 0)),
                  pl.BlockSpec((tm, LANES), lambda i: (i, 0)), _const_spec((1, d)), _const_spec((1, d))],
        out_specs=[pl.BlockSpec((tm, d), lambda i: (jnp.minimum(i, lead_tiles - 1), 0)),
                   pl.BlockSpec((tm, d), lambda i: (jnp.maximum(i - lead_tiles, 0), 0))],
        out_shape=[jax.ShapeDtypeStruct((n_lead, d), F32), jax.ShapeDtypeStruct((n - n_lead, d), F32)],
        compiler_params=_params(("arbitrary",)),
    )(y, y, x1, gate, g.reshape(1, d), b.reshape(1, d))


def moe_routing(experts):
    n, top_k = experts.shape
    n_exp = N_GROUPS * EXPERTS_PER_GROUP
    a = n * top_k
    flat_e = experts.T.reshape(a)
    order = jnp.argsort(flat_e, stable=True).astype(jnp.int32)
    counts = jnp.sum((flat_e[:, None] == jnp.arange(n_exp, dtype=jnp.int32)[None, :]).astype(jnp.int32), axis=0)
    ends = jnp.cumsum(counts)
    padded = (counts + EXPERT_BLOCK - 1) // EXPERT_BLOCK * EXPERT_BLOCK
    pad_end = jnp.cumsum(padded)
    n_blocks = -(-a // EXPERT_BLOCK) + n_exp
    blk_start = jnp.arange(n_blocks, dtype=jnp.int32) * EXPERT_BLOCK
    blk_e = jnp.minimum(jnp.sum((pad_end[None, :] <= blk_start[:, None]).astype(jnp.int32), axis=1), n_exp - 1)
    slot = jnp.arange(n_blocks * EXPERT_BLOCK, dtype=jnp.int32)
    slot_e = jnp.repeat(blk_e, EXPERT_BLOCK)
    rank = slot - (pad_end - padded)[slot_e]
    spare = a + jnp.clip(slot - ends[slot_e], 0, n_exp * EXPERT_BLOCK - 1)
    asg = jnp.where(rank < counts[slot_e], order[jnp.clip((ends - counts)[slot_e] + rank, 0, a - 1)], spare)
    n_used = (pad_end[-1:] // EXPERT_BLOCK).astype(jnp.int32)
    return asg.astype(jnp.int32), blk_e.astype(jnp.int32), n_used


def hier_moe_ln(x1, x1_tiles, eidx, gate, w, tm, alpha, n_tokens, n_lead):
    asg, blk_e, n_used = moe_routing(eidx[:n_tokens, :2])
    y = moe_experts(x1_tiles, 2 * n_tokens, asg, blk_e, n_used, w['e_gate'], w['e_up'], w['e_down'])
    return moe_combine(y, x1, gate, w['ln2_g'], w['ln2_b'], tm, alpha, n_tokens, n_lead)


def kernel(x_prompt, x_sample, mem_prompt, state_wkv, state_shift, cache_win_k, cache_win_v, cache_mem_k, cache_mem_v, ln_in_g, ln_in_b, w_in, mu, w0, w_up, a0, a_up, g_up, k_k, k_a, r_k, lnx_g, lnx_b, sinks, w_mem_kv, p_a, p_b, p_m, w_o, ln1_g, ln1_b, w_group, b_group, w_router, b_router, e_gate, e_up, e_down, ln2_g, ln2_b):
    depth = w_in.shape[0]
    assert depth == 1, "single-layer step"
    bsz, seq, d = x_prompt.shape
    dec = x_sample.shape[0]
    assert x_sample.shape[1] == 1
    c_shift = mu.shape[-1]
    c_a = w0.shape[-1]
    window, kv_w = cache_win_k.shape[2], cache_win_k.shape[3] * cache_win_k.shape[4]
    n_mem, mem_w = cache_mem_k.shape[2], cache_mem_k.shape[3] * cache_mem_k.shape[4]
    q_w = sinks.shape[-1] * HEAD
    qkvm_w = q_w + 2 * kv_w + mem_w
    alpha = (2.0 * depth) ** 0.25
    past_len = float(PAST_LEN)
    chunk = 64

    w_in_b = w_in[0].astype(BF16)
    w_parts = [w_in_b[:, :c_shift], w_in_b[:, c_shift:c_shift + qkvm_w], w_in_b[:, c_shift + qkvm_w:]]
    rp = dict(mu=mu[0], w0=w0[0], w_up=w_up[0], a0=a0[0], a_up=a_up[0], g_up=g_up[0], k_k=k_k[0], k_a=k_a[0],
              r_k=r_k[0].reshape(-1))
    n_route = N_GROUPS * (1 + EXPERTS_PER_GROUP)
    mw = dict(ln_in_g=ln_in_g.reshape(1, d), ln_in_b=ln_in_b.reshape(1, d), p_a=p_a[0].astype(BF16),
              p_b=p_b[0].astype(BF16), p_m=p_m[0].astype(BF16), w_o=w_o[0].astype(BF16),
              ln1_g=ln1_g[0].reshape(1, d), ln1_b=ln1_b[0].reshape(1, d),
              w_route=jnp.stack(_pieces(jnp.pad(jnp.concatenate([w_group[0], w_router[0]], axis=1),
                                                ((0, 0), (0, LANES - n_route))), 2)),
              b_route=jnp.pad(jnp.concatenate([b_group[0], b_router[0]]), (0, LANES - n_route)).reshape(1, LANES),
              e_gate=e_gate, e_up=e_up, e_down=e_down, ln2_g=ln2_g[0], ln2_b=ln2_b[0])

    xp = x_prompt.reshape(bsz * seq, d)
    zq, zg, shift_p, prep = ln_proj_prep(x_prompt, ln_in_g, ln_in_b, w_parts, jnp.zeros((bsz, 1, c_shift), F32), rp,
                                         chunk, 256, BF16)
    o_a, wkv_p = wkv(prep, jnp.zeros((bsz, c_a // HEAD, HEAD, HEAD), F32), lnx_g[0], lnx_b[0], chunk)
    mkv = proj(mem_prompt.reshape(bsz * n_mem, d), w_mem_kv[0].astype(BF16), 256).reshape(bsz, n_mem, 2 * mem_w)
    mk_p, mv_p = mkv[..., :mem_w], mkv[..., mem_w:]
    tables = rope_tables(jnp.arange(seq, dtype=F32))
    o_b, o_m, k_rot = attn_prompt(zq, mk_p, mv_p, sinks[0], tables, window, q_w, kv_w)
    n_all = bsz * seq + dec
    n_buf = -(-n_all // 256) * 256
    routed = merge(xp, o_a.reshape(-1, c_a), o_b.reshape(-1, q_w), o_m.reshape(-1, mem_w),
                   zg.reshape(bsz * seq, -1), mw, 256, alpha, n_buf)
    shift_p = shift_p[:, 0]
    kb_p = k_rot[:, -window:].reshape(bsz, window, H_KV, HEAD)
    vb_p = zq[:, -window:, q_w + kv_w:q_w + 2 * kv_w].reshape(bsz, window, H_KV, HEAD)

    xs = x_sample.reshape(dec, d)
    zq_s, zg_s, zr_s, ops_s = ln_proj_prep(xs.reshape(1, dec, d), ln_in_g, ln_in_b, w_parts,
                                           state_shift[0].reshape(1, dec, c_shift), rp, 1, dec, F32)
    zq_s, zg_s, zr_s = zq_s[0], zg_s[0], zr_s[0]
    o_a_s, wkv_s = wkv_step([a.reshape(dec, c_a) for a in ops_s], state_wkv[0], lnx_g[0], lnx_b[0], 8)
    tables_s = rope_tables(jnp.full((1,), past_len, F32))
    o_b_s, o_m_s, nk_s, nv_s = attn_step(
        zq_s, cache_win_k[0].reshape(dec, window, kv_w), cache_win_v[0].reshape(dec, window, kv_w),
        cache_mem_k[0].reshape(dec, n_mem, mem_w), cache_mem_v[0].reshape(dec, n_mem, mem_w),
        sinks[0], tables_s, q_w, kv_w, 8)
    x1, x1_tiles, eidx, gate = merge(xs, o_a_s, o_b_s, o_m_s, zg_s, mw, dec, alpha, n_buf, bsz * seq, routed)

    y_prompt, y_sample = hier_moe_ln(x1, x1_tiles, eidx, gate, mw, dec, alpha, n_all, bsz * seq)
    y_prompt = y_prompt.reshape(bsz, seq, d)
    y_sample = y_sample.reshape(dec, 1, d)

    sd = state_wkv.dtype
    return (y_prompt, y_sample, wkv_p[None].astype(sd), wkv_s[None].astype(sd), shift_p[None], zr_s[None],
            kb_p[None], vb_p[None], nk_s.reshape(dec, window, H_KV, HEAD)[None],
            nv_s.reshape(dec, window, H_KV, HEAD)[None],
            mk_p.reshape(bsz, n_mem, -1, HEAD)[None], mv_p.reshape(bsz, n_mem, -1, HEAD)[None])
```

```python
import functools
import math

import jax
import jax.numpy as jnp
from jax import lax
from jax.experimental import pallas as pl
from jax.experimental.pallas import tpu as pltpu

F32 = jnp.float32
BF16 = jnp.bfloat16
SCAN_BATCH = 8
PROJ_PIECE = 512
ATTN_BLOCKS = 4
CHUNKS_PER_STEP = 4

HEAD = 64
LANES = 128
H_KV = 2
ROT_HALF = 8
ROPE_THETA = 500000.0
PAST_LEN = 8192
N_GROUPS = 4
EXPERTS_PER_GROUP = 8
EXPERT_BLOCK = 128
LN_EPS = 1e-5
LNX_EPS = 64e-5
NEG_INF = -1e30
VMEM_LIMIT = 48 * 1024 * 1024


def _pieces(x, n):
    out = []
    for _ in range(n):
        p = x.astype(BF16)
        out.append(p)
        x = x - p.astype(F32)
    return out


def _mask_dot(x, mask, n=2):
    return sum(jnp.dot(p, mask, preferred_element_type=F32) for p in _pieces(x, n))


def _split3(x, axis, lhs):
    hi = x.astype(BF16).astype(F32)
    lo = x - hi
    return jnp.concatenate([hi, hi, lo] if lhs else [hi, lo, hi], axis=axis).astype(BF16)


def _dot3(a, b):
    return jnp.dot(_split3(a, 1, True), _split3(b, 0, False), preferred_element_type=F32)


def _dot3_t(a, b):
    return lax.dot_general(_split3(a, 1, True), _split3(b, 1, False), (((1,), (1,)), ((), ())),
                           preferred_element_type=F32)


def _bdot(a, b):
    return jnp.dot(a.astype(BF16), b.astype(BF16), preferred_element_type=F32)


def _bdot_t(a, b):
    return lax.dot_general(a.astype(BF16), b.astype(BF16), (((1,), (1,)), ((), ())), preferred_element_type=F32)


def _sigmoid(x):
    return 0.5 * jnp.tanh(0.5 * x) + 0.5


def _ln(x, g, b):
    mu = jnp.mean(x, axis=-1, keepdims=True)
    xc = x - mu
    var = jnp.mean(xc * xc, axis=-1, keepdims=True)
    return xc * lax.rsqrt(var + LN_EPS) * g + b


def _const_spec(shape):
    nd = len(shape)
    return pl.BlockSpec(shape, lambda *_: (0,) * nd)


def _params(sem):
    return pltpu.CompilerParams(dimension_semantics=sem, vmem_limit_bytes=VMEM_LIMIT)


def _ln_proj_prep_kernel(chunk, x_ref, g_ref, b_ref, wr_ref, wq_ref, wg_ref, prev_ref, mu_ref, w0_ref, wup_ref,
                         a0_ref, aup_ref, gup_ref, kk_ref, ka_ref, rk_ref, hsum_ref, tril_ref,
                         zq_ref, zg_ref, zr_ref, *refs):
    out_refs, carry_ref = refs[:-1], refs[-1]
    xn = _ln(x_ref[0], g_ref[...], b_ref[...]).astype(BF16)
    z = jnp.dot(xn, wr_ref[...], preferred_element_type=F32)
    pending = [(o_ref, w_ref, c) for o_ref, w_ref in ((zq_ref, wq_ref), (zg_ref, wg_ref))
               for c in range(0, w_ref.shape[1], PROJ_PIECE)]

    def project(n_pieces):
        for _ in range(min(n_pieces, len(pending))):
            o_ref, w_ref, c = pending.pop(0)
            o_ref[0, :, c:c + PROJ_PIECE] = jnp.dot(xn, w_ref[:, c:c + PROJ_PIECE],
                                                    preferred_element_type=F32).astype(o_ref.dtype)

    tt = z.shape[0]
    c_a = w0_ref.shape[-1]
    r_w, r_a, r_g = wup_ref.shape[0], aup_ref.shape[0], gup_ref.shape[0]
    if chunk == 1:
        zr_ref[0] = z
        prev = prev_ref[0]
    else:
        zr_ref[0] = z[tt - 1:tt, :]

        @pl.when(pl.program_id(1) == 0)
        def _():
            carry_ref[...] = prev_ref[0]

        row = lax.broadcasted_iota(jnp.int32, z.shape, 0)
        prev = jnp.where(row == 0, carry_ref[...], pltpu.roll(z, 1, 0))
        carry_ref[...] = z[tt - 1:tt, :]
    zs = z + (prev - z) * mu_ref[...]
    r = zs[:, :c_a]
    k = zs[:, c_a:2 * c_a]
    v = zs[:, 2 * c_a:3 * c_a]
    o = 3 * c_a
    xw = zs[:, o:o + r_w]
    xa = zs[:, o + r_w:o + r_w + r_a]
    xg = zs[:, o + r_w + r_a:o + r_w + r_a + r_g]
    project(1)
    warg = -(w0_ref[...] + _dot3(jnp.tanh(xw), wup_ref[...]))
    softplus = jnp.maximum(warg, 0.0) + jnp.log1p(jnp.exp(-jnp.abs(warg)))
    lw = -jnp.exp(-softplus - 0.5)
    project(1)
    a = _sigmoid(a0_ref[...] + _dot3(xa, aup_ref[...]))
    g = _dot3(_sigmoid(xg), gup_ref[...])
    project(1)
    kkr = k * kk_ref[...]
    kk = kkr / jnp.maximum(jnp.sqrt(_mask_dot(kkr * kkr, hsum_ref[...])), 1e-12)
    project(1)
    k2 = k * (1.0 + (a - 1.0) * ka_ref[...])
    bonus = _mask_dot(r * k2 * rk_ref[...], hsum_ref[...]) * v
    kb = kk * a
    project(1)
    if chunk == 1:
        outs = (r, jnp.exp(lw), k2, v, -kk, kb, g, bonus)
        for ref, val in zip(out_refs, outs):
            ref[0] = val
        project(len(pending))
        return
    pieces = _pieces(lw, 3)
    ones3 = jnp.ones((chunk, 3 * chunk), BF16)
    cw, cwl = [], []
    for c in range(tt // chunk):
        stack = jnp.concatenate([p[c * chunk:(c + 1) * chunk] for p in pieces], axis=0)
        cw.append(jnp.dot(tril_ref[...], stack, preferred_element_type=F32))
        cwl.append(jnp.dot(ones3, stack, preferred_element_type=F32))
    cw = jnp.concatenate(cw, axis=0)
    cwl = jnp.concatenate(cwl, axis=0)
    project(1)
    e_inv = jnp.exp(-cw)
    outs = (r * jnp.exp(cw), -kk * jnp.exp(cw - lw), kb * e_inv, k2 * e_inv, v, g, bonus)
    for ref, val in zip(out_refs[:-1], outs):
        ref[0] = val
        project(1)
    wl = jnp.exp(cwl)
    for c in range(tt // chunk):
        out_refs[-1][0, c] = wl[c * chunk:c * chunk + 1, :]
    project(len(pending))


def _head_sum_matrix(width):
    idx = jnp.arange(width)
    return ((idx[:, None] // HEAD) == (idx[None, :] // HEAD)).astype(BF16)


def ln_proj_prep(x, g, b, ws, prev, p, chunk, tt, gate_dtype):
    bsz, t, d = x.shape
    w_r, w_q, w_g = ws
    cs = w_r.shape[1]
    c_a = p['w0'].shape[-1]
    assert prev.shape[1] == (t if chunk == 1 else 1)
    ridx = jnp.arange(chunk)
    tril = jnp.tile((ridx[None, :] <= ridx[:, None]).astype(BF16), (1, 3))
    hsum = _head_sum_matrix(c_a)
    row = lambda a: a.reshape(1, -1)
    rows = lambda width: pl.BlockSpec((1, tt, width), lambda bi, i: (bi, i, 0))
    full = jax.ShapeDtypeStruct((bsz, t, c_a), F32)
    if chunk == 1:
        out_specs, out_shape = [rows(c_a)] * 8, [full] * 8
        prev_spec, zr_spec, zr_rows = rows(cs), rows(cs), t
    else:
        out_specs = [rows(c_a)] * 7 + [pl.BlockSpec((1, tt // chunk, 1, c_a), lambda bi, i: (bi, i, 0, 0))]
        out_shape = [full] * 7 + [jax.ShapeDtypeStruct((bsz, t // chunk, 1, c_a), F32)]
        prev_spec = zr_spec = pl.BlockSpec((1, 1, cs), lambda bi, i: (bi, 0, 0))
        zr_rows = 1
    consts = [row(p['mu']), row(p['w0']), p['w_up'], row(p['a0']), p['a_up'], p['g_up'], row(p['k_k']),
              row(p['k_a']), row(p['r_k']), hsum, tril]
    weight = lambda w: pl.BlockSpec(w.shape, lambda bi, i: (0, 0), pipeline_mode=pl.Buffered(1))
    outs = pl.pallas_call(
        functools.partial(_ln_proj_prep_kernel, chunk), name="ln_proj_prep",
        grid=(bsz, t // tt),
        in_specs=[rows(d), _const_spec((1, d)), _const_spec((1, d)), weight(w_r), weight(w_q), weight(w_g), prev_spec]
        + [_const_spec(c.shape) for c in consts],
        out_specs=[rows(w_q.shape[1]), rows(w_g.shape[1]), zr_spec] + out_specs,
        out_shape=[jax.ShapeDtypeStruct((bsz, t, w_q.shape[1]), F32),
                   jax.ShapeDtypeStruct((bsz, t, w_g.shape[1]), gate_dtype),
                   jax.ShapeDtypeStruct((bsz, zr_rows, cs), F32)] + out_shape,
        scratch_shapes=[pltpu.VMEM((1, cs), F32)],
        compiler_params=_params(("parallel", "arbitrary")),
    )(x, row(g), row(b), w_r, w_q, w_g, prev, *consts)
    return outs[0], outs[1], outs[2], outs[3:]


def _wkv_chunk_kernel(ra_ref, at_ref, bt_ref, kt_ref, v_ref, wl_ref, rp_ref, y0_ref, m_ref, n_ref):
    n_chunks = m_ref.shape[1]
    length = ra_ref.shape[1] // n_chunks
    heads = ra_ref.shape[2] // HEAD
    row = lax.broadcasted_iota(jnp.int32, (length, length), 0)
    col = lax.broadcasted_iota(jnp.int32, (length, length), 1)
    strict = row > col
    incl = row >= col
    hrow = lax.broadcasted_iota(jnp.int32, (HEAD, HEAD), 0)
    hcol = lax.broadcasted_iota(jnp.int32, (HEAD, HEAD), 1)
    units = [(c, h) for c in range(n_chunks) for h in range(heads)]
    us = range(len(units))
    at = lambda ref, u: ref[0, units[u][0] * length:(units[u][0] + 1) * length,
                            units[u][1] * HEAD:(units[u][1] + 1) * HEAD]
    gram = [_bdot_t(jnp.concatenate([at(at_ref, u), at(ra_ref, u)], axis=0),
                    jnp.concatenate([at(bt_ref, u), at(kt_ref, u)], axis=0)) for u in us]
    a_ab = [jnp.where(strict, g[:length, :length], 0.0) for g in gram]
    a_kk = [jnp.concatenate([jnp.where(strict, g[:length, length:], 0.0),
                             jnp.where(incl, g[length:, length:], 0.0)], axis=0) for g in gram]
    a_rb = [jnp.where(incl, g[length:, :length], 0.0) for g in gram]
    inv = [jnp.where(row == col, 1.0, a) for a in a_ab]
    pw = [_bdot(a, a) for a in a_ab]
    avy = [_bdot(a_kk[u], at(v_ref, u)) for u in us]
    w_l = [wl_ref[0, units[u][0], :, units[u][1] * HEAD:(units[u][1] + 1) * HEAD] for u in us]
    nk = [_bdot(at(v_ref, u).T, at(kt_ref, u) * w_l[u]) for u in us]
    for _ in range(int(math.log2(length)) - 2):
        both = [_bdot(jnp.concatenate([pw[u], inv[u]], axis=0), pw[u]) for u in us]
        pw = [b[:length] for b in both]
        inv = [inv[u] + both[u][length:] for u in us]
    inv = [inv[u] + _bdot(inv[u], pw[u]) for u in us]
    pq = [_bdot(inv[u], jnp.concatenate([at(at_ref, u), avy[u][:length]], axis=1)) for u in us]
    ry = [_bdot(a_rb[u], pq[u]) for u in us]
    mn = [_bdot(pq[u].T, at(bt_ref, u) * w_l[u]) for u in us]
    lanes = lambda f, c: jnp.concatenate([f(c * heads + h) for h in range(heads)], axis=1)
    rows = lambda f: jnp.concatenate([lanes(f, c) for c in range(n_chunks)], axis=0)
    rp_ref[0] = rows(lambda u: at(ra_ref, u) + ry[u][:, :HEAD])
    y0_ref[0] = rows(lambda u: avy[u][length:] + ry[u][:, HEAD:])
    for c in range(n_chunks):
        m_ref[0, c] = lanes(lambda u: jnp.where(hrow == hcol, w_l[u], 0.0) + mn[u][:HEAD], c)
        n_ref[0, c] = lanes(lambda u: nk[u] + mn[u][HEAD:], c)


def _head_norm_wide(y, hsum):
    yc = y - _mask_dot(y, hsum) * (1.0 / HEAD)
    return yc * lax.rsqrt(_mask_dot(yc * yc, hsum) * (1.0 / HEAD) + LNX_EPS)


def _wkv_scan_kernel(rp_ref, y0_ref, m_ref, n_ref, g_ref, bonus_ref, s0_ref, lg_ref, lb_ref, hsum_ref,
                     o_ref, sout_ref, s_ref):
    c = pl.program_id(1)
    nb = rp_ref.shape[0]
    heads = rp_ref.shape[2] // HEAD

    @pl.when(c == 0)
    def _():
        s_ref[...] = s0_ref[...]

    pairs = [(b, h, slice(h * HEAD, (h + 1) * HEAD)) for b in range(nb) for h in range(heads)]
    s = [s_ref[b, h] for b, h, _ in pairs]
    s_new = [n_ref[b, 0, :, sl] + _dot3(s[j], m_ref[b, 0, :, sl]) for j, (b, h, sl) in enumerate(pairs)]
    ys = [y0_ref[b, :, sl] + _bdot_t(rp_ref[b, :, sl], s[j]) for j, (b, h, sl) in enumerate(pairs)]
    for j, (b, h, _) in enumerate(pairs):
        s_ref[b, h] = s_new[j]
    length = rp_ref.shape[1]
    y = _head_norm_wide(jnp.concatenate([jnp.concatenate(ys[b * heads:(b + 1) * heads], axis=-1) for b in range(nb)],
                                        axis=0), hsum_ref[...])
    for b in range(nb):
        o = (y[b * length:(b + 1) * length] * lg_ref[...] + lb_ref[...] + bonus_ref[b]) * g_ref[b]
        o_ref[b] = o.astype(o_ref.dtype)

    @pl.when(c == pl.num_programs(1) - 1)
    def _():
        sout_ref[...] = s_ref[...]


def wkv(prep, s0, lnx_g, lnx_b, chunk):
    ra, at, bt, kt, v, g, bonus, wl = prep
    bsz, t, c_a = ra.shape
    heads = c_a // HEAD
    n_chunks = t // chunk
    assert chunk & (chunk - 1) == 0 and chunk >= 4 and t % chunk == 0
    per_step = CHUNKS_PER_STEP if n_chunks % CHUNKS_PER_STEP == 0 else 1
    tile = pl.BlockSpec((1, per_step * chunk, c_a), lambda b, c: (b, c, 0))
    mat = pl.BlockSpec((1, per_step, HEAD, c_a), lambda b, c: (b, c, 0, 0))
    full = jax.ShapeDtypeStruct((bsz, t, c_a), F32)
    mats = jax.ShapeDtypeStruct((bsz, n_chunks, HEAD, c_a), F32)
    rp, y0, m, n = pl.pallas_call(
        _wkv_chunk_kernel, name="wkv_chunk",
        grid=(bsz, n_chunks // per_step),
        in_specs=[tile] * 5 + [pl.BlockSpec((1, per_step, 1, c_a), lambda b, c: (b, c, 0, 0))],
        out_specs=[tile, tile, mat, mat],
        out_shape=[full, full, mats, mats],
        compiler_params=_params(("parallel", "parallel")),
    )(ra, at, bt, kt, v, wl)
    nb = SCAN_BATCH if bsz % SCAN_BATCH == 0 else 1
    tile = pl.BlockSpec((nb, chunk, c_a), lambda b, c: (b, c, 0))
    mat = pl.BlockSpec((nb, 1, HEAD, c_a), lambda b, c: (b, c, 0, 0))
    st = pl.BlockSpec((nb, heads, HEAD, HEAD), lambda b, c: (b, 0, 0, 0))
    return pl.pallas_call(
        _wkv_scan_kernel, name="wkv_scan",
        grid=(bsz // nb, n_chunks),
        in_specs=[tile, tile, mat, mat, tile, tile, st, _const_spec((1, c_a)), _const_spec((1, c_a)),
                  _const_spec((c_a, c_a))],
        out_specs=[tile, st],
        out_shape=[jax.ShapeDtypeStruct((bsz, t, c_a), BF16), jax.ShapeDtypeStruct(s0.shape, F32)],
        scratch_shapes=[pltpu.VMEM((nb, heads, HEAD, HEAD), F32)],
        compiler_params=_params(("parallel", "arbitrary")),
    )(rp, y0, m, n, g, bonus, s0, lnx_g.reshape(1, c_a), lnx_b.reshape(1, c_a), _head_sum_matrix(c_a))


def _wkv_step_kernel(r_ref, w_ref, k_ref, v_ref, a_ref, b_ref, g_ref, bonus_ref, s_ref, lg_ref, lb_ref, hsum_ref,
                     o_ref, sout_ref):
    bt, heads = s_ref.shape[0], s_ref.shape[1]
    c_a = heads * HEAD
    hsum = hsum_ref[...]
    diag = (lax.broadcasted_iota(jnp.int32, (HEAD, c_a), 1) % HEAD
            == lax.broadcasted_iota(jnp.int32, (HEAD, c_a), 0))
    seqs = range(bt)
    row = lambda ref, i: ref[i:i + 1, :]
    stack = lambda f: jnp.concatenate([f(i) for i in seqs], axis=0)
    piece = lambda x, i: x[i * HEAD:(i + 1) * HEAD]
    s = stack(lambda i: jnp.concatenate([s_ref[i, h] for h in range(heads)], axis=1))
    sa = _mask_dot(s * stack(lambda i: jnp.broadcast_to(row(a_ref, i), (HEAD, c_a))), hsum)
    v_rows = _mask_dot(stack(lambda i: jnp.where(diag, row(v_ref, i), 0.0)), hsum)
    s = stack(lambda i: piece(s, i) * row(w_ref, i) + piece(sa, i) * row(b_ref, i) + piece(v_rows, i) * row(k_ref, i))
    for i in seqs:
        for h in range(heads):
            sout_ref[i, h] = piece(s, i)[:, h * HEAD:(h + 1) * HEAD]
    y_rows = _mask_dot(stack(lambda i: piece(s, i) * row(r_ref, i)), hsum)
    y = stack(lambda i: jnp.sum(jnp.where(diag, piece(y_rows, i), 0.0), axis=0, keepdims=True))
    y = _head_norm_wide(y, hsum)
    o_ref[...] = (y * lg_ref[...] + lb_ref[...] + bonus_ref[...]) * g_ref[...]


def wkv_step(ops, s0, lnx_g, lnx_b, bt):
    bsz, c_a = ops[0].shape
    rows = pl.BlockSpec((bt, c_a), lambda i: (i, 0))
    st = pl.BlockSpec((bt,) + s0.shape[1:], lambda i: (i, 0, 0, 0))
    hsum = _head_sum_matrix(c_a)
    return pl.pallas_call(
        _wkv_step_kernel, name="wkv_step",
        grid=(bsz // bt,),
        in_specs=[rows] * 8 + [st, _const_spec((1, c_a)), _const_spec((1, c_a)), _const_spec(hsum.shape)],
        out_specs=[rows, st],
        out_shape=[jax.ShapeDtypeStruct((bsz, c_a), F32), jax.ShapeDtypeStruct(s0.shape, F32)],
        compiler_params=_params(("parallel",)),
    )(*ops, s0, lnx_g.reshape(1, c_a), lnx_b.reshape(1, c_a), hsum)


def rope_tables(pos):
    inv_freq = ROPE_THETA ** (-jnp.arange(ROT_HALF, dtype=F32) / ROT_HALF)
    ang = pos[:, None] * inv_freq[None, :]
    cos, sin = jnp.cos(ang), jnp.sin(ang)
    t = pos.shape[0]
    rest = HEAD - 2 * ROT_HALF
    c = jnp.concatenate([cos, cos, jnp.ones((t, rest), F32)], axis=1)
    s1 = jnp.concatenate([jnp.zeros((t, ROT_HALF), F32), sin, jnp.zeros((t, rest), F32)], axis=1)
    s2 = jnp.concatenate([-sin, jnp.zeros((t, HEAD - ROT_HALF), F32)], axis=1)
    rep = LANES // HEAD
    return tuple(jnp.tile(x, (1, rep)) for x in (c, s1, s2))


def _rope(x, c, s1, s2):
    width = x.shape[-1]
    rep = width // c.shape[-1]
    if rep > 1:
        c, s1, s2 = (jnp.concatenate([t] * rep, axis=1) for t in (c, s1, s2))
    return x * c + pltpu.roll(x, ROT_HALF, 1) * s1 + pltpu.roll(x, width - ROT_HALF, 1) * s2


def _attn_kernel(window, q_w, kv_w, zq_ref, kp_ref, vp_ref, rc_ref, rs1_ref, rs2_ref, pc_ref, ps1_ref, ps2_ref,
                 mk_ref, mv_ref, sink_ref, ob_ref, om_ref, kr_ref):
    zq = zq_ref[0]
    nblk = zq.shape[0] // window
    first = pl.program_id(1) * nblk
    scale = HEAD ** -0.5
    q = zq[:, :q_w]
    k = zq[:, q_w:q_w + kv_w]
    v = zq[:, q_w + kv_w:q_w + 2 * kv_w]
    qm = zq[:, q_w + 2 * kv_w:]
    qr = _rope(q, rc_ref[...], rs1_ref[...], rs2_ref[...]) * scale
    kr = _rope(k, rc_ref[...], rs1_ref[...], rs2_ref[...])
    kr_ref[0] = kr
    k_all = jnp.concatenate([_rope(kp_ref[0], pc_ref[...], ps1_ref[...], ps2_ref[...]), kr], axis=0)
    v_all = jnp.concatenate([vp_ref[0], v], axis=0)
    gqa = q_w // kv_w
    assert window & (window - 1) == 0
    qi = lax.broadcasted_iota(jnp.int32, (gqa * window, 2 * window), 0) & (window - 1)
    kj = lax.broadcasted_iota(jnp.int32, (gqa * window, 2 * window), 1)
    band = (kj > qi) & (kj <= qi + window)
    tdot = lambda a, b: lax.dot_general(a, b, (((1,), (1,)), ((), ())), preferred_element_type=F32)
    dot = lambda a, b: jnp.dot(a, b, preferred_element_type=F32)
    hsl = lambda h: slice(h * HEAD, (h + 1) * HEAD)
    rows = lambda i: slice(i * window, (i + 1) * window)
    keys = lambda i: slice(i * window, (i + 2) * window)
    wins = [(i, hk) for i in range(nblk) for hk in range(kv_w // HEAD)]
    mems = [(i, h) for i in range(nblk) for h in range(qm.shape[1] // HEAD)]
    qmb = (qm * scale).astype(BF16)
    mk = mk_ref[0].astype(BF16)
    mv = mv_ref[0].astype(BF16)
    kcat = [k_all[keys(i), hsl(hk)].astype(BF16) for i, hk in wins]
    vcat = [v_all[keys(i), hsl(hk)].astype(BF16) for i, hk in wins]
    qs = [jnp.concatenate([qr[rows(i), hsl(hk * gqa + g)] for g in range(gqa)], axis=0).astype(BF16)
          for i, hk in wins]
    s = [jnp.where(band & ((kj >= window) | (first + i > 0)), tdot(qs[u], kcat[u]), NEG_INF)
         for u, (i, hk) in enumerate(wins)]
    sm = [tdot(qmb[rows(i), hsl(h)], mk[:, hsl(h)]) for i, h in mems]
    sink = [jnp.concatenate([jnp.full((window, 1), sink_ref[hk * gqa + g], F32) for g in range(gqa)], axis=0)
            for i, hk in wins]
    m = [jnp.maximum(jnp.max(s[u], axis=-1, keepdims=True), sink[u]) for u in range(len(wins))]
    p = [jnp.exp(s[u] - m[u]) for u in range(len(wins))]
    pm = [jnp.exp(x - jnp.max(x, axis=-1, keepdims=True)) for x in sm]
    o = [dot(p[u].astype(BF16), vcat[u])
         / (jnp.sum(p[u], axis=-1, keepdims=True) + jnp.exp(sink[u] - m[u])) for u in range(len(wins))]
    om = [dot(pm[u].astype(BF16), mv[:, hsl(h)]) / jnp.sum(pm[u], axis=-1, keepdims=True)
          for u, (i, h) in enumerate(mems)]
    n_kv, n_mh = kv_w // HEAD, qm.shape[1] // HEAD
    ob = jnp.concatenate([jnp.concatenate([o[i * n_kv + hk][g * window:(g + 1) * window]
                                           for hk in range(n_kv) for g in range(gqa)], axis=1)
                          for i in range(nblk)], axis=0)
    ob_ref[0] = ob.astype(ob_ref.dtype)
    om_ref[0] = jnp.concatenate([jnp.concatenate(om[i * n_mh:(i + 1) * n_mh], axis=1) for i in range(nblk)],
                                axis=0).astype(om_ref.dtype)


def attn_prompt(zq, mk, mv, sinks, tables, window, q_w, kv_w):
    bsz, t, zw = zq.shape
    mem_w = zw - q_w - 2 * kv_w
    assert kv_w == LANES and q_w % kv_w == 0
    kcol, vcol = q_w // kv_w, q_w // kv_w + 1
    nblk = ATTN_BLOCKS if (t // window) % ATTN_BLOCKS == 0 else 1
    rows = nblk * window
    prev = lambda n: jnp.maximum(n * nblk - 1, 0)
    tab = pl.BlockSpec((rows, LANES), lambda b, n: (n, 0))
    ptab = pl.BlockSpec((window, LANES), lambda b, n: (prev(n), 0))
    mem = pl.BlockSpec((1,) + mk.shape[1:], lambda b, n: (b, 0, 0))
    return pl.pallas_call(
        functools.partial(_attn_kernel, window, q_w, kv_w), name="attn",
        grid=(bsz, t // rows),
        in_specs=[pl.BlockSpec((1, rows, zw), lambda b, n: (b, n, 0)),
                  pl.BlockSpec((1, window, kv_w), lambda b, n: (b, prev(n), kcol)),
                  pl.BlockSpec((1, window, kv_w), lambda b, n: (b, prev(n), vcol)),
                  tab, tab, tab, ptab, ptab, ptab, mem, mem,
                  pl.BlockSpec(memory_space=pltpu.SMEM)],
        out_specs=[pl.BlockSpec((1, rows, q_w), lambda b, n: (b, n, 0)),
                   pl.BlockSpec((1, rows, mem_w), lambda b, n: (b, n, 0)),
                   pl.BlockSpec((1, rows, kv_w), lambda b, n: (b, n, 0))],
        out_shape=[jax.ShapeDtypeStruct((bsz, t, q_w), BF16), jax.ShapeDtypeStruct((bsz, t, mem_w), BF16),
                   jax.ShapeDtypeStruct((bsz, t, kv_w), F32)],
        compiler_params=_params(("parallel", "parallel")),
    )(zq, zq, zq, *tables, *tables, mk, mv, sinks)


def _attn_step_kernel(q_w, kv_w, zq_ref, ck_ref, cv_ref, mk_ref, mv_ref, rc_ref, rs1_ref, rs2_ref, sink_ref,
                      ob_ref, om_ref, nk_ref, nv_ref):
    bt = zq_ref.shape[0]
    window = ck_ref.shape[1]
    mem_w = om_ref.shape[1]
    n_q, gqa, per_vreg = q_w // HEAD, q_w // kv_w, LANES // HEAD
    scale = HEAD ** -0.5
    zq = zq_ref[...]
    q = _rope(zq[:, :q_w], rc_ref[...], rs1_ref[...], rs2_ref[...]) * scale
    k_new = _rope(zq[:, q_w:q_w + kv_w], rc_ref[...], rs1_ref[...], rs2_ref[...])
    v_new = zq[:, q_w + kv_w:q_w + 2 * kv_w]
    qm = zq[:, q_w + 2 * kv_w:] * scale
    own = lambda w: (lax.broadcasted_iota(jnp.int32, (n_q, w), 1) // HEAD
                     == lax.broadcasted_iota(jnp.int32, (n_q, w), 0))
    own_q, own_m = own(q_w), own(mem_w)
    hrow = lax.broadcasted_iota(jnp.int32, (n_q, LANES), 0)
    hblk = lax.broadcasted_iota(jnp.int32, (n_q, LANES), 1) // HEAD
    swap = (hrow % per_vreg) != (hrow // gqa)
    keep = hblk == hrow % per_vreg
    key_ok = lax.broadcasted_iota(jnp.int32, (n_q, window), 1) >= 1
    wrow = lax.broadcasted_iota(jnp.int32, (window, kv_w), 0)
    sink = sink_ref[...]
    bs = range(bt)
    tdot = lambda a, b: lax.dot_general(a.astype(BF16), b.astype(BF16), (((1,), (1,)), ((), ())),
                                        preferred_element_type=F32)
    for b in bs:
        nk_ref[b] = jnp.where(wrow == window - 1, k_new[b:b + 1], pltpu.roll(ck_ref[b], window - 1, 0))
        nv_ref[b] = jnp.where(wrow == window - 1, v_new[b:b + 1], pltpu.roll(cv_ref[b], window - 1, 0))
    q8 = []
    for b in bs:
        rep = jnp.where(own_q, q[b:b + 1], 0.0)
        fold = sum(rep[:, c * LANES:(c + 1) * LANES] for c in range(q_w // LANES))
        q8.append(jnp.where(swap, pltpu.roll(fold, HEAD, 1), fold))
    qm8 = [jnp.where(own_m, qm[b:b + 1], 0.0) for b in bs]
    s = [jnp.where(key_ok, tdot(q8[b], ck_ref[b]), NEG_INF) for b in bs]
    sm = [tdot(qm8[b], mk_ref[b]) for b in bs]
    s_new = [jnp.sum(q8[b] * k_new[b:b + 1], axis=1, keepdims=True) for b in bs]
    m = [jnp.maximum(jnp.maximum(jnp.max(s[b], axis=1, keepdims=True), s_new[b]), sink) for b in bs]
    p = [jnp.exp(s[b] - m[b]) for b in bs]
    p_new = [jnp.exp(s_new[b] - m[b]) for b in bs]
    pm = [jnp.exp(sm[b] - jnp.max(sm[b], axis=1, keepdims=True)) for b in bs]
    o8 = [(_bdot(p[b], cv_ref[b]) + p_new[b] * v_new[b:b + 1])
          / (jnp.sum(p[b], axis=1, keepdims=True) + p_new[b] + jnp.exp(sink - m[b])) for b in bs]
    om8 = [_bdot(pm[b], mv_ref[b]) / jnp.sum(pm[b], axis=1, keepdims=True) for b in bs]
    ob, om = [], []
    for b in bs:
        o = jnp.where(swap, pltpu.roll(o8[b], HEAD, 1), o8[b])
        o = jnp.concatenate([jnp.where(keep, o, 0.0)] * (q_w // LANES), axis=1)
        ob.append(jnp.sum(jnp.where(own_q, o, 0.0), axis=0, keepdims=True))
        om.append(jnp.sum(jnp.where(own_m, om8[b], 0.0), axis=0, keepdims=True))
    ob_ref[...] = jnp.concatenate(ob, axis=0)
    om_ref[...] = jnp.concatenate(om, axis=0)


def attn_step(zq, ck, cv, mk, mv, sinks, tables, q_w, kv_w, bt):
    bsz, zw = zq.shape
    mem_w = zw - q_w - 2 * kv_w
    window, n_mem = ck.shape[1], mk.shape[1]
    n_q = q_w // HEAD
    assert kv_w == LANES and mem_w // HEAD <= n_q and n_q == 8
    rows = lambda w: pl.BlockSpec((bt, w), lambda i: (i, 0))
    cache = pl.BlockSpec((bt, window, kv_w), lambda i: (i, 0, 0))
    mem = pl.BlockSpec((bt, n_mem, mem_w), lambda i: (i, 0, 0))
    consts = list(tables) + [sinks.reshape(n_q, 1)]
    const_specs = [_const_spec(c.shape) for c in consts]
    return pl.pallas_call(
        functools.partial(_attn_step_kernel, q_w, kv_w), name="attn_step",
        grid=(bsz // bt,),
        in_specs=[rows(zw), cache, cache, mem, mem] + const_specs,
        out_specs=[rows(q_w), rows(mem_w), cache, cache],
        out_shape=[jax.ShapeDtypeStruct((bsz, q_w), F32), jax.ShapeDtypeStruct((bsz, mem_w), F32),
                   jax.ShapeDtypeStruct(ck.shape, F32), jax.ShapeDtypeStruct(cv.shape, F32)],
        compiler_params=_params(("parallel",)),
    )(zq, ck, cv, mk, mv, *consts)


def _proj_kernel(x_ref, w_ref, o_ref):
    o_ref[...] = jnp.dot(x_ref[...].astype(BF16), w_ref[...], preferred_element_type=F32)


def proj(x, w, tm):
    n, d = x.shape
    return pl.pallas_call(
        _proj_kernel, name="proj",
        grid=(n // tm,),
        in_specs=[pl.BlockSpec((tm, d), lambda i: (i, 0)), _const_spec(w.shape)],
        out_specs=pl.BlockSpec((tm, w.shape[1]), lambda i: (i, 0)),
        out_shape=jax.ShapeDtypeStruct((n, w.shape[1]), F32),
        compiler_params=_params(("parallel",)),
    )(x, w)


def _merge_kernel(alpha, n_tiles, x_ref, oa_ref, ob_ref, om_ref, gt_ref, lig_ref, lib_ref, pa_ref, pb_ref, pm_ref,
                  wo_ref, l1g_ref, l1b_ref, wr_ref, br_ref, *refs):
    outs = refs[-4:]

    @pl.when(pl.program_id(0) >= n_tiles)
    def _():
        for ref in outs:
            ref[...] = jnp.zeros_like(ref)

    @pl.when(pl.program_id(0) < n_tiles)
    def _():
        _merge_tile(alpha, x_ref, oa_ref, ob_ref, om_ref, gt_ref, lig_ref, lib_ref, pa_ref, pb_ref, pm_ref, wo_ref,
                    l1g_ref, l1b_ref, wr_ref, br_ref, *outs)


def _merge_tile(alpha, x_ref, oa_ref, ob_ref, om_ref, gt_ref, lig_ref, lib_ref, pa_ref, pb_ref, pm_ref, wo_ref,
                l1g_ref, l1b_ref, wr_ref, br_ref, x1_ref, x1t_ref, eidx_ref, gate_ref):
    d = x_ref.shape[1]
    xn = _ln(x_ref[...], lig_ref[...], lib_ref[...])
    gts = _sigmoid(gt_ref[...].astype(F32))
    merged = (gts[:, :d] * _bdot(oa_ref[...], pa_ref[...]) + gts[:, d:2 * d] * _bdot(ob_ref[...], pb_ref[...])
              + gts[:, 2 * d:] * _bdot(om_ref[...], pm_ref[...]))
    x1 = _ln(alpha * xn + _bdot(merged, wo_ref[...]), l1g_ref[...], l1b_ref[...])
    x1_ref[...] = x1
    _rows_to_tiles(x1t_ref, x1)
    x_hi, x_lo = _pieces(x1, 2)
    w_hi, w_lo = wr_ref[0], wr_ref[1]
    dot = lambda a, b: jnp.dot(a, b, preferred_element_type=F32)
    logits = dot(x_hi, w_hi) + dot(x_hi, w_lo) + dot(x_lo, w_hi) + br_ref[...]
    lane = lax.broadcasted_iota(jnp.int32, logits.shape, 1)
    lane_f = lane.astype(F32)
    first = lambda hit: jnp.min(jnp.where(hit, lane_f, float(LANES)), axis=-1, keepdims=True).astype(jnp.int32)
    gmask = lane < N_GROUPS
    gl = jnp.where(gmask, logits, NEG_INF)
    gmax = jnp.max(gl, axis=-1, keepdims=True)
    gidx = first(gl == gmax)
    g_w = 1.0 / jnp.sum(jnp.where(gmask, jnp.exp(gl - gmax), 0.0), axis=-1, keepdims=True)
    lo = N_GROUPS + gidx * EXPERTS_PER_GROUP
    el = jnp.where((lane >= lo) & (lane < lo + EXPERTS_PER_GROUP), logits, NEG_INF)
    v1 = jnp.max(el, axis=-1, keepdims=True)
    i1 = first(el == v1)
    el2 = jnp.where(lane == i1, NEG_INF, el)
    v2 = jnp.max(el2, axis=-1, keepdims=True)
    i2 = first(el2 == v2)
    e2 = jnp.exp(v2 - v1)
    gate1 = g_w / (1.0 + e2)
    eidx = jnp.where(lane == 0, (i1 - N_GROUPS).astype(F32), jnp.where(lane == 1, (i2 - N_GROUPS).astype(F32), 0.0))
    eidx_ref[...] = eidx.T[:eidx_ref.shape[0]].astype(jnp.int32)
    gate_ref[...] = jnp.where(lane == 0, gate1, jnp.where(lane == 1, gate1 * e2, 0.0))


def merge(x, oa, ob, om, gt, w, tm, alpha, n_total, row_offset=0, into=None):
    n, d = x.shape
    assert row_offset % tm == 0 and n % tm == 0 and n_total % tm == 0
    off = row_offset // tm
    into = list(into or [])
    n_tiles = n // tm
    steps = n_tiles if into else n_total // tm
    rows = lambda a: pl.BlockSpec((tm, a.shape[1]), lambda i: (jnp.minimum(i, n_tiles - 1), 0))
    out = lambda width: pl.BlockSpec((tm, width), lambda i: (i + off, 0))
    consts = [w['ln_in_g'], w['ln_in_b'], w['p_a'], w['p_b'], w['p_m'], w['w_o'], w['ln1_g'], w['ln1_b'],
              w['w_route'], w['b_route']]
    n_in = 5 + len(consts)
    return pl.pallas_call(
        functools.partial(_merge_kernel, alpha, n_tiles), name="merge",
        grid=(steps,),
        in_specs=[rows(a) for a in (x, oa, ob, om, gt)] + [_const_spec(c.shape) for c in consts]
        + [pl.BlockSpec(memory_space=pl.ANY)] * len(into),
        out_specs=[out(d), pl.BlockSpec((tm * TILE_ROWS, LANES), lambda i: (i + off, 0)),
                   pl.BlockSpec((TILE_ROWS, tm), lambda i: (0, i + off)), out(LANES)],
        out_shape=[jax.ShapeDtypeStruct((n_total, d), F32), jax.ShapeDtypeStruct((n_total * TILE_ROWS, LANES), F32),
                   jax.ShapeDtypeStruct((TILE_ROWS, n_total), jnp.int32), jax.ShapeDtypeStruct((n_total, LANES), F32)],
        input_output_aliases={n_in + k: k for k in range(len(into))},
        compiler_params=_params(("parallel",)),
    )(x, oa, ob, om, gt, *consts, *into)


ROW_DMA_UNROLL = 8
MOE_BUFFERS = 3
DRAIN_STEPS = 2


TILE_ROWS = 8


def _rows_from_tiles(ref, n):
    return jnp.concatenate([ref[pl.ds(s, n, stride=TILE_ROWS), :] for s in range(TILE_ROWS)], axis=1)


def _rows_to_tiles(ref, x):
    for s in range(TILE_ROWS):
        ref[pl.ds(s, x.shape[0], stride=TILE_ROWS), :] = x[:, s * LANES:(s + 1) * LANES]


def _row_copies(asg_ref, base, count, n_asg, x_hbm, buf, y_hbm, sem, gather, unrolled):
    def tile(ref, idx):
        start = idx * TILE_ROWS
        return ref.at[pl.ds(start if isinstance(idx, int) else pl.multiple_of(start, TILE_ROWS), TILE_ROWS)]

    def one(r, priority):
        a = asg_ref[base + r]
        if gather:
            tok = jnp.minimum(a, n_asg - 1)
            tok = jnp.where(tok >= n_asg // 2, tok - n_asg // 2, tok)
            copy = pltpu.make_async_copy(tile(x_hbm, tok), tile(buf, r), sem)
        else:
            copy = pltpu.make_async_copy(tile(buf, r), tile(y_hbm, a), sem)
        copy.start(priority=priority)

    if unrolled:
        for r in range(count):
            one(r, r % 2)
        return

    def body(g, carry):
        for j in range(ROW_DMA_UNROLL):
            one(g * ROW_DMA_UNROLL + j, j % 2)
        return carry
    lax.fori_loop(0, count // ROW_DMA_UNROLL, body, 0)


def _moe_expert_kernel(n_asg, asg_ref, be_ref, nu_ref, x_hbm, wg_ref, wu_ref, wd_ref, y_hbm,
                       xbuf, ybuf, wgb, wub, wdb, gsem, ssem):
    i = pl.program_id(0)
    used = nu_ref[0]
    rows = xbuf.shape[1] // TILE_ROWS
    n_buf = xbuf.shape[0]
    n_blocks = be_ref.shape[0]
    blk = jnp.minimum(i, n_blocks - 1)

    @pl.when((i == 0) | (be_ref[blk] != be_ref[jnp.maximum(blk - 1, 0)]))
    def _():
        wgb[...] = wg_ref[0, 0].astype(BF16)
        wub[...] = wu_ref[0, 0].astype(BF16)
        wdb[...] = wd_ref[0, 0].astype(BF16)

    def wait_gather(slot):
        pltpu.make_async_copy(x_hbm.at[pl.ds(0, rows * TILE_ROWS)], xbuf.at[slot], gsem.at[slot]).wait()

    def wait_scatter(slot):
        pltpu.make_async_copy(ybuf.at[slot], y_hbm.at[pl.ds(0, rows * TILE_ROWS)], ssem.at[slot]).wait()

    def gather(b, slot, unrolled):
        _row_copies(asg_ref, b * rows, rows, n_asg, x_hbm, xbuf.at[slot], y_hbm, gsem.at[slot], True, unrolled)

    def scatter(b, slot, unrolled):
        _row_copies(asg_ref, b * rows, rows, n_asg, x_hbm, ybuf.at[slot], y_hbm, ssem.at[slot], False, unrolled)

    def expert(slot):
        xb = _rows_from_tiles(xbuf.at[slot], rows).astype(BF16)
        hg = jnp.dot(xb, wgb[...], preferred_element_type=F32)
        hu = jnp.dot(xb, wub[...], preferred_element_type=F32)
        h = hg * _sigmoid(hg) * hu
        _rows_to_tiles(ybuf.at[slot], jnp.dot(h.astype(BF16), wdb[...], preferred_element_type=F32))

    @pl.when(i == 0)
    def _():
        ybuf[1] = jnp.zeros(ybuf.shape[1:], F32)
        spare = [pltpu.make_async_copy(ybuf.at[1],
                                       y_hbm.at[pl.ds((n_asg + j * rows) * TILE_ROWS, rows * TILE_ROWS)], ssem.at[1])
                 for j in range((y_hbm.shape[0] // TILE_ROWS - n_asg) // rows)]
        for copy in spare:
            copy.start()
        for copy in spare:
            copy.wait()

    @pl.when((i >= 2) & (i - 2 < used))
    def _():
        wait_scatter((i - 2) % n_buf)

    steady = (i >= 1) & (i + 2 < used)
    for slot in range(n_buf):
        @pl.when(steady & (i % n_buf == slot))
        def _():
            wait_gather(slot)
            gather(i + 2, (slot + 2) % n_buf, True)
            scatter(i - 1, (slot - 1) % n_buf, True)
            expert(slot)

    @pl.when(jnp.logical_not(steady))
    def _():
        slot = i % n_buf

        @pl.when(i == 0)
        def _():
            for b in range(2):
                @pl.when(b < used)
                def _():
                    gather(b, b, False)

        @pl.when(i < used)
        def _():
            wait_gather(slot)

        @pl.when(i + 2 < used)
        def _():
            gather(i + 2, (i + 2) % n_buf, False)

        @pl.when((i >= 1) & (i - 1 < used))
        def _():
            scatter(i - 1, (i - 1) % n_buf, False)

        @pl.when(i < used)
        def _():
            expert(slot)


def moe_experts(x1_tiles, n_asg, asg, blk_e, n_used, e_gate, e_up, e_down):
    d = e_gate.shape[2]
    assert d == TILE_ROWS * LANES and x1_tiles.shape[1] == LANES
    n_blocks = blk_e.shape[0]
    ff = e_gate.shape[-1]
    n_rows = n_asg + e_gate.shape[1] * EXPERT_BLOCK
    weight = lambda shape: pl.BlockSpec(
        (1, 1) + shape, lambda i, asg, be, nu: (0, be[jnp.minimum(i, n_blocks - 1)], 0, 0))
    return pl.pallas_call(
        functools.partial(_moe_expert_kernel, n_asg), name="moe_expert",
        grid_spec=pltpu.PrefetchScalarGridSpec(
            num_scalar_prefetch=3,
            grid=(n_blocks + DRAIN_STEPS,),
            in_specs=[pl.BlockSpec(memory_space=pl.ANY), weight((d, ff)), weight((d, ff)), weight((ff, d))],
            out_specs=pl.BlockSpec(memory_space=pl.ANY),
            scratch_shapes=[pltpu.VMEM((MOE_BUFFERS, EXPERT_BLOCK * TILE_ROWS, LANES), F32),
                            pltpu.VMEM((MOE_BUFFERS, EXPERT_BLOCK * TILE_ROWS, LANES), F32),
                            pltpu.VMEM((d, ff), BF16), pltpu.VMEM((d, ff), BF16), pltpu.VMEM((ff, d), BF16),
                            pltpu.SemaphoreType.DMA((MOE_BUFFERS,)), pltpu.SemaphoreType.DMA((MOE_BUFFERS,))]),
        out_shape=jax.ShapeDtypeStruct((n_rows * TILE_ROWS, LANES), F32),
        compiler_params=_params(("arbitrary",)),
    )(asg, blk_e, n_used, x1_tiles, e_gate, e_up, e_down)


def _moe_combine_kernel(alpha, lead_tiles, y0_ref, y1_ref, x1_ref, gate_ref, g_ref, b_ref, lead_ref, tail_ref):
    i = pl.program_id(0)
    gate = gate_ref[...]
    tm = x1_ref.shape[0]
    moe = gate[:, 0:1] * _rows_from_tiles(y0_ref, tm) + gate[:, 1:2] * _rows_from_tiles(y1_ref, tm)
    out = _ln(alpha * x1_ref[...] + moe, g_ref[...], b_ref[...])

    @pl.when(i < lead_tiles)
    def _():
        lead_ref[...] = out

    @pl.when(i >= lead_tiles)
    def _():
        tail_ref[...] = out


def moe_combine(y, x1, gate, g, b, tm, alpha, n, n_lead):
    d = x1.shape[1]
    assert n_lead % tm == 0 and (n - n_lead) % tm == 0 and n > n_lead
    lead_tiles = n_lead // tm
    return pl.pallas_call(
        functools.partial(_moe_combine_kernel, alpha, lead_tiles), name="moe_combine",
        grid=(n // tm,),
        in_specs=[pl.BlockSpec((tm * TILE_ROWS, LANES), lambda i: (i, 0)),
                  pl.BlockSpec((tm * TILE_ROWS, LANES), lambda i: (i + n // tm, 0)),
                  pl.BlockSpec((tm, d), lambda i: (i, 0)),
                  pl.BlockSpec((tm, LANES), lambda i: (i, 0)), _const_spec((1, d)), _const_spec((1, d))],
        out_specs=[pl.BlockSpec((tm, d), lambda i: (jnp.minimum(i, lead_tiles - 1), 0)),
                   pl.BlockSpec((tm, d), lambda i: (jnp.maximum(i - lead_tiles, 0), 0))],
        out_shape=[jax.ShapeDtypeStruct((n_lead, d), F32), jax.ShapeDtypeStruct((n - n_lead, d), F32)],
        compiler_params=_params(("arbitrary",)),
    )(y, y, x1, gate, g.reshape(1, d), b.reshape(1, d))


def moe_routing(experts):
    top_k, n = experts.shape
    n_exp = N_GROUPS * EXPERTS_PER_GROUP
    a = n * top_k
    flat_e = experts.reshape(a)
    order = jnp.argsort(flat_e, stable=True).astype(jnp.int32)
    counts = jnp.sum((flat_e[:, None] == jnp.arange(n_exp, dtype=jnp.int32)[None, :]).astype(jnp.int32), axis=0)
    ends = jnp.cumsum(counts)
    padded = (counts + EXPERT_BLOCK - 1) // EXPERT_BLOCK * EXPERT_BLOCK
    pad_end = jnp.cumsum(padded)
    n_blocks = -(-a // EXPERT_BLOCK) + n_exp
    blk_start = jnp.arange(n_blocks, dtype=jnp.int32) * EXPERT_BLOCK
    blk_e = jnp.minimum(jnp.sum((pad_end[None, :] <= blk_start[:, None]).astype(jnp.int32), axis=1), n_exp - 1)
    slot = jnp.arange(n_blocks * EXPERT_BLOCK, dtype=jnp.int32)
    slot_e = jnp.repeat(blk_e, EXPERT_BLOCK)
    rank = slot - (pad_end - padded)[slot_e]
    spare = a + jnp.clip(slot - ends[slot_e], 0, n_exp * EXPERT_BLOCK - 1)
    asg = jnp.where(rank < counts[slot_e], order[jnp.clip((ends - counts)[slot_e] + rank, 0, a - 1)], spare)
    n_used = (pad_end[-1:] // EXPERT_BLOCK).astype(jnp.int32)
    return asg.astype(jnp.int32), blk_e.astype(jnp.int32), n_used


def hier_moe_ln(x1, x1_tiles, eidx, gate, w, tm, alpha, n_tokens, n_lead):
    asg, blk_e, n_used = moe_routing(eidx[:2, :n_tokens])
    y = moe_experts(x1_tiles, 2 * n_tokens, asg, blk_e, n_used, w['e_gate'], w['e_up'], w['e_down'])
    return moe_combine(y, x1, gate, w['ln2_g'], w['ln2_b'], tm, alpha, n_tokens, n_lead)


def kernel(x_prompt, x_sample, mem_prompt, state_wkv, state_shift, cache_win_k, cache_win_v, cache_mem_k, cache_mem_v, ln_in_g, ln_in_b, w_in, mu, w0, w_up, a0, a_up, g_up, k_k, k_a, r_k, lnx_g, lnx_b, sinks, w_mem_kv, p_a, p_b, p_m, w_o, ln1_g, ln1_b, w_group, b_group, w_router, b_router, e_gate, e_up, e_down, ln2_g, ln2_b):
    depth = w_in.shape[0]
    assert depth == 1, "single-layer step"
    bsz, seq, d = x_prompt.shape
    dec = x_sample.shape[0]
    assert x_sample.shape[1] == 1
    c_shift = mu.shape[-1]
    c_a = w0.shape[-1]
    window, kv_w = cache_win_k.shape[2], cache_win_k.shape[3] * cache_win_k.shape[4]
    n_mem, mem_w = cache_mem_k.shape[2], cache_mem_k.shape[3] * cache_mem_k.shape[4]
    q_w = sinks.shape[-1] * HEAD
    qkvm_w = q_w + 2 * kv_w + mem_w
    alpha = (2.0 * depth) ** 0.25
    past_len = float(PAST_LEN)
    chunk = 64

    w_in_b = w_in[0].astype(BF16)
    w_parts = [w_in_b[:, :c_shift], w_in_b[:, c_shift:c_shift + qkvm_w], w_in_b[:, c_shift + qkvm_w:]]
    rp = dict(mu=mu[0], w0=w0[0], w_up=w_up[0], a0=a0[0], a_up=a_up[0], g_up=g_up[0], k_k=k_k[0], k_a=k_a[0],
              r_k=r_k[0].reshape(-1))
    n_route = N_GROUPS * (1 + EXPERTS_PER_GROUP)
    mw = dict(ln_in_g=ln_in_g.reshape(1, d), ln_in_b=ln_in_b.reshape(1, d), p_a=p_a[0].astype(BF16),
              p_b=p_b[0].astype(BF16), p_m=p_m[0].astype(BF16), w_o=w_o[0].astype(BF16),
              ln1_g=ln1_g[0].reshape(1, d), ln1_b=ln1_b[0].reshape(1, d),
              w_route=jnp.stack(_pieces(jnp.pad(jnp.concatenate([w_group[0], w_router[0]], axis=1),
                                                ((0, 0), (0, LANES - n_route))), 2)),
              b_route=jnp.pad(jnp.concatenate([b_group[0], b_router[0]]), (0, LANES - n_route)).reshape(1, LANES),
              e_gate=e_gate, e_up=e_up, e_down=e_down, ln2_g=ln2_g[0], ln2_b=ln2_b[0])

    xp = x_prompt.reshape(bsz * seq, d)
    zq, zg, shift_p, prep = ln_proj_prep(x_prompt, ln_in_g, ln_in_b, w_parts, jnp.zeros((bsz, 1, c_shift), F32), rp,
                                         chunk, 256, BF16)
    o_a, wkv_p = wkv(prep, jnp.zeros((bsz, c_a // HEAD, HEAD, HEAD), F32), lnx_g[0], lnx_b[0], chunk)
    mkv = proj(mem_prompt.reshape(bsz * n_mem, d), w_mem_kv[0].astype(BF16), 256).reshape(bsz, n_mem, 2 * mem_w)
    mk_p, mv_p = mkv[..., :mem_w], mkv[..., mem_w:]
    tables = rope_tables(jnp.arange(seq, dtype=F32))
    o_b, o_m, k_rot = attn_prompt(zq, mk_p, mv_p, sinks[0], tables, window, q_w, kv_w)
    n_all = bsz * seq + dec
    n_buf = -(-n_all // 256) * 256
    routed = merge(xp, o_a.reshape(-1, c_a), o_b.reshape(-1, q_w), o_m.reshape(-1, mem_w),
                   zg.reshape(bsz * seq, -1), mw, 256, alpha, n_buf)
    shift_p = shift_p[:, 0]
    kb_p = k_rot[:, -window:].reshape(bsz, window, H_KV, HEAD)
    vb_p = zq[:, -window:, q_w + kv_w:q_w + 2 * kv_w].reshape(bsz, window, H_KV, HEAD)

    xs = x_sample.reshape(dec, d)
    zq_s, zg_s, zr_s, ops_s = ln_proj_prep(xs.reshape(1, dec, d), ln_in_g, ln_in_b, w_parts,
                                           state_shift[0].reshape(1, dec, c_shift), rp, 1, dec, F32)
    zq_s, zg_s, zr_s = zq_s[0], zg_s[0], zr_s[0]
    o_a_s, wkv_s = wkv_step([a.reshape(dec, c_a) for a in ops_s], state_wkv[0], lnx_g[0], lnx_b[0], 8)
    tables_s = rope_tables(jnp.full((1,), past_len, F32))
    o_b_s, o_m_s, nk_s, nv_s = attn_step(
        zq_s, cache_win_k[0].reshape(dec, window, kv_w), cache_win_v[0].reshape(dec, window, kv_w),
        cache_mem_k[0].reshape(dec, n_mem, mem_w), cache_mem_v[0].reshape(dec, n_mem, mem_w),
        sinks[0], tables_s, q_w, kv_w, 8)
    x1, x1_tiles, eidx, gate = merge(xs, o_a_s, o_b_s, o_m_s, zg_s, mw, dec, alpha, n_buf, bsz * seq, routed)

    y_prompt, y_sample = hier_moe_ln(x1, x1_tiles, eidx, gate, mw, dec, alpha, n_all, bsz * seq)
    y_prompt = y_prompt.reshape(bsz, seq, d)
    y_sample = y_sample.reshape(dec, 1, d)

    sd = state_wkv.dtype
    return (y_prompt, y_sample, wkv_p[None].astype(sd), wkv_s[None].astype(sd), shift_p[None], zr_s[None],
            kb_p[None], vb_p[None], nk_s.reshape(dec, window, H_KV, HEAD)[None],
            nv_s.reshape(dec, window, H_KV, HEAD)[None],
            mk_p.reshape(bsz, n_mem, -1, HEAD)[None], mv_p.reshape(bsz, n_mem, -1, HEAD)[None])
```

```python
import functools
import math

import jax
import jax.numpy as jnp
from jax import lax
from jax.experimental import pallas as pl
from jax.experimental.pallas import tpu as pltpu

F32 = jnp.float32
BF16 = jnp.bfloat16

ROW_TILE = 256
WKV_CHUNK = 64
CHUNKS_PER_STEP = 4
SCAN_BATCH = 8
ATTN_BLOCKS = 4
STEP_BATCH = 8
PROJ_PIECE = 512

HEAD = 64
LANES = 128
H_KV = 2
ROT_HALF = 8
ROPE_THETA = 500000.0
PAST_LEN = 8192
N_GROUPS = 4
EXPERTS_PER_GROUP = 8
EXPERT_BLOCK = 128
LN_EPS = 1e-5
LNX_EPS = 64e-5
NEG_INF = -1e30
VMEM_LIMIT = 48 * 1024 * 1024


def _pieces(x, n):
    out = []
    for _ in range(n):
        p = x.astype(BF16)
        out.append(p)
        x = x - p.astype(F32)
    return out


def _mask_dot(x, mask, n=2):
    return sum(jnp.dot(p, mask, preferred_element_type=F32) for p in _pieces(x, n))


def _split3(x, axis, lhs):
    hi = x.astype(BF16).astype(F32)
    lo = x - hi
    return jnp.concatenate([hi, hi, lo] if lhs else [hi, lo, hi], axis=axis).astype(BF16)


def _dot3(a, b):
    return jnp.dot(_split3(a, 1, True), _split3(b, 0, False), preferred_element_type=F32)


def _dot3_t(a, b):
    return lax.dot_general(_split3(a, 1, True), _split3(b, 1, False), (((1,), (1,)), ((), ())),
                           preferred_element_type=F32)


def _bdot(a, b):
    return jnp.dot(a.astype(BF16), b.astype(BF16), preferred_element_type=F32)


def _bdot_t(a, b):
    return lax.dot_general(a.astype(BF16), b.astype(BF16), (((1,), (1,)), ((), ())), preferred_element_type=F32)


def _sigmoid(x):
    return 0.5 * jnp.tanh(0.5 * x) + 0.5


def _ln(x, g, b):
    mu = jnp.mean(x, axis=-1, keepdims=True)
    xc = x - mu
    var = jnp.mean(xc * xc, axis=-1, keepdims=True)
    return xc * lax.rsqrt(var + LN_EPS) * g + b


def _const_spec(shape):
    nd = len(shape)
    return pl.BlockSpec(shape, lambda *_: (0,) * nd)


def _params(sem):
    return pltpu.CompilerParams(dimension_semantics=sem, vmem_limit_bytes=VMEM_LIMIT)


def _ln_proj_prep_kernel(chunk, x_ref, g_ref, b_ref, wr_ref, wq_ref, wg_ref, prev_ref, mu_ref, w0_ref, wup_ref,
                         a0_ref, aup_ref, gup_ref, kk_ref, ka_ref, rk_ref, hsum_ref, tril_ref,
                         zq_ref, zg_ref, zr_ref, *refs):
    out_refs, carry_ref = refs[:-1], refs[-1]
    xn = _ln(x_ref[0], g_ref[...], b_ref[...]).astype(BF16)
    z = jnp.dot(xn, wr_ref[...], preferred_element_type=F32)
    pending = [(o_ref, w_ref, c) for o_ref, w_ref in ((zq_ref, wq_ref), (zg_ref, wg_ref))
               for c in range(0, w_ref.shape[1], PROJ_PIECE)]

    def project(n_pieces):
        for _ in range(min(n_pieces, len(pending))):
            o_ref, w_ref, c = pending.pop(0)
            o_ref[0, :, c:c + PROJ_PIECE] = jnp.dot(xn, w_ref[:, c:c + PROJ_PIECE],
                                                    preferred_element_type=F32).astype(o_ref.dtype)

    tt = z.shape[0]
    c_a = w0_ref.shape[-1]
    r_w, r_a, r_g = wup_ref.shape[0], aup_ref.shape[0], gup_ref.shape[0]
    if chunk == 1:
        zr_ref[0] = z
        prev = prev_ref[0]
    else:
        zr_ref[0] = z[tt - 1:tt, :]

        @pl.when(pl.program_id(1) == 0)
        def _():
            carry_ref[...] = prev_ref[0]

        row = lax.broadcasted_iota(jnp.int32, z.shape, 0)
        prev = jnp.where(row == 0, carry_ref[...], pltpu.roll(z, 1, 0))
        carry_ref[...] = z[tt - 1:tt, :]
    zs = z + (prev - z) * mu_ref[...]
    r = zs[:, :c_a]
    k = zs[:, c_a:2 * c_a]
    v = zs[:, 2 * c_a:3 * c_a]
    o = 3 * c_a
    xw = zs[:, o:o + r_w]
    xa = zs[:, o + r_w:o + r_w + r_a]
    xg = zs[:, o + r_w + r_a:o + r_w + r_a + r_g]
    project(1)
    warg = -(w0_ref[...] + _dot3(jnp.tanh(xw), wup_ref[...]))
    softplus = jnp.maximum(warg, 0.0) + jnp.log1p(jnp.exp(-jnp.abs(warg)))
    lw = -jnp.exp(-softplus - 0.5)
    project(1)
    a = _sigmoid(a0_ref[...] + _dot3(xa, aup_ref[...]))
    g = _dot3(_sigmoid(xg), gup_ref[...])
    project(1)
    kkr = k * kk_ref[...]
    kk = kkr / jnp.maximum(jnp.sqrt(_mask_dot(kkr * kkr, hsum_ref[...])), 1e-12)
    project(1)
    k2 = k * (1.0 + (a - 1.0) * ka_ref[...])
    bonus = _mask_dot(r * k2 * rk_ref[...], hsum_ref[...]) * v
    kb = kk * a
    project(1)
    if chunk == 1:
        outs = (r, jnp.exp(lw), k2, v, -kk, kb, g, bonus)
        for ref, val in zip(out_refs, outs):
            ref[0] = val
        project(len(pending))
        return
    pieces = _pieces(lw, 3)
    ones3 = jnp.ones((chunk, 3 * chunk), BF16)
    cw, cwl = [], []
    for c in range(tt // chunk):
        stack = jnp.concatenate([p[c * chunk:(c + 1) * chunk] for p in pieces], axis=0)
        cw.append(jnp.dot(tril_ref[...], stack, preferred_element_type=F32))
        cwl.append(jnp.dot(ones3, stack, preferred_element_type=F32))
    cw = jnp.concatenate(cw, axis=0)
    cwl = jnp.concatenate(cwl, axis=0)
    project(1)
    e_inv = jnp.exp(-cw)
    outs = (r * jnp.exp(cw), -kk * jnp.exp(cw - lw), kb * e_inv, k2 * e_inv, v, g, bonus)
    for ref, val in zip(out_refs[:-1], outs):
        ref[0] = val
        project(1)
    wl = jnp.exp(cwl)
    for c in range(tt // chunk):
        out_refs[-1][0, c] = wl[c * chunk:c * chunk + 1, :]
    project(len(pending))


def _head_sum_matrix(width):
    idx = jnp.arange(width)
    return ((idx[:, None] // HEAD) == (idx[None, :] // HEAD)).astype(BF16)


def ln_proj_prep(x, g, b, ws, prev, p, chunk, tt, gate_dtype):
    bsz, t, d = x.shape
    w_r, w_q, w_g = ws
    cs = w_r.shape[1]
    c_a = p['w0'].shape[-1]
    assert prev.shape[1] == (t if chunk == 1 else 1)
    ridx = jnp.arange(chunk)
    tril = jnp.tile((ridx[None, :] <= ridx[:, None]).astype(BF16), (1, 3))
    hsum = _head_sum_matrix(c_a)
    row = lambda a: a.reshape(1, -1)
    rows = lambda width: pl.BlockSpec((1, tt, width), lambda bi, i: (bi, i, 0))
    full = jax.ShapeDtypeStruct((bsz, t, c_a), F32)
    if chunk == 1:
        out_specs, out_shape = [rows(c_a)] * 8, [full] * 8
        prev_spec, zr_spec, zr_rows = rows(cs), rows(cs), t
    else:
        out_specs = [rows(c_a)] * 7 + [pl.BlockSpec((1, tt // chunk, 1, c_a), lambda bi, i: (bi, i, 0, 0))]
        out_shape = [full] * 7 + [jax.ShapeDtypeStruct((bsz, t // chunk, 1, c_a), F32)]
        prev_spec = zr_spec = pl.BlockSpec((1, 1, cs), lambda bi, i: (bi, 0, 0))
        zr_rows = 1
    consts = [row(p['mu']), row(p['w0']), p['w_up'], row(p['a0']), p['a_up'], p['g_up'], row(p['k_k']),
              row(p['k_a']), row(p['r_k']), hsum, tril]
    weight = lambda w: pl.BlockSpec(w.shape, lambda bi, i: (0, 0), pipeline_mode=pl.Buffered(1))
    outs = pl.pallas_call(
        functools.partial(_ln_proj_prep_kernel, chunk), name="ln_proj_prep",
        grid=(bsz, t // tt),
        in_specs=[rows(d), _const_spec((1, d)), _const_spec((1, d)), weight(w_r), weight(w_q), weight(w_g), prev_spec]
        + [_const_spec(c.shape) for c in consts],
        out_specs=[rows(w_q.shape[1]), rows(w_g.shape[1]), zr_spec] + out_specs,
        out_shape=[jax.ShapeDtypeStruct((bsz, t, w_q.shape[1]), F32),
                   jax.ShapeDtypeStruct((bsz, t, w_g.shape[1]), gate_dtype),
                   jax.ShapeDtypeStruct((bsz, zr_rows, cs), F32)] + out_shape,
        scratch_shapes=[pltpu.VMEM((1, cs), F32)],
        compiler_params=_params(("parallel", "arbitrary")),
    )(x, row(g), row(b), w_r, w_q, w_g, prev, *consts)
    return outs[0], outs[1], outs[2], outs[3:]


def _wkv_chunk_kernel(ra_ref, at_ref, bt_ref, kt_ref, v_ref, wl_ref, rp_ref, y0_ref, m_ref, n_ref):
    n_chunks = m_ref.shape[1]
    length = ra_ref.shape[1] // n_chunks
    heads = ra_ref.shape[2] // HEAD
    row = lax.broadcasted_iota(jnp.int32, (length, length), 0)
    col = lax.broadcasted_iota(jnp.int32, (length, length), 1)
    strict = row > col
    incl = row >= col
    hrow = lax.broadcasted_iota(jnp.int32, (HEAD, HEAD), 0)
    hcol = lax.broadcasted_iota(jnp.int32, (HEAD, HEAD), 1)
    units = [(c, h) for c in range(n_chunks) for h in range(heads)]
    us = range(len(units))
    at = lambda ref, u: ref[0, units[u][0] * length:(units[u][0] + 1) * length,
                            units[u][1] * HEAD:(units[u][1] + 1) * HEAD]
    gram = [_bdot_t(jnp.concatenate([at(at_ref, u), at(ra_ref, u)], axis=0),
                    jnp.concatenate([at(bt_ref, u), at(kt_ref, u)], axis=0)) for u in us]
    a_ab = [jnp.where(strict, g[:length, :length], 0.0) for g in gram]
    a_kk = [jnp.concatenate([jnp.where(strict, g[:length, length:], 0.0),
                             jnp.where(incl, g[length:, length:], 0.0)], axis=0) for g in gram]
    a_rb = [jnp.where(incl, g[length:, :length], 0.0) for g in gram]
    inv = [jnp.where(row == col, 1.0, a) for a in a_ab]
    pw = [_bdot(a, a) for a in a_ab]
    avy = [_bdot(a_kk[u], at(v_ref, u)) for u in us]
    w_l = [wl_ref[0, units[u][0], :, units[u][1] * HEAD:(units[u][1] + 1) * HEAD] for u in us]
    nk = [_bdot(at(v_ref, u).T, at(kt_ref, u) * w_l[u]) for u in us]
    for _ in range(int(math.log2(length)) - 2):
        both = [_bdot(jnp.concatenate([pw[u], inv[u]], axis=0), pw[u]) for u in us]
        pw = [b[:length] for b in both]
        inv = [inv[u] + both[u][length:] for u in us]
    inv = [inv[u] + _bdot(inv[u], pw[u]) for u in us]
    pq = [_bdot(inv[u], jnp.concatenate([at(at_ref, u), avy[u][:length]], axis=1)) for u in us]
    ry = [_bdot(a_rb[u], pq[u]) for u in us]
    mn = [_bdot(pq[u].T, at(bt_ref, u) * w_l[u]) for u in us]
    lanes = lambda f, c: jnp.concatenate([f(c * heads + h) for h in range(heads)], axis=1)
    rows = lambda f: jnp.concatenate([lanes(f, c) for c in range(n_chunks)], axis=0)
    rp_ref[0] = rows(lambda u: at(ra_ref, u) + ry[u][:, :HEAD])
    y0_ref[0] = rows(lambda u: avy[u][length:] + ry[u][:, HEAD:])
    for c in range(n_chunks):
        m_ref[0, c] = lanes(lambda u: jnp.where(hrow == hcol, w_l[u], 0.0) + mn[u][:HEAD], c)
        n_ref[0, c] = lanes(lambda u: nk[u] + mn[u][HEAD:], c)


def _head_norm_wide(y, hsum):
    yc = y - _mask_dot(y, hsum) * (1.0 / HEAD)
    return yc * lax.rsqrt(_mask_dot(yc * yc, hsum) * (1.0 / HEAD) + LNX_EPS)


def _wkv_scan_kernel(rp_ref, y0_ref, m_ref, n_ref, g_ref, bonus_ref, s0_ref, lg_ref, lb_ref, hsum_ref,
                     o_ref, sout_ref, s_ref):
    c = pl.program_id(1)
    nb = rp_ref.shape[0]
    heads = rp_ref.shape[2] // HEAD

    @pl.when(c == 0)
    def _():
        s_ref[...] = s0_ref[...]

    pairs = [(b, h, slice(h * HEAD, (h + 1) * HEAD)) for b in range(nb) for h in range(heads)]
    s = [s_ref[b, h] for b, h, _ in pairs]
    s_new = [n_ref[b, 0, :, sl] + _dot3(s[j], m_ref[b, 0, :, sl]) for j, (b, h, sl) in enumerate(pairs)]
    ys = [y0_ref[b, :, sl] + _bdot_t(rp_ref[b, :, sl], s[j]) for j, (b, h, sl) in enumerate(pairs)]
    for j, (b, h, _) in enumerate(pairs):
        s_ref[b, h] = s_new[j]
    length = rp_ref.shape[1]
    y = _head_norm_wide(jnp.concatenate([jnp.concatenate(ys[b * heads:(b + 1) * heads], axis=-1) for b in range(nb)],
                                        axis=0), hsum_ref[...])
    for b in range(nb):
        o = (y[b * length:(b + 1) * length] * lg_ref[...] + lb_ref[...] + bonus_ref[b]) * g_ref[b]
        o_ref[b] = o.astype(o_ref.dtype)

    @pl.when(c == pl.num_programs(1) - 1)
    def _():
        sout_ref[...] = s_ref[...]


def wkv(prep, s0, lnx_g, lnx_b, chunk):
    ra, at, bt, kt, v, g, bonus, wl = prep
    bsz, t, c_a = ra.shape
    heads = c_a // HEAD
    n_chunks = t // chunk
    assert chunk & (chunk - 1) == 0 and chunk >= 4 and t % chunk == 0
    per_step = CHUNKS_PER_STEP if n_chunks % CHUNKS_PER_STEP == 0 else 1
    tile = pl.BlockSpec((1, per_step * chunk, c_a), lambda b, c: (b, c, 0))
    mat = pl.BlockSpec((1, per_step, HEAD, c_a), lambda b, c: (b, c, 0, 0))
    full = jax.ShapeDtypeStruct((bsz, t, c_a), F32)
    mats = jax.ShapeDtypeStruct((bsz, n_chunks, HEAD, c_a), F32)
    rp, y0, m, n = pl.pallas_call(
        _wkv_chunk_kernel, name="wkv_chunk",
        grid=(bsz, n_chunks // per_step),
        in_specs=[tile] * 5 + [pl.BlockSpec((1, per_step, 1, c_a), lambda b, c: (b, c, 0, 0))],
        out_specs=[tile, tile, mat, mat],
        out_shape=[full, full, mats, mats],
        compiler_params=_params(("parallel", "parallel")),
    )(ra, at, bt, kt, v, wl)
    nb = SCAN_BATCH if bsz % SCAN_BATCH == 0 else 1
    tile = pl.BlockSpec((nb, chunk, c_a), lambda b, c: (b, c, 0))
    mat = pl.BlockSpec((nb, 1, HEAD, c_a), lambda b, c: (b, c, 0, 0))
    st = pl.BlockSpec((nb, heads, HEAD, HEAD), lambda b, c: (b, 0, 0, 0))
    return pl.pallas_call(
        _wkv_scan_kernel, name="wkv_scan",
        grid=(bsz // nb, n_chunks),
        in_specs=[tile, tile, mat, mat, tile, tile, st, _const_spec((1, c_a)), _const_spec((1, c_a)),
                  _const_spec((c_a, c_a))],
        out_specs=[tile, st],
        out_shape=[jax.ShapeDtypeStruct((bsz, t, c_a), BF16), jax.ShapeDtypeStruct(s0.shape, F32)],
        scratch_shapes=[pltpu.VMEM((nb, heads, HEAD, HEAD), F32)],
        compiler_params=_params(("parallel", "arbitrary")),
    )(rp, y0, m, n, g, bonus, s0, lnx_g.reshape(1, c_a), lnx_b.reshape(1, c_a), _head_sum_matrix(c_a))


def _wkv_step_kernel(r_ref, w_ref, k_ref, v_ref, a_ref, b_ref, g_ref, bonus_ref, s_ref, lg_ref, lb_ref, hsum_ref,
                     o_ref, sout_ref):
    bt, heads = s_ref.shape[0], s_ref.shape[1]
    c_a = heads * HEAD
    hsum = hsum_ref[...]
    diag = (lax.broadcasted_iota(jnp.int32, (HEAD, c_a), 1) % HEAD
            == lax.broadcasted_iota(jnp.int32, (HEAD, c_a), 0))
    seqs = range(bt)
    row = lambda ref, i: ref[i:i + 1, :]
    stack = lambda f: jnp.concatenate([f(i) for i in seqs], axis=0)
    piece = lambda x, i: x[i * HEAD:(i + 1) * HEAD]
    s = stack(lambda i: jnp.concatenate([s_ref[i, h] for h in range(heads)], axis=1))
    sa = _mask_dot(s * stack(lambda i: jnp.broadcast_to(row(a_ref, i), (HEAD, c_a))), hsum)
    v_rows = _mask_dot(stack(lambda i: jnp.where(diag, row(v_ref, i), 0.0)), hsum)
    s = stack(lambda i: piece(s, i) * row(w_ref, i) + piece(sa, i) * row(b_ref, i) + piece(v_rows, i) * row(k_ref, i))
    for i in seqs:
        for h in range(heads):
            sout_ref[i, h] = piece(s, i)[:, h * HEAD:(h + 1) * HEAD]
    y_rows = _mask_dot(stack(lambda i: piece(s, i) * row(r_ref, i)), hsum)
    y = stack(lambda i: jnp.sum(jnp.where(diag, piece(y_rows, i), 0.0), axis=0, keepdims=True))
    y = _head_norm_wide(y, hsum)
    o_ref[...] = (y * lg_ref[...] + lb_ref[...] + bonus_ref[...]) * g_ref[...]


def wkv_step(ops, s0, lnx_g, lnx_b, bt):
    bsz, c_a = ops[0].shape
    rows = pl.BlockSpec((bt, c_a), lambda i: (i, 0))
    st = pl.BlockSpec((bt,) + s0.shape[1:], lambda i: (i, 0, 0, 0))
    hsum = _head_sum_matrix(c_a)
    return pl.pallas_call(
        _wkv_step_kernel, name="wkv_step",
        grid=(bsz // bt,),
        in_specs=[rows] * 8 + [st, _const_spec((1, c_a)), _const_spec((1, c_a)), _const_spec(hsum.shape)],
        out_specs=[rows, st],
        out_shape=[jax.ShapeDtypeStruct((bsz, c_a), F32), jax.ShapeDtypeStruct(s0.shape, F32)],
        compiler_params=_params(("parallel",)),
    )(*ops, s0, lnx_g.reshape(1, c_a), lnx_b.reshape(1, c_a), hsum)


def rope_tables(pos):
    inv_freq = ROPE_THETA ** (-jnp.arange(ROT_HALF, dtype=F32) / ROT_HALF)
    ang = pos[:, None] * inv_freq[None, :]
    cos, sin = jnp.cos(ang), jnp.sin(ang)
    t = pos.shape[0]
    rest = HEAD - 2 * ROT_HALF
    c = jnp.concatenate([cos, cos, jnp.ones((t, rest), F32)], axis=1)
    s1 = jnp.concatenate([jnp.zeros((t, ROT_HALF), F32), sin, jnp.zeros((t, rest), F32)], axis=1)
    s2 = jnp.concatenate([-sin, jnp.zeros((t, HEAD - ROT_HALF), F32)], axis=1)
    rep = LANES // HEAD
    return tuple(jnp.tile(x, (1, rep)) for x in (c, s1, s2))


def _rope(x, c, s1, s2):
    width = x.shape[-1]
    rep = width // c.shape[-1]
    if rep > 1:
        c, s1, s2 = (jnp.concatenate([t] * rep, axis=1) for t in (c, s1, s2))
    return x * c + pltpu.roll(x, ROT_HALF, 1) * s1 + pltpu.roll(x, width - ROT_HALF, 1) * s2


def _attn_kernel(window, q_w, kv_w, zq_ref, kp_ref, vp_ref, rc_ref, rs1_ref, rs2_ref, pc_ref, ps1_ref, ps2_ref,
                 mk_ref, mv_ref, sink_ref, ob_ref, om_ref, kr_ref):
    zq = zq_ref[0]
    nblk = zq.shape[0] // window
    first = pl.program_id(1) * nblk
    scale = HEAD ** -0.5
    q = zq[:, :q_w]
    k = zq[:, q_w:q_w + kv_w]
    v = zq[:, q_w + kv_w:q_w + 2 * kv_w]
    qm = zq[:, q_w + 2 * kv_w:]
    qr = _rope(q, rc_ref[...], rs1_ref[...], rs2_ref[...]) * scale
    kr = _rope(k, rc_ref[...], rs1_ref[...], rs2_ref[...])
    kr_ref[0] = kr
    k_all = jnp.concatenate([_rope(kp_ref[0], pc_ref[...], ps1_ref[...], ps2_ref[...]), kr], axis=0)
    v_all = jnp.concatenate([vp_ref[0], v], axis=0)
    gqa = q_w // kv_w
    assert window & (window - 1) == 0
    qi = lax.broadcasted_iota(jnp.int32, (gqa * window, 2 * window), 0) & (window - 1)
    kj = lax.broadcasted_iota(jnp.int32, (gqa * window, 2 * window), 1)
    band = (kj > qi) & (kj <= qi + window)
    tdot = lambda a, b: lax.dot_general(a, b, (((1,), (1,)), ((), ())), preferred_element_type=F32)
    dot = lambda a, b: jnp.dot(a, b, preferred_element_type=F32)
    hsl = lambda h: slice(h * HEAD, (h + 1) * HEAD)
    rows = lambda i: slice(i * window, (i + 1) * window)
    keys = lambda i: slice(i * window, (i + 2) * window)
    wins = [(i, hk) for i in range(nblk) for hk in range(kv_w // HEAD)]
    mems = [(i, h) for i in range(nblk) for h in range(qm.shape[1] // HEAD)]
    qmb = (qm * scale).astype(BF16)
    mk = mk_ref[0].astype(BF16)
    mv = mv_ref[0].astype(BF16)
    kcat = [k_all[keys(i), hsl(hk)].astype(BF16) for i, hk in wins]
    vcat = [v_all[keys(i), hsl(hk)].astype(BF16) for i, hk in wins]
    qs = [jnp.concatenate([qr[rows(i), hsl(hk * gqa + g)] for g in range(gqa)], axis=0).astype(BF16)
          for i, hk in wins]
    s = [jnp.where(band & ((kj >= window) | (first + i > 0)), tdot(qs[u], kcat[u]), NEG_INF)
         for u, (i, hk) in enumerate(wins)]
    sm = [tdot(qmb[rows(i), hsl(h)], mk[:, hsl(h)]) for i, h in mems]
    sink = [jnp.concatenate([jnp.full((window, 1), sink_ref[hk * gqa + g], F32) for g in range(gqa)], axis=0)
            for i, hk in wins]
    m = [jnp.maximum(jnp.max(s[u], axis=-1, keepdims=True), sink[u]) for u in range(len(wins))]
    p = [jnp.exp(s[u] - m[u]) for u in range(len(wins))]
    pm = [jnp.exp(x - jnp.max(x, axis=-1, keepdims=True)) for x in sm]
    o = [dot(p[u].astype(BF16), vcat[u])
         / (jnp.sum(p[u], axis=-1, keepdims=True) + jnp.exp(sink[u] - m[u])) for u in range(len(wins))]
    om = [dot(pm[u].astype(BF16), mv[:, hsl(h)]) / jnp.sum(pm[u], axis=-1, keepdims=True)
          for u, (i, h) in enumerate(mems)]
    n_kv, n_mh = kv_w // HEAD, qm.shape[1] // HEAD
    ob = jnp.concatenate([jnp.concatenate([o[i * n_kv + hk][g * window:(g + 1) * window]
                                           for hk in range(n_kv) for g in range(gqa)], axis=1)
                          for i in range(nblk)], axis=0)
    ob_ref[0] = ob.astype(ob_ref.dtype)
    om_ref[0] = jnp.concatenate([jnp.concatenate(om[i * n_mh:(i + 1) * n_mh], axis=1) for i in range(nblk)],
                                axis=0).astype(om_ref.dtype)


def attn_prompt(zq, mk, mv, sinks, tables, window, q_w, kv_w):
    bsz, t, zw = zq.shape
    mem_w = zw - q_w - 2 * kv_w
    assert kv_w == LANES and q_w % kv_w == 0
    kcol, vcol = q_w // kv_w, q_w // kv_w + 1
    nblk = ATTN_BLOCKS if (t // window) % ATTN_BLOCKS == 0 else 1
    rows = nblk * window
    prev = lambda n: jnp.maximum(n * nblk - 1, 0)
    tab = pl.BlockSpec((rows, LANES), lambda b, n: (n, 0))
    ptab = pl.BlockSpec((window, LANES), lambda b, n: (prev(n), 0))
    mem = pl.BlockSpec((1,) + mk.shape[1:], lambda b, n: (b, 0, 0))
    return pl.pallas_call(
        functools.partial(_attn_kernel, window, q_w, kv_w), name="attn",
        grid=(bsz, t // rows),
        in_specs=[pl.BlockSpec((1, rows, zw), lambda b, n: (b, n, 0)),
                  pl.BlockSpec((1, window, kv_w), lambda b, n: (b, prev(n), kcol)),
                  pl.BlockSpec((1, window, kv_w), lambda b, n: (b, prev(n), vcol)),
                  tab, tab, tab, ptab, ptab, ptab, mem, mem,
                  pl.BlockSpec(memory_space=pltpu.SMEM)],
        out_specs=[pl.BlockSpec((1, rows, q_w), lambda b, n: (b, n, 0)),
                   pl.BlockSpec((1, rows, mem_w), lambda b, n: (b, n, 0)),
                   pl.BlockSpec((1, rows, kv_w), lambda b, n: (b, n, 0))],
        out_shape=[jax.ShapeDtypeStruct((bsz, t, q_w), BF16), jax.ShapeDtypeStruct((bsz, t, mem_w), BF16),
                   jax.ShapeDtypeStruct((bsz, t, kv_w), F32)],
        compiler_params=_params(("parallel", "parallel")),
    )(zq, zq, zq, *tables, *tables, mk, mv, sinks)


def _attn_step_kernel(q_w, kv_w, zq_ref, ck_ref, cv_ref, mk_ref, mv_ref, rc_ref, rs1_ref, rs2_ref, sink_ref,
                      ob_ref, om_ref, nk_ref, nv_ref):
    bt = zq_ref.shape[0]
    window = ck_ref.shape[1]
    mem_w = om_ref.shape[1]
    n_q, gqa, per_vreg = q_w // HEAD, q_w // kv_w, LANES // HEAD
    scale = HEAD ** -0.5
    zq = zq_ref[...]
    q = _rope(zq[:, :q_w], rc_ref[...], rs1_ref[...], rs2_ref[...]) * scale
    k_new = _rope(zq[:, q_w:q_w + kv_w], rc_ref[...], rs1_ref[...], rs2_ref[...])
    v_new = zq[:, q_w + kv_w:q_w + 2 * kv_w]
    qm = zq[:, q_w + 2 * kv_w:] * scale
    own = lambda w: (lax.broadcasted_iota(jnp.int32, (n_q, w), 1) // HEAD
                     == lax.broadcasted_iota(jnp.int32, (n_q, w), 0))
    own_q, own_m = own(q_w), own(mem_w)
    hrow = lax.broadcasted_iota(jnp.int32, (n_q, LANES), 0)
    hblk = lax.broadcasted_iota(jnp.int32, (n_q, LANES), 1) // HEAD
    swap = (hrow % per_vreg) != (hrow // gqa)
    keep = hblk == hrow % per_vreg
    key_ok = lax.broadcasted_iota(jnp.int32, (n_q, window), 1) >= 1
    wrow = lax.broadcasted_iota(jnp.int32, (window, kv_w), 0)
    sink = sink_ref[...]
    bs = range(bt)
    tdot = lambda a, b: lax.dot_general(a.astype(BF16), b.astype(BF16), (((1,), (1,)), ((), ())),
                                        preferred_element_type=F32)
    for b in bs:
        nk_ref[b] = jnp.where(wrow == window - 1, k_new[b:b + 1], pltpu.roll(ck_ref[b], window - 1, 0))
        nv_ref[b] = jnp.where(wrow == window - 1, v_new[b:b + 1], pltpu.roll(cv_ref[b], window - 1, 0))
    q8 = []
    for b in bs:
        rep = jnp.where(own_q, q[b:b + 1], 0.0)
        fold = sum(rep[:, c * LANES:(c + 1) * LANES] for c in range(q_w // LANES))
        q8.append(jnp.where(swap, pltpu.roll(fold, HEAD, 1), fold))
    qm8 = [jnp.where(own_m, qm[b:b + 1], 0.0) for b in bs]
    s = [jnp.where(key_ok, tdot(q8[b], ck_ref[b]), NEG_INF) for b in bs]
    sm = [tdot(qm8[b], mk_ref[b]) for b in bs]
    s_new = [jnp.sum(q8[b] * k_new[b:b + 1], axis=1, keepdims=True) for b in bs]
    m = [jnp.maximum(jnp.maximum(jnp.max(s[b], axis=1, keepdims=True), s_new[b]), sink) for b in bs]
    p = [jnp.exp(s[b] - m[b]) for b in bs]
    p_new = [jnp.exp(s_new[b] - m[b]) for b in bs]
    pm = [jnp.exp(sm[b] - jnp.max(sm[b], axis=1, keepdims=True)) for b in bs]
    o8 = [(_bdot(p[b], cv_ref[b]) + p_new[b] * v_new[b:b + 1])
          / (jnp.sum(p[b], axis=1, keepdims=True) + p_new[b] + jnp.exp(sink - m[b])) for b in bs]
    om8 = [_bdot(pm[b], mv_ref[b]) / jnp.sum(pm[b], axis=1, keepdims=True) for b in bs]
    ob, om = [], []
    for b in bs:
        o = jnp.where(swap, pltpu.roll(o8[b], HEAD, 1), o8[b])
        o = jnp.concatenate([jnp.where(keep, o, 0.0)] * (q_w // LANES), axis=1)
        ob.append(jnp.sum(jnp.where(own_q, o, 0.0), axis=0, keepdims=True))
        om.append(jnp.sum(jnp.where(own_m, om8[b], 0.0), axis=0, keepdims=True))
    ob_ref[...] = jnp.concatenate(ob, axis=0)
    om_ref[...] = jnp.concatenate(om, axis=0)


def attn_step(zq, ck, cv, mk, mv, sinks, tables, q_w, kv_w, bt):
    bsz, zw = zq.shape
    mem_w = zw - q_w - 2 * kv_w
    window, n_mem = ck.shape[1], mk.shape[1]
    n_q = q_w // HEAD
    assert kv_w == LANES and mem_w // HEAD <= n_q and n_q == 8
    rows = lambda w: pl.BlockSpec((bt, w), lambda i: (i, 0))
    cache = pl.BlockSpec((bt, window, kv_w), lambda i: (i, 0, 0))
    mem = pl.BlockSpec((bt, n_mem, mem_w), lambda i: (i, 0, 0))
    consts = list(tables) + [sinks.reshape(n_q, 1)]
    const_specs = [_const_spec(c.shape) for c in consts]
    return pl.pallas_call(
        functools.partial(_attn_step_kernel, q_w, kv_w), name="attn_step",
        grid=(bsz // bt,),
        in_specs=[rows(zw), cache, cache, mem, mem] + const_specs,
        out_specs=[rows(q_w), rows(mem_w), cache, cache],
        out_shape=[jax.ShapeDtypeStruct((bsz, q_w), F32), jax.ShapeDtypeStruct((bsz, mem_w), F32),
                   jax.ShapeDtypeStruct(ck.shape, F32), jax.ShapeDtypeStruct(cv.shape, F32)],
        compiler_params=_params(("parallel",)),
    )(zq, ck, cv, mk, mv, *consts)


def _proj_kernel(x_ref, w_ref, o_ref):
    o_ref[...] = jnp.dot(x_ref[...].astype(BF16), w_ref[...], preferred_element_type=F32)


def proj(x, w, tm):
    n, d = x.shape
    return pl.pallas_call(
        _proj_kernel, name="proj",
        grid=(n // tm,),
        in_specs=[pl.BlockSpec((tm, d), lambda i: (i, 0)), _const_spec(w.shape)],
        out_specs=pl.BlockSpec((tm, w.shape[1]), lambda i: (i, 0)),
        out_shape=jax.ShapeDtypeStruct((n, w.shape[1]), F32),
        compiler_params=_params(("parallel",)),
    )(x, w)


def _merge_kernel(alpha, n_tiles, x_ref, oa_ref, ob_ref, om_ref, gt_ref, lig_ref, lib_ref, pa_ref, pb_ref, pm_ref,
                  wo_ref, l1g_ref, l1b_ref, wr_ref, br_ref, *refs):
    outs = refs[-4:]

    @pl.when(pl.program_id(0) >= n_tiles)
    def _():
        for ref in outs:
            ref[...] = jnp.zeros_like(ref)

    @pl.when(pl.program_id(0) < n_tiles)
    def _():
        _merge_tile(alpha, x_ref, oa_ref, ob_ref, om_ref, gt_ref, lig_ref, lib_ref, pa_ref, pb_ref, pm_ref, wo_ref,
                    l1g_ref, l1b_ref, wr_ref, br_ref, *outs)


def _merge_tile(alpha, x_ref, oa_ref, ob_ref, om_ref, gt_ref, lig_ref, lib_ref, pa_ref, pb_ref, pm_ref, wo_ref,
                l1g_ref, l1b_ref, wr_ref, br_ref, x1_ref, x1t_ref, eidx_ref, gate_ref):
    d = x_ref.shape[1]
    xn = _ln(x_ref[...], lig_ref[...], lib_ref[...])
    gts = _sigmoid(gt_ref[...].astype(F32))
    merged = (gts[:, :d] * _bdot(oa_ref[...], pa_ref[...]) + gts[:, d:2 * d] * _bdot(ob_ref[...], pb_ref[...])
              + gts[:, 2 * d:] * _bdot(om_ref[...], pm_ref[...]))
    x1 = _ln(alpha * xn + _bdot(merged, wo_ref[...]), l1g_ref[...], l1b_ref[...])
    x1_ref[...] = x1
    _rows_to_tiles(x1t_ref, x1)
    x_hi, x_lo = _pieces(x1, 2)
    w_hi, w_lo = wr_ref[0], wr_ref[1]
    dot = lambda a, b: jnp.dot(a, b, preferred_element_type=F32)
    logits = dot(x_hi, w_hi) + dot(x_hi, w_lo) + dot(x_lo, w_hi) + br_ref[...]
    lane = lax.broadcasted_iota(jnp.int32, logits.shape, 1)
    lane_f = lane.astype(F32)
    first = lambda hit: jnp.min(jnp.where(hit, lane_f, float(LANES)), axis=-1, keepdims=True).astype(jnp.int32)
    gmask = lane < N_GROUPS
    gl = jnp.where(gmask, logits, NEG_INF)
    gmax = jnp.max(gl, axis=-1, keepdims=True)
    gidx = first(gl == gmax)
    g_w = 1.0 / jnp.sum(jnp.where(gmask, jnp.exp(gl - gmax), 0.0), axis=-1, keepdims=True)
    lo = N_GROUPS + gidx * EXPERTS_PER_GROUP
    el = jnp.where((lane >= lo) & (lane < lo + EXPERTS_PER_GROUP), logits, NEG_INF)
    v1 = jnp.max(el, axis=-1, keepdims=True)
    i1 = first(el == v1)
    el2 = jnp.where(lane == i1, NEG_INF, el)
    v2 = jnp.max(el2, axis=-1, keepdims=True)
    i2 = first(el2 == v2)
    e2 = jnp.exp(v2 - v1)
    gate1 = g_w / (1.0 + e2)
    eidx = jnp.where(lane == 0, (i1 - N_GROUPS).astype(F32), jnp.where(lane == 1, (i2 - N_GROUPS).astype(F32), 0.0))
    eidx_ref[...] = eidx.T[:eidx_ref.shape[0]].astype(jnp.int32)
    gate_ref[...] = jnp.where(lane == 0, gate1, jnp.where(lane == 1, gate1 * e2, 0.0))


def merge(x, oa, ob, om, gt, w, tm, alpha, n_total, row_offset=0, into=None):
    n, d = x.shape
    assert row_offset % tm == 0 and n % tm == 0 and n_total % tm == 0
    off = row_offset // tm
    into = list(into or [])
    n_tiles = n // tm
    steps = n_tiles if into else n_total // tm
    rows = lambda a: pl.BlockSpec((tm, a.shape[1]), lambda i: (jnp.minimum(i, n_tiles - 1), 0))
    out = lambda width: pl.BlockSpec((tm, width), lambda i: (i + off, 0))
    consts = [w['ln_in_g'], w['ln_in_b'], w['p_a'], w['p_b'], w['p_m'], w['w_o'], w['ln1_g'], w['ln1_b'],
              w['w_route'], w['b_route']]
    n_in = 5 + len(consts)
    return pl.pallas_call(
        functools.partial(_merge_kernel, alpha, n_tiles), name="merge",
        grid=(steps,),
        in_specs=[rows(a) for a in (x, oa, ob, om, gt)] + [_const_spec(c.shape) for c in consts]
        + [pl.BlockSpec(memory_space=pl.ANY)] * len(into),
        out_specs=[out(d), pl.BlockSpec((tm * TILE_ROWS, LANES), lambda i: (i + off, 0)),
                   pl.BlockSpec((TILE_ROWS, tm), lambda i: (0, i + off)), out(LANES)],
        out_shape=[jax.ShapeDtypeStruct((n_total, d), F32), jax.ShapeDtypeStruct((n_total * TILE_ROWS, LANES), F32),
                   jax.ShapeDtypeStruct((TILE_ROWS, n_total), jnp.int32), jax.ShapeDtypeStruct((n_total, LANES), F32)],
        input_output_aliases={n_in + k: k for k in range(len(into))},
        compiler_params=_params(("parallel",)),
    )(x, oa, ob, om, gt, *consts, *into)


ROW_DMA_UNROLL = 8
MOE_BUFFERS = 3
DRAIN_STEPS = 2


TILE_ROWS = 8


def _rows_from_tiles(ref, n):
    return jnp.concatenate([ref[pl.ds(s, n, stride=TILE_ROWS), :] for s in range(TILE_ROWS)], axis=1)


def _rows_to_tiles(ref, x):
    for s in range(TILE_ROWS):
        ref[pl.ds(s, x.shape[0], stride=TILE_ROWS), :] = x[:, s * LANES:(s + 1) * LANES]


def _row_copies(asg_ref, base, count, n_asg, x_hbm, buf, y_hbm, sem, gather, unrolled):
    def tile(ref, idx):
        start = idx * TILE_ROWS
        return ref.at[pl.ds(start if isinstance(idx, int) else pl.multiple_of(start, TILE_ROWS), TILE_ROWS)]

    def one(r, priority):
        a = asg_ref[base + r]
        if gather:
            tok = jnp.minimum(a, n_asg - 1)
            tok = jnp.where(tok >= n_asg // 2, tok - n_asg // 2, tok)
            copy = pltpu.make_async_copy(tile(x_hbm, tok), tile(buf, r), sem)
        else:
            copy = pltpu.make_async_copy(tile(buf, r), tile(y_hbm, a), sem)
        copy.start(priority=priority)

    if unrolled:
        for r in range(count):
            one(r, r % 2)
        return

    def body(g, carry):
        for j in range(ROW_DMA_UNROLL):
            one(g * ROW_DMA_UNROLL + j, j % 2)
        return carry
    lax.fori_loop(0, count // ROW_DMA_UNROLL, body, 0)


def _moe_expert_kernel(n_tokens, n_asg, asg_ref, be_ref, nu_ref, x_hbm, wg_ref, wu_ref, wd_ref, y_hbm,
                       xbuf, ybuf, wgb, wub, wdb, gsem, ssem):
    i = pl.program_id(0)
    used = nu_ref[0]
    rows = xbuf.shape[1] // TILE_ROWS
    n_buf = xbuf.shape[0]
    n_blocks = be_ref.shape[0]
    blk = jnp.minimum(i, n_blocks - 1)

    @pl.when((i == 0) | (be_ref[blk] != be_ref[jnp.maximum(blk - 1, 0)]))
    def _():
        wgb[...] = wg_ref[0, 0].astype(BF16)
        wub[...] = wu_ref[0, 0].astype(BF16)
        wdb[...] = wd_ref[0, 0].astype(BF16)

    def wait_gather(slot):
        pltpu.make_async_copy(x_hbm.at[pl.ds(0, rows * TILE_ROWS)], xbuf.at[slot], gsem.at[slot]).wait()

    def wait_scatter(slot):
        pltpu.make_async_copy(ybuf.at[slot], y_hbm.at[pl.ds(0, rows * TILE_ROWS)], ssem.at[slot]).wait()

    def gather(b, slot, unrolled):
        _row_copies(asg_ref, b * rows, rows, n_asg, x_hbm, xbuf.at[slot], y_hbm, gsem.at[slot], True, unrolled)

    def scatter(b, slot, unrolled):
        _row_copies(asg_ref, b * rows, rows, n_asg, x_hbm, ybuf.at[slot], y_hbm, ssem.at[slot], False, unrolled)

    def expert(slot):
        xb = _rows_from_tiles(xbuf.at[slot], rows).astype(BF16)
        hg = jnp.dot(xb, wgb[...], preferred_element_type=F32)
        hu = jnp.dot(xb, wub[...], preferred_element_type=F32)
        h = hg * _sigmoid(hg) * hu
        _rows_to_tiles(ybuf.at[slot], jnp.dot(h.astype(BF16), wdb[...], preferred_element_type=F32))

    @pl.when(i == 0)
    def _():
        ybuf[1] = jnp.zeros(ybuf.shape[1:], F32)
        stride = n_asg // 2
        gaps = [(k * stride + n_tokens, stride - n_tokens) for k in range(2)]
        gaps.append((n_asg, y_hbm.shape[0] // TILE_ROWS - n_asg))
        fills = [pltpu.make_async_copy(ybuf.at[1, pl.ds(0, min(rows, count - j) * TILE_ROWS)],
                                       y_hbm.at[pl.ds((start + j) * TILE_ROWS, min(rows, count - j) * TILE_ROWS)],
                                       ssem.at[1])
                 for start, count in gaps for j in range(0, count, rows)]
        for copy in fills:
            copy.start()
        for copy in fills:
            copy.wait()

    @pl.when((i >= 2) & (i - 2 < used))
    def _():
        wait_scatter((i - 2) % n_buf)

    steady = (i >= 1) & (i + 2 < used)
    for slot in range(n_buf):
        @pl.when(steady & (i % n_buf == slot))
        def _():
            wait_gather(slot)
            gather(i + 2, (slot + 2) % n_buf, True)
            scatter(i - 1, (slot - 1) % n_buf, True)
            expert(slot)

    @pl.when(jnp.logical_not(steady))
    def _():
        slot = i % n_buf

        @pl.when(i == 0)
        def _():
            for b in range(2):
                @pl.when(b < used)
                def _():
                    gather(b, b, False)

        @pl.when(i < used)
        def _():
            wait_gather(slot)

        @pl.when(i + 2 < used)
        def _():
            gather(i + 2, (i + 2) % n_buf, False)

        @pl.when((i >= 1) & (i - 1 < used))
        def _():
            scatter(i - 1, (i - 1) % n_buf, False)

        @pl.when(i < used)
        def _():
            expert(slot)


def moe_experts(x1_tiles, n_tokens, stride, asg, blk_e, n_used, e_gate, e_up, e_down):
    d = e_gate.shape[2]
    assert d == TILE_ROWS * LANES and x1_tiles.shape == (stride * TILE_ROWS, LANES)
    n_blocks = blk_e.shape[0]
    ff = e_gate.shape[-1]
    n_asg = 2 * stride
    n_rows = n_asg + e_gate.shape[1] * EXPERT_BLOCK
    weight = lambda shape: pl.BlockSpec(
        (1, 1) + shape, lambda i, asg, be, nu: (0, be[jnp.minimum(i, n_blocks - 1)], 0, 0))
    return pl.pallas_call(
        functools.partial(_moe_expert_kernel, n_tokens, n_asg), name="moe_expert",
        grid_spec=pltpu.PrefetchScalarGridSpec(
            num_scalar_prefetch=3,
            grid=(n_blocks + DRAIN_STEPS,),
            in_specs=[pl.BlockSpec(memory_space=pl.ANY), weight((d, ff)), weight((d, ff)), weight((ff, d))],
            out_specs=pl.BlockSpec(memory_space=pl.ANY),
            scratch_shapes=[pltpu.VMEM((MOE_BUFFERS, EXPERT_BLOCK * TILE_ROWS, LANES), F32),
                            pltpu.VMEM((MOE_BUFFERS, EXPERT_BLOCK * TILE_ROWS, LANES), F32),
                            pltpu.VMEM((d, ff), BF16), pltpu.VMEM((d, ff), BF16), pltpu.VMEM((ff, d), BF16),
                            pltpu.SemaphoreType.DMA((MOE_BUFFERS,)), pltpu.SemaphoreType.DMA((MOE_BUFFERS,))]),
        out_shape=jax.ShapeDtypeStruct((n_rows * TILE_ROWS, LANES), F32),
        compiler_params=_params(("arbitrary",)),
    )(asg, blk_e, n_used, x1_tiles, e_gate, e_up, e_down)


def _moe_combine_kernel(alpha, lead_tiles, y0_ref, y1_ref, x1_ref, gate_ref, g_ref, b_ref, lead_ref, tail_ref):
    i = pl.program_id(0)
    gate = gate_ref[...]
    tm = x1_ref.shape[0]
    moe = gate[:, 0:1] * _rows_from_tiles(y0_ref, tm) + gate[:, 1:2] * _rows_from_tiles(y1_ref, tm)
    out = _ln(alpha * x1_ref[...] + moe, g_ref[...], b_ref[...])

    @pl.when(i < lead_tiles)
    def _():
        lead_ref[...] = out

    @pl.when(i >= lead_tiles)
    def _():
        tail_ref[...] = out[:tail_ref.shape[0]]


def moe_combine(y, x1, gate, g, b, tm, alpha, n, n_lead):
    stride, d = x1.shape
    assert n_lead % tm == 0 and stride - n_lead == tm and n_lead < n <= stride
    lead_tiles = n_lead // tm
    return pl.pallas_call(
        functools.partial(_moe_combine_kernel, alpha, lead_tiles), name="moe_combine",
        grid=(stride // tm,),
        in_specs=[pl.BlockSpec((tm * TILE_ROWS, LANES), lambda i: (i, 0)),
                  pl.BlockSpec((tm * TILE_ROWS, LANES), lambda i: (i + stride // tm, 0)),
                  pl.BlockSpec((tm, d), lambda i: (i, 0)),
                  pl.BlockSpec((tm, LANES), lambda i: (i, 0)), _const_spec((1, d)), _const_spec((1, d))],
        out_specs=[pl.BlockSpec((tm, d), lambda i: (jnp.minimum(i, lead_tiles - 1), 0)),
                   pl.BlockSpec((n - n_lead, d), lambda i: (0, 0))],
        out_shape=[jax.ShapeDtypeStruct((n_lead, d), F32), jax.ShapeDtypeStruct((n - n_lead, d), F32)],
        compiler_params=_params(("arbitrary",)),
    )(y, y, x1, gate, g.reshape(1, d), b.reshape(1, d))


def moe_routing(experts, stride):
    top_k, n = experts.shape
    n_exp = N_GROUPS * EXPERTS_PER_GROUP
    a = n * top_k
    flat_e = experts.reshape(a)
    order = jnp.argsort(flat_e, stable=True).astype(jnp.int32)
    order = order + (order // n) * (stride - n)
    counts = jnp.sum((flat_e[:, None] == jnp.arange(n_exp, dtype=jnp.int32)[None, :]).astype(jnp.int32), axis=0)
    ends = jnp.cumsum(counts)
    padded = (counts + EXPERT_BLOCK - 1) // EXPERT_BLOCK * EXPERT_BLOCK
    pad_end = jnp.cumsum(padded)
    n_blocks = -(-a // EXPERT_BLOCK) + n_exp
    blk_start = jnp.arange(n_blocks, dtype=jnp.int32) * EXPERT_BLOCK
    blk_e = jnp.minimum(jnp.sum((pad_end[None, :] <= blk_start[:, None]).astype(jnp.int32), axis=1), n_exp - 1)
    slot = jnp.arange(n_blocks * EXPERT_BLOCK, dtype=jnp.int32)
    slot_e = jnp.repeat(blk_e, EXPERT_BLOCK)
    rank = slot - (pad_end - padded)[slot_e]
    spare = top_k * stride + jnp.clip(slot - ends[slot_e], 0, n_exp * EXPERT_BLOCK - 1)
    asg = jnp.where(rank < counts[slot_e], order[jnp.clip((ends - counts)[slot_e] + rank, 0, a - 1)], spare)
    n_used = (pad_end[-1:] // EXPERT_BLOCK).astype(jnp.int32)
    return asg.astype(jnp.int32), blk_e.astype(jnp.int32), n_used


def hier_moe_ln(x1, x1_tiles, eidx, gate, w, tm, alpha, n_tokens, n_lead):
    stride = x1.shape[0]
    asg, blk_e, n_used = moe_routing(eidx[:2, :n_tokens], stride)
    y = moe_experts(x1_tiles, n_tokens, stride, asg, blk_e, n_used, w['e_gate'], w['e_up'], w['e_down'])
    return moe_combine(y, x1, gate, w['ln2_g'], w['ln2_b'], tm, alpha, n_tokens, n_lead)


def kernel(x_prompt, x_sample, mem_prompt, state_wkv, state_shift, cache_win_k, cache_win_v, cache_mem_k, cache_mem_v, ln_in_g, ln_in_b, w_in, mu, w0, w_up, a0, a_up, g_up, k_k, k_a, r_k, lnx_g, lnx_b, sinks, w_mem_kv, p_a, p_b, p_m, w_o, ln1_g, ln1_b, w_group, b_group, w_router, b_router, e_gate, e_up, e_down, ln2_g, ln2_b):
    depth = w_in.shape[0]
    assert depth == 1, "single-layer step"
    bsz, seq, d = x_prompt.shape
    dec = x_sample.shape[0]
    assert x_sample.shape[1] == 1
    c_shift = mu.shape[-1]
    c_a = w0.shape[-1]
    window, kv_w = cache_win_k.shape[2], cache_win_k.shape[3] * cache_win_k.shape[4]
    n_mem, mem_w = cache_mem_k.shape[2], cache_mem_k.shape[3] * cache_mem_k.shape[4]
    q_w = sinks.shape[-1] * HEAD
    qkvm_w = q_w + 2 * kv_w + mem_w
    alpha = (2.0 * depth) ** 0.25
    past_len = float(PAST_LEN)
    chunk = WKV_CHUNK

    w_in_b = w_in[0].astype(BF16)
    w_parts = [w_in_b[:, :c_shift], w_in_b[:, c_shift:c_shift + qkvm_w], w_in_b[:, c_shift + qkvm_w:]]
    rp = dict(mu=mu[0], w0=w0[0], w_up=w_up[0], a0=a0[0], a_up=a_up[0], g_up=g_up[0], k_k=k_k[0], k_a=k_a[0],
              r_k=r_k[0].reshape(-1))
    n_route = N_GROUPS * (1 + EXPERTS_PER_GROUP)
    mw = dict(ln_in_g=ln_in_g.reshape(1, d), ln_in_b=ln_in_b.reshape(1, d), p_a=p_a[0].astype(BF16),
              p_b=p_b[0].astype(BF16), p_m=p_m[0].astype(BF16), w_o=w_o[0].astype(BF16),
              ln1_g=ln1_g[0].reshape(1, d), ln1_b=ln1_b[0].reshape(1, d),
              w_route=jnp.stack(_pieces(jnp.pad(jnp.concatenate([w_group[0], w_router[0]], axis=1),
                                                ((0, 0), (0, LANES - n_route))), 2)),
              b_route=jnp.pad(jnp.concatenate([b_group[0], b_router[0]]), (0, LANES - n_route)).reshape(1, LANES),
              e_gate=e_gate, e_up=e_up, e_down=e_down, ln2_g=ln2_g[0], ln2_b=ln2_b[0])

    xp = x_prompt.reshape(bsz * seq, d)
    zq, zg, shift_p, prep = ln_proj_prep(x_prompt, ln_in_g, ln_in_b, w_parts, jnp.zeros((bsz, 1, c_shift), F32), rp,
                                         chunk, ROW_TILE, BF16)
    o_a, wkv_p = wkv(prep, jnp.zeros((bsz, c_a // HEAD, HEAD, HEAD), F32), lnx_g[0], lnx_b[0], chunk)
    mkv = proj(mem_prompt.reshape(bsz * n_mem, d), w_mem_kv[0].astype(BF16), ROW_TILE).reshape(bsz, n_mem, 2 * mem_w)
    mk_p, mv_p = mkv[..., :mem_w], mkv[..., mem_w:]
    tables = rope_tables(jnp.arange(seq, dtype=F32))
    o_b, o_m, k_rot = attn_prompt(zq, mk_p, mv_p, sinks[0], tables, window, q_w, kv_w)
    n_all = bsz * seq + dec
    n_buf = -(-n_all // ROW_TILE) * ROW_TILE
    routed = merge(xp, o_a.reshape(-1, c_a), o_b.reshape(-1, q_w), o_m.reshape(-1, mem_w),
                   zg.reshape(bsz * seq, -1), mw, ROW_TILE, alpha, n_buf)
    shift_p = shift_p[:, 0]
    kb_p = k_rot[:, -window:].reshape(bsz, window, H_KV, HEAD)
    vb_p = zq[:, -window:, q_w + kv_w:q_w + 2 * kv_w].reshape(bsz, window, H_KV, HEAD)

    xs = x_sample.reshape(dec, d)
    zq_s, zg_s, zr_s, ops_s = ln_proj_prep(xs.reshape(1, dec, d), ln_in_g, ln_in_b, w_parts,
                                           state_shift[0].reshape(1, dec, c_shift), rp, 1, dec, F32)
    zq_s, zg_s, zr_s = zq_s[0], zg_s[0], zr_s[0]
    o_a_s, wkv_s = wkv_step([a.reshape(dec, c_a) for a in ops_s], state_wkv[0], lnx_g[0], lnx_b[0], STEP_BATCH)
    tables_s = rope_tables(jnp.full((1,), past_len, F32))
    o_b_s, o_m_s, nk_s, nv_s = attn_step(
        zq_s, cache_win_k[0].reshape(dec, window, kv_w), cache_win_v[0].reshape(dec, window, kv_w),
        cache_mem_k[0].reshape(dec, n_mem, mem_w), cache_mem_v[0].reshape(dec, n_mem, mem_w),
        sinks[0], tables_s, q_w, kv_w, STEP_BATCH)
    x1, x1_tiles, eidx, gate = merge(xs, o_a_s, o_b_s, o_m_s, zg_s, mw, dec, alpha, n_buf, bsz * seq, routed)

    y_prompt, y_sample = hier_moe_ln(x1, x1_tiles, eidx, gate, mw, ROW_TILE, alpha, n_all, bsz * seq)
    y_prompt = y_prompt.reshape(bsz, seq, d)
    y_sample = y_sample.reshape(dec, 1, d)

    sd = state_wkv.dtype
    return (y_prompt, y_sample, wkv_p[None].astype(sd), wkv_s[None].astype(sd), shift_p[None], zr_s[None],
            kb_p[None], vb_p[None], nk_s.reshape(dec, window, H_KV, HEAD)[None],
            nv_s.reshape(dec, window, H_KV, HEAD)[None],
            mk_p.reshape(bsz, n_mem, -1, HEAD)[None], mv_p.reshape(bsz, n_mem, -1, HEAD)[None])
```

```python
import functools
import math

import jax
import jax.numpy as jnp
from jax import lax
from jax.experimental import pallas as pl
from jax.experimental.pallas import tpu as pltpu

F32 = jnp.float32
BF16 = jnp.bfloat16

ROW_TILE = 256
WKV_CHUNK = 64
CHUNKS_PER_STEP = 8
SCAN_BATCH = 8
ATTN_BLOCKS = 4
STEP_BATCH = 8
PROJ_PIECE = 512

HEAD = 64
LANES = 128
H_KV = 2
ROT_HALF = 8
ROPE_THETA = 500000.0
PAST_LEN = 8192
N_GROUPS = 4
EXPERTS_PER_GROUP = 8
EXPERT_BLOCK = 128
LN_EPS = 1e-5
LNX_EPS = 64e-5
NEG_INF = -1e30
VMEM_LIMIT = 48 * 1024 * 1024


def _pieces(x, n):
    out = []
    for _ in range(n):
        p = x.astype(BF16)
        out.append(p)
        x = x - p.astype(F32)
    return out


def _mask_dot(x, mask, n=2):
    return sum(jnp.dot(p, mask, preferred_element_type=F32) for p in _pieces(x, n))


def _split3(x, axis, lhs):
    hi = x.astype(BF16).astype(F32)
    lo = x - hi
    return jnp.concatenate([hi, hi, lo] if lhs else [hi, lo, hi], axis=axis).astype(BF16)


def _dot3(a, b):
    return jnp.dot(_split3(a, 1, True), _split3(b, 0, False), preferred_element_type=F32)


def _dot3_t(a, b):
    return lax.dot_general(_split3(a, 1, True), _split3(b, 1, False), (((1,), (1,)), ((), ())),
                           preferred_element_type=F32)


def _bdot(a, b):
    return jnp.dot(a.astype(BF16), b.astype(BF16), preferred_element_type=F32)


def _bdot_t(a, b):
    return lax.dot_general(a.astype(BF16), b.astype(BF16), (((1,), (1,)), ((), ())), preferred_element_type=F32)


def _sigmoid(x):
    return 0.5 * jnp.tanh(0.5 * x) + 0.5


def _ln(x, g, b):
    mu = jnp.mean(x, axis=-1, keepdims=True)
    xc = x - mu
    var = jnp.mean(xc * xc, axis=-1, keepdims=True)
    return xc * lax.rsqrt(var + LN_EPS) * g + b


def _const_spec(shape):
    nd = len(shape)
    return pl.BlockSpec(shape, lambda *_: (0,) * nd)


def _params(sem):
    return pltpu.CompilerParams(dimension_semantics=sem, vmem_limit_bytes=VMEM_LIMIT)


def _ln_proj_prep_kernel(chunk, x_ref, g_ref, b_ref, wr_ref, wq_ref, wg_ref, prev_ref, mu_ref, w0_ref, wup_ref,
                         a0_ref, aup_ref, gup_ref, kk_ref, ka_ref, rk_ref, hsum_ref, tril_ref,
                         zq_ref, zg_ref, zr_ref, *refs):
    out_refs, carry_ref = refs[:-1], refs[-1]
    xn = _ln(x_ref[0], g_ref[...], b_ref[...]).astype(BF16)
    z = jnp.dot(xn, wr_ref[...], preferred_element_type=F32)
    pending = [(o_ref, w_ref, c) for o_ref, w_ref in ((zq_ref, wq_ref), (zg_ref, wg_ref))
               for c in range(0, w_ref.shape[1], PROJ_PIECE)]

    def project(n_pieces):
        for _ in range(min(n_pieces, len(pending))):
            o_ref, w_ref, c = pending.pop(0)
            o_ref[0, :, c:c + PROJ_PIECE] = jnp.dot(xn, w_ref[:, c:c + PROJ_PIECE],
                                                    preferred_element_type=F32).astype(o_ref.dtype)

    tt = z.shape[0]
    c_a = w0_ref.shape[-1]
    r_w, r_a, r_g = wup_ref.shape[0], aup_ref.shape[0], gup_ref.shape[0]
    if chunk == 1:
        zr_ref[0] = z
        prev = prev_ref[0]
    else:
        zr_ref[0] = z[tt - 1:tt, :]

        @pl.when(pl.program_id(1) == 0)
        def _():
            carry_ref[...] = prev_ref[0]

        row = lax.broadcasted_iota(jnp.int32, z.shape, 0)
        prev = jnp.where(row == 0, carry_ref[...], pltpu.roll(z, 1, 0))
        carry_ref[...] = z[tt - 1:tt, :]
    zs = z + (prev - z) * mu_ref[...]
    r = zs[:, :c_a]
    k = zs[:, c_a:2 * c_a]
    v = zs[:, 2 * c_a:3 * c_a]
    o = 3 * c_a
    xw = zs[:, o:o + r_w]
    xa = zs[:, o + r_w:o + r_w + r_a]
    xg = zs[:, o + r_w + r_a:o + r_w + r_a + r_g]
    project(1)
    warg = -(w0_ref[...] + _dot3(jnp.tanh(xw), wup_ref[...]))
    softplus = jnp.maximum(warg, 0.0) + jnp.log1p(jnp.exp(-jnp.abs(warg)))
    lw = -jnp.exp(-softplus - 0.5)
    project(1)
    a = _sigmoid(a0_ref[...] + _dot3(xa, aup_ref[...]))
    g = _dot3(_sigmoid(xg), gup_ref[...])
    project(1)
    kkr = k * kk_ref[...]
    kk = kkr / jnp.maximum(jnp.sqrt(_mask_dot(kkr * kkr, hsum_ref[...])), 1e-12)
    project(1)
    k2 = k * (1.0 + (a - 1.0) * ka_ref[...])
    bonus = _mask_dot(r * k2 * rk_ref[...], hsum_ref[...]) * v
    kb = kk * a
    project(1)
    if chunk == 1:
        outs = (r, jnp.exp(lw), k2, v, -kk, kb, g, bonus)
        for ref, val in zip(out_refs, outs):
            ref[0] = val
        project(len(pending))
        return
    pieces = _pieces(lw, 3)
    ones3 = jnp.ones((chunk, 3 * chunk), BF16)
    cw, cwl = [], []
    for c in range(tt // chunk):
        stack = jnp.concatenate([p[c * chunk:(c + 1) * chunk] for p in pieces], axis=0)
        cw.append(jnp.dot(tril_ref[...], stack, preferred_element_type=F32))
        cwl.append(jnp.dot(ones3, stack, preferred_element_type=F32))
    cw = jnp.concatenate(cw, axis=0)
    cwl = jnp.concatenate(cwl, axis=0)
    project(1)
    e_inv = jnp.exp(-cw)
    outs = (r * jnp.exp(cw), -kk * jnp.exp(cw - lw), kb * e_inv, k2 * e_inv, v, g, bonus)
    for ref, val in zip(out_refs[:-1], outs):
        ref[0] = val
        project(1)
    wl = jnp.exp(cwl)
    for c in range(tt // chunk):
        out_refs[-1][0, c] = wl[c * chunk:c * chunk + 1, :]
    project(len(pending))


def _head_sum_matrix(width):
    idx = jnp.arange(width)
    return ((idx[:, None] // HEAD) == (idx[None, :] // HEAD)).astype(BF16)


def ln_proj_prep(x, g, b, ws, prev, p, chunk, tt, gate_dtype):
    bsz, t, d = x.shape
    w_r, w_q, w_g = ws
    cs = w_r.shape[1]
    c_a = p['w0'].shape[-1]
    assert prev.shape[1] == (t if chunk == 1 else 1)
    ridx = jnp.arange(chunk)
    tril = jnp.tile((ridx[None, :] <= ridx[:, None]).astype(BF16), (1, 3))
    hsum = _head_sum_matrix(c_a)
    row = lambda a: a.reshape(1, -1)
    rows = lambda width: pl.BlockSpec((1, tt, width), lambda bi, i: (bi, i, 0))
    full = jax.ShapeDtypeStruct((bsz, t, c_a), F32)
    if chunk == 1:
        out_specs, out_shape = [rows(c_a)] * 8, [full] * 8
        prev_spec, zr_spec, zr_rows = rows(cs), rows(cs), t
    else:
        out_specs = [rows(c_a)] * 7 + [pl.BlockSpec((1, tt // chunk, 1, c_a), lambda bi, i: (bi, i, 0, 0))]
        out_shape = [full] * 7 + [jax.ShapeDtypeStruct((bsz, t // chunk, 1, c_a), F32)]
        prev_spec = zr_spec = pl.BlockSpec((1, 1, cs), lambda bi, i: (bi, 0, 0))
        zr_rows = 1
    consts = [row(p['mu']), row(p['w0']), p['w_up'], row(p['a0']), p['a_up'], p['g_up'], row(p['k_k']),
              row(p['k_a']), row(p['r_k']), hsum, tril]
    weight = lambda w: pl.BlockSpec(w.shape, lambda bi, i: (0, 0), pipeline_mode=pl.Buffered(1))
    outs = pl.pallas_call(
        functools.partial(_ln_proj_prep_kernel, chunk), name="ln_proj_prep",
        grid=(bsz, t // tt),
        in_specs=[rows(d), _const_spec((1, d)), _const_spec((1, d)), weight(w_r), weight(w_q), weight(w_g), prev_spec]
        + [_const_spec(c.shape) for c in consts],
        out_specs=[rows(w_q.shape[1]), rows(w_g.shape[1]), zr_spec] + out_specs,
        out_shape=[jax.ShapeDtypeStruct((bsz, t, w_q.shape[1]), F32),
                   jax.ShapeDtypeStruct((bsz, t, w_g.shape[1]), gate_dtype),
                   jax.ShapeDtypeStruct((bsz, zr_rows, cs), F32)] + out_shape,
        scratch_shapes=[pltpu.VMEM((1, cs), F32)],
        compiler_params=_params(("parallel", "arbitrary")),
    )(x, row(g), row(b), w_r, w_q, w_g, prev, *consts)
    return outs[0], outs[1], outs[2], outs[3:]


def _wkv_chunk_kernel(ra_ref, at_ref, bt_ref, kt_ref, v_ref, wl_ref, rp_ref, y0_ref, m_ref, n_ref):
    n_chunks = m_ref.shape[1]
    length = ra_ref.shape[1] // n_chunks
    heads = ra_ref.shape[2] // HEAD
    row = lax.broadcasted_iota(jnp.int32, (length, length), 0)
    col = lax.broadcasted_iota(jnp.int32, (length, length), 1)
    strict = row > col
    incl = row >= col
    hrow = lax.broadcasted_iota(jnp.int32, (HEAD, HEAD), 0)
    hcol = lax.broadcasted_iota(jnp.int32, (HEAD, HEAD), 1)
    units = [(c, h) for c in range(n_chunks) for h in range(heads)]
    us = range(len(units))
    at = lambda ref, u: ref[0, units[u][0] * length:(units[u][0] + 1) * length,
                            units[u][1] * HEAD:(units[u][1] + 1) * HEAD]
    gram = [_bdot_t(jnp.concatenate([at(at_ref, u), at(ra_ref, u)], axis=0),
                    jnp.concatenate([at(bt_ref, u), at(kt_ref, u)], axis=0)) for u in us]
    a_ab = [jnp.where(strict, g[:length, :length], 0.0) for g in gram]
    a_kk = [jnp.concatenate([jnp.where(strict, g[:length, length:], 0.0),
                             jnp.where(incl, g[length:, length:], 0.0)], axis=0) for g in gram]
    a_rb = [jnp.where(incl, g[length:, :length], 0.0) for g in gram]
    inv = [jnp.where(row == col, 1.0, a) for a in a_ab]
    pw = [_bdot(a, a) for a in a_ab]
    avy = [_bdot(a_kk[u], at(v_ref, u)) for u in us]
    w_l = [wl_ref[0, units[u][0], :, units[u][1] * HEAD:(units[u][1] + 1) * HEAD] for u in us]
    nk = [_bdot(at(v_ref, u).T, at(kt_ref, u) * w_l[u]) for u in us]
    for _ in range(int(math.log2(length)) - 2):
        both = [_bdot(jnp.concatenate([pw[u], inv[u]], axis=0), pw[u]) for u in us]
        pw = [b[:length] for b in both]
        inv = [inv[u] + both[u][length:] for u in us]
    inv = [inv[u] + _bdot(inv[u], pw[u]) for u in us]
    pq = [_bdot(inv[u], jnp.concatenate([at(at_ref, u), avy[u][:length]], axis=1)) for u in us]
    ry = [_bdot(a_rb[u], pq[u]) for u in us]
    mn = [_bdot(pq[u].T, at(bt_ref, u) * w_l[u]) for u in us]
    lanes = lambda f, c: jnp.concatenate([f(c * heads + h) for h in range(heads)], axis=1)
    rows = lambda f: jnp.concatenate([lanes(f, c) for c in range(n_chunks)], axis=0)
    rp_ref[0] = rows(lambda u: at(ra_ref, u) + ry[u][:, :HEAD])
    y0_ref[0] = rows(lambda u: avy[u][length:] + ry[u][:, HEAD:])
    for c in range(n_chunks):
        m_ref[0, c] = lanes(lambda u: jnp.where(hrow == hcol, w_l[u], 0.0) + mn[u][:HEAD], c)
        n_ref[0, c] = lanes(lambda u: nk[u] + mn[u][HEAD:], c)


def _head_norm_wide(y, hsum):
    yc = y - _mask_dot(y, hsum) * (1.0 / HEAD)
    return yc * lax.rsqrt(_mask_dot(yc * yc, hsum) * (1.0 / HEAD) + LNX_EPS)


def _wkv_scan_kernel(rp_ref, y0_ref, m_ref, n_ref, g_ref, bonus_ref, s0_ref, lg_ref, lb_ref, hsum_ref,
                     o_ref, sout_ref, s_ref):
    c = pl.program_id(1)
    nb = rp_ref.shape[0]
    heads = rp_ref.shape[2] // HEAD

    @pl.when(c == 0)
    def _():
        s_ref[...] = s0_ref[...]

    pairs = [(b, h, slice(h * HEAD, (h + 1) * HEAD)) for b in range(nb) for h in range(heads)]
    s = [s_ref[b, h] for b, h, _ in pairs]
    s_new = [n_ref[b, 0, :, sl] + _dot3(s[j], m_ref[b, 0, :, sl]) for j, (b, h, sl) in enumerate(pairs)]
    ys = [y0_ref[b, :, sl] + _bdot_t(rp_ref[b, :, sl], s[j]) for j, (b, h, sl) in enumerate(pairs)]
    for j, (b, h, _) in enumerate(pairs):
        s_ref[b, h] = s_new[j]
    length = rp_ref.shape[1]
    y = _head_norm_wide(jnp.concatenate([jnp.concatenate(ys[b * heads:(b + 1) * heads], axis=-1) for b in range(nb)],
                                        axis=0), hsum_ref[...])
    for b in range(nb):
        o = (y[b * length:(b + 1) * length] * lg_ref[...] + lb_ref[...] + bonus_ref[b]) * g_ref[b]
        o_ref[b] = o.astype(o_ref.dtype)

    @pl.when(c == pl.num_programs(1) - 1)
    def _():
        sout_ref[...] = s_ref[...]


def wkv(prep, s0, lnx_g, lnx_b, chunk):
    ra, at, bt, kt, v, g, bonus, wl = prep
    bsz, t, c_a = ra.shape
    heads = c_a // HEAD
    n_chunks = t // chunk
    assert chunk & (chunk - 1) == 0 and chunk >= 4 and t % chunk == 0
    per_step = CHUNKS_PER_STEP if n_chunks % CHUNKS_PER_STEP == 0 else 1
    tile = pl.BlockSpec((1, per_step * chunk, c_a), lambda b, c: (b, c, 0))
    mat = pl.BlockSpec((1, per_step, HEAD, c_a), lambda b, c: (b, c, 0, 0))
    full = jax.ShapeDtypeStruct((bsz, t, c_a), F32)
    mats = jax.ShapeDtypeStruct((bsz, n_chunks, HEAD, c_a), F32)
    rp, y0, m, n = pl.pallas_call(
        _wkv_chunk_kernel, name="wkv_chunk",
        grid=(bsz, n_chunks // per_step),
        in_specs=[tile] * 5 + [pl.BlockSpec((1, per_step, 1, c_a), lambda b, c: (b, c, 0, 0))],
        out_specs=[tile, tile, mat, mat],
        out_shape=[full, full, mats, mats],
        compiler_params=_params(("parallel", "parallel")),
    )(ra, at, bt, kt, v, wl)
    nb = SCAN_BATCH if bsz % SCAN_BATCH == 0 else 1
    tile = pl.BlockSpec((nb, chunk, c_a), lambda b, c: (b, c, 0))
    mat = pl.BlockSpec((nb, 1, HEAD, c_a), lambda b, c: (b, c, 0, 0))
    st = pl.BlockSpec((nb, heads, HEAD, HEAD), lambda b, c: (b, 0, 0, 0))
    return pl.pallas_call(
        _wkv_scan_kernel, name="wkv_scan",
        grid=(bsz // nb, n_chunks),
        in_specs=[tile, tile, mat, mat, tile, tile, st, _const_spec((1, c_a)), _const_spec((1, c_a)),
                  _const_spec((c_a, c_a))],
        out_specs=[tile, st],
        out_shape=[jax.ShapeDtypeStruct((bsz, t, c_a), BF16), jax.ShapeDtypeStruct(s0.shape, F32)],
        scratch_shapes=[pltpu.VMEM((nb, heads, HEAD, HEAD), F32)],
        compiler_params=_params(("parallel", "arbitrary")),
    )(rp, y0, m, n, g, bonus, s0, lnx_g.reshape(1, c_a), lnx_b.reshape(1, c_a), _head_sum_matrix(c_a))


def _wkv_step_kernel(r_ref, w_ref, k_ref, v_ref, a_ref, b_ref, g_ref, bonus_ref, s_ref, lg_ref, lb_ref, hsum_ref,
                     o_ref, sout_ref):
    bt, heads = s_ref.shape[0], s_ref.shape[1]
    c_a = heads * HEAD
    hsum = hsum_ref[...]
    diag = (lax.broadcasted_iota(jnp.int32, (HEAD, c_a), 1) % HEAD
            == lax.broadcasted_iota(jnp.int32, (HEAD, c_a), 0))
    seqs = range(bt)
    row = lambda ref, i: ref[i:i + 1, :]
    stack = lambda f: jnp.concatenate([f(i) for i in seqs], axis=0)
    piece = lambda x, i: x[i * HEAD:(i + 1) * HEAD]
    s = stack(lambda i: jnp.concatenate([s_ref[i, h] for h in range(heads)], axis=1))
    sa = _mask_dot(s * stack(lambda i: jnp.broadcast_to(row(a_ref, i), (HEAD, c_a))), hsum)
    v_rows = _mask_dot(stack(lambda i: jnp.where(diag, row(v_ref, i), 0.0)), hsum)
    s = stack(lambda i: piece(s, i) * row(w_ref, i) + piece(sa, i) * row(b_ref, i) + piece(v_rows, i) * row(k_ref, i))
    for i in seqs:
        for h in range(heads):
            sout_ref[i, h] = piece(s, i)[:, h * HEAD:(h + 1) * HEAD]
    y_rows = _mask_dot(stack(lambda i: piece(s, i) * row(r_ref, i)), hsum)
    y = stack(lambda i: jnp.sum(jnp.where(diag, piece(y_rows, i), 0.0), axis=0, keepdims=True))
    y = _head_norm_wide(y, hsum)
    o_ref[...] = (y * lg_ref[...] + lb_ref[...] + bonus_ref[...]) * g_ref[...]


def wkv_step(ops, s0, lnx_g, lnx_b, bt):
    bsz, c_a = ops[0].shape
    rows = pl.BlockSpec((bt, c_a), lambda i: (i, 0))
    st = pl.BlockSpec((bt,) + s0.shape[1:], lambda i: (i, 0, 0, 0))
    hsum = _head_sum_matrix(c_a)
    return pl.pallas_call(
        _wkv_step_kernel, name="wkv_step",
        grid=(bsz // bt,),
        in_specs=[rows] * 8 + [st, _const_spec((1, c_a)), _const_spec((1, c_a)), _const_spec(hsum.shape)],
        out_specs=[rows, st],
        out_shape=[jax.ShapeDtypeStruct((bsz, c_a), F32), jax.ShapeDtypeStruct(s0.shape, F32)],
        compiler_params=_params(("parallel",)),
    )(*ops, s0, lnx_g.reshape(1, c_a), lnx_b.reshape(1, c_a), hsum)


def rope_tables(pos):
    inv_freq = ROPE_THETA ** (-jnp.arange(ROT_HALF, dtype=F32) / ROT_HALF)
    ang = pos[:, None] * inv_freq[None, :]
    cos, sin = jnp.cos(ang), jnp.sin(ang)
    t = pos.shape[0]
    rest = HEAD - 2 * ROT_HALF
    c = jnp.concatenate([cos, cos, jnp.ones((t, rest), F32)], axis=1)
    s1 = jnp.concatenate([jnp.zeros((t, ROT_HALF), F32), sin, jnp.zeros((t, rest), F32)], axis=1)
    s2 = jnp.concatenate([-sin, jnp.zeros((t, HEAD - ROT_HALF), F32)], axis=1)
    rep = LANES // HEAD
    return tuple(jnp.tile(x, (1, rep)) for x in (c, s1, s2))


def _rope(x, c, s1, s2):
    width = x.shape[-1]
    rep = width // c.shape[-1]
    if rep > 1:
        c, s1, s2 = (jnp.concatenate([t] * rep, axis=1) for t in (c, s1, s2))
    return x * c + pltpu.roll(x, ROT_HALF, 1) * s1 + pltpu.roll(x, width - ROT_HALF, 1) * s2


def _attn_kernel(window, q_w, kv_w, zq_ref, kp_ref, vp_ref, rc_ref, rs1_ref, rs2_ref, pc_ref, ps1_ref, ps2_ref,
                 mk_ref, mv_ref, sink_ref, ob_ref, om_ref, kr_ref):
    zq = zq_ref[0]
    nblk = zq.shape[0] // window
    first = pl.program_id(1) * nblk
    scale = HEAD ** -0.5
    q = zq[:, :q_w]
    k = zq[:, q_w:q_w + kv_w]
    v = zq[:, q_w + kv_w:q_w + 2 * kv_w]
    qm = zq[:, q_w + 2 * kv_w:]
    qr = _rope(q, rc_ref[...], rs1_ref[...], rs2_ref[...]) * scale
    kr = _rope(k, rc_ref[...], rs1_ref[...], rs2_ref[...])
    kr_ref[0] = kr
    k_all = jnp.concatenate([_rope(kp_ref[0], pc_ref[...], ps1_ref[...], ps2_ref[...]), kr], axis=0)
    v_all = jnp.concatenate([vp_ref[0], v], axis=0)
    gqa = q_w // kv_w
    assert window & (window - 1) == 0
    qi = lax.broadcasted_iota(jnp.int32, (gqa * window, 2 * window), 0) & (window - 1)
    kj = lax.broadcasted_iota(jnp.int32, (gqa * window, 2 * window), 1)
    band = (kj > qi) & (kj <= qi + window)
    tdot = lambda a, b: lax.dot_general(a, b, (((1,), (1,)), ((), ())), preferred_element_type=F32)
    dot = lambda a, b: jnp.dot(a, b, preferred_element_type=F32)
    hsl = lambda h: slice(h * HEAD, (h + 1) * HEAD)
    rows = lambda i: slice(i * window, (i + 1) * window)
    keys = lambda i: slice(i * window, (i + 2) * window)
    wins = [(i, hk) for i in range(nblk) for hk in range(kv_w // HEAD)]
    mems = [(i, h) for i in range(nblk) for h in range(qm.shape[1] // HEAD)]
    qmb = (qm * scale).astype(BF16)
    mk = mk_ref[0].astype(BF16)
    mv = mv_ref[0].astype(BF16)
    kcat = [k_all[keys(i), hsl(hk)].astype(BF16) for i, hk in wins]
    vcat = [v_all[keys(i), hsl(hk)].astype(BF16) for i, hk in wins]
    qs = [jnp.concatenate([qr[rows(i), hsl(hk * gqa + g)] for g in range(gqa)], axis=0).astype(BF16)
          for i, hk in wins]
    s = [jnp.where(band & ((kj >= window) | (first + i > 0)), tdot(qs[u], kcat[u]), NEG_INF)
         for u, (i, hk) in enumerate(wins)]
    sm = [tdot(qmb[rows(i), hsl(h)], mk[:, hsl(h)]) for i, h in mems]
    sink = [jnp.concatenate([jnp.full((window, 1), sink_ref[hk * gqa + g], F32) for g in range(gqa)], axis=0)
            for i, hk in wins]
    m = [jnp.maximum(jnp.max(s[u], axis=-1, keepdims=True), sink[u]) for u in range(len(wins))]
    p = [jnp.exp(s[u] - m[u]) for u in range(len(wins))]
    pm = [jnp.exp(x - jnp.max(x, axis=-1, keepdims=True)) for x in sm]
    o = [dot(p[u].astype(BF16), vcat[u])
         / (jnp.sum(p[u], axis=-1, keepdims=True) + jnp.exp(sink[u] - m[u])) for u in range(len(wins))]
    om = [dot(pm[u].astype(BF16), mv[:, hsl(h)]) / jnp.sum(pm[u], axis=-1, keepdims=True)
          for u, (i, h) in enumerate(mems)]
    n_kv, n_mh = kv_w // HEAD, qm.shape[1] // HEAD
    ob = jnp.concatenate([jnp.concatenate([o[i * n_kv + hk][g * window:(g + 1) * window]
                                           for hk in range(n_kv) for g in range(gqa)], axis=1)
                          for i in range(nblk)], axis=0)
    ob_ref[0] = ob.astype(ob_ref.dtype)
    om_ref[0] = jnp.concatenate([jnp.concatenate(om[i * n_mh:(i + 1) * n_mh], axis=1) for i in range(nblk)],
                                axis=0).astype(om_ref.dtype)


def attn_prompt(zq, mk, mv, sinks, tables, window, q_w, kv_w):
    bsz, t, zw = zq.shape
    mem_w = zw - q_w - 2 * kv_w
    assert kv_w == LANES and q_w % kv_w == 0
    kcol, vcol = q_w // kv_w, q_w // kv_w + 1
    nblk = ATTN_BLOCKS if (t // window) % ATTN_BLOCKS == 0 else 1
    rows = nblk * window
    prev = lambda n: jnp.maximum(n * nblk - 1, 0)
    tab = pl.BlockSpec((rows, LANES), lambda b, n: (n, 0))
    ptab = pl.BlockSpec((window, LANES), lambda b, n: (prev(n), 0))
    mem = pl.BlockSpec((1,) + mk.shape[1:], lambda b, n: (b, 0, 0))
    return pl.pallas_call(
        functools.partial(_attn_kernel, window, q_w, kv_w), name="attn",
        grid=(bsz, t // rows),
        in_specs=[pl.BlockSpec((1, rows, zw), lambda b, n: (b, n, 0)),
                  pl.BlockSpec((1, window, kv_w), lambda b, n: (b, prev(n), kcol)),
                  pl.BlockSpec((1, window, kv_w), lambda b, n: (b, prev(n), vcol)),
                  tab, tab, tab, ptab, ptab, ptab, mem, mem,
                  pl.BlockSpec(memory_space=pltpu.SMEM)],
        out_specs=[pl.BlockSpec((1, rows, q_w), lambda b, n: (b, n, 0)),
                   pl.BlockSpec((1, rows, mem_w), lambda b, n: (b, n, 0)),
                   pl.BlockSpec((1, rows, kv_w), lambda b, n: (b, n, 0))],
        out_shape=[jax.ShapeDtypeStruct((bsz, t, q_w), BF16), jax.ShapeDtypeStruct((bsz, t, mem_w), BF16),
                   jax.ShapeDtypeStruct((bsz, t, kv_w), F32)],
        compiler_params=_params(("parallel", "parallel")),
    )(zq, zq, zq, *tables, *tables, mk, mv, sinks)


def _attn_step_kernel(q_w, kv_w, zq_ref, ck_ref, cv_ref, mk_ref, mv_ref, rc_ref, rs1_ref, rs2_ref, sink_ref,
                      ob_ref, om_ref, nk_ref, nv_ref):
    bt = zq_ref.shape[0]
    window = ck_ref.shape[1]
    mem_w = om_ref.shape[1]
    n_q, gqa, per_vreg = q_w // HEAD, q_w // kv_w, LANES // HEAD
    scale = HEAD ** -0.5
    zq = zq_ref[...]
    q = _rope(zq[:, :q_w], rc_ref[...], rs1_ref[...], rs2_ref[...]) * scale
    k_new = _rope(zq[:, q_w:q_w + kv_w], rc_ref[...], rs1_ref[...], rs2_ref[...])
    v_new = zq[:, q_w + kv_w:q_w + 2 * kv_w]
    qm = zq[:, q_w + 2 * kv_w:] * scale
    own = lambda w: (lax.broadcasted_iota(jnp.int32, (n_q, w), 1) // HEAD
                     == lax.broadcasted_iota(jnp.int32, (n_q, w), 0))
    own_q, own_m = own(q_w), own(mem_w)
    hrow = lax.broadcasted_iota(jnp.int32, (n_q, LANES), 0)
    hblk = lax.broadcasted_iota(jnp.int32, (n_q, LANES), 1) // HEAD
    swap = (hrow % per_vreg) != (hrow // gqa)
    keep = hblk == hrow % per_vreg
    key_ok = lax.broadcasted_iota(jnp.int32, (n_q, window), 1) >= 1
    wrow = lax.broadcasted_iota(jnp.int32, (window, kv_w), 0)
    sink = sink_ref[...]
    bs = range(bt)
    tdot = lambda a, b: lax.dot_general(a.astype(BF16), b.astype(BF16), (((1,), (1,)), ((), ())),
                                        preferred_element_type=F32)
    for b in bs:
        nk_ref[b] = jnp.where(wrow == window - 1, k_new[b:b + 1], pltpu.roll(ck_ref[b], window - 1, 0))
        nv_ref[b] = jnp.where(wrow == window - 1, v_new[b:b + 1], pltpu.roll(cv_ref[b], window - 1, 0))
    q8 = []
    for b in bs:
        rep = jnp.where(own_q, q[b:b + 1], 0.0)
        fold = sum(rep[:, c * LANES:(c + 1) * LANES] for c in range(q_w // LANES))
        q8.append(jnp.where(swap, pltpu.roll(fold, HEAD, 1), fold))
    qm8 = [jnp.where(own_m, qm[b:b + 1], 0.0) for b in bs]
    s = [jnp.where(key_ok, tdot(q8[b], ck_ref[b]), NEG_INF) for b in bs]
    sm = [tdot(qm8[b], mk_ref[b]) for b in bs]
    s_new = [jnp.sum(q8[b] * k_new[b:b + 1], axis=1, keepdims=True) for b in bs]
    m = [jnp.maximum(jnp.maximum(jnp.max(s[b], axis=1, keepdims=True), s_new[b]), sink) for b in bs]
    p = [jnp.exp(s[b] - m[b]) for b in bs]
    p_new = [jnp.exp(s_new[b] - m[b]) for b in bs]
    pm = [jnp.exp(sm[b] - jnp.max(sm[b], axis=1, keepdims=True)) for b in bs]
    o8 = [(_bdot(p[b], cv_ref[b]) + p_new[b] * v_new[b:b + 1])
          / (jnp.sum(p[b], axis=1, keepdims=True) + p_new[b] + jnp.exp(sink - m[b])) for b in bs]
    om8 = [_bdot(pm[b], mv_ref[b]) / jnp.sum(pm[b], axis=1, keepdims=True) for b in bs]
    ob, om = [], []
    for b in bs:
        o = jnp.where(swap, pltpu.roll(o8[b], HEAD, 1), o8[b])
        o = jnp.concatenate([jnp.where(keep, o, 0.0)] * (q_w // LANES), axis=1)
        ob.append(jnp.sum(jnp.where(own_q, o, 0.0), axis=0, keepdims=True))
        om.append(jnp.sum(jnp.where(own_m, om8[b], 0.0), axis=0, keepdims=True))
    ob_ref[...] = jnp.concatenate(ob, axis=0)
    om_ref[...] = jnp.concatenate(om, axis=0)


def attn_step(zq, ck, cv, mk, mv, sinks, tables, q_w, kv_w, bt):
    bsz, zw = zq.shape
    mem_w = zw - q_w - 2 * kv_w
    window, n_mem = ck.shape[1], mk.shape[1]
    n_q = q_w // HEAD
    assert kv_w == LANES and mem_w // HEAD <= n_q and n_q == 8
    rows = lambda w: pl.BlockSpec((bt, w), lambda i: (i, 0))
    cache = pl.BlockSpec((bt, window, kv_w), lambda i: (i, 0, 0))
    mem = pl.BlockSpec((bt, n_mem, mem_w), lambda i: (i, 0, 0))
    consts = list(tables) + [sinks.reshape(n_q, 1)]
    const_specs = [_const_spec(c.shape) for c in consts]
    return pl.pallas_call(
        functools.partial(_attn_step_kernel, q_w, kv_w), name="attn_step",
        grid=(bsz // bt,),
        in_specs=[rows(zw), cache, cache, mem, mem] + const_specs,
        out_specs=[rows(q_w), rows(mem_w), cache, cache],
        out_shape=[jax.ShapeDtypeStruct((bsz, q_w), F32), jax.ShapeDtypeStruct((bsz, mem_w), F32),
                   jax.ShapeDtypeStruct(ck.shape, F32), jax.ShapeDtypeStruct(cv.shape, F32)],
        compiler_params=_params(("parallel",)),
    )(zq, ck, cv, mk, mv, *consts)


def _proj_kernel(x_ref, w_ref, o_ref):
    o_ref[...] = jnp.dot(x_ref[...].astype(BF16), w_ref[...], preferred_element_type=F32)


def proj(x, w, tm):
    n, d = x.shape
    return pl.pallas_call(
        _proj_kernel, name="proj",
        grid=(n // tm,),
        in_specs=[pl.BlockSpec((tm, d), lambda i: (i, 0)), _const_spec(w.shape)],
        out_specs=pl.BlockSpec((tm, w.shape[1]), lambda i: (i, 0)),
        out_shape=jax.ShapeDtypeStruct((n, w.shape[1]), F32),
        compiler_params=_params(("parallel",)),
    )(x, w)


def _merge_kernel(alpha, n_tiles, x_ref, oa_ref, ob_ref, om_ref, gt_ref, lig_ref, lib_ref, pa_ref, pb_ref, pm_ref,
                  wo_ref, l1g_ref, l1b_ref, wr_ref, br_ref, *refs):
    outs = refs[-4:]

    @pl.when(pl.program_id(0) >= n_tiles)
    def _():
        for ref in outs:
            ref[...] = jnp.zeros_like(ref)

    @pl.when(pl.program_id(0) < n_tiles)
    def _():
        _merge_tile(alpha, x_ref, oa_ref, ob_ref, om_ref, gt_ref, lig_ref, lib_ref, pa_ref, pb_ref, pm_ref, wo_ref,
                    l1g_ref, l1b_ref, wr_ref, br_ref, *outs)


def _merge_tile(alpha, x_ref, oa_ref, ob_ref, om_ref, gt_ref, lig_ref, lib_ref, pa_ref, pb_ref, pm_ref, wo_ref,
                l1g_ref, l1b_ref, wr_ref, br_ref, x1_ref, x1t_ref, eidx_ref, gate_ref):
    d = x_ref.shape[1]
    xn = _ln(x_ref[...], lig_ref[...], lib_ref[...])
    gts = _sigmoid(gt_ref[...].astype(F32))
    merged = (gts[:, :d] * _bdot(oa_ref[...], pa_ref[...]) + gts[:, d:2 * d] * _bdot(ob_ref[...], pb_ref[...])
              + gts[:, 2 * d:] * _bdot(om_ref[...], pm_ref[...]))
    x1 = _ln(alpha * xn + _bdot(merged, wo_ref[...]), l1g_ref[...], l1b_ref[...])
    x1_ref[...] = x1
    _rows_to_tiles(x1t_ref, x1)
    x_hi, x_lo = _pieces(x1, 2)
    w_hi, w_lo = wr_ref[0], wr_ref[1]
    dot = lambda a, b: jnp.dot(a, b, preferred_element_type=F32)
    logits = dot(x_hi, w_hi) + dot(x_hi, w_lo) + dot(x_lo, w_hi) + br_ref[...]
    lane = lax.broadcasted_iota(jnp.int32, logits.shape, 1)
    lane_f = lane.astype(F32)
    first = lambda hit: jnp.min(jnp.where(hit, lane_f, float(LANES)), axis=-1, keepdims=True).astype(jnp.int32)
    gmask = lane < N_GROUPS
    gl = jnp.where(gmask, logits, NEG_INF)
    gmax = jnp.max(gl, axis=-1, keepdims=True)
    gidx = first(gl == gmax)
    g_w = 1.0 / jnp.sum(jnp.where(gmask, jnp.exp(gl - gmax), 0.0), axis=-1, keepdims=True)
    lo = N_GROUPS + gidx * EXPERTS_PER_GROUP
    el = jnp.where((lane >= lo) & (lane < lo + EXPERTS_PER_GROUP), logits, NEG_INF)
    v1 = jnp.max(el, axis=-1, keepdims=True)
    i1 = first(el == v1)
    el2 = jnp.where(lane == i1, NEG_INF, el)
    v2 = jnp.max(el2, axis=-1, keepdims=True)
    i2 = first(el2 == v2)
    e2 = jnp.exp(v2 - v1)
    gate1 = g_w / (1.0 + e2)
    eidx = jnp.where(lane == 0, (i1 - N_GROUPS).astype(F32), jnp.where(lane == 1, (i2 - N_GROUPS).astype(F32), 0.0))
    eidx_ref[...] = eidx.T[:eidx_ref.shape[0]].astype(jnp.int32)
    gate_ref[...] = jnp.where(lane == 0, gate1, jnp.where(lane == 1, gate1 * e2, 0.0))


def merge(x, oa, ob, om, gt, w, tm, alpha, n_total, row_offset=0, into=None):
    n, d = x.shape
    assert row_offset % tm == 0 and n % tm == 0 and n_total % tm == 0
    off = row_offset // tm
    into = list(into or [])
    n_tiles = n // tm
    steps = n_tiles if into else n_total // tm
    rows = lambda a: pl.BlockSpec((tm, a.shape[1]), lambda i: (jnp.minimum(i, n_tiles - 1), 0))
    out = lambda width: pl.BlockSpec((tm, width), lambda i: (i + off, 0))
    consts = [w['ln_in_g'], w['ln_in_b'], w['p_a'], w['p_b'], w['p_m'], w['w_o'], w['ln1_g'], w['ln1_b'],
              w['w_route'], w['b_route']]
    n_in = 5 + len(consts)
    return pl.pallas_call(
        functools.partial(_merge_kernel, alpha, n_tiles), name="merge",
        grid=(steps,),
        in_specs=[rows(a) for a in (x, oa, ob, om, gt)] + [_const_spec(c.shape) for c in consts]
        + [pl.BlockSpec(memory_space=pl.ANY)] * len(into),
        out_specs=[out(d), pl.BlockSpec((tm * TILE_ROWS, LANES), lambda i: (i + off, 0)),
                   pl.BlockSpec((TILE_ROWS, tm), lambda i: (0, i + off)), out(LANES)],
        out_shape=[jax.ShapeDtypeStruct((n_total, d), F32), jax.ShapeDtypeStruct((n_total * TILE_ROWS, LANES), F32),
                   jax.ShapeDtypeStruct((TILE_ROWS, n_total), jnp.int32), jax.ShapeDtypeStruct((n_total, LANES), F32)],
        input_output_aliases={n_in + k: k for k in range(len(into))},
        compiler_params=_params(("parallel",)),
    )(x, oa, ob, om, gt, *consts, *into)


ROW_DMA_UNROLL = 8
MOE_BUFFERS = 3
DRAIN_STEPS = 2


TILE_ROWS = 8


def _rows_from_tiles(ref, n):
    return jnp.concatenate([ref[pl.ds(s, n, stride=TILE_ROWS), :] for s in range(TILE_ROWS)], axis=1)


def _rows_to_tiles(ref, x):
    for s in range(TILE_ROWS):
        ref[pl.ds(s, x.shape[0], stride=TILE_ROWS), :] = x[:, s * LANES:(s + 1) * LANES]


def _row_copies(asg_ref, base, count, n_asg, x_hbm, buf, y_hbm, sem, gather, unrolled):
    def tile(ref, idx):
        start = idx * TILE_ROWS
        return ref.at[pl.ds(start if isinstance(idx, int) else pl.multiple_of(start, TILE_ROWS), TILE_ROWS)]

    def one(r, priority):
        a = asg_ref[base + r]
        if gather:
            tok = jnp.minimum(a, n_asg - 1)
            tok = jnp.where(tok >= n_asg // 2, tok - n_asg // 2, tok)
            copy = pltpu.make_async_copy(tile(x_hbm, tok), tile(buf, r), sem)
        else:
            copy = pltpu.make_async_copy(tile(buf, r), tile(y_hbm, a), sem)
        copy.start(priority=priority)

    if unrolled:
        for r in range(count):
            one(r, r % 2)
        return

    def body(g, carry):
        for j in range(ROW_DMA_UNROLL):
            one(g * ROW_DMA_UNROLL + j, j % 2)
        return carry
    lax.fori_loop(0, count // ROW_DMA_UNROLL, body, 0)


def _moe_expert_kernel(n_tokens, n_asg, asg_ref, be_ref, nu_ref, x_hbm, wg_ref, wu_ref, wd_ref, y_hbm,
                       xbuf, ybuf, wgb, wub, wdb, gsem, ssem):
    i = pl.program_id(0)
    used = nu_ref[0]
    rows = xbuf.shape[1] // TILE_ROWS
    n_buf = xbuf.shape[0]
    n_blocks = be_ref.shape[0]
    blk = jnp.minimum(i, n_blocks - 1)

    @pl.when((i == 0) | (be_ref[blk] != be_ref[jnp.maximum(blk - 1, 0)]))
    def _():
        wgb[...] = wg_ref[0, 0].astype(BF16)
        wub[...] = wu_ref[0, 0].astype(BF16)
        wdb[...] = wd_ref[0, 0].astype(BF16)

    def wait_gather(slot):
        pltpu.make_async_copy(x_hbm.at[pl.ds(0, rows * TILE_ROWS)], xbuf.at[slot], gsem.at[slot]).wait()

    def wait_scatter(slot):
        pltpu.make_async_copy(ybuf.at[slot], y_hbm.at[pl.ds(0, rows * TILE_ROWS)], ssem.at[slot]).wait()

    def gather(b, slot, unrolled):
        _row_copies(asg_ref, b * rows, rows, n_asg, x_hbm, xbuf.at[slot], y_hbm, gsem.at[slot], True, unrolled)

    def scatter(b, slot, unrolled):
        _row_copies(asg_ref, b * rows, rows, n_asg, x_hbm, ybuf.at[slot], y_hbm, ssem.at[slot], False, unrolled)

    def expert(slot):
        xb = _rows_from_tiles(xbuf.at[slot], rows).astype(BF16)
        hg = jnp.dot(xb, wgb[...], preferred_element_type=F32)
        hu = jnp.dot(xb, wub[...], preferred_element_type=F32)
        h = hg * _sigmoid(hg) * hu
        _rows_to_tiles(ybuf.at[slot], jnp.dot(h.astype(BF16), wdb[...], preferred_element_type=F32))

    @pl.when(i == 0)
    def _():
        ybuf[1] = jnp.zeros(ybuf.shape[1:], F32)
        stride = n_asg // 2
        gaps = [(k * stride + n_tokens, stride - n_tokens) for k in range(2)]
        gaps.append((n_asg, y_hbm.shape[0] // TILE_ROWS - n_asg))
        fills = [pltpu.make_async_copy(ybuf.at[1, pl.ds(0, min(rows, count - j) * TILE_ROWS)],
                                       y_hbm.at[pl.ds((start + j) * TILE_ROWS, min(rows, count - j) * TILE_ROWS)],
                                       ssem.at[1])
                 for start, count in gaps for j in range(0, count, rows)]
        for copy in fills:
            copy.start()
        for copy in fills:
            copy.wait()

    @pl.when((i >= 2) & (i - 2 < used))
    def _():
        wait_scatter((i - 2) % n_buf)

    steady = (i >= 1) & (i + 2 < used)
    for slot in range(n_buf):
        @pl.when(steady & (i % n_buf == slot))
        def _():
            wait_gather(slot)
            gather(i + 2, (slot + 2) % n_buf, True)
            scatter(i - 1, (slot - 1) % n_buf, True)
            expert(slot)

    @pl.when(jnp.logical_not(steady))
    def _():
        slot = i % n_buf

        @pl.when(i == 0)
        def _():
            for b in range(2):
                @pl.when(b < used)
                def _():
                    gather(b, b, False)

        @pl.when(i < used)
        def _():
            wait_gather(slot)

        @pl.when(i + 2 < used)
        def _():
            gather(i + 2, (i + 2) % n_buf, False)

        @pl.when((i >= 1) & (i - 1 < used))
        def _():
            scatter(i - 1, (i - 1) % n_buf, False)

        @pl.when(i < used)
        def _():
            expert(slot)


def moe_experts(x1_tiles, n_tokens, stride, asg, blk_e, n_used, e_gate, e_up, e_down):
    d = e_gate.shape[2]
    assert d == TILE_ROWS * LANES and x1_tiles.shape == (stride * TILE_ROWS, LANES)
    n_blocks = blk_e.shape[0]
    ff = e_gate.shape[-1]
    n_asg = 2 * stride
    n_rows = n_asg + e_gate.shape[1] * EXPERT_BLOCK
    weight = lambda shape: pl.BlockSpec(
        (1, 1) + shape, lambda i, asg, be, nu: (0, be[jnp.minimum(i, n_blocks - 1)], 0, 0))
    return pl.pallas_call(
        functools.partial(_moe_expert_kernel, n_tokens, n_asg), name="moe_expert",
        grid_spec=pltpu.PrefetchScalarGridSpec(
            num_scalar_prefetch=3,
            grid=(n_blocks + DRAIN_STEPS,),
            in_specs=[pl.BlockSpec(memory_space=pl.ANY), weight((d, ff)), weight((d, ff)), weight((ff, d))],
            out_specs=pl.BlockSpec(memory_space=pl.ANY),
            scratch_shapes=[pltpu.VMEM((MOE_BUFFERS, EXPERT_BLOCK * TILE_ROWS, LANES), F32),
                            pltpu.VMEM((MOE_BUFFERS, EXPERT_BLOCK * TILE_ROWS, LANES), F32),
                            pltpu.VMEM((d, ff), BF16), pltpu.VMEM((d, ff), BF16), pltpu.VMEM((ff, d), BF16),
                            pltpu.SemaphoreType.DMA((MOE_BUFFERS,)), pltpu.SemaphoreType.DMA((MOE_BUFFERS,))]),
        out_shape=jax.ShapeDtypeStruct((n_rows * TILE_ROWS, LANES), F32),
        compiler_params=_params(("arbitrary",)),
    )(asg, blk_e, n_used, x1_tiles, e_gate, e_up, e_down)


def _moe_combine_kernel(alpha, lead_tiles, y0_ref, y1_ref, x1_ref, gate_ref, g_ref, b_ref, lead_ref, tail_ref):
    i = pl.program_id(0)
    gate = gate_ref[...]
    tm = x1_ref.shape[0]
    moe = gate[:, 0:1] * _rows_from_tiles(y0_ref, tm) + gate[:, 1:2] * _rows_from_tiles(y1_ref, tm)
    out = _ln(alpha * x1_ref[...] + moe, g_ref[...], b_ref[...])

    @pl.when(i < lead_tiles)
    def _():
        lead_ref[...] = out

    @pl.when(i >= lead_tiles)
    def _():
        tail_ref[...] = out[:tail_ref.shape[0]]


def moe_combine(y, x1, gate, g, b, tm, alpha, n, n_lead):
    stride, d = x1.shape
    assert n_lead % tm == 0 and stride - n_lead == tm and n_lead < n <= stride
    lead_tiles = n_lead // tm
    return pl.pallas_call(
        functools.partial(_moe_combine_kernel, alpha, lead_tiles), name="moe_combine",
        grid=(stride // tm,),
        in_specs=[pl.BlockSpec((tm * TILE_ROWS, LANES), lambda i: (i, 0)),
                  pl.BlockSpec((tm * TILE_ROWS, LANES), lambda i: (i + stride // tm, 0)),
                  pl.BlockSpec((tm, d), lambda i: (i, 0)),
                  pl.BlockSpec((tm, LANES), lambda i: (i, 0)), _const_spec((1, d)), _const_spec((1, d))],
        out_specs=[pl.BlockSpec((tm, d), lambda i: (jnp.minimum(i, lead_tiles - 1), 0)),
                   pl.BlockSpec((n - n_lead, d), lambda i: (0, 0))],
        out_shape=[jax.ShapeDtypeStruct((n_lead, d), F32), jax.ShapeDtypeStruct((n - n_lead, d), F32)],
        compiler_params=_params(("arbitrary",)),
    )(y, y, x1, gate, g.reshape(1, d), b.reshape(1, d))


def moe_routing(experts, stride):
    top_k, n = experts.shape
    n_exp = N_GROUPS * EXPERTS_PER_GROUP
    a = n * top_k
    flat_e = experts.reshape(a)
    order = jnp.argsort(flat_e, stable=True).astype(jnp.int32)
    order = order + (order // n) * (stride - n)
    counts = jnp.sum((flat_e[:, None] == jnp.arange(n_exp, dtype=jnp.int32)[None, :]).astype(jnp.int32), axis=0)
    ends = jnp.cumsum(counts)
    padded = (counts + EXPERT_BLOCK - 1) // EXPERT_BLOCK * EXPERT_BLOCK
    pad_end = jnp.cumsum(padded)
    n_blocks = -(-a // EXPERT_BLOCK) + n_exp
    blk_start = jnp.arange(n_blocks, dtype=jnp.int32) * EXPERT_BLOCK
    blk_e = jnp.minimum(jnp.sum((pad_end[None, :] <= blk_start[:, None]).astype(jnp.int32), axis=1), n_exp - 1)
    slot = jnp.arange(n_blocks * EXPERT_BLOCK, dtype=jnp.int32)
    slot_e = jnp.repeat(blk_e, EXPERT_BLOCK)
    rank = slot - (pad_end - padded)[slot_e]
    spare = top_k * stride + jnp.clip(slot - ends[slot_e], 0, n_exp * EXPERT_BLOCK - 1)
    asg = jnp.where(rank < counts[slot_e], order[jnp.clip((ends - counts)[slot_e] + rank, 0, a - 1)], spare)
    n_used = (pad_end[-1:] // EXPERT_BLOCK).astype(jnp.int32)
    return asg.astype(jnp.int32), blk_e.astype(jnp.int32), n_used


def hier_moe_ln(x1, x1_tiles, eidx, gate, w, tm, alpha, n_tokens, n_lead):
    stride = x1.shape[0]
    asg, blk_e, n_used = moe_routing(eidx[:2, :n_tokens], stride)
    y = moe_experts(x1_tiles, n_tokens, stride, asg, blk_e, n_used, w['e_gate'], w['e_up'], w['e_down'])
    return moe_combine(y, x1, gate, w['ln2_g'], w['ln2_b'], tm, alpha, n_tokens, n_lead)


def kernel(x_prompt, x_sample, mem_prompt, state_wkv, state_shift, cache_win_k, cache_win_v, cache_mem_k, cache_mem_v, ln_in_g, ln_in_b, w_in, mu, w0, w_up, a0, a_up, g_up, k_k, k_a, r_k, lnx_g, lnx_b, sinks, w_mem_kv, p_a, p_b, p_m, w_o, ln1_g, ln1_b, w_group, b_group, w_router, b_router, e_gate, e_up, e_down, ln2_g, ln2_b):
    depth = w_in.shape[0]
    assert depth == 1, "single-layer step"
    bsz, seq, d = x_prompt.shape
    dec = x_sample.shape[0]
    assert x_sample.shape[1] == 1
    c_shift = mu.shape[-1]
    c_a = w0.shape[-1]
    window, kv_w = cache_win_k.shape[2], cache_win_k.shape[3] * cache_win_k.shape[4]
    n_mem, mem_w = cache_mem_k.shape[2], cache_mem_k.shape[3] * cache_mem_k.shape[4]
    q_w = sinks.shape[-1] * HEAD
    qkvm_w = q_w + 2 * kv_w + mem_w
    alpha = (2.0 * depth) ** 0.25
    past_len = float(PAST_LEN)
    chunk = WKV_CHUNK

    w_in_b = w_in[0].astype(BF16)
    w_parts = [w_in_b[:, :c_shift], w_in_b[:, c_shift:c_shift + qkvm_w], w_in_b[:, c_shift + qkvm_w:]]
    rp = dict(mu=mu[0], w0=w0[0], w_up=w_up[0], a0=a0[0], a_up=a_up[0], g_up=g_up[0], k_k=k_k[0], k_a=k_a[0],
              r_k=r_k[0].reshape(-1))
    n_route = N_GROUPS * (1 + EXPERTS_PER_GROUP)
    mw = dict(ln_in_g=ln_in_g.reshape(1, d), ln_in_b=ln_in_b.reshape(1, d), p_a=p_a[0].astype(BF16),
              p_b=p_b[0].astype(BF16), p_m=p_m[0].astype(BF16), w_o=w_o[0].astype(BF16),
              ln1_g=ln1_g[0].reshape(1, d), ln1_b=ln1_b[0].reshape(1, d),
              w_route=jnp.stack(_pieces(jnp.pad(jnp.concatenate([w_group[0], w_router[0]], axis=1),
                                                ((0, 0), (0, LANES - n_route))), 2)),
              b_route=jnp.pad(jnp.concatenate([b_group[0], b_router[0]]), (0, LANES - n_route)).reshape(1, LANES),
              e_gate=e_gate, e_up=e_up, e_down=e_down, ln2_g=ln2_g[0], ln2_b=ln2_b[0])

    xp = x_prompt.reshape(bsz * seq, d)
    zq, zg, shift_p, prep = ln_proj_prep(x_prompt, ln_in_g, ln_in_b, w_parts, jnp.zeros((bsz, 1, c_shift), F32), rp,
                                         chunk, ROW_TILE, BF16)
    o_a, wkv_p = wkv(prep, jnp.zeros((bsz, c_a // HEAD, HEAD, HEAD), F32), lnx_g[0], lnx_b[0], chunk)
    mkv = proj(mem_prompt.reshape(bsz * n_mem, d), w_mem_kv[0].astype(BF16), ROW_TILE).reshape(bsz, n_mem, 2 * mem_w)
    mk_p, mv_p = mkv[..., :mem_w], mkv[..., mem_w:]
    tables = rope_tables(jnp.arange(seq, dtype=F32))
    o_b, o_m, k_rot = attn_prompt(zq, mk_p, mv_p, sinks[0], tables, window, q_w, kv_w)
    n_all = bsz * seq + dec
    n_buf = -(-n_all // ROW_TILE) * ROW_TILE
    routed = merge(xp, o_a.reshape(-1, c_a), o_b.reshape(-1, q_w), o_m.reshape(-1, mem_w),
                   zg.reshape(bsz * seq, -1), mw, ROW_TILE, alpha, n_buf)
    shift_p = shift_p[:, 0]
    kb_p = k_rot[:, -window:].reshape(bsz, window, H_KV, HEAD)
    vb_p = zq[:, -window:, q_w + kv_w:q_w + 2 * kv_w].reshape(bsz, window, H_KV, HEAD)

    xs = x_sample.reshape(dec, d)
    zq_s, zg_s, zr_s, ops_s = ln_proj_prep(xs.reshape(1, dec, d), ln_in_g, ln_in_b, w_parts,
                                           state_shift[0].reshape(1, dec, c_shift), rp, 1, dec, F32)
    zq_s, zg_s, zr_s = zq_s[0], zg_s[0], zr_s[0]
    o_a_s, wkv_s = wkv_step([a.reshape(dec, c_a) for a in ops_s], state_wkv[0], lnx_g[0], lnx_b[0], STEP_BATCH)
    tables_s = rope_tables(jnp.full((1,), past_len, F32))
    o_b_s, o_m_s, nk_s, nv_s = attn_step(
        zq_s, cache_win_k[0].reshape(dec, window, kv_w), cache_win_v[0].reshape(dec, window, kv_w),
        cache_mem_k[0].reshape(dec, n_mem, mem_w), cache_mem_v[0].reshape(dec, n_mem, mem_w),
        sinks[0], tables_s, q_w, kv_w, STEP_BATCH)
    x1, x1_tiles, eidx, gate = merge(xs, o_a_s, o_b_s, o_m_s, zg_s, mw, dec, alpha, n_buf, bsz * seq, routed)

    y_prompt, y_sample = hier_moe_ln(x1, x1_tiles, eidx, gate, mw, ROW_TILE, alpha, n_all, bsz * seq)
    y_prompt = y_prompt.reshape(bsz, seq, d)
    y_sample = y_sample.reshape(dec, 1, d)

    sd = state_wkv.dtype
    return (y_prompt, y_sample, wkv_p[None].astype(sd), wkv_s[None].astype(sd), shift_p[None], zr_s[None],
            kb_p[None], vb_p[None], nk_s.reshape(dec, window, H_KV, HEAD)[None],
            nv_s.reshape(dec, window, H_KV, HEAD)[None],
            mk_p.reshape(bsz, n_mem, -1, HEAD)[None], mv_p.reshape(bsz, n_mem, -1, HEAD)[None])
```

```python
import functools
import math

import jax
import jax.numpy as jnp
from jax import lax
from jax.experimental import pallas as pl
from jax.experimental.pallas import tpu as pltpu

F32 = jnp.float32
BF16 = jnp.bfloat16

ROW_TILE = 256
WKV_CHUNK = 64
CHUNKS_PER_STEP = 8
SCAN_BATCH = 8
ATTN_BLOCKS = 4
STEP_BATCH = 16
PROJ_PIECE = 512

HEAD = 64
LANES = 128
H_KV = 2
ROT_HALF = 8
ROPE_THETA = 500000.0
PAST_LEN = 8192
N_GROUPS = 4
EXPERTS_PER_GROUP = 8
EXPERT_BLOCK = 128
LN_EPS = 1e-5
LNX_EPS = 64e-5
NEG_INF = -1e30
VMEM_LIMIT = 48 * 1024 * 1024


def _pieces(x, n):
    out = []
    for _ in range(n):
        p = x.astype(BF16)
        out.append(p)
        x = x - p.astype(F32)
    return out


def _mask_dot(x, mask, n=2):
    return sum(jnp.dot(p, mask, preferred_element_type=F32) for p in _pieces(x, n))


def _split3(x, axis, lhs):
    hi = x.astype(BF16).astype(F32)
    lo = x - hi
    return jnp.concatenate([hi, hi, lo] if lhs else [hi, lo, hi], axis=axis).astype(BF16)


def _dot3(a, b):
    return jnp.dot(_split3(a, 1, True), _split3(b, 0, False), preferred_element_type=F32)


def _dot3_t(a, b):
    return lax.dot_general(_split3(a, 1, True), _split3(b, 1, False), (((1,), (1,)), ((), ())),
                           preferred_element_type=F32)


def _bdot(a, b):
    return jnp.dot(a.astype(BF16), b.astype(BF16), preferred_element_type=F32)


def _bdot_t(a, b):
    return lax.dot_general(a.astype(BF16), b.astype(BF16), (((1,), (1,)), ((), ())), preferred_element_type=F32)


def _sigmoid(x):
    return 0.5 * jnp.tanh(0.5 * x) + 0.5


def _ln(x, g, b):
    mu = jnp.mean(x, axis=-1, keepdims=True)
    xc = x - mu
    var = jnp.mean(xc * xc, axis=-1, keepdims=True)
    return xc * lax.rsqrt(var + LN_EPS) * g + b


def _const_spec(shape):
    nd = len(shape)
    return pl.BlockSpec(shape, lambda *_: (0,) * nd)


def _params(sem):
    return pltpu.CompilerParams(dimension_semantics=sem, vmem_limit_bytes=VMEM_LIMIT)


def _ln_proj_prep_kernel(chunk, x_ref, g_ref, b_ref, wr_ref, wq_ref, wg_ref, prev_ref, mu_ref, w0_ref, wup_ref,
                         a0_ref, aup_ref, gup_ref, kk_ref, ka_ref, rk_ref, hsum_ref, tril_ref,
                         zq_ref, zg_ref, zr_ref, *refs):
    out_refs, carry_ref = refs[:-1], refs[-1]
    xn = _ln(x_ref[0], g_ref[...], b_ref[...]).astype(BF16)
    z = jnp.dot(xn, wr_ref[...], preferred_element_type=F32)
    pending = [(o_ref, w_ref, c) for o_ref, w_ref in ((zq_ref, wq_ref), (zg_ref, wg_ref))
               for c in range(0, w_ref.shape[1], PROJ_PIECE)]

    def project(n_pieces):
        for _ in range(min(n_pieces, len(pending))):
            o_ref, w_ref, c = pending.pop(0)
            o_ref[0, :, c:c + PROJ_PIECE] = jnp.dot(xn, w_ref[:, c:c + PROJ_PIECE],
                                                    preferred_element_type=F32).astype(o_ref.dtype)

    tt = z.shape[0]
    c_a = w0_ref.shape[-1]
    r_w, r_a, r_g = wup_ref.shape[0], aup_ref.shape[0], gup_ref.shape[0]
    if chunk == 1:
        zr_ref[0] = z
        prev = prev_ref[0]
    else:
        zr_ref[0] = z[tt - 1:tt, :]

        @pl.when(pl.program_id(1) == 0)
        def _():
            carry_ref[...] = prev_ref[0]

        row = lax.broadcasted_iota(jnp.int32, z.shape, 0)
        prev = jnp.where(row == 0, carry_ref[...], pltpu.roll(z, 1, 0))
        carry_ref[...] = z[tt - 1:tt, :]
    zs = z + (prev - z) * mu_ref[...]
    r = zs[:, :c_a]
    k = zs[:, c_a:2 * c_a]
    v = zs[:, 2 * c_a:3 * c_a]
    o = 3 * c_a
    xw = zs[:, o:o + r_w]
    xa = zs[:, o + r_w:o + r_w + r_a]
    xg = zs[:, o + r_w + r_a:o + r_w + r_a + r_g]
    project(1)
    warg = -(w0_ref[...] + _dot3(jnp.tanh(xw), wup_ref[...]))
    softplus = jnp.maximum(warg, 0.0) + jnp.log1p(jnp.exp(-jnp.abs(warg)))
    lw = -jnp.exp(-softplus - 0.5)
    project(1)
    a = _sigmoid(a0_ref[...] + _dot3(xa, aup_ref[...]))
    g = _dot3(_sigmoid(xg), gup_ref[...])
    project(1)
    kkr = k * kk_ref[...]
    kk = kkr / jnp.maximum(jnp.sqrt(_mask_dot(kkr * kkr, hsum_ref[...])), 1e-12)
    project(1)
    k2 = k * (1.0 + (a - 1.0) * ka_ref[...])
    bonus = _mask_dot(r * k2 * rk_ref[...], hsum_ref[...]) * v
    kb = kk * a
    project(1)
    if chunk == 1:
        outs = (r, jnp.exp(lw), k2, v, -kk, kb, g, bonus)
        for ref, val in zip(out_refs, outs):
            ref[0] = val
        project(len(pending))
        return
    pieces = _pieces(lw, 3)
    ones3 = jnp.ones((chunk, 3 * chunk), BF16)
    cw, cwl = [], []
    for c in range(tt // chunk):
        stack = jnp.concatenate([p[c * chunk:(c + 1) * chunk] for p in pieces], axis=0)
        cw.append(jnp.dot(tril_ref[...], stack, preferred_element_type=F32))
        cwl.append(jnp.dot(ones3, stack, preferred_element_type=F32))
    cw = jnp.concatenate(cw, axis=0)
    cwl = jnp.concatenate(cwl, axis=0)
    project(1)
    e_inv = jnp.exp(-cw)
    outs = (r * jnp.exp(cw), -kk * jnp.exp(cw - lw), kb * e_inv, k2 * e_inv, v, g, bonus)
    for ref, val in zip(out_refs[:-1], outs):
        ref[0] = val
        project(1)
    wl = jnp.exp(cwl)
    for c in range(tt // chunk):
        out_refs[-1][0, c] = wl[c * chunk:c * chunk + 1, :]
    project(len(pending))


def _head_sum_matrix(width):
    idx = jnp.arange(width)
    return ((idx[:, None] // HEAD) == (idx[None, :] // HEAD)).astype(BF16)


def ln_proj_prep(x, g, b, ws, prev, p, chunk, tt, gate_dtype):
    bsz, t, d = x.shape
    w_r, w_q, w_g = ws
    cs = w_r.shape[1]
    c_a = p['w0'].shape[-1]
    assert prev.shape[1] == (t if chunk == 1 else 1)
    ridx = jnp.arange(chunk)
    tril = jnp.tile((ridx[None, :] <= ridx[:, None]).astype(BF16), (1, 3))
    hsum = _head_sum_matrix(c_a)
    row = lambda a: a.reshape(1, -1)
    rows = lambda width: pl.BlockSpec((1, tt, width), lambda bi, i: (bi, i, 0))
    full = jax.ShapeDtypeStruct((bsz, t, c_a), F32)
    if chunk == 1:
        out_specs, out_shape = [rows(c_a)] * 8, [full] * 8
        prev_spec, zr_spec, zr_rows = rows(cs), rows(cs), t
    else:
        out_specs = [rows(c_a)] * 7 + [pl.BlockSpec((1, tt // chunk, 1, c_a), lambda bi, i: (bi, i, 0, 0))]
        out_shape = [full] * 7 + [jax.ShapeDtypeStruct((bsz, t // chunk, 1, c_a), F32)]
        prev_spec = zr_spec = pl.BlockSpec((1, 1, cs), lambda bi, i: (bi, 0, 0))
        zr_rows = 1
    consts = [row(p['mu']), row(p['w0']), p['w_up'], row(p['a0']), p['a_up'], p['g_up'], row(p['k_k']),
              row(p['k_a']), row(p['r_k']), hsum, tril]
    weight = lambda w: pl.BlockSpec(w.shape, lambda bi, i: (0, 0), pipeline_mode=pl.Buffered(1))
    outs = pl.pallas_call(
        functools.partial(_ln_proj_prep_kernel, chunk), name="ln_proj_prep",
        grid=(bsz, t // tt),
        in_specs=[rows(d), _const_spec((1, d)), _const_spec((1, d)), weight(w_r), weight(w_q), weight(w_g), prev_spec]
        + [_const_spec(c.shape) for c in consts],
        out_specs=[rows(w_q.shape[1]), rows(w_g.shape[1]), zr_spec] + out_specs,
        out_shape=[jax.ShapeDtypeStruct((bsz, t, w_q.shape[1]), F32),
                   jax.ShapeDtypeStruct((bsz, t, w_g.shape[1]), gate_dtype),
                   jax.ShapeDtypeStruct((bsz, zr_rows, cs), F32)] + out_shape,
        scratch_shapes=[pltpu.VMEM((1, cs), F32)],
        compiler_params=_params(("parallel", "arbitrary")),
    )(x, row(g), row(b), w_r, w_q, w_g, prev, *consts)
    return outs[0], outs[1], outs[2], outs[3:]


def _wkv_chunk_kernel(ra_ref, at_ref, bt_ref, kt_ref, v_ref, wl_ref, rp_ref, y0_ref, m_ref, n_ref):
    n_chunks = m_ref.shape[1]
    length = ra_ref.shape[1] // n_chunks
    heads = ra_ref.shape[2] // HEAD
    row = lax.broadcasted_iota(jnp.int32, (length, length), 0)
    col = lax.broadcasted_iota(jnp.int32, (length, length), 1)
    strict = row > col
    incl = row >= col
    hrow = lax.broadcasted_iota(jnp.int32, (HEAD, HEAD), 0)
    hcol = lax.broadcasted_iota(jnp.int32, (HEAD, HEAD), 1)
    units = [(c, h) for c in range(n_chunks) for h in range(heads)]
    us = range(len(units))
    at = lambda ref, u: ref[0, units[u][0] * length:(units[u][0] + 1) * length,
                            units[u][1] * HEAD:(units[u][1] + 1) * HEAD]
    gram = [_bdot_t(jnp.concatenate([at(at_ref, u), at(ra_ref, u)], axis=0),
                    jnp.concatenate([at(bt_ref, u), at(kt_ref, u)], axis=0)) for u in us]
    a_ab = [jnp.where(strict, g[:length, :length], 0.0) for g in gram]
    a_kk = [jnp.concatenate([jnp.where(strict, g[:length, length:], 0.0),
                             jnp.where(incl, g[length:, length:], 0.0)], axis=0) for g in gram]
    a_rb = [jnp.where(incl, g[length:, :length], 0.0) for g in gram]
    inv = [jnp.where(row == col, 1.0, a) for a in a_ab]
    pw = [_bdot(a, a) for a in a_ab]
    avy = [_bdot(a_kk[u], at(v_ref, u)) for u in us]
    w_l = [wl_ref[0, units[u][0], :, units[u][1] * HEAD:(units[u][1] + 1) * HEAD] for u in us]
    nk = [_bdot(at(v_ref, u).T, at(kt_ref, u) * w_l[u]) for u in us]
    for _ in range(int(math.log2(length)) - 2):
        both = [_bdot(jnp.concatenate([pw[u], inv[u]], axis=0), pw[u]) for u in us]
        pw = [b[:length] for b in both]
        inv = [inv[u] + both[u][length:] for u in us]
    inv = [inv[u] + _bdot(inv[u], pw[u]) for u in us]
    pq = [_bdot(inv[u], jnp.concatenate([at(at_ref, u), avy[u][:length]], axis=1)) for u in us]
    ry = [_bdot(a_rb[u], pq[u]) for u in us]
    mn = [_bdot(pq[u].T, at(bt_ref, u) * w_l[u]) for u in us]
    lanes = lambda f, c: jnp.concatenate([f(c * heads + h) for h in range(heads)], axis=1)
    rows = lambda f: jnp.concatenate([lanes(f, c) for c in range(n_chunks)], axis=0)
    rp_ref[0] = rows(lambda u: at(ra_ref, u) + ry[u][:, :HEAD])
    y0_ref[0] = rows(lambda u: avy[u][length:] + ry[u][:, HEAD:])
    for c in range(n_chunks):
        m_ref[0, c] = lanes(lambda u: jnp.where(hrow == hcol, w_l[u], 0.0) + mn[u][:HEAD], c)
        n_ref[0, c] = lanes(lambda u: nk[u] + mn[u][HEAD:], c)


def _head_norm_wide(y, hsum):
    yc = y - _mask_dot(y, hsum) * (1.0 / HEAD)
    return yc * lax.rsqrt(_mask_dot(yc * yc, hsum) * (1.0 / HEAD) + LNX_EPS)


def _wkv_scan_kernel(rp_ref, y0_ref, m_ref, n_ref, g_ref, bonus_ref, s0_ref, lg_ref, lb_ref, hsum_ref,
                     o_ref, sout_ref, s_ref):
    c = pl.program_id(1)
    nb = rp_ref.shape[0]
    heads = rp_ref.shape[2] // HEAD

    @pl.when(c == 0)
    def _():
        s_ref[...] = s0_ref[...]

    pairs = [(b, h, slice(h * HEAD, (h + 1) * HEAD)) for b in range(nb) for h in range(heads)]
    s = [s_ref[b, h] for b, h, _ in pairs]
    s_new = [n_ref[b, 0, :, sl] + _dot3(s[j], m_ref[b, 0, :, sl]) for j, (b, h, sl) in enumerate(pairs)]
    ys = [y0_ref[b, :, sl] + _bdot_t(rp_ref[b, :, sl], s[j]) for j, (b, h, sl) in enumerate(pairs)]
    for j, (b, h, _) in enumerate(pairs):
        s_ref[b, h] = s_new[j]
    length = rp_ref.shape[1]
    y = _head_norm_wide(jnp.concatenate([jnp.concatenate(ys[b * heads:(b + 1) * heads], axis=-1) for b in range(nb)],
                                        axis=0), hsum_ref[...])
    for b in range(nb):
        o = (y[b * length:(b + 1) * length] * lg_ref[...] + lb_ref[...] + bonus_ref[b]) * g_ref[b]
        o_ref[b] = o.astype(o_ref.dtype)

    @pl.when(c == pl.num_programs(1) - 1)
    def _():
        sout_ref[...] = s_ref[...]


def wkv(prep, s0, lnx_g, lnx_b, chunk):
    ra, at, bt, kt, v, g, bonus, wl = prep
    bsz, t, c_a = ra.shape
    heads = c_a // HEAD
    n_chunks = t // chunk
    assert chunk & (chunk - 1) == 0 and chunk >= 4 and t % chunk == 0
    per_step = CHUNKS_PER_STEP if n_chunks % CHUNKS_PER_STEP == 0 else 1
    tile = pl.BlockSpec((1, per_step * chunk, c_a), lambda b, c: (b, c, 0))
    mat = pl.BlockSpec((1, per_step, HEAD, c_a), lambda b, c: (b, c, 0, 0))
    full = jax.ShapeDtypeStruct((bsz, t, c_a), F32)
    mats = jax.ShapeDtypeStruct((bsz, n_chunks, HEAD, c_a), F32)
    rp, y0, m, n = pl.pallas_call(
        _wkv_chunk_kernel, name="wkv_chunk",
        grid=(bsz, n_chunks // per_step),
        in_specs=[tile] * 5 + [pl.BlockSpec((1, per_step, 1, c_a), lambda b, c: (b, c, 0, 0))],
        out_specs=[tile, tile, mat, mat],
        out_shape=[full, full, mats, mats],
        compiler_params=_params(("parallel", "parallel")),
    )(ra, at, bt, kt, v, wl)
    nb = SCAN_BATCH if bsz % SCAN_BATCH == 0 else 1
    tile = pl.BlockSpec((nb, chunk, c_a), lambda b, c: (b, c, 0))
    mat = pl.BlockSpec((nb, 1, HEAD, c_a), lambda b, c: (b, c, 0, 0))
    st = pl.BlockSpec((nb, heads, HEAD, HEAD), lambda b, c: (b, 0, 0, 0))
    return pl.pallas_call(
        _wkv_scan_kernel, name="wkv_scan",
        grid=(bsz // nb, n_chunks),
        in_specs=[tile, tile, mat, mat, tile, tile, st, _const_spec((1, c_a)), _const_spec((1, c_a)),
                  _const_spec((c_a, c_a))],
        out_specs=[tile, st],
        out_shape=[jax.ShapeDtypeStruct((bsz, t, c_a), BF16), jax.ShapeDtypeStruct(s0.shape, F32)],
        scratch_shapes=[pltpu.VMEM((nb, heads, HEAD, HEAD), F32)],
        compiler_params=_params(("parallel", "arbitrary")),
    )(rp, y0, m, n, g, bonus, s0, lnx_g.reshape(1, c_a), lnx_b.reshape(1, c_a), _head_sum_matrix(c_a))


def _wkv_step_kernel(r_ref, w_ref, k_ref, v_ref, a_ref, b_ref, g_ref, bonus_ref, s_ref, lg_ref, lb_ref, hsum_ref,
                     o_ref, sout_ref):
    bt, heads = s_ref.shape[0], s_ref.shape[1]
    c_a = heads * HEAD
    hsum = hsum_ref[...]
    diag = (lax.broadcasted_iota(jnp.int32, (HEAD, c_a), 1) % HEAD
            == lax.broadcasted_iota(jnp.int32, (HEAD, c_a), 0))
    seqs = range(bt)
    row = lambda ref, i: ref[i:i + 1, :]
    stack = lambda f: jnp.concatenate([f(i) for i in seqs], axis=0)
    piece = lambda x, i: x[i * HEAD:(i + 1) * HEAD]
    s = stack(lambda i: jnp.concatenate([s_ref[i, h] for h in range(heads)], axis=1))
    sa = _mask_dot(s * stack(lambda i: jnp.broadcast_to(row(a_ref, i), (HEAD, c_a))), hsum)
    v_rows = _mask_dot(stack(lambda i: jnp.where(diag, row(v_ref, i), 0.0)), hsum)
    s = stack(lambda i: piece(s, i) * row(w_ref, i) + piece(sa, i) * row(b_ref, i) + piece(v_rows, i) * row(k_ref, i))
    for i in seqs:
        for h in range(heads):
            sout_ref[i, h] = piece(s, i)[:, h * HEAD:(h + 1) * HEAD]
    y_rows = _mask_dot(stack(lambda i: piece(s, i) * row(r_ref, i)), hsum)
    y = stack(lambda i: jnp.sum(jnp.where(diag, piece(y_rows, i), 0.0), axis=0, keepdims=True))
    y = _head_norm_wide(y, hsum)
    o_ref[...] = (y * lg_ref[...] + lb_ref[...] + bonus_ref[...]) * g_ref[...]


def wkv_step(ops, s0, lnx_g, lnx_b, bt):
    bsz, c_a = ops[0].shape
    rows = pl.BlockSpec((bt, c_a), lambda i: (i, 0))
    st = pl.BlockSpec((bt,) + s0.shape[1:], lambda i: (i, 0, 0, 0))
    hsum = _head_sum_matrix(c_a)
    return pl.pallas_call(
        _wkv_step_kernel, name="wkv_step",
        grid=(bsz // bt,),
        in_specs=[rows] * 8 + [st, _const_spec((1, c_a)), _const_spec((1, c_a)), _const_spec(hsum.shape)],
        out_specs=[rows, st],
        out_shape=[jax.ShapeDtypeStruct((bsz, c_a), F32), jax.ShapeDtypeStruct(s0.shape, F32)],
        compiler_params=_params(("parallel",)),
    )(*ops, s0, lnx_g.reshape(1, c_a), lnx_b.reshape(1, c_a), hsum)


def rope_tables(pos):
    inv_freq = ROPE_THETA ** (-jnp.arange(ROT_HALF, dtype=F32) / ROT_HALF)
    ang = pos[:, None] * inv_freq[None, :]
    cos, sin = jnp.cos(ang), jnp.sin(ang)
    t = pos.shape[0]
    rest = HEAD - 2 * ROT_HALF
    c = jnp.concatenate([cos, cos, jnp.ones((t, rest), F32)], axis=1)
    s1 = jnp.concatenate([jnp.zeros((t, ROT_HALF), F32), sin, jnp.zeros((t, rest), F32)], axis=1)
    s2 = jnp.concatenate([-sin, jnp.zeros((t, HEAD - ROT_HALF), F32)], axis=1)
    rep = LANES // HEAD
    return tuple(jnp.tile(x, (1, rep)) for x in (c, s1, s2))


def _rope(x, c, s1, s2):
    width = x.shape[-1]
    rep = width // c.shape[-1]
    if rep > 1:
        c, s1, s2 = (jnp.concatenate([t] * rep, axis=1) for t in (c, s1, s2))
    return x * c + pltpu.roll(x, ROT_HALF, 1) * s1 + pltpu.roll(x, width - ROT_HALF, 1) * s2


def _attn_kernel(window, q_w, kv_w, zq_ref, kp_ref, vp_ref, rc_ref, rs1_ref, rs2_ref, pc_ref, ps1_ref, ps2_ref,
                 mk_ref, mv_ref, sink_ref, ob_ref, om_ref, kr_ref):
    zq = zq_ref[0]
    nblk = zq.shape[0] // window
    first = pl.program_id(1) * nblk
    scale = HEAD ** -0.5
    q = zq[:, :q_w]
    k = zq[:, q_w:q_w + kv_w]
    v = zq[:, q_w + kv_w:q_w + 2 * kv_w]
    qm = zq[:, q_w + 2 * kv_w:]
    qr = _rope(q, rc_ref[...], rs1_ref[...], rs2_ref[...]) * scale
    kr = _rope(k, rc_ref[...], rs1_ref[...], rs2_ref[...])
    kr_ref[0] = kr
    k_all = jnp.concatenate([_rope(kp_ref[0], pc_ref[...], ps1_ref[...], ps2_ref[...]), kr], axis=0)
    v_all = jnp.concatenate([vp_ref[0], v], axis=0)
    gqa = q_w // kv_w
    assert window & (window - 1) == 0
    qi = lax.broadcasted_iota(jnp.int32, (gqa * window, 2 * window), 0) & (window - 1)
    kj = lax.broadcasted_iota(jnp.int32, (gqa * window, 2 * window), 1)
    band = (kj > qi) & (kj <= qi + window)
    tdot = lambda a, b: lax.dot_general(a, b, (((1,), (1,)), ((), ())), preferred_element_type=F32)
    dot = lambda a, b: jnp.dot(a, b, preferred_element_type=F32)
    hsl = lambda h: slice(h * HEAD, (h + 1) * HEAD)
    rows = lambda i: slice(i * window, (i + 1) * window)
    keys = lambda i: slice(i * window, (i + 2) * window)
    wins = [(i, hk) for i in range(nblk) for hk in range(kv_w // HEAD)]
    mems = [(i, h) for i in range(nblk) for h in range(qm.shape[1] // HEAD)]
    qmb = (qm * scale).astype(BF16)
    mk = mk_ref[0].astype(BF16)
    mv = mv_ref[0].astype(BF16)
    kcat = [k_all[keys(i), hsl(hk)].astype(BF16) for i, hk in wins]
    vcat = [v_all[keys(i), hsl(hk)].astype(BF16) for i, hk in wins]
    qs = [jnp.concatenate([qr[rows(i), hsl(hk * gqa + g)] for g in range(gqa)], axis=0).astype(BF16)
          for i, hk in wins]
    s = [jnp.where(band & ((kj >= window) | (first + i > 0)), tdot(qs[u], kcat[u]), NEG_INF)
         for u, (i, hk) in enumerate(wins)]
    sm = [tdot(qmb[rows(i), hsl(h)], mk[:, hsl(h)]) for i, h in mems]
    sink = [jnp.concatenate([jnp.full((window, 1), sink_ref[hk * gqa + g], F32) for g in range(gqa)], axis=0)
            for i, hk in wins]
    m = [jnp.maximum(jnp.max(s[u], axis=-1, keepdims=True), sink[u]) for u in range(len(wins))]
    p = [jnp.exp(s[u] - m[u]) for u in range(len(wins))]
    pm = [jnp.exp(x - jnp.max(x, axis=-1, keepdims=True)) for x in sm]
    o = [dot(p[u].astype(BF16), vcat[u])
         / (jnp.sum(p[u], axis=-1, keepdims=True) + jnp.exp(sink[u] - m[u])) for u in range(len(wins))]
    om = [dot(pm[u].astype(BF16), mv[:, hsl(h)]) / jnp.sum(pm[u], axis=-1, keepdims=True)
          for u, (i, h) in enumerate(mems)]
    n_kv, n_mh = kv_w // HEAD, qm.shape[1] // HEAD
    ob = jnp.concatenate([jnp.concatenate([o[i * n_kv + hk][g * window:(g + 1) * window]
                                           for hk in range(n_kv) for g in range(gqa)], axis=1)
                          for i in range(nblk)], axis=0)
    ob_ref[0] = ob.astype(ob_ref.dtype)
    om_ref[0] = jnp.concatenate([jnp.concatenate(om[i * n_mh:(i + 1) * n_mh], axis=1) for i in range(nblk)],
                                axis=0).astype(om_ref.dtype)


def attn_prompt(zq, mk, mv, sinks, tables, window, q_w, kv_w):
    bsz, t, zw = zq.shape
    mem_w = zw - q_w - 2 * kv_w
    assert kv_w == LANES and q_w % kv_w == 0
    kcol, vcol = q_w // kv_w, q_w // kv_w + 1
    nblk = ATTN_BLOCKS if (t // window) % ATTN_BLOCKS == 0 else 1
    rows = nblk * window
    prev = lambda n: jnp.maximum(n * nblk - 1, 0)
    tab = pl.BlockSpec((rows, LANES), lambda b, n: (n, 0))
    ptab = pl.BlockSpec((window, LANES), lambda b, n: (prev(n), 0))
    mem = pl.BlockSpec((1,) + mk.shape[1:], lambda b, n: (b, 0, 0))
    return pl.pallas_call(
        functools.partial(_attn_kernel, window, q_w, kv_w), name="attn",
        grid=(bsz, t // rows),
        in_specs=[pl.BlockSpec((1, rows, zw), lambda b, n: (b, n, 0)),
                  pl.BlockSpec((1, window, kv_w), lambda b, n: (b, prev(n), kcol)),
                  pl.BlockSpec((1, window, kv_w), lambda b, n: (b, prev(n), vcol)),
                  tab, tab, tab, ptab, ptab, ptab, mem, mem,
                  pl.BlockSpec(memory_space=pltpu.SMEM)],
        out_specs=[pl.BlockSpec((1, rows, q_w), lambda b, n: (b, n, 0)),
                   pl.BlockSpec((1, rows, mem_w), lambda b, n: (b, n, 0)),
                   pl.BlockSpec((1, rows, kv_w), lambda b, n: (b, n, 0))],
        out_shape=[jax.ShapeDtypeStruct((bsz, t, q_w), BF16), jax.ShapeDtypeStruct((bsz, t, mem_w), BF16),
                   jax.ShapeDtypeStruct((bsz, t, kv_w), F32)],
        compiler_params=_params(("parallel", "parallel")),
    )(zq, zq, zq, *tables, *tables, mk, mv, sinks)


def _attn_step_kernel(q_w, kv_w, zq_ref, ck_ref, cv_ref, mk_ref, mv_ref, rc_ref, rs1_ref, rs2_ref, sink_ref,
                      ob_ref, om_ref, nk_ref, nv_ref):
    bt = zq_ref.shape[0]
    window = ck_ref.shape[1]
    mem_w = om_ref.shape[1]
    n_q, gqa, per_vreg = q_w // HEAD, q_w // kv_w, LANES // HEAD
    scale = HEAD ** -0.5
    zq = zq_ref[...]
    q = _rope(zq[:, :q_w], rc_ref[...], rs1_ref[...], rs2_ref[...]) * scale
    k_new = _rope(zq[:, q_w:q_w + kv_w], rc_ref[...], rs1_ref[...], rs2_ref[...])
    v_new = zq[:, q_w + kv_w:q_w + 2 * kv_w]
    qm = zq[:, q_w + 2 * kv_w:] * scale
    own = lambda w: (lax.broadcasted_iota(jnp.int32, (n_q, w), 1) // HEAD
                     == lax.broadcasted_iota(jnp.int32, (n_q, w), 0))
    own_q, own_m = own(q_w), own(mem_w)
    hrow = lax.broadcasted_iota(jnp.int32, (n_q, LANES), 0)
    hblk = lax.broadcasted_iota(jnp.int32, (n_q, LANES), 1) // HEAD
    swap = (hrow % per_vreg) != (hrow // gqa)
    keep = hblk == hrow % per_vreg
    key_ok = lax.broadcasted_iota(jnp.int32, (n_q, window), 1) >= 1
    wrow = lax.broadcasted_iota(jnp.int32, (window, kv_w), 0)
    sink = sink_ref[...]
    bs = range(bt)
    tdot = lambda a, b: lax.dot_general(a.astype(BF16), b.astype(BF16), (((1,), (1,)), ((), ())),
                                        preferred_element_type=F32)
    for b in bs:
        nk_ref[b] = jnp.where(wrow == window - 1, k_new[b:b + 1], pltpu.roll(ck_ref[b], window - 1, 0))
        nv_ref[b] = jnp.where(wrow == window - 1, v_new[b:b + 1], pltpu.roll(cv_ref[b], window - 1, 0))
    q8 = []
    for b in bs:
        rep = jnp.where(own_q, q[b:b + 1], 0.0)
        fold = sum(rep[:, c * LANES:(c + 1) * LANES] for c in range(q_w // LANES))
        q8.append(jnp.where(swap, pltpu.roll(fold, HEAD, 1), fold))
    qm8 = [jnp.where(own_m, qm[b:b + 1], 0.0) for b in bs]
    s = [jnp.where(key_ok, tdot(q8[b], ck_ref[b]), NEG_INF) for b in bs]
    sm = [tdot(qm8[b], mk_ref[b]) for b in bs]
    s_new = [jnp.sum(q8[b] * k_new[b:b + 1], axis=1, keepdims=True) for b in bs]
    m = [jnp.maximum(jnp.maximum(jnp.max(s[b], axis=1, keepdims=True), s_new[b]), sink) for b in bs]
    p = [jnp.exp(s[b] - m[b]) for b in bs]
    p_new = [jnp.exp(s_new[b] - m[b]) for b in bs]
    pm = [jnp.exp(sm[b] - jnp.max(sm[b], axis=1, keepdims=True)) for b in bs]
    o8 = [(_bdot(p[b], cv_ref[b]) + p_new[b] * v_new[b:b + 1])
          / (jnp.sum(p[b], axis=1, keepdims=True) + p_new[b] + jnp.exp(sink - m[b])) for b in bs]
    om8 = [_bdot(pm[b], mv_ref[b]) / jnp.sum(pm[b], axis=1, keepdims=True) for b in bs]
    ob, om = [], []
    for b in bs:
        o = jnp.where(swap, pltpu.roll(o8[b], HEAD, 1), o8[b])
        o = jnp.concatenate([jnp.where(keep, o, 0.0)] * (q_w // LANES), axis=1)
        ob.append(jnp.sum(jnp.where(own_q, o, 0.0), axis=0, keepdims=True))
        om.append(jnp.sum(jnp.where(own_m, om8[b], 0.0), axis=0, keepdims=True))
    ob_ref[...] = jnp.concatenate(ob, axis=0)
    om_ref[...] = jnp.concatenate(om, axis=0)


def attn_step(zq, ck, cv, mk, mv, sinks, tables, q_w, kv_w, bt):
    bsz, zw = zq.shape
    mem_w = zw - q_w - 2 * kv_w
    window, n_mem = ck.shape[1], mk.shape[1]
    n_q = q_w // HEAD
    assert kv_w == LANES and mem_w // HEAD <= n_q and n_q == 8
    rows = lambda w: pl.BlockSpec((bt, w), lambda i: (i, 0))
    cache = pl.BlockSpec((bt, window, kv_w), lambda i: (i, 0, 0))
    mem = pl.BlockSpec((bt, n_mem, mem_w), lambda i: (i, 0, 0))
    consts = list(tables) + [sinks.reshape(n_q, 1)]
    const_specs = [_const_spec(c.shape) for c in consts]
    return pl.pallas_call(
        functools.partial(_attn_step_kernel, q_w, kv_w), name="attn_step",
        grid=(bsz // bt,),
        in_specs=[rows(zw), cache, cache, mem, mem] + const_specs,
        out_specs=[rows(q_w), rows(mem_w), cache, cache],
        out_shape=[jax.ShapeDtypeStruct((bsz, q_w), F32), jax.ShapeDtypeStruct((bsz, mem_w), F32),
                   jax.ShapeDtypeStruct(ck.shape, F32), jax.ShapeDtypeStruct(cv.shape, F32)],
        compiler_params=_params(("parallel",)),
    )(zq, ck, cv, mk, mv, *consts)


def _proj_kernel(x_ref, w_ref, o_ref):
    o_ref[...] = jnp.dot(x_ref[...].astype(BF16), w_ref[...], preferred_element_type=F32)


def proj(x, w, tm):
    n, d = x.shape
    return pl.pallas_call(
        _proj_kernel, name="proj",
        grid=(n // tm,),
        in_specs=[pl.BlockSpec((tm, d), lambda i: (i, 0)), _const_spec(w.shape)],
        out_specs=pl.BlockSpec((tm, w.shape[1]), lambda i: (i, 0)),
        out_shape=jax.ShapeDtypeStruct((n, w.shape[1]), F32),
        compiler_params=_params(("parallel",)),
    )(x, w)


def _merge_kernel(alpha, n_tiles, x_ref, oa_ref, ob_ref, om_ref, gt_ref, lig_ref, lib_ref, pa_ref, pb_ref, pm_ref,
                  wo_ref, l1g_ref, l1b_ref, wr_ref, br_ref, *refs):
    outs = refs[-4:]

    @pl.when(pl.program_id(0) >= n_tiles)
    def _():
        for ref in outs:
            ref[...] = jnp.zeros_like(ref)

    @pl.when(pl.program_id(0) < n_tiles)
    def _():
        _merge_tile(alpha, x_ref, oa_ref, ob_ref, om_ref, gt_ref, lig_ref, lib_ref, pa_ref, pb_ref, pm_ref, wo_ref,
                    l1g_ref, l1b_ref, wr_ref, br_ref, *outs)


def _merge_tile(alpha, x_ref, oa_ref, ob_ref, om_ref, gt_ref, lig_ref, lib_ref, pa_ref, pb_ref, pm_ref, wo_ref,
                l1g_ref, l1b_ref, wr_ref, br_ref, x1_ref, x1t_ref, eidx_ref, gate_ref):
    d = x_ref.shape[1]
    xn = _ln(x_ref[...], lig_ref[...], lib_ref[...])
    gts = _sigmoid(gt_ref[...].astype(F32))
    merged = (gts[:, :d] * _bdot(oa_ref[...], pa_ref[...]) + gts[:, d:2 * d] * _bdot(ob_ref[...], pb_ref[...])
              + gts[:, 2 * d:] * _bdot(om_ref[...], pm_ref[...]))
    x1 = _ln(alpha * xn + _bdot(merged, wo_ref[...]), l1g_ref[...], l1b_ref[...])
    x1_ref[...] = x1
    _rows_to_tiles(x1t_ref, x1)
    x_hi, x_lo = _pieces(x1, 2)
    w_hi, w_lo = wr_ref[0], wr_ref[1]
    dot = lambda a, b: jnp.dot(a, b, preferred_element_type=F32)
    logits = dot(x_hi, w_hi) + dot(x_hi, w_lo) + dot(x_lo, w_hi) + br_ref[...]
    lane = lax.broadcasted_iota(jnp.int32, logits.shape, 1)
    lane_f = lane.astype(F32)
    first = lambda hit: jnp.min(jnp.where(hit, lane_f, float(LANES)), axis=-1, keepdims=True).astype(jnp.int32)
    gmask = lane < N_GROUPS
    gl = jnp.where(gmask, logits, NEG_INF)
    gmax = jnp.max(gl, axis=-1, keepdims=True)
    gidx = first(gl == gmax)
    g_w = 1.0 / jnp.sum(jnp.where(gmask, jnp.exp(gl - gmax), 0.0), axis=-1, keepdims=True)
    lo = N_GROUPS + gidx * EXPERTS_PER_GROUP
    el = jnp.where((lane >= lo) & (lane < lo + EXPERTS_PER_GROUP), logits, NEG_INF)
    v1 = jnp.max(el, axis=-1, keepdims=True)
    i1 = first(el == v1)
    el2 = jnp.where(lane == i1, NEG_INF, el)
    v2 = jnp.max(el2, axis=-1, keepdims=True)
    i2 = first(el2 == v2)
    e2 = jnp.exp(v2 - v1)
    gate1 = g_w / (1.0 + e2)
    eidx = jnp.where(lane == 0, (i1 - N_GROUPS).astype(F32), jnp.where(lane == 1, (i2 - N_GROUPS).astype(F32), 0.0))
    eidx_ref[...] = eidx.T[:eidx_ref.shape[0]].astype(jnp.int32)
    gate_ref[...] = jnp.where(lane == 0, gate1, jnp.where(lane == 1, gate1 * e2, 0.0))


def merge(x, oa, ob, om, gt, w, tm, alpha, n_total, row_offset=0, into=None):
    n, d = x.shape
    assert row_offset % tm == 0 and n % tm == 0 and n_total % tm == 0
    off = row_offset // tm
    into = list(into or [])
    n_tiles = n // tm
    steps = n_tiles if into else n_total // tm
    rows = lambda a: pl.BlockSpec((tm, a.shape[1]), lambda i: (jnp.minimum(i, n_tiles - 1), 0))
    out = lambda width: pl.BlockSpec((tm, width), lambda i: (i + off, 0))
    consts = [w['ln_in_g'], w['ln_in_b'], w['p_a'], w['p_b'], w['p_m'], w['w_o'], w['ln1_g'], w['ln1_b'],
              w['w_route'], w['b_route']]
    n_in = 5 + len(consts)
    return pl.pallas_call(
        functools.partial(_merge_kernel, alpha, n_tiles), name="merge",
        grid=(steps,),
        in_specs=[rows(a) for a in (x, oa, ob, om, gt)] + [_const_spec(c.shape) for c in consts]
        + [pl.BlockSpec(memory_space=pl.ANY)] * len(into),
        out_specs=[out(d), pl.BlockSpec((tm * TILE_ROWS, LANES), lambda i: (i + off, 0)),
                   pl.BlockSpec((TILE_ROWS, tm), lambda i: (0, i + off)), out(LANES)],
        out_shape=[jax.ShapeDtypeStruct((n_total, d), F32), jax.ShapeDtypeStruct((n_total * TILE_ROWS, LANES), F32),
                   jax.ShapeDtypeStruct((TILE_ROWS, n_total), jnp.int32), jax.ShapeDtypeStruct((n_total, LANES), F32)],
        input_output_aliases={n_in + k: k for k in range(len(into))},
        compiler_params=_params(("parallel",)),
    )(x, oa, ob, om, gt, *consts, *into)


ROW_DMA_UNROLL = 8
MOE_BUFFERS = 3
DRAIN_STEPS = 2


TILE_ROWS = 8


def _rows_from_tiles(ref, n):
    return jnp.concatenate([ref[pl.ds(s, n, stride=TILE_ROWS), :] for s in range(TILE_ROWS)], axis=1)


def _rows_to_tiles(ref, x):
    for s in range(TILE_ROWS):
        ref[pl.ds(s, x.shape[0], stride=TILE_ROWS), :] = x[:, s * LANES:(s + 1) * LANES]


def _row_copies(asg_ref, base, count, n_asg, x_hbm, buf, y_hbm, sem, gather, unrolled):
    def tile(ref, idx):
        start = idx * TILE_ROWS
        return ref.at[pl.ds(start if isinstance(idx, int) else pl.multiple_of(start, TILE_ROWS), TILE_ROWS)]

    def one(r, priority):
        a = asg_ref[base + r]
        if gather:
            tok = jnp.minimum(a, n_asg - 1)
            tok = jnp.where(tok >= n_asg // 2, tok - n_asg // 2, tok)
            copy = pltpu.make_async_copy(tile(x_hbm, tok), tile(buf, r), sem)
        else:
            copy = pltpu.make_async_copy(tile(buf, r), tile(y_hbm, a), sem)
        copy.start(priority=priority)

    if unrolled:
        for r in range(count):
            one(r, r % 2)
        return

    def body(g, carry):
        for j in range(ROW_DMA_UNROLL):
            one(g * ROW_DMA_UNROLL + j, j % 2)
        return carry
    lax.fori_loop(0, count // ROW_DMA_UNROLL, body, 0)


def _moe_expert_kernel(n_tokens, n_asg, asg_ref, be_ref, nu_ref, x_hbm, wg_ref, wu_ref, wd_ref, y_hbm,
                       xbuf, ybuf, wgb, wub, wdb, gsem, ssem):
    i = pl.program_id(0)
    used = nu_ref[0]
    rows = xbuf.shape[1] // TILE_ROWS
    n_buf = xbuf.shape[0]
    n_blocks = be_ref.shape[0]
    blk = jnp.minimum(i, n_blocks - 1)

    @pl.when((i == 0) | (be_ref[blk] != be_ref[jnp.maximum(blk - 1, 0)]))
    def _():
        wgb[...] = wg_ref[0, 0].astype(BF16)
        wub[...] = wu_ref[0, 0].astype(BF16)
        wdb[...] = wd_ref[0, 0].astype(BF16)

    def wait_gather(slot):
        pltpu.make_async_copy(x_hbm.at[pl.ds(0, rows * TILE_ROWS)], xbuf.at[slot], gsem.at[slot]).wait()

    def wait_scatter(slot):
        pltpu.make_async_copy(ybuf.at[slot], y_hbm.at[pl.ds(0, rows * TILE_ROWS)], ssem.at[slot]).wait()

    def gather(b, slot, unrolled):
        _row_copies(asg_ref, b * rows, rows, n_asg, x_hbm, xbuf.at[slot], y_hbm, gsem.at[slot], True, unrolled)

    def scatter(b, slot, unrolled):
        _row_copies(asg_ref, b * rows, rows, n_asg, x_hbm, ybuf.at[slot], y_hbm, ssem.at[slot], False, unrolled)

    def expert(slot):
        xb = _rows_from_tiles(xbuf.at[slot], rows).astype(BF16)
        hg = jnp.dot(xb, wgb[...], preferred_element_type=F32)
        hu = jnp.dot(xb, wub[...], preferred_element_type=F32)
        h = hg * _sigmoid(hg) * hu
        _rows_to_tiles(ybuf.at[slot], jnp.dot(h.astype(BF16), wdb[...], preferred_element_type=F32))

    @pl.when(i == 0)
    def _():
        ybuf[1] = jnp.zeros(ybuf.shape[1:], F32)
        stride = n_asg // 2
        gaps = [(k * stride + n_tokens, stride - n_tokens) for k in range(2)]
        gaps.append((n_asg, y_hbm.shape[0] // TILE_ROWS - n_asg))
        fills = [pltpu.make_async_copy(ybuf.at[1, pl.ds(0, min(rows, count - j) * TILE_ROWS)],
                                       y_hbm.at[pl.ds((start + j) * TILE_ROWS, min(rows, count - j) * TILE_ROWS)],
                                       ssem.at[1])
                 for start, count in gaps for j in range(0, count, rows)]
        for copy in fills:
            copy.start()
        for copy in fills:
            copy.wait()

    @pl.when((i >= 2) & (i - 2 < used))
    def _():
        wait_scatter((i - 2) % n_buf)

    steady = (i >= 1) & (i + 2 < used)
    for slot in range(n_buf):
        @pl.when(steady & (i % n_buf == slot))
        def _():
            wait_gather(slot)
            gather(i + 2, (slot + 2) % n_buf, True)
            scatter(i - 1, (slot - 1) % n_buf, True)
            expert(slot)

    @pl.when(jnp.logical_not(steady))
    def _():
        slot = i % n_buf

        @pl.when(i == 0)
        def _():
            for b in range(2):
                @pl.when(b < used)
                def _():
                    gather(b, b, False)

        @pl.when(i < used)
        def _():
            wait_gather(slot)

        @pl.when(i + 2 < used)
        def _():
            gather(i + 2, (i + 2) % n_buf, False)

        @pl.when((i >= 1) & (i - 1 < used))
        def _():
            scatter(i - 1, (i - 1) % n_buf, False)

        @pl.when(i < used)
        def _():
            expert(slot)


def moe_experts(x1_tiles, n_tokens, stride, asg, blk_e, n_used, e_gate, e_up, e_down):
    d = e_gate.shape[2]
    assert d == TILE_ROWS * LANES and x1_tiles.shape == (stride * TILE_ROWS, LANES)
    n_blocks = blk_e.shape[0]
    ff = e_gate.shape[-1]
    n_asg = 2 * stride
    n_rows = n_asg + e_gate.shape[1] * EXPERT_BLOCK
    weight = lambda shape: pl.BlockSpec(
        (1, 1) + shape, lambda i, asg, be, nu: (0, be[jnp.minimum(i, n_blocks - 1)], 0, 0))
    return pl.pallas_call(
        functools.partial(_moe_expert_kernel, n_tokens, n_asg), name="moe_expert",
        grid_spec=pltpu.PrefetchScalarGridSpec(
            num_scalar_prefetch=3,
            grid=(n_blocks + DRAIN_STEPS,),
            in_specs=[pl.BlockSpec(memory_space=pl.ANY), weight((d, ff)), weight((d, ff)), weight((ff, d))],
            out_specs=pl.BlockSpec(memory_space=pl.ANY),
            scratch_shapes=[pltpu.VMEM((MOE_BUFFERS, EXPERT_BLOCK * TILE_ROWS, LANES), F32),
                            pltpu.VMEM((MOE_BUFFERS, EXPERT_BLOCK * TILE_ROWS, LANES), F32),
                            pltpu.VMEM((d, ff), BF16), pltpu.VMEM((d, ff), BF16), pltpu.VMEM((ff, d), BF16),
                            pltpu.SemaphoreType.DMA((MOE_BUFFERS,)), pltpu.SemaphoreType.DMA((MOE_BUFFERS,))]),
        out_shape=jax.ShapeDtypeStruct((n_rows * TILE_ROWS, LANES), F32),
        compiler_params=_params(("arbitrary",)),
    )(asg, blk_e, n_used, x1_tiles, e_gate, e_up, e_down)


def _moe_combine_kernel(alpha, lead_tiles, y0_ref, y1_ref, x1_ref, gate_ref, g_ref, b_ref, lead_ref, tail_ref):
    i = pl.program_id(0)
    gate = gate_ref[...]
    tm = x1_ref.shape[0]
    moe = gate[:, 0:1] * _rows_from_tiles(y0_ref, tm) + gate[:, 1:2] * _rows_from_tiles(y1_ref, tm)
    out = _ln(alpha * x1_ref[...] + moe, g_ref[...], b_ref[...])

    @pl.when(i < lead_tiles)
    def _():
        lead_ref[...] = out

    @pl.when(i >= lead_tiles)
    def _():
        tail_ref[...] = out[:tail_ref.shape[0]]


def moe_combine(y, x1, gate, g, b, tm, alpha, n, n_lead):
    stride, d = x1.shape
    assert n_lead % tm == 0 and stride - n_lead == tm and n_lead < n <= stride
    lead_tiles = n_lead // tm
    return pl.pallas_call(
        functools.partial(_moe_combine_kernel, alpha, lead_tiles), name="moe_combine",
        grid=(stride // tm,),
        in_specs=[pl.BlockSpec((tm * TILE_ROWS, LANES), lambda i: (i, 0)),
                  pl.BlockSpec((tm * TILE_ROWS, LANES), lambda i: (i + stride // tm, 0)),
                  pl.BlockSpec((tm, d), lambda i: (i, 0)),
                  pl.BlockSpec((tm, LANES), lambda i: (i, 0)), _const_spec((1, d)), _const_spec((1, d))],
        out_specs=[pl.BlockSpec((tm, d), lambda i: (jnp.minimum(i, lead_tiles - 1), 0)),
                   pl.BlockSpec((n - n_lead, d), lambda i: (0, 0))],
        out_shape=[jax.ShapeDtypeStruct((n_lead, d), F32), jax.ShapeDtypeStruct((n - n_lead, d), F32)],
        compiler_params=_params(("arbitrary",)),
    )(y, y, x1, gate, g.reshape(1, d), b.reshape(1, d))


def moe_routing(experts, stride):
    top_k, n = experts.shape
    n_exp = N_GROUPS * EXPERTS_PER_GROUP
    a = n * top_k
    flat_e = experts.reshape(a)
    order = jnp.argsort(flat_e, stable=True).astype(jnp.int32)
    order = order + (order // n) * (stride - n)
    counts = jnp.sum((flat_e[:, None] == jnp.arange(n_exp, dtype=jnp.int32)[None, :]).astype(jnp.int32), axis=0)
    ends = jnp.cumsum(counts)
    padded = (counts + EXPERT_BLOCK - 1) // EXPERT_BLOCK * EXPERT_BLOCK
    pad_end = jnp.cumsum(padded)
    n_blocks = -(-a // EXPERT_BLOCK) + n_exp
    blk_start = jnp.arange(n_blocks, dtype=jnp.int32) * EXPERT_BLOCK
    blk_e = jnp.minimum(jnp.sum((pad_end[None, :] <= blk_start[:, None]).astype(jnp.int32), axis=1), n_exp - 1)
    per_block = lambda table: table[blk_e][:, None]
    slot = blk_start[:, None] + jnp.arange(EXPERT_BLOCK, dtype=jnp.int32)[None, :]
    rank = slot - per_block(pad_end - padded)
    spare = top_k * stride + jnp.clip(slot - per_block(ends), 0, n_exp * EXPERT_BLOCK - 1)
    asg = jnp.where(rank < per_block(counts), order[jnp.clip(per_block(ends - counts) + rank, 0, a - 1)], spare)
    n_used = (pad_end[-1:] // EXPERT_BLOCK).astype(jnp.int32)
    return asg.reshape(-1).astype(jnp.int32), blk_e.astype(jnp.int32), n_used


def hier_moe_ln(x1, x1_tiles, eidx, gate, w, tm, alpha, n_tokens, n_lead):
    stride = x1.shape[0]
    asg, blk_e, n_used = moe_routing(eidx[:2, :n_tokens], stride)
    y = moe_experts(x1_tiles, n_tokens, stride, asg, blk_e, n_used, w['e_gate'], w['e_up'], w['e_down'])
    return moe_combine(y, x1, gate, w['ln2_g'], w['ln2_b'], tm, alpha, n_tokens, n_lead)


def kernel(x_prompt, x_sample, mem_prompt, state_wkv, state_shift, cache_win_k, cache_win_v, cache_mem_k, cache_mem_v, ln_in_g, ln_in_b, w_in, mu, w0, w_up, a0, a_up, g_up, k_k, k_a, r_k, lnx_g, lnx_b, sinks, w_mem_kv, p_a, p_b, p_m, w_o, ln1_g, ln1_b, w_group, b_group, w_router, b_router, e_gate, e_up, e_down, ln2_g, ln2_b):
    depth = w_in.shape[0]
    assert depth == 1, "single-layer step"
    bsz, seq, d = x_prompt.shape
    dec = x_sample.shape[0]
    assert x_sample.shape[1] == 1
    c_shift = mu.shape[-1]
    c_a = w0.shape[-1]
    window, kv_w = cache_win_k.shape[2], cache_win_k.shape[3] * cache_win_k.shape[4]
    n_mem, mem_w = cache_mem_k.shape[2], cache_mem_k.shape[3] * cache_mem_k.shape[4]
    q_w = sinks.shape[-1] * HEAD
    qkvm_w = q_w + 2 * kv_w + mem_w
    alpha = (2.0 * depth) ** 0.25
    past_len = float(PAST_LEN)
    chunk = WKV_CHUNK

    w_in_b = w_in[0].astype(BF16)
    w_parts = [w_in_b[:, :c_shift], w_in_b[:, c_shift:c_shift + qkvm_w], w_in_b[:, c_shift + qkvm_w:]]
    rp = dict(mu=mu[0], w0=w0[0], w_up=w_up[0], a0=a0[0], a_up=a_up[0], g_up=g_up[0], k_k=k_k[0], k_a=k_a[0],
              r_k=r_k[0].reshape(-1))
    n_route = N_GROUPS * (1 + EXPERTS_PER_GROUP)
    mw = dict(ln_in_g=ln_in_g.reshape(1, d), ln_in_b=ln_in_b.reshape(1, d), p_a=p_a[0].astype(BF16),
              p_b=p_b[0].astype(BF16), p_m=p_m[0].astype(BF16), w_o=w_o[0].astype(BF16),
              ln1_g=ln1_g[0].reshape(1, d), ln1_b=ln1_b[0].reshape(1, d),
              w_route=jnp.stack(_pieces(jnp.pad(jnp.concatenate([w_group[0], w_router[0]], axis=1),
                                                ((0, 0), (0, LANES - n_route))), 2)),
              b_route=jnp.pad(jnp.concatenate([b_group[0], b_router[0]]), (0, LANES - n_route)).reshape(1, LANES),
              e_gate=e_gate, e_up=e_up, e_down=e_down, ln2_g=ln2_g[0], ln2_b=ln2_b[0])

    xp = x_prompt.reshape(bsz * seq, d)
    zq, zg, shift_p, prep = ln_proj_prep(x_prompt, ln_in_g, ln_in_b, w_parts, jnp.zeros((bsz, 1, c_shift), F32), rp,
                                         chunk, ROW_TILE, BF16)
    o_a, wkv_p = wkv(prep, jnp.zeros((bsz, c_a // HEAD, HEAD, HEAD), F32), lnx_g[0], lnx_b[0], chunk)
    mkv = proj(mem_prompt.reshape(bsz * n_mem, d), w_mem_kv[0].astype(BF16), ROW_TILE).reshape(bsz, n_mem, 2 * mem_w)
    mk_p, mv_p = mkv[..., :mem_w], mkv[..., mem_w:]
    tables = rope_tables(jnp.arange(seq, dtype=F32))
    o_b, o_m, k_rot = attn_prompt(zq, mk_p, mv_p, sinks[0], tables, window, q_w, kv_w)
    n_all = bsz * seq + dec
    n_buf = -(-n_all // ROW_TILE) * ROW_TILE
    routed = merge(xp, o_a.reshape(-1, c_a), o_b.reshape(-1, q_w), o_m.reshape(-1, mem_w),
                   zg.reshape(bsz * seq, -1), mw, ROW_TILE, alpha, n_buf)
    shift_p = shift_p[:, 0]
    kb_p = k_rot[:, -window:].reshape(bsz, window, H_KV, HEAD)
    vb_p = zq[:, -window:, q_w + kv_w:q_w + 2 * kv_w].reshape(bsz, window, H_KV, HEAD)

    xs = x_sample.reshape(dec, d)
    zq_s, zg_s, zr_s, ops_s = ln_proj_prep(xs.reshape(1, dec, d), ln_in_g, ln_in_b, w_parts,
                                           state_shift[0].reshape(1, dec, c_shift), rp, 1, dec, F32)
    zq_s, zg_s, zr_s = zq_s[0], zg_s[0], zr_s[0]
    o_a_s, wkv_s = wkv_step([a.reshape(dec, c_a) for a in ops_s], state_wkv[0], lnx_g[0], lnx_b[0], STEP_BATCH)
    tables_s = rope_tables(jnp.full((1,), past_len, F32))
    o_b_s, o_m_s, nk_s, nv_s = attn_step(
        zq_s, cache_win_k[0].reshape(dec, window, kv_w), cache_win_v[0].reshape(dec, window, kv_w),
        cache_mem_k[0].reshape(dec, n_mem, mem_w), cache_mem_v[0].reshape(dec, n_mem, mem_w),
        sinks[0], tables_s, q_w, kv_w, STEP_BATCH)
    x1, x1_tiles, eidx, gate = merge(xs, o_a_s, o_b_s, o_m_s, zg_s, mw, dec, alpha, n_buf, bsz * seq, routed)

    y_prompt, y_sample = hier_moe_ln(x1, x1_tiles, eidx, gate, mw, ROW_TILE, alpha, n_all, bsz * seq)
    y_prompt = y_prompt.reshape(bsz, seq, d)
    y_sample = y_sample.reshape(dec, 1, d)

    sd = state_wkv.dtype
    return (y_prompt, y_sample, wkv_p[None].astype(sd), wkv_s[None].astype(sd), shift_p[None], zr_s[None],
            kb_p[None], vb_p[None], nk_s.reshape(dec, window, H_KV, HEAD)[None],
            nv_s.reshape(dec, window, H_KV, HEAD)[None],
            mk_p.reshape(bsz, n_mem, -1, HEAD)[None], mv_p.reshape(bsz, n_mem, -1, HEAD)[None])
```

```python
import functools
import math

import jax
import jax.numpy as jnp
from jax import lax
from jax.experimental import pallas as pl
from jax.experimental.pallas import tpu as pltpu

F32 = jnp.float32
BF16 = jnp.bfloat16

ROW_TILE = 256
WKV_CHUNK = 64
CHUNKS_PER_STEP = 8
SCAN_BATCH = 8
ATTN_BLOCKS = 4
STEP_BATCH = 16
PROJ_PIECE = 512

HEAD = 64
LANES = 128
H_KV = 2
ROT_HALF = 8
ROPE_THETA = 500000.0
PAST_LEN = 8192
N_GROUPS = 4
EXPERTS_PER_GROUP = 8
EXPERT_BLOCK = 128
LN_EPS = 1e-5
LNX_EPS = 64e-5
NEG_INF = -1e30
VMEM_LIMIT = 48 * 1024 * 1024


def _pieces(x, n):
    out = []
    for _ in range(n):
        p = x.astype(BF16)
        out.append(p)
        x = x - p.astype(F32)
    return out


def _mask_dot(x, mask, n=2):
    return sum(jnp.dot(p, mask, preferred_element_type=F32) for p in _pieces(x, n))


def _split3(x, axis, lhs):
    hi = x.astype(BF16).astype(F32)
    lo = x - hi
    return jnp.concatenate([hi, hi, lo] if lhs else [hi, lo, hi], axis=axis).astype(BF16)


def _dot3(a, b):
    return jnp.dot(_split3(a, 1, True), _split3(b, 0, False), preferred_element_type=F32)


def _dot3_t(a, b):
    return lax.dot_general(_split3(a, 1, True), _split3(b, 1, False), (((1,), (1,)), ((), ())),
                           preferred_element_type=F32)


def _bdot(a, b):
    return jnp.dot(a.astype(BF16), b.astype(BF16), preferred_element_type=F32)


def _bdot_t(a, b):
    return lax.dot_general(a.astype(BF16), b.astype(BF16), (((1,), (1,)), ((), ())), preferred_element_type=F32)


def _sigmoid(x):
    return 0.5 * jnp.tanh(0.5 * x) + 0.5


def _ln(x, g, b):
    mu = jnp.mean(x, axis=-1, keepdims=True)
    xc = x - mu
    var = jnp.mean(xc * xc, axis=-1, keepdims=True)
    return xc * lax.rsqrt(var + LN_EPS) * g + b


def _const_spec(shape):
    nd = len(shape)
    return pl.BlockSpec(shape, lambda *_: (0,) * nd)


def _params(sem):
    return pltpu.CompilerParams(dimension_semantics=sem, vmem_limit_bytes=VMEM_LIMIT)


def _ln_proj_prep_kernel(chunk, x_ref, g_ref, b_ref, wr_ref, wq_ref, wg_ref, prev_ref, mu_ref, w0_ref, wup_ref,
                         a0_ref, aup_ref, gup_ref, kk_ref, ka_ref, rk_ref, hsum_ref, tril_ref,
                         zq_ref, zg_ref, zr_ref, *refs):
    out_refs, carry_ref = refs[:-1], refs[-1]
    xn = _ln(x_ref[0], g_ref[...], b_ref[...]).astype(BF16)
    z = jnp.dot(xn, wr_ref[...], preferred_element_type=F32)
    pending = [(o_ref, w_ref, c) for o_ref, w_ref in ((zq_ref, wq_ref), (zg_ref, wg_ref))
               for c in range(0, w_ref.shape[1], PROJ_PIECE)]

    def project(n_pieces):
        for _ in range(min(n_pieces, len(pending))):
            o_ref, w_ref, c = pending.pop(0)
            o_ref[0, :, c:c + PROJ_PIECE] = jnp.dot(xn, w_ref[:, c:c + PROJ_PIECE],
                                                    preferred_element_type=F32).astype(o_ref.dtype)

    tt = z.shape[0]
    c_a = w0_ref.shape[-1]
    r_w, r_a, r_g = wup_ref.shape[0], aup_ref.shape[0], gup_ref.shape[0]
    if chunk == 1:
        zr_ref[0] = z
        prev = prev_ref[0]
    else:
        zr_ref[0] = z[tt - 1:tt, :]

        @pl.when(pl.program_id(1) == 0)
        def _():
            carry_ref[...] = prev_ref[0]

        row = lax.broadcasted_iota(jnp.int32, z.shape, 0)
        prev = jnp.where(row == 0, carry_ref[...], pltpu.roll(z, 1, 0))
        carry_ref[...] = z[tt - 1:tt, :]
    zs = z + (prev - z) * mu_ref[...]
    r = zs[:, :c_a]
    k = zs[:, c_a:2 * c_a]
    v = zs[:, 2 * c_a:3 * c_a]
    o = 3 * c_a
    xw = zs[:, o:o + r_w]
    xa = zs[:, o + r_w:o + r_w + r_a]
    xg = zs[:, o + r_w + r_a:o + r_w + r_a + r_g]
    project(1)
    warg = -(w0_ref[...] + _dot3(jnp.tanh(xw), wup_ref[...]))
    softplus = jnp.maximum(warg, 0.0) + jnp.log1p(jnp.exp(-jnp.abs(warg)))
    lw = -jnp.exp(-softplus - 0.5)
    project(1)
    a = _sigmoid(a0_ref[...] + _dot3(xa, aup_ref[...]))
    g = _dot3(_sigmoid(xg), gup_ref[...])
    project(1)
    kkr = k * kk_ref[...]
    kk = kkr / jnp.maximum(jnp.sqrt(_mask_dot(kkr * kkr, hsum_ref[...])), 1e-12)
    project(1)
    k2 = k * (1.0 + (a - 1.0) * ka_ref[...])
    bonus = _mask_dot(r * k2 * rk_ref[...], hsum_ref[...]) * v
    kb = kk * a
    project(1)
    if chunk == 1:
        outs = (r, jnp.exp(lw), k2, v, -kk, kb, g, bonus)
        for ref, val in zip(out_refs, outs):
            ref[0] = val
        project(len(pending))
        return
    pieces = _pieces(lw, 3)
    ones3 = jnp.ones((chunk, 3 * chunk), BF16)
    cw, cwl = [], []
    for c in range(tt // chunk):
        stack = jnp.concatenate([p[c * chunk:(c + 1) * chunk] for p in pieces], axis=0)
        cw.append(jnp.dot(tril_ref[...], stack, preferred_element_type=F32))
        cwl.append(jnp.dot(ones3, stack, preferred_element_type=F32))
    cw = jnp.concatenate(cw, axis=0)
    cwl = jnp.concatenate(cwl, axis=0)
    project(1)
    e_inv = jnp.exp(-cw)
    outs = (r * jnp.exp(cw), -kk * jnp.exp(cw - lw), kb * e_inv, k2 * e_inv, v, g, bonus)
    for ref, val in zip(out_refs[:-1], outs):
        ref[0] = val
        project(1)
    wl = jnp.exp(cwl)
    for c in range(tt // chunk):
        out_refs[-1][0, c] = wl[c * chunk:c * chunk + 1, :]
    project(len(pending))


def _head_sum_matrix(width):
    idx = jnp.arange(width)
    return ((idx[:, None] // HEAD) == (idx[None, :] // HEAD)).astype(BF16)


def ln_proj_prep(x, g, b, ws, prev, p, chunk, tt, gate_dtype):
    bsz, t, d = x.shape
    w_r, w_q, w_g = ws
    cs = w_r.shape[1]
    c_a = p['w0'].shape[-1]
    assert prev.shape[1] == (t if chunk == 1 else 1)
    ridx = jnp.arange(chunk)
    tril = jnp.tile((ridx[None, :] <= ridx[:, None]).astype(BF16), (1, 3))
    hsum = _head_sum_matrix(c_a)
    row = lambda a: a.reshape(1, -1)
    rows = lambda width: pl.BlockSpec((1, tt, width), lambda bi, i: (bi, i, 0))
    full = jax.ShapeDtypeStruct((bsz, t, c_a), F32)
    if chunk == 1:
        out_specs, out_shape = [rows(c_a)] * 8, [full] * 8
        prev_spec, zr_spec, zr_rows = rows(cs), rows(cs), t
    else:
        out_specs = [rows(c_a)] * 7 + [pl.BlockSpec((1, tt // chunk, 1, c_a), lambda bi, i: (bi, i, 0, 0))]
        out_shape = [full] * 7 + [jax.ShapeDtypeStruct((bsz, t // chunk, 1, c_a), F32)]
        prev_spec = zr_spec = pl.BlockSpec((1, 1, cs), lambda bi, i: (bi, 0, 0))
        zr_rows = 1
    consts = [row(p['mu']), row(p['w0']), p['w_up'], row(p['a0']), p['a_up'], p['g_up'], row(p['k_k']),
              row(p['k_a']), row(p['r_k']), hsum, tril]
    weight = lambda w: pl.BlockSpec(w.shape, lambda bi, i: (0, 0), pipeline_mode=pl.Buffered(1))
    outs = pl.pallas_call(
        functools.partial(_ln_proj_prep_kernel, chunk), name="ln_proj_prep",
        grid=(bsz, t // tt),
        in_specs=[rows(d), _const_spec((1, d)), _const_spec((1, d)), weight(w_r), weight(w_q), weight(w_g), prev_spec]
        + [_const_spec(c.shape) for c in consts],
        out_specs=[rows(w_q.shape[1]), rows(w_g.shape[1]), zr_spec] + out_specs,
        out_shape=[jax.ShapeDtypeStruct((bsz, t, w_q.shape[1]), F32),
                   jax.ShapeDtypeStruct((bsz, t, w_g.shape[1]), gate_dtype),
                   jax.ShapeDtypeStruct((bsz, zr_rows, cs), F32)] + out_shape,
        scratch_shapes=[pltpu.VMEM((1, cs), F32)],
        compiler_params=_params(("parallel", "arbitrary")),
    )(x, row(g), row(b), w_r, w_q, w_g, prev, *consts)
    return outs[0], outs[1], outs[2], outs[3:]


def _wkv_chunk_kernel(ra_ref, at_ref, bt_ref, kt_ref, v_ref, wl_ref, rp_ref, y0_ref, m_ref, n_ref):
    n_chunks = m_ref.shape[1]
    length = ra_ref.shape[1] // n_chunks
    heads = ra_ref.shape[2] // HEAD
    row = lax.broadcasted_iota(jnp.int32, (length, length), 0)
    col = lax.broadcasted_iota(jnp.int32, (length, length), 1)
    strict = row > col
    incl = row >= col
    hrow = lax.broadcasted_iota(jnp.int32, (HEAD, HEAD), 0)
    hcol = lax.broadcasted_iota(jnp.int32, (HEAD, HEAD), 1)
    units = [(c, h) for c in range(n_chunks) for h in range(heads)]
    us = range(len(units))
    at = lambda ref, u: ref[0, units[u][0] * length:(units[u][0] + 1) * length,
                            units[u][1] * HEAD:(units[u][1] + 1) * HEAD]
    gram = [_bdot_t(jnp.concatenate([at(at_ref, u), at(ra_ref, u)], axis=0),
                    jnp.concatenate([at(bt_ref, u), at(kt_ref, u)], axis=0)) for u in us]
    a_ab = [jnp.where(strict, g[:length, :length], 0.0) for g in gram]
    a_kk = [jnp.concatenate([jnp.where(strict, g[:length, length:], 0.0),
                             jnp.where(incl, g[length:, length:], 0.0)], axis=0) for g in gram]
    a_rb = [jnp.where(incl, g[length:, :length], 0.0) for g in gram]
    inv = [jnp.where(row == col, 1.0, a) for a in a_ab]
    pw = [_bdot(a, a) for a in a_ab]
    avy = [_bdot(a_kk[u], at(v_ref, u)) for u in us]
    w_l = [wl_ref[0, units[u][0], :, units[u][1] * HEAD:(units[u][1] + 1) * HEAD] for u in us]
    nk = [_bdot(at(v_ref, u).T, at(kt_ref, u) * w_l[u]) for u in us]
    for _ in range(int(math.log2(length)) - 2):
        both = [_bdot(jnp.concatenate([pw[u], inv[u]], axis=0), pw[u]) for u in us]
        pw = [b[:length] for b in both]
        inv = [inv[u] + both[u][length:] for u in us]
    inv = [inv[u] + _bdot(inv[u], pw[u]) for u in us]
    pq = [_bdot(inv[u], jnp.concatenate([at(at_ref, u), avy[u][:length]], axis=1)) for u in us]
    ry = [_bdot(a_rb[u], pq[u]) for u in us]
    mn = [_bdot(pq[u].T, at(bt_ref, u) * w_l[u]) for u in us]
    lanes = lambda f, c: jnp.concatenate([f(c * heads + h) for h in range(heads)], axis=1)
    rows = lambda f: jnp.concatenate([lanes(f, c) for c in range(n_chunks)], axis=0)
    rp_ref[0] = rows(lambda u: at(ra_ref, u) + ry[u][:, :HEAD])
    y0_ref[0] = rows(lambda u: avy[u][length:] + ry[u][:, HEAD:])
    for c in range(n_chunks):
        m_ref[0, c] = lanes(lambda u: jnp.where(hrow == hcol, w_l[u], 0.0) + mn[u][:HEAD], c)
        n_ref[0, c] = lanes(lambda u: nk[u] + mn[u][HEAD:], c)


def _head_norm_wide(y, hsum):
    yc = y - _mask_dot(y, hsum) * (1.0 / HEAD)
    return yc * lax.rsqrt(_mask_dot(yc * yc, hsum) * (1.0 / HEAD) + LNX_EPS)


def _wkv_scan_kernel(rp_ref, y0_ref, m_ref, n_ref, g_ref, bonus_ref, s0_ref, lg_ref, lb_ref, hsum_ref,
                     o_ref, sout_ref, s_ref):
    c = pl.program_id(1)
    nb = rp_ref.shape[0]
    heads = rp_ref.shape[2] // HEAD

    @pl.when(c == 0)
    def _():
        s_ref[...] = s0_ref[...]

    pairs = [(b, h, slice(h * HEAD, (h + 1) * HEAD)) for b in range(nb) for h in range(heads)]
    s = [s_ref[b, h] for b, h, _ in pairs]
    s_new = [n_ref[b, 0, :, sl] + _dot3(s[j], m_ref[b, 0, :, sl]) for j, (b, h, sl) in enumerate(pairs)]
    ys = [y0_ref[b, :, sl] + _bdot_t(rp_ref[b, :, sl], s[j]) for j, (b, h, sl) in enumerate(pairs)]
    for j, (b, h, _) in enumerate(pairs):
        s_ref[b, h] = s_new[j]
    length = rp_ref.shape[1]
    y = _head_norm_wide(jnp.concatenate([jnp.concatenate(ys[b * heads:(b + 1) * heads], axis=-1) for b in range(nb)],
                                        axis=0), hsum_ref[...])
    for b in range(nb):
        o = (y[b * length:(b + 1) * length] * lg_ref[...] + lb_ref[...] + bonus_ref[b]) * g_ref[b]
        o_ref[b] = o.astype(o_ref.dtype)

    @pl.when(c == pl.num_programs(1) - 1)
    def _():
        sout_ref[...] = s_ref[...]


def wkv(prep, s0, lnx_g, lnx_b, chunk):
    ra, at, bt, kt, v, g, bonus, wl = prep
    bsz, t, c_a = ra.shape
    heads = c_a // HEAD
    n_chunks = t // chunk
    assert chunk & (chunk - 1) == 0 and chunk >= 4 and t % chunk == 0
    per_step = CHUNKS_PER_STEP if n_chunks % CHUNKS_PER_STEP == 0 else 1
    tile = pl.BlockSpec((1, per_step * chunk, c_a), lambda b, c: (b, c, 0))
    mat = pl.BlockSpec((1, per_step, HEAD, c_a), lambda b, c: (b, c, 0, 0))
    full = jax.ShapeDtypeStruct((bsz, t, c_a), F32)
    mats = jax.ShapeDtypeStruct((bsz, n_chunks, HEAD, c_a), F32)
    rp, y0, m, n = pl.pallas_call(
        _wkv_chunk_kernel, name="wkv_chunk",
        grid=(bsz, n_chunks // per_step),
        in_specs=[tile] * 5 + [pl.BlockSpec((1, per_step, 1, c_a), lambda b, c: (b, c, 0, 0))],
        out_specs=[tile, tile, mat, mat],
        out_shape=[full, full, mats, mats],
        compiler_params=_params(("parallel", "parallel")),
    )(ra, at, bt, kt, v, wl)
    nb = SCAN_BATCH if bsz % SCAN_BATCH == 0 else 1
    tile = pl.BlockSpec((nb, chunk, c_a), lambda b, c: (b, c, 0))
    mat = pl.BlockSpec((nb, 1, HEAD, c_a), lambda b, c: (b, c, 0, 0))
    st = pl.BlockSpec((nb, heads, HEAD, HEAD), lambda b, c: (b, 0, 0, 0))
    return pl.pallas_call(
        _wkv_scan_kernel, name="wkv_scan",
        grid=(bsz // nb, n_chunks),
        in_specs=[tile, tile, mat, mat, tile, tile, st, _const_spec((1, c_a)), _const_spec((1, c_a)),
                  _const_spec((c_a, c_a))],
        out_specs=[tile, st],
        out_shape=[jax.ShapeDtypeStruct((bsz, t, c_a), BF16), jax.ShapeDtypeStruct(s0.shape, F32)],
        scratch_shapes=[pltpu.VMEM((nb, heads, HEAD, HEAD), F32)],
        compiler_params=_params(("parallel", "arbitrary")),
    )(rp, y0, m, n, g, bonus, s0, lnx_g.reshape(1, c_a), lnx_b.reshape(1, c_a), _head_sum_matrix(c_a))


def _wkv_step_kernel(r_ref, w_ref, k_ref, v_ref, a_ref, b_ref, g_ref, bonus_ref, s_ref, lg_ref, lb_ref, hsum_ref,
                     o_ref, sout_ref):
    bt, heads = s_ref.shape[0], s_ref.shape[1]
    c_a = heads * HEAD
    hsum = hsum_ref[...]
    diag = (lax.broadcasted_iota(jnp.int32, (HEAD, c_a), 1) % HEAD
            == lax.broadcasted_iota(jnp.int32, (HEAD, c_a), 0))
    seqs = range(bt)
    row = lambda ref, i: ref[i:i + 1, :]
    stack = lambda f: jnp.concatenate([f(i) for i in seqs], axis=0)
    piece = lambda x, i: x[i * HEAD:(i + 1) * HEAD]
    s = stack(lambda i: jnp.concatenate([s_ref[i, h] for h in range(heads)], axis=1))
    sa = _mask_dot(s * stack(lambda i: jnp.broadcast_to(row(a_ref, i), (HEAD, c_a))), hsum)
    v_rows = _mask_dot(stack(lambda i: jnp.where(diag, row(v_ref, i), 0.0)), hsum)
    s = stack(lambda i: piece(s, i) * row(w_ref, i) + piece(sa, i) * row(b_ref, i) + piece(v_rows, i) * row(k_ref, i))
    for i in seqs:
        for h in range(heads):
            sout_ref[i, h] = piece(s, i)[:, h * HEAD:(h + 1) * HEAD]
    y_rows = _mask_dot(stack(lambda i: piece(s, i) * row(r_ref, i)), hsum)
    y = stack(lambda i: jnp.sum(jnp.where(diag, piece(y_rows, i), 0.0), axis=0, keepdims=True))
    y = _head_norm_wide(y, hsum)
    o_ref[...] = (y * lg_ref[...] + lb_ref[...] + bonus_ref[...]) * g_ref[...]


def wkv_step(ops, s0, lnx_g, lnx_b, bt):
    bsz, c_a = ops[0].shape
    rows = pl.BlockSpec((bt, c_a), lambda i: (i, 0))
    st = pl.BlockSpec((bt,) + s0.shape[1:], lambda i: (i, 0, 0, 0))
    hsum = _head_sum_matrix(c_a)
    return pl.pallas_call(
        _wkv_step_kernel, name="wkv_step",
        grid=(bsz // bt,),
        in_specs=[rows] * 8 + [st, _const_spec((1, c_a)), _const_spec((1, c_a)), _const_spec(hsum.shape)],
        out_specs=[rows, st],
        out_shape=[jax.ShapeDtypeStruct((bsz, c_a), F32), jax.ShapeDtypeStruct(s0.shape, F32)],
        compiler_params=_params(("parallel",)),
    )(*ops, s0, lnx_g.reshape(1, c_a), lnx_b.reshape(1, c_a), hsum)


def rope_tables(pos):
    inv_freq = ROPE_THETA ** (-jnp.arange(ROT_HALF, dtype=F32) / ROT_HALF)
    ang = pos[:, None] * inv_freq[None, :]
    cos, sin = jnp.cos(ang), jnp.sin(ang)
    t = pos.shape[0]
    rest = HEAD - 2 * ROT_HALF
    c = jnp.concatenate([cos, cos, jnp.ones((t, rest), F32)], axis=1)
    s1 = jnp.concatenate([jnp.zeros((t, ROT_HALF), F32), sin, jnp.zeros((t, rest), F32)], axis=1)
    s2 = jnp.concatenate([-sin, jnp.zeros((t, HEAD - ROT_HALF), F32)], axis=1)
    rep = LANES // HEAD
    return tuple(jnp.tile(x, (1, rep)) for x in (c, s1, s2))


def _rope(x, c, s1, s2):
    width = x.shape[-1]
    rep = width // c.shape[-1]
    if rep > 1:
        c, s1, s2 = (jnp.concatenate([t] * rep, axis=1) for t in (c, s1, s2))
    return x * c + pltpu.roll(x, ROT_HALF, 1) * s1 + pltpu.roll(x, width - ROT_HALF, 1) * s2


def _attn_kernel(window, q_w, kv_w, zq_ref, kp_ref, vp_ref, rc_ref, rs1_ref, rs2_ref, pc_ref, ps1_ref, ps2_ref,
                 mk_ref, mv_ref, sink_ref, ob_ref, om_ref, kr_ref):
    zq = zq_ref[0]
    nblk = zq.shape[0] // window
    first = pl.program_id(1) * nblk
    scale = HEAD ** -0.5
    q = zq[:, :q_w]
    k = zq[:, q_w:q_w + kv_w]
    v = zq[:, q_w + kv_w:q_w + 2 * kv_w]
    qm = zq[:, q_w + 2 * kv_w:]
    qr = _rope(q, rc_ref[...], rs1_ref[...], rs2_ref[...]) * scale
    kr = _rope(k, rc_ref[...], rs1_ref[...], rs2_ref[...])
    kr_ref[0] = kr
    k_all = jnp.concatenate([_rope(kp_ref[0], pc_ref[...], ps1_ref[...], ps2_ref[...]), kr], axis=0)
    v_all = jnp.concatenate([vp_ref[0], v], axis=0)
    gqa = q_w // kv_w
    assert window & (window - 1) == 0
    qi = lax.broadcasted_iota(jnp.int32, (gqa * window, 2 * window), 0) & (window - 1)
    kj = lax.broadcasted_iota(jnp.int32, (gqa * window, 2 * window), 1)
    band = (kj > qi) & (kj <= qi + window)
    tdot = lambda a, b: lax.dot_general(a, b, (((1,), (1,)), ((), ())), preferred_element_type=F32)
    dot = lambda a, b: jnp.dot(a, b, preferred_element_type=F32)
    hsl = lambda h: slice(h * HEAD, (h + 1) * HEAD)
    rows = lambda i: slice(i * window, (i + 1) * window)
    keys = lambda i: slice(i * window, (i + 2) * window)
    wins = [(i, hk) for i in range(nblk) for hk in range(kv_w // HEAD)]
    mems = [(i, h) for i in range(nblk) for h in range(qm.shape[1] // HEAD)]
    qmb = (qm * scale).astype(BF16)
    mk = mk_ref[0].astype(BF16)
    mv = mv_ref[0].astype(BF16)
    kcat = [k_all[keys(i), hsl(hk)].astype(BF16) for i, hk in wins]
    vcat = [v_all[keys(i), hsl(hk)].astype(BF16) for i, hk in wins]
    qs = [jnp.concatenate([qr[rows(i), hsl(hk * gqa + g)] for g in range(gqa)], axis=0).astype(BF16)
          for i, hk in wins]
    s = [jnp.where(band & ((kj >= window) | (first + i > 0)), tdot(qs[u], kcat[u]), NEG_INF)
         for u, (i, hk) in enumerate(wins)]
    sm = [tdot(qmb[rows(i), hsl(h)], mk[:, hsl(h)]) for i, h in mems]
    sink = [jnp.concatenate([jnp.full((window, 1), sink_ref[hk * gqa + g], F32) for g in range(gqa)], axis=0)
            for i, hk in wins]
    m = [jnp.maximum(jnp.max(s[u], axis=-1, keepdims=True), sink[u]) for u in range(len(wins))]
    p = [jnp.exp(s[u] - m[u]) for u in range(len(wins))]
    pm = [jnp.exp(x - jnp.max(x, axis=-1, keepdims=True)) for x in sm]
    o = [dot(p[u].astype(BF16), vcat[u])
         / (jnp.sum(p[u], axis=-1, keepdims=True) + jnp.exp(sink[u] - m[u])) for u in range(len(wins))]
    om = [dot(pm[u].astype(BF16), mv[:, hsl(h)]) / jnp.sum(pm[u], axis=-1, keepdims=True)
          for u, (i, h) in enumerate(mems)]
    n_kv, n_mh = kv_w // HEAD, qm.shape[1] // HEAD
    ob = jnp.concatenate([jnp.concatenate([o[i * n_kv + hk][g * window:(g + 1) * window]
                                           for hk in range(n_kv) for g in range(gqa)], axis=1)
                          for i in range(nblk)], axis=0)
    ob_ref[0] = ob.astype(ob_ref.dtype)
    om_ref[0] = jnp.concatenate([jnp.concatenate(om[i * n_mh:(i + 1) * n_mh], axis=1) for i in range(nblk)],
                                axis=0).astype(om_ref.dtype)


def attn_prompt(zq, mk, mv, sinks, tables, window, q_w, kv_w):
    bsz, t, zw = zq.shape
    mem_w = zw - q_w - 2 * kv_w
    assert kv_w == LANES and q_w % kv_w == 0
    kcol, vcol = q_w // kv_w, q_w // kv_w + 1
    nblk = ATTN_BLOCKS if (t // window) % ATTN_BLOCKS == 0 else 1
    rows = nblk * window
    prev = lambda n: jnp.maximum(n * nblk - 1, 0)
    tab = pl.BlockSpec((rows, LANES), lambda b, n: (n, 0))
    ptab = pl.BlockSpec((window, LANES), lambda b, n: (prev(n), 0))
    mem = pl.BlockSpec((1,) + mk.shape[1:], lambda b, n: (b, 0, 0))
    return pl.pallas_call(
        functools.partial(_attn_kernel, window, q_w, kv_w), name="attn",
        grid=(bsz, t // rows),
        in_specs=[pl.BlockSpec((1, rows, zw), lambda b, n: (b, n, 0)),
                  pl.BlockSpec((1, window, kv_w), lambda b, n: (b, prev(n), kcol)),
                  pl.BlockSpec((1, window, kv_w), lambda b, n: (b, prev(n), vcol)),
                  tab, tab, tab, ptab, ptab, ptab, mem, mem,
                  pl.BlockSpec(memory_space=pltpu.SMEM)],
        out_specs=[pl.BlockSpec((1, rows, q_w), lambda b, n: (b, n, 0)),
                   pl.BlockSpec((1, rows, mem_w), lambda b, n: (b, n, 0)),
                   pl.BlockSpec((1, rows, kv_w), lambda b, n: (b, n, 0))],
        out_shape=[jax.ShapeDtypeStruct((bsz, t, q_w), BF16), jax.ShapeDtypeStruct((bsz, t, mem_w), BF16),
                   jax.ShapeDtypeStruct((bsz, t, kv_w), F32)],
        compiler_params=_params(("parallel", "parallel")),
    )(zq, zq, zq, *tables, *tables, mk, mv, sinks)


def _attn_step_kernel(q_w, kv_w, zq_ref, ck_ref, cv_ref, mk_ref, mv_ref, rc_ref, rs1_ref, rs2_ref, sink_ref,
                      ob_ref, om_ref, nk_ref, nv_ref):
    bt = zq_ref.shape[0]
    window = ck_ref.shape[1]
    mem_w = om_ref.shape[1]
    n_q, gqa, per_vreg = q_w // HEAD, q_w // kv_w, LANES // HEAD
    scale = HEAD ** -0.5
    zq = zq_ref[...]
    q = _rope(zq[:, :q_w], rc_ref[...], rs1_ref[...], rs2_ref[...]) * scale
    k_new = _rope(zq[:, q_w:q_w + kv_w], rc_ref[...], rs1_ref[...], rs2_ref[...])
    v_new = zq[:, q_w + kv_w:q_w + 2 * kv_w]
    qm = zq[:, q_w + 2 * kv_w:] * scale
    own = lambda w: (lax.broadcasted_iota(jnp.int32, (n_q, w), 1) // HEAD
                     == lax.broadcasted_iota(jnp.int32, (n_q, w), 0))
    own_q, own_m = own(q_w), own(mem_w)
    hrow = lax.broadcasted_iota(jnp.int32, (n_q, LANES), 0)
    hblk = lax.broadcasted_iota(jnp.int32, (n_q, LANES), 1) // HEAD
    swap = (hrow % per_vreg) != (hrow // gqa)
    keep = hblk == hrow % per_vreg
    key_ok = lax.broadcasted_iota(jnp.int32, (n_q, window), 1) >= 1
    wrow = lax.broadcasted_iota(jnp.int32, (window, kv_w), 0)
    sink = sink_ref[...]
    bs = range(bt)
    tdot = lambda a, b: lax.dot_general(a.astype(BF16), b.astype(BF16), (((1,), (1,)), ((), ())),
                                        preferred_element_type=F32)
    for b in bs:
        nk_ref[b] = jnp.where(wrow == window - 1, k_new[b:b + 1], pltpu.roll(ck_ref[b], window - 1, 0))
        nv_ref[b] = jnp.where(wrow == window - 1, v_new[b:b + 1], pltpu.roll(cv_ref[b], window - 1, 0))
    q8 = []
    for b in bs:
        rep = jnp.where(own_q, q[b:b + 1], 0.0)
        fold = sum(rep[:, c * LANES:(c + 1) * LANES] for c in range(q_w // LANES))
        q8.append(jnp.where(swap, pltpu.roll(fold, HEAD, 1), fold))
    qm8 = [jnp.where(own_m, qm[b:b + 1], 0.0) for b in bs]
    s = [jnp.where(key_ok, tdot(q8[b], ck_ref[b]), NEG_INF) for b in bs]
    sm = [tdot(qm8[b], mk_ref[b]) for b in bs]
    s_new = [jnp.sum(q8[b] * k_new[b:b + 1], axis=1, keepdims=True) for b in bs]
    m = [jnp.maximum(jnp.maximum(jnp.max(s[b], axis=1, keepdims=True), s_new[b]), sink) for b in bs]
    p = [jnp.exp(s[b] - m[b]) for b in bs]
    p_new = [jnp.exp(s_new[b] - m[b]) for b in bs]
    pm = [jnp.exp(sm[b] - jnp.max(sm[b], axis=1, keepdims=True)) for b in bs]
    o8 = [(_bdot(p[b], cv_ref[b]) + p_new[b] * v_new[b:b + 1])
          / (jnp.sum(p[b], axis=1, keepdims=True) + p_new[b] + jnp.exp(sink - m[b])) for b in bs]
    om8 = [_bdot(pm[b], mv_ref[b]) / jnp.sum(pm[b], axis=1, keepdims=True) for b in bs]
    ob, om = [], []
    for b in bs:
        o = jnp.where(swap, pltpu.roll(o8[b], HEAD, 1), o8[b])
        o = jnp.concatenate([jnp.where(keep, o, 0.0)] * (q_w // LANES), axis=1)
        ob.append(jnp.sum(jnp.where(own_q, o, 0.0), axis=0, keepdims=True))
        om.append(jnp.sum(jnp.where(own_m, om8[b], 0.0), axis=0, keepdims=True))
    ob_ref[...] = jnp.concatenate(ob, axis=0)
    om_ref[...] = jnp.concatenate(om, axis=0)


def attn_step(zq, ck, cv, mk, mv, sinks, tables, q_w, kv_w, bt):
    bsz, zw = zq.shape
    mem_w = zw - q_w - 2 * kv_w
    window, n_mem = ck.shape[1], mk.shape[1]
    n_q = q_w // HEAD
    assert kv_w == LANES and mem_w // HEAD <= n_q and n_q == 8
    rows = lambda w: pl.BlockSpec((bt, w), lambda i: (i, 0))
    cache = pl.BlockSpec((bt, window, kv_w), lambda i: (i, 0, 0))
    mem = pl.BlockSpec((bt, n_mem, mem_w), lambda i: (i, 0, 0))
    consts = list(tables) + [sinks.reshape(n_q, 1)]
    const_specs = [_const_spec(c.shape) for c in consts]
    return pl.pallas_call(
        functools.partial(_attn_step_kernel, q_w, kv_w), name="attn_step",
        grid=(bsz // bt,),
        in_specs=[rows(zw), cache, cache, mem, mem] + const_specs,
        out_specs=[rows(q_w), rows(mem_w), cache, cache],
        out_shape=[jax.ShapeDtypeStruct((bsz, q_w), F32), jax.ShapeDtypeStruct((bsz, mem_w), F32),
                   jax.ShapeDtypeStruct(ck.shape, F32), jax.ShapeDtypeStruct(cv.shape, F32)],
        compiler_params=_params(("parallel",)),
    )(zq, ck, cv, mk, mv, *consts)


def _proj_kernel(x_ref, w_ref, o_ref):
    o_ref[...] = jnp.dot(x_ref[...].astype(BF16), w_ref[...], preferred_element_type=F32)


def proj(x, w, tm):
    n, d = x.shape
    return pl.pallas_call(
        _proj_kernel, name="proj",
        grid=(n // tm,),
        in_specs=[pl.BlockSpec((tm, d), lambda i: (i, 0)), _const_spec(w.shape)],
        out_specs=pl.BlockSpec((tm, w.shape[1]), lambda i: (i, 0)),
        out_shape=jax.ShapeDtypeStruct((n, w.shape[1]), F32),
        compiler_params=_params(("parallel",)),
    )(x, w)


def _merge_kernel(alpha, n_tiles, x_ref, oa_ref, ob_ref, om_ref, gt_ref, lig_ref, lib_ref, pa_ref, pb_ref, pm_ref,
                  wo_ref, l1g_ref, l1b_ref, wr_ref, br_ref, *refs):
    outs = refs[-4:]

    @pl.when(pl.program_id(0) >= n_tiles)
    def _():
        for ref in outs:
            ref[...] = jnp.zeros_like(ref)

    @pl.when(pl.program_id(0) < n_tiles)
    def _():
        _merge_tile(alpha, x_ref, oa_ref, ob_ref, om_ref, gt_ref, lig_ref, lib_ref, pa_ref, pb_ref, pm_ref, wo_ref,
                    l1g_ref, l1b_ref, wr_ref, br_ref, *outs)


def _merge_tile(alpha, x_ref, oa_ref, ob_ref, om_ref, gt_ref, lig_ref, lib_ref, pa_ref, pb_ref, pm_ref, wo_ref,
                l1g_ref, l1b_ref, wr_ref, br_ref, x1_ref, x1t_ref, eidx_ref, gate_ref):
    d = x_ref.shape[1]
    xn = _ln(x_ref[...], lig_ref[...], lib_ref[...])
    gts = _sigmoid(gt_ref[...].astype(F32))
    merged = (gts[:, :d] * _bdot(oa_ref[...], pa_ref[...]) + gts[:, d:2 * d] * _bdot(ob_ref[...], pb_ref[...])
              + gts[:, 2 * d:] * _bdot(om_ref[...], pm_ref[...]))
    x1 = _ln(alpha * xn + _bdot(merged, wo_ref[...]), l1g_ref[...], l1b_ref[...])
    x1_ref[...] = x1
    _rows_to_tiles(x1t_ref, x1)
    x_hi, x_lo = _pieces(x1, 2)
    w_hi, w_lo = wr_ref[0], wr_ref[1]
    dot = lambda a, b: jnp.dot(a, b, preferred_element_type=F32)
    logits = dot(x_hi, w_hi) + dot(x_hi, w_lo) + dot(x_lo, w_hi) + br_ref[...]
    lane = lax.broadcasted_iota(jnp.int32, logits.shape, 1)
    lane_f = lane.astype(F32)
    first = lambda hit: jnp.min(jnp.where(hit, lane_f, float(LANES)), axis=-1, keepdims=True).astype(jnp.int32)
    gmask = lane < N_GROUPS
    gl = jnp.where(gmask, logits, NEG_INF)
    gmax = jnp.max(gl, axis=-1, keepdims=True)
    gidx = first(gl == gmax)
    g_w = 1.0 / jnp.sum(jnp.where(gmask, jnp.exp(gl - gmax), 0.0), axis=-1, keepdims=True)
    lo = N_GROUPS + gidx * EXPERTS_PER_GROUP
    el = jnp.where((lane >= lo) & (lane < lo + EXPERTS_PER_GROUP), logits, NEG_INF)
    v1 = jnp.max(el, axis=-1, keepdims=True)
    i1 = first(el == v1)
    el2 = jnp.where(lane == i1, NEG_INF, el)
    v2 = jnp.max(el2, axis=-1, keepdims=True)
    i2 = first(el2 == v2)
    e2 = jnp.exp(v2 - v1)
    gate1 = g_w / (1.0 + e2)
    eidx = jnp.where(lane == 0, (i1 - N_GROUPS).astype(F32), jnp.where(lane == 1, (i2 - N_GROUPS).astype(F32), 0.0))
    eidx_ref[...] = eidx.T[:eidx_ref.shape[0]].astype(jnp.int32)
    gate_ref[...] = jnp.where(lane == 0, gate1, jnp.where(lane == 1, gate1 * e2, 0.0))


def merge(x, oa, ob, om, gt, w, tm, alpha, n_total, row_offset=0, into=None):
    n, d = x.shape
    assert row_offset % tm == 0 and n % tm == 0 and n_total % tm == 0
    off = row_offset // tm
    into = list(into or [])
    n_tiles = n // tm
    steps = n_tiles if into else n_total // tm
    rows = lambda a: pl.BlockSpec((tm, a.shape[1]), lambda i: (jnp.minimum(i, n_tiles - 1), 0))
    out = lambda width: pl.BlockSpec((tm, width), lambda i: (i + off, 0))
    consts = [w['ln_in_g'], w['ln_in_b'], w['p_a'], w['p_b'], w['p_m'], w['w_o'], w['ln1_g'], w['ln1_b'],
              w['w_route'], w['b_route']]
    n_in = 5 + len(consts)
    return pl.pallas_call(
        functools.partial(_merge_kernel, alpha, n_tiles), name="merge",
        grid=(steps,),
        in_specs=[rows(a) for a in (x, oa, ob, om, gt)] + [_const_spec(c.shape) for c in consts]
        + [pl.BlockSpec(memory_space=pl.ANY)] * len(into),
        out_specs=[out(d), pl.BlockSpec((tm * TILE_ROWS, LANES), lambda i: (i + off, 0)),
                   pl.BlockSpec((TILE_ROWS, tm), lambda i: (0, i + off)), out(LANES)],
        out_shape=[jax.ShapeDtypeStruct((n_total, d), F32), jax.ShapeDtypeStruct((n_total * TILE_ROWS, LANES), F32),
                   jax.ShapeDtypeStruct((TILE_ROWS, n_total), jnp.int32), jax.ShapeDtypeStruct((n_total, LANES), F32)],
        input_output_aliases={n_in + k: k for k in range(len(into))},
        compiler_params=_params(("parallel",)),
    )(x, oa, ob, om, gt, *consts, *into)


ROW_DMA_UNROLL = 8
MOE_BUFFERS = 3
DRAIN_STEPS = 2


TILE_ROWS = 8


def _rows_from_tiles(ref, n):
    return jnp.concatenate([ref[pl.ds(s, n, stride=TILE_ROWS), :] for s in range(TILE_ROWS)], axis=1)


def _rows_to_tiles(ref, x):
    for s in range(TILE_ROWS):
        ref[pl.ds(s, x.shape[0], stride=TILE_ROWS), :] = x[:, s * LANES:(s + 1) * LANES]


def _row_copies(asg_ref, base, count, n_asg, x_hbm, buf, y_hbm, sem, gather, unrolled):
    def tile(ref, idx):
        start = idx * TILE_ROWS
        return ref.at[pl.ds(start if isinstance(idx, int) else pl.multiple_of(start, TILE_ROWS), TILE_ROWS)]

    def one(r, priority):
        a = asg_ref[base + r]
        if gather:
            tok = jnp.minimum(a, n_asg - 1)
            tok = jnp.where(tok >= n_asg // 2, tok - n_asg // 2, tok)
            copy = pltpu.make_async_copy(tile(x_hbm, tok), tile(buf, r), sem)
        else:
            copy = pltpu.make_async_copy(tile(buf, r), tile(y_hbm, a), sem)
        copy.start(priority=priority)

    if unrolled:
        for r in range(count):
            one(r, r % 2)
        return

    def body(g, carry):
        for j in range(ROW_DMA_UNROLL):
            one(g * ROW_DMA_UNROLL + j, j % 2)
        return carry
    lax.fori_loop(0, count // ROW_DMA_UNROLL, body, 0)


def _moe_expert_kernel(n_tokens, n_asg, asg_ref, be_ref, nu_ref, x_hbm, wg_ref, wu_ref, wd_ref, y_hbm,
                       xbuf, ybuf, wgb, wub, wdb, gsem, ssem):
    i = pl.program_id(0)
    used = nu_ref[0]
    rows = xbuf.shape[1] // TILE_ROWS
    n_buf = xbuf.shape[0]
    n_blocks = be_ref.shape[0]
    blk = jnp.minimum(i, n_blocks - 1)

    @pl.when((i == 0) | (be_ref[blk] != be_ref[jnp.maximum(blk - 1, 0)]))
    def _():
        wgb[...] = wg_ref[0, 0].astype(BF16)
        wub[...] = wu_ref[0, 0].astype(BF16)
        wdb[...] = wd_ref[0, 0].astype(BF16)

    def wait_gather(slot):
        pltpu.make_async_copy(x_hbm.at[pl.ds(0, rows * TILE_ROWS)], xbuf.at[slot], gsem.at[slot]).wait()

    def wait_scatter(slot):
        pltpu.make_async_copy(ybuf.at[slot], y_hbm.at[pl.ds(0, rows * TILE_ROWS)], ssem.at[slot]).wait()

    def gather(b, slot, unrolled):
        _row_copies(asg_ref, b * rows, rows, n_asg, x_hbm, xbuf.at[slot], y_hbm, gsem.at[slot], True, unrolled)

    def scatter(b, slot, unrolled):
        _row_copies(asg_ref, b * rows, rows, n_asg, x_hbm, ybuf.at[slot], y_hbm, ssem.at[slot], False, unrolled)

    def expert(slot):
        xb = _rows_from_tiles(xbuf.at[slot], rows).astype(BF16)
        hg = jnp.dot(xb, wgb[...], preferred_element_type=F32)
        hu = jnp.dot(xb, wub[...], preferred_element_type=F32)
        h = hg * _sigmoid(hg) * hu
        _rows_to_tiles(ybuf.at[slot], jnp.dot(h.astype(BF16), wdb[...], preferred_element_type=F32))

    @pl.when(i == 0)
    def _():
        ybuf[1] = jnp.zeros(ybuf.shape[1:], F32)
        stride = n_asg // 2
        gaps = [(k * stride + n_tokens, stride - n_tokens) for k in range(2)]
        gaps.append((n_asg, y_hbm.shape[0] // TILE_ROWS - n_asg))
        fills = [pltpu.make_async_copy(ybuf.at[1, pl.ds(0, min(rows, count - j) * TILE_ROWS)],
                                       y_hbm.at[pl.ds((start + j) * TILE_ROWS, min(rows, count - j) * TILE_ROWS)],
                                       ssem.at[1])
                 for start, count in gaps for j in range(0, count, rows)]
        for copy in fills:
            copy.start()
        for copy in fills:
            copy.wait()

    @pl.when((i >= 2) & (i - 2 < used))
    def _():
        wait_scatter((i - 2) % n_buf)

    steady = (i >= 1) & (i + 2 < used)
    for slot in range(n_buf):
        @pl.when(steady & (i % n_buf == slot))
        def _():
            wait_gather(slot)
            gather(i + 2, (slot + 2) % n_buf, True)
            scatter(i - 1, (slot - 1) % n_buf, True)
            expert(slot)

    @pl.when(jnp.logical_not(steady))
    def _():
        slot = i % n_buf

        @pl.when(i == 0)
        def _():
            for b in range(2):
                @pl.when(b < used)
                def _():
                    gather(b, b, False)

        @pl.when(i < used)
        def _():
            wait_gather(slot)

        @pl.when(i + 2 < used)
        def _():
            gather(i + 2, (i + 2) % n_buf, False)

        @pl.when((i >= 1) & (i - 1 < used))
        def _():
            scatter(i - 1, (i - 1) % n_buf, False)

        @pl.when(i < used)
        def _():
            expert(slot)


def moe_experts(x1_tiles, n_tokens, stride, asg, blk_e, n_used, e_gate, e_up, e_down):
    d = e_gate.shape[2]
    assert d == TILE_ROWS * LANES and x1_tiles.shape == (stride * TILE_ROWS, LANES)
    n_blocks = blk_e.shape[0]
    ff = e_gate.shape[-1]
    n_asg = 2 * stride
    n_rows = n_asg + e_gate.shape[1] * EXPERT_BLOCK
    weight = lambda shape: pl.BlockSpec(
        (1, 1) + shape, lambda i, asg, be, nu: (0, be[jnp.minimum(i, n_blocks - 1)], 0, 0))
    return pl.pallas_call(
        functools.partial(_moe_expert_kernel, n_tokens, n_asg), name="moe_expert",
        grid_spec=pltpu.PrefetchScalarGridSpec(
            num_scalar_prefetch=3,
            grid=(n_blocks + DRAIN_STEPS,),
            in_specs=[pl.BlockSpec(memory_space=pl.ANY), weight((d, ff)), weight((d, ff)), weight((ff, d))],
            out_specs=pl.BlockSpec(memory_space=pl.ANY),
            scratch_shapes=[pltpu.VMEM((MOE_BUFFERS, EXPERT_BLOCK * TILE_ROWS, LANES), F32),
                            pltpu.VMEM((MOE_BUFFERS, EXPERT_BLOCK * TILE_ROWS, LANES), F32),
                            pltpu.VMEM((d, ff), BF16), pltpu.VMEM((d, ff), BF16), pltpu.VMEM((ff, d), BF16),
                            pltpu.SemaphoreType.DMA((MOE_BUFFERS,)), pltpu.SemaphoreType.DMA((MOE_BUFFERS,))]),
        out_shape=jax.ShapeDtypeStruct((n_rows * TILE_ROWS, LANES), F32),
        compiler_params=_params(("arbitrary",)),
    )(asg, blk_e, n_used, x1_tiles, e_gate, e_up, e_down)


def _moe_combine_kernel(alpha, lead_tiles, y0_ref, y1_ref, x1_ref, gate_ref, g_ref, b_ref, lead_ref, tail_ref):
    i = pl.program_id(0)
    gate = gate_ref[...]
    tm = x1_ref.shape[0]
    moe = gate[:, 0:1] * _rows_from_tiles(y0_ref, tm) + gate[:, 1:2] * _rows_from_tiles(y1_ref, tm)
    out = _ln(alpha * x1_ref[...] + moe, g_ref[...], b_ref[...])

    @pl.when(i < lead_tiles)
    def _():
        lead_ref[...] = out

    @pl.when(i >= lead_tiles)
    def _():
        tail_ref[...] = out[:tail_ref.shape[0]]


def moe_combine(y, x1, gate, g, b, tm, alpha, n, n_lead):
    stride, d = x1.shape
    assert n_lead % tm == 0 and stride - n_lead == tm and n_lead < n <= stride
    lead_tiles = n_lead // tm
    return pl.pallas_call(
        functools.partial(_moe_combine_kernel, alpha, lead_tiles), name="moe_combine",
        grid=(stride // tm,),
        in_specs=[pl.BlockSpec((tm * TILE_ROWS, LANES), lambda i: (i, 0)),
                  pl.BlockSpec((tm * TILE_ROWS, LANES), lambda i: (i + stride // tm, 0)),
                  pl.BlockSpec((tm, d), lambda i: (i, 0)),
                  pl.BlockSpec((tm, LANES), lambda i: (i, 0)), _const_spec((1, d)), _const_spec((1, d))],
        out_specs=[pl.BlockSpec((tm, d), lambda i: (jnp.minimum(i, lead_tiles - 1), 0)),
                   pl.BlockSpec((n - n_lead, d), lambda i: (0, 0))],
        out_shape=[jax.ShapeDtypeStruct((n_lead, d), F32), jax.ShapeDtypeStruct((n - n_lead, d), F32)],
        compiler_params=_params(("arbitrary",)),
    )(y, y, x1, gate, g.reshape(1, d), b.reshape(1, d))


def moe_routing(experts, stride):
    top_k, n = experts.shape
    n_exp = N_GROUPS * EXPERTS_PER_GROUP
    a = n * top_k
    flat_e = experts.reshape(a)
    order = jnp.argsort(flat_e, stable=True).astype(jnp.int32)
    order = order + (order // n) * (stride - n)
    counts = jnp.sum((flat_e[:, None] == jnp.arange(n_exp, dtype=jnp.int32)[None, :]).astype(jnp.int32), axis=0)
    ends = jnp.cumsum(counts)
    padded = (counts + EXPERT_BLOCK - 1) // EXPERT_BLOCK * EXPERT_BLOCK
    pad_end = jnp.cumsum(padded)
    n_blocks = -(-a // EXPERT_BLOCK) + n_exp
    blk_start = jnp.arange(n_blocks, dtype=jnp.int32) * EXPERT_BLOCK
    blk_e = jnp.minimum(jnp.sum((pad_end[None, :] <= blk_start[:, None]).astype(jnp.int32), axis=1), n_exp - 1)
    is_e = blk_e[:, None] == jnp.arange(n_exp, dtype=jnp.int32)[None, :]
    per_block = lambda table: jnp.sum(jnp.where(is_e, table[None, :], 0), axis=1, keepdims=True)
    slot = blk_start[:, None] + jnp.arange(EXPERT_BLOCK, dtype=jnp.int32)[None, :]
    rank = slot - per_block(pad_end - padded)
    spare = top_k * stride + jnp.clip(slot - per_block(ends), 0, n_exp * EXPERT_BLOCK - 1)
    asg = jnp.where(rank < per_block(counts), order[jnp.clip(per_block(ends - counts) + rank, 0, a - 1)], spare)
    n_used = (pad_end[-1:] // EXPERT_BLOCK).astype(jnp.int32)
    return asg.reshape(-1).astype(jnp.int32), blk_e.astype(jnp.int32), n_used


def hier_moe_ln(x1, x1_tiles, eidx, gate, w, tm, alpha, n_tokens, n_lead):
    stride = x1.shape[0]
    asg, blk_e, n_used = moe_routing(eidx[:2, :n_tokens], stride)
    y = moe_experts(x1_tiles, n_tokens, stride, asg, blk_e, n_used, w['e_gate'], w['e_up'], w['e_down'])
    return moe_combine(y, x1, gate, w['ln2_g'], w['ln2_b'], tm, alpha, n_tokens, n_lead)


def kernel(x_prompt, x_sample, mem_prompt, state_wkv, state_shift, cache_win_k, cache_win_v, cache_mem_k, cache_mem_v, ln_in_g, ln_in_b, w_in, mu, w0, w_up, a0, a_up, g_up, k_k, k_a, r_k, lnx_g, lnx_b, sinks, w_mem_kv, p_a, p_b, p_m, w_o, ln1_g, ln1_b, w_group, b_group, w_router, b_router, e_gate, e_up, e_down, ln2_g, ln2_b):
    depth = w_in.shape[0]
    assert depth == 1, "single-layer step"
    bsz, seq, d = x_prompt.shape
    dec = x_sample.shape[0]
    assert x_sample.shape[1] == 1
    c_shift = mu.shape[-1]
    c_a = w0.shape[-1]
    window, kv_w = cache_win_k.shape[2], cache_win_k.shape[3] * cache_win_k.shape[4]
    n_mem, mem_w = cache_mem_k.shape[2], cache_mem_k.shape[3] * cache_mem_k.shape[4]
    q_w = sinks.shape[-1] * HEAD
    qkvm_w = q_w + 2 * kv_w + mem_w
    alpha = (2.0 * depth) ** 0.25
    past_len = float(PAST_LEN)
    chunk = WKV_CHUNK

    w_in_b = w_in[0].astype(BF16)
    w_parts = [w_in_b[:, :c_shift], w_in_b[:, c_shift:c_shift + qkvm_w], w_in_b[:, c_shift + qkvm_w:]]
    rp = dict(mu=mu[0], w0=w0[0], w_up=w_up[0], a0=a0[0], a_up=a_up[0], g_up=g_up[0], k_k=k_k[0], k_a=k_a[0],
              r_k=r_k[0].reshape(-1))
    n_route = N_GROUPS * (1 + EXPERTS_PER_GROUP)
    mw = dict(ln_in_g=ln_in_g.reshape(1, d), ln_in_b=ln_in_b.reshape(1, d), p_a=p_a[0].astype(BF16),
              p_b=p_b[0].astype(BF16), p_m=p_m[0].astype(BF16), w_o=w_o[0].astype(BF16),
              ln1_g=ln1_g[0].reshape(1, d), ln1_b=ln1_b[0].reshape(1, d),
              w_route=jnp.stack(_pieces(jnp.pad(jnp.concatenate([w_group[0], w_router[0]], axis=1),
                                                ((0, 0), (0, LANES - n_route))), 2)),
              b_route=jnp.pad(jnp.concatenate([b_group[0], b_router[0]]), (0, LANES - n_route)).reshape(1, LANES),
              e_gate=e_gate, e_up=e_up, e_down=e_down, ln2_g=ln2_g[0], ln2_b=ln2_b[0])

    xp = x_prompt.reshape(bsz * seq, d)
    zq, zg, shift_p, prep = ln_proj_prep(x_prompt, ln_in_g, ln_in_b, w_parts, jnp.zeros((bsz, 1, c_shift), F32), rp,
                                         chunk, ROW_TILE, BF16)
    o_a, wkv_p = wkv(prep, jnp.zeros((bsz, c_a // HEAD, HEAD, HEAD), F32), lnx_g[0], lnx_b[0], chunk)
    mkv = proj(mem_prompt.reshape(bsz * n_mem, d), w_mem_kv[0].astype(BF16), ROW_TILE).reshape(bsz, n_mem, 2 * mem_w)
    mk_p, mv_p = mkv[..., :mem_w], mkv[..., mem_w:]
    tables = rope_tables(jnp.arange(seq, dtype=F32))
    o_b, o_m, k_rot = attn_prompt(zq, mk_p, mv_p, sinks[0], tables, window, q_w, kv_w)
    n_all = bsz * seq + dec
    n_buf = -(-n_all // ROW_TILE) * ROW_TILE
    routed = merge(xp, o_a.reshape(-1, c_a), o_b.reshape(-1, q_w), o_m.reshape(-1, mem_w),
                   zg.reshape(bsz * seq, -1), mw, ROW_TILE, alpha, n_buf)
    shift_p = shift_p[:, 0]
    kb_p = k_rot[:, -window:].reshape(bsz, window, H_KV, HEAD)
    vb_p = zq[:, -window:, q_w + kv_w:q_w + 2 * kv_w].reshape(bsz, window, H_KV, HEAD)

    xs = x_sample.reshape(dec, d)
    zq_s, zg_s, zr_s, ops_s = ln_proj_prep(xs.reshape(1, dec, d), ln_in_g, ln_in_b, w_parts,
                                           state_shift[0].reshape(1, dec, c_shift), rp, 1, dec, F32)
    zq_s, zg_s, zr_s = zq_s[0], zg_s[0], zr_s[0]
    o_a_s, wkv_s = wkv_step([a.reshape(dec, c_a) for a in ops_s], state_wkv[0], lnx_g[0], lnx_b[0], STEP_BATCH)
    tables_s = rope_tables(jnp.full((1,), past_len, F32))
    o_b_s, o_m_s, nk_s, nv_s = attn_step(
        zq_s, cache_win_k[0].reshape(dec, window, kv_w), cache_win_v[0].reshape(dec, window, kv_w),
        cache_mem_k[0].reshape(dec, n_mem, mem_w), cache_mem_v[0].reshape(dec, n_mem, mem_w),
        sinks[0], tables_s, q_w, kv_w, STEP_BATCH)
    x1, x1_tiles, eidx, gate = merge(xs, o_a_s, o_b_s, o_m_s, zg_s, mw, dec, alpha, n_buf, bsz * seq, routed)

    y_prompt, y_sample = hier_moe_ln(x1, x1_tiles, eidx, gate, mw, ROW_TILE, alpha, n_all, bsz * seq)
    y_prompt = y_prompt.reshape(bsz, seq, d)
    y_sample = y_sample.reshape(dec, 1, d)

    sd = state_wkv.dtype
    return (y_prompt, y_sample, wkv_p[None].astype(sd), wkv_s[None].astype(sd), shift_p[None], zr_s[None],
            kb_p[None], vb_p[None], nk_s.reshape(dec, window, H_KV, HEAD)[None],
            nv_s.reshape(dec, window, H_KV, HEAD)[None],
            mk_p.reshape(bsz, n_mem, -1, HEAD)[None], mv_p.reshape(bsz, n_mem, -1, HEAD)[None])
```

```python
import functools
import math

import jax
import jax.numpy as jnp
from jax import lax
from jax.experimental import pallas as pl
from jax.experimental.pallas import tpu as pltpu

F32 = jnp.float32
BF16 = jnp.bfloat16

ROW_TILE = 256
WKV_CHUNK = 64
CHUNKS_PER_STEP = 8
SCAN_BATCH = 8
ATTN_BLOCKS = 4
STEP_BATCH = 16
PROJ_PIECE = 512

HEAD = 64
LANES = 128
H_KV = 2
ROT_HALF = 8
ROPE_THETA = 500000.0
PAST_LEN = 8192
N_GROUPS = 4
EXPERTS_PER_GROUP = 8
EXPERT_BLOCK = 128
LN_EPS = 1e-5
LNX_EPS = 64e-5
NEG_INF = -1e30
VMEM_LIMIT = 48 * 1024 * 1024


def _pieces(x, n):
    out = []
    for _ in range(n):
        p = x.astype(BF16)
        out.append(p)
        x = x - p.astype(F32)
    return out


def _mask_dot(x, mask, n=2):
    return sum(jnp.dot(p, mask, preferred_element_type=F32) for p in _pieces(x, n))


def _split3(x, axis, lhs):
    hi = x.astype(BF16).astype(F32)
    lo = x - hi
    return jnp.concatenate([hi, hi, lo] if lhs else [hi, lo, hi], axis=axis).astype(BF16)


def _dot3(a, b):
    return jnp.dot(_split3(a, 1, True), _split3(b, 0, False), preferred_element_type=F32)


def _dot3_t(a, b):
    return lax.dot_general(_split3(a, 1, True), _split3(b, 1, False), (((1,), (1,)), ((), ())),
                           preferred_element_type=F32)


def _bdot(a, b):
    return jnp.dot(a.astype(BF16), b.astype(BF16), preferred_element_type=F32)


def _bdot_t(a, b):
    return lax.dot_general(a.astype(BF16), b.astype(BF16), (((1,), (1,)), ((), ())), preferred_element_type=F32)


def _sigmoid(x):
    return 0.5 * jnp.tanh(0.5 * x) + 0.5


def _ln(x, g, b):
    mu = jnp.mean(x, axis=-1, keepdims=True)
    xc = x - mu
    var = jnp.mean(xc * xc, axis=-1, keepdims=True)
    return xc * lax.rsqrt(var + LN_EPS) * g + b


def _const_spec(shape):
    nd = len(shape)
    return pl.BlockSpec(shape, lambda *_: (0,) * nd)


def _params(sem):
    return pltpu.CompilerParams(dimension_semantics=sem, vmem_limit_bytes=VMEM_LIMIT)


def _ln_proj_prep_kernel(chunk, x_ref, g_ref, b_ref, wr_ref, wq_ref, wg_ref, prev_ref, mu_ref, w0_ref, wup_ref,
                         a0_ref, aup_ref, gup_ref, kk_ref, ka_ref, rk_ref, hsum_ref, tril_ref,
                         zq_ref, zg_ref, zr_ref, *refs):
    out_refs, carry_ref = refs[:-1], refs[-1]
    xn = _ln(x_ref[0], g_ref[...], b_ref[...]).astype(BF16)
    z = jnp.dot(xn, wr_ref[...], preferred_element_type=F32)
    pending = [(o_ref, w_ref, c) for o_ref, w_ref in ((zq_ref, wq_ref), (zg_ref, wg_ref))
               for c in range(0, w_ref.shape[1], PROJ_PIECE)]

    def project(n_pieces):
        for _ in range(min(n_pieces, len(pending))):
            o_ref, w_ref, c = pending.pop(0)
            o_ref[0, :, c:c + PROJ_PIECE] = jnp.dot(xn, w_ref[:, c:c + PROJ_PIECE],
                                                    preferred_element_type=F32).astype(o_ref.dtype)

    tt = z.shape[0]
    c_a = w0_ref.shape[-1]
    r_w, r_a, r_g = wup_ref.shape[0], aup_ref.shape[0], gup_ref.shape[0]
    if chunk == 1:
        zr_ref[0] = z
        prev = prev_ref[0]
    else:
        zr_ref[0] = z[tt - 1:tt, :]

        @pl.when(pl.program_id(1) == 0)
        def _():
            carry_ref[...] = prev_ref[0]

        row = lax.broadcasted_iota(jnp.int32, z.shape, 0)
        prev = jnp.where(row == 0, carry_ref[...], pltpu.roll(z, 1, 0))
        carry_ref[...] = z[tt - 1:tt, :]
    zs = z + (prev - z) * mu_ref[...]
    r = zs[:, :c_a]
    k = zs[:, c_a:2 * c_a]
    v = zs[:, 2 * c_a:3 * c_a]
    o = 3 * c_a
    xw = zs[:, o:o + r_w]
    xa = zs[:, o + r_w:o + r_w + r_a]
    xg = zs[:, o + r_w + r_a:o + r_w + r_a + r_g]
    project(1)
    warg = -(w0_ref[...] + _dot3(jnp.tanh(xw), wup_ref[...]))
    softplus = jnp.maximum(warg, 0.0) + jnp.log1p(jnp.exp(-jnp.abs(warg)))
    lw = -jnp.exp(-softplus - 0.5)
    project(1)
    a = _sigmoid(a0_ref[...] + _dot3(xa, aup_ref[...]))
    g = _dot3(_sigmoid(xg), gup_ref[...])
    project(1)
    kkr = k * kk_ref[...]
    kk = kkr / jnp.maximum(jnp.sqrt(_mask_dot(kkr * kkr, hsum_ref[...])), 1e-12)
    project(1)
    k2 = k * (1.0 + (a - 1.0) * ka_ref[...])
    bonus = _mask_dot(r * k2 * rk_ref[...], hsum_ref[...]) * v
    kb = kk * a
    project(1)
    if chunk == 1:
        outs = (r, jnp.exp(lw), k2, v, -kk, kb, g, bonus)
        for ref, val in zip(out_refs, outs):
            ref[0] = val
        project(len(pending))
        return
    pieces = _pieces(lw, 3)
    ones3 = jnp.ones((chunk, 3 * chunk), BF16)
    cw, cwl = [], []
    for c in range(tt // chunk):
        stack = jnp.concatenate([p[c * chunk:(c + 1) * chunk] for p in pieces], axis=0)
        cw.append(jnp.dot(tril_ref[...], stack, preferred_element_type=F32))
        cwl.append(jnp.dot(ones3, stack, preferred_element_type=F32))
    cw = jnp.concatenate(cw, axis=0)
    cwl = jnp.concatenate(cwl, axis=0)
    project(1)
    e_inv = jnp.exp(-cw)
    outs = (r * jnp.exp(cw), -kk * jnp.exp(cw - lw), kb * e_inv, k2 * e_inv, v, g, bonus)
    for ref, val in zip(out_refs[:-1], outs):
        ref[0] = val
        project(1)
    wl = jnp.exp(cwl)
    for c in range(tt // chunk):
        out_refs[-1][0, c] = wl[c * chunk:c * chunk + 1, :]
    project(len(pending))


def _head_sum_matrix(width):
    idx = jnp.arange(width)
    return ((idx[:, None] // HEAD) == (idx[None, :] // HEAD)).astype(BF16)


def ln_proj_prep(x, g, b, ws, prev, p, chunk, tt, gate_dtype):
    bsz, t, d = x.shape
    w_r, w_q, w_g = ws
    cs = w_r.shape[1]
    c_a = p['w0'].shape[-1]
    assert prev.shape[1] == (t if chunk == 1 else 1)
    ridx = jnp.arange(chunk)
    tril = jnp.tile((ridx[None, :] <= ridx[:, None]).astype(BF16), (1, 3))
    hsum = _head_sum_matrix(c_a)
    row = lambda a: a.reshape(1, -1)
    rows = lambda width: pl.BlockSpec((1, tt, width), lambda bi, i: (bi, i, 0))
    full = jax.ShapeDtypeStruct((bsz, t, c_a), F32)
    if chunk == 1:
        out_specs, out_shape = [rows(c_a)] * 8, [full] * 8
        prev_spec, zr_spec, zr_rows = rows(cs), rows(cs), t
    else:
        out_specs = [rows(c_a)] * 7 + [pl.BlockSpec((1, tt // chunk, 1, c_a), lambda bi, i: (bi, i, 0, 0))]
        out_shape = [full] * 7 + [jax.ShapeDtypeStruct((bsz, t // chunk, 1, c_a), F32)]
        prev_spec = zr_spec = pl.BlockSpec((1, 1, cs), lambda bi, i: (bi, 0, 0))
        zr_rows = 1
    consts = [row(p['mu']), row(p['w0']), p['w_up'], row(p['a0']), p['a_up'], p['g_up'], row(p['k_k']),
              row(p['k_a']), row(p['r_k']), hsum, tril]
    weight = lambda w: pl.BlockSpec(w.shape, lambda bi, i: (0, 0), pipeline_mode=pl.Buffered(1))
    outs = pl.pallas_call(
        functools.partial(_ln_proj_prep_kernel, chunk), name="ln_proj_prep",
        grid=(bsz, t // tt),
        in_specs=[rows(d), _const_spec((1, d)), _const_spec((1, d)), weight(w_r), weight(w_q), weight(w_g), prev_spec]
        + [_const_spec(c.shape) for c in consts],
        out_specs=[rows(w_q.shape[1]), rows(w_g.shape[1]), zr_spec] + out_specs,
        out_shape=[jax.ShapeDtypeStruct((bsz, t, w_q.shape[1]), F32),
                   jax.ShapeDtypeStruct((bsz, t, w_g.shape[1]), gate_dtype),
                   jax.ShapeDtypeStruct((bsz, zr_rows, cs), F32)] + out_shape,
        scratch_shapes=[pltpu.VMEM((1, cs), F32)],
        compiler_params=_params(("parallel", "arbitrary")),
    )(x, row(g), row(b), w_r, w_q, w_g, prev, *consts)
    return outs[0], outs[1], outs[2], outs[3:]


def _wkv_chunk_kernel(ra_ref, at_ref, bt_ref, kt_ref, v_ref, wl_ref, rp_ref, y0_ref, m_ref, n_ref):
    n_chunks = m_ref.shape[1]
    length = ra_ref.shape[1] // n_chunks
    heads = ra_ref.shape[2] // HEAD
    row = lax.broadcasted_iota(jnp.int32, (length, length), 0)
    col = lax.broadcasted_iota(jnp.int32, (length, length), 1)
    strict = row > col
    incl = row >= col
    hrow = lax.broadcasted_iota(jnp.int32, (HEAD, HEAD), 0)
    hcol = lax.broadcasted_iota(jnp.int32, (HEAD, HEAD), 1)
    units = [(c, h) for c in range(n_chunks) for h in range(heads)]
    us = range(len(units))
    at = lambda ref, u: ref[0, units[u][0] * length:(units[u][0] + 1) * length,
                            units[u][1] * HEAD:(units[u][1] + 1) * HEAD]
    gram = [_bdot_t(jnp.concatenate([at(at_ref, u), at(ra_ref, u)], axis=0),
                    jnp.concatenate([at(bt_ref, u), at(kt_ref, u)], axis=0)) for u in us]
    a_ab = [jnp.where(strict, g[:length, :length], 0.0) for g in gram]
    a_kk = [jnp.concatenate([jnp.where(strict, g[:length, length:], 0.0),
                             jnp.where(incl, g[length:, length:], 0.0)], axis=0) for g in gram]
    a_rb = [jnp.where(incl, g[length:, :length], 0.0) for g in gram]
    inv = [jnp.where(row == col, 1.0, a) for a in a_ab]
    pw = [_bdot(a, a) for a in a_ab]
    avy = [_bdot(a_kk[u], at(v_ref, u)) for u in us]
    w_l = [wl_ref[0, units[u][0], :, units[u][1] * HEAD:(units[u][1] + 1) * HEAD] for u in us]
    nk = [_bdot(at(v_ref, u).T, at(kt_ref, u) * w_l[u]) for u in us]
    for _ in range(int(math.log2(length)) - 2):
        both = [_bdot(jnp.concatenate([pw[u], inv[u]], axis=0), pw[u]) for u in us]
        pw = [b[:length] for b in both]
        inv = [inv[u] + both[u][length:] for u in us]
    inv = [inv[u] + _bdot(inv[u], pw[u]) for u in us]
    pq = [_bdot(inv[u], jnp.concatenate([at(at_ref, u), avy[u][:length]], axis=1)) for u in us]
    ry = [_bdot(a_rb[u], pq[u]) for u in us]
    mn = [_bdot(pq[u].T, at(bt_ref, u) * w_l[u]) for u in us]
    lanes = lambda f, c: jnp.concatenate([f(c * heads + h) for h in range(heads)], axis=1)
    rows = lambda f: jnp.concatenate([lanes(f, c) for c in range(n_chunks)], axis=0)
    rp_ref[0] = rows(lambda u: at(ra_ref, u) + ry[u][:, :HEAD])
    y0_ref[0] = rows(lambda u: avy[u][length:] + ry[u][:, HEAD:])
    for c in range(n_chunks):
        m_ref[0, c] = lanes(lambda u: jnp.where(hrow == hcol, w_l[u], 0.0) + mn[u][:HEAD], c)
        n_ref[0, c] = lanes(lambda u: nk[u] + mn[u][HEAD:], c)


def _head_norm_wide(y, hsum):
    yc = y - _mask_dot(y, hsum) * (1.0 / HEAD)
    return yc * lax.rsqrt(_mask_dot(yc * yc, hsum) * (1.0 / HEAD) + LNX_EPS)


def _wkv_scan_kernel(rp_ref, y0_ref, m_ref, n_ref, g_ref, bonus_ref, s0_ref, lg_ref, lb_ref, hsum_ref,
                     o_ref, sout_ref, s_ref):
    c = pl.program_id(1)
    nb = rp_ref.shape[0]
    heads = rp_ref.shape[2] // HEAD

    @pl.when(c == 0)
    def _():
        s_ref[...] = s0_ref[...]

    pairs = [(b, h, slice(h * HEAD, (h + 1) * HEAD)) for b in range(nb) for h in range(heads)]
    s = [s_ref[b, h] for b, h, _ in pairs]
    s_new = [n_ref[b, 0, :, sl] + _dot3(s[j], m_ref[b, 0, :, sl]) for j, (b, h, sl) in enumerate(pairs)]
    ys = [y0_ref[b, :, sl] + _bdot_t(rp_ref[b, :, sl], s[j]) for j, (b, h, sl) in enumerate(pairs)]
    for j, (b, h, _) in enumerate(pairs):
        s_ref[b, h] = s_new[j]
    length = rp_ref.shape[1]
    y = _head_norm_wide(jnp.concatenate([jnp.concatenate(ys[b * heads:(b + 1) * heads], axis=-1) for b in range(nb)],
                                        axis=0), hsum_ref[...])
    for b in range(nb):
        o = (y[b * length:(b + 1) * length] * lg_ref[...] + lb_ref[...] + bonus_ref[b]) * g_ref[b]
        o_ref[b] = o.astype(o_ref.dtype)

    @pl.when(c == pl.num_programs(1) - 1)
    def _():
        sout_ref[...] = s_ref[...]


def wkv(prep, s0, lnx_g, lnx_b, chunk):
    ra, at, bt, kt, v, g, bonus, wl = prep
    bsz, t, c_a = ra.shape
    heads = c_a // HEAD
    n_chunks = t // chunk
    assert chunk & (chunk - 1) == 0 and chunk >= 4 and t % chunk == 0
    per_step = CHUNKS_PER_STEP if n_chunks % CHUNKS_PER_STEP == 0 else 1
    tile = pl.BlockSpec((1, per_step * chunk, c_a), lambda b, c: (b, c, 0))
    mat = pl.BlockSpec((1, per_step, HEAD, c_a), lambda b, c: (b, c, 0, 0))
    full = jax.ShapeDtypeStruct((bsz, t, c_a), F32)
    mats = jax.ShapeDtypeStruct((bsz, n_chunks, HEAD, c_a), F32)
    rp, y0, m, n = pl.pallas_call(
        _wkv_chunk_kernel, name="wkv_chunk",
        grid=(bsz, n_chunks // per_step),
        in_specs=[tile] * 5 + [pl.BlockSpec((1, per_step, 1, c_a), lambda b, c: (b, c, 0, 0))],
        out_specs=[tile, tile, mat, mat],
        out_shape=[full, full, mats, mats],
        compiler_params=_params(("parallel", "parallel")),
    )(ra, at, bt, kt, v, wl)
    nb = SCAN_BATCH if bsz % SCAN_BATCH == 0 else 1
    tile = pl.BlockSpec((nb, chunk, c_a), lambda b, c: (b, c, 0))
    mat = pl.BlockSpec((nb, 1, HEAD, c_a), lambda b, c: (b, c, 0, 0))
    st = pl.BlockSpec((nb, heads, HEAD, HEAD), lambda b, c: (b, 0, 0, 0))
    return pl.pallas_call(
        _wkv_scan_kernel, name="wkv_scan",
        grid=(bsz // nb, n_chunks),
        in_specs=[tile, tile, mat, mat, tile, tile, st, _const_spec((1, c_a)), _const_spec((1, c_a)),
                  _const_spec((c_a, c_a))],
        out_specs=[tile, st],
        out_shape=[jax.ShapeDtypeStruct((bsz, t, c_a), BF16), jax.ShapeDtypeStruct(s0.shape, F32)],
        scratch_shapes=[pltpu.VMEM((nb, heads, HEAD, HEAD), F32)],
        compiler_params=_params(("parallel", "arbitrary")),
    )(rp, y0, m, n, g, bonus, s0, lnx_g.reshape(1, c_a), lnx_b.reshape(1, c_a), _head_sum_matrix(c_a))


def _wkv_step_kernel(r_ref, w_ref, k_ref, v_ref, a_ref, b_ref, g_ref, bonus_ref, s_ref, lg_ref, lb_ref, hsum_ref,
                     o_ref, sout_ref):
    bt, heads = s_ref.shape[0], s_ref.shape[1]
    c_a = heads * HEAD
    hsum = hsum_ref[...]
    diag = (lax.broadcasted_iota(jnp.int32, (HEAD, c_a), 1) % HEAD
            == lax.broadcasted_iota(jnp.int32, (HEAD, c_a), 0))
    seqs = range(bt)
    row = lambda ref, i: ref[i:i + 1, :]
    stack = lambda f: jnp.concatenate([f(i) for i in seqs], axis=0)
    piece = lambda x, i: x[i * HEAD:(i + 1) * HEAD]
    s = stack(lambda i: jnp.concatenate([s_ref[i, h] for h in range(heads)], axis=1))
    sa = _mask_dot(s * stack(lambda i: jnp.broadcast_to(row(a_ref, i), (HEAD, c_a))), hsum)
    v_rows = _mask_dot(stack(lambda i: jnp.where(diag, row(v_ref, i), 0.0)), hsum)
    s = stack(lambda i: piece(s, i) * row(w_ref, i) + piece(sa, i) * row(b_ref, i) + piece(v_rows, i) * row(k_ref, i))
    for i in seqs:
        for h in range(heads):
            sout_ref[i, h] = piece(s, i)[:, h * HEAD:(h + 1) * HEAD]
    y_rows = _mask_dot(stack(lambda i: piece(s, i) * row(r_ref, i)), hsum)
    y = stack(lambda i: jnp.sum(jnp.where(diag, piece(y_rows, i), 0.0), axis=0, keepdims=True))
    y = _head_norm_wide(y, hsum)
    o_ref[...] = (y * lg_ref[...] + lb_ref[...] + bonus_ref[...]) * g_ref[...]


def wkv_step(ops, s0, lnx_g, lnx_b, bt):
    bsz, c_a = ops[0].shape
    rows = pl.BlockSpec((bt, c_a), lambda i: (i, 0))
    st = pl.BlockSpec((bt,) + s0.shape[1:], lambda i: (i, 0, 0, 0))
    hsum = _head_sum_matrix(c_a)
    return pl.pallas_call(
        _wkv_step_kernel, name="wkv_step",
        grid=(bsz // bt,),
        in_specs=[rows] * 8 + [st, _const_spec((1, c_a)), _const_spec((1, c_a)), _const_spec(hsum.shape)],
        out_specs=[rows, st],
        out_shape=[jax.ShapeDtypeStruct((bsz, c_a), F32), jax.ShapeDtypeStruct(s0.shape, F32)],
        compiler_params=_params(("parallel",)),
    )(*ops, s0, lnx_g.reshape(1, c_a), lnx_b.reshape(1, c_a), hsum)


def rope_tables(pos):
    inv_freq = ROPE_THETA ** (-jnp.arange(ROT_HALF, dtype=F32) / ROT_HALF)
    ang = pos[:, None] * inv_freq[None, :]
    cos, sin = jnp.cos(ang), jnp.sin(ang)
    t = pos.shape[0]
    rest = HEAD - 2 * ROT_HALF
    c = jnp.concatenate([cos, cos, jnp.ones((t, rest), F32)], axis=1)
    s1 = jnp.concatenate([jnp.zeros((t, ROT_HALF), F32), sin, jnp.zeros((t, rest), F32)], axis=1)
    s2 = jnp.concatenate([-sin, jnp.zeros((t, HEAD - ROT_HALF), F32)], axis=1)
    rep = LANES // HEAD
    return tuple(jnp.tile(x, (1, rep)) for x in (c, s1, s2))


def _rope(x, c, s1, s2):
    width = x.shape[-1]
    rep = width // c.shape[-1]
    if rep > 1:
        c, s1, s2 = (jnp.concatenate([t] * rep, axis=1) for t in (c, s1, s2))
    return x * c + pltpu.roll(x, ROT_HALF, 1) * s1 + pltpu.roll(x, width - ROT_HALF, 1) * s2


def _attn_kernel(window, q_w, kv_w, zq_ref, kp_ref, vp_ref, rc_ref, rs1_ref, rs2_ref, pc_ref, ps1_ref, ps2_ref,
                 mk_ref, mv_ref, sink_ref, ob_ref, om_ref, kr_ref):
    zq = zq_ref[0]
    nblk = zq.shape[0] // window
    first = pl.program_id(1) * nblk
    scale = HEAD ** -0.5
    q = zq[:, :q_w]
    k = zq[:, q_w:q_w + kv_w]
    v = zq[:, q_w + kv_w:q_w + 2 * kv_w]
    qm = zq[:, q_w + 2 * kv_w:]
    qr = _rope(q, rc_ref[...], rs1_ref[...], rs2_ref[...]) * scale
    kr = _rope(k, rc_ref[...], rs1_ref[...], rs2_ref[...])
    kr_ref[0] = kr
    k_all = jnp.concatenate([_rope(kp_ref[0], pc_ref[...], ps1_ref[...], ps2_ref[...]), kr], axis=0)
    v_all = jnp.concatenate([vp_ref[0], v], axis=0)
    gqa = q_w // kv_w
    assert window & (window - 1) == 0
    qi = lax.broadcasted_iota(jnp.int32, (gqa * window, 2 * window), 0) & (window - 1)
    kj = lax.broadcasted_iota(jnp.int32, (gqa * window, 2 * window), 1)
    band = (kj > qi) & (kj <= qi + window)
    tdot = lambda a, b: lax.dot_general(a, b, (((1,), (1,)), ((), ())), preferred_element_type=F32)
    dot = lambda a, b: jnp.dot(a, b, preferred_element_type=F32)
    hsl = lambda h: slice(h * HEAD, (h + 1) * HEAD)
    rows = lambda i: slice(i * window, (i + 1) * window)
    keys = lambda i: slice(i * window, (i + 2) * window)
    wins = [(i, hk) for i in range(nblk) for hk in range(kv_w // HEAD)]
    mems = [(i, h) for i in range(nblk) for h in range(qm.shape[1] // HEAD)]
    qmb = (qm * scale).astype(BF16)
    mk = mk_ref[0].astype(BF16)
    mv = mv_ref[0].astype(BF16)
    kcat = [k_all[keys(i), hsl(hk)].astype(BF16) for i, hk in wins]
    vcat = [v_all[keys(i), hsl(hk)].astype(BF16) for i, hk in wins]
    qs = [jnp.concatenate([qr[rows(i), hsl(hk * gqa + g)] for g in range(gqa)], axis=0).astype(BF16)
          for i, hk in wins]
    s = [jnp.where(band & ((kj >= window) | (first + i > 0)), tdot(qs[u], kcat[u]), NEG_INF)
         for u, (i, hk) in enumerate(wins)]
    sm = [tdot(qmb[rows(i), hsl(h)], mk[:, hsl(h)]) for i, h in mems]
    sink = [jnp.concatenate([jnp.full((window, 1), sink_ref[hk * gqa + g], F32) for g in range(gqa)], axis=0)
            for i, hk in wins]
    m = [jnp.maximum(jnp.max(s[u], axis=-1, keepdims=True), sink[u]) for u in range(len(wins))]
    p = [jnp.exp(s[u] - m[u]) for u in range(len(wins))]
    pm = [jnp.exp(x - jnp.max(x, axis=-1, keepdims=True)) for x in sm]
    o = [dot(p[u].astype(BF16), vcat[u])
         / (jnp.sum(p[u], axis=-1, keepdims=True) + jnp.exp(sink[u] - m[u])) for u in range(len(wins))]
    om = [dot(pm[u].astype(BF16), mv[:, hsl(h)]) / jnp.sum(pm[u], axis=-1, keepdims=True)
          for u, (i, h) in enumerate(mems)]
    n_kv, n_mh = kv_w // HEAD, qm.shape[1] // HEAD
    ob = jnp.concatenate([jnp.concatenate([o[i * n_kv + hk][g * window:(g + 1) * window]
                                           for hk in range(n_kv) for g in range(gqa)], axis=1)
                          for i in range(nblk)], axis=0)
    ob_ref[0] = ob.astype(ob_ref.dtype)
    om_ref[0] = jnp.concatenate([jnp.concatenate(om[i * n_mh:(i + 1) * n_mh], axis=1) for i in range(nblk)],
                                axis=0).astype(om_ref.dtype)


def attn_prompt(zq, mk, mv, sinks, tables, window, q_w, kv_w):
    bsz, t, zw = zq.shape
    mem_w = zw - q_w - 2 * kv_w
    assert kv_w == LANES and q_w % kv_w == 0
    kcol, vcol = q_w // kv_w, q_w // kv_w + 1
    nblk = ATTN_BLOCKS if (t // window) % ATTN_BLOCKS == 0 else 1
    rows = nblk * window
    prev = lambda n: jnp.maximum(n * nblk - 1, 0)
    tab = pl.BlockSpec((rows, LANES), lambda b, n: (n, 0))
    ptab = pl.BlockSpec((window, LANES), lambda b, n: (prev(n), 0))
    mem = pl.BlockSpec((1,) + mk.shape[1:], lambda b, n: (b, 0, 0))
    return pl.pallas_call(
        functools.partial(_attn_kernel, window, q_w, kv_w), name="attn",
        grid=(bsz, t // rows),
        in_specs=[pl.BlockSpec((1, rows, zw), lambda b, n: (b, n, 0)),
                  pl.BlockSpec((1, window, kv_w), lambda b, n: (b, prev(n), kcol)),
                  pl.BlockSpec((1, window, kv_w), lambda b, n: (b, prev(n), vcol)),
                  tab, tab, tab, ptab, ptab, ptab, mem, mem,
                  pl.BlockSpec(memory_space=pltpu.SMEM)],
        out_specs=[pl.BlockSpec((1, rows, q_w), lambda b, n: (b, n, 0)),
                   pl.BlockSpec((1, rows, mem_w), lambda b, n: (b, n, 0)),
                   pl.BlockSpec((1, rows, kv_w), lambda b, n: (b, n, 0))],
        out_shape=[jax.ShapeDtypeStruct((bsz, t, q_w), BF16), jax.ShapeDtypeStruct((bsz, t, mem_w), BF16),
                   jax.ShapeDtypeStruct((bsz, t, kv_w), F32)],
        compiler_params=_params(("parallel", "parallel")),
    )(zq, zq, zq, *tables, *tables, mk, mv, sinks)


def _attn_step_kernel(q_w, kv_w, zq_ref, ck_ref, cv_ref, mk_ref, mv_ref, rc_ref, rs1_ref, rs2_ref, sink_ref,
                      ob_ref, om_ref, nk_ref, nv_ref):
    bt = zq_ref.shape[0]
    window = ck_ref.shape[1]
    mem_w = om_ref.shape[1]
    n_q, gqa, per_vreg = q_w // HEAD, q_w // kv_w, LANES // HEAD
    scale = HEAD ** -0.5
    zq = zq_ref[...]
    q = _rope(zq[:, :q_w], rc_ref[...], rs1_ref[...], rs2_ref[...]) * scale
    k_new = _rope(zq[:, q_w:q_w + kv_w], rc_ref[...], rs1_ref[...], rs2_ref[...])
    v_new = zq[:, q_w + kv_w:q_w + 2 * kv_w]
    qm = zq[:, q_w + 2 * kv_w:] * scale
    own = lambda w: (lax.broadcasted_iota(jnp.int32, (n_q, w), 1) // HEAD
                     == lax.broadcasted_iota(jnp.int32, (n_q, w), 0))
    own_q, own_m = own(q_w), own(mem_w)
    hrow = lax.broadcasted_iota(jnp.int32, (n_q, LANES), 0)
    hblk = lax.broadcasted_iota(jnp.int32, (n_q, LANES), 1) // HEAD
    swap = (hrow % per_vreg) != (hrow // gqa)
    keep = hblk == hrow % per_vreg
    key_ok = lax.broadcasted_iota(jnp.int32, (n_q, window), 1) >= 1
    wrow = lax.broadcasted_iota(jnp.int32, (window, kv_w), 0)
    sink = sink_ref[...]
    bs = range(bt)
    tdot = lambda a, b: lax.dot_general(a.astype(BF16), b.astype(BF16), (((1,), (1,)), ((), ())),
                                        preferred_element_type=F32)
    for b in bs:
        nk_ref[b] = jnp.where(wrow == window - 1, k_new[b:b + 1], pltpu.roll(ck_ref[b], window - 1, 0))
        nv_ref[b] = jnp.where(wrow == window - 1, v_new[b:b + 1], pltpu.roll(cv_ref[b], window - 1, 0))
    q8 = []
    for b in bs:
        rep = jnp.where(own_q, q[b:b + 1], 0.0)
        fold = sum(rep[:, c * LANES:(c + 1) * LANES] for c in range(q_w // LANES))
        q8.append(jnp.where(swap, pltpu.roll(fold, HEAD, 1), fold))
    qm8 = [jnp.where(own_m, qm[b:b + 1], 0.0) for b in bs]
    s = [jnp.where(key_ok, tdot(q8[b], ck_ref[b]), NEG_INF) for b in bs]
    sm = [tdot(qm8[b], mk_ref[b]) for b in bs]
    s_new = [jnp.sum(q8[b] * k_new[b:b + 1], axis=1, keepdims=True) for b in bs]
    m = [jnp.maximum(jnp.maximum(jnp.max(s[b], axis=1, keepdims=True), s_new[b]), sink) for b in bs]
    p = [jnp.exp(s[b] - m[b]) for b in bs]
    p_new = [jnp.exp(s_new[b] - m[b]) for b in bs]
    pm = [jnp.exp(sm[b] - jnp.max(sm[b], axis=1, keepdims=True)) for b in bs]
    o8 = [(_bdot(p[b], cv_ref[b]) + p_new[b] * v_new[b:b + 1])
          / (jnp.sum(p[b], axis=1, keepdims=True) + p_new[b] + jnp.exp(sink - m[b])) for b in bs]
    om8 = [_bdot(pm[b], mv_ref[b]) / jnp.sum(pm[b], axis=1, keepdims=True) for b in bs]
    ob, om = [], []
    for b in bs:
        o = jnp.where(swap, pltpu.roll(o8[b], HEAD, 1), o8[b])
        o = jnp.concatenate([jnp.where(keep, o, 0.0)] * (q_w // LANES), axis=1)
        ob.append(jnp.sum(jnp.where(own_q, o, 0.0), axis=0, keepdims=True))
        om.append(jnp.sum(jnp.where(own_m, om8[b], 0.0), axis=0, keepdims=True))
    ob_ref[...] = jnp.concatenate(ob, axis=0)
    om_ref[...] = jnp.concatenate(om, axis=0)


def attn_step(zq, ck, cv, mk, mv, sinks, tables, q_w, kv_w, bt):
    bsz, zw = zq.shape
    mem_w = zw - q_w - 2 * kv_w
    window, n_mem = ck.shape[1], mk.shape[1]
    n_q = q_w // HEAD
    assert kv_w == LANES and mem_w // HEAD <= n_q and n_q == 8
    rows = lambda w: pl.BlockSpec((bt, w), lambda i: (i, 0))
    cache = pl.BlockSpec((bt, window, kv_w), lambda i: (i, 0, 0))
    mem = pl.BlockSpec((bt, n_mem, mem_w), lambda i: (i, 0, 0))
    consts = list(tables) + [sinks.reshape(n_q, 1)]
    const_specs = [_const_spec(c.shape) for c in consts]
    return pl.pallas_call(
        functools.partial(_attn_step_kernel, q_w, kv_w), name="attn_step",
        grid=(bsz // bt,),
        in_specs=[rows(zw), cache, cache, mem, mem] + const_specs,
        out_specs=[rows(q_w), rows(mem_w), cache, cache],
        out_shape=[jax.ShapeDtypeStruct((bsz, q_w), F32), jax.ShapeDtypeStruct((bsz, mem_w), F32),
                   jax.ShapeDtypeStruct(ck.shape, F32), jax.ShapeDtypeStruct(cv.shape, F32)],
        compiler_params=_params(("parallel",)),
    )(zq, ck, cv, mk, mv, *consts)


def _proj_kernel(x_ref, w_ref, o_ref):
    o_ref[...] = jnp.dot(x_ref[...].astype(BF16), w_ref[...], preferred_element_type=F32)


def proj(x, w, tm):
    n, d = x.shape
    return pl.pallas_call(
        _proj_kernel, name="proj",
        grid=(n // tm,),
        in_specs=[pl.BlockSpec((tm, d), lambda i: (i, 0)), _const_spec(w.shape)],
        out_specs=pl.BlockSpec((tm, w.shape[1]), lambda i: (i, 0)),
        out_shape=jax.ShapeDtypeStruct((n, w.shape[1]), F32),
        compiler_params=_params(("parallel",)),
    )(x, w)


def _merge_kernel(alpha, n_tiles, x_ref, oa_ref, ob_ref, om_ref, gt_ref, lig_ref, lib_ref, pa_ref, pb_ref, pm_ref,
                  wo_ref, l1g_ref, l1b_ref, wr_ref, br_ref, *refs):
    outs = refs[-4:]

    @pl.when(pl.program_id(0) >= n_tiles)
    def _():
        for ref in outs:
            ref[...] = jnp.zeros_like(ref)

    @pl.when(pl.program_id(0) < n_tiles)
    def _():
        _merge_tile(alpha, x_ref, oa_ref, ob_ref, om_ref, gt_ref, lig_ref, lib_ref, pa_ref, pb_ref, pm_ref, wo_ref,
                    l1g_ref, l1b_ref, wr_ref, br_ref, *outs)


def _merge_tile(alpha, x_ref, oa_ref, ob_ref, om_ref, gt_ref, lig_ref, lib_ref, pa_ref, pb_ref, pm_ref, wo_ref,
                l1g_ref, l1b_ref, wr_ref, br_ref, x1_ref, x1t_ref, eidx_ref, gate_ref):
    d = x_ref.shape[1]
    xn = _ln(x_ref[...], lig_ref[...], lib_ref[...])
    gts = _sigmoid(gt_ref[...].astype(F32))
    merged = (gts[:, :d] * _bdot(oa_ref[...], pa_ref[...]) + gts[:, d:2 * d] * _bdot(ob_ref[...], pb_ref[...])
              + gts[:, 2 * d:] * _bdot(om_ref[...], pm_ref[...]))
    x1 = _ln(alpha * xn + _bdot(merged, wo_ref[...]), l1g_ref[...], l1b_ref[...])
    x1_ref[...] = x1
    _rows_to_tiles(x1t_ref, x1)
    x_hi, x_lo = _pieces(x1, 2)
    w_hi, w_lo = wr_ref[0], wr_ref[1]
    dot = lambda a, b: jnp.dot(a, b, preferred_element_type=F32)
    logits = dot(x_hi, w_hi) + dot(x_hi, w_lo) + dot(x_lo, w_hi) + br_ref[...]
    lane = lax.broadcasted_iota(jnp.int32, logits.shape, 1)
    lane_f = lane.astype(F32)
    first = lambda hit: jnp.min(jnp.where(hit, lane_f, float(LANES)), axis=-1, keepdims=True).astype(jnp.int32)
    gmask = lane < N_GROUPS
    gl = jnp.where(gmask, logits, NEG_INF)
    gmax = jnp.max(gl, axis=-1, keepdims=True)
    gidx = first(gl == gmax)
    g_w = 1.0 / jnp.sum(jnp.where(gmask, jnp.exp(gl - gmax), 0.0), axis=-1, keepdims=True)
    lo = N_GROUPS + gidx * EXPERTS_PER_GROUP
    el = jnp.where((lane >= lo) & (lane < lo + EXPERTS_PER_GROUP), logits, NEG_INF)
    v1 = jnp.max(el, axis=-1, keepdims=True)
    i1 = first(el == v1)
    el2 = jnp.where(lane == i1, NEG_INF, el)
    v2 = jnp.max(el2, axis=-1, keepdims=True)
    i2 = first(el2 == v2)
    e2 = jnp.exp(v2 - v1)
    gate1 = g_w / (1.0 + e2)
    eidx = jnp.where(lane == 0, (i1 - N_GROUPS).astype(F32), jnp.where(lane == 1, (i2 - N_GROUPS).astype(F32), 0.0))
    eidx_ref[...] = eidx.T[:eidx_ref.shape[0]].astype(jnp.int32)
    gate_ref[...] = jnp.where(lane == 0, gate1, jnp.where(lane == 1, gate1 * e2, 0.0))


def merge(x, oa, ob, om, gt, w, tm, alpha, n_total, row_offset=0, into=None):
    n, d = x.shape
    assert row_offset % tm == 0 and n % tm == 0 and n_total % tm == 0
    off = row_offset // tm
    into = list(into or [])
    n_tiles = n // tm
    steps = n_tiles if into else n_total // tm
    rows = lambda a: pl.BlockSpec((tm, a.shape[1]), lambda i: (jnp.minimum(i, n_tiles - 1), 0))
    out = lambda width: pl.BlockSpec((tm, width), lambda i: (i + off, 0))
    consts = [w['ln_in_g'], w['ln_in_b'], w['p_a'], w['p_b'], w['p_m'], w['w_o'], w['ln1_g'], w['ln1_b'],
              w['w_route'], w['b_route']]
    n_in = 5 + len(consts)
    return pl.pallas_call(
        functools.partial(_merge_kernel, alpha, n_tiles), name="merge",
        grid=(steps,),
        in_specs=[rows(a) for a in (x, oa, ob, om, gt)] + [_const_spec(c.shape) for c in consts]
        + [pl.BlockSpec(memory_space=pl.ANY)] * len(into),
        out_specs=[out(d), pl.BlockSpec((tm * TILE_ROWS, LANES), lambda i: (i + off, 0)),
                   pl.BlockSpec((TILE_ROWS, tm), lambda i: (0, i + off)), out(LANES)],
        out_shape=[jax.ShapeDtypeStruct((n_total, d), F32), jax.ShapeDtypeStruct((n_total * TILE_ROWS, LANES), F32),
                   jax.ShapeDtypeStruct((TILE_ROWS, n_total), jnp.int32), jax.ShapeDtypeStruct((n_total, LANES), F32)],
        input_output_aliases={n_in + k: k for k in range(len(into))},
        compiler_params=_params(("parallel",)),
    )(x, oa, ob, om, gt, *consts, *into)


ROW_DMA_UNROLL = 8
MOE_BUFFERS = 3
DRAIN_STEPS = 2


TILE_ROWS = 8


def _rows_from_tiles(ref, n):
    return jnp.concatenate([ref[pl.ds(s, n, stride=TILE_ROWS), :] for s in range(TILE_ROWS)], axis=1)


def _rows_to_tiles(ref, x):
    for s in range(TILE_ROWS):
        ref[pl.ds(s, x.shape[0], stride=TILE_ROWS), :] = x[:, s * LANES:(s + 1) * LANES]


def _row_copies(asg_ref, base, count, n_asg, x_hbm, buf, y_hbm, sem, gather, unrolled):
    def tile(ref, idx):
        start = idx * TILE_ROWS
        return ref.at[pl.ds(start if isinstance(idx, int) else pl.multiple_of(start, TILE_ROWS), TILE_ROWS)]

    def one(r, priority):
        a = asg_ref[base + r]
        if gather:
            tok = jnp.minimum(a, n_asg - 1)
            tok = jnp.where(tok >= n_asg // 2, tok - n_asg // 2, tok)
            copy = pltpu.make_async_copy(tile(x_hbm, tok), tile(buf, r), sem)
        else:
            copy = pltpu.make_async_copy(tile(buf, r), tile(y_hbm, a), sem)
        copy.start(priority=priority)

    if unrolled:
        for r in range(count):
            one(r, r % 2)
        return

    def body(g, carry):
        for j in range(ROW_DMA_UNROLL):
            one(g * ROW_DMA_UNROLL + j, j % 2)
        return carry
    lax.fori_loop(0, count // ROW_DMA_UNROLL, body, 0)


def _moe_expert_kernel(n_tokens, n_asg, asg_ref, be_ref, nu_ref, x_hbm, wg_ref, wu_ref, wd_ref, y_hbm,
                       xbuf, ybuf, wgb, wub, wdb, gsem, ssem):
    i = pl.program_id(0)
    used = nu_ref[0]
    rows = xbuf.shape[1] // TILE_ROWS
    n_buf = xbuf.shape[0]
    n_blocks = be_ref.shape[0]
    blk = jnp.minimum(i, n_blocks - 1)

    @pl.when((i == 0) | (be_ref[blk] != be_ref[jnp.maximum(blk - 1, 0)]))
    def _():
        wgb[...] = wg_ref[0, 0].astype(BF16)
        wub[...] = wu_ref[0, 0].astype(BF16)
        wdb[...] = wd_ref[0, 0].astype(BF16)

    def wait_gather(slot):
        pltpu.make_async_copy(x_hbm.at[pl.ds(0, rows * TILE_ROWS)], xbuf.at[slot], gsem.at[slot]).wait()

    def wait_scatter(slot):
        pltpu.make_async_copy(ybuf.at[slot], y_hbm.at[pl.ds(0, rows * TILE_ROWS)], ssem.at[slot]).wait()

    def gather(b, slot, unrolled):
        _row_copies(asg_ref, b * rows, rows, n_asg, x_hbm, xbuf.at[slot], y_hbm, gsem.at[slot], True, unrolled)

    def scatter(b, slot, unrolled):
        _row_copies(asg_ref, b * rows, rows, n_asg, x_hbm, ybuf.at[slot], y_hbm, ssem.at[slot], False, unrolled)

    def expert(slot):
        xb = _rows_from_tiles(xbuf.at[slot], rows).astype(BF16)
        hg = jnp.dot(xb, wgb[...], preferred_element_type=F32)
        hu = jnp.dot(xb, wub[...], preferred_element_type=F32)
        h = hg * _sigmoid(hg) * hu
        _rows_to_tiles(ybuf.at[slot], jnp.dot(h.astype(BF16), wdb[...], preferred_element_type=F32))

    @pl.when(i == 0)
    def _():
        ybuf[1] = jnp.zeros(ybuf.shape[1:], F32)
        stride = n_asg // 2
        gaps = [(k * stride + n_tokens, stride - n_tokens) for k in range(2)]
        gaps.append((n_asg, y_hbm.shape[0] // TILE_ROWS - n_asg))
        fills = [pltpu.make_async_copy(ybuf.at[1, pl.ds(0, min(rows, count - j) * TILE_ROWS)],
                                       y_hbm.at[pl.ds((start + j) * TILE_ROWS, min(rows, count - j) * TILE_ROWS)],
                                       ssem.at[1])
                 for start, count in gaps for j in range(0, count, rows)]
        for copy in fills:
            copy.start()
        for copy in fills:
            copy.wait()

    @pl.when((i >= 2) & (i - 2 < used))
    def _():
        wait_scatter((i - 2) % n_buf)

    steady = (i >= 1) & (i + 2 < used)
    for slot in range(n_buf):
        @pl.when(steady & (i % n_buf == slot))
        def _():
            wait_gather(slot)
            gather(i + 2, (slot + 2) % n_buf, True)
            scatter(i - 1, (slot - 1) % n_buf, True)
            expert(slot)

    @pl.when(jnp.logical_not(steady))
    def _():
        slot = i % n_buf

        @pl.when(i == 0)
        def _():
            for b in range(2):
                @pl.when(b < used)
                def _():
                    gather(b, b, False)

        @pl.when(i < used)
        def _():
            wait_gather(slot)

        @pl.when(i + 2 < used)
        def _():
            gather(i + 2, (i + 2) % n_buf, False)

        @pl.when((i >= 1) & (i - 1 < used))
        def _():
            scatter(i - 1, (i - 1) % n_buf, False)

        @pl.when(i < used)
        def _():
            expert(slot)


def moe_experts(x1_tiles, n_tokens, stride, asg, blk_e, n_used, e_gate, e_up, e_down):
    d = e_gate.shape[2]
    assert d == TILE_ROWS * LANES and x1_tiles.shape == (stride * TILE_ROWS, LANES)
    n_blocks = blk_e.shape[0]
    ff = e_gate.shape[-1]
    n_asg = 2 * stride
    n_rows = n_asg + e_gate.shape[1] * EXPERT_BLOCK
    weight = lambda shape: pl.BlockSpec(
        (1, 1) + shape, lambda i, asg, be, nu: (0, be[jnp.minimum(i, n_blocks - 1)], 0, 0))
    return pl.pallas_call(
        functools.partial(_moe_expert_kernel, n_tokens, n_asg), name="moe_expert",
        grid_spec=pltpu.PrefetchScalarGridSpec(
            num_scalar_prefetch=3,
            grid=(n_blocks + DRAIN_STEPS,),
            in_specs=[pl.BlockSpec(memory_space=pl.ANY), weight((d, ff)), weight((d, ff)), weight((ff, d))],
            out_specs=pl.BlockSpec(memory_space=pl.ANY),
            scratch_shapes=[pltpu.VMEM((MOE_BUFFERS, EXPERT_BLOCK * TILE_ROWS, LANES), F32),
                            pltpu.VMEM((MOE_BUFFERS, EXPERT_BLOCK * TILE_ROWS, LANES), F32),
                            pltpu.VMEM((d, ff), BF16), pltpu.VMEM((d, ff), BF16), pltpu.VMEM((ff, d), BF16),
                            pltpu.SemaphoreType.DMA((MOE_BUFFERS,)), pltpu.SemaphoreType.DMA((MOE_BUFFERS,))]),
        out_shape=jax.ShapeDtypeStruct((n_rows * TILE_ROWS, LANES), F32),
        compiler_params=_params(("arbitrary",)),
    )(asg, blk_e, n_used, x1_tiles, e_gate, e_up, e_down)


def _moe_combine_kernel(alpha, lead_tiles, y0_ref, y1_ref, x1_ref, gate_ref, g_ref, b_ref, lead_ref, tail_ref):
    i = pl.program_id(0)
    gate = gate_ref[...]
    tm = x1_ref.shape[0]
    moe = gate[:, 0:1] * _rows_from_tiles(y0_ref, tm) + gate[:, 1:2] * _rows_from_tiles(y1_ref, tm)
    out = _ln(alpha * x1_ref[...] + moe, g_ref[...], b_ref[...])

    @pl.when(i < lead_tiles)
    def _():
        lead_ref[...] = out

    @pl.when(i >= lead_tiles)
    def _():
        tail_ref[...] = out[:tail_ref.shape[0]]


def moe_combine(y, x1, gate, g, b, tm, alpha, n, n_lead):
    stride, d = x1.shape
    assert n_lead % tm == 0 and stride - n_lead == tm and n_lead < n <= stride
    lead_tiles = n_lead // tm
    return pl.pallas_call(
        functools.partial(_moe_combine_kernel, alpha, lead_tiles), name="moe_combine",
        grid=(stride // tm,),
        in_specs=[pl.BlockSpec((tm * TILE_ROWS, LANES), lambda i: (i, 0)),
                  pl.BlockSpec((tm * TILE_ROWS, LANES), lambda i: (i + stride // tm, 0)),
                  pl.BlockSpec((tm, d), lambda i: (i, 0)),
                  pl.BlockSpec((tm, LANES), lambda i: (i, 0)), _const_spec((1, d)), _const_spec((1, d))],
        out_specs=[pl.BlockSpec((tm, d), lambda i: (jnp.minimum(i, lead_tiles - 1), 0)),
                   pl.BlockSpec((n - n_lead, d), lambda i: (0, 0))],
        out_shape=[jax.ShapeDtypeStruct((n_lead, d), F32), jax.ShapeDtypeStruct((n - n_lead, d), F32)],
        compiler_params=_params(("arbitrary",)),
    )(y, y, x1, gate, g.reshape(1, d), b.reshape(1, d))


def moe_routing(experts, stride):
    top_k, n = experts.shape
    n_exp = N_GROUPS * EXPERTS_PER_GROUP
    a = n * top_k
    flat_e = experts.reshape(a)
    pos_bits = max(a - 1, 1).bit_length()
    assert n_exp << pos_bits < 2 ** 31
    order = jnp.sort((flat_e << pos_bits) | jnp.arange(a, dtype=jnp.int32)) & ((1 << pos_bits) - 1)
    order = order + (order // n) * (stride - n)
    counts = jnp.sum((flat_e[:, None] == jnp.arange(n_exp, dtype=jnp.int32)[None, :]).astype(jnp.int32), axis=0)
    ends = jnp.cumsum(counts)
    padded = (counts + EXPERT_BLOCK - 1) // EXPERT_BLOCK * EXPERT_BLOCK
    pad_end = jnp.cumsum(padded)
    n_blocks = -(-a // EXPERT_BLOCK) + n_exp
    blk_start = jnp.arange(n_blocks, dtype=jnp.int32) * EXPERT_BLOCK
    blk_e = jnp.minimum(jnp.sum((pad_end[None, :] <= blk_start[:, None]).astype(jnp.int32), axis=1), n_exp - 1)
    is_e = blk_e[:, None] == jnp.arange(n_exp, dtype=jnp.int32)[None, :]
    per_block = lambda table: jnp.sum(jnp.where(is_e, table[None, :], 0), axis=1, keepdims=True)
    slot = blk_start[:, None] + jnp.arange(EXPERT_BLOCK, dtype=jnp.int32)[None, :]
    rank = slot - per_block(pad_end - padded)
    spare = top_k * stride + jnp.clip(slot - per_block(ends), 0, n_exp * EXPERT_BLOCK - 1)
    asg = jnp.where(rank < per_block(counts), order[jnp.clip(per_block(ends - counts) + rank, 0, a - 1)], spare)
    n_used = (pad_end[-1:] // EXPERT_BLOCK).astype(jnp.int32)
    return asg.reshape(-1).astype(jnp.int32), blk_e.astype(jnp.int32), n_used


def hier_moe_ln(x1, x1_tiles, eidx, gate, w, tm, alpha, n_tokens, n_lead):
    stride = x1.shape[0]
    asg, blk_e, n_used = moe_routing(eidx[:2, :n_tokens], stride)
    y = moe_experts(x1_tiles, n_tokens, stride, asg, blk_e, n_used, w['e_gate'], w['e_up'], w['e_down'])
    return moe_combine(y, x1, gate, w['ln2_g'], w['ln2_b'], tm, alpha, n_tokens, n_lead)


def kernel(x_prompt, x_sample, mem_prompt, state_wkv, state_shift, cache_win_k, cache_win_v, cache_mem_k, cache_mem_v, ln_in_g, ln_in_b, w_in, mu, w0, w_up, a0, a_up, g_up, k_k, k_a, r_k, lnx_g, lnx_b, sinks, w_mem_kv, p_a, p_b, p_m, w_o, ln1_g, ln1_b, w_group, b_group, w_router, b_router, e_gate, e_up, e_down, ln2_g, ln2_b):
    depth = w_in.shape[0]
    assert depth == 1, "single-layer step"
    bsz, seq, d = x_prompt.shape
    dec = x_sample.shape[0]
    assert x_sample.shape[1] == 1
    c_shift = mu.shape[-1]
    c_a = w0.shape[-1]
    window, kv_w = cache_win_k.shape[2], cache_win_k.shape[3] * cache_win_k.shape[4]
    n_mem, mem_w = cache_mem_k.shape[2], cache_mem_k.shape[3] * cache_mem_k.shape[4]
    q_w = sinks.shape[-1] * HEAD
    qkvm_w = q_w + 2 * kv_w + mem_w
    alpha = (2.0 * depth) ** 0.25
    past_len = float(PAST_LEN)
    chunk = WKV_CHUNK

    w_in_b = w_in[0].astype(BF16)
    w_parts = [w_in_b[:, :c_shift], w_in_b[:, c_shift:c_shift + qkvm_w], w_in_b[:, c_shift + qkvm_w:]]
    rp = dict(mu=mu[0], w0=w0[0], w_up=w_up[0], a0=a0[0], a_up=a_up[0], g_up=g_up[0], k_k=k_k[0], k_a=k_a[0],
              r_k=r_k[0].reshape(-1))
    n_route = N_GROUPS * (1 + EXPERTS_PER_GROUP)
    mw = dict(ln_in_g=ln_in_g.reshape(1, d), ln_in_b=ln_in_b.reshape(1, d), p_a=p_a[0].astype(BF16),
              p_b=p_b[0].astype(BF16), p_m=p_m[0].astype(BF16), w_o=w_o[0].astype(BF16),
              ln1_g=ln1_g[0].reshape(1, d), ln1_b=ln1_b[0].reshape(1, d),
              w_route=jnp.stack(_pieces(jnp.pad(jnp.concatenate([w_group[0], w_router[0]], axis=1),
                                                ((0, 0), (0, LANES - n_route))), 2)),
              b_route=jnp.pad(jnp.concatenate([b_group[0], b_router[0]]), (0, LANES - n_route)).reshape(1, LANES),
              e_gate=e_gate, e_up=e_up, e_down=e_down, ln2_g=ln2_g[0], ln2_b=ln2_b[0])

    xp = x_prompt.reshape(bsz * seq, d)
    zq, zg, shift_p, prep = ln_proj_prep(x_prompt, ln_in_g, ln_in_b, w_parts, jnp.zeros((bsz, 1, c_shift), F32), rp,
                                         chunk, ROW_TILE, BF16)
    o_a, wkv_p = wkv(prep, jnp.zeros((bsz, c_a // HEAD, HEAD, HEAD), F32), lnx_g[0], lnx_b[0], chunk)
    mkv = proj(mem_prompt.reshape(bsz * n_mem, d), w_mem_kv[0].astype(BF16), ROW_TILE).reshape(bsz, n_mem, 2 * mem_w)
    mk_p, mv_p = mkv[..., :mem_w], mkv[..., mem_w:]
    tables = rope_tables(jnp.arange(seq, dtype=F32))
    o_b, o_m, k_rot = attn_prompt(zq, mk_p, mv_p, sinks[0], tables, window, q_w, kv_w)
    n_all = bsz * seq + dec
    n_buf = -(-n_all // ROW_TILE) * ROW_TILE
    routed = merge(xp, o_a.reshape(-1, c_a), o_b.reshape(-1, q_w), o_m.reshape(-1, mem_w),
                   zg.reshape(bsz * seq, -1), mw, ROW_TILE, alpha, n_buf)
    shift_p = shift_p[:, 0]
    kb_p = k_rot[:, -window:].reshape(bsz, window, H_KV, HEAD)
    vb_p = zq[:, -window:, q_w + kv_w:q_w + 2 * kv_w].reshape(bsz, window, H_KV, HEAD)

    xs = x_sample.reshape(dec, d)
    zq_s, zg_s, zr_s, ops_s = ln_proj_prep(xs.reshape(1, dec, d), ln_in_g, ln_in_b, w_parts,
                                           state_shift[0].reshape(1, dec, c_shift), rp, 1, dec, F32)
    zq_s, zg_s, zr_s = zq_s[0], zg_s[0], zr_s[0]
    o_a_s, wkv_s = wkv_step([a.reshape(dec, c_a) for a in ops_s], state_wkv[0], lnx_g[0], lnx_b[0], STEP_BATCH)
    tables_s = rope_tables(jnp.full((1,), past_len, F32))
    o_b_s, o_m_s, nk_s, nv_s = attn_step(
        zq_s, cache_win_k[0].reshape(dec, window, kv_w), cache_win_v[0].reshape(dec, window, kv_w),
        cache_mem_k[0].reshape(dec, n_mem, mem_w), cache_mem_v[0].reshape(dec, n_mem, mem_w),
        sinks[0], tables_s, q_w, kv_w, STEP_BATCH)
    x1, x1_tiles, eidx, gate = merge(xs, o_a_s, o_b_s, o_m_s, zg_s, mw, dec, alpha, n_buf, bsz * seq, routed)

    y_prompt, y_sample = hier_moe_ln(x1, x1_tiles, eidx, gate, mw, ROW_TILE, alpha, n_all, bsz * seq)
    y_prompt = y_prompt.reshape(bsz, seq, d)
    y_sample = y_sample.reshape(dec, 1, d)

    sd = state_wkv.dtype
    return (y_prompt, y_sample, wkv_p[None].astype(sd), wkv_s[None].astype(sd), shift_p[None], zr_s[None],
            kb_p[None], vb_p[None], nk_s.reshape(dec, window, H_KV, HEAD)[None],
            nv_s.reshape(dec, window, H_KV, HEAD)[None],
            mk_p.reshape(bsz, n_mem, -1, HEAD)[None], mv_p.reshape(bsz, n_mem, -1, HEAD)[None])
```

```python
import functools
import math

import jax
import jax.numpy as jnp
from jax import lax
from jax.experimental import pallas as pl
from jax.experimental.pallas import tpu as pltpu

F32 = jnp.float32
BF16 = jnp.bfloat16

ROW_TILE = 256
WKV_CHUNK = 64
CHUNKS_PER_STEP = 8
SCAN_BATCH = 8
ATTN_BLOCKS = 4
STEP_BATCH = 16
PROJ_PIECE = 512

HEAD = 64
LANES = 128
H_KV = 2
ROT_HALF = 8
ROPE_THETA = 500000.0
PAST_LEN = 8192
N_GROUPS = 4
EXPERTS_PER_GROUP = 8
EXPERT_BLOCK = 128
LN_EPS = 1e-5
LNX_EPS = 64e-5
NEG_INF = -1e30
VMEM_LIMIT = 48 * 1024 * 1024


def _pieces(x, n):
    out = []
    for _ in range(n):
        p = x.astype(BF16)
        out.append(p)
        x = x - p.astype(F32)
    return out


def _mask_dot(x, mask, n=2):
    return sum(jnp.dot(p, mask, preferred_element_type=F32) for p in _pieces(x, n))


def _split3(x, axis, lhs):
    hi = x.astype(BF16).astype(F32)
    lo = x - hi
    return jnp.concatenate([hi, hi, lo] if lhs else [hi, lo, hi], axis=axis).astype(BF16)


def _dot3(a, b):
    return jnp.dot(_split3(a, 1, True), _split3(b, 0, False), preferred_element_type=F32)


def _dot3_t(a, b):
    return lax.dot_general(_split3(a, 1, True), _split3(b, 1, False), (((1,), (1,)), ((), ())),
                           preferred_element_type=F32)


def _bdot(a, b):
    return jnp.dot(a.astype(BF16), b.astype(BF16), preferred_element_type=F32)


def _bdot_t(a, b):
    return lax.dot_general(a.astype(BF16), b.astype(BF16), (((1,), (1,)), ((), ())), preferred_element_type=F32)


def _sigmoid(x):
    return 0.5 * jnp.tanh(0.5 * x) + 0.5


def _ln(x, g, b):
    mu = jnp.mean(x, axis=-1, keepdims=True)
    xc = x - mu
    var = jnp.mean(xc * xc, axis=-1, keepdims=True)
    return xc * lax.rsqrt(var + LN_EPS) * g + b


def _const_spec(shape):
    nd = len(shape)
    return pl.BlockSpec(shape, lambda *_: (0,) * nd)


def _params(sem):
    return pltpu.CompilerParams(dimension_semantics=sem, vmem_limit_bytes=VMEM_LIMIT)


def _ln_proj_prep_kernel(chunk, x_ref, g_ref, b_ref, wr_ref, wq_ref, wg_ref, prev_ref, mu_ref, w0_ref, wup_ref,
                         a0_ref, aup_ref, gup_ref, kk_ref, ka_ref, rk_ref, hsum_ref, tril_ref,
                         zq_ref, zg_ref, zr_ref, *refs):
    out_refs, carry_ref = refs[:-1], refs[-1]
    xn = _ln(x_ref[0], g_ref[...], b_ref[...]).astype(BF16)
    z = jnp.dot(xn, wr_ref[...], preferred_element_type=F32)
    pending = [(o_ref, w_ref, c) for o_ref, w_ref in ((zq_ref, wq_ref), (zg_ref, wg_ref))
               for c in range(0, w_ref.shape[1], PROJ_PIECE)]

    def project(n_pieces):
        for _ in range(min(n_pieces, len(pending))):
            o_ref, w_ref, c = pending.pop(0)
            o_ref[0, :, c:c + PROJ_PIECE] = jnp.dot(xn, w_ref[:, c:c + PROJ_PIECE],
                                                    preferred_element_type=F32).astype(o_ref.dtype)

    tt = z.shape[0]
    c_a = w0_ref.shape[-1]
    r_w, r_a, r_g = wup_ref.shape[0], aup_ref.shape[0], gup_ref.shape[0]
    if chunk == 1:
        zr_ref[0] = z
        prev = prev_ref[0]
    else:
        zr_ref[0] = z[tt - 1:tt, :]

        @pl.when(pl.program_id(1) == 0)
        def _():
            carry_ref[...] = prev_ref[0]

        row = lax.broadcasted_iota(jnp.int32, z.shape, 0)
        prev = jnp.where(row == 0, carry_ref[...], pltpu.roll(z, 1, 0))
        carry_ref[...] = z[tt - 1:tt, :]
    zs = z + (prev - z) * mu_ref[...]
    r = zs[:, :c_a]
    k = zs[:, c_a:2 * c_a]
    v = zs[:, 2 * c_a:3 * c_a]
    o = 3 * c_a
    xw = zs[:, o:o + r_w]
    xa = zs[:, o + r_w:o + r_w + r_a]
    xg = zs[:, o + r_w + r_a:o + r_w + r_a + r_g]
    project(1)
    warg = -(w0_ref[...] + _dot3(jnp.tanh(xw), wup_ref[...]))
    softplus = jnp.maximum(warg, 0.0) + jnp.log1p(jnp.exp(-jnp.abs(warg)))
    lw = -jnp.exp(-softplus - 0.5)
    project(1)
    a = _sigmoid(a0_ref[...] + _dot3(xa, aup_ref[...]))
    g = _dot3(_sigmoid(xg), gup_ref[...])
    project(1)
    kkr = k * kk_ref[...]
    kk = kkr / jnp.maximum(jnp.sqrt(_mask_dot(kkr * kkr, hsum_ref[...])), 1e-12)
    project(1)
    k2 = k * (1.0 + (a - 1.0) * ka_ref[...])
    bonus = _mask_dot(r * k2 * rk_ref[...], hsum_ref[...]) * v
    kb = kk * a
    project(1)
    if chunk == 1:
        outs = (r, jnp.exp(lw), k2, v, -kk, kb, g, bonus)
        for ref, val in zip(out_refs, outs):
            ref[0] = val
        project(len(pending))
        return
    pieces = _pieces(lw, 3)
    ones3 = jnp.ones((chunk, 3 * chunk), BF16)
    cw, cwl = [], []
    for c in range(tt // chunk):
        stack = jnp.concatenate([p[c * chunk:(c + 1) * chunk] for p in pieces], axis=0)
        cw.append(jnp.dot(tril_ref[...], stack, preferred_element_type=F32))
        cwl.append(jnp.dot(ones3, stack, preferred_element_type=F32))
    cw = jnp.concatenate(cw, axis=0)
    cwl = jnp.concatenate(cwl, axis=0)
    project(1)
    e_inv = jnp.exp(-cw)
    outs = (r * jnp.exp(cw), -kk * jnp.exp(cw - lw), kb * e_inv, k2 * e_inv, v, g, bonus)
    for ref, val in zip(out_refs[:-1], outs):
        ref[0] = val
        project(1)
    wl = jnp.exp(cwl)
    for c in range(tt // chunk):
        out_refs[-1][0, c] = wl[c * chunk:c * chunk + 1, :]
    project(len(pending))


def _head_sum_matrix(width):
    idx = jnp.arange(width)
    return ((idx[:, None] // HEAD) == (idx[None, :] // HEAD)).astype(BF16)


def ln_proj_prep(x, g, b, ws, prev, p, chunk, tt, gate_dtype):
    bsz, t, d = x.shape
    w_r, w_q, w_g = ws
    cs = w_r.shape[1]
    c_a = p['w0'].shape[-1]
    assert prev.shape[1] == (t if chunk == 1 else 1)
    ridx = jnp.arange(chunk)
    tril = jnp.tile((ridx[None, :] <= ridx[:, None]).astype(BF16), (1, 3))
    hsum = _head_sum_matrix(c_a)
    row = lambda a: a.reshape(1, -1)
    rows = lambda width: pl.BlockSpec((1, tt, width), lambda bi, i: (bi, i, 0))
    full = jax.ShapeDtypeStruct((bsz, t, c_a), F32)
    if chunk == 1:
        out_specs, out_shape = [rows(c_a)] * 8, [full] * 8
        prev_spec, zr_spec, zr_rows = rows(cs), rows(cs), t
    else:
        out_specs = [rows(c_a)] * 7 + [pl.BlockSpec((1, tt // chunk, 1, c_a), lambda bi, i: (bi, i, 0, 0))]
        out_shape = [full] * 7 + [jax.ShapeDtypeStruct((bsz, t // chunk, 1, c_a), F32)]
        prev_spec = zr_spec = pl.BlockSpec((1, 1, cs), lambda bi, i: (bi, 0, 0))
        zr_rows = 1
    consts = [row(p['mu']), row(p['w0']), p['w_up'], row(p['a0']), p['a_up'], p['g_up'], row(p['k_k']),
              row(p['k_a']), row(p['r_k']), hsum, tril]
    weight = lambda w: pl.BlockSpec(w.shape, lambda bi, i: (0, 0), pipeline_mode=pl.Buffered(1))
    outs = pl.pallas_call(
        functools.partial(_ln_proj_prep_kernel, chunk), name="ln_proj_prep",
        grid=(bsz, t // tt),
        in_specs=[rows(d), _const_spec((1, d)), _const_spec((1, d)), weight(w_r), weight(w_q), weight(w_g), prev_spec]
        + [_const_spec(c.shape) for c in consts],
        out_specs=[rows(w_q.shape[1]), rows(w_g.shape[1]), zr_spec] + out_specs,
        out_shape=[jax.ShapeDtypeStruct((bsz, t, w_q.shape[1]), F32),
                   jax.ShapeDtypeStruct((bsz, t, w_g.shape[1]), gate_dtype),
                   jax.ShapeDtypeStruct((bsz, zr_rows, cs), F32)] + out_shape,
        scratch_shapes=[pltpu.VMEM((1, cs), F32)],
        compiler_params=_params(("parallel", "arbitrary")),
    )(x, row(g), row(b), w_r, w_q, w_g, prev, *consts)
    return outs[0], outs[1], outs[2], outs[3:]


def _wkv_chunk_kernel(ra_ref, at_ref, bt_ref, kt_ref, v_ref, wl_ref, rp_ref, y0_ref, m_ref, n_ref):
    n_chunks = m_ref.shape[1]
    length = ra_ref.shape[1] // n_chunks
    heads = ra_ref.shape[2] // HEAD
    row = lax.broadcasted_iota(jnp.int32, (length, length), 0)
    col = lax.broadcasted_iota(jnp.int32, (length, length), 1)
    strict = row > col
    incl = row >= col
    hrow = lax.broadcasted_iota(jnp.int32, (HEAD, HEAD), 0)
    hcol = lax.broadcasted_iota(jnp.int32, (HEAD, HEAD), 1)
    units = [(c, h) for c in range(n_chunks) for h in range(heads)]
    us = range(len(units))
    at = lambda ref, u: ref[0, units[u][0] * length:(units[u][0] + 1) * length,
                            units[u][1] * HEAD:(units[u][1] + 1) * HEAD]
    gram = [_bdot_t(jnp.concatenate([at(at_ref, u), at(ra_ref, u)], axis=0),
                    jnp.concatenate([at(bt_ref, u), at(kt_ref, u)], axis=0)) for u in us]
    a_ab = [jnp.where(strict, g[:length, :length], 0.0) for g in gram]
    a_kk = [jnp.concatenate([jnp.where(strict, g[:length, length:], 0.0),
                             jnp.where(incl, g[length:, length:], 0.0)], axis=0) for g in gram]
    a_rb = [jnp.where(incl, g[length:, :length], 0.0) for g in gram]
    inv = [jnp.where(row == col, 1.0, a) for a in a_ab]
    pw = [_bdot(a, a) for a in a_ab]
    avy = [_bdot(a_kk[u], at(v_ref, u)) for u in us]
    w_l = [wl_ref[0, units[u][0], :, units[u][1] * HEAD:(units[u][1] + 1) * HEAD] for u in us]
    nk = [_bdot(at(v_ref, u).T, at(kt_ref, u) * w_l[u]) for u in us]
    for _ in range(int(math.log2(length)) - 2):
        both = [_bdot(jnp.concatenate([pw[u], inv[u]], axis=0), pw[u]) for u in us]
        pw = [b[:length] for b in both]
        inv = [inv[u] + both[u][length:] for u in us]
    inv = [inv[u] + _bdot(inv[u], pw[u]) for u in us]
    pq = [_bdot(inv[u], jnp.concatenate([at(at_ref, u), avy[u][:length]], axis=1)) for u in us]
    ry = [_bdot(a_rb[u], pq[u]) for u in us]
    mn = [_bdot(pq[u].T, at(bt_ref, u) * w_l[u]) for u in us]
    lanes = lambda f, c: jnp.concatenate([f(c * heads + h) for h in range(heads)], axis=1)
    rows = lambda f: jnp.concatenate([lanes(f, c) for c in range(n_chunks)], axis=0)
    rp_ref[0] = rows(lambda u: at(ra_ref, u) + ry[u][:, :HEAD])
    y0_ref[0] = rows(lambda u: avy[u][length:] + ry[u][:, HEAD:])
    for c in range(n_chunks):
        m_ref[0, c] = lanes(lambda u: jnp.where(hrow == hcol, w_l[u], 0.0) + mn[u][:HEAD], c)
        n_ref[0, c] = lanes(lambda u: nk[u] + mn[u][HEAD:], c)


def _head_norm_wide(y, hsum):
    yc = y - _mask_dot(y, hsum) * (1.0 / HEAD)
    return yc * lax.rsqrt(_mask_dot(yc * yc, hsum) * (1.0 / HEAD) + LNX_EPS)


def _wkv_scan_kernel(rp_ref, y0_ref, m_ref, n_ref, g_ref, bonus_ref, s0_ref, lg_ref, lb_ref, hsum_ref,
                     o_ref, sout_ref, s_ref):
    c = pl.program_id(1)
    nb = rp_ref.shape[0]
    heads = rp_ref.shape[2] // HEAD

    @pl.when(c == 0)
    def _():
        s_ref[...] = s0_ref[...]

    pairs = [(b, h, slice(h * HEAD, (h + 1) * HEAD)) for b in range(nb) for h in range(heads)]
    s = [s_ref[b, h] for b, h, _ in pairs]
    s_new = [n_ref[b, 0, :, sl] + _dot3(s[j], m_ref[b, 0, :, sl]) for j, (b, h, sl) in enumerate(pairs)]
    ys = [y0_ref[b, :, sl] + _bdot_t(rp_ref[b, :, sl], s[j]) for j, (b, h, sl) in enumerate(pairs)]
    for j, (b, h, _) in enumerate(pairs):
        s_ref[b, h] = s_new[j]
    length = rp_ref.shape[1]
    y = _head_norm_wide(jnp.concatenate([jnp.concatenate(ys[b * heads:(b + 1) * heads], axis=-1) for b in range(nb)],
                                        axis=0), hsum_ref[...])
    for b in range(nb):
        o = (y[b * length:(b + 1) * length] * lg_ref[...] + lb_ref[...] + bonus_ref[b]) * g_ref[b]
        o_ref[b] = o.astype(o_ref.dtype)

    @pl.when(c == pl.num_programs(1) - 1)
    def _():
        sout_ref[...] = s_ref[...]


def wkv(prep, s0, lnx_g, lnx_b, chunk):
    ra, at, bt, kt, v, g, bonus, wl = prep
    bsz, t, c_a = ra.shape
    heads = c_a // HEAD
    n_chunks = t // chunk
    assert chunk & (chunk - 1) == 0 and chunk >= 4 and t % chunk == 0
    per_step = CHUNKS_PER_STEP if n_chunks % CHUNKS_PER_STEP == 0 else 1
    tile = pl.BlockSpec((1, per_step * chunk, c_a), lambda b, c: (b, c, 0))
    mat = pl.BlockSpec((1, per_step, HEAD, c_a), lambda b, c: (b, c, 0, 0))
    full = jax.ShapeDtypeStruct((bsz, t, c_a), F32)
    mats = jax.ShapeDtypeStruct((bsz, n_chunks, HEAD, c_a), F32)
    rp, y0, m, n = pl.pallas_call(
        _wkv_chunk_kernel, name="wkv_chunk",
        grid=(bsz, n_chunks // per_step),
        in_specs=[tile] * 5 + [pl.BlockSpec((1, per_step, 1, c_a), lambda b, c: (b, c, 0, 0))],
        out_specs=[tile, tile, mat, mat],
        out_shape=[full, full, mats, mats],
        compiler_params=_params(("parallel", "parallel")),
    )(ra, at, bt, kt, v, wl)
    nb = SCAN_BATCH if bsz % SCAN_BATCH == 0 else 1
    tile = pl.BlockSpec((nb, chunk, c_a), lambda b, c: (b, c, 0))
    mat = pl.BlockSpec((nb, 1, HEAD, c_a), lambda b, c: (b, c, 0, 0))
    st = pl.BlockSpec((nb, heads, HEAD, HEAD), lambda b, c: (b, 0, 0, 0))
    return pl.pallas_call(
        _wkv_scan_kernel, name="wkv_scan",
        grid=(bsz // nb, n_chunks),
        in_specs=[tile, tile, mat, mat, tile, tile, st, _const_spec((1, c_a)), _const_spec((1, c_a)),
                  _const_spec((c_a, c_a))],
        out_specs=[tile, st],
        out_shape=[jax.ShapeDtypeStruct((bsz, t, c_a), BF16), jax.ShapeDtypeStruct(s0.shape, F32)],
        scratch_shapes=[pltpu.VMEM((nb, heads, HEAD, HEAD), F32)],
        compiler_params=_params(("parallel", "arbitrary")),
    )(rp, y0, m, n, g, bonus, s0, lnx_g.reshape(1, c_a), lnx_b.reshape(1, c_a), _head_sum_matrix(c_a))


def _wkv_step_kernel(r_ref, w_ref, k_ref, v_ref, a_ref, b_ref, g_ref, bonus_ref, s_ref, lg_ref, lb_ref, hsum_ref,
                     o_ref, sout_ref):
    bt, heads = s_ref.shape[0], s_ref.shape[1]
    c_a = heads * HEAD
    hsum = hsum_ref[...]
    diag = (lax.broadcasted_iota(jnp.int32, (HEAD, c_a), 1) % HEAD
            == lax.broadcasted_iota(jnp.int32, (HEAD, c_a), 0))
    seqs = range(bt)
    row = lambda ref, i: ref[i:i + 1, :]
    stack = lambda f: jnp.concatenate([f(i) for i in seqs], axis=0)
    piece = lambda x, i: x[i * HEAD:(i + 1) * HEAD]
    s = stack(lambda i: jnp.concatenate([s_ref[i, h] for h in range(heads)], axis=1))
    sa = _mask_dot(s * stack(lambda i: jnp.broadcast_to(row(a_ref, i), (HEAD, c_a))), hsum)
    v_rows = _mask_dot(stack(lambda i: jnp.where(diag, row(v_ref, i), 0.0)), hsum)
    s = stack(lambda i: piece(s, i) * row(w_ref, i) + piece(sa, i) * row(b_ref, i) + piece(v_rows, i) * row(k_ref, i))
    for i in seqs:
        for h in range(heads):
            sout_ref[i, h] = piece(s, i)[:, h * HEAD:(h + 1) * HEAD]
    y_rows = _mask_dot(stack(lambda i: piece(s, i) * row(r_ref, i)), hsum)
    y = stack(lambda i: jnp.sum(jnp.where(diag, piece(y_rows, i), 0.0), axis=0, keepdims=True))
    y = _head_norm_wide(y, hsum)
    o_ref[...] = (y * lg_ref[...] + lb_ref[...] + bonus_ref[...]) * g_ref[...]


def wkv_step(ops, s0, lnx_g, lnx_b, bt):
    bsz, c_a = ops[0].shape
    rows = pl.BlockSpec((bt, c_a), lambda i: (i, 0))
    st = pl.BlockSpec((bt,) + s0.shape[1:], lambda i: (i, 0, 0, 0))
    hsum = _head_sum_matrix(c_a)
    return pl.pallas_call(
        _wkv_step_kernel, name="wkv_step",
        grid=(bsz // bt,),
        in_specs=[rows] * 8 + [st, _const_spec((1, c_a)), _const_spec((1, c_a)), _const_spec(hsum.shape)],
        out_specs=[rows, st],
        out_shape=[jax.ShapeDtypeStruct((bsz, c_a), F32), jax.ShapeDtypeStruct(s0.shape, F32)],
        compiler_params=_params(("parallel",)),
    )(*ops, s0, lnx_g.reshape(1, c_a), lnx_b.reshape(1, c_a), hsum)


def rope_tables(pos):
    inv_freq = ROPE_THETA ** (-jnp.arange(ROT_HALF, dtype=F32) / ROT_HALF)
    ang = pos[:, None] * inv_freq[None, :]
    cos, sin = jnp.cos(ang), jnp.sin(ang)
    t = pos.shape[0]
    rest = HEAD - 2 * ROT_HALF
    c = jnp.concatenate([cos, cos, jnp.ones((t, rest), F32)], axis=1)
    s1 = jnp.concatenate([jnp.zeros((t, ROT_HALF), F32), sin, jnp.zeros((t, rest), F32)], axis=1)
    s2 = jnp.concatenate([-sin, jnp.zeros((t, HEAD - ROT_HALF), F32)], axis=1)
    rep = LANES // HEAD
    return tuple(jnp.tile(x, (1, rep)) for x in (c, s1, s2))


def _rope(x, c, s1, s2):
    width = x.shape[-1]
    rep = width // c.shape[-1]
    if rep > 1:
        c, s1, s2 = (jnp.concatenate([t] * rep, axis=1) for t in (c, s1, s2))
    return x * c + pltpu.roll(x, ROT_HALF, 1) * s1 + pltpu.roll(x, width - ROT_HALF, 1) * s2


def _attn_kernel(window, q_w, kv_w, zq_ref, kp_ref, vp_ref, rc_ref, rs1_ref, rs2_ref, pc_ref, ps1_ref, ps2_ref,
                 mk_ref, mv_ref, sink_ref, ob_ref, om_ref, kr_ref):
    zq = zq_ref[0]
    nblk = zq.shape[0] // window
    first = pl.program_id(1) * nblk
    scale = HEAD ** -0.5
    q = zq[:, :q_w]
    k = zq[:, q_w:q_w + kv_w]
    v = zq[:, q_w + kv_w:q_w + 2 * kv_w]
    qm = zq[:, q_w + 2 * kv_w:]
    qr = _rope(q, rc_ref[...], rs1_ref[...], rs2_ref[...]) * scale
    kr = _rope(k, rc_ref[...], rs1_ref[...], rs2_ref[...])
    kr_ref[0] = kr
    k_all = jnp.concatenate([_rope(kp_ref[0], pc_ref[...], ps1_ref[...], ps2_ref[...]), kr], axis=0)
    v_all = jnp.concatenate([vp_ref[0], v], axis=0)
    gqa = q_w // kv_w
    assert window & (window - 1) == 0
    qi = lax.broadcasted_iota(jnp.int32, (gqa * window, 2 * window), 0) & (window - 1)
    kj = lax.broadcasted_iota(jnp.int32, (gqa * window, 2 * window), 1)
    band = (kj > qi) & (kj <= qi + window)
    tdot = lambda a, b: lax.dot_general(a, b, (((1,), (1,)), ((), ())), preferred_element_type=F32)
    dot = lambda a, b: jnp.dot(a, b, preferred_element_type=F32)
    hsl = lambda h: slice(h * HEAD, (h + 1) * HEAD)
    rows = lambda i: slice(i * window, (i + 1) * window)
    keys = lambda i: slice(i * window, (i + 2) * window)
    wins = [(i, hk) for i in range(nblk) for hk in range(kv_w // HEAD)]
    mems = [(i, h) for i in range(nblk) for h in range(qm.shape[1] // HEAD)]
    qmb = (qm * scale).astype(BF16)
    mk = mk_ref[0].astype(BF16)
    mv = mv_ref[0].astype(BF16)
    kcat = [k_all[keys(i), hsl(hk)].astype(BF16) for i, hk in wins]
    vcat = [v_all[keys(i), hsl(hk)].astype(BF16) for i, hk in wins]
    qs = [jnp.concatenate([qr[rows(i), hsl(hk * gqa + g)] for g in range(gqa)], axis=0).astype(BF16)
          for i, hk in wins]
    s = [jnp.where(band & ((kj >= window) | (first + i > 0)), tdot(qs[u], kcat[u]), NEG_INF)
         for u, (i, hk) in enumerate(wins)]
    sm = [tdot(qmb[rows(i), hsl(h)], mk[:, hsl(h)]) for i, h in mems]
    sink = [jnp.concatenate([jnp.full((window, 1), sink_ref[hk * gqa + g], F32) for g in range(gqa)], axis=0)
            for i, hk in wins]
    m = [jnp.maximum(jnp.max(s[u], axis=-1, keepdims=True), sink[u]) for u in range(len(wins))]
    p = [jnp.exp(s[u] - m[u]) for u in range(len(wins))]
    pm = [jnp.exp(x - jnp.max(x, axis=-1, keepdims=True)) for x in sm]
    o = [dot(p[u].astype(BF16), vcat[u])
         / (jnp.sum(p[u], axis=-1, keepdims=True) + jnp.exp(sink[u] - m[u])) for u in range(len(wins))]
    om = [dot(pm[u].astype(BF16), mv[:, hsl(h)]) / jnp.sum(pm[u], axis=-1, keepdims=True)
          for u, (i, h) in enumerate(mems)]
    n_kv, n_mh = kv_w // HEAD, qm.shape[1] // HEAD
    ob = jnp.concatenate([jnp.concatenate([o[i * n_kv + hk][g * window:(g + 1) * window]
                                           for hk in range(n_kv) for g in range(gqa)], axis=1)
                          for i in range(nblk)], axis=0)
    ob_ref[0] = ob.astype(ob_ref.dtype)
    om_ref[0] = jnp.concatenate([jnp.concatenate(om[i * n_mh:(i + 1) * n_mh], axis=1) for i in range(nblk)],
                                axis=0).astype(om_ref.dtype)


def attn_prompt(zq, mk, mv, sinks, tables, window, q_w, kv_w):
    bsz, t, zw = zq.shape
    mem_w = zw - q_w - 2 * kv_w
    assert kv_w == LANES and q_w % kv_w == 0
    kcol, vcol = q_w // kv_w, q_w // kv_w + 1
    nblk = ATTN_BLOCKS if (t // window) % ATTN_BLOCKS == 0 else 1
    rows = nblk * window
    prev = lambda n: jnp.maximum(n * nblk - 1, 0)
    tab = pl.BlockSpec((rows, LANES), lambda b, n: (n, 0))
    ptab = pl.BlockSpec((window, LANES), lambda b, n: (prev(n), 0))
    mem = pl.BlockSpec((1,) + mk.shape[1:], lambda b, n: (b, 0, 0))
    return pl.pallas_call(
        functools.partial(_attn_kernel, window, q_w, kv_w), name="attn",
        grid=(bsz, t // rows),
        in_specs=[pl.BlockSpec((1, rows, zw), lambda b, n: (b, n, 0)),
                  pl.BlockSpec((1, window, kv_w), lambda b, n: (b, prev(n), kcol)),
                  pl.BlockSpec((1, window, kv_w), lambda b, n: (b, prev(n), vcol)),
                  tab, tab, tab, ptab, ptab, ptab, mem, mem,
                  pl.BlockSpec(memory_space=pltpu.SMEM)],
        out_specs=[pl.BlockSpec((1, rows, q_w), lambda b, n: (b, n, 0)),
                   pl.BlockSpec((1, rows, mem_w), lambda b, n: (b, n, 0)),
                   pl.BlockSpec((1, rows, kv_w), lambda b, n: (b, n, 0))],
        out_shape=[jax.ShapeDtypeStruct((bsz, t, q_w), BF16), jax.ShapeDtypeStruct((bsz, t, mem_w), BF16),
                   jax.ShapeDtypeStruct((bsz, t, kv_w), F32)],
        compiler_params=_params(("parallel", "parallel")),
    )(zq, zq, zq, *tables, *tables, mk, mv, sinks)


def _attn_step_kernel(q_w, kv_w, zq_ref, ck_ref, cv_ref, mk_ref, mv_ref, rc_ref, rs1_ref, rs2_ref, sink_ref,
                      ob_ref, om_ref, nk_ref, nv_ref):
    bt = zq_ref.shape[0]
    window = ck_ref.shape[1]
    mem_w = om_ref.shape[1]
    n_q, gqa, per_vreg = q_w // HEAD, q_w // kv_w, LANES // HEAD
    scale = HEAD ** -0.5
    zq = zq_ref[...]
    q = _rope(zq[:, :q_w], rc_ref[...], rs1_ref[...], rs2_ref[...]) * scale
    k_new = _rope(zq[:, q_w:q_w + kv_w], rc_ref[...], rs1_ref[...], rs2_ref[...])
    v_new = zq[:, q_w + kv_w:q_w + 2 * kv_w]
    qm = zq[:, q_w + 2 * kv_w:] * scale
    own = lambda w: (lax.broadcasted_iota(jnp.int32, (n_q, w), 1) // HEAD
                     == lax.broadcasted_iota(jnp.int32, (n_q, w), 0))
    own_q, own_m = own(q_w), own(mem_w)
    hrow = lax.broadcasted_iota(jnp.int32, (n_q, LANES), 0)
    hblk = lax.broadcasted_iota(jnp.int32, (n_q, LANES), 1) // HEAD
    swap = (hrow % per_vreg) != (hrow // gqa)
    keep = hblk == hrow % per_vreg
    key_ok = lax.broadcasted_iota(jnp.int32, (n_q, window), 1) >= 1
    wrow = lax.broadcasted_iota(jnp.int32, (window, kv_w), 0)
    sink = sink_ref[...]
    bs = range(bt)
    tdot = lambda a, b: lax.dot_general(a.astype(BF16), b.astype(BF16), (((1,), (1,)), ((), ())),
                                        preferred_element_type=F32)
    for b in bs:
        nk_ref[b] = jnp.where(wrow == window - 1, k_new[b:b + 1], pltpu.roll(ck_ref[b], window - 1, 0))
        nv_ref[b] = jnp.where(wrow == window - 1, v_new[b:b + 1], pltpu.roll(cv_ref[b], window - 1, 0))
    q8 = []
    for b in bs:
        rep = jnp.where(own_q, q[b:b + 1], 0.0)
        fold = sum(rep[:, c * LANES:(c + 1) * LANES] for c in range(q_w // LANES))
        q8.append(jnp.where(swap, pltpu.roll(fold, HEAD, 1), fold))
    qm8 = [jnp.where(own_m, qm[b:b + 1], 0.0) for b in bs]
    s = [jnp.where(key_ok, tdot(q8[b], ck_ref[b]), NEG_INF) for b in bs]
    sm = [tdot(qm8[b], mk_ref[b]) for b in bs]
    s_new = [jnp.sum(q8[b] * k_new[b:b + 1], axis=1, keepdims=True) for b in bs]
    m = [jnp.maximum(jnp.maximum(jnp.max(s[b], axis=1, keepdims=True), s_new[b]), sink) for b in bs]
    p = [jnp.exp(s[b] - m[b]) for b in bs]
    p_new = [jnp.exp(s_new[b] - m[b]) for b in bs]
    pm = [jnp.exp(sm[b] - jnp.max(sm[b], axis=1, keepdims=True)) for b in bs]
    o8 = [(_bdot(p[b], cv_ref[b]) + p_new[b] * v_new[b:b + 1])
          / (jnp.sum(p[b], axis=1, keepdims=True) + p_new[b] + jnp.exp(sink - m[b])) for b in bs]
    om8 = [_bdot(pm[b], mv_ref[b]) / jnp.sum(pm[b], axis=1, keepdims=True) for b in bs]
    ob, om = [], []
    for b in bs:
        o = jnp.where(swap, pltpu.roll(o8[b], HEAD, 1), o8[b])
        o = jnp.concatenate([jnp.where(keep, o, 0.0)] * (q_w // LANES), axis=1)
        ob.append(jnp.sum(jnp.where(own_q, o, 0.0), axis=0, keepdims=True))
        om.append(jnp.sum(jnp.where(own_m, om8[b], 0.0), axis=0, keepdims=True))
    ob_ref[...] = jnp.concatenate(ob, axis=0)
    om_ref[...] = jnp.concatenate(om, axis=0)


def attn_step(zq, ck, cv, mk, mv, sinks, tables, q_w, kv_w, bt):
    bsz, zw = zq.shape
    mem_w = zw - q_w - 2 * kv_w
    window, n_mem = ck.shape[1], mk.shape[1]
    n_q = q_w // HEAD
    assert kv_w == LANES and mem_w // HEAD <= n_q and n_q == 8
    rows = lambda w: pl.BlockSpec((bt, w), lambda i: (i, 0))
    cache = pl.BlockSpec((bt, window, kv_w), lambda i: (i, 0, 0))
    mem = pl.BlockSpec((bt, n_mem, mem_w), lambda i: (i, 0, 0))
    consts = list(tables) + [sinks.reshape(n_q, 1)]
    const_specs = [_const_spec(c.shape) for c in consts]
    return pl.pallas_call(
        functools.partial(_attn_step_kernel, q_w, kv_w), name="attn_step",
        grid=(bsz // bt,),
        in_specs=[rows(zw), cache, cache, mem, mem] + const_specs,
        out_specs=[rows(q_w), rows(mem_w), cache, cache],
        out_shape=[jax.ShapeDtypeStruct((bsz, q_w), F32), jax.ShapeDtypeStruct((bsz, mem_w), F32),
                   jax.ShapeDtypeStruct(ck.shape, F32), jax.ShapeDtypeStruct(cv.shape, F32)],
        compiler_params=_params(("parallel",)),
    )(zq, ck, cv, mk, mv, *consts)


def _proj_kernel(x_ref, w_ref, o_ref):
    o_ref[...] = jnp.dot(x_ref[...].astype(BF16), w_ref[...], preferred_element_type=F32)


def proj(x, w, tm):
    n, d = x.shape
    return pl.pallas_call(
        _proj_kernel, name="proj",
        grid=(n // tm,),
        in_specs=[pl.BlockSpec((tm, d), lambda i: (i, 0)), _const_spec(w.shape)],
        out_specs=pl.BlockSpec((tm, w.shape[1]), lambda i: (i, 0)),
        out_shape=jax.ShapeDtypeStruct((n, w.shape[1]), F32),
        compiler_params=_params(("parallel",)),
    )(x, w)


def _merge_kernel(alpha, n_tiles, x_ref, oa_ref, ob_ref, om_ref, gt_ref, lig_ref, lib_ref, pa_ref, pb_ref, pm_ref,
                  wo_ref, l1g_ref, l1b_ref, wr_ref, br_ref, *refs):
    outs = refs[-4:]

    @pl.when(pl.program_id(0) >= n_tiles)
    def _():
        for ref in outs:
            ref[...] = jnp.zeros_like(ref)

    @pl.when(pl.program_id(0) < n_tiles)
    def _():
        _merge_tile(alpha, x_ref, oa_ref, ob_ref, om_ref, gt_ref, lig_ref, lib_ref, pa_ref, pb_ref, pm_ref, wo_ref,
                    l1g_ref, l1b_ref, wr_ref, br_ref, *outs)


def _merge_tile(alpha, x_ref, oa_ref, ob_ref, om_ref, gt_ref, lig_ref, lib_ref, pa_ref, pb_ref, pm_ref, wo_ref,
                l1g_ref, l1b_ref, wr_ref, br_ref, x1_ref, x1t_ref, eidx_ref, gate_ref):
    d = x_ref.shape[1]
    xn = _ln(x_ref[...], lig_ref[...], lib_ref[...])
    gts = _sigmoid(gt_ref[...].astype(F32))
    merged = (gts[:, :d] * _bdot(oa_ref[...], pa_ref[...]) + gts[:, d:2 * d] * _bdot(ob_ref[...], pb_ref[...])
              + gts[:, 2 * d:] * _bdot(om_ref[...], pm_ref[...]))
    x1 = _ln(alpha * xn + _bdot(merged, wo_ref[...]), l1g_ref[...], l1b_ref[...])
    x1_ref[...] = x1
    _rows_to_tiles(x1t_ref, x1)
    x_hi, x_lo = _pieces(x1, 2)
    w_hi, w_lo = wr_ref[0], wr_ref[1]
    dot = lambda a, b: jnp.dot(a, b, preferred_element_type=F32)
    logits = dot(x_hi, w_hi) + dot(x_hi, w_lo) + dot(x_lo, w_hi) + br_ref[...]
    lane = lax.broadcasted_iota(jnp.int32, logits.shape, 1)
    lane_f = lane.astype(F32)
    first = lambda hit: jnp.min(jnp.where(hit, lane_f, float(LANES)), axis=-1, keepdims=True).astype(jnp.int32)
    gmask = lane < N_GROUPS
    gl = jnp.where(gmask, logits, NEG_INF)
    gmax = jnp.max(gl, axis=-1, keepdims=True)
    gidx = first(gl == gmax)
    g_w = 1.0 / jnp.sum(jnp.where(gmask, jnp.exp(gl - gmax), 0.0), axis=-1, keepdims=True)
    lo = N_GROUPS + gidx * EXPERTS_PER_GROUP
    el = jnp.where((lane >= lo) & (lane < lo + EXPERTS_PER_GROUP), logits, NEG_INF)
    v1 = jnp.max(el, axis=-1, keepdims=True)
    i1 = first(el == v1)
    el2 = jnp.where(lane == i1, NEG_INF, el)
    v2 = jnp.max(el2, axis=-1, keepdims=True)
    i2 = first(el2 == v2)
    e2 = jnp.exp(v2 - v1)
    gate1 = g_w / (1.0 + e2)
    eidx = jnp.where(lane == 0, (i1 - N_GROUPS).astype(F32), jnp.where(lane == 1, (i2 - N_GROUPS).astype(F32), 0.0))
    eidx_ref[...] = eidx.T[:eidx_ref.shape[0]].astype(jnp.int32)
    gate_ref[...] = jnp.where(lane == 0, gate1, jnp.where(lane == 1, gate1 * e2, 0.0))


def merge(x, oa, ob, om, gt, w, tm, alpha, n_total, row_offset=0, into=None):
    n, d = x.shape
    assert row_offset % tm == 0 and n % tm == 0 and n_total % tm == 0
    off = row_offset // tm
    into = list(into or [])
    n_tiles = n // tm
    steps = n_tiles if into else n_total // tm
    rows = lambda a: pl.BlockSpec((tm, a.shape[1]), lambda i: (jnp.minimum(i, n_tiles - 1), 0))
    out = lambda width: pl.BlockSpec((tm, width), lambda i: (i + off, 0))
    consts = [w['ln_in_g'], w['ln_in_b'], w['p_a'], w['p_b'], w['p_m'], w['w_o'], w['ln1_g'], w['ln1_b'],
              w['w_route'], w['b_route']]
    n_in = 5 + len(consts)
    return pl.pallas_call(
        functools.partial(_merge_kernel, alpha, n_tiles), name="merge",
        grid=(steps,),
        in_specs=[rows(a) for a in (x, oa, ob, om, gt)] + [_const_spec(c.shape) for c in consts]
        + [pl.BlockSpec(memory_space=pl.ANY)] * len(into),
        out_specs=[out(d), pl.BlockSpec((tm * TILE_ROWS, LANES), lambda i: (i + off, 0)),
                   pl.BlockSpec((TILE_ROWS, tm), lambda i: (0, i + off)), out(LANES)],
        out_shape=[jax.ShapeDtypeStruct((n_total, d), F32), jax.ShapeDtypeStruct((n_total * TILE_ROWS, LANES), F32),
                   jax.ShapeDtypeStruct((TILE_ROWS, n_total), jnp.int32), jax.ShapeDtypeStruct((n_total, LANES), F32)],
        input_output_aliases={n_in + k: k for k in range(len(into))},
        compiler_params=_params(("parallel",)),
    )(x, oa, ob, om, gt, *consts, *into)


ROW_DMA_UNROLL = 8
MOE_BUFFERS = 4
DRAIN_STEPS = 2


TILE_ROWS = 8


def _rows_from_tiles(ref, n):
    return jnp.concatenate([ref[pl.ds(s, n, stride=TILE_ROWS), :] for s in range(TILE_ROWS)], axis=1)


def _rows_to_tiles(ref, x):
    for s in range(TILE_ROWS):
        ref[pl.ds(s, x.shape[0], stride=TILE_ROWS), :] = x[:, s * LANES:(s + 1) * LANES]


def _row_copies(asg_ref, base, count, n_asg, x_hbm, buf, y_hbm, sem, gather, unrolled):
    def tile(ref, idx):
        start = idx * TILE_ROWS
        return ref.at[pl.ds(start if isinstance(idx, int) else pl.multiple_of(start, TILE_ROWS), TILE_ROWS)]

    def one(r, priority):
        a = asg_ref[base + r]
        if gather:
            tok = jnp.minimum(a, n_asg - 1)
            tok = jnp.where(tok >= n_asg // 2, tok - n_asg // 2, tok)
            copy = pltpu.make_async_copy(tile(x_hbm, tok), tile(buf, r), sem)
        else:
            copy = pltpu.make_async_copy(tile(buf, r), tile(y_hbm, a), sem)
        copy.start(priority=priority)

    if unrolled:
        for r in range(count):
            one(r, r % 2)
        return

    def body(g, carry):
        for j in range(ROW_DMA_UNROLL):
            one(g * ROW_DMA_UNROLL + j, j % 2)
        return carry
    lax.fori_loop(0, count // ROW_DMA_UNROLL, body, 0)


def _moe_expert_kernel(n_tokens, n_asg, asg_ref, be_ref, nu_ref, x_hbm, wg_ref, wu_ref, wd_ref, y_hbm,
                       xbuf, ybuf, wgb, wub, wdb, gsem, ssem):
    i = pl.program_id(0)
    used = nu_ref[0]
    rows = xbuf.shape[1] // TILE_ROWS
    n_buf = xbuf.shape[0]
    n_blocks = be_ref.shape[0]
    blk = jnp.minimum(i, n_blocks - 1)

    @pl.when((i == 0) | (be_ref[blk] != be_ref[jnp.maximum(blk - 1, 0)]))
    def _():
        wgb[...] = wg_ref[0, 0].astype(BF16)
        wub[...] = wu_ref[0, 0].astype(BF16)
        wdb[...] = wd_ref[0, 0].astype(BF16)

    def wait_gather(slot):
        pltpu.make_async_copy(x_hbm.at[pl.ds(0, rows * TILE_ROWS)], xbuf.at[slot], gsem.at[slot]).wait()

    def wait_scatter(slot):
        pltpu.make_async_copy(ybuf.at[slot], y_hbm.at[pl.ds(0, rows * TILE_ROWS)], ssem.at[slot]).wait()

    def gather(b, slot, unrolled):
        _row_copies(asg_ref, b * rows, rows, n_asg, x_hbm, xbuf.at[slot], y_hbm, gsem.at[slot], True, unrolled)

    def scatter(b, slot, unrolled):
        _row_copies(asg_ref, b * rows, rows, n_asg, x_hbm, ybuf.at[slot], y_hbm, ssem.at[slot], False, unrolled)

    def expert(slot):
        xb = _rows_from_tiles(xbuf.at[slot], rows).astype(BF16)
        hg = jnp.dot(xb, wgb[...], preferred_element_type=F32)
        hu = jnp.dot(xb, wub[...], preferred_element_type=F32)
        h = hg * _sigmoid(hg) * hu
        _rows_to_tiles(ybuf.at[slot], jnp.dot(h.astype(BF16), wdb[...], preferred_element_type=F32))

    @pl.when(i == 0)
    def _():
        ybuf[1] = jnp.zeros(ybuf.shape[1:], F32)
        stride = n_asg // 2
        gaps = [(k * stride + n_tokens, stride - n_tokens) for k in range(2)]
        gaps.append((n_asg, y_hbm.shape[0] // TILE_ROWS - n_asg))
        fills = [pltpu.make_async_copy(ybuf.at[1, pl.ds(0, min(rows, count - j) * TILE_ROWS)],
                                       y_hbm.at[pl.ds((start + j) * TILE_ROWS, min(rows, count - j) * TILE_ROWS)],
                                       ssem.at[1])
                 for start, count in gaps for j in range(0, count, rows)]
        for copy in fills:
            copy.start()
        for copy in fills:
            copy.wait()

    @pl.when((i >= 2) & (i - 2 < used))
    def _():
        wait_scatter((i - 2) % n_buf)

    ahead = n_buf - 1
    steady = (i >= 1) & (i + ahead < used)
    for slot in range(n_buf):
        @pl.when(steady & (i % n_buf == slot))
        def _():
            wait_gather(slot)
            gather(i + ahead, (slot + ahead) % n_buf, True)
            scatter(i - 1, (slot - 1) % n_buf, True)
            expert(slot)

    @pl.when(jnp.logical_not(steady))
    def _():
        slot = i % n_buf

        @pl.when(i == 0)
        def _():
            for b in range(ahead):
                @pl.when(b < used)
                def _():
                    gather(b, b, False)

        @pl.when(i < used)
        def _():
            wait_gather(slot)

        @pl.when(i + ahead < used)
        def _():
            gather(i + ahead, (i + ahead) % n_buf, False)

        @pl.when((i >= 1) & (i - 1 < used))
        def _():
            scatter(i - 1, (i - 1) % n_buf, False)

        @pl.when(i < used)
        def _():
            expert(slot)


def moe_experts(x1_tiles, n_tokens, stride, asg, blk_e, n_used, e_gate, e_up, e_down):
    d = e_gate.shape[2]
    assert d == TILE_ROWS * LANES and x1_tiles.shape == (stride * TILE_ROWS, LANES)
    n_blocks = blk_e.shape[0]
    ff = e_gate.shape[-1]
    n_asg = 2 * stride
    n_rows = n_asg + e_gate.shape[1] * EXPERT_BLOCK
    weight = lambda shape: pl.BlockSpec(
        (1, 1) + shape, lambda i, asg, be, nu: (0, be[jnp.minimum(i, n_blocks - 1)], 0, 0))
    return pl.pallas_call(
        functools.partial(_moe_expert_kernel, n_tokens, n_asg), name="moe_expert",
        grid_spec=pltpu.PrefetchScalarGridSpec(
            num_scalar_prefetch=3,
            grid=(n_blocks + DRAIN_STEPS,),
            in_specs=[pl.BlockSpec(memory_space=pl.ANY), weight((d, ff)), weight((d, ff)), weight((ff, d))],
            out_specs=pl.BlockSpec(memory_space=pl.ANY),
            scratch_shapes=[pltpu.VMEM((MOE_BUFFERS, EXPERT_BLOCK * TILE_ROWS, LANES), F32),
                            pltpu.VMEM((MOE_BUFFERS, EXPERT_BLOCK * TILE_ROWS, LANES), F32),
                            pltpu.VMEM((d, ff), BF16), pltpu.VMEM((d, ff), BF16), pltpu.VMEM((ff, d), BF16),
                            pltpu.SemaphoreType.DMA((MOE_BUFFERS,)), pltpu.SemaphoreType.DMA((MOE_BUFFERS,))]),
        out_shape=jax.ShapeDtypeStruct((n_rows * TILE_ROWS, LANES), F32),
        compiler_params=_params(("arbitrary",)),
    )(asg, blk_e, n_used, x1_tiles, e_gate, e_up, e_down)


def _moe_combine_kernel(alpha, lead_tiles, y0_ref, y1_ref, x1_ref, gate_ref, g_ref, b_ref, lead_ref, tail_ref):
    i = pl.program_id(0)
    gate = gate_ref[...]
    tm = x1_ref.shape[0]
    moe = gate[:, 0:1] * _rows_from_tiles(y0_ref, tm) + gate[:, 1:2] * _rows_from_tiles(y1_ref, tm)
    out = _ln(alpha * x1_ref[...] + moe, g_ref[...], b_ref[...])

    @pl.when(i < lead_tiles)
    def _():
        lead_ref[...] = out

    @pl.when(i >= lead_tiles)
    def _():
        tail_ref[...] = out[:tail_ref.shape[0]]


def moe_combine(y, x1, gate, g, b, tm, alpha, n, n_lead):
    stride, d = x1.shape
    assert n_lead % tm == 0 and stride - n_lead == tm and n_lead < n <= stride
    lead_tiles = n_lead // tm
    return pl.pallas_call(
        functools.partial(_moe_combine_kernel, alpha, lead_tiles), name="moe_combine",
        grid=(stride // tm,),
        in_specs=[pl.BlockSpec((tm * TILE_ROWS, LANES), lambda i: (i, 0)),
                  pl.BlockSpec((tm * TILE_ROWS, LANES), lambda i: (i + stride // tm, 0)),
                  pl.BlockSpec((tm, d), lambda i: (i, 0)),
                  pl.BlockSpec((tm, LANES), lambda i: (i, 0)), _const_spec((1, d)), _const_spec((1, d))],
        out_specs=[pl.BlockSpec((tm, d), lambda i: (jnp.minimum(i, lead_tiles - 1), 0)),
                   pl.BlockSpec((n - n_lead, d), lambda i: (0, 0))],
        out_shape=[jax.ShapeDtypeStruct((n_lead, d), F32), jax.ShapeDtypeStruct((n - n_lead, d), F32)],
        compiler_params=_params(("arbitrary",)),
    )(y, y, x1, gate, g.reshape(1, d), b.reshape(1, d))


def moe_routing(experts, stride):
    top_k, n = experts.shape
    n_exp = N_GROUPS * EXPERTS_PER_GROUP
    a = n * top_k
    flat_e = experts.reshape(a)
    pos_bits = max(a - 1, 1).bit_length()
    assert n_exp << pos_bits < 2 ** 31
    order = jnp.sort((flat_e << pos_bits) | jnp.arange(a, dtype=jnp.int32)) & ((1 << pos_bits) - 1)
    order = order + (order // n) * (stride - n)
    counts = jnp.sum((flat_e[:, None] == jnp.arange(n_exp, dtype=jnp.int32)[None, :]).astype(jnp.int32), axis=0)
    ends = jnp.cumsum(counts)
    padded = (counts + EXPERT_BLOCK - 1) // EXPERT_BLOCK * EXPERT_BLOCK
    pad_end = jnp.cumsum(padded)
    n_blocks = -(-a // EXPERT_BLOCK) + n_exp
    blk_start = jnp.arange(n_blocks, dtype=jnp.int32) * EXPERT_BLOCK
    blk_e = jnp.minimum(jnp.sum((pad_end[None, :] <= blk_start[:, None]).astype(jnp.int32), axis=1), n_exp - 1)
    is_e = blk_e[:, None] == jnp.arange(n_exp, dtype=jnp.int32)[None, :]
    per_block = lambda table: jnp.sum(jnp.where(is_e, table[None, :], 0), axis=1, keepdims=True)
    slot = blk_start[:, None] + jnp.arange(EXPERT_BLOCK, dtype=jnp.int32)[None, :]
    rank = slot - per_block(pad_end - padded)
    spare = top_k * stride + jnp.clip(slot - per_block(ends), 0, n_exp * EXPERT_BLOCK - 1)
    asg = jnp.where(rank < per_block(counts), order[jnp.clip(per_block(ends - counts) + rank, 0, a - 1)], spare)
    n_used = (pad_end[-1:] // EXPERT_BLOCK).astype(jnp.int32)
    return asg.reshape(-1).astype(jnp.int32), blk_e.astype(jnp.int32), n_used


def hier_moe_ln(x1, x1_tiles, eidx, gate, w, tm, alpha, n_tokens, n_lead):
    stride = x1.shape[0]
    asg, blk_e, n_used = moe_routing(eidx[:2, :n_tokens], stride)
    y = moe_experts(x1_tiles, n_tokens, stride, asg, blk_e, n_used, w['e_gate'], w['e_up'], w['e_down'])
    return moe_combine(y, x1, gate, w['ln2_g'], w['ln2_b'], tm, alpha, n_tokens, n_lead)


def kernel(x_prompt, x_sample, mem_prompt, state_wkv, state_shift, cache_win_k, cache_win_v, cache_mem_k, cache_mem_v, ln_in_g, ln_in_b, w_in, mu, w0, w_up, a0, a_up, g_up, k_k, k_a, r_k, lnx_g, lnx_b, sinks, w_mem_kv, p_a, p_b, p_m, w_o, ln1_g, ln1_b, w_group, b_group, w_router, b_router, e_gate, e_up, e_down, ln2_g, ln2_b):
    depth = w_in.shape[0]
    assert depth == 1, "single-layer step"
    bsz, seq, d = x_prompt.shape
    dec = x_sample.shape[0]
    assert x_sample.shape[1] == 1
    c_shift = mu.shape[-1]
    c_a = w0.shape[-1]
    window, kv_w = cache_win_k.shape[2], cache_win_k.shape[3] * cache_win_k.shape[4]
    n_mem, mem_w = cache_mem_k.shape[2], cache_mem_k.shape[3] * cache_mem_k.shape[4]
    q_w = sinks.shape[-1] * HEAD
    qkvm_w = q_w + 2 * kv_w + mem_w
    alpha = (2.0 * depth) ** 0.25
    past_len = float(PAST_LEN)
    chunk = WKV_CHUNK

    w_in_b = w_in[0].astype(BF16)
    w_parts = [w_in_b[:, :c_shift], w_in_b[:, c_shift:c_shift + qkvm_w], w_in_b[:, c_shift + qkvm_w:]]
    rp = dict(mu=mu[0], w0=w0[0], w_up=w_up[0], a0=a0[0], a_up=a_up[0], g_up=g_up[0], k_k=k_k[0], k_a=k_a[0],
              r_k=r_k[0].reshape(-1))
    n_route = N_GROUPS * (1 + EXPERTS_PER_GROUP)
    mw = dict(ln_in_g=ln_in_g.reshape(1, d), ln_in_b=ln_in_b.reshape(1, d), p_a=p_a[0].astype(BF16),
              p_b=p_b[0].astype(BF16), p_m=p_m[0].astype(BF16), w_o=w_o[0].astype(BF16),
              ln1_g=ln1_g[0].reshape(1, d), ln1_b=ln1_b[0].reshape(1, d),
              w_route=jnp.stack(_pieces(jnp.pad(jnp.concatenate([w_group[0], w_router[0]], axis=1),
                                                ((0, 0), (0, LANES - n_route))), 2)),
              b_route=jnp.pad(jnp.concatenate([b_group[0], b_router[0]]), (0, LANES - n_route)).reshape(1, LANES),
              e_gate=e_gate, e_up=e_up, e_down=e_down, ln2_g=ln2_g[0], ln2_b=ln2_b[0])

    xp = x_prompt.reshape(bsz * seq, d)
    zq, zg, shift_p, prep = ln_proj_prep(x_prompt, ln_in_g, ln_in_b, w_parts, jnp.zeros((bsz, 1, c_shift), F32), rp,
                                         chunk, ROW_TILE, BF16)
    o_a, wkv_p = wkv(prep, jnp.zeros((bsz, c_a // HEAD, HEAD, HEAD), F32), lnx_g[0], lnx_b[0], chunk)
    mkv = proj(mem_prompt.reshape(bsz * n_mem, d), w_mem_kv[0].astype(BF16), ROW_TILE).reshape(bsz, n_mem, 2 * mem_w)
    mk_p, mv_p = mkv[..., :mem_w], mkv[..., mem_w:]
    tables = rope_tables(jnp.arange(seq, dtype=F32))
    o_b, o_m, k_rot = attn_prompt(zq, mk_p, mv_p, sinks[0], tables, window, q_w, kv_w)
    n_all = bsz * seq + dec
    n_buf = -(-n_all // ROW_TILE) * ROW_TILE
    routed = merge(xp, o_a.reshape(-1, c_a), o_b.reshape(-1, q_w), o_m.reshape(-1, mem_w),
                   zg.reshape(bsz * seq, -1), mw, ROW_TILE, alpha, n_buf)
    shift_p = shift_p[:, 0]
    kb_p = k_rot[:, -window:].reshape(bsz, window, H_KV, HEAD)
    vb_p = zq[:, -window:, q_w + kv_w:q_w + 2 * kv_w].reshape(bsz, window, H_KV, HEAD)

    xs = x_sample.reshape(dec, d)
    zq_s, zg_s, zr_s, ops_s = ln_proj_prep(xs.reshape(1, dec, d), ln_in_g, ln_in_b, w_parts,
                                           state_shift[0].reshape(1, dec, c_shift), rp, 1, dec, F32)
    zq_s, zg_s, zr_s = zq_s[0], zg_s[0], zr_s[0]
    o_a_s, wkv_s = wkv_step([a.reshape(dec, c_a) for a in ops_s], state_wkv[0], lnx_g[0], lnx_b[0], STEP_BATCH)
    tables_s = rope_tables(jnp.full((1,), past_len, F32))
    o_b_s, o_m_s, nk_s, nv_s = attn_step(
        zq_s, cache_win_k[0].reshape(dec, window, kv_w), cache_win_v[0].reshape(dec, window, kv_w),
        cache_mem_k[0].reshape(dec, n_mem, mem_w), cache_mem_v[0].reshape(dec, n_mem, mem_w),
        sinks[0], tables_s, q_w, kv_w, STEP_BATCH)
    x1, x1_tiles, eidx, gate = merge(xs, o_a_s, o_b_s, o_m_s, zg_s, mw, dec, alpha, n_buf, bsz * seq, routed)

    y_prompt, y_sample = hier_moe_ln(x1, x1_tiles, eidx, gate, mw, ROW_TILE, alpha, n_all, bsz * seq)
    y_prompt = y_prompt.reshape(bsz, seq, d)
    y_sample = y_sample.reshape(dec, 1, d)

    sd = state_wkv.dtype
    return (y_prompt, y_sample, wkv_p[None].astype(sd), wkv_s[None].astype(sd), shift_p[None], zr_s[None],
            kb_p[None], vb_p[None], nk_s.reshape(dec, window, H_KV, HEAD)[None],
            nv_s.reshape(dec, window, H_KV, HEAD)[None],
            mk_p.reshape(bsz, n_mem, -1, HEAD)[None], mv_p.reshape(bsz, n_mem, -1, HEAD)[None])
```

```python
import functools
import math

import jax
import jax.numpy as jnp
from jax import lax
from jax.experimental import pallas as pl
from jax.experimental.pallas import tpu as pltpu

F32 = jnp.float32
BF16 = jnp.bfloat16

ROW_TILE = 256
WKV_CHUNK = 64
CHUNKS_PER_STEP = 8
SCAN_BATCH = 8
ATTN_BLOCKS = 4
STEP_BATCH = 16
PROJ_PIECE = 512

HEAD = 64
LANES = 128
H_KV = 2
ROT_HALF = 8
ROPE_THETA = 500000.0
PAST_LEN = 8192
N_GROUPS = 4
EXPERTS_PER_GROUP = 8
EXPERT_BLOCK = 128
LN_EPS = 1e-5
LNX_EPS = 64e-5
NEG_INF = -1e30
VMEM_LIMIT = 48 * 1024 * 1024


def _pieces(x, n):
    out = []
    for _ in range(n):
        p = x.astype(BF16)
        out.append(p)
        x = x - p.astype(F32)
    return out


def _mask_dot(x, mask, n=2):
    w = mask.shape[0]
    if w < x.shape[1]:
        return jnp.concatenate([_mask_dot(x[:, c:c + w], mask, n) for c in range(0, x.shape[1], w)], axis=1)
    return sum(jnp.dot(p, mask, preferred_element_type=F32) for p in _pieces(x, n))


def _split3(x, axis, lhs):
    hi = x.astype(BF16).astype(F32)
    lo = x - hi
    return jnp.concatenate([hi, hi, lo] if lhs else [hi, lo, hi], axis=axis).astype(BF16)


def _dot3(a, b):
    return jnp.dot(_split3(a, 1, True), _split3(b, 0, False), preferred_element_type=F32)


def _dot3_t(a, b):
    return lax.dot_general(_split3(a, 1, True), _split3(b, 1, False), (((1,), (1,)), ((), ())),
                           preferred_element_type=F32)


def _bdot(a, b):
    return jnp.dot(a.astype(BF16), b.astype(BF16), preferred_element_type=F32)


def _bdot_t(a, b):
    return lax.dot_general(a.astype(BF16), b.astype(BF16), (((1,), (1,)), ((), ())), preferred_element_type=F32)


def _sigmoid(x):
    return 0.5 * jnp.tanh(0.5 * x) + 0.5


def _ln(x, g, b):
    mu = jnp.mean(x, axis=-1, keepdims=True)
    xc = x - mu
    var = jnp.mean(xc * xc, axis=-1, keepdims=True)
    return xc * lax.rsqrt(var + LN_EPS) * g + b


def _const_spec(shape):
    nd = len(shape)
    return pl.BlockSpec(shape, lambda *_: (0,) * nd)


def _params(sem):
    return pltpu.CompilerParams(dimension_semantics=sem, vmem_limit_bytes=VMEM_LIMIT)


def _ln_proj_prep_kernel(chunk, x_ref, g_ref, b_ref, wr_ref, wq_ref, wg_ref, prev_ref, mu_ref, w0_ref, wup_ref,
                         a0_ref, aup_ref, gup_ref, kk_ref, ka_ref, rk_ref, hsum_ref, tril_ref,
                         zq_ref, zg_ref, zr_ref, *refs):
    out_refs, carry_ref = refs[:-1], refs[-1]
    xn = _ln(x_ref[0], g_ref[...], b_ref[...]).astype(BF16)
    z = jnp.dot(xn, wr_ref[...], preferred_element_type=F32)
    pending = [(o_ref, w_ref, c) for o_ref, w_ref in ((zq_ref, wq_ref), (zg_ref, wg_ref))
               for c in range(0, w_ref.shape[1], PROJ_PIECE)]

    def project(n_pieces):
        for _ in range(min(n_pieces, len(pending))):
            o_ref, w_ref, c = pending.pop(0)
            o_ref[0, :, c:c + PROJ_PIECE] = jnp.dot(xn, w_ref[:, c:c + PROJ_PIECE],
                                                    preferred_element_type=F32).astype(o_ref.dtype)

    tt = z.shape[0]
    c_a = w0_ref.shape[-1]
    r_w, r_a, r_g = wup_ref.shape[0], aup_ref.shape[0], gup_ref.shape[0]
    if chunk == 1:
        zr_ref[0] = z
        prev = prev_ref[0]
    else:
        zr_ref[0] = z[tt - 1:tt, :]

        @pl.when(pl.program_id(1) == 0)
        def _():
            carry_ref[...] = prev_ref[0]

        row = lax.broadcasted_iota(jnp.int32, z.shape, 0)
        prev = jnp.where(row == 0, carry_ref[...], pltpu.roll(z, 1, 0))
        carry_ref[...] = z[tt - 1:tt, :]
    zs = z + (prev - z) * mu_ref[...]
    r = zs[:, :c_a]
    k = zs[:, c_a:2 * c_a]
    v = zs[:, 2 * c_a:3 * c_a]
    o = 3 * c_a
    xw = zs[:, o:o + r_w]
    xa = zs[:, o + r_w:o + r_w + r_a]
    xg = zs[:, o + r_w + r_a:o + r_w + r_a + r_g]
    project(1)
    warg = -(w0_ref[...] + _dot3(jnp.tanh(xw), wup_ref[...]))
    softplus = jnp.maximum(warg, 0.0) + jnp.log1p(jnp.exp(-jnp.abs(warg)))
    lw = -jnp.exp(-softplus - 0.5)
    project(1)
    a = _sigmoid(a0_ref[...] + _dot3(xa, aup_ref[...]))
    g = _dot3(_sigmoid(xg), gup_ref[...])
    project(1)
    kkr = k * kk_ref[...]
    kk = kkr / jnp.maximum(jnp.sqrt(_mask_dot(kkr * kkr, hsum_ref[...])), 1e-12)
    project(1)
    k2 = k * (1.0 + (a - 1.0) * ka_ref[...])
    bonus = _mask_dot(r * k2 * rk_ref[...], hsum_ref[...]) * v
    kb = kk * a
    project(1)
    if chunk == 1:
        outs = (r, jnp.exp(lw), k2, v, -kk, kb, g, bonus)
        for ref, val in zip(out_refs, outs):
            ref[0] = val
        project(len(pending))
        return
    pieces = _pieces(lw, 3)
    ones3 = jnp.ones((chunk, 3 * chunk), BF16)
    cw, cwl = [], []
    for c in range(tt // chunk):
        stack = jnp.concatenate([p[c * chunk:(c + 1) * chunk] for p in pieces], axis=0)
        cw.append(jnp.dot(tril_ref[...], stack, preferred_element_type=F32))
        cwl.append(jnp.dot(ones3, stack, preferred_element_type=F32))
    cw = jnp.concatenate(cw, axis=0)
    cwl = jnp.concatenate(cwl, axis=0)
    project(1)
    e_inv = jnp.exp(-cw)
    outs = (r * jnp.exp(cw), -kk * jnp.exp(cw - lw), kb * e_inv, k2 * e_inv, v, g, bonus)
    for ref, val in zip(out_refs[:-1], outs):
        ref[0] = val
        project(1)
    wl = jnp.exp(cwl)
    for c in range(tt // chunk):
        out_refs[-1][0, c] = wl[c * chunk:c * chunk + 1, :]
    project(len(pending))


def _head_sum_matrix(width):
    assert width % LANES == 0 and LANES % HEAD == 0
    idx = jnp.arange(LANES)
    return ((idx[:, None] // HEAD) == (idx[None, :] // HEAD)).astype(BF16)


def ln_proj_prep(x, g, b, ws, prev, p, chunk, tt, gate_dtype):
    bsz, t, d = x.shape
    w_r, w_q, w_g = ws
    cs = w_r.shape[1]
    c_a = p['w0'].shape[-1]
    assert prev.shape[1] == (t if chunk == 1 else 1)
    ridx = jnp.arange(chunk)
    tril = jnp.tile((ridx[None, :] <= ridx[:, None]).astype(BF16), (1, 3))
    hsum = _head_sum_matrix(c_a)
    row = lambda a: a.reshape(1, -1)
    rows = lambda width: pl.BlockSpec((1, tt, width), lambda bi, i: (bi, i, 0))
    full = jax.ShapeDtypeStruct((bsz, t, c_a), F32)
    if chunk == 1:
        out_specs, out_shape = [rows(c_a)] * 8, [full] * 8
        prev_spec, zr_spec, zr_rows = rows(cs), rows(cs), t
    else:
        out_specs = [rows(c_a)] * 7 + [pl.BlockSpec((1, tt // chunk, 1, c_a), lambda bi, i: (bi, i, 0, 0))]
        out_shape = [full] * 7 + [jax.ShapeDtypeStruct((bsz, t // chunk, 1, c_a), F32)]
        prev_spec = zr_spec = pl.BlockSpec((1, 1, cs), lambda bi, i: (bi, 0, 0))
        zr_rows = 1
    consts = [row(p['mu']), row(p['w0']), p['w_up'], row(p['a0']), p['a_up'], p['g_up'], row(p['k_k']),
              row(p['k_a']), row(p['r_k']), hsum, tril]
    weight = lambda w: pl.BlockSpec(w.shape, lambda bi, i: (0, 0), pipeline_mode=pl.Buffered(1))
    outs = pl.pallas_call(
        functools.partial(_ln_proj_prep_kernel, chunk), name="ln_proj_prep",
        grid=(bsz, t // tt),
        in_specs=[rows(d), _const_spec((1, d)), _const_spec((1, d)), weight(w_r), weight(w_q), weight(w_g), prev_spec]
        + [_const_spec(c.shape) for c in consts],
        out_specs=[rows(w_q.shape[1]), rows(w_g.shape[1]), zr_spec] + out_specs,
        out_shape=[jax.ShapeDtypeStruct((bsz, t, w_q.shape[1]), F32),
                   jax.ShapeDtypeStruct((bsz, t, w_g.shape[1]), gate_dtype),
                   jax.ShapeDtypeStruct((bsz, zr_rows, cs), F32)] + out_shape,
        scratch_shapes=[pltpu.VMEM((1, cs), F32)],
        compiler_params=_params(("parallel", "arbitrary")),
    )(x, row(g), row(b), w_r, w_q, w_g, prev, *consts)
    return outs[0], outs[1], outs[2], outs[3:]


def _wkv_chunk_kernel(ra_ref, at_ref, bt_ref, kt_ref, v_ref, wl_ref, rp_ref, y0_ref, m_ref, n_ref):
    n_chunks = m_ref.shape[1]
    length = ra_ref.shape[1] // n_chunks
    heads = ra_ref.shape[2] // HEAD
    row = lax.broadcasted_iota(jnp.int32, (length, length), 0)
    col = lax.broadcasted_iota(jnp.int32, (length, length), 1)
    strict = row > col
    incl = row >= col
    hrow = lax.broadcasted_iota(jnp.int32, (HEAD, HEAD), 0)
    hcol = lax.broadcasted_iota(jnp.int32, (HEAD, HEAD), 1)
    units = [(c, h) for c in range(n_chunks) for h in range(heads)]
    us = range(len(units))
    at = lambda ref, u: ref[0, units[u][0] * length:(units[u][0] + 1) * length,
                            units[u][1] * HEAD:(units[u][1] + 1) * HEAD]
    gram = [_bdot_t(jnp.concatenate([at(at_ref, u), at(ra_ref, u)], axis=0),
                    jnp.concatenate([at(bt_ref, u), at(kt_ref, u)], axis=0)) for u in us]
    a_ab = [jnp.where(strict, g[:length, :length], 0.0) for g in gram]
    a_kk = [jnp.concatenate([jnp.where(strict, g[:length, length:], 0.0),
                             jnp.where(incl, g[length:, length:], 0.0)], axis=0) for g in gram]
    a_rb = [jnp.where(incl, g[length:, :length], 0.0) for g in gram]
    inv = [jnp.where(row == col, 1.0, a) for a in a_ab]
    pw = [_bdot(a, a) for a in a_ab]
    avy = [_bdot(a_kk[u], at(v_ref, u)) for u in us]
    w_l = [wl_ref[0, units[u][0], :, units[u][1] * HEAD:(units[u][1] + 1) * HEAD] for u in us]
    nk = [_bdot(at(v_ref, u).T, at(kt_ref, u) * w_l[u]) for u in us]
    for _ in range(int(math.log2(length)) - 2):
        both = [_bdot(jnp.concatenate([pw[u], inv[u]], axis=0), pw[u]) for u in us]
        pw = [b[:length] for b in both]
        inv = [inv[u] + both[u][length:] for u in us]
    inv = [inv[u] + _bdot(inv[u], pw[u]) for u in us]
    pq = [_bdot(inv[u], jnp.concatenate([at(at_ref, u), avy[u][:length]], axis=1)) for u in us]
    ry = [_bdot(a_rb[u], pq[u]) for u in us]
    mn = [_bdot(pq[u].T, at(bt_ref, u) * w_l[u]) for u in us]
    lanes = lambda f, c: jnp.concatenate([f(c * heads + h) for h in range(heads)], axis=1)
    rows = lambda f: jnp.concatenate([lanes(f, c) for c in range(n_chunks)], axis=0)
    rp_ref[0] = rows(lambda u: at(ra_ref, u) + ry[u][:, :HEAD])
    y0_ref[0] = rows(lambda u: avy[u][length:] + ry[u][:, HEAD:])
    for c in range(n_chunks):
        m_ref[0, c] = lanes(lambda u: jnp.where(hrow == hcol, w_l[u], 0.0) + mn[u][:HEAD], c)
        n_ref[0, c] = lanes(lambda u: nk[u] + mn[u][HEAD:], c)


def _head_norm_wide(y, hsum):
    yc = y - _mask_dot(y, hsum) * (1.0 / HEAD)
    return yc * lax.rsqrt(_mask_dot(yc * yc, hsum) * (1.0 / HEAD) + LNX_EPS)


def _wkv_scan_kernel(rp_ref, y0_ref, m_ref, n_ref, g_ref, bonus_ref, s0_ref, lg_ref, lb_ref, hsum_ref,
                     o_ref, sout_ref, s_ref):
    c = pl.program_id(1)
    nb = rp_ref.shape[0]
    heads = rp_ref.shape[2] // HEAD

    @pl.when(c == 0)
    def _():
        s_ref[...] = s0_ref[...]

    pairs = [(b, h, slice(h * HEAD, (h + 1) * HEAD)) for b in range(nb) for h in range(heads)]
    s = [s_ref[b, h] for b, h, _ in pairs]
    s_new = [n_ref[b, 0, :, sl] + _dot3(s[j], m_ref[b, 0, :, sl]) for j, (b, h, sl) in enumerate(pairs)]
    ys = [y0_ref[b, :, sl] + _bdot_t(rp_ref[b, :, sl], s[j]) for j, (b, h, sl) in enumerate(pairs)]
    for j, (b, h, _) in enumerate(pairs):
        s_ref[b, h] = s_new[j]
    length = rp_ref.shape[1]
    y = _head_norm_wide(jnp.concatenate([jnp.concatenate(ys[b * heads:(b + 1) * heads], axis=-1) for b in range(nb)],
                                        axis=0), hsum_ref[...])
    for b in range(nb):
        o = (y[b * length:(b + 1) * length] * lg_ref[...] + lb_ref[...] + bonus_ref[b]) * g_ref[b]
        o_ref[b] = o.astype(o_ref.dtype)

    @pl.when(c == pl.num_programs(1) - 1)
    def _():
        sout_ref[...] = s_ref[...]


def wkv(prep, s0, lnx_g, lnx_b, chunk):
    ra, at, bt, kt, v, g, bonus, wl = prep
    bsz, t, c_a = ra.shape
    heads = c_a // HEAD
    n_chunks = t // chunk
    assert chunk & (chunk - 1) == 0 and chunk >= 4 and t % chunk == 0
    per_step = CHUNKS_PER_STEP if n_chunks % CHUNKS_PER_STEP == 0 else 1
    tile = pl.BlockSpec((1, per_step * chunk, c_a), lambda b, c: (b, c, 0))
    mat = pl.BlockSpec((1, per_step, HEAD, c_a), lambda b, c: (b, c, 0, 0))
    full = jax.ShapeDtypeStruct((bsz, t, c_a), F32)
    mats = jax.ShapeDtypeStruct((bsz, n_chunks, HEAD, c_a), F32)
    rp, y0, m, n = pl.pallas_call(
        _wkv_chunk_kernel, name="wkv_chunk",
        grid=(bsz, n_chunks // per_step),
        in_specs=[tile] * 5 + [pl.BlockSpec((1, per_step, 1, c_a), lambda b, c: (b, c, 0, 0))],
        out_specs=[tile, tile, mat, mat],
        out_shape=[full, full, mats, mats],
        compiler_params=_params(("parallel", "parallel")),
    )(ra, at, bt, kt, v, wl)
    nb = SCAN_BATCH if bsz % SCAN_BATCH == 0 else 1
    tile = pl.BlockSpec((nb, chunk, c_a), lambda b, c: (b, c, 0))
    mat = pl.BlockSpec((nb, 1, HEAD, c_a), lambda b, c: (b, c, 0, 0))
    st = pl.BlockSpec((nb, heads, HEAD, HEAD), lambda b, c: (b, 0, 0, 0))
    return pl.pallas_call(
        _wkv_scan_kernel, name="wkv_scan",
        grid=(bsz // nb, n_chunks),
        in_specs=[tile, tile, mat, mat, tile, tile, st, _const_spec((1, c_a)), _const_spec((1, c_a)),
                  _const_spec((LANES, LANES))],
        out_specs=[tile, st],
        out_shape=[jax.ShapeDtypeStruct((bsz, t, c_a), BF16), jax.ShapeDtypeStruct(s0.shape, F32)],
        scratch_shapes=[pltpu.VMEM((nb, heads, HEAD, HEAD), F32)],
        compiler_params=_params(("parallel", "arbitrary")),
    )(rp, y0, m, n, g, bonus, s0, lnx_g.reshape(1, c_a), lnx_b.reshape(1, c_a), _head_sum_matrix(c_a))


def _wkv_step_kernel(r_ref, w_ref, k_ref, v_ref, a_ref, b_ref, g_ref, bonus_ref, s_ref, lg_ref, lb_ref, hsum_ref,
                     o_ref, sout_ref):
    bt, heads = s_ref.shape[0], s_ref.shape[1]
    c_a = heads * HEAD
    hsum = hsum_ref[...]
    diag = (lax.broadcasted_iota(jnp.int32, (HEAD, c_a), 1) % HEAD
            == lax.broadcasted_iota(jnp.int32, (HEAD, c_a), 0))
    seqs = range(bt)
    row = lambda ref, i: ref[i:i + 1, :]
    stack = lambda f: jnp.concatenate([f(i) for i in seqs], axis=0)
    piece = lambda x, i: x[i * HEAD:(i + 1) * HEAD]
    s = stack(lambda i: jnp.concatenate([s_ref[i, h] for h in range(heads)], axis=1))
    sa = _mask_dot(s * stack(lambda i: jnp.broadcast_to(row(a_ref, i), (HEAD, c_a))), hsum)
    v_rows = _mask_dot(stack(lambda i: jnp.where(diag, row(v_ref, i), 0.0)), hsum)
    s = stack(lambda i: piece(s, i) * row(w_ref, i) + piece(sa, i) * row(b_ref, i) + piece(v_rows, i) * row(k_ref, i))
    for i in seqs:
        for h in range(heads):
            sout_ref[i, h] = piece(s, i)[:, h * HEAD:(h + 1) * HEAD]
    y_rows = _mask_dot(stack(lambda i: piece(s, i) * row(r_ref, i)), hsum)
    y = stack(lambda i: jnp.sum(jnp.where(diag, piece(y_rows, i), 0.0), axis=0, keepdims=True))
    y = _head_norm_wide(y, hsum)
    o_ref[...] = (y * lg_ref[...] + lb_ref[...] + bonus_ref[...]) * g_ref[...]


def wkv_step(ops, s0, lnx_g, lnx_b, bt):
    bsz, c_a = ops[0].shape
    rows = pl.BlockSpec((bt, c_a), lambda i: (i, 0))
    st = pl.BlockSpec((bt,) + s0.shape[1:], lambda i: (i, 0, 0, 0))
    hsum = _head_sum_matrix(c_a)
    return pl.pallas_call(
        _wkv_step_kernel, name="wkv_step",
        grid=(bsz // bt,),
        in_specs=[rows] * 8 + [st, _const_spec((1, c_a)), _const_spec((1, c_a)), _const_spec(hsum.shape)],
        out_specs=[rows, st],
        out_shape=[jax.ShapeDtypeStruct((bsz, c_a), F32), jax.ShapeDtypeStruct(s0.shape, F32)],
        compiler_params=_params(("parallel",)),
    )(*ops, s0, lnx_g.reshape(1, c_a), lnx_b.reshape(1, c_a), hsum)


def rope_tables(pos):
    inv_freq = ROPE_THETA ** (-jnp.arange(ROT_HALF, dtype=F32) / ROT_HALF)
    ang = pos[:, None] * inv_freq[None, :]
    cos, sin = jnp.cos(ang), jnp.sin(ang)
    t = pos.shape[0]
    rest = HEAD - 2 * ROT_HALF
    c = jnp.concatenate([cos, cos, jnp.ones((t, rest), F32)], axis=1)
    s1 = jnp.concatenate([jnp.zeros((t, ROT_HALF), F32), sin, jnp.zeros((t, rest), F32)], axis=1)
    s2 = jnp.concatenate([-sin, jnp.zeros((t, HEAD - ROT_HALF), F32)], axis=1)
    rep = LANES // HEAD
    return tuple(jnp.tile(x, (1, rep)) for x in (c, s1, s2))


def _rope(x, c, s1, s2):
    width = x.shape[-1]
    rep = width // c.shape[-1]
    if rep > 1:
        c, s1, s2 = (jnp.concatenate([t] * rep, axis=1) for t in (c, s1, s2))
    return x * c + pltpu.roll(x, ROT_HALF, 1) * s1 + pltpu.roll(x, width - ROT_HALF, 1) * s2


def _attn_kernel(window, q_w, kv_w, zq_ref, kp_ref, vp_ref, rc_ref, rs1_ref, rs2_ref, pc_ref, ps1_ref, ps2_ref,
                 mk_ref, mv_ref, sink_ref, ob_ref, om_ref, kr_ref):
    zq = zq_ref[0]
    nblk = zq.shape[0] // window
    first = pl.program_id(1) * nblk
    scale = HEAD ** -0.5
    q = zq[:, :q_w]
    k = zq[:, q_w:q_w + kv_w]
    v = zq[:, q_w + kv_w:q_w + 2 * kv_w]
    qm = zq[:, q_w + 2 * kv_w:]
    qr = _rope(q, rc_ref[...], rs1_ref[...], rs2_ref[...]) * scale
    kr = _rope(k, rc_ref[...], rs1_ref[...], rs2_ref[...])
    kr_ref[0] = kr
    k_all = jnp.concatenate([_rope(kp_ref[0], pc_ref[...], ps1_ref[...], ps2_ref[...]), kr], axis=0)
    v_all = jnp.concatenate([vp_ref[0], v], axis=0)
    gqa = q_w // kv_w
    assert window & (window - 1) == 0
    qi = lax.broadcasted_iota(jnp.int32, (gqa * window, 2 * window), 0) & (window - 1)
    kj = lax.broadcasted_iota(jnp.int32, (gqa * window, 2 * window), 1)
    band = (kj > qi) & (kj <= qi + window)
    tdot = lambda a, b: lax.dot_general(a, b, (((1,), (1,)), ((), ())), preferred_element_type=F32)
    dot = lambda a, b: jnp.dot(a, b, preferred_element_type=F32)
    hsl = lambda h: slice(h * HEAD, (h + 1) * HEAD)
    rows = lambda i: slice(i * window, (i + 1) * window)
    keys = lambda i: slice(i * window, (i + 2) * window)
    wins = [(i, hk) for i in range(nblk) for hk in range(kv_w // HEAD)]
    mems = [(i, h) for i in range(nblk) for h in range(qm.shape[1] // HEAD)]
    qmb = (qm * scale).astype(BF16)
    mk = mk_ref[0].astype(BF16)
    mv = mv_ref[0].astype(BF16)
    kcat = [k_all[keys(i), hsl(hk)].astype(BF16) for i, hk in wins]
    vcat = [v_all[keys(i), hsl(hk)].astype(BF16) for i, hk in wins]
    qs = [jnp.concatenate([qr[rows(i), hsl(hk * gqa + g)] for g in range(gqa)], axis=0).astype(BF16)
          for i, hk in wins]
    s = [jnp.where(band & ((kj >= window) | (first + i > 0)), tdot(qs[u], kcat[u]), NEG_INF)
         for u, (i, hk) in enumerate(wins)]
    sm = [tdot(qmb[rows(i), hsl(h)], mk[:, hsl(h)]) for i, h in mems]
    sink = [jnp.concatenate([jnp.full((window, 1), sink_ref[hk * gqa + g], F32) for g in range(gqa)], axis=0)
            for i, hk in wins]
    m = [jnp.maximum(jnp.max(s[u], axis=-1, keepdims=True), sink[u]) for u in range(len(wins))]
    p = [jnp.exp(s[u] - m[u]) for u in range(len(wins))]
    pm = [jnp.exp(x - jnp.max(x, axis=-1, keepdims=True)) for x in sm]
    o = [dot(p[u].astype(BF16), vcat[u])
         / (jnp.sum(p[u], axis=-1, keepdims=True) + jnp.exp(sink[u] - m[u])) for u in range(len(wins))]
    om = [dot(pm[u].astype(BF16), mv[:, hsl(h)]) / jnp.sum(pm[u], axis=-1, keepdims=True)
          for u, (i, h) in enumerate(mems)]
    n_kv, n_mh = kv_w // HEAD, qm.shape[1] // HEAD
    ob = jnp.concatenate([jnp.concatenate([o[i * n_kv + hk][g * window:(g + 1) * window]
                                           for hk in range(n_kv) for g in range(gqa)], axis=1)
                          for i in range(nblk)], axis=0)
    ob_ref[0] = ob.astype(ob_ref.dtype)
    om_ref[0] = jnp.concatenate([jnp.concatenate(om[i * n_mh:(i + 1) * n_mh], axis=1) for i in range(nblk)],
                                axis=0).astype(om_ref.dtype)


def attn_prompt(zq, mk, mv, sinks, tables, window, q_w, kv_w):
    bsz, t, zw = zq.shape
    mem_w = zw - q_w - 2 * kv_w
    assert kv_w == LANES and q_w % kv_w == 0
    kcol, vcol = q_w // kv_w, q_w // kv_w + 1
    nblk = ATTN_BLOCKS if (t // window) % ATTN_BLOCKS == 0 else 1
    rows = nblk * window
    prev = lambda n: jnp.maximum(n * nblk - 1, 0)
    tab = pl.BlockSpec((rows, LANES), lambda b, n: (n, 0))
    ptab = pl.BlockSpec((window, LANES), lambda b, n: (prev(n), 0))
    mem = pl.BlockSpec((1,) + mk.shape[1:], lambda b, n: (b, 0, 0))
    return pl.pallas_call(
        functools.partial(_attn_kernel, window, q_w, kv_w), name="attn",
        grid=(bsz, t // rows),
        in_specs=[pl.BlockSpec((1, rows, zw), lambda b, n: (b, n, 0)),
                  pl.BlockSpec((1, window, kv_w), lambda b, n: (b, prev(n), kcol)),
                  pl.BlockSpec((1, window, kv_w), lambda b, n: (b, prev(n), vcol)),
                  tab, tab, tab, ptab, ptab, ptab, mem, mem,
                  pl.BlockSpec(memory_space=pltpu.SMEM)],
        out_specs=[pl.BlockSpec((1, rows, q_w), lambda b, n: (b, n, 0)),
                   pl.BlockSpec((1, rows, mem_w), lambda b, n: (b, n, 0)),
                   pl.BlockSpec((1, rows, kv_w), lambda b, n: (b, n, 0))],
        out_shape=[jax.ShapeDtypeStruct((bsz, t, q_w), BF16), jax.ShapeDtypeStruct((bsz, t, mem_w), BF16),
                   jax.ShapeDtypeStruct((bsz, t, kv_w), F32)],
        compiler_params=_params(("parallel", "parallel")),
    )(zq, zq, zq, *tables, *tables, mk, mv, sinks)


def _attn_step_kernel(q_w, kv_w, zq_ref, ck_ref, cv_ref, mk_ref, mv_ref, rc_ref, rs1_ref, rs2_ref, sink_ref,
                      ob_ref, om_ref, nk_ref, nv_ref):
    bt = zq_ref.shape[0]
    window = ck_ref.shape[1]
    mem_w = om_ref.shape[1]
    n_q, gqa, per_vreg = q_w // HEAD, q_w // kv_w, LANES // HEAD
    scale = HEAD ** -0.5
    zq = zq_ref[...]
    q = _rope(zq[:, :q_w], rc_ref[...], rs1_ref[...], rs2_ref[...]) * scale
    k_new = _rope(zq[:, q_w:q_w + kv_w], rc_ref[...], rs1_ref[...], rs2_ref[...])
    v_new = zq[:, q_w + kv_w:q_w + 2 * kv_w]
    qm = zq[:, q_w + 2 * kv_w:] * scale
    own = lambda w: (lax.broadcasted_iota(jnp.int32, (n_q, w), 1) // HEAD
                     == lax.broadcasted_iota(jnp.int32, (n_q, w), 0))
    own_q, own_m = own(q_w), own(mem_w)
    hrow = lax.broadcasted_iota(jnp.int32, (n_q, LANES), 0)
    hblk = lax.broadcasted_iota(jnp.int32, (n_q, LANES), 1) // HEAD
    swap = (hrow % per_vreg) != (hrow // gqa)
    keep = hblk == hrow % per_vreg
    key_ok = lax.broadcasted_iota(jnp.int32, (n_q, window), 1) >= 1
    wrow = lax.broadcasted_iota(jnp.int32, (window, kv_w), 0)
    sink = sink_ref[...]
    bs = range(bt)
    tdot = lambda a, b: lax.dot_general(a.astype(BF16), b.astype(BF16), (((1,), (1,)), ((), ())),
                                        preferred_element_type=F32)
    for b in bs:
        nk_ref[b] = jnp.where(wrow == window - 1, k_new[b:b + 1], pltpu.roll(ck_ref[b], window - 1, 0))
        nv_ref[b] = jnp.where(wrow == window - 1, v_new[b:b + 1], pltpu.roll(cv_ref[b], window - 1, 0))
    q8 = []
    for b in bs:
        rep = jnp.where(own_q, q[b:b + 1], 0.0)
        fold = sum(rep[:, c * LANES:(c + 1) * LANES] for c in range(q_w // LANES))
        q8.append(jnp.where(swap, pltpu.roll(fold, HEAD, 1), fold))
    qm8 = [jnp.where(own_m, qm[b:b + 1], 0.0) for b in bs]
    s = [jnp.where(key_ok, tdot(q8[b], ck_ref[b]), NEG_INF) for b in bs]
    sm = [tdot(qm8[b], mk_ref[b]) for b in bs]
    s_new = [jnp.sum(q8[b] * k_new[b:b + 1], axis=1, keepdims=True) for b in bs]
    m = [jnp.maximum(jnp.maximum(jnp.max(s[b], axis=1, keepdims=True), s_new[b]), sink) for b in bs]
    p = [jnp.exp(s[b] - m[b]) for b in bs]
    p_new = [jnp.exp(s_new[b] - m[b]) for b in bs]
    pm = [jnp.exp(sm[b] - jnp.max(sm[b], axis=1, keepdims=True)) for b in bs]
    o8 = [(_bdot(p[b], cv_ref[b]) + p_new[b] * v_new[b:b + 1])
          / (jnp.sum(p[b], axis=1, keepdims=True) + p_new[b] + jnp.exp(sink - m[b])) for b in bs]
    om8 = [_bdot(pm[b], mv_ref[b]) / jnp.sum(pm[b], axis=1, keepdims=True) for b in bs]
    ob, om = [], []
    for b in bs:
        o = jnp.where(swap, pltpu.roll(o8[b], HEAD, 1), o8[b])
        o = jnp.concatenate([jnp.where(keep, o, 0.0)] * (q_w // LANES), axis=1)
        ob.append(jnp.sum(jnp.where(own_q, o, 0.0), axis=0, keepdims=True))
        om.append(jnp.sum(jnp.where(own_m, om8[b], 0.0), axis=0, keepdims=True))
    ob_ref[...] = jnp.concatenate(ob, axis=0)
    om_ref[...] = jnp.concatenate(om, axis=0)


def attn_step(zq, ck, cv, mk, mv, sinks, tables, q_w, kv_w, bt):
    bsz, zw = zq.shape
    mem_w = zw - q_w - 2 * kv_w
    window, n_mem = ck.shape[1], mk.shape[1]
    n_q = q_w // HEAD
    assert kv_w == LANES and mem_w // HEAD <= n_q and n_q == 8
    rows = lambda w: pl.BlockSpec((bt, w), lambda i: (i, 0))
    cache = pl.BlockSpec((bt, window, kv_w), lambda i: (i, 0, 0))
    mem = pl.BlockSpec((bt, n_mem, mem_w), lambda i: (i, 0, 0))
    consts = list(tables) + [sinks.reshape(n_q, 1)]
    const_specs = [_const_spec(c.shape) for c in consts]
    return pl.pallas_call(
        functools.partial(_attn_step_kernel, q_w, kv_w), name="attn_step",
        grid=(bsz // bt,),
        in_specs=[rows(zw), cache, cache, mem, mem] + const_specs,
        out_specs=[rows(q_w), rows(mem_w), cache, cache],
        out_shape=[jax.ShapeDtypeStruct((bsz, q_w), F32), jax.ShapeDtypeStruct((bsz, mem_w), F32),
                   jax.ShapeDtypeStruct(ck.shape, F32), jax.ShapeDtypeStruct(cv.shape, F32)],
        compiler_params=_params(("parallel",)),
    )(zq, ck, cv, mk, mv, *consts)


def _proj_kernel(x_ref, w_ref, o_ref):
    o_ref[...] = jnp.dot(x_ref[...].astype(BF16), w_ref[...], preferred_element_type=F32)


def proj(x, w, tm):
    n, d = x.shape
    return pl.pallas_call(
        _proj_kernel, name="proj",
        grid=(n // tm,),
        in_specs=[pl.BlockSpec((tm, d), lambda i: (i, 0)), _const_spec(w.shape)],
        out_specs=pl.BlockSpec((tm, w.shape[1]), lambda i: (i, 0)),
        out_shape=jax.ShapeDtypeStruct((n, w.shape[1]), F32),
        compiler_params=_params(("parallel",)),
    )(x, w)


def _merge_kernel(alpha, n_tiles, x_ref, oa_ref, ob_ref, om_ref, gt_ref, lig_ref, lib_ref, pa_ref, pb_ref, pm_ref,
                  wo_ref, l1g_ref, l1b_ref, wr_ref, br_ref, *refs):
    outs = refs[-4:]

    @pl.when(pl.program_id(0) >= n_tiles)
    def _():
        for ref in outs:
            ref[...] = jnp.zeros_like(ref)

    @pl.when(pl.program_id(0) < n_tiles)
    def _():
        _merge_tile(alpha, x_ref, oa_ref, ob_ref, om_ref, gt_ref, lig_ref, lib_ref, pa_ref, pb_ref, pm_ref, wo_ref,
                    l1g_ref, l1b_ref, wr_ref, br_ref, *outs)


def _merge_tile(alpha, x_ref, oa_ref, ob_ref, om_ref, gt_ref, lig_ref, lib_ref, pa_ref, pb_ref, pm_ref, wo_ref,
                l1g_ref, l1b_ref, wr_ref, br_ref, x1_ref, x1t_ref, eidx_ref, gate_ref):
    d = x_ref.shape[1]
    xn = _ln(x_ref[...], lig_ref[...], lib_ref[...])
    gts = _sigmoid(gt_ref[...].astype(F32))
    merged = (gts[:, :d] * _bdot(oa_ref[...], pa_ref[...]) + gts[:, d:2 * d] * _bdot(ob_ref[...], pb_ref[...])
              + gts[:, 2 * d:] * _bdot(om_ref[...], pm_ref[...]))
    x1 = _ln(alpha * xn + _bdot(merged, wo_ref[...]), l1g_ref[...], l1b_ref[...])
    x1_ref[...] = x1
    _rows_to_tiles(x1t_ref, x1)
    x_hi, x_lo = _pieces(x1, 2)
    w_hi, w_lo = wr_ref[0], wr_ref[1]
    dot = lambda a, b: jnp.dot(a, b, preferred_element_type=F32)
    logits = dot(x_hi, w_hi) + dot(x_hi, w_lo) + dot(x_lo, w_hi) + br_ref[...]
    lane = lax.broadcasted_iota(jnp.int32, logits.shape, 1)
    lane_f = lane.astype(F32)
    first = lambda hit: jnp.min(jnp.where(hit, lane_f, float(LANES)), axis=-1, keepdims=True).astype(jnp.int32)
    gmask = lane < N_GROUPS
    gl = jnp.where(gmask, logits, NEG_INF)
    gmax = jnp.max(gl, axis=-1, keepdims=True)
    gidx = first(gl == gmax)
    g_w = 1.0 / jnp.sum(jnp.where(gmask, jnp.exp(gl - gmax), 0.0), axis=-1, keepdims=True)
    lo = N_GROUPS + gidx * EXPERTS_PER_GROUP
    el = jnp.where((lane >= lo) & (lane < lo + EXPERTS_PER_GROUP), logits, NEG_INF)
    v1 = jnp.max(el, axis=-1, keepdims=True)
    i1 = first(el == v1)
    el2 = jnp.where(lane == i1, NEG_INF, el)
    v2 = jnp.max(el2, axis=-1, keepdims=True)
    i2 = first(el2 == v2)
    e2 = jnp.exp(v2 - v1)
    gate1 = g_w / (1.0 + e2)
    eidx = jnp.where(lane == 0, (i1 - N_GROUPS).astype(F32), jnp.where(lane == 1, (i2 - N_GROUPS).astype(F32), 0.0))
    eidx_ref[...] = eidx.T[:eidx_ref.shape[0]].astype(jnp.int32)
    gate_ref[...] = jnp.where(lane == 0, gate1, jnp.where(lane == 1, gate1 * e2, 0.0))


def merge(x, oa, ob, om, gt, w, tm, alpha, n_total, row_offset=0, into=None):
    n, d = x.shape
    assert row_offset % tm == 0 and n % tm == 0 and n_total % tm == 0
    off = row_offset // tm
    into = list(into or [])
    n_tiles = n // tm
    steps = n_tiles if into else n_total // tm
    rows = lambda a: pl.BlockSpec((tm, a.shape[1]), lambda i: (jnp.minimum(i, n_tiles - 1), 0))
    out = lambda width: pl.BlockSpec((tm, width), lambda i: (i + off, 0))
    consts = [w['ln_in_g'], w['ln_in_b'], w['p_a'], w['p_b'], w['p_m'], w['w_o'], w['ln1_g'], w['ln1_b'],
              w['w_route'], w['b_route']]
    n_in = 5 + len(consts)
    return pl.pallas_call(
        functools.partial(_merge_kernel, alpha, n_tiles), name="merge",
        grid=(steps,),
        in_specs=[rows(a) for a in (x, oa, ob, om, gt)] + [_const_spec(c.shape) for c in consts]
        + [pl.BlockSpec(memory_space=pl.ANY)] * len(into),
        out_specs=[out(d), pl.BlockSpec((tm * TILE_ROWS, LANES), lambda i: (i + off, 0)),
                   pl.BlockSpec((TILE_ROWS, tm), lambda i: (0, i + off)), out(LANES)],
        out_shape=[jax.ShapeDtypeStruct((n_total, d), F32), jax.ShapeDtypeStruct((n_total * TILE_ROWS, LANES), F32),
                   jax.ShapeDtypeStruct((TILE_ROWS, n_total), jnp.int32), jax.ShapeDtypeStruct((n_total, LANES), F32)],
        input_output_aliases={n_in + k: k for k in range(len(into))},
        compiler_params=_params(("parallel",)),
    )(x, oa, ob, om, gt, *consts, *into)


ROW_DMA_UNROLL = 8
MOE_BUFFERS = 4
DRAIN_STEPS = 2


TILE_ROWS = 8


def _rows_from_tiles(ref, n):
    return jnp.concatenate([ref[pl.ds(s, n, stride=TILE_ROWS), :] for s in range(TILE_ROWS)], axis=1)


def _rows_to_tiles(ref, x):
    for s in range(TILE_ROWS):
        ref[pl.ds(s, x.shape[0], stride=TILE_ROWS), :] = x[:, s * LANES:(s + 1) * LANES]


def _row_copies(asg_ref, base, count, n_asg, x_hbm, buf, y_hbm, sem, gather, unrolled):
    def tile(ref, idx):
        start = idx * TILE_ROWS
        return ref.at[pl.ds(start if isinstance(idx, int) else pl.multiple_of(start, TILE_ROWS), TILE_ROWS)]

    def one(r, priority):
        a = asg_ref[base + r]
        if gather:
            tok = jnp.minimum(a, n_asg - 1)
            tok = jnp.where(tok >= n_asg // 2, tok - n_asg // 2, tok)
            copy = pltpu.make_async_copy(tile(x_hbm, tok), tile(buf, r), sem)
        else:
            copy = pltpu.make_async_copy(tile(buf, r), tile(y_hbm, a), sem)
        copy.start(priority=priority)

    if unrolled:
        for r in range(count):
            one(r, r % 2)
        return

    def body(g, carry):
        for j in range(ROW_DMA_UNROLL):
            one(g * ROW_DMA_UNROLL + j, j % 2)
        return carry
    lax.fori_loop(0, count // ROW_DMA_UNROLL, body, 0)


def _moe_expert_kernel(n_tokens, n_asg, asg_ref, be_ref, nu_ref, x_hbm, wg_ref, wu_ref, wd_ref, y_hbm,
                       xbuf, ybuf, wgb, wub, wdb, gsem, ssem):
    i = pl.program_id(0)
    used = nu_ref[0]
    rows = xbuf.shape[1] // TILE_ROWS
    n_buf = xbuf.shape[0]
    n_blocks = be_ref.shape[0]
    blk = jnp.minimum(i, n_blocks - 1)

    @pl.when((i == 0) | (be_ref[blk] != be_ref[jnp.maximum(blk - 1, 0)]))
    def _():
        wgb[...] = wg_ref[0, 0].astype(BF16)
        wub[...] = wu_ref[0, 0].astype(BF16)
        wdb[...] = wd_ref[0, 0].astype(BF16)

    def wait_gather(slot):
        pltpu.make_async_copy(x_hbm.at[pl.ds(0, rows * TILE_ROWS)], xbuf.at[slot], gsem.at[slot]).wait()

    def wait_scatter(slot):
        pltpu.make_async_copy(ybuf.at[slot], y_hbm.at[pl.ds(0, rows * TILE_ROWS)], ssem.at[slot]).wait()

    def gather(b, slot, unrolled):
        _row_copies(asg_ref, b * rows, rows, n_asg, x_hbm, xbuf.at[slot], y_hbm, gsem.at[slot], True, unrolled)

    def scatter(b, slot, unrolled):
        _row_copies(asg_ref, b * rows, rows, n_asg, x_hbm, ybuf.at[slot], y_hbm, ssem.at[slot], False, unrolled)

    def expert(slot):
        xb = _rows_from_tiles(xbuf.at[slot], rows).astype(BF16)
        hg = jnp.dot(xb, wgb[...], preferred_element_type=F32)
        hu = jnp.dot(xb, wub[...], preferred_element_type=F32)
        h = hg * _sigmoid(hg) * hu
        _rows_to_tiles(ybuf.at[slot], jnp.dot(h.astype(BF16), wdb[...], preferred_element_type=F32))

    @pl.when(i == 0)
    def _():
        ybuf[1] = jnp.zeros(ybuf.shape[1:], F32)
        stride = n_asg // 2
        gaps = [(k * stride + n_tokens, stride - n_tokens) for k in range(2)]
        gaps.append((n_asg, y_hbm.shape[0] // TILE_ROWS - n_asg))
        fills = [pltpu.make_async_copy(ybuf.at[1, pl.ds(0, min(rows, count - j) * TILE_ROWS)],
                                       y_hbm.at[pl.ds((start + j) * TILE_ROWS, min(rows, count - j) * TILE_ROWS)],
                                       ssem.at[1])
                 for start, count in gaps for j in range(0, count, rows)]
        for copy in fills:
            copy.start()
        for copy in fills:
            copy.wait()

    @pl.when((i >= 2) & (i - 2 < used))
    def _():
        wait_scatter((i - 2) % n_buf)

    ahead = n_buf - 1
    steady = (i >= 1) & (i + ahead < used)
    for slot in range(n_buf):
        @pl.when(steady & (i % n_buf == slot))
        def _():
            wait_gather(slot)
            gather(i + ahead, (slot + ahead) % n_buf, True)
            scatter(i - 1, (slot - 1) % n_buf, True)
            expert(slot)

    @pl.when(jnp.logical_not(steady))
    def _():
        slot = i % n_buf

        @pl.when(i == 0)
        def _():
            for b in range(ahead):
                @pl.when(b < used)
                def _():
                    gather(b, b, False)

        @pl.when(i < used)
        def _():
            wait_gather(slot)

        @pl.when(i + ahead < used)
        def _():
            gather(i + ahead, (i + ahead) % n_buf, False)

        @pl.when((i >= 1) & (i - 1 < used))
        def _():
            scatter(i - 1, (i - 1) % n_buf, False)

        @pl.when(i < used)
        def _():
            expert(slot)


def moe_experts(x1_tiles, n_tokens, stride, asg, blk_e, n_used, e_gate, e_up, e_down):
    d = e_gate.shape[2]
    assert d == TILE_ROWS * LANES and x1_tiles.shape == (stride * TILE_ROWS, LANES)
    n_blocks = blk_e.shape[0]
    ff = e_gate.shape[-1]
    n_asg = 2 * stride
    n_rows = n_asg + e_gate.shape[1] * EXPERT_BLOCK
    weight = lambda shape: pl.BlockSpec(
        (1, 1) + shape, lambda i, asg, be, nu: (0, be[jnp.minimum(i, n_blocks - 1)], 0, 0))
    return pl.pallas_call(
        functools.partial(_moe_expert_kernel, n_tokens, n_asg), name="moe_expert",
        grid_spec=pltpu.PrefetchScalarGridSpec(
            num_scalar_prefetch=3,
            grid=(n_blocks + DRAIN_STEPS,),
            in_specs=[pl.BlockSpec(memory_space=pl.ANY), weight((d, ff)), weight((d, ff)), weight((ff, d))],
            out_specs=pl.BlockSpec(memory_space=pl.ANY),
            scratch_shapes=[pltpu.VMEM((MOE_BUFFERS, EXPERT_BLOCK * TILE_ROWS, LANES), F32),
                            pltpu.VMEM((MOE_BUFFERS, EXPERT_BLOCK * TILE_ROWS, LANES), F32),
                            pltpu.VMEM((d, ff), BF16), pltpu.VMEM((d, ff), BF16), pltpu.VMEM((ff, d), BF16),
                            pltpu.SemaphoreType.DMA((MOE_BUFFERS,)), pltpu.SemaphoreType.DMA((MOE_BUFFERS,))]),
        out_shape=jax.ShapeDtypeStruct((n_rows * TILE_ROWS, LANES), F32),
        compiler_params=_params(("arbitrary",)),
    )(asg, blk_e, n_used, x1_tiles, e_gate, e_up, e_down)


def _moe_combine_kernel(alpha, lead_tiles, y0_ref, y1_ref, x1_ref, gate_ref, g_ref, b_ref, lead_ref, tail_ref):
    i = pl.program_id(0)
    gate = gate_ref[...]
    tm = x1_ref.shape[0]
    moe = gate[:, 0:1] * _rows_from_tiles(y0_ref, tm) + gate[:, 1:2] * _rows_from_tiles(y1_ref, tm)
    out = _ln(alpha * x1_ref[...] + moe, g_ref[...], b_ref[...])

    @pl.when(i < lead_tiles)
    def _():
        lead_ref[...] = out

    @pl.when(i >= lead_tiles)
    def _():
        tail_ref[...] = out[:tail_ref.shape[0]]


def moe_combine(y, x1, gate, g, b, tm, alpha, n, n_lead):
    stride, d = x1.shape
    assert n_lead % tm == 0 and stride - n_lead == tm and n_lead < n <= stride
    lead_tiles = n_lead // tm
    return pl.pallas_call(
        functools.partial(_moe_combine_kernel, alpha, lead_tiles), name="moe_combine",
        grid=(stride // tm,),
        in_specs=[pl.BlockSpec((tm * TILE_ROWS, LANES), lambda i: (i, 0)),
                  pl.BlockSpec((tm * TILE_ROWS, LANES), lambda i: (i + stride // tm, 0)),
                  pl.BlockSpec((tm, d), lambda i: (i, 0)),
                  pl.BlockSpec((tm, LANES), lambda i: (i, 0)), _const_spec((1, d)), _const_spec((1, d))],
        out_specs=[pl.BlockSpec((tm, d), lambda i: (jnp.minimum(i, lead_tiles - 1), 0)),
                   pl.BlockSpec((n - n_lead, d), lambda i: (0, 0))],
        out_shape=[jax.ShapeDtypeStruct((n_lead, d), F32), jax.ShapeDtypeStruct((n - n_lead, d), F32)],
        compiler_params=_params(("arbitrary",)),
    )(y, y, x1, gate, g.reshape(1, d), b.reshape(1, d))


def moe_routing(experts, stride):
    top_k, n = experts.shape
    n_exp = N_GROUPS * EXPERTS_PER_GROUP
    a = n * top_k
    flat_e = experts.reshape(a)
    pos_bits = max(a - 1, 1).bit_length()
    assert n_exp << pos_bits < 2 ** 31
    order = jnp.sort((flat_e << pos_bits) | jnp.arange(a, dtype=jnp.int32)) & ((1 << pos_bits) - 1)
    order = order + (order // n) * (stride - n)
    counts = jnp.sum((flat_e[:, None] == jnp.arange(n_exp, dtype=jnp.int32)[None, :]).astype(jnp.int32), axis=0)
    ends = jnp.cumsum(counts)
    padded = (counts + EXPERT_BLOCK - 1) // EXPERT_BLOCK * EXPERT_BLOCK
    pad_end = jnp.cumsum(padded)
    n_blocks = -(-a // EXPERT_BLOCK) + n_exp
    blk_start = jnp.arange(n_blocks, dtype=jnp.int32) * EXPERT_BLOCK
    blk_e = jnp.minimum(jnp.sum((pad_end[None, :] <= blk_start[:, None]).astype(jnp.int32), axis=1), n_exp - 1)
    is_e = blk_e[:, None] == jnp.arange(n_exp, dtype=jnp.int32)[None, :]
    per_block = lambda table: jnp.sum(jnp.where(is_e, table[None, :], 0), axis=1, keepdims=True)
    slot = blk_start[:, None] + jnp.arange(EXPERT_BLOCK, dtype=jnp.int32)[None, :]
    rank = slot - per_block(pad_end - padded)
    spare = top_k * stride + jnp.clip(slot - per_block(ends), 0, n_exp * EXPERT_BLOCK - 1)
    asg = jnp.where(rank < per_block(counts), order[jnp.clip(per_block(ends - counts) + rank, 0, a - 1)], spare)
    n_used = (pad_end[-1:] // EXPERT_BLOCK).astype(jnp.int32)
    return asg.reshape(-1).astype(jnp.int32), blk_e.astype(jnp.int32), n_used


def hier_moe_ln(x1, x1_tiles, eidx, gate, w, tm, alpha, n_tokens, n_lead):
    stride = x1.shape[0]
    asg, blk_e, n_used = moe_routing(eidx[:2, :n_tokens], stride)
    y = moe_experts(x1_tiles, n_tokens, stride, asg, blk_e, n_used, w['e_gate'], w['e_up'], w['e_down'])
    return moe_combine(y, x1, gate, w['ln2_g'], w['ln2_b'], tm, alpha, n_tokens, n_lead)


def kernel(x_prompt, x_sample, mem_prompt, state_wkv, state_shift, cache_win_k, cache_win_v, cache_mem_k, cache_mem_v, ln_in_g, ln_in_b, w_in, mu, w0, w_up, a0, a_up, g_up, k_k, k_a, r_k, lnx_g, lnx_b, sinks, w_mem_kv, p_a, p_b, p_m, w_o, ln1_g, ln1_b, w_group, b_group, w_router, b_router, e_gate, e_up, e_down, ln2_g, ln2_b):
    depth = w_in.shape[0]
    assert depth == 1, "single-layer step"
    bsz, seq, d = x_prompt.shape
    dec = x_sample.shape[0]
    assert x_sample.shape[1] == 1
    c_shift = mu.shape[-1]
    c_a = w0.shape[-1]
    window, kv_w = cache_win_k.shape[2], cache_win_k.shape[3] * cache_win_k.shape[4]
    n_mem, mem_w = cache_mem_k.shape[2], cache_mem_k.shape[3] * cache_mem_k.shape[4]
    q_w = sinks.shape[-1] * HEAD
    qkvm_w = q_w + 2 * kv_w + mem_w
    alpha = (2.0 * depth) ** 0.25
    past_len = float(PAST_LEN)
    chunk = WKV_CHUNK

    w_in_b = w_in[0].astype(BF16)
    w_parts = [w_in_b[:, :c_shift], w_in_b[:, c_shift:c_shift + qkvm_w], w_in_b[:, c_shift + qkvm_w:]]
    rp = dict(mu=mu[0], w0=w0[0], w_up=w_up[0], a0=a0[0], a_up=a_up[0], g_up=g_up[0], k_k=k_k[0], k_a=k_a[0],
              r_k=r_k[0].reshape(-1))
    n_route = N_GROUPS * (1 + EXPERTS_PER_GROUP)
    mw = dict(ln_in_g=ln_in_g.reshape(1, d), ln_in_b=ln_in_b.reshape(1, d), p_a=p_a[0].astype(BF16),
              p_b=p_b[0].astype(BF16), p_m=p_m[0].astype(BF16), w_o=w_o[0].astype(BF16),
              ln1_g=ln1_g[0].reshape(1, d), ln1_b=ln1_b[0].reshape(1, d),
              w_route=jnp.stack(_pieces(jnp.pad(jnp.concatenate([w_group[0], w_router[0]], axis=1),
                                                ((0, 0), (0, LANES - n_route))), 2)),
              b_route=jnp.pad(jnp.concatenate([b_group[0], b_router[0]]), (0, LANES - n_route)).reshape(1, LANES),
              e_gate=e_gate, e_up=e_up, e_down=e_down, ln2_g=ln2_g[0], ln2_b=ln2_b[0])

    xp = x_prompt.reshape(bsz * seq, d)
    zq, zg, shift_p, prep = ln_proj_prep(x_prompt, ln_in_g, ln_in_b, w_parts, jnp.zeros((bsz, 1, c_shift), F32), rp,
                                         chunk, ROW_TILE, BF16)
    o_a, wkv_p = wkv(prep, jnp.zeros((bsz, c_a // HEAD, HEAD, HEAD), F32), lnx_g[0], lnx_b[0], chunk)
    mkv = proj(mem_prompt.reshape(bsz * n_mem, d), w_mem_kv[0].astype(BF16), ROW_TILE).reshape(bsz, n_mem, 2 * mem_w)
    mk_p, mv_p = mkv[..., :mem_w], mkv[..., mem_w:]
    tables = rope_tables(jnp.arange(seq, dtype=F32))
    o_b, o_m, k_rot = attn_prompt(zq, mk_p, mv_p, sinks[0], tables, window, q_w, kv_w)
    n_all = bsz * seq + dec
    n_buf = -(-n_all // ROW_TILE) * ROW_TILE
    routed = merge(xp, o_a.reshape(-1, c_a), o_b.reshape(-1, q_w), o_m.reshape(-1, mem_w),
                   zg.reshape(bsz * seq, -1), mw, ROW_TILE, alpha, n_buf)
    shift_p = shift_p[:, 0]
    kb_p = k_rot[:, -window:].reshape(bsz, window, H_KV, HEAD)
    vb_p = zq[:, -window:, q_w + kv_w:q_w + 2 * kv_w].reshape(bsz, window, H_KV, HEAD)

    xs = x_sample.reshape(dec, d)
    zq_s, zg_s, zr_s, ops_s = ln_proj_prep(xs.reshape(1, dec, d), ln_in_g, ln_in_b, w_parts,
                                           state_shift[0].reshape(1, dec, c_shift), rp, 1, dec, F32)
    zq_s, zg_s, zr_s = zq_s[0], zg_s[0], zr_s[0]
    o_a_s, wkv_s = wkv_step([a.reshape(dec, c_a) for a in ops_s], state_wkv[0], lnx_g[0], lnx_b[0], STEP_BATCH)
    tables_s = rope_tables(jnp.full((1,), past_len, F32))
    o_b_s, o_m_s, nk_s, nv_s = attn_step(
        zq_s, cache_win_k[0].reshape(dec, window, kv_w), cache_win_v[0].reshape(dec, window, kv_w),
        cache_mem_k[0].reshape(dec, n_mem, mem_w), cache_mem_v[0].reshape(dec, n_mem, mem_w),
        sinks[0], tables_s, q_w, kv_w, STEP_BATCH)
    x1, x1_tiles, eidx, gate = merge(xs, o_a_s, o_b_s, o_m_s, zg_s, mw, dec, alpha, n_buf, bsz * seq, routed)

    y_prompt, y_sample = hier_moe_ln(x1, x1_tiles, eidx, gate, mw, ROW_TILE, alpha, n_all, bsz * seq)
    y_prompt = y_prompt.reshape(bsz, seq, d)
    y_sample = y_sample.reshape(dec, 1, d)

    sd = state_wkv.dtype
    return (y_prompt, y_sample, wkv_p[None].astype(sd), wkv_s[None].astype(sd), shift_p[None], zr_s[None],
            kb_p[None], vb_p[None], nk_s.reshape(dec, window, H_KV, HEAD)[None],
            nv_s.reshape(dec, window, H_KV, HEAD)[None],
            mk_p.reshape(bsz, n_mem, -1, HEAD)[None], mv_p.reshape(bsz, n_mem, -1, HEAD)[None])
```

```python
import functools
import math

import jax
import jax.numpy as jnp
from jax import lax
from jax.experimental import pallas as pl
from jax.experimental.pallas import tpu as pltpu

F32 = jnp.float32
BF16 = jnp.bfloat16

ROW_TILE = 256
WKV_CHUNK = 64
CHUNKS_PER_STEP = 8
SCAN_BATCH = 8
ATTN_BLOCKS = 4
STEP_BATCH = 16
PROJ_PIECE = 512

HEAD = 64
LANES = 128
H_KV = 2
ROT_HALF = 8
ROPE_THETA = 500000.0
PAST_LEN = 8192
N_GROUPS = 4
EXPERTS_PER_GROUP = 8
EXPERT_BLOCK = 128
LN_EPS = 1e-5
LNX_EPS = 64e-5
NEG_INF = -1e30
VMEM_LIMIT = 48 * 1024 * 1024


def _pieces(x, n):
    out = []
    for _ in range(n):
        p = x.astype(BF16)
        out.append(p)
        x = x - p.astype(F32)
    return out


def _mask_dot(x, mask, n=2):
    w = mask.shape[0]
    if w < x.shape[1]:
        return jnp.concatenate([_mask_dot(x[:, c:c + w], mask, n) for c in range(0, x.shape[1], w)], axis=1)
    return sum(jnp.dot(p, mask, preferred_element_type=F32) for p in _pieces(x, n))


def _split3(x, axis, lhs):
    hi = x.astype(BF16).astype(F32)
    lo = x - hi
    return jnp.concatenate([hi, hi, lo] if lhs else [hi, lo, hi], axis=axis).astype(BF16)


def _dot3(a, b):
    return jnp.dot(_split3(a, 1, True), _split3(b, 0, False), preferred_element_type=F32)


def _dot3_t(a, b):
    return lax.dot_general(_split3(a, 1, True), _split3(b, 1, False), (((1,), (1,)), ((), ())),
                           preferred_element_type=F32)


def _bdot(a, b):
    return jnp.dot(a.astype(BF16), b.astype(BF16), preferred_element_type=F32)


def _bdot_t(a, b):
    return lax.dot_general(a.astype(BF16), b.astype(BF16), (((1,), (1,)), ((), ())), preferred_element_type=F32)


def _sigmoid(x):
    return 0.5 * jnp.tanh(0.5 * x) + 0.5


def _ln(x, g, b):
    mu = jnp.mean(x, axis=-1, keepdims=True)
    xc = x - mu
    var = jnp.mean(xc * xc, axis=-1, keepdims=True)
    return xc * lax.rsqrt(var + LN_EPS) * g + b


def _const_spec(shape):
    nd = len(shape)
    return pl.BlockSpec(shape, lambda *_: (0,) * nd)


def _params(sem):
    return pltpu.CompilerParams(dimension_semantics=sem, vmem_limit_bytes=VMEM_LIMIT)


def _ln_proj_prep_kernel(chunk, x_ref, g_ref, b_ref, wr_ref, wq_ref, wg_ref, prev_ref, mu_ref, w0_ref, wup_ref,
                         a0_ref, aup_ref, gup_ref, kk_ref, ka_ref, rk_ref, hsum_ref, tril_ref,
                         zq_ref, zg_ref, zr_ref, *refs):
    out_refs, carry_ref = refs[:-1], refs[-1]
    xn = _ln(x_ref[0], g_ref[...], b_ref[...]).astype(BF16)
    z = jnp.dot(xn, wr_ref[...], preferred_element_type=F32)
    pending = [(o_ref, w_ref, c) for o_ref, w_ref in ((zq_ref, wq_ref), (zg_ref, wg_ref))
               for c in range(0, w_ref.shape[1], PROJ_PIECE)]

    def project(n_pieces):
        for _ in range(min(n_pieces, len(pending))):
            o_ref, w_ref, c = pending.pop(0)
            o_ref[0, :, c:c + PROJ_PIECE] = jnp.dot(xn, w_ref[:, c:c + PROJ_PIECE],
                                                    preferred_element_type=F32).astype(o_ref.dtype)

    tt = z.shape[0]
    c_a = w0_ref.shape[-1]
    r_w, r_a, r_g = wup_ref.shape[0], aup_ref.shape[0], gup_ref.shape[0]
    if chunk == 1:
        zr_ref[0] = z
        prev = prev_ref[0]
    else:
        zr_ref[0] = z[tt - 1:tt, :]

        @pl.when(pl.program_id(1) == 0)
        def _():
            carry_ref[...] = prev_ref[0]

        row = lax.broadcasted_iota(jnp.int32, z.shape, 0)
        prev = jnp.where(row == 0, carry_ref[...], pltpu.roll(z, 1, 0))
        carry_ref[...] = z[tt - 1:tt, :]
    zs = z + (prev - z) * mu_ref[...]
    r = zs[:, :c_a]
    k = zs[:, c_a:2 * c_a]
    v = zs[:, 2 * c_a:3 * c_a]
    o = 3 * c_a
    xw = zs[:, o:o + r_w]
    xa = zs[:, o + r_w:o + r_w + r_a]
    xg = zs[:, o + r_w + r_a:o + r_w + r_a + r_g]
    project(1)
    warg = -(w0_ref[...] + _dot3(jnp.tanh(xw), wup_ref[...]))
    softplus = jnp.maximum(warg, 0.0) + jnp.log1p(jnp.exp(-jnp.abs(warg)))
    lw = -jnp.exp(-softplus - 0.5)
    project(1)
    a = _sigmoid(a0_ref[...] + _dot3(xa, aup_ref[...]))
    g = _dot3(_sigmoid(xg), gup_ref[...])
    project(1)
    kkr = k * kk_ref[...]
    kk = kkr / jnp.maximum(jnp.sqrt(_mask_dot(kkr * kkr, hsum_ref[...])), 1e-12)
    project(1)
    k2 = k * (1.0 + (a - 1.0) * ka_ref[...])
    bonus = _mask_dot(r * k2 * rk_ref[...], hsum_ref[...]) * v
    kb = kk * a
    project(1)
    if chunk == 1:
        outs = (r, jnp.exp(lw), k2, v, -kk, kb, g, bonus)
        for ref, val in zip(out_refs, outs):
            ref[0] = val
        project(len(pending))
        return
    pieces = _pieces(lw, 3)
    ones3 = jnp.ones((chunk, 3 * chunk), BF16)
    cw, cwl = [], []
    for c in range(tt // chunk):
        stack = jnp.concatenate([p[c * chunk:(c + 1) * chunk] for p in pieces], axis=0)
        cw.append(jnp.dot(tril_ref[...], stack, preferred_element_type=F32))
        cwl.append(jnp.dot(ones3, stack, preferred_element_type=F32))
    cw = jnp.concatenate(cw, axis=0)
    cwl = jnp.concatenate(cwl, axis=0)
    project(1)
    e_inv = jnp.exp(-cw)
    outs = (r * jnp.exp(cw), -kk * jnp.exp(cw - lw), kb * e_inv, k2 * e_inv, v, g, bonus)
    for ref, val in zip(out_refs[:-1], outs):
        ref[0] = val
        project(1)
    wl = jnp.exp(cwl)
    for c in range(tt // chunk):
        out_refs[-1][0, c] = wl[c * chunk:c * chunk + 1, :]
    project(len(pending))


def _head_sum_matrix(width):
    assert width % LANES == 0 and LANES % HEAD == 0
    idx = jnp.arange(LANES)
    return ((idx[:, None] // HEAD) == (idx[None, :] // HEAD)).astype(BF16)


def ln_proj_prep(x, g, b, ws, prev, p, chunk, tt, gate_dtype):
    bsz, t, d = x.shape
    w_r, w_q, w_g = ws
    cs = w_r.shape[1]
    c_a = p['w0'].shape[-1]
    assert prev.shape[1] == (t if chunk == 1 else 1)
    ridx = jnp.arange(chunk)
    tril = jnp.tile((ridx[None, :] <= ridx[:, None]).astype(BF16), (1, 3))
    hsum = _head_sum_matrix(c_a)
    row = lambda a: a.reshape(1, -1)
    rows = lambda width: pl.BlockSpec((1, tt, width), lambda bi, i: (bi, i, 0))
    full = jax.ShapeDtypeStruct((bsz, t, c_a), F32)
    if chunk == 1:
        out_specs, out_shape = [rows(c_a)] * 8, [full] * 8
        prev_spec, zr_spec, zr_rows = rows(cs), rows(cs), t
    else:
        out_specs = [rows(c_a)] * 7 + [pl.BlockSpec((1, tt // chunk, 1, c_a), lambda bi, i: (bi, i, 0, 0))]
        out_shape = [full] * 7 + [jax.ShapeDtypeStruct((bsz, t // chunk, 1, c_a), F32)]
        prev_spec = zr_spec = pl.BlockSpec((1, 1, cs), lambda bi, i: (bi, 0, 0))
        zr_rows = 1
    consts = [row(p['mu']), row(p['w0']), p['w_up'], row(p['a0']), p['a_up'], p['g_up'], row(p['k_k']),
              row(p['k_a']), row(p['r_k']), hsum, tril]
    weight = lambda w: pl.BlockSpec(w.shape, lambda bi, i: (0, 0), pipeline_mode=pl.Buffered(1))
    outs = pl.pallas_call(
        functools.partial(_ln_proj_prep_kernel, chunk), name="ln_proj_prep",
        grid=(bsz, t // tt),
        in_specs=[rows(d), _const_spec((1, d)), _const_spec((1, d)), weight(w_r), weight(w_q), weight(w_g), prev_spec]
        + [_const_spec(c.shape) for c in consts],
        out_specs=[rows(w_q.shape[1]), rows(w_g.shape[1]), zr_spec] + out_specs,
        out_shape=[jax.ShapeDtypeStruct((bsz, t, w_q.shape[1]), F32),
                   jax.ShapeDtypeStruct((bsz, t, w_g.shape[1]), gate_dtype),
                   jax.ShapeDtypeStruct((bsz, zr_rows, cs), F32)] + out_shape,
        scratch_shapes=[pltpu.VMEM((1, cs), F32)],
        compiler_params=_params(("parallel", "arbitrary")),
    )(x, row(g), row(b), w_r, w_q, w_g, prev, *consts)
    return outs[0], outs[1], outs[2], outs[3:]


def _wkv_chunk_kernel(ra_ref, at_ref, bt_ref, kt_ref, v_ref, wl_ref, rp_ref, y0_ref, m_ref, n_ref):
    n_chunks = m_ref.shape[1]
    length = ra_ref.shape[1] // n_chunks
    heads = ra_ref.shape[2] // HEAD
    row = lax.broadcasted_iota(jnp.int32, (length, length), 0)
    col = lax.broadcasted_iota(jnp.int32, (length, length), 1)
    strict = row > col
    incl = row >= col
    hrow = lax.broadcasted_iota(jnp.int32, (HEAD, HEAD), 0)
    hcol = lax.broadcasted_iota(jnp.int32, (HEAD, HEAD), 1)
    units = [(c, h) for c in range(n_chunks) for h in range(heads)]
    us = range(len(units))
    at = lambda ref, u: ref[0, units[u][0] * length:(units[u][0] + 1) * length,
                            units[u][1] * HEAD:(units[u][1] + 1) * HEAD]
    gram = [_bdot_t(jnp.concatenate([at(at_ref, u), at(ra_ref, u)], axis=0),
                    jnp.concatenate([at(bt_ref, u), at(kt_ref, u)], axis=0)) for u in us]
    a_ab = [jnp.where(strict, g[:length, :length], 0.0) for g in gram]
    a_kk = [jnp.concatenate([jnp.where(strict, g[:length, length:], 0.0),
                             jnp.where(incl, g[length:, length:], 0.0)], axis=0) for g in gram]
    a_rb = [jnp.where(incl, g[length:, :length], 0.0) for g in gram]
    inv = [jnp.where(row == col, 1.0, a) for a in a_ab]
    pw = [_bdot(a, a) for a in a_ab]
    avy = [_bdot(a_kk[u], at(v_ref, u)) for u in us]
    w_l = [wl_ref[0, units[u][0], :, units[u][1] * HEAD:(units[u][1] + 1) * HEAD] for u in us]
    nk = [_bdot(at(v_ref, u).T, at(kt_ref, u) * w_l[u]) for u in us]
    for _ in range(int(math.log2(length)) - 2):
        both = [_bdot(jnp.concatenate([pw[u], inv[u]], axis=0), pw[u]) for u in us]
        pw = [b[:length] for b in both]
        inv = [inv[u] + both[u][length:] for u in us]
    inv = [inv[u] + _bdot(inv[u], pw[u]) for u in us]
    pq = [_bdot(inv[u], jnp.concatenate([at(at_ref, u), avy[u][:length]], axis=1)) for u in us]
    ry = [_bdot(a_rb[u], pq[u]) for u in us]
    mn = [_bdot(pq[u].T, at(bt_ref, u) * w_l[u]) for u in us]
    lanes = lambda f, c: jnp.concatenate([f(c * heads + h) for h in range(heads)], axis=1)
    rows = lambda f: jnp.concatenate([lanes(f, c) for c in range(n_chunks)], axis=0)
    rp_ref[0] = rows(lambda u: at(ra_ref, u) + ry[u][:, :HEAD])
    y0_ref[0] = rows(lambda u: avy[u][length:] + ry[u][:, HEAD:])
    for c in range(n_chunks):
        m_ref[0, c] = lanes(lambda u: jnp.where(hrow == hcol, w_l[u], 0.0) + mn[u][:HEAD], c)
        n_ref[0, c] = lanes(lambda u: nk[u] + mn[u][HEAD:], c)


def _head_norm_wide(y, hsum):
    yc = y - _mask_dot(y, hsum) * (1.0 / HEAD)
    return yc * lax.rsqrt(_mask_dot(yc * yc, hsum) * (1.0 / HEAD) + LNX_EPS)


def _wkv_scan_kernel(rp_ref, y0_ref, m_ref, n_ref, g_ref, bonus_ref, s0_ref, lg_ref, lb_ref, hsum_ref,
                     o_ref, sout_ref, s_ref):
    c = pl.program_id(1)
    nb = rp_ref.shape[0]
    heads = rp_ref.shape[2] // HEAD

    @pl.when(c == 0)
    def _():
        s_ref[...] = s0_ref[...]

    pairs = [(b, h, slice(h * HEAD, (h + 1) * HEAD)) for b in range(nb) for h in range(heads)]
    s = [s_ref[b, h] for b, h, _ in pairs]
    s_new = [n_ref[b, 0, :, sl] + _dot3(s[j], m_ref[b, 0, :, sl]) for j, (b, h, sl) in enumerate(pairs)]
    ys = [y0_ref[b, :, sl] + _bdot_t(rp_ref[b, :, sl], s[j]) for j, (b, h, sl) in enumerate(pairs)]
    for j, (b, h, _) in enumerate(pairs):
        s_ref[b, h] = s_new[j]
    length = rp_ref.shape[1]
    y = _head_norm_wide(jnp.concatenate([jnp.concatenate(ys[b * heads:(b + 1) * heads], axis=-1) for b in range(nb)],
                                        axis=0), hsum_ref[...])
    for b in range(nb):
        o = (y[b * length:(b + 1) * length] * lg_ref[...] + lb_ref[...] + bonus_ref[b]) * g_ref[b]
        o_ref[b] = o.astype(o_ref.dtype)

    @pl.when(c == pl.num_programs(1) - 1)
    def _():
        sout_ref[...] = s_ref[...]


def wkv(prep, s0, lnx_g, lnx_b, chunk):
    ra, at, bt, kt, v, g, bonus, wl = prep
    bsz, t, c_a = ra.shape
    heads = c_a // HEAD
    n_chunks = t // chunk
    assert chunk & (chunk - 1) == 0 and chunk >= 4 and t % chunk == 0
    per_step = CHUNKS_PER_STEP if n_chunks % CHUNKS_PER_STEP == 0 else 1
    tile = pl.BlockSpec((1, per_step * chunk, c_a), lambda b, c: (b, c, 0))
    mat = pl.BlockSpec((1, per_step, HEAD, c_a), lambda b, c: (b, c, 0, 0))
    full = jax.ShapeDtypeStruct((bsz, t, c_a), F32)
    mats = jax.ShapeDtypeStruct((bsz, n_chunks, HEAD, c_a), F32)
    rp, y0, m, n = pl.pallas_call(
        _wkv_chunk_kernel, name="wkv_chunk",
        grid=(bsz, n_chunks // per_step),
        in_specs=[tile] * 5 + [pl.BlockSpec((1, per_step, 1, c_a), lambda b, c: (b, c, 0, 0))],
        out_specs=[tile, tile, mat, mat],
        out_shape=[full, full, mats, mats],
        compiler_params=_params(("parallel", "parallel")),
    )(ra, at, bt, kt, v, wl)
    nb = SCAN_BATCH if bsz % SCAN_BATCH == 0 else 1
    tile = pl.BlockSpec((nb, chunk, c_a), lambda b, c: (b, c, 0))
    mat = pl.BlockSpec((nb, 1, HEAD, c_a), lambda b, c: (b, c, 0, 0))
    st = pl.BlockSpec((nb, heads, HEAD, HEAD), lambda b, c: (b, 0, 0, 0))
    return pl.pallas_call(
        _wkv_scan_kernel, name="wkv_scan",
        grid=(bsz // nb, n_chunks),
        in_specs=[tile, tile, mat, mat, tile, tile, st, _const_spec((1, c_a)), _const_spec((1, c_a)),
                  _const_spec((LANES, LANES))],
        out_specs=[tile, st],
        out_shape=[jax.ShapeDtypeStruct((bsz, t, c_a), BF16), jax.ShapeDtypeStruct(s0.shape, F32)],
        scratch_shapes=[pltpu.VMEM((nb, heads, HEAD, HEAD), F32)],
        compiler_params=_params(("parallel", "arbitrary")),
    )(rp, y0, m, n, g, bonus, s0, lnx_g.reshape(1, c_a), lnx_b.reshape(1, c_a), _head_sum_matrix(c_a))


def _wkv_step_kernel(r_ref, w_ref, k_ref, v_ref, a_ref, b_ref, g_ref, bonus_ref, s_ref, lg_ref, lb_ref, hsum_ref,
                     o_ref, sout_ref):
    bt, heads = s_ref.shape[0], s_ref.shape[1]
    c_a = heads * HEAD
    hsum = hsum_ref[...]
    diag = (lax.broadcasted_iota(jnp.int32, (HEAD, c_a), 1) % HEAD
            == lax.broadcasted_iota(jnp.int32, (HEAD, c_a), 0))
    seqs = range(bt)
    row = lambda ref, i: ref[i:i + 1, :]
    stack = lambda f: jnp.concatenate([f(i) for i in seqs], axis=0)
    piece = lambda x, i: x[i * HEAD:(i + 1) * HEAD]
    s = stack(lambda i: jnp.concatenate([s_ref[i, h] for h in range(heads)], axis=1))
    sa = _mask_dot(s * stack(lambda i: jnp.broadcast_to(row(a_ref, i), (HEAD, c_a))), hsum)
    v_rows = _mask_dot(stack(lambda i: jnp.where(diag, row(v_ref, i), 0.0)), hsum)
    s = stack(lambda i: piece(s, i) * row(w_ref, i) + piece(sa, i) * row(b_ref, i) + piece(v_rows, i) * row(k_ref, i))
    for i in seqs:
        for h in range(heads):
            sout_ref[i, h] = piece(s, i)[:, h * HEAD:(h + 1) * HEAD]
    y_rows = _mask_dot(stack(lambda i: piece(s, i) * row(r_ref, i)), hsum)
    y = stack(lambda i: jnp.sum(jnp.where(diag, piece(y_rows, i), 0.0), axis=0, keepdims=True))
    y = _head_norm_wide(y, hsum)
    o_ref[...] = (y * lg_ref[...] + lb_ref[...] + bonus_ref[...]) * g_ref[...]


def wkv_step(ops, s0, lnx_g, lnx_b, bt):
    bsz, c_a = ops[0].shape
    rows = pl.BlockSpec((bt, c_a), lambda i: (i, 0))
    st = pl.BlockSpec((bt,) + s0.shape[1:], lambda i: (i, 0, 0, 0))
    hsum = _head_sum_matrix(c_a)
    return pl.pallas_call(
        _wkv_step_kernel, name="wkv_step",
        grid=(bsz // bt,),
        in_specs=[rows] * 8 + [st, _const_spec((1, c_a)), _const_spec((1, c_a)), _const_spec(hsum.shape)],
        out_specs=[rows, st],
        out_shape=[jax.ShapeDtypeStruct((bsz, c_a), F32), jax.ShapeDtypeStruct(s0.shape, F32)],
        compiler_params=_params(("parallel",)),
    )(*ops, s0, lnx_g.reshape(1, c_a), lnx_b.reshape(1, c_a), hsum)


def rope_tables(pos):
    inv_freq = ROPE_THETA ** (-jnp.arange(ROT_HALF, dtype=F32) / ROT_HALF)
    ang = pos[:, None] * inv_freq[None, :]
    cos, sin = jnp.cos(ang), jnp.sin(ang)
    t = pos.shape[0]
    rest = HEAD - 2 * ROT_HALF
    c = jnp.concatenate([cos, cos, jnp.ones((t, rest), F32)], axis=1)
    s1 = jnp.concatenate([jnp.zeros((t, ROT_HALF), F32), sin, jnp.zeros((t, rest), F32)], axis=1)
    s2 = jnp.concatenate([-sin, jnp.zeros((t, HEAD - ROT_HALF), F32)], axis=1)
    rep = LANES // HEAD
    return tuple(jnp.tile(x, (1, rep)) for x in (c, s1, s2))


def _rope(x, c, s1, s2):
    width = x.shape[-1]
    rep = width // c.shape[-1]
    if rep > 1:
        c, s1, s2 = (jnp.concatenate([t] * rep, axis=1) for t in (c, s1, s2))
    return x * c + pltpu.roll(x, ROT_HALF, 1) * s1 + pltpu.roll(x, width - ROT_HALF, 1) * s2


def _attn_kernel(window, q_w, kv_w, zq_ref, kp_ref, vp_ref, rc_ref, rs1_ref, rs2_ref, pc_ref, ps1_ref, ps2_ref,
                 mk_ref, mv_ref, sink_ref, ob_ref, om_ref, kr_ref):
    zq = zq_ref[0]
    nblk = zq.shape[0] // window
    first = pl.program_id(1) * nblk
    scale = HEAD ** -0.5
    q = zq[:, :q_w]
    k = zq[:, q_w:q_w + kv_w]
    v = zq[:, q_w + kv_w:q_w + 2 * kv_w]
    qm = zq[:, q_w + 2 * kv_w:]
    qr = _rope(q, rc_ref[...], rs1_ref[...], rs2_ref[...]) * scale
    kr = _rope(k, rc_ref[...], rs1_ref[...], rs2_ref[...])
    kr_ref[0] = kr
    k_all = jnp.concatenate([_rope(kp_ref[0], pc_ref[...], ps1_ref[...], ps2_ref[...]), kr], axis=0)
    v_all = jnp.concatenate([vp_ref[0], v], axis=0)
    gqa = q_w // kv_w
    assert window & (window - 1) == 0
    qi = lax.broadcasted_iota(jnp.int32, (gqa * window, 2 * window), 0) & (window - 1)
    kj = lax.broadcasted_iota(jnp.int32, (gqa * window, 2 * window), 1)
    band = (kj > qi) & (kj <= qi + window)
    tdot = lambda a, b: lax.dot_general(a, b, (((1,), (1,)), ((), ())), preferred_element_type=F32)
    dot = lambda a, b: jnp.dot(a, b, preferred_element_type=F32)
    hsl = lambda h: slice(h * HEAD, (h + 1) * HEAD)
    rows = lambda i: slice(i * window, (i + 1) * window)
    keys = lambda i: slice(i * window, (i + 2) * window)
    wins = [(i, hk) for i in range(nblk) for hk in range(kv_w // HEAD)]
    mems = [(i, h) for i in range(nblk) for h in range(qm.shape[1] // HEAD)]
    qmb = (qm * scale).astype(BF16)
    mk = mk_ref[0].astype(BF16)
    mv = mv_ref[0].astype(BF16)
    kcat = [k_all[keys(i), hsl(hk)].astype(BF16) for i, hk in wins]
    vcat = [v_all[keys(i), hsl(hk)].astype(BF16) for i, hk in wins]
    qs = [jnp.concatenate([qr[rows(i), hsl(hk * gqa + g)] for g in range(gqa)], axis=0).astype(BF16)
          for i, hk in wins]
    s = [jnp.where(band & ((kj >= window) | (first + i > 0)), tdot(qs[u], kcat[u]), NEG_INF)
         for u, (i, hk) in enumerate(wins)]
    sm = [tdot(qmb[rows(i), hsl(h)], mk[:, hsl(h)]) for i, h in mems]
    sink = [jnp.concatenate([jnp.full((window, 1), sink_ref[hk * gqa + g], F32) for g in range(gqa)], axis=0)
            for i, hk in wins]
    m = [jnp.maximum(jnp.max(s[u], axis=-1, keepdims=True), sink[u]) for u in range(len(wins))]
    p = [jnp.exp(s[u] - m[u]) for u in range(len(wins))]
    pm = [jnp.exp(x - jnp.max(x, axis=-1, keepdims=True)) for x in sm]
    o = [dot(p[u].astype(BF16), vcat[u])
         / (jnp.sum(p[u], axis=-1, keepdims=True) + jnp.exp(sink[u] - m[u])) for u in range(len(wins))]
    om = [dot(pm[u].astype(BF16), mv[:, hsl(h)]) / jnp.sum(pm[u], axis=-1, keepdims=True)
          for u, (i, h) in enumerate(mems)]
    n_kv, n_mh = kv_w // HEAD, qm.shape[1] // HEAD
    ob = jnp.concatenate([jnp.concatenate([o[i * n_kv + hk][g * window:(g + 1) * window]
                                           for hk in range(n_kv) for g in range(gqa)], axis=1)
                          for i in range(nblk)], axis=0)
    ob_ref[0] = ob.astype(ob_ref.dtype)
    om_ref[0] = jnp.concatenate([jnp.concatenate(om[i * n_mh:(i + 1) * n_mh], axis=1) for i in range(nblk)],
                                axis=0).astype(om_ref.dtype)


def attn_prompt(zq, mk, mv, sinks, tables, window, q_w, kv_w):
    bsz, t, zw = zq.shape
    mem_w = zw - q_w - 2 * kv_w
    assert kv_w == LANES and q_w % kv_w == 0
    kcol, vcol = q_w // kv_w, q_w // kv_w + 1
    nblk = ATTN_BLOCKS if (t // window) % ATTN_BLOCKS == 0 else 1
    rows = nblk * window
    prev = lambda n: jnp.maximum(n * nblk - 1, 0)
    tab = pl.BlockSpec((rows, LANES), lambda b, n: (n, 0))
    ptab = pl.BlockSpec((window, LANES), lambda b, n: (prev(n), 0))
    mem = pl.BlockSpec((1,) + mk.shape[1:], lambda b, n: (b, 0, 0))
    return pl.pallas_call(
        functools.partial(_attn_kernel, window, q_w, kv_w), name="attn",
        grid=(bsz, t // rows),
        in_specs=[pl.BlockSpec((1, rows, zw), lambda b, n: (b, n, 0)),
                  pl.BlockSpec((1, window, kv_w), lambda b, n: (b, prev(n), kcol)),
                  pl.BlockSpec((1, window, kv_w), lambda b, n: (b, prev(n), vcol)),
                  tab, tab, tab, ptab, ptab, ptab, mem, mem,
                  pl.BlockSpec(memory_space=pltpu.SMEM)],
        out_specs=[pl.BlockSpec((1, rows, q_w), lambda b, n: (b, n, 0)),
                   pl.BlockSpec((1, rows, mem_w), lambda b, n: (b, n, 0)),
                   pl.BlockSpec((1, rows, kv_w), lambda b, n: (b, n, 0))],
        out_shape=[jax.ShapeDtypeStruct((bsz, t, q_w), BF16), jax.ShapeDtypeStruct((bsz, t, mem_w), BF16),
                   jax.ShapeDtypeStruct((bsz, t, kv_w), F32)],
        compiler_params=_params(("parallel", "parallel")),
    )(zq, zq, zq, *tables, *tables, mk, mv, sinks)


def _attn_step_kernel(q_w, kv_w, zq_ref, ck_ref, cv_ref, mk_ref, mv_ref, rc_ref, rs1_ref, rs2_ref, sink_ref,
                      ob_ref, om_ref, nk_ref, nv_ref):
    bt = zq_ref.shape[0]
    window = ck_ref.shape[1]
    mem_w = om_ref.shape[1]
    n_q, gqa, per_vreg = q_w // HEAD, q_w // kv_w, LANES // HEAD
    scale = HEAD ** -0.5
    zq = zq_ref[...]
    q = _rope(zq[:, :q_w], rc_ref[...], rs1_ref[...], rs2_ref[...]) * scale
    k_new = _rope(zq[:, q_w:q_w + kv_w], rc_ref[...], rs1_ref[...], rs2_ref[...])
    v_new = zq[:, q_w + kv_w:q_w + 2 * kv_w]
    qm = zq[:, q_w + 2 * kv_w:] * scale
    own = lambda w: (lax.broadcasted_iota(jnp.int32, (n_q, w), 1) // HEAD
                     == lax.broadcasted_iota(jnp.int32, (n_q, w), 0))
    own_q, own_m = own(q_w), own(mem_w)
    hrow = lax.broadcasted_iota(jnp.int32, (n_q, LANES), 0)
    hblk = lax.broadcasted_iota(jnp.int32, (n_q, LANES), 1) // HEAD
    swap = (hrow % per_vreg) != (hrow // gqa)
    keep = hblk == hrow % per_vreg
    key_ok = lax.broadcasted_iota(jnp.int32, (n_q, window), 1) >= 1
    wrow = lax.broadcasted_iota(jnp.int32, (window, kv_w), 0)
    sink = sink_ref[...]
    bs = range(bt)
    tdot = lambda a, b: lax.dot_general(a.astype(BF16), b.astype(BF16), (((1,), (1,)), ((), ())),
                                        preferred_element_type=F32)
    for b in bs:
        nk_ref[b] = jnp.where(wrow == window - 1, k_new[b:b + 1], pltpu.roll(ck_ref[b], window - 1, 0))
        nv_ref[b] = jnp.where(wrow == window - 1, v_new[b:b + 1], pltpu.roll(cv_ref[b], window - 1, 0))
    q8 = []
    for b in bs:
        rep = jnp.where(own_q, q[b:b + 1], 0.0)
        fold = sum(rep[:, c * LANES:(c + 1) * LANES] for c in range(q_w // LANES))
        q8.append(jnp.where(swap, pltpu.roll(fold, HEAD, 1), fold))
    qm8 = [jnp.where(own_m, qm[b:b + 1], 0.0) for b in bs]
    s = [jnp.where(key_ok, tdot(q8[b], ck_ref[b]), NEG_INF) for b in bs]
    sm = [tdot(qm8[b], mk_ref[b]) for b in bs]
    s_new = [jnp.sum(q8[b] * k_new[b:b + 1], axis=1, keepdims=True) for b in bs]
    m = [jnp.maximum(jnp.maximum(jnp.max(s[b], axis=1, keepdims=True), s_new[b]), sink) for b in bs]
    p = [jnp.exp(s[b] - m[b]) for b in bs]
    p_new = [jnp.exp(s_new[b] - m[b]) for b in bs]
    pm = [jnp.exp(sm[b] - jnp.max(sm[b], axis=1, keepdims=True)) for b in bs]
    o8 = [(_bdot(p[b], cv_ref[b]) + p_new[b] * v_new[b:b + 1])
          / (jnp.sum(p[b], axis=1, keepdims=True) + p_new[b] + jnp.exp(sink - m[b])) for b in bs]
    om8 = [_bdot(pm[b], mv_ref[b]) / jnp.sum(pm[b], axis=1, keepdims=True) for b in bs]
    ob, om = [], []
    for b in bs:
        o = jnp.where(swap, pltpu.roll(o8[b], HEAD, 1), o8[b])
        o = jnp.concatenate([jnp.where(keep, o, 0.0)] * (q_w // LANES), axis=1)
        ob.append(jnp.sum(jnp.where(own_q, o, 0.0), axis=0, keepdims=True))
        om.append(jnp.sum(jnp.where(own_m, om8[b], 0.0), axis=0, keepdims=True))
    ob_ref[...] = jnp.concatenate(ob, axis=0)
    om_ref[...] = jnp.concatenate(om, axis=0)


def attn_step(zq, ck, cv, mk, mv, sinks, tables, q_w, kv_w, bt):
    bsz, zw = zq.shape
    mem_w = zw - q_w - 2 * kv_w
    window, n_mem = ck.shape[1], mk.shape[1]
    n_q = q_w // HEAD
    assert kv_w == LANES and mem_w // HEAD <= n_q and n_q == 8
    rows = lambda w: pl.BlockSpec((bt, w), lambda i: (i, 0))
    cache = pl.BlockSpec((bt, window, kv_w), lambda i: (i, 0, 0))
    mem = pl.BlockSpec((bt, n_mem, mem_w), lambda i: (i, 0, 0))
    consts = list(tables) + [sinks.reshape(n_q, 1)]
    const_specs = [_const_spec(c.shape) for c in consts]
    return pl.pallas_call(
        functools.partial(_attn_step_kernel, q_w, kv_w), name="attn_step",
        grid=(bsz // bt,),
        in_specs=[rows(zw), cache, cache, mem, mem] + const_specs,
        out_specs=[rows(q_w), rows(mem_w), cache, cache],
        out_shape=[jax.ShapeDtypeStruct((bsz, q_w), F32), jax.ShapeDtypeStruct((bsz, mem_w), F32),
                   jax.ShapeDtypeStruct(ck.shape, F32), jax.ShapeDtypeStruct(cv.shape, F32)],
        compiler_params=_params(("parallel",)),
    )(zq, ck, cv, mk, mv, *consts)


def _proj_kernel(x_ref, w_ref, o_ref):
    o_ref[...] = jnp.dot(x_ref[...].astype(BF16), w_ref[...], preferred_element_type=F32)


def proj(x, w, tm):
    n, d = x.shape
    return pl.pallas_call(
        _proj_kernel, name="proj",
        grid=(n // tm,),
        in_specs=[pl.BlockSpec((tm, d), lambda i: (i, 0)), _const_spec(w.shape)],
        out_specs=pl.BlockSpec((tm, w.shape[1]), lambda i: (i, 0)),
        out_shape=jax.ShapeDtypeStruct((n, w.shape[1]), F32),
        compiler_params=_params(("parallel",)),
    )(x, w)


def _merge_kernel(alpha, n_tiles, x_ref, oa_ref, ob_ref, om_ref, gt_ref, lig_ref, lib_ref, pa_ref, pb_ref, pm_ref,
                  wo_ref, l1g_ref, l1b_ref, wr_ref, br_ref, *refs):
    outs = refs[-4:]

    @pl.when(pl.program_id(0) >= n_tiles)
    def _():
        for ref in outs:
            ref[...] = jnp.zeros_like(ref)

    @pl.when(pl.program_id(0) < n_tiles)
    def _():
        _merge_tile(alpha, x_ref, oa_ref, ob_ref, om_ref, gt_ref, lig_ref, lib_ref, pa_ref, pb_ref, pm_ref, wo_ref,
                    l1g_ref, l1b_ref, wr_ref, br_ref, *outs)


def _merge_tile(alpha, x_ref, oa_ref, ob_ref, om_ref, gt_ref, lig_ref, lib_ref, pa_ref, pb_ref, pm_ref, wo_ref,
                l1g_ref, l1b_ref, wr_ref, br_ref, x1_ref, x1t_ref, eidx_ref, gate_ref):
    d = x_ref.shape[1]
    xn = _ln(x_ref[...], lig_ref[...], lib_ref[...])
    gts = _sigmoid(gt_ref[...].astype(F32))
    merged = (gts[:, :d] * _bdot(oa_ref[...], pa_ref[...]) + gts[:, d:2 * d] * _bdot(ob_ref[...], pb_ref[...])
              + gts[:, 2 * d:] * _bdot(om_ref[...], pm_ref[...]))
    x1 = _ln(alpha * xn + _bdot(merged, wo_ref[...]), l1g_ref[...], l1b_ref[...])
    x1_ref[...] = x1
    _rows_to_tiles(x1t_ref, x1)
    x_hi, x_lo = _pieces(x1, 2)
    w_hi, w_lo = wr_ref[0], wr_ref[1]
    dot = lambda a, b: jnp.dot(a, b, preferred_element_type=F32)
    logits = dot(x_hi, w_hi) + dot(x_hi, w_lo) + dot(x_lo, w_hi) + br_ref[...]
    lane = lax.broadcasted_iota(jnp.int32, logits.shape, 1)
    lane_f = lane.astype(F32)
    first = lambda hit: jnp.min(jnp.where(hit, lane_f, float(LANES)), axis=-1, keepdims=True).astype(jnp.int32)
    gmask = lane < N_GROUPS
    gl = jnp.where(gmask, logits, NEG_INF)
    gmax = jnp.max(gl, axis=-1, keepdims=True)
    gidx = first(gl == gmax)
    g_w = 1.0 / jnp.sum(jnp.where(gmask, jnp.exp(gl - gmax), 0.0), axis=-1, keepdims=True)
    lo = N_GROUPS + gidx * EXPERTS_PER_GROUP
    el = jnp.where((lane >= lo) & (lane < lo + EXPERTS_PER_GROUP), logits, NEG_INF)
    v1 = jnp.max(el, axis=-1, keepdims=True)
    i1 = first(el == v1)
    el2 = jnp.where(lane == i1, NEG_INF, el)
    v2 = jnp.max(el2, axis=-1, keepdims=True)
    i2 = first(el2 == v2)
    e2 = jnp.exp(v2 - v1)
    gate1 = g_w / (1.0 + e2)
    eidx = jnp.where(lane == 0, (i1 - N_GROUPS).astype(F32), jnp.where(lane == 1, (i2 - N_GROUPS).astype(F32), 0.0))
    eidx_ref[...] = eidx.T[:eidx_ref.shape[0]].astype(jnp.int32)
    gate_ref[...] = jnp.where(lane == 0, gate1, jnp.where(lane == 1, gate1 * e2, 0.0))


def merge(x, oa, ob, om, gt, w, tm, alpha, n_total, row_offset=0, into=None):
    n, d = x.shape
    assert row_offset % tm == 0 and n % tm == 0 and n_total % tm == 0
    off = row_offset // tm
    into = list(into or [])
    n_tiles = n // tm
    steps = n_tiles if into else n_total // tm
    rows = lambda a: pl.BlockSpec((tm, a.shape[1]), lambda i: (jnp.minimum(i, n_tiles - 1), 0))
    out = lambda width: pl.BlockSpec((tm, width), lambda i: (i + off, 0))
    consts = [w['ln_in_g'], w['ln_in_b'], w['p_a'], w['p_b'], w['p_m'], w['w_o'], w['ln1_g'], w['ln1_b'],
              w['w_route'], w['b_route']]
    n_in = 5 + len(consts)
    return pl.pallas_call(
        functools.partial(_merge_kernel, alpha, n_tiles), name="merge",
        grid=(steps,),
        in_specs=[rows(a) for a in (x, oa, ob, om, gt)] + [_const_spec(c.shape) for c in consts]
        + [pl.BlockSpec(memory_space=pl.ANY)] * len(into),
        out_specs=[out(d), pl.BlockSpec((tm * TILE_ROWS, LANES), lambda i: (i + off, 0)),
                   pl.BlockSpec((TILE_ROWS, tm), lambda i: (0, i + off)), out(LANES)],
        out_shape=[jax.ShapeDtypeStruct((n_total, d), F32), jax.ShapeDtypeStruct((n_total * TILE_ROWS, LANES), F32),
                   jax.ShapeDtypeStruct((TILE_ROWS, n_total), jnp.int32), jax.ShapeDtypeStruct((n_total, LANES), F32)],
        input_output_aliases={n_in + k: k for k in range(len(into))},
        compiler_params=_params(("parallel",)),
    )(x, oa, ob, om, gt, *consts, *into)


ROW_DMA_UNROLL = 8
MOE_BUFFERS = 4
DRAIN_STEPS = 2


TILE_ROWS = 8


def _rows_from_tiles(ref, n):
    return jnp.concatenate([ref[pl.ds(s, n, stride=TILE_ROWS), :] for s in range(TILE_ROWS)], axis=1)


def _rows_to_tiles(ref, x):
    for s in range(TILE_ROWS):
        ref[pl.ds(s, x.shape[0], stride=TILE_ROWS), :] = x[:, s * LANES:(s + 1) * LANES]


def _row_copies(asg_ref, base, count, n_asg, x_hbm, buf, y_hbm, sem, gather, unrolled):
    def tile(ref, idx):
        start = idx * TILE_ROWS
        return ref.at[pl.ds(start if isinstance(idx, int) else pl.multiple_of(start, TILE_ROWS), TILE_ROWS)]

    def one(r, priority):
        a = asg_ref[base + r]
        if gather:
            tok = jnp.minimum(a, n_asg - 1)
            tok = jnp.where(tok >= n_asg // 2, tok - n_asg // 2, tok)
            copy = pltpu.make_async_copy(tile(x_hbm, tok), tile(buf, r), sem)
        else:
            copy = pltpu.make_async_copy(tile(buf, r), tile(y_hbm, a), sem)
        copy.start(priority=priority)

    if unrolled:
        for r in range(count):
            one(r, r % 2)
        return

    def body(g, carry):
        for j in range(ROW_DMA_UNROLL):
            one(g * ROW_DMA_UNROLL + j, j % 2)
        return carry
    lax.fori_loop(0, count // ROW_DMA_UNROLL, body, 0)


def _moe_expert_kernel(n_tokens, n_asg, asg_ref, be_ref, nu_ref, x_hbm, wg_ref, wu_ref, wd_ref, y_hbm,
                       xbuf, ybuf, wgb, wub, wdb, gsem, ssem):
    i = pl.program_id(0)
    used = nu_ref[0]
    rows = xbuf.shape[1] // TILE_ROWS
    n_buf = xbuf.shape[0]
    n_blocks = be_ref.shape[0]
    blk = jnp.minimum(i, n_blocks - 1)

    @pl.when((i == 0) | (be_ref[blk] != be_ref[jnp.maximum(blk - 1, 0)]))
    def _():
        wgb[...] = wg_ref[0, 0].astype(BF16)
        wub[...] = wu_ref[0, 0].astype(BF16)
        wdb[...] = wd_ref[0, 0].astype(BF16)

    def wait_gather(slot):
        pltpu.make_async_copy(x_hbm.at[pl.ds(0, rows * TILE_ROWS)], xbuf.at[slot], gsem.at[slot]).wait()

    def wait_scatter(slot):
        pltpu.make_async_copy(ybuf.at[slot], y_hbm.at[pl.ds(0, rows * TILE_ROWS)], ssem.at[slot]).wait()

    def gather(b, slot, unrolled):
        _row_copies(asg_ref, b * rows, rows, n_asg, x_hbm, xbuf.at[slot], y_hbm, gsem.at[slot], True, unrolled)

    def scatter(b, slot, unrolled):
        _row_copies(asg_ref, b * rows, rows, n_asg, x_hbm, ybuf.at[slot], y_hbm, ssem.at[slot], False, unrolled)

    def expert(slot):
        xb = _rows_from_tiles(xbuf.at[slot], rows).astype(BF16)
        hg = jnp.dot(xb, wgb[...], preferred_element_type=F32)
        hu = jnp.dot(xb, wub[...], preferred_element_type=F32)
        h = hg * _sigmoid(hg) * hu
        _rows_to_tiles(ybuf.at[slot], jnp.dot(h.astype(BF16), wdb[...], preferred_element_type=F32))

    @pl.when(i == 0)
    def _():
        ybuf[1] = jnp.zeros(ybuf.shape[1:], F32)
        stride = n_asg // 2
        gaps = [(k * stride + n_tokens, stride - n_tokens) for k in range(2)]
        gaps.append((n_asg, y_hbm.shape[0] // TILE_ROWS - n_asg))
        fills = [pltpu.make_async_copy(ybuf.at[1, pl.ds(0, min(rows, count - j) * TILE_ROWS)],
                                       y_hbm.at[pl.ds((start + j) * TILE_ROWS, min(rows, count - j) * TILE_ROWS)],
                                       ssem.at[1])
                 for start, count in gaps for j in range(0, count, rows)]
        for copy in fills:
            copy.start()
        for copy in fills:
            copy.wait()

    @pl.when((i >= 2) & (i - 2 < used))
    def _():
        wait_scatter((i - 2) % n_buf)

    ahead = n_buf - 1
    steady = (i >= 1) & (i + ahead < used)
    for slot in range(n_buf):
        @pl.when(steady & (i % n_buf == slot))
        def _():
            wait_gather(slot)
            expert(slot)
            gather(i + ahead, (slot + ahead) % n_buf, True)
            scatter(i - 1, (slot - 1) % n_buf, True)

    @pl.when(jnp.logical_not(steady))
    def _():
        slot = i % n_buf

        @pl.when(i == 0)
        def _():
            for b in range(ahead):
                @pl.when(b < used)
                def _():
                    gather(b, b, False)

        @pl.when(i < used)
        def _():
            wait_gather(slot)

        @pl.when(i + ahead < used)
        def _():
            gather(i + ahead, (i + ahead) % n_buf, False)

        @pl.when((i >= 1) & (i - 1 < used))
        def _():
            scatter(i - 1, (i - 1) % n_buf, False)

        @pl.when(i < used)
        def _():
            expert(slot)


def moe_experts(x1_tiles, n_tokens, stride, asg, blk_e, n_used, e_gate, e_up, e_down):
    d = e_gate.shape[2]
    assert d == TILE_ROWS * LANES and x1_tiles.shape == (stride * TILE_ROWS, LANES)
    n_blocks = blk_e.shape[0]
    ff = e_gate.shape[-1]
    n_asg = 2 * stride
    n_rows = n_asg + e_gate.shape[1] * EXPERT_BLOCK
    weight = lambda shape: pl.BlockSpec(
        (1, 1) + shape, lambda i, asg, be, nu: (0, be[jnp.minimum(i, n_blocks - 1)], 0, 0))
    return pl.pallas_call(
        functools.partial(_moe_expert_kernel, n_tokens, n_asg), name="moe_expert",
        grid_spec=pltpu.PrefetchScalarGridSpec(
            num_scalar_prefetch=3,
            grid=(n_blocks + DRAIN_STEPS,),
            in_specs=[pl.BlockSpec(memory_space=pl.ANY), weight((d, ff)), weight((d, ff)), weight((ff, d))],
            out_specs=pl.BlockSpec(memory_space=pl.ANY),
            scratch_shapes=[pltpu.VMEM((MOE_BUFFERS, EXPERT_BLOCK * TILE_ROWS, LANES), F32),
                            pltpu.VMEM((MOE_BUFFERS, EXPERT_BLOCK * TILE_ROWS, LANES), F32),
                            pltpu.VMEM((d, ff), BF16), pltpu.VMEM((d, ff), BF16), pltpu.VMEM((ff, d), BF16),
                            pltpu.SemaphoreType.DMA((MOE_BUFFERS,)), pltpu.SemaphoreType.DMA((MOE_BUFFERS,))]),
        out_shape=jax.ShapeDtypeStruct((n_rows * TILE_ROWS, LANES), F32),
        compiler_params=_params(("arbitrary",)),
    )(asg, blk_e, n_used, x1_tiles, e_gate, e_up, e_down)


def _moe_combine_kernel(alpha, lead_tiles, y0_ref, y1_ref, x1_ref, gate_ref, g_ref, b_ref, lead_ref, tail_ref):
    i = pl.program_id(0)
    gate = gate_ref[...]
    tm = x1_ref.shape[0]
    moe = gate[:, 0:1] * _rows_from_tiles(y0_ref, tm) + gate[:, 1:2] * _rows_from_tiles(y1_ref, tm)
    out = _ln(alpha * x1_ref[...] + moe, g_ref[...], b_ref[...])

    @pl.when(i < lead_tiles)
    def _():
        lead_ref[...] = out

    @pl.when(i >= lead_tiles)
    def _():
        tail_ref[...] = out[:tail_ref.shape[0]]


def moe_combine(y, x1, gate, g, b, tm, alpha, n, n_lead):
    stride, d = x1.shape
    assert n_lead % tm == 0 and stride - n_lead == tm and n_lead < n <= stride
    lead_tiles = n_lead // tm
    return pl.pallas_call(
        functools.partial(_moe_combine_kernel, alpha, lead_tiles), name="moe_combine",
        grid=(stride // tm,),
        in_specs=[pl.BlockSpec((tm * TILE_ROWS, LANES), lambda i: (i, 0)),
                  pl.BlockSpec((tm * TILE_ROWS, LANES), lambda i: (i + stride // tm, 0)),
                  pl.BlockSpec((tm, d), lambda i: (i, 0)),
                  pl.BlockSpec((tm, LANES), lambda i: (i, 0)), _const_spec((1, d)), _const_spec((1, d))],
        out_specs=[pl.BlockSpec((tm, d), lambda i: (jnp.minimum(i, lead_tiles - 1), 0)),
                   pl.BlockSpec((n - n_lead, d), lambda i: (0, 0))],
        out_shape=[jax.ShapeDtypeStruct((n_lead, d), F32), jax.ShapeDtypeStruct((n - n_lead, d), F32)],
        compiler_params=_params(("arbitrary",)),
    )(y, y, x1, gate, g.reshape(1, d), b.reshape(1, d))


def moe_routing(experts, stride):
    top_k, n = experts.shape
    n_exp = N_GROUPS * EXPERTS_PER_GROUP
    a = n * top_k
    flat_e = experts.reshape(a)
    pos_bits = max(a - 1, 1).bit_length()
    assert n_exp << pos_bits < 2 ** 31
    order = jnp.sort((flat_e << pos_bits) | jnp.arange(a, dtype=jnp.int32)) & ((1 << pos_bits) - 1)
    order = order + (order // n) * (stride - n)
    counts = jnp.sum((flat_e[:, None] == jnp.arange(n_exp, dtype=jnp.int32)[None, :]).astype(jnp.int32), axis=0)
    ends = jnp.cumsum(counts)
    padded = (counts + EXPERT_BLOCK - 1) // EXPERT_BLOCK * EXPERT_BLOCK
    pad_end = jnp.cumsum(padded)
    n_blocks = -(-a // EXPERT_BLOCK) + n_exp
    blk_start = jnp.arange(n_blocks, dtype=jnp.int32) * EXPERT_BLOCK
    blk_e = jnp.minimum(jnp.sum((pad_end[None, :] <= blk_start[:, None]).astype(jnp.int32), axis=1), n_exp - 1)
    is_e = blk_e[:, None] == jnp.arange(n_exp, dtype=jnp.int32)[None, :]
    per_block = lambda table: jnp.sum(jnp.where(is_e, table[None, :], 0), axis=1, keepdims=True)
    slot = blk_start[:, None] + jnp.arange(EXPERT_BLOCK, dtype=jnp.int32)[None, :]
    rank = slot - per_block(pad_end - padded)
    spare = top_k * stride + jnp.clip(slot - per_block(ends), 0, n_exp * EXPERT_BLOCK - 1)
    asg = jnp.where(rank < per_block(counts), order[jnp.clip(per_block(ends - counts) + rank, 0, a - 1)], spare)
    n_used = (pad_end[-1:] // EXPERT_BLOCK).astype(jnp.int32)
    return asg.reshape(-1).astype(jnp.int32), blk_e.astype(jnp.int32), n_used


def hier_moe_ln(x1, x1_tiles, eidx, gate, w, tm, alpha, n_tokens, n_lead):
    stride = x1.shape[0]
    asg, blk_e, n_used = moe_routing(eidx[:2, :n_tokens], stride)
    y = moe_experts(x1_tiles, n_tokens, stride, asg, blk_e, n_used, w['e_gate'], w['e_up'], w['e_down'])
    return moe_combine(y, x1, gate, w['ln2_g'], w['ln2_b'], tm, alpha, n_tokens, n_lead)


def kernel(x_prompt, x_sample, mem_prompt, state_wkv, state_shift, cache_win_k, cache_win_v, cache_mem_k, cache_mem_v, ln_in_g, ln_in_b, w_in, mu, w0, w_up, a0, a_up, g_up, k_k, k_a, r_k, lnx_g, lnx_b, sinks, w_mem_kv, p_a, p_b, p_m, w_o, ln1_g, ln1_b, w_group, b_group, w_router, b_router, e_gate, e_up, e_down, ln2_g, ln2_b):
    depth = w_in.shape[0]
    assert depth == 1, "single-layer step"
    bsz, seq, d = x_prompt.shape
    dec = x_sample.shape[0]
    assert x_sample.shape[1] == 1
    c_shift = mu.shape[-1]
    c_a = w0.shape[-1]
    window, kv_w = cache_win_k.shape[2], cache_win_k.shape[3] * cache_win_k.shape[4]
    n_mem, mem_w = cache_mem_k.shape[2], cache_mem_k.shape[3] * cache_mem_k.shape[4]
    q_w = sinks.shape[-1] * HEAD
    qkvm_w = q_w + 2 * kv_w + mem_w
    alpha = (2.0 * depth) ** 0.25
    past_len = float(PAST_LEN)
    chunk = WKV_CHUNK

    w_in_b = w_in[0].astype(BF16)
    w_parts = [w_in_b[:, :c_shift], w_in_b[:, c_shift:c_shift + qkvm_w], w_in_b[:, c_shift + qkvm_w:]]
    rp = dict(mu=mu[0], w0=w0[0], w_up=w_up[0], a0=a0[0], a_up=a_up[0], g_up=g_up[0], k_k=k_k[0], k_a=k_a[0],
              r_k=r_k[0].reshape(-1))
    n_route = N_GROUPS * (1 + EXPERTS_PER_GROUP)
    mw = dict(ln_in_g=ln_in_g.reshape(1, d), ln_in_b=ln_in_b.reshape(1, d), p_a=p_a[0].astype(BF16),
              p_b=p_b[0].astype(BF16), p_m=p_m[0].astype(BF16), w_o=w_o[0].astype(BF16),
              ln1_g=ln1_g[0].reshape(1, d), ln1_b=ln1_b[0].reshape(1, d),
              w_route=jnp.stack(_pieces(jnp.pad(jnp.concatenate([w_group[0], w_router[0]], axis=1),
                                                ((0, 0), (0, LANES - n_route))), 2)),
              b_route=jnp.pad(jnp.concatenate([b_group[0], b_router[0]]), (0, LANES - n_route)).reshape(1, LANES),
              e_gate=e_gate, e_up=e_up, e_down=e_down, ln2_g=ln2_g[0], ln2_b=ln2_b[0])

    xp = x_prompt.reshape(bsz * seq, d)
    zq, zg, shift_p, prep = ln_proj_prep(x_prompt, ln_in_g, ln_in_b, w_parts, jnp.zeros((bsz, 1, c_shift), F32), rp,
                                         chunk, ROW_TILE, BF16)
    o_a, wkv_p = wkv(prep, jnp.zeros((bsz, c_a // HEAD, HEAD, HEAD), F32), lnx_g[0], lnx_b[0], chunk)
    mkv = proj(mem_prompt.reshape(bsz * n_mem, d), w_mem_kv[0].astype(BF16), ROW_TILE).reshape(bsz, n_mem, 2 * mem_w)
    mk_p, mv_p = mkv[..., :mem_w], mkv[..., mem_w:]
    tables = rope_tables(jnp.arange(seq, dtype=F32))
    o_b, o_m, k_rot = attn_prompt(zq, mk_p, mv_p, sinks[0], tables, window, q_w, kv_w)
    n_all = bsz * seq + dec
    n_buf = -(-n_all // ROW_TILE) * ROW_TILE
    routed = merge(xp, o_a.reshape(-1, c_a), o_b.reshape(-1, q_w), o_m.reshape(-1, mem_w),
                   zg.reshape(bsz * seq, -1), mw, ROW_TILE, alpha, n_buf)
    shift_p = shift_p[:, 0]
    kb_p = k_rot[:, -window:].reshape(bsz, window, H_KV, HEAD)
    vb_p = zq[:, -window:, q_w + kv_w:q_w + 2 * kv_w].reshape(bsz, window, H_KV, HEAD)

    xs = x_sample.reshape(dec, d)
    zq_s, zg_s, zr_s, ops_s = ln_proj_prep(xs.reshape(1, dec, d), ln_in_g, ln_in_b, w_parts,
                                           state_shift[0].reshape(1, dec, c_shift), rp, 1, dec, F32)
    zq_s, zg_s, zr_s = zq_s[0], zg_s[0], zr_s[0]
    o_a_s, wkv_s = wkv_step([a.reshape(dec, c_a) for a in ops_s], state_wkv[0], lnx_g[0], lnx_b[0], STEP_BATCH)
    tables_s = rope_tables(jnp.full((1,), past_len, F32))
    o_b_s, o_m_s, nk_s, nv_s = attn_step(
        zq_s, cache_win_k[0].reshape(dec, window, kv_w), cache_win_v[0].reshape(dec, window, kv_w),
        cache_mem_k[0].reshape(dec, n_mem, mem_w), cache_mem_v[0].reshape(dec, n_mem, mem_w),
        sinks[0], tables_s, q_w, kv_w, STEP_BATCH)
    x1, x1_tiles, eidx, gate = merge(xs, o_a_s, o_b_s, o_m_s, zg_s, mw, dec, alpha, n_buf, bsz * seq, routed)

    y_prompt, y_sample = hier_moe_ln(x1, x1_tiles, eidx, gate, mw, ROW_TILE, alpha, n_all, bsz * seq)
    y_prompt = y_prompt.reshape(bsz, seq, d)
    y_sample = y_sample.reshape(dec, 1, d)

    sd = state_wkv.dtype
    return (y_prompt, y_sample, wkv_p[None].astype(sd), wkv_s[None].astype(sd), shift_p[None], zr_s[None],
            kb_p[None], vb_p[None], nk_s.reshape(dec, window, H_KV, HEAD)[None],
            nv_s.reshape(dec, window, H_KV, HEAD)[None],
            mk_p.reshape(bsz, n_mem, -1, HEAD)[None], mv_p.reshape(bsz, n_mem, -1, HEAD)[None])
```
